```python
import jax, jax.numpy as jnp
from jax import lax
import numpy as np

D_MODEL = 1024
BATCH = 32
SEQ = 2048
DEPTH = 1

POOL_WIDTH = D_MODEL
POOL_WINDOWS = (2, 4, 8, 16)
POOL_GROUPS = len(POOL_WINDOWS)
POOL_GROUP_WIDTH = POOL_WIDTH // POOL_GROUPS
SSD_EXPAND = 2
SSD_INNER = SSD_EXPAND * D_MODEL
SSD_HEAD_DIM = 64
SSD_HEADS = SSD_INNER // SSD_HEAD_DIM
SSD_GROUPS = 8
SSD_HEADS_PER_GROUP = SSD_HEADS // SSD_GROUPS
SSD_STATE = 128
SSD_CONV = 4
SSD_CHUNK = 256
SSD_CONV_DIM = SSD_INNER + 2 * SSD_GROUPS * SSD_STATE
SSD_NORM_GROUP = SSD_INNER // SSD_GROUPS
D_FF = 4 * D_MODEL
N_BRANCHES = 2
OFF_POOL = POOL_WIDTH
OFF_Z = OFF_POOL + SSD_INNER
OFF_XBC = OFF_Z + SSD_CONV_DIM
OFF_DT = OFF_XBC + SSD_HEADS
IN_PROJ_WIDTH = OFF_DT + N_BRANCHES * D_MODEL
DEEPNORM_ALPHA = (2.0 * DEPTH) ** 0.25
DEEPNORM_BETA = (8.0 * DEPTH) ** -0.25
LN_EPS = 1e-5
RMS_EPS = 1e-5

kernel_name = "pool_ssd_gated_hybrid_deepnorm"


def layer_norm(x, g, b):
    xf = x.astype(jnp.float32)
    mu = jnp.mean(xf, axis=-1, keepdims=True)
    var = jnp.mean(jnp.square(xf - mu), axis=-1, keepdims=True)
    return ((xf - mu) * lax.rsqrt(var + LN_EPS) * g + b).astype(x.dtype)


def causal_multiscale_pool(u):
    bsz, s, _ = u.shape
    uf = u.astype(jnp.float32).reshape(bsz, s, POOL_GROUPS, POOL_GROUP_WIDTH)
    cs = jnp.cumsum(uf, axis=1)
    pos = jnp.arange(1, s + 1, dtype=jnp.float32)
    outs = []
    for gi, w in enumerate(POOL_WINDOWS):
        csg = cs[:, :, gi]
        lag = jnp.pad(csg, ((0, 0), (w, 0), (0, 0)))[:, :s]
        cnt = jnp.minimum(pos, float(w))[None, :, None]
        outs.append((csg - lag) / cnt)
    pooled = jnp.stack(outs, axis=2)
    return pooled - uf


def causal_depthwise_conv(u, w, b):
    k_width = w.shape[0]
    s = u.shape[1]
    up = jnp.pad(u, ((0, 0), (k_width - 1, 0), (0, 0)))
    y = up[:, 0:s] * w[0]
    for k in range(1, k_width):
        y = y + up[:, k:k + s] * w[k]
    return y + b


def segsum_exp(a):
    t = a.shape[-1]
    cs = jnp.cumsum(a, axis=-1)
    seg = cs[..., :, None] - cs[..., None, :]
    mask = jnp.tril(jnp.ones((t, t), dtype=bool))
    return jnp.exp(jnp.where(mask, seg, -jnp.inf))


def ssd_chunked(xdt, da, bm, cm):
    bsz, s = xdt.shape[:2]
    n_chunks = -(-s // SSD_CHUNK)
    pad = n_chunks * SSD_CHUNK - s

    def to_chunks(t):
        t = jnp.pad(t, ((0, 0), (0, pad)) + ((0, 0),) * (t.ndim - 2))
        return t.reshape((bsz, n_chunks, SSD_CHUNK) + t.shape[2:])

    xc, ac, bc, cc = to_chunks(xdt), to_chunks(da), to_chunks(bm), to_chunks(cm)
    a_cs = jnp.cumsum(ac, axis=2)
    lmat = segsum_exp(jnp.moveaxis(ac, 2, -1))
    cb = jnp.einsum('bclgn,bcsgn->bcgls', cc, bc)
    y_diag = jnp.einsum('bcgls,bcgrls,bcsgrp->bclgrp', cb, lmat, xc)
    decay_to_end = jnp.exp(a_cs[:, :, -1:] - a_cs)
    states = jnp.einsum('bclgn,bclgr,bclgrp->bcgrpn', bc, decay_to_end, xc)
    chunk_decay = jnp.exp(a_cs[:, :, -1])

    def step(h, inp):
        s_c, d_c = inp
        return d_c[..., None, None] * h + s_c, h

    h0 = jnp.zeros_like(states[:, 0])
    _, prev = lax.scan(step, h0, (jnp.moveaxis(states, 1, 0), jnp.moveaxis(chunk_decay, 1, 0)))
    prev = jnp.moveaxis(prev, 0, 1)
    y_off = jnp.einsum('bclgn,bcgrpn,bclgr->bclgrp', cc, prev, jnp.exp(a_cs))
    y = (y_diag + y_off).reshape((bsz, n_chunks * SSD_CHUNK) + xdt.shape[2:])
    return y[:, :s]


def hybrid_layer(h, w_in, b_gates, conv_w, conv_b, dt_bias, a_log, d_skip,
                 ssd_norm_w, w_ssd_proj, w_pool_group, pool_scale, w_out,
                 ln1_g, ln1_b, w_up, w_down, ln2_g, ln2_b):
    bsz, s, _ = h.shape
    proj = h @ w_in
    u_pool, z, xbc, dt_raw, gate_logits = jnp.split(
        proj, [OFF_POOL, OFF_Z, OFF_XBC, OFF_DT], axis=-1)
    gates = jax.nn.sigmoid((gate_logits + b_gates).astype(jnp.float32))
    gates = gates.reshape(bsz, s, N_BRANCHES, D_MODEL)

    pooled = causal_multiscale_pool(u_pool)
    y_pool = jnp.einsum('bsgc,gcd->bsgd', pooled, w_pool_group)
    y_pool = y_pool.reshape(bsz, s, POOL_WIDTH) * pool_scale

    xbc = jax.nn.silu(causal_depthwise_conv(xbc, conv_w, conv_b))
    xs, bm, cm = jnp.split(xbc, [SSD_INNER, SSD_INNER + SSD_GROUPS * SSD_STATE], axis=-1)
    xs = xs.astype(jnp.float32).reshape(bsz, s, SSD_GROUPS, SSD_HEADS_PER_GROUP, SSD_HEAD_DIM)
    bm = bm.astype(jnp.float32).reshape(bsz, s, SSD_GROUPS, SSD_STATE)
    cm = cm.astype(jnp.float32).reshape(bsz, s, SSD_GROUPS, SSD_STATE)
    dt = jax.nn.softplus(dt_raw.astype(jnp.float32) + dt_bias)
    dt = dt.reshape(bsz, s, SSD_GROUPS, SSD_HEADS_PER_GROUP)
    a = -jnp.exp(a_log.astype(jnp.float32)).reshape(SSD_GROUPS, SSD_HEADS_PER_GROUP)
    d = d_skip.astype(jnp.float32).reshape(SSD_GROUPS, SSD_HEADS_PER_GROUP)
    y = ssd_chunked(xs * dt[..., None], dt * a, bm, cm) + d[..., None] * xs
    y = y.reshape(bsz, s, SSD_INNER)
    yg = (y * jax.nn.silu(z.astype(jnp.float32))).reshape(bsz, s, SSD_GROUPS, SSD_NORM_GROUP)
    yg = yg * lax.rsqrt(jnp.mean(jnp.square(yg), axis=-1, keepdims=True) + RMS_EPS)
    yg = yg.reshape(bsz, s, SSD_INNER) * ssd_norm_w
    y_ssd = yg.astype(h.dtype) @ w_ssd_proj

    merged = gates[:, :, 0] * y_pool + gates[:, :, 1] * y_ssd
    mix = merged.astype(h.dtype) @ w_out
    h = layer_norm(DEEPNORM_ALPHA * h + mix, ln1_g, ln1_b)

    ff = jnp.square(jax.nn.relu(h @ w_up)) @ w_down
    h = layer_norm(DEEPNORM_ALPHA * h + ff, ln2_g, ln2_b)
    return h


def _fwd_setup_inputs(seed: int = 0) -> dict:
    key = jax.random.key(seed)
    ks = jax.random.split(key, 20)
    nrm = lambda k, shape: jax.random.normal(k, shape, dtype=jnp.float32)
    x = nrm(ks[0], (BATCH, SEQ, D_MODEL))
    w_in = nrm(ks[1], (DEPTH, D_MODEL, IN_PROJ_WIDTH)) * D_MODEL ** -0.5
    b_gates = 0.1 * nrm(ks[2], (DEPTH, N_BRANCHES * D_MODEL))
    conv_w = nrm(ks[3], (DEPTH, SSD_CONV, SSD_CONV_DIM)) * SSD_CONV ** -0.5
    conv_b = 0.02 * nrm(ks[4], (DEPTH, SSD_CONV_DIM))
    dt0 = jnp.exp(jax.random.uniform(ks[5], (DEPTH, SSD_HEADS), dtype=jnp.float32,
                                     minval=np.log(1e-3), maxval=np.log(1e-1)))
    dt_bias = dt0 + jnp.log(-jnp.expm1(-dt0))
    a_log = jnp.log(jax.random.uniform(ks[6], (DEPTH, SSD_HEADS), dtype=jnp.float32,
                                       minval=1.0, maxval=16.0))
    d_skip = 1.0 + 0.1 * nrm(ks[7], (DEPTH, SSD_HEADS))
    ssd_norm_w = 1.0 + 0.02 * nrm(ks[8], (DEPTH, SSD_INNER))
    w_ssd_proj = nrm(ks[9], (DEPTH, SSD_INNER, D_MODEL)) * SSD_INNER ** -0.5
    w_pool_group = nrm(ks[10], (DEPTH, POOL_GROUPS, POOL_GROUP_WIDTH, POOL_GROUP_WIDTH)) * POOL_GROUP_WIDTH ** -0.5
    pool_scale = 1.0 + 0.02 * nrm(ks[11], (DEPTH, POOL_WIDTH))
    w_out = nrm(ks[12], (DEPTH, D_MODEL, D_MODEL)) * (D_MODEL ** -0.5 * DEEPNORM_BETA)
    ln1_g = 1.0 + 0.02 * nrm(ks[13], (DEPTH, D_MODEL))
    ln1_b = 0.02 * nrm(ks[14], (DEPTH, D_MODEL))
    w_up = nrm(ks[15], (DEPTH, D_MODEL, D_FF)) * D_MODEL ** -0.5
    w_down = nrm(ks[16], (DEPTH, D_FF, D_MODEL)) * (D_FF ** -0.5 * DEEPNORM_BETA)
    ln2_g = 1.0 + 0.02 * nrm(ks[17], (DEPTH, D_MODEL))
    ln2_b = 0.02 * nrm(ks[18], (DEPTH, D_MODEL))
    return {"x": x, "w_in": w_in, "b_gates": b_gates, "conv_w": conv_w, "conv_b": conv_b,
            "dt_bias": dt_bias, "a_log": a_log, "d_skip": d_skip, "ssd_norm_w": ssd_norm_w,
            "w_ssd_proj": w_ssd_proj, "w_pool_group": w_pool_group, "pool_scale": pool_scale,
            "w_out": w_out, "ln1_g": ln1_g, "ln1_b": ln1_b, "w_up": w_up, "w_down": w_down,
            "ln2_g": ln2_g, "ln2_b": ln2_b}


def _fwd_reference(x, w_in, b_gates, conv_w, conv_b, dt_bias, a_log, d_skip, ssd_norm_w,
              w_ssd_proj, w_pool_group, pool_scale, w_out, ln1_g, ln1_b, w_up, w_down,
              ln2_g, ln2_b):
    h = x
    for layer in range(DEPTH):
        h = hybrid_layer(h, w_in[layer], b_gates[layer], conv_w[layer], conv_b[layer],
                         dt_bias[layer], a_log[layer], d_skip[layer], ssd_norm_w[layer],
                         w_ssd_proj[layer], w_pool_group[layer], pool_scale[layer],
                         w_out[layer], ln1_g[layer], ln1_b[layer], w_up[layer],
                         w_down[layer], ln2_g[layer], ln2_b[layer])
    return h


import jax as _jax
import jax.numpy as _jnp

TWIN_FORMAT = 'train_step'
FWD_PARAMS = ['x', 'w_in', 'b_gates', 'conv_w', 'conv_b', 'dt_bias', 'a_log', 'd_skip', 'ssd_norm_w', 'w_ssd_proj', 'w_pool_group', 'pool_scale', 'w_out', 'ln1_g', 'ln1_b', 'w_up', 'w_down', 'ln2_g', 'ln2_b']
TWIN_WEIGHTS = ['w_in', 'b_gates', 'conv_w', 'conv_b', 'dt_bias', 'a_log', 'd_skip', 'ssd_norm_w', 'w_ssd_proj', 'w_pool_group', 'pool_scale', 'w_out', 'ln1_g', 'ln1_b', 'w_up', 'w_down', 'ln2_g', 'ln2_b']
TWIN_DIFF_INPUT = 'x'
TWIN_INPUTS = ['x', 'w_in', 'b_gates', 'conv_w', 'conv_b', 'dt_bias', 'a_log', 'd_skip', 'ssd_norm_w', 'w_ssd_proj', 'w_pool_group', 'pool_scale', 'w_out', 'ln1_g', 'ln1_b', 'w_up', 'w_down', 'ln2_g', 'ln2_b', 'loss_target', 'm_w_in', 'm_b_gates', 'm_conv_w', 'm_conv_b', 'm_dt_bias', 'm_a_log', 'm_d_skip', 'm_ssd_norm_w', 'm_w_ssd_proj', 'm_w_pool_group', 'm_pool_scale', 'm_w_out', 'm_ln1_g', 'm_ln1_b', 'm_w_up', 'm_w_down', 'm_ln2_g', 'm_ln2_b', 'v_w_in', 'v_b_gates', 'v_conv_w', 'v_conv_b', 'v_dt_bias', 'v_a_log', 'v_d_skip', 'v_ssd_norm_w', 'v_w_ssd_proj', 'v_w_pool_group', 'v_pool_scale', 'v_w_out', 'v_ln1_g', 'v_ln1_b', 'v_w_up', 'v_w_down', 'v_ln2_g', 'v_ln2_b']
TWIN_OUTPUTS = ['loss', 'grad_x', 'grad_w_in', 'grad_b_gates', 'grad_conv_w', 'grad_conv_b', 'grad_dt_bias', 'grad_a_log', 'grad_d_skip', 'grad_ssd_norm_w', 'grad_w_ssd_proj', 'grad_w_pool_group', 'grad_pool_scale', 'grad_w_out', 'grad_ln1_g', 'grad_ln1_b', 'grad_w_up', 'grad_w_down', 'grad_ln2_g', 'grad_ln2_b', 'delta_w_in', 'delta_b_gates', 'delta_conv_w', 'delta_conv_b', 'delta_dt_bias', 'delta_a_log', 'delta_d_skip', 'delta_ssd_norm_w', 'delta_w_ssd_proj', 'delta_w_pool_group', 'delta_pool_scale', 'delta_w_out', 'delta_ln1_g', 'delta_ln1_b', 'delta_w_up', 'delta_w_down', 'delta_ln2_g', 'delta_ln2_b', 'new_m_w_in', 'new_m_b_gates', 'new_m_conv_w', 'new_m_conv_b', 'new_m_dt_bias', 'new_m_a_log', 'new_m_d_skip', 'new_m_ssd_norm_w', 'new_m_w_ssd_proj', 'new_m_w_pool_group', 'new_m_pool_scale', 'new_m_w_out', 'new_m_ln1_g', 'new_m_ln1_b', 'new_m_w_up', 'new_m_w_down', 'new_m_ln2_g', 'new_m_ln2_b', 'new_v_w_in', 'new_v_b_gates', 'new_v_conv_w', 'new_v_conv_b', 'new_v_dt_bias', 'new_v_a_log', 'new_v_d_skip', 'new_v_ssd_norm_w', 'new_v_w_ssd_proj', 'new_v_w_pool_group', 'new_v_pool_scale', 'new_v_w_out', 'new_v_ln1_g', 'new_v_ln1_b', 'new_v_w_up', 'new_v_w_down', 'new_v_ln2_g', 'new_v_ln2_b']
TWIN_LEAF_KINDS = {'loss': 'loss', 'grad_x': 'grad_x', 'grad_w_in': 'grad_w', 'grad_b_gates': 'grad_w', 'grad_conv_w': 'grad_w', 'grad_conv_b': 'grad_w', 'grad_dt_bias': 'grad_w', 'grad_a_log': 'grad_w', 'grad_d_skip': 'grad_w', 'grad_ssd_norm_w': 'grad_w', 'grad_w_ssd_proj': 'grad_w', 'grad_w_pool_group': 'grad_w', 'grad_pool_scale': 'grad_w', 'grad_w_out': 'grad_w', 'grad_ln1_g': 'grad_w', 'grad_ln1_b': 'grad_w', 'grad_w_up': 'grad_w', 'grad_w_down': 'grad_w', 'grad_ln2_g': 'grad_w', 'grad_ln2_b': 'grad_w', 'delta_w_in': 'delta_w', 'delta_b_gates': 'delta_w', 'delta_conv_w': 'delta_w', 'delta_conv_b': 'delta_w', 'delta_dt_bias': 'delta_w', 'delta_a_log': 'delta_w', 'delta_d_skip': 'delta_w', 'delta_ssd_norm_w': 'delta_w', 'delta_w_ssd_proj': 'delta_w', 'delta_w_pool_group': 'delta_w', 'delta_pool_scale': 'delta_w', 'delta_w_out': 'delta_w', 'delta_ln1_g': 'delta_w', 'delta_ln1_b': 'delta_w', 'delta_w_up': 'delta_w', 'delta_w_down': 'delta_w', 'delta_ln2_g': 'delta_w', 'delta_ln2_b': 'delta_w', 'new_m_w_in': 'new_m', 'new_m_b_gates': 'new_m', 'new_m_conv_w': 'new_m', 'new_m_conv_b': 'new_m', 'new_m_dt_bias': 'new_m', 'new_m_a_log': 'new_m', 'new_m_d_skip': 'new_m', 'new_m_ssd_norm_w': 'new_m', 'new_m_w_ssd_proj': 'new_m', 'new_m_w_pool_group': 'new_m', 'new_m_pool_scale': 'new_m', 'new_m_w_out': 'new_m', 'new_m_ln1_g': 'new_m', 'new_m_ln1_b': 'new_m', 'new_m_w_up': 'new_m', 'new_m_w_down': 'new_m', 'new_m_ln2_g': 'new_m', 'new_m_ln2_b': 'new_m', 'new_v_w_in': 'new_v', 'new_v_b_gates': 'new_v', 'new_v_conv_w': 'new_v', 'new_v_conv_b': 'new_v', 'new_v_dt_bias': 'new_v', 'new_v_a_log': 'new_v', 'new_v_d_skip': 'new_v', 'new_v_ssd_norm_w': 'new_v', 'new_v_w_ssd_proj': 'new_v', 'new_v_w_pool_group': 'new_v', 'new_v_pool_scale': 'new_v', 'new_v_w_out': 'new_v', 'new_v_ln1_g': 'new_v', 'new_v_ln1_b': 'new_v', 'new_v_w_up': 'new_v', 'new_v_w_down': 'new_v', 'new_v_ln2_g': 'new_v', 'new_v_ln2_b': 'new_v'}


def _forward(args):
    return _fwd_reference(*[args[k] for k in FWD_PARAMS])


def _output_shape():
    out = _jax.eval_shape(lambda: _forward(_fwd_setup_inputs(0)))
    return out.shape, out.dtype

N_MICROBATCH = 1
ADAM_LR = 0.001
ADAM_B1 = 0.9
ADAM_B2 = 0.999
ADAM_EPS = 1e-08
ADAM_WD = 0.01
ADAM_STEP = 10
PER_EXAMPLE_BATCH_AXIS = {'x': 0, 'loss_target': 0}
SHARED_INPUTS = []
_WEIGHT_DTYPES = {'w_in': _jnp.float32, 'b_gates': _jnp.float32, 'conv_w': _jnp.float32, 'conv_b': _jnp.float32, 'dt_bias': _jnp.float32, 'a_log': _jnp.float32, 'd_skip': _jnp.float32, 'ssd_norm_w': _jnp.float32, 'w_ssd_proj': _jnp.float32, 'w_pool_group': _jnp.float32, 'pool_scale': _jnp.float32, 'w_out': _jnp.float32, 'ln1_g': _jnp.float32, 'ln1_b': _jnp.float32, 'w_up': _jnp.float32, 'w_down': _jnp.float32, 'ln2_g': _jnp.float32, 'ln2_b': _jnp.float32}
MOMENT_SCALE = {'w_in': 4.073518e-02, 'b_gates': 2.650865e-02, 'conv_w': 3.635524e-02, 'conv_b': 5.407633e-02, 'dt_bias': 9.859652e-02, 'a_log': 3.819450e-01, 'd_skip': 2.415903e-01, 'ssd_norm_w': 4.776731e-02, 'w_ssd_proj': 6.987037e-02, 'w_pool_group': 5.874176e-02, 'pool_scale': 6.704899e-02, 'w_out': 1.548254e-01, 'ln1_g': 1.219035e+00, 'ln1_b': 7.497707e-01, 'w_up': 7.562790e-02, 'w_down': 3.099656e-01, 'ln2_g': 6.400978e+01, 'ln2_b': 1.368549e+01}


def _to_microbatches(a, axis):
    t = _jnp.moveaxis(a, axis, 0)
    t = t.reshape((N_MICROBATCH, t.shape[0] // N_MICROBATCH) + t.shape[1:])
    return _jnp.moveaxis(t, 1, axis + 1)


def setup_inputs(seed: int = 0) -> dict:
    inp = _fwd_setup_inputs(seed)
    key = _jax.random.fold_in(_jax.random.key(seed), 7919)
    shape, _ = _output_shape()
    out = dict(inp)
    out["loss_target"] = _jax.random.normal(_jax.random.fold_in(key, 0), shape, _jnp.float32)
    for i, name in enumerate(TWIN_WEIGHTS):
        w = inp[name].astype(_jnp.float32)
        if MOMENT_SCALE is None:
            s = _jnp.sqrt(_jnp.mean(_jnp.square(w)) + 1e-30)
        else:
            s = MOMENT_SCALE[name]
        km, kv = _jax.random.split(_jax.random.fold_in(key, i + 1))
        out[name] = w
        out["m_" + name] = s * _jax.random.normal(km, w.shape, _jnp.float32)
        out["v_" + name] = (s * s) * _jax.random.uniform(kv, w.shape, _jnp.float32, 0.5, 1.5)
    if N_MICROBATCH > 1:
        for name, axis in PER_EXAMPLE_BATCH_AXIS.items():
            out[name] = _to_microbatches(out[name], axis)
    return {'x': out['x'], 'w_in': out['w_in'], 'b_gates': out['b_gates'], 'conv_w': out['conv_w'], 'conv_b': out['conv_b'], 'dt_bias': out['dt_bias'], 'a_log': out['a_log'], 'd_skip': out['d_skip'], 'ssd_norm_w': out['ssd_norm_w'], 'w_ssd_proj': out['w_ssd_proj'], 'w_pool_group': out['w_pool_group'], 'pool_scale': out['pool_scale'], 'w_out': out['w_out'], 'ln1_g': out['ln1_g'], 'ln1_b': out['ln1_b'], 'w_up': out['w_up'], 'w_down': out['w_down'], 'ln2_g': out['ln2_g'], 'ln2_b': out['ln2_b'], 'loss_target': out['loss_target'], 'm_w_in': out['m_w_in'], 'm_b_gates': out['m_b_gates'], 'm_conv_w': out['m_conv_w'], 'm_conv_b': out['m_conv_b'], 'm_dt_bias': out['m_dt_bias'], 'm_a_log': out['m_a_log'], 'm_d_skip': out['m_d_skip'], 'm_ssd_norm_w': out['m_ssd_norm_w'], 'm_w_ssd_proj': out['m_w_ssd_proj'], 'm_w_pool_group': out['m_w_pool_group'], 'm_pool_scale': out['m_pool_scale'], 'm_w_out': out['m_w_out'], 'm_ln1_g': out['m_ln1_g'], 'm_ln1_b': out['m_ln1_b'], 'm_w_up': out['m_w_up'], 'm_w_down': out['m_w_down'], 'm_ln2_g': out['m_ln2_g'], 'm_ln2_b': out['m_ln2_b'], 'v_w_in': out['v_w_in'], 'v_b_gates': out['v_b_gates'], 'v_conv_w': out['v_conv_w'], 'v_conv_b': out['v_conv_b'], 'v_dt_bias': out['v_dt_bias'], 'v_a_log': out['v_a_log'], 'v_d_skip': out['v_d_skip'], 'v_ssd_norm_w': out['v_ssd_norm_w'], 'v_w_ssd_proj': out['v_w_ssd_proj'], 'v_w_pool_group': out['v_w_pool_group'], 'v_pool_scale': out['v_pool_scale'], 'v_w_out': out['v_w_out'], 'v_ln1_g': out['v_ln1_g'], 'v_ln1_b': out['v_ln1_b'], 'v_w_up': out['v_w_up'], 'v_w_down': out['v_w_down'], 'v_ln2_g': out['v_ln2_g'], 'v_ln2_b': out['v_ln2_b']}


def _loss(weights, diff, rest, loss_target):
    with _jax.named_scope("forward"):
        args = {**rest, TWIN_DIFF_INPUT: diff, **{k: w.astype(_WEIGHT_DTYPES[k]) for k, w in weights.items()}}
        y = _forward(args)
    with _jax.named_scope("loss_head"):
        err = _jnp.square(y.astype(_jnp.float32) - loss_target)
        return 0.5 * _jnp.sum(_jnp.mean(err, axis=-1)) if err.ndim else 0.5 * err


def _adamw(w, g, m, v):
    m = ADAM_B1 * m + (1.0 - ADAM_B1) * g
    v = ADAM_B2 * v + (1.0 - ADAM_B2) * _jnp.square(g)
    m_hat = m / (1.0 - ADAM_B1 ** ADAM_STEP)
    v_hat = v / (1.0 - ADAM_B2 ** ADAM_STEP)
    delta = -ADAM_LR * (m_hat / (_jnp.sqrt(v_hat) + ADAM_EPS) + ADAM_WD * w)
    return delta, m, v


def reference(x, w_in, b_gates, conv_w, conv_b, dt_bias, a_log, d_skip, ssd_norm_w, w_ssd_proj, w_pool_group, pool_scale, w_out, ln1_g, ln1_b, w_up, w_down, ln2_g, ln2_b, loss_target, m_w_in, m_b_gates, m_conv_w, m_conv_b, m_dt_bias, m_a_log, m_d_skip, m_ssd_norm_w, m_w_ssd_proj, m_w_pool_group, m_pool_scale, m_w_out, m_ln1_g, m_ln1_b, m_w_up, m_w_down, m_ln2_g, m_ln2_b, v_w_in, v_b_gates, v_conv_w, v_conv_b, v_dt_bias, v_a_log, v_d_skip, v_ssd_norm_w, v_w_ssd_proj, v_w_pool_group, v_pool_scale, v_w_out, v_ln1_g, v_ln1_b, v_w_up, v_w_down, v_ln2_g, v_ln2_b):
    given = dict(x=x, w_in=w_in, b_gates=b_gates, conv_w=conv_w, conv_b=conv_b, dt_bias=dt_bias, a_log=a_log, d_skip=d_skip, ssd_norm_w=ssd_norm_w, w_ssd_proj=w_ssd_proj, w_pool_group=w_pool_group, pool_scale=pool_scale, w_out=w_out, ln1_g=ln1_g, ln1_b=ln1_b, w_up=w_up, w_down=w_down, ln2_g=ln2_g, ln2_b=ln2_b, loss_target=loss_target, m_w_in=m_w_in, m_b_gates=m_b_gates, m_conv_w=m_conv_w, m_conv_b=m_conv_b, m_dt_bias=m_dt_bias, m_a_log=m_a_log, m_d_skip=m_d_skip, m_ssd_norm_w=m_ssd_norm_w, m_w_ssd_proj=m_w_ssd_proj, m_w_pool_group=m_w_pool_group, m_pool_scale=m_pool_scale, m_w_out=m_w_out, m_ln1_g=m_ln1_g, m_ln1_b=m_ln1_b, m_w_up=m_w_up, m_w_down=m_w_down, m_ln2_g=m_ln2_g, m_ln2_b=m_ln2_b, v_w_in=v_w_in, v_b_gates=v_b_gates, v_conv_w=v_conv_w, v_conv_b=v_conv_b, v_dt_bias=v_dt_bias, v_a_log=v_a_log, v_d_skip=v_d_skip, v_ssd_norm_w=v_ssd_norm_w, v_w_ssd_proj=v_w_ssd_proj, v_w_pool_group=v_w_pool_group, v_pool_scale=v_pool_scale, v_w_out=v_w_out, v_ln1_g=v_ln1_g, v_ln1_b=v_ln1_b, v_w_up=v_w_up, v_w_down=v_w_down, v_ln2_g=v_ln2_g, v_ln2_b=v_ln2_b)
    weights = {n: given[n] for n in TWIN_WEIGHTS}
    shared = {n: given[n] for n in SHARED_INPUTS}
    per_example = {n: given[n] for n in ['x']}
    grad_fn = _jax.value_and_grad(_loss, argnums=(0, 1))

    def one_microbatch(ex, loss_target):
        ex = dict(ex)
        diff = ex.pop(TWIN_DIFF_INPUT)
        return grad_fn(weights, diff, {**shared, **ex}, loss_target)

    if N_MICROBATCH == 1:
        loss, (grad_w, grad_x) = one_microbatch(per_example, given["loss_target"])
    else:
        def body(carry, xs):
            loss_sum, grad_sum = carry
            l_k, (gw_k, gx_k) = one_microbatch(xs[0], xs[1])
            with _jax.named_scope("update"):
                return (loss_sum + l_k, _jax.tree.map(_jnp.add, grad_sum, gw_k)), gx_k

        init = (_jnp.zeros((), _jnp.float32), _jax.tree.map(_jnp.zeros_like, weights))
        (loss, grad_w), grad_x = _jax.lax.scan(body, init, (per_example, given["loss_target"]))
    with _jax.named_scope("update"):
        delta_w, new_m, new_v = {}, {}, {}
        for n in TWIN_WEIGHTS:
            delta_w[n], new_m[n], new_v[n] = _adamw(weights[n], grad_w[n], given["m_" + n], given["v_" + n])
    return (loss, grad_x, *[grad_w[n] for n in TWIN_WEIGHTS], *[delta_w[n] for n in TWIN_WEIGHTS],
            *[new_m[n] for n in TWIN_WEIGHTS], *[new_v[n] for n in TWIN_WEIGHTS])
```

```python
import functools

import jax
import jax.numpy as jnp
from jax import lax
from jax.experimental import pallas as pl
from jax.experimental.pallas import tpu as pltpu

F32 = jnp.float32
BF16 = jnp.bfloat16
MESH = pl.DeviceIdType.MESH

HEAD_DIM = 64
STATE = 128
GROUPS = 8
CONV_K = 4
CHUNK = 256
POOL_WINDOWS = (2, 4, 8, 16)
ALPHA = 2.0 ** 0.25
LN_EPS = 1e-5
RMS_EPS = 1e-5
LR, B1, B2, ADAM_EPS, WD, STEP = 0.001, 0.9, 0.999, 1e-08, 0.01, 10
N_DEV = 8
LANES = 128
SUBLANES = 8
VMEM_LIMIT = 56 * 1024 * 1024
NEG_BIG = -1e30

NN = (((1,), (0,)), ((), ()))
NT = (((1,), (1,)), ((), ()))
TN = (((0,), (0,)), ((), ()))


def _dot(a, b, dims=NN):
    return lax.dot_general(a.astype(BF16), b.astype(BF16), dims, preferred_element_type=F32)


def _dot_exact01(q, e, dims=NN):
    hi = q.astype(BF16)
    r1 = q - hi.astype(F32)
    mid = r1.astype(BF16)
    lo = (r1 - mid.astype(F32)).astype(BF16)
    f = lambda p: lax.dot_general(p, e, dims, preferred_element_type=F32)
    return f(hi) + f(mid) + f(lo)


def _params(sem):
    return pltpu.CompilerParams(dimension_semantics=sem, vmem_limit_bytes=VMEM_LIMIT)


def _sigmoid(x):
    return 1.0 / (1.0 + jnp.exp(-x))


def _colsum(x):
    return jnp.sum(x, axis=0, keepdims=True)


def _ln_fwd(r):
    mu = jnp.mean(r, axis=-1, keepdims=True)
    xc = r - mu
    var = jnp.mean(xc * xc, axis=-1, keepdims=True)
    rstd = lax.rsqrt(var + LN_EPS)
    return xc * rstd, rstd


def _ln_bwd(dy, xhat, rstd, g):
    dxh = dy * g
    m1 = jnp.mean(dxh, axis=-1, keepdims=True)
    m2 = jnp.mean(dxh * xhat, axis=-1, keepdims=True)
    return rstd * (dxh - m1 - xhat * m2)


def _matmul(a, b, mode, out_dtype, bm, bn, bk, name, a_fn=None):
    if mode == "nn":
        (m, k), n, dims = a.shape, b.shape[1], NN
    elif mode == "nt":
        (m, k), n, dims = a.shape, b.shape[0], NT
    else:
        (k, m), n, dims = a.shape, b.shape[1], TN
    bm, bn, bk = min(bm, m), min(bn, n), min(bk, k)
    assert m % bm == 0 and n % bn == 0 and k % bk == 0, (name, m, n, k, bm, bn, bk)
    nk = k // bk
    if mode == "nn":
        a_spec = pl.BlockSpec((bm, bk), lambda i, j, kk: (i, kk))
        b_spec = pl.BlockSpec((bk, bn), lambda i, j, kk: (kk, j))
    elif mode == "nt":
        a_spec = pl.BlockSpec((bm, bk), lambda i, j, kk: (i, kk))
        b_spec = pl.BlockSpec((bn, bk), lambda i, j, kk: (j, kk))
    else:
        a_spec = pl.BlockSpec((bk, bm), lambda i, j, kk: (kk, i))
        b_spec = pl.BlockSpec((bk, bn), lambda i, j, kk: (kk, j))

    def body(a_ref, b_ref, o_ref, acc_ref):
        kk = pl.program_id(2)

        @pl.when(kk == 0)
        def _():
            acc_ref[...] = jnp.zeros_like(acc_ref)

        av = a_ref[...]
        if a_fn is not None:
            av = a_fn(av.astype(F32))
        acc_ref[...] += _dot(av, b_ref[...], dims)

        @pl.when(kk == nk - 1)
        def _():
            o_ref[...] = acc_ref[...].astype(o_ref.dtype)

    return pl.pallas_call(
        body, name=name,
        grid=(m // bm, n // bn, nk),
        in_specs=[a_spec, b_spec],
        out_specs=pl.BlockSpec((bm, bn), lambda i, j, kk: (i, j)),
        out_shape=jax.ShapeDtypeStruct((m, n), out_dtype),
        scratch_shapes=[pltpu.VMEM((bm, bn), F32)],
        compiler_params=_params(("parallel", "parallel", "arbitrary")),
    )(a, b)


def _conv_fwd(proj, conv_w8, conv_b, n_seq_chunks, cd, ct):
    t = proj.shape[0]
    L = CHUNK
    nbc = t // L
    hb = L // SUBLANES

    def body(x_ref, halo_ref, w_ref, b_ref, o_ref, ext_ref):
        bc = pl.program_id(0)
        first = (bc % n_seq_chunks) == 0
        halo = halo_ref[...].astype(F32)
        ext_ref[0:SUBLANES, :] = jnp.where(first, 0.0, halo)
        ext_ref[SUBLANES:, :] = x_ref[...].astype(F32)
        acc = jnp.zeros((L, ct), F32) + b_ref[...]
        for k in range(CONV_K):
            acc = acc + w_ref[k:k + 1, :] * ext_ref[pl.ds(SUBLANES - (CONV_K - 1) + k, L), :]
        o_ref[...] = (acc * _sigmoid(acc)).astype(o_ref.dtype)

    return pl.pallas_call(
        body, name="conv_fwd",
        grid=(nbc, cd // ct),
        in_specs=[
            pl.BlockSpec((L, ct), lambda i, j: (i, j)),
            pl.BlockSpec((SUBLANES, ct), lambda i, j: (jnp.maximum(i * hb - 1, 0), j)),
            pl.BlockSpec((SUBLANES, ct), lambda i, j: (0, j)),
            pl.BlockSpec((1, ct), lambda i, j: (0, j)),
        ],
        out_specs=pl.BlockSpec((L, ct), lambda i, j: (i, j)),
        out_shape=jax.ShapeDtypeStruct((t, cd), BF16),
        scratch_shapes=[pltpu.VMEM((L + SUBLANES, ct), F32)],
        compiler_params=_params(("parallel", "parallel")),
    )(proj, proj, conv_w8, conv_b)


def _conv_bwd(proj, dxbc, conv_w8, conv_b, n_seq_chunks, col0, width, ct, name):
    t = proj.shape[0]
    L = CHUNK
    nbc = t // L
    hb = L // SUBLANES
    ct = min(ct, width)
    assert col0 % ct == 0 and width % ct == 0
    cb0 = col0 // ct
    last_hb = t // SUBLANES - 1

    def body(x_ref, xb_ref, xa_ref, d_ref, da_ref, w_ref, b_ref, o_ref, dw_ref, db_ref, ext_ref, dc_ref):
        bc = pl.program_id(1)
        first = (bc % n_seq_chunks) == 0
        last = (bc % n_seq_chunks) == n_seq_chunks - 1

        @pl.when(bc == 0)
        def _():
            dw_ref[...] = jnp.zeros_like(dw_ref)
            db_ref[...] = jnp.zeros_like(db_ref)

        ext_ref[0:SUBLANES, :] = jnp.where(first, 0.0, xb_ref[...].astype(F32))
        ext_ref[SUBLANES:SUBLANES + L, :] = x_ref[...].astype(F32)
        ext_ref[SUBLANES + L:, :] = xa_ref[...].astype(F32)
        le = L + SUBLANES
        acc = jnp.zeros((le, ct), F32) + b_ref[...]
        for k in range(CONV_K):
            acc = acc + w_ref[k:k + 1, :] * ext_ref[pl.ds(SUBLANES - (CONV_K - 1) + k, le), :]
        sg = _sigmoid(acc)
        dsilu = sg * (1.0 + acc * (1.0 - sg))
        dc_ref[0:L, :] = d_ref[...].astype(F32) * dsilu[0:L]
        dc_ref[L:, :] = jnp.where(last, 0.0, da_ref[...].astype(F32)) * dsilu[L:]
        dc = dc_ref[0:L, :]
        dx = jnp.zeros((L, ct), F32)
        for k in range(CONV_K):
            dx = dx + w_ref[k:k + 1, :] * dc_ref[pl.ds(CONV_K - 1 - k, L), :]
            dw_ref[k:k + 1, :] += _colsum(dc * ext_ref[pl.ds(SUBLANES - (CONV_K - 1) + k, L), :])
        db_ref[0:1, :] += _colsum(dc)
        o_ref[...] = dx.astype(o_ref.dtype)

    return pl.pallas_call(
        body, name=name,
        grid=(width // ct, nbc),
        in_specs=[
            pl.BlockSpec((L, ct), lambda j, i: (i, cb0 + j)),
            pl.BlockSpec((SUBLANES, ct), lambda j, i: (jnp.maximum(i * hb - 1, 0), cb0 + j)),
            pl.BlockSpec((SUBLANES, ct), lambda j, i: (jnp.minimum((i + 1) * hb, last_hb), cb0 + j)),
            pl.BlockSpec((L, ct), lambda j, i: (i, j)),
            pl.BlockSpec((SUBLANES, ct), lambda j, i: (jnp.minimum((i + 1) * hb, last_hb), j)),
            pl.BlockSpec((SUBLANES, ct), lambda j, i: (0, cb0 + j)),
            pl.BlockSpec((1, ct), lambda j, i: (0, cb0 + j)),
        ],
        out_specs=[
            pl.BlockSpec((L, ct), lambda j, i: (i, j)),
            pl.BlockSpec((SUBLANES, ct), lambda j, i: (0, j)),
            pl.BlockSpec((SUBLANES, ct), lambda j, i: (0, j)),
        ],
        out_shape=[
            jax.ShapeDtypeStruct((t, width), BF16),
            jax.ShapeDtypeStruct((SUBLANES, width), F32),
            jax.ShapeDtypeStruct((SUBLANES, width), F32),
        ],
        scratch_shapes=[pltpu.VMEM((L + 2 * SUBLANES, ct), F32), pltpu.VMEM((L + SUBLANES, ct), F32)],
        compiler_params=_params(("parallel", "arbitrary")),
    )(proj, proj, proj, dxbc, dxbc, conv_w8, conv_b)


def _cumsum_rows(x, reverse=False):
    n = x.shape[0]
    row = lax.broadcasted_iota(jnp.int32, x.shape, 0)
    s = 1
    while s < n:
        if reverse:
            x = x + jnp.where(row < n - s, pltpu.roll(x, n - s, 0), 0.0)
        else:
            x = x + jnp.where(row >= s, pltpu.roll(x, s, 0), 0.0)
        s *= 2
    return x


def _ssd_scalars(dtr, dtb, alog):
    pre = dtr + dtb
    dt = jnp.maximum(pre, 0.0) + jnp.log(1.0 + jnp.exp(-jnp.abs(pre)))
    a = -jnp.exp(alog)
    acs = _cumsum_rows(dt * a)
    return pre, dt, a, acs


def _ssd_group_common(xs, dt_s, acs_s, e_ref, gw):
    L = xs.shape[0]
    e = e_ref[...]
    dt_x = _dot_exact01(dt_s, e)
    acs_x = _dot_exact01(acs_s, e)
    e_x = jnp.exp(acs_x)
    a_last = acs_x[L - 1:L, :]
    dec_x = jnp.exp(a_last - acs_x)
    return dt_x, acs_x, e_x, dec_x


def _decay_matrix(acs_x, acs_t, r, tri):
    col = acs_x[:, r * HEAD_DIM:r * HEAD_DIM + 1]
    rowv = acs_t[r * HEAD_DIM:r * HEAD_DIM + 1, :]
    return jnp.exp(jnp.where(tri, col - rowv, NEG_BIG))


def _ssd_fwd(xbc, proj, dt_raw, dtb, alog, dskip_x, normw, emat, bl, inner, z_col0):
    t = xbc.shape[0]
    L = CHUNK
    nc = t // bl // L
    G = GROUPS
    gw = inner // G
    hpg = gw // HEAD_DIM
    assert z_col0 % gw == 0
    zb0 = z_col0 // gw
    bb0 = inner // STATE
    cb0 = bb0 + G

    def body(xs_ref, b_ref, c_ref, z_ref, dtr_ref, dtb_ref, alog_ref, dsk_ref, nw_ref, e_ref,
             y_ref, yn_ref, st_ref, h_ref, dt_s, acs_s):
        c = pl.program_id(1)
        g = pl.program_id(2)

        @pl.when(g == 0)
        def _():
            _, dt, _, acs = _ssd_scalars(dtr_ref[...], dtb_ref[...], alog_ref[...])
            dt_s[...] = dt
            acs_s[...] = acs

        @pl.when(c == 0)
        def _():
            h_ref[g] = jnp.zeros((STATE, gw), F32)

        xs = xs_ref[...].astype(F32)
        bg = b_ref[...]
        cg = c_ref[...]
        dt_x, acs_x, e_x, dec_x = _ssd_group_common(xs, dt_s[...], acs_s[...], e_ref, gw)
        xdt = xs * dt_x
        cb = _dot(cg, bg, NT)
        acs_t = acs_x.T
        h = h_ref[g]
        st_ref[0, 0] = h
        tri = lax.broadcasted_iota(jnp.int32, (L, L), 0) >= lax.broadcasted_iota(jnp.int32, (L, L), 1)
        lane = lax.broadcasted_iota(jnp.int32, (L, gw), 1)
        y = _dot(cg, h) * e_x + dsk_ref[...] * xs
        for r in range(hpg):
            lm = _decay_matrix(acs_x, acs_t, r, tri)
            m = cb * lm
            xr = jnp.where((lane >= r * HEAD_DIM) & (lane < (r + 1) * HEAD_DIM), xdt, 0.0)
            y = y + _dot(m, xr)
        h_ref[g] = h * e_x[L - 1:L, :] + _dot(bg, xdt * dec_x, TN)
        y_ref[...] = y.astype(y_ref.dtype)
        yq = y_ref[...].astype(F32)
        z = z_ref[...].astype(F32)
        yg = yq * (z * _sigmoid(z))
        rs = lax.rsqrt(jnp.mean(yg * yg, axis=-1, keepdims=True) + RMS_EPS)
        yn_ref[...] = (yg * rs * nw_ref[...]).astype(yn_ref.dtype)

    return pl.pallas_call(
        body, name="ssd_fwd",
        grid=(bl, nc, G),
        in_specs=[
            pl.BlockSpec((L, gw), lambda b, c, g: (b * nc + c, g)),
            pl.BlockSpec((L, STATE), lambda b, c, g: (b * nc + c, bb0 + g)),
            pl.BlockSpec((L, STATE), lambda b, c, g: (b * nc + c, cb0 + g)),
            pl.BlockSpec((L, gw), lambda b, c, g: (b * nc + c, zb0 + g)),
            pl.BlockSpec((L, LANES), lambda b, c, g: (b * nc + c, 0)),
            pl.BlockSpec((1, LANES), lambda b, c, g: (0, 0)),
            pl.BlockSpec((1, LANES), lambda b, c, g: (0, 0)),
            pl.BlockSpec((1, gw), lambda b, c, g: (0, g)),
            pl.BlockSpec((1, gw), lambda b, c, g: (0, g)),
            pl.BlockSpec((LANES, gw), lambda b, c, g: (0, g)),
        ],
        out_specs=[
            pl.BlockSpec((L, gw), lambda b, c, g: (b * nc + c, g)),
            pl.BlockSpec((L, gw), lambda b, c, g: (b * nc + c, g)),
            pl.BlockSpec((1, 1, STATE, gw), lambda b, c, g: (b * nc + c, g, 0, 0)),
        ],
        out_shape=[
            jax.ShapeDtypeStruct((t, inner), BF16),
            jax.ShapeDtypeStruct((t, inner), BF16),
            jax.ShapeDtypeStruct((bl * nc, G, STATE, gw), F32),
        ],
        scratch_shapes=[pltpu.VMEM((G, STATE, gw), F32), pltpu.VMEM((L, LANES), F32), pltpu.VMEM((L, LANES), F32)],
        compiler_params=_params(("arbitrary", "arbitrary", "arbitrary")),
    )(xbc, xbc, xbc, proj, dt_raw, dtb, alog, dskip_x, normw, emat)


def _ssd_bwd(xbc, proj, dt_raw, y, dyn, states, dtb, alog, dskip_x, normw, emat, emat_t, bl, inner, z_col0):
    t = xbc.shape[0]
    L = CHUNK
    nc = t // bl // L
    G = GROUPS
    gw = inner // G
    hpg = gw // HEAD_DIM
    zb0 = z_col0 // gw
    bb0 = inner // STATE
    cb0 = bb0 + G

    def rc(j):
        return nc - 1 - j

    def body(xs_ref, b_ref, c_ref, z_ref, dtr_ref, y_ref, dyn_ref, st_ref, dtb_ref, alog_ref, dsk_ref,
             nw_ref, e_ref, et_ref,
             dxs_ref, db_ref, dc_ref, dz_ref, ddt_ref, dnw_ref, dsk_acc, dalog_acc, ddtb_acc,
             dh_ref, pre_s, dt_s, acs_s, wacs_s, wdt_s):
        b = pl.program_id(0)
        j = pl.program_id(1)
        g = pl.program_id(2)

        @pl.when((b == 0) & (j == 0) & (g == 0))
        def _():
            dsk_acc[...] = jnp.zeros_like(dsk_acc)
            dalog_acc[...] = jnp.zeros_like(dalog_acc)
            ddtb_acc[...] = jnp.zeros_like(ddtb_acc)

        @pl.when((b == 0) & (j == 0))
        def _():
            dnw_ref[g] = jnp.zeros((SUBLANES, gw), F32)

        @pl.when(g == 0)
        def _():
            pre, dt, _, acs = _ssd_scalars(dtr_ref[...], dtb_ref[...], alog_ref[...])
            pre_s[...] = pre
            dt_s[...] = dt
            acs_s[...] = acs
            wacs_s[...] = jnp.zeros_like(wacs_s)
            wdt_s[...] = jnp.zeros_like(wdt_s)

        @pl.when(j == 0)
        def _():
            dh_ref[g] = jnp.zeros((STATE, gw), F32)

        xs = xs_ref[...].astype(F32)
        bg = b_ref[...]
        cg = c_ref[...]
        dt_x, acs_x, e_x, dec_x = _ssd_group_common(xs, dt_s[...], acs_s[...], e_ref, gw)
        xdt = xs * dt_x
        xdt_b = xdt.astype(BF16)
        cb = _dot(cg, bg, NT)
        acs_t = acs_x.T
        h = st_ref[0, 0]
        hb16 = h.astype(BF16)
        dsk = dsk_ref[...]

        yv = y_ref[...].astype(F32)
        z = z_ref[...].astype(F32)
        sgz = _sigmoid(z)
        sz = z * sgz
        yg = yv * sz
        rs = lax.rsqrt(jnp.mean(yg * yg, axis=-1, keepdims=True) + RMS_EPS)
        yhat = yg * rs
        dyn_v = dyn_ref[...].astype(F32)
        dnw_ref[g] += _colsum(dyn_v * yhat)
        dyh = dyn_v * nw_ref[...]
        dyg = rs * (dyh - yhat * jnp.mean(dyh * yhat, axis=-1, keepdims=True))
        dy = dyg * sz
        dz_ref[...] = (dyg * yv * (sgz * (1.0 + z * (1.0 - sgz)))).astype(dz_ref.dtype)

        tri = lax.broadcasted_iota(jnp.int32, (L, L), 0) >= lax.broadcasted_iota(jnp.int32, (L, L), 1)
        lane = lax.broadcasted_iota(jnp.int32, (L, gw), 1)
        dy_b = dy.astype(BF16)
        dcb = jnp.zeros((L, L), F32)
        dxdt_d = jnp.zeros((L, gw), F32)
        ydiag = jnp.zeros((L, gw), F32)
        for r in range(hpg):
            lm = _decay_matrix(acs_x, acs_t, r, tri)
            m = (cb * lm).astype(BF16)
            sel = (lane >= r * HEAD_DIM) & (lane < (r + 1) * HEAD_DIM)
            dyr = jnp.where(sel, dy_b, jnp.zeros_like(dy_b))
            xr = jnp.where(sel, xdt_b, jnp.zeros_like(xdt_b))
            ydiag = ydiag + _dot(m, xr)
            dcb = dcb + _dot(dyr, xdt_b, NT) * lm
            dxdt_d = dxdt_d + _dot(m, dyr, TN)
        dh = dh_ref[g]
        dh16 = dh.astype(BF16)
        xdec_b = (xdt * dec_x).astype(BF16)
        bdh = _dot(bg, dh16)
        dxdt = dxdt_d + dec_x * bdh
        dcb16 = dcb.astype(BF16)
        dye = (dy * e_x).astype(BF16)
        db_ref[...] = (_dot(dcb16, cg, TN) + _dot(xdec_b, dh16, NT)).astype(db_ref.dtype)
        dc_ref[...] = (_dot(dcb16, bg) + _dot(dye, hb16, NT)).astype(dc_ref.dtype)
        dprev = _dot(cg, dye, TN)
        cd_row = e_x[L - 1:L, :]
        s_new = _dot(bg, xdec_b, TN)
        last_term = _colsum(dh16.astype(F32) * s_new) + _colsum(dh * h) * cd_row
        rowi = lax.broadcasted_iota(jnp.int32, (L, gw), 0)
        yoff = _dot(cg, hb16) * e_x
        wfold = (dy_b.astype(F32) * ydiag + dy * yoff - dxdt_d * xdt_b.astype(F32) - bdh * xdec_b.astype(F32)
                 + jnp.where(rowi == L - 1, last_term, 0.0))
        et = et_ref[...]
        wacs_s[...] += _dot_exact01(wfold, et)
        wdt_s[...] += _dot_exact01(dxdt * xs, et)
        dsk_acc[...] += _dot_exact01(jnp.broadcast_to(_colsum(dy * xs), (SUBLANES, gw)), et)
        dxs_ref[...] = (dsk * dy + dxdt * dt_x).astype(dxs_ref.dtype)
        dh_ref[g] = dprev + cd_row * dh

        @pl.when(g == G - 1)
        def _():
            a = -jnp.exp(alog_ref[...])
            dda = _cumsum_rows(wacs_s[...], reverse=True)
            ddt = wdt_s[...] + dda * a
            ddt_raw = ddt * _sigmoid(pre_s[...])
            ddt_ref[...] = ddt_raw
            dalog_acc[...] += _colsum(dda * dt_s[...]) * a
            ddtb_acc[...] += _colsum(ddt_raw)

    def cidx(b, j):
        return b * nc + rc(j)

    accs = lambda shape: pl.BlockSpec(shape, lambda b, j, g: tuple(0 for _ in shape))
    return pl.pallas_call(
        body, name="ssd_bwd",
        grid=(bl, nc, G),
        in_specs=[
            pl.BlockSpec((L, gw), lambda b, j, g: (cidx(b, j), g)),
            pl.BlockSpec((L, STATE), lambda b, j, g: (cidx(b, j), bb0 + g)),
            pl.BlockSpec((L, STATE), lambda b, j, g: (cidx(b, j), cb0 + g)),
            pl.BlockSpec((L, gw), lambda b, j, g: (cidx(b, j), zb0 + g)),
            pl.BlockSpec((L, LANES), lambda b, j, g: (cidx(b, j), 0)),
            pl.BlockSpec((L, gw), lambda b, j, g: (cidx(b, j), g)),
            pl.BlockSpec((L, gw), lambda b, j, g: (cidx(b, j), g)),
            pl.BlockSpec((1, 1, STATE, gw), lambda b, j, g: (cidx(b, j), g, 0, 0)),
            pl.BlockSpec((1, LANES), lambda b, j, g: (0, 0)),
            pl.BlockSpec((1, LANES), lambda b, j, g: (0, 0)),
            pl.BlockSpec((1, gw), lambda b, j, g: (0, g)),
            pl.BlockSpec((1, gw), lambda b, j, g: (0, g)),
            pl.BlockSpec((LANES, gw), lambda b, j, g: (0, g)),
            pl.BlockSpec((gw, LANES), lambda b, j, g: (g, 0)),
        ],
        out_specs=[
            pl.BlockSpec((L, gw), lambda b, j, g: (cidx(b, j), g)),
            pl.BlockSpec((L, STATE), lambda b, j, g: (cidx(b, j), g)),
            pl.BlockSpec((L, STATE), lambda b, j, g: (cidx(b, j), g)),
            pl.BlockSpec((L, gw), lambda b, j, g: (cidx(b, j), g)),
            pl.BlockSpec((L, LANES), lambda b, j, g: (cidx(b, j), 0)),
            accs((G, SUBLANES, gw)),
            accs((SUBLANES, LANES)),
            accs((SUBLANES, LANES)),
            accs((SUBLANES, LANES)),
        ],
        out_shape=[
            jax.ShapeDtypeStruct((t, inner), BF16),
            jax.ShapeDtypeStruct((t, G * STATE), BF16),
            jax.ShapeDtypeStruct((t, G * STATE), BF16),
            jax.ShapeDtypeStruct((t, inner), BF16),
            jax.ShapeDtypeStruct((t, LANES), F32),
            jax.ShapeDtypeStruct((G, SUBLANES, gw), F32),
            jax.ShapeDtypeStruct((SUBLANES, LANES), F32),
            jax.ShapeDtypeStruct((SUBLANES, LANES), F32),
            jax.ShapeDtypeStruct((SUBLANES, LANES), F32),
        ],
        scratch_shapes=[pltpu.VMEM((G, STATE, gw), F32)] + [pltpu.VMEM((L, LANES), F32)] * 5,
        compiler_params=_params(("arbitrary", "arbitrary", "arbitrary")),
    )(xbc, xbc, xbc, proj, dt_raw, y, dyn, states, dtb, alog, dskip_x, normw, emat, emat_t)


def _pool_window(u, w, anti):
    n = u.shape[0]
    row = lax.broadcasted_iota(jnp.int32, u.shape, 0)
    acc = u
    s = 1
    while s < w:
        if anti:
            acc = acc + jnp.where(row < n - s, pltpu.roll(acc, n - s, 0), 0.0)
        else:
            acc = acc + jnp.where(row >= s, pltpu.roll(acc, s, 0), 0.0)
        s *= 2
    return acc


def _pool_cnt(shape, w):
    row = lax.broadcasted_iota(jnp.int32, shape, 0)
    return jnp.minimum(row + 1, w).astype(F32)


def _pool_fwd(proj, wpg, bl, d, u_col0):
    t = proj.shape[0]
    s = t // bl
    pg = len(POOL_WINDOWS)
    cg = d // pg
    ub0 = u_col0 // d

    def body(u_ref, w_ref, o_ref):
        for gi, w in enumerate(POOL_WINDOWS):
            u = u_ref[:, gi * cg:(gi + 1) * cg].astype(F32)
            pooled = _pool_window(u, w, False) / _pool_cnt(u.shape, w) - u
            o_ref[:, gi * cg:(gi + 1) * cg] = _dot(pooled, w_ref[gi]).astype(o_ref.dtype)

    return pl.pallas_call(
        body, name="pool_fwd",
        grid=(bl,),
        in_specs=[pl.BlockSpec((s, d), lambda b: (b, ub0)), pl.BlockSpec((pg, cg, cg), lambda b: (0, 0, 0))],
        out_specs=pl.BlockSpec((s, d), lambda b: (b, 0)),
        out_shape=jax.ShapeDtypeStruct((t, d), BF16),
        compiler_params=_params(("parallel",)),
    )(proj, wpg)


def _pool_bwd(proj, dyp, wpg, bl, d, u_col0):
    t = proj.shape[0]
    s = t // bl
    pg = len(POOL_WINDOWS)
    cg = d // pg
    ub0 = u_col0 // d

    def body(u_ref, dy_ref, w_ref, du_ref, dw_ref):
        @pl.when(pl.program_id(0) == 0)
        def _():
            dw_ref[...] = jnp.zeros_like(dw_ref)

        for gi, w in enumerate(POOL_WINDOWS):
            u = u_ref[:, gi * cg:(gi + 1) * cg].astype(F32)
            cnt = _pool_cnt(u.shape, w)
            pooled = _pool_window(u, w, False) / cnt - u
            dy = dy_ref[:, gi * cg:(gi + 1) * cg]
            dw_ref[gi] += _dot(pooled, dy, TN)
            dp = _dot(dy, w_ref[gi], NT)
            du_ref[:, gi * cg:(gi + 1) * cg] = (_pool_window(dp / cnt, w, True) - dp).astype(du_ref.dtype)

    return pl.pallas_call(
        body, name="pool_bwd",
        grid=(bl,),
        in_specs=[pl.BlockSpec((s, d), lambda b: (b, ub0)), pl.BlockSpec((s, d), lambda b: (b, 0)),
                  pl.BlockSpec((pg, cg, cg), lambda b: (0, 0, 0))],
        out_specs=[pl.BlockSpec((s, d), lambda b: (b, 0)), pl.BlockSpec((pg, cg, cg), lambda b: (0, 0, 0))],
        out_shape=[jax.ShapeDtypeStruct((t, d), BF16), jax.ShapeDtypeStruct((pg, cg, cg), F32)],
        compiler_params=_params(("arbitrary",)),
    )(proj, dyp, wpg)


def _merge_fwd(proj, ypr, yssd, x, w_out, b_gates, pool_scale, d, lg_col0, tm):
    t = x.shape[0]
    lb0 = lg_col0 // (2 * d)

    def body(lg_ref, yp_ref, ys_ref, x_ref, w_ref, bg_ref, ps_ref, mg_ref, r1_ref):
        lg = lg_ref[...].astype(F32) + bg_ref[...]
        ga = _sigmoid(lg[:, :d])
        gb = _sigmoid(lg[:, d:])
        merged = ga * (yp_ref[...].astype(F32) * ps_ref[...]) + gb * ys_ref[...].astype(F32)
        mg_ref[...] = merged.astype(mg_ref.dtype)
        r1_ref[...] = ALPHA * x_ref[...] + _dot(mg_ref[...], w_ref[...])

    row = lambda w: pl.BlockSpec((tm, w), lambda i: (i, 0))
    full = lambda a: pl.BlockSpec(a.shape, lambda i: (0, 0))
    return pl.pallas_call(
        body, name="merge_fwd",
        grid=(t // tm,),
        in_specs=[pl.BlockSpec((tm, 2 * d), lambda i: (i, lb0)), row(d), row(d), row(d), full(w_out), full(b_gates),
                  full(pool_scale)],
        out_specs=[row(d), row(d)],
        out_shape=[jax.ShapeDtypeStruct((t, d), BF16), jax.ShapeDtypeStruct((t, d), F32)],
        compiler_params=_params(("parallel",)),
    )(proj, ypr, yssd, x, w_out, b_gates, pool_scale)


def _merge_bwd(dr1, proj, ypr, yssd, w_out, b_gates, pool_scale, d, lg_col0, tm):
    t = dr1.shape[0]
    lb0 = lg_col0 // (2 * d)

    def body(dr_ref, lg_ref, yp_ref, ys_ref, w_ref, bg_ref, ps_ref, dlg_ref, dyp_ref, dys_ref, dbg_ref, dps_ref):
        @pl.when(pl.program_id(0) == 0)
        def _():
            dbg_ref[...] = jnp.zeros_like(dbg_ref)
            dps_ref[...] = jnp.zeros_like(dps_ref)

        dm = _dot(dr_ref[...], w_ref[...], NT)
        lg = lg_ref[...].astype(F32) + bg_ref[...]
        ga = _sigmoid(lg[:, :d])
        gb = _sigmoid(lg[:, d:])
        ypr_v = yp_ref[...].astype(F32)
        ys_v = ys_ref[...].astype(F32)
        ps = ps_ref[...]
        dga = dm * ypr_v * ps
        dla = dga * ga * (1.0 - ga)
        dlb = dm * ys_v * gb * (1.0 - gb)
        dlg_ref[:, :d] = dla.astype(dlg_ref.dtype)
        dlg_ref[:, d:] = dlb.astype(dlg_ref.dtype)
        dyp_ref[...] = (dm * ga * ps).astype(dyp_ref.dtype)
        dys_ref[...] = (dm * gb).astype(dys_ref.dtype)
        dbg_ref[0:1, :d] += _colsum(dla)
        dbg_ref[0:1, d:] += _colsum(dlb)
        dps_ref[0:1, :] += _colsum(dm * ga * ypr_v)

    row = lambda w: pl.BlockSpec((tm, w), lambda i: (i, 0))
    full = lambda a: pl.BlockSpec(a.shape, lambda i: (0, 0))
    acc = lambda w: pl.BlockSpec((SUBLANES, w), lambda i: (0, 0))
    return pl.pallas_call(
        body, name="merge_bwd",
        grid=(t // tm,),
        in_specs=[row(d), pl.BlockSpec((tm, 2 * d), lambda i: (i, lb0)), row(d), row(d), full(w_out), full(b_gates),
                  full(pool_scale)],
        out_specs=[row(2 * d), row(d), row(d), acc(2 * d), acc(d)],
        out_shape=[jax.ShapeDtypeStruct((t, 2 * d), BF16), jax.ShapeDtypeStruct((t, d), BF16),
                   jax.ShapeDtypeStruct((t, d), BF16), jax.ShapeDtypeStruct((SUBLANES, 2 * d), F32),
                   jax.ShapeDtypeStruct((SUBLANES, d), F32)],
        compiler_params=_params(("arbitrary",)),
    )(dr1, proj, ypr, yssd, w_out, b_gates, pool_scale)


def _mlp_fwd(r1, target, w_up, w_down, ln1_g, ln1_b, ln2_g, ln2_b, tm, tf):
    t, d = r1.shape
    ff = w_up.shape[1]
    nf = ff // tf

    def body(r1_ref, tg_ref, wu_ref, wd_ref, g1_ref, b1_ref, g2_ref, b2_ref,
             up_ref, h1_ref, dr2_ref, loss_ref, dg2_ref, db2_ref, h1f, acc):
        i = pl.program_id(0)
        f = pl.program_id(1)

        @pl.when((i == 0) & (f == 0))
        def _():
            loss_ref[...] = jnp.zeros_like(loss_ref)
            dg2_ref[...] = jnp.zeros_like(dg2_ref)
            db2_ref[...] = jnp.zeros_like(db2_ref)

        @pl.when(f == 0)
        def _():
            xhat, _ = _ln_fwd(r1_ref[...])
            h1 = xhat * g1_ref[...] + b1_ref[...]
            h1f[...] = h1
            h1_ref[...] = h1.astype(h1_ref.dtype)
            acc[...] = jnp.zeros_like(acc)

        up_ref[...] = _dot(h1_ref[...], wu_ref[...]).astype(up_ref.dtype)
        upq = jnp.maximum(up_ref[...].astype(F32), 0.0)
        acc[...] += _dot(upq * upq, wd_ref[...])

        @pl.when(f == nf - 1)
        def _():
            xhat, rstd = _ln_fwd(ALPHA * h1f[...] + acc[...])
            g2 = g2_ref[...]
            diff = xhat * g2 + b2_ref[...] - tg_ref[...]
            loss_ref[...] += 0.5 / d * jnp.sum(diff * diff)
            dh2 = diff * (1.0 / d)
            dg2_ref[0:1, :] += _colsum(dh2 * xhat)
            db2_ref[0:1, :] += _colsum(dh2)
            dr2_ref[...] = _ln_bwd(dh2, xhat, rstd, g2).astype(dr2_ref.dtype)

    row = pl.BlockSpec((tm, d), lambda i, f: (i, 0))
    vec = pl.BlockSpec((1, d), lambda i, f: (0, 0))
    acc8 = pl.BlockSpec((SUBLANES, d), lambda i, f: (0, 0))
    return pl.pallas_call(
        body, name="mlp_fwd",
        grid=(t // tm, nf),
        in_specs=[row, row, pl.BlockSpec((d, tf), lambda i, f: (0, f)), pl.BlockSpec((tf, d), lambda i, f: (f, 0)),
                  vec, vec, vec, vec],
        out_specs=[pl.BlockSpec((tm, tf), lambda i, f: (i, f)), row, row,
                   pl.BlockSpec((SUBLANES, LANES), lambda i, f: (0, 0)), acc8, acc8],
        out_shape=[jax.ShapeDtypeStruct((t, ff), BF16), jax.ShapeDtypeStruct((t, d), BF16),
                   jax.ShapeDtypeStruct((t, d), BF16), jax.ShapeDtypeStruct((SUBLANES, LANES), F32),
                   jax.ShapeDtypeStruct((SUBLANES, d), F32), jax.ShapeDtypeStruct((SUBLANES, d), F32)],
        scratch_shapes=[pltpu.VMEM((tm, d), F32), pltpu.VMEM((tm, d), F32)],
        compiler_params=_params(("arbitrary", "arbitrary")),
    )(r1, target, w_up, w_down, ln1_g, ln1_b, ln2_g, ln2_b)


def _mlp_bwd(dr2, up, r1, w_up, w_down, ln1_g, tm, tf):
    t, d = r1.shape
    ff = w_up.shape[1]
    nf = ff // tf

    def body(dr2_ref, up_ref, r1_ref, wu_ref, wd_ref, g1_ref, dup_ref, dr1_ref, dg1_ref, db1_ref, acc):
        i = pl.program_id(0)
        f = pl.program_id(1)

        @pl.when((i == 0) & (f == 0))
        def _():
            dg1_ref[...] = jnp.zeros_like(dg1_ref)
            db1_ref[...] = jnp.zeros_like(db1_ref)

        @pl.when(f == 0)
        def _():
            acc[...] = jnp.zeros_like(acc)

        dact = _dot(dr2_ref[...], wd_ref[...], NT)
        dup_ref[...] = (dact * 2.0 * jnp.maximum(up_ref[...].astype(F32), 0.0)).astype(dup_ref.dtype)
        acc[...] += _dot(dup_ref[...], wu_ref[...], NT)

        @pl.when(f == nf - 1)
        def _():
            dh1 = acc[...] + ALPHA * dr2_ref[...].astype(F32)
            xhat, rstd = _ln_fwd(r1_ref[...])
            dg1_ref[0:1, :] += _colsum(dh1 * xhat)
            db1_ref[0:1, :] += _colsum(dh1)
            dr1_ref[...] = _ln_bwd(dh1, xhat, rstd, g1_ref[...]).astype(dr1_ref.dtype)

    row = pl.BlockSpec((tm, d), lambda i, f: (i, 0))
    acc8 = pl.BlockSpec((SUBLANES, d), lambda i, f: (0, 0))
    return pl.pallas_call(
        body, name="mlp_bwd",
        grid=(t // tm, nf),
        in_specs=[row, pl.BlockSpec((tm, tf), lambda i, f: (i, f)), row,
                  pl.BlockSpec((d, tf), lambda i, f: (0, f)), pl.BlockSpec((tf, d), lambda i, f: (f, 0)),
                  pl.BlockSpec((1, d), lambda i, f: (0, 0))],
        out_specs=[pl.BlockSpec((tm, tf), lambda i, f: (i, f)), row, acc8, acc8],
        out_shape=[jax.ShapeDtypeStruct((t, ff), BF16), jax.ShapeDtypeStruct((t, d), BF16),
                   jax.ShapeDtypeStruct((SUBLANES, d), F32), jax.ShapeDtypeStruct((SUBLANES, d), F32)],
        scratch_shapes=[pltpu.VMEM((tm, d), F32)],
        compiler_params=_params(("arbitrary", "arbitrary")),
    )(dr2, up, r1, w_up, w_down, ln1_g)


def _dx_kernel(segs, w_main, ddt, w_dt, dr1, tm, tk):
    t, d = dr1.shape
    nblk = [s.shape[1] // tk for s in segs]
    starts = [sum(nblk[:i]) for i in range(len(segs))]
    nk = sum(nblk)
    nseg = len(segs)

    def body(*refs):
        seg_refs = refs[:nseg]
        w_ref, ddt_ref, wdt_ref, dr1_ref, o_ref, acc = refs[nseg:]
        k = pl.program_id(1)

        @pl.when(k == 0)
        def _():
            acc[...] = ALPHA * dr1_ref[...].astype(F32) + _dot(ddt_ref[...], wdt_ref[...], NT)

        for si in range(nseg):
            @pl.when((k >= starts[si]) & (k < starts[si] + nblk[si]))
            def _(si=si):
                acc[...] += _dot(seg_refs[si][...], w_ref[...], NT)

        @pl.when(k == nk - 1)
        def _():
            o_ref[...] = acc[...]

    def seg_spec(si):
        return pl.BlockSpec((tm, tk), lambda i, k: (i, jnp.clip(k - starts[si], 0, nblk[si] - 1)))

    row = pl.BlockSpec((tm, d), lambda i, k: (i, 0))
    return pl.pallas_call(
        body, name="dx",
        grid=(t // tm, nk),
        in_specs=[seg_spec(si) for si in range(nseg)] + [
            pl.BlockSpec((d, tk), lambda i, k: (0, k)), pl.BlockSpec((tm, LANES), lambda i, k: (i, 0)),
            pl.BlockSpec((d, LANES), lambda i, k: (0, 0)), row],
        out_specs=row,
        out_shape=jax.ShapeDtypeStruct((t, d), F32),
        scratch_shapes=[pltpu.VMEM((tm, d), F32)],
        compiler_params=_params(("parallel", "arbitrary")),
    )(*segs, w_main, ddt, w_dt, dr1)


def _dims(d):
    inner = 2 * d
    heads = inner // HEAD_DIM
    cd = inner + 2 * GROUPS * STATE
    assert heads <= LANES and inner % (GROUPS * LANES) == 0 and d % (len(POOL_WINDOWS) * LANES) == 0
    o_z, o_xbc, o_dt, o_lg = d, d + inner, d + inner + cd, d + inner + cd + heads
    return inner, heads, cd, (o_z, o_xbc, o_dt, o_lg)


def _row(v, width=None):
    v = v.reshape(1, -1).astype(F32)
    if width is not None and v.shape[1] < width:
        v = jnp.pad(v, ((0, 0), (0, width - v.shape[1])))
    return v


def _local_step(x2, tgt2, w, bl):
    t, d = x2.shape
    inner, heads, cd, (o_z, o_xbc, o_dt, o_lg) = _dims(d)
    gs = GROUPS * STATE
    nc = t // bl // CHUNK
    w_in = w["w_in"]
    w_main = jnp.concatenate([w_in[:, o_xbc:o_dt], w_in[:, o_z:o_xbc], w_in[:, o_lg:], w_in[:, :o_z]], axis=1).astype(BF16)
    w_dt = jnp.pad(w_in[:, o_dt:o_lg], ((0, 0), (0, LANES - heads))).astype(BF16)
    c_z, c_lg, c_u = cd, cd + inner, cd + inner + 2 * d
    conv_w8 = jnp.pad(w["conv_w"].astype(F32), ((0, SUBLANES - CONV_K), (0, 0)))
    conv_b = _row(w["conv_b"])
    dtb, alog = _row(w["dt_bias"], LANES), _row(w["a_log"], LANES)
    dskip_x = _row(jnp.repeat(w["d_skip"].reshape(-1), HEAD_DIM))
    normw = _row(w["ssd_norm_w"])
    col_head = lax.broadcasted_iota(jnp.int32, (LANES, inner), 1) // HEAD_DIM
    emat = (col_head == lax.broadcasted_iota(jnp.int32, (LANES, inner), 0)).astype(BF16)
    emat_t = emat.T
    w_ssd, wpg, w_out = w["w_ssd_proj"].astype(BF16), w["w_pool_group"].astype(BF16), w["w_out"].astype(BF16)
    w_up, w_down = w["w_up"].astype(BF16), w["w_down"].astype(BF16)
    b_gates, pool_scale = _row(w["b_gates"]), _row(w["pool_scale"])
    ln1_g, ln1_b, ln2_g, ln2_b = _row(w["ln1_g"]), _row(w["ln1_b"]), _row(w["ln2_g"]), _row(w["ln2_b"])

    tm = min(512, t)
    tk = min(1024, d)
    ct = min(1024, d)
    mm = functools.partial(_matmul, bm=1024, bn=tk, bk=1024)
    xb = x2.astype(BF16)

    proj = mm(xb, w_main, "nn", BF16, name="in_proj")
    dt_raw = mm(xb, w_dt, "nn", F32, name="in_proj_dt")
    xbc = _conv_fwd(proj, conv_w8, conv_b, nc, cd, ct)
    y, yn, states = _ssd_fwd(xbc, proj, dt_raw, dtb, alog, dskip_x, normw, emat, bl, inner, c_z)
    yssd = mm(yn, w_ssd, "nn", BF16, name="ssd_proj")
    ypr = _pool_fwd(proj, wpg, bl, d, c_u)
    merged, r1 = _merge_fwd(proj, ypr, yssd, x2, w_out, b_gates, pool_scale, d, c_lg, tm)
    up, h1, dr2, loss8, dg2, db2 = _mlp_fwd(r1, tgt2, w_up, w_down, ln1_g, ln1_b, ln2_g, ln2_b, tm, tk)

    dup, dr1, dg1, db1 = _mlp_bwd(dr2, up, r1, w_up, w_down, ln1_g, tm, tk)
    relu2 = lambda v: jnp.square(jnp.maximum(v, 0.0))
    g = {}
    g["w_down"] = mm(up, dr2, "tn", F32, name="dw_down", a_fn=relu2)
    g["w_up"] = mm(h1, dup, "tn", F32, name="dw_up")
    g["w_out"] = mm(merged, dr1, "tn", F32, name="dw_out")
    dlg, dyp, dys, dbg, dps = _merge_bwd(dr1, proj, ypr, yssd, w_out, b_gates, pool_scale, d, c_lg, tm)
    du, g["w_pool_group"] = _pool_bwd(proj, dyp, wpg, bl, d, c_u)
    dyn = mm(dys, w_ssd, "nt", BF16, name="d_ssd_proj")
    g["w_ssd_proj"] = mm(yn, dys, "tn", F32, name="dw_ssd_proj")
    dxs, dbm, dcm, dz, ddt, dnw, dsk, dalog, ddtb = _ssd_bwd(
        xbc, proj, dt_raw, y, dyn, states, dtb, alog, dskip_x, normw, emat, emat_t, bl, inner, c_z)
    dxs_p, dcw_x, dcb_x = _conv_bwd(proj, dxs, conv_w8, conv_b, nc, 0, inner, ct, "conv_bwd_x")
    dbm_p, dcw_b, dcb_b = _conv_bwd(proj, dbm, conv_w8, conv_b, nc, inner, gs, ct, "conv_bwd_b")
    dcm_p, dcw_c, dcb_c = _conv_bwd(proj, dcm, conv_w8, conv_b, nc, inner + gs, gs, ct, "conv_bwd_c")
    segs = [dxs_p, dbm_p, dcm_p, dz, dlg, du]
    grad_x = _dx_kernel(segs, w_main, ddt, w_dt, dr1, tm, tk)
    dws = [mm(xb, s, "tn", F32, name="dw_in_%d" % i) for i, s in enumerate(segs)]
    dw_dt = mm(xb, ddt, "tn", F32, name="dw_in_dt")
    g["w_in"] = jnp.concatenate([dws[5], dws[3], dws[0], dws[1], dws[2], dw_dt[:, :heads], dws[4]], axis=1)
    g["conv_w"] = jnp.concatenate([dcw_x, dcw_b, dcw_c], axis=1)[:CONV_K]
    g["conv_b"] = jnp.concatenate([dcb_x, dcb_b, dcb_c], axis=1)[0]
    g["b_gates"], g["pool_scale"] = dbg[0], dps[0]
    g["dt_bias"], g["a_log"], g["d_skip"] = ddtb[0, :heads], dalog[0, :heads], dsk[0, :heads]
    g["ssd_norm_w"] = dnw[:, 0, :].reshape(inner)
    g["ln1_g"], g["ln1_b"], g["ln2_g"], g["ln2_b"] = dg1[0], db1[0], dg2[0], db2[0]
    return loss8, grad_x, g


BIG = (("w_in", 1), ("conv_w", 1), ("w_ssd_proj", 0), ("w_pool_group", 1), ("w_out", 0), ("w_up", 1), ("w_down", 0))
SMALL = ("b_gates", "conv_b", "dt_bias", "a_log", "d_skip", "ssd_norm_w", "pool_scale", "ln1_g", "ln1_b", "ln2_g",
         "ln2_b")
NAMES = ("w_in", "b_gates", "conv_w", "conv_b", "dt_bias", "a_log", "d_skip", "ssd_norm_w", "w_ssd_proj",
         "w_pool_group", "pool_scale", "w_out", "ln1_g", "ln1_b", "w_up", "w_down", "ln2_g", "ln2_b")
PACK_COLS = 1024
PACK_ROW_UNIT = 128


def _size(shape):
    n = 1
    for s in shape:
        n *= s
    return n


def _packed_rows(local_shapes):
    rows = -(-sum(_size(s) for s in local_shapes) // PACK_COLS)
    return -(-rows // PACK_ROW_UNIT) * PACK_ROW_UNIT


def _row_tile(rows):
    n = rows // PACK_ROW_UNIT
    k = max(k for k in (1, 2, 3) if n % k == 0)
    return k * PACK_ROW_UNIT


def _pack_flat(parts, rows):
    flat = jnp.concatenate(parts, axis=-1)
    lead = flat.shape[:-1]
    flat = jnp.pad(flat, [(0, 0)] * len(lead) + [(0, rows * PACK_COLS - flat.shape[-1])])
    return flat.reshape(lead + (rows, PACK_COLS))


def _pack_local(loc, rows):
    return _pack_flat([loc[n].reshape(-1) for n, _ in BIG], rows)


def _unpack_local(buf, shapes):
    flat, out, off = buf.reshape(-1), {}, 0
    for n, _ in BIG:
        size = 1
        for s in shapes[n]:
            size *= s
        out[n] = flat[off:off + size].reshape(shapes[n])
        off += size
    return out


def _pack_dest(full, rows):
    parts = []
    for n, ax in BIG:
        a = full[n]
        blocks = jnp.stack(jnp.split(a, N_DEV, axis=ax))
        parts.append(blocks.reshape(N_DEV, -1))
    return _pack_flat(parts, rows)


def _unpack_full(buf, shapes):
    flat, out, off = buf.reshape(N_DEV, -1), {}, 0
    for n, ax in BIG:
        size = 1
        for s in shapes[n]:
            size *= s
        blocks = flat[:, off:off + size].reshape((N_DEV,) + tuple(shapes[n]))
        out[n] = jnp.concatenate([blocks[k] for k in range(N_DEV)], axis=ax)
        off += size
    return out


def _small_rows(loc):
    return sum(-(-loc[n].size // LANES) for n in SMALL) + 1


def _pack_small(vals, extra):
    parts = []
    for n in SMALL:
        v = vals[n].reshape(-1).astype(F32)
        parts.append(jnp.pad(v, (0, -v.size % LANES)))
    parts.append(jnp.pad(extra.reshape(1).astype(F32), (0, LANES - 1)))
    flat = jnp.concatenate(parts)
    rows = flat.size // LANES
    flat = jnp.pad(flat, (0, (-rows % SUBLANES) * LANES))
    return flat.reshape(-1, LANES)


def _unpack_small(buf, shapes):
    flat, out, off = buf.reshape(-1), {}, 0
    for n in SMALL:
        size = 1
        for s in shapes[n]:
            size *= s
        out[n] = flat[off:off + size].reshape(shapes[n])
        off += -(-size // LANES) * LANES
    return out, flat[off]


def _mesh_pos():
    return lax.axis_index("x"), lax.axis_index("y"), lax.axis_index("c")


ANY = pl.BlockSpec(memory_space=pl.ANY)


def _all_gather(own):
    r, c_ = own.shape

    def body(x_ref, out_ref, send_sems, recv_sems, local_sem):
        x, y, c = _mesh_pos()
        me, sibling = (x, y, c), (x, y, 1 - c)
        chips = [(1 - x, y), (x, 1 - y), (1 - x, 1 - y)]

        def blk(px, py, pc):
            return out_ref.at[4 * px + 2 * py + pc]

        def copy(k, block, to, src=None):
            return pltpu.make_async_remote_copy(
                src_ref=blk(*block) if src is None else src, dst_ref=blk(*block),
                send_sem=send_sems.at[k], recv_sem=recv_sems.at[k], device_id=to, device_id_type=MESH)

        mine = pltpu.make_async_copy(x_ref, blk(*me), local_sem)
        mine.start()
        first = [copy(0, me, sibling, src=x_ref)]
        first += [copy(1 + j, me, (*chip, c), src=x_ref) for j, chip in enumerate(chips)]
        for cp in first:
            cp.start()
        passed = [copy(4 + j, (*chip, c), sibling) for j, chip in enumerate(chips)]
        for j, chip in enumerate(chips):
            copy(1 + j, (*chip, c), me).wait_recv()
            passed[j].start()
        copy(0, sibling, me).wait_recv()
        for j, chip in enumerate(chips):
            copy(4 + j, (*chip, 1 - c), me).wait_recv()
        for cp in first + passed:
            cp.wait_send()
        mine.wait()

    return pl.pallas_call(
        body, name="all_gather_weights",
        in_specs=[ANY], out_specs=ANY,
        out_shape=jax.ShapeDtypeStruct((N_DEV, r, c_), own.dtype),
        scratch_shapes=[pltpu.SemaphoreType.DMA((7,)), pltpu.SemaphoreType.DMA((7,)), pltpu.SemaphoreType.DMA],
    )(own)


def _rs_sibling(to_sibling):
    def body(s_ref, o_ref, send_sem, recv_sem):
        x, y, c = _mesh_pos()
        cp = pltpu.make_async_remote_copy(src_ref=s_ref, dst_ref=o_ref, send_sem=send_sem, recv_sem=recv_sem,
                                          device_id=(x, y, 1 - c), device_id_type=MESH)
        cp.start()
        cp.wait()

    return pl.pallas_call(
        body, name="rs_sibling",
        in_specs=[ANY], out_specs=ANY,
        out_shape=jax.ShapeDtypeStruct(to_sibling.shape, to_sibling.dtype),
        scratch_shapes=[pltpu.SemaphoreType.DMA, pltpu.SemaphoreType.DMA],
    )(to_sibling)


def _rs_chips(tb):
    def body(t_ref, o_ref, send_sems, recv_sems, local_sem):
        x, y, c = _mesh_pos()
        p = 2 * x + y
        chips = [(1 - x, y), (x, 1 - y), (1 - x, 1 - y)]
        own = pltpu.make_async_copy(t_ref.at[p], o_ref.at[p], local_sem)
        own.start()
        cps = []
        for j, (qx, qy) in enumerate(chips):
            cps.append(pltpu.make_async_remote_copy(
                src_ref=t_ref.at[2 * qx + qy], dst_ref=o_ref.at[p], send_sem=send_sems.at[j],
                recv_sem=recv_sems.at[j], device_id=(qx, qy, c), device_id_type=MESH))
        for cp in cps:
            cp.start()
        for j, (qx, qy) in enumerate(chips):
            pltpu.make_async_remote_copy(
                src_ref=t_ref.at[p], dst_ref=o_ref.at[2 * qx + qy], send_sem=send_sems.at[j],
                recv_sem=recv_sems.at[j], device_id=(qx, qy, c), device_id_type=MESH).wait_recv()
        for cp in cps:
            cp.wait_send()
        own.wait()

    return pl.pallas_call(
        body, name="rs_chips",
        in_specs=[ANY], out_specs=ANY,
        out_shape=jax.ShapeDtypeStruct(tb.shape, tb.dtype),
        scratch_shapes=[pltpu.SemaphoreType.DMA((3,)), pltpu.SemaphoreType.DMA((3,)), pltpu.SemaphoreType.DMA],
    )(tb)


def _add_pairs(a, b):
    n, r, c_ = a.shape
    tr = _row_tile(r)

    def body(a_ref, b_ref, o_ref):
        o_ref[...] = (a_ref[...].astype(F32) + b_ref[...].astype(F32)).astype(o_ref.dtype)

    spec = pl.BlockSpec((1, tr, c_), lambda q, i: (q, i, 0))
    return pl.pallas_call(
        body, name="rs_add", grid=(n, r // tr), in_specs=[spec, spec], out_specs=spec,
        out_shape=jax.ShapeDtypeStruct(a.shape, BF16), compiler_params=_params(("parallel", "parallel")),
    )(a, b)


def _small_allreduce(vec):
    rows = vec.shape[0]

    def body(x_ref, o_ref, buf, send_sems, recv_sems):
        x, y, c = _mesh_pos()
        me = 4 * x + 2 * y + c
        buf[me] = x_ref[...]
        cps = []
        for k in range(1, N_DEV):
            peer = (1 - x if k & 4 else x, 1 - y if k & 2 else y, 1 - c if k & 1 else c)
            cps.append(pltpu.make_async_remote_copy(
                src_ref=x_ref, dst_ref=buf.at[me], send_sem=send_sems.at[k - 1], recv_sem=recv_sems.at[k - 1],
                device_id=peer, device_id_type=MESH))
        for cp in cps:
            cp.start()
        for k in range(1, N_DEV):
            px, py, pc = (1 - x if k & 4 else x, 1 - y if k & 2 else y, 1 - c if k & 1 else c)
            pltpu.make_async_remote_copy(
                src_ref=x_ref, dst_ref=buf.at[4 * px + 2 * py + pc], send_sem=send_sems.at[k - 1],
                recv_sem=recv_sems.at[k - 1], device_id=(px, py, pc), device_id_type=MESH).wait_recv()
        for cp in cps:
            cp.wait_send()
        acc = buf[0]
        for k in range(1, N_DEV):
            acc = acc + buf[k]
        o_ref[...] = acc

    vm = pl.BlockSpec(memory_space=pltpu.VMEM)
    return pl.pallas_call(
        body, name="small_allreduce",
        in_specs=[vm], out_specs=vm,
        out_shape=jax.ShapeDtypeStruct(vec.shape, F32),
        scratch_shapes=[pltpu.VMEM((N_DEV, rows, LANES), F32), pltpu.SemaphoreType.DMA((N_DEV - 1,)),
                        pltpu.SemaphoreType.DMA((N_DEV - 1,))],
    )(vec)


def _adamw(gparts, w, m, v, name):
    n, r, c_ = gparts.shape
    tr = _row_tile(r) if r % PACK_ROW_UNIT == 0 else r
    c1 = 1.0 / (1.0 - B1 ** STEP)
    c2 = 1.0 / (1.0 - B2 ** STEP)

    def body(g_ref, w_ref, m_ref, v_ref, go_ref, d_ref, mo_ref, vo_ref):
        g = g_ref[0].astype(F32)
        for q in range(1, n):
            g = g + g_ref[q].astype(F32)
        mn = B1 * m_ref[...] + (1.0 - B1) * g
        vn = B2 * v_ref[...] + (1.0 - B2) * (g * g)
        go_ref[...] = g
        mo_ref[...] = mn
        vo_ref[...] = vn
        d_ref[...] = -LR * ((mn * c1) / (jnp.sqrt(vn * c2) + ADAM_EPS) + WD * w_ref[...])

    spec = pl.BlockSpec((tr, c_), lambda i: (i, 0))
    out = jax.ShapeDtypeStruct((r, c_), F32)
    return pl.pallas_call(
        body, name=name, grid=(r // tr,),
        in_specs=[pl.BlockSpec((n, tr, c_), lambda i: (0, i, 0)), spec, spec, spec],
        out_specs=[spec] * 4, out_shape=[out] * 4, compiler_params=_params(("parallel",)),
    )(gparts, w, m, v)


def kernel(x, w_in, b_gates, conv_w, conv_b, dt_bias, a_log, d_skip, ssd_norm_w, w_ssd_proj, w_pool_group, pool_scale, w_out, ln1_g, ln1_b, w_up, w_down, ln2_g, ln2_b, loss_target, m_w_in, m_b_gates, m_conv_w, m_conv_b, m_dt_bias, m_a_log, m_d_skip, m_ssd_norm_w, m_w_ssd_proj, m_w_pool_group, m_pool_scale, m_w_out, m_ln1_g, m_ln1_b, m_w_up, m_w_down, m_ln2_g, m_ln2_b, v_w_in, v_b_gates, v_conv_w, v_conv_b, v_dt_bias, v_a_log, v_d_skip, v_ssd_norm_w, v_w_ssd_proj, v_w_pool_group, v_pool_scale, v_w_out, v_ln1_g, v_ln1_b, v_w_up, v_w_down, v_ln2_g, v_ln2_b):
    ws = (w_in, b_gates, conv_w, conv_b, dt_bias, a_log, d_skip, ssd_norm_w, w_ssd_proj, w_pool_group, pool_scale,
          w_out, ln1_g, ln1_b, w_up, w_down, ln2_g, ln2_b)
    ms = (m_w_in, m_b_gates, m_conv_w, m_conv_b, m_dt_bias, m_a_log, m_d_skip, m_ssd_norm_w, m_w_ssd_proj,
          m_w_pool_group, m_pool_scale, m_w_out, m_ln1_g, m_ln1_b, m_w_up, m_w_down, m_ln2_g, m_ln2_b)
    vs = (v_w_in, v_b_gates, v_conv_w, v_conv_b, v_dt_bias, v_a_log, v_d_skip, v_ssd_norm_w, v_w_ssd_proj,
          v_w_pool_group, v_pool_scale, v_w_out, v_ln1_g, v_ln1_b, v_w_up, v_w_down, v_ln2_g, v_ln2_b)
    w = {n: a[0] for n, a in zip(NAMES, ws)}
    m = {n: a[0] for n, a in zip(NAMES, ms)}
    v = {n: a[0] for n, a in zip(NAMES, vs)}
    out_shapes = {n: a.shape for n, a in zip(NAMES, ws)}
    bl, s, d = x.shape
    x2, tgt2 = x.reshape(bl * s, d), loss_target.reshape(bl * s, d)
    ci = lax.axis_index("c")

    big_shapes = {n: w[n].shape for n, _ in BIG}
    small_shapes = {n: w[n].shape for n in SMALL}
    rows = _packed_rows([big_shapes[n] for n, _ in BIG] + [big_shapes["conv_w"]])
    w_pk, m_pk, v_pk = (_pack_local(t_, rows) for t_ in (w, m, v))

    conv_hi = w["conv_w"].astype(BF16)
    conv_lo = (w["conv_w"] - conv_hi.astype(F32)).astype(BF16)
    own = _pack_flat([w[n].astype(BF16).reshape(-1) for n, _ in BIG] + [conv_lo.reshape(-1)], rows)
    gathered = _all_gather(own)
    full = _unpack_full(gathered, big_shapes)
    lo_off = sum(_size(big_shapes[n]) for n, _ in BIG)
    lo = gathered.reshape(N_DEV, -1)[:, lo_off:lo_off + _size(big_shapes["conv_w"])]
    lo = lo.reshape((N_DEV,) + tuple(big_shapes["conv_w"]))
    full["conv_w"] = full["conv_w"].astype(F32) + jnp.concatenate([lo[k] for k in range(N_DEV)], axis=1).astype(F32)
    full.update({n: w[n] for n in SMALL})
    loss8, grad_x, g = _local_step(x2, tgt2, full, bl)

    zero = jnp.zeros((), F32)
    small_sum = _small_allreduce(_pack_small(g, loss8[0, 0]))
    ws_pk, ms_pk, vs_pk = (_pack_small(t_, zero) for t_ in (w, m, v))
    gs_o, ds_o, mms_o, vvs_o = _adamw(small_sum[None], ws_pk, ms_pk, vs_pk, "adamw_small")
    gs, loss = _unpack_small(gs_o, small_shapes)
    ds, _ = _unpack_small(ds_o, small_shapes)
    mms, _ = _unpack_small(mms_o, small_shapes)
    vvs, _ = _unpack_small(vvs_o, small_shapes)

    part = _pack_dest(g, rows).astype(BF16).reshape(N_DEV // 2, 2, rows, PACK_COLS)
    mine = lax.dynamic_index_in_dim(part, ci, axis=1, keepdims=False)
    sib = lax.dynamic_index_in_dim(part, 1 - ci, axis=1, keepdims=False)
    chip_sum = _add_pairs(mine, _rs_sibling(sib))
    gb_o, db_o, mb_o, vb_o = _adamw(_rs_chips(chip_sum), w_pk, m_pk, v_pk, "adamw_big")
    gb, db, mb, vb = (_unpack_local(t_, big_shapes) for t_ in (gb_o, db_o, mb_o, vb_o))

    def ordered(small, big):
        return [(big[n] if n in big else small[n]).reshape(out_shapes[n]) for n in NAMES]

    return (loss, grad_x.reshape(bl, s, d), *ordered(gs, gb), *ordered(ds, db), *ordered(mms, mb),
            *ordered(vvs, vb))
```

```python
import functools

import jax
import jax.numpy as jnp
from jax import lax
from jax.experimental import pallas as pl
from jax.experimental.pallas import tpu as pltpu

F32 = jnp.float32
BF16 = jnp.bfloat16
MESH = pl.DeviceIdType.MESH

HEAD_DIM = 64
STATE = 128
GROUPS = 8
CONV_K = 4
CHUNK = 256
POOL_WINDOWS = (2, 4, 8, 16)
ALPHA = 2.0 ** 0.25
LN_EPS = 1e-5
RMS_EPS = 1e-5
LR, B1, B2, ADAM_EPS, WD, STEP = 0.001, 0.9, 0.999, 1e-08, 0.01, 10
N_DEV = 8
LANES = 128
SUBLANES = 8
VMEM_LIMIT = 56 * 1024 * 1024
NEG_BIG = -1e30

NN = (((1,), (0,)), ((), ()))
NT = (((1,), (1,)), ((), ()))
TN = (((0,), (0,)), ((), ()))


def _dot(a, b, dims=NN):
    return lax.dot_general(a.astype(BF16), b.astype(BF16), dims, preferred_element_type=F32)


def _dot_exact01(q, e, dims=NN):
    hi = q.astype(BF16)
    r1 = q - hi.astype(F32)
    mid = r1.astype(BF16)
    lo = (r1 - mid.astype(F32)).astype(BF16)
    f = lambda p: lax.dot_general(p, e, dims, preferred_element_type=F32)
    return f(hi) + f(mid) + f(lo)


def _params(sem):
    return pltpu.CompilerParams(dimension_semantics=sem, vmem_limit_bytes=VMEM_LIMIT)


def _sigmoid(x):
    return 1.0 / (1.0 + jnp.exp(-x))


def _colsum(x):
    return jnp.sum(x, axis=0, keepdims=True)


def _ln_fwd(r):
    mu = jnp.mean(r, axis=-1, keepdims=True)
    xc = r - mu
    var = jnp.mean(xc * xc, axis=-1, keepdims=True)
    rstd = lax.rsqrt(var + LN_EPS)
    return xc * rstd, rstd


def _ln_bwd(dy, xhat, rstd, g):
    dxh = dy * g
    m1 = jnp.mean(dxh, axis=-1, keepdims=True)
    m2 = jnp.mean(dxh * xhat, axis=-1, keepdims=True)
    return rstd * (dxh - m1 - xhat * m2)


def _matmul(a, b, mode, out_dtype, bm, bn, bk, name, a_fn=None, col_blocks=0):
    if mode == "nn":
        (m, k), n, dims = a.shape, b.shape[1], NN
    elif mode == "nt":
        (m, k), n, dims = a.shape, b.shape[0], NT
    else:
        (k, m), n, dims = a.shape, b.shape[1], TN
    bm, bn, bk = min(bm, m), min(bn, n), min(bk, k)
    assert m % bm == 0 and n % bn == 0 and k % bk == 0, (name, m, n, k, bm, bn, bk)
    nk = k // bk
    if mode == "nn":
        a_spec = pl.BlockSpec((bm, bk), lambda i, j, kk: (i, kk))
        b_spec = pl.BlockSpec((bk, bn), lambda i, j, kk: (kk, j))
    elif mode == "nt":
        a_spec = pl.BlockSpec((bm, bk), lambda i, j, kk: (i, kk))
        b_spec = pl.BlockSpec((bn, bk), lambda i, j, kk: (j, kk))
    else:
        a_spec = pl.BlockSpec((bk, bm), lambda i, j, kk: (kk, i))
        b_spec = pl.BlockSpec((bk, bn), lambda i, j, kk: (kk, j))

    def body(a_ref, b_ref, o_ref, acc_ref):
        kk = pl.program_id(2)

        @pl.when(kk == 0)
        def _():
            acc_ref[...] = jnp.zeros_like(acc_ref)

        av = a_ref[...]
        if a_fn is not None:
            av = a_fn(av.astype(F32))
        acc_ref[...] += _dot(av, b_ref[...], dims)

        @pl.when(kk == nk - 1)
        def _():
            o_ref[...] = acc_ref[...].astype(o_ref.dtype).reshape(o_ref.shape)

    if col_blocks:
        assert n == col_blocks * bn, (name, n, col_blocks, bn)
        out_spec = pl.BlockSpec((1, bm, bn), lambda i, j, kk: (j, i, 0))
        out_shape = jax.ShapeDtypeStruct((col_blocks, m, bn), out_dtype)
    else:
        out_spec = pl.BlockSpec((bm, bn), lambda i, j, kk: (i, j))
        out_shape = jax.ShapeDtypeStruct((m, n), out_dtype)
    return pl.pallas_call(
        body, name=name,
        grid=(m // bm, n // bn, nk),
        in_specs=[a_spec, b_spec],
        out_specs=out_spec,
        out_shape=out_shape,
        scratch_shapes=[pltpu.VMEM((bm, bn), F32)],
        compiler_params=_params(("parallel", "parallel", "arbitrary")),
    )(a, b)


def _conv_fwd(proj, conv_w8, conv_b, n_seq_chunks, cd, ct):
    t = proj.shape[0]
    L = CHUNK
    nbc = t // L
    hb = L // SUBLANES

    def body(x_ref, halo_ref, w_ref, b_ref, o_ref, ext_ref):
        bc = pl.program_id(0)
        first = (bc % n_seq_chunks) == 0
        halo = halo_ref[...].astype(F32)
        ext_ref[0:SUBLANES, :] = jnp.where(first, 0.0, halo)
        ext_ref[SUBLANES:, :] = x_ref[...].astype(F32)
        acc = jnp.zeros((L, ct), F32) + b_ref[...]
        for k in range(CONV_K):
            acc = acc + w_ref[k:k + 1, :] * ext_ref[pl.ds(SUBLANES - (CONV_K - 1) + k, L), :]
        o_ref[...] = (acc * _sigmoid(acc)).astype(o_ref.dtype)

    return pl.pallas_call(
        body, name="conv_fwd",
        grid=(nbc, cd // ct),
        in_specs=[
            pl.BlockSpec((L, ct), lambda i, j: (i, j)),
            pl.BlockSpec((SUBLANES, ct), lambda i, j: (jnp.maximum(i * hb - 1, 0), j)),
            pl.BlockSpec((SUBLANES, ct), lambda i, j: (0, j)),
            pl.BlockSpec((1, ct), lambda i, j: (0, j)),
        ],
        out_specs=pl.BlockSpec((L, ct), lambda i, j: (i, j)),
        out_shape=jax.ShapeDtypeStruct((t, cd), BF16),
        scratch_shapes=[pltpu.VMEM((L + SUBLANES, ct), F32)],
        compiler_params=_params(("parallel", "parallel")),
    )(proj, proj, conv_w8, conv_b)


def _conv_bwd(proj, dxbc, conv_w8, conv_b, n_seq_chunks, col0, width, ct, name):
    t = proj.shape[0]
    L = CHUNK
    nbc = t // L
    hb = L // SUBLANES
    ct = min(ct, width)
    assert col0 % ct == 0 and width % ct == 0
    cb0 = col0 // ct
    last_hb = t // SUBLANES - 1

    def body(x_ref, xb_ref, xa_ref, d_ref, da_ref, w_ref, b_ref, o_ref, dw_ref, db_ref, ext_ref, dc_ref):
        bc = pl.program_id(1)
        first = (bc % n_seq_chunks) == 0
        last = (bc % n_seq_chunks) == n_seq_chunks - 1

        @pl.when(bc == 0)
        def _():
            dw_ref[...] = jnp.zeros_like(dw_ref)
            db_ref[...] = jnp.zeros_like(db_ref)

        ext_ref[0:SUBLANES, :] = jnp.where(first, 0.0, xb_ref[...].astype(F32))
        ext_ref[SUBLANES:SUBLANES + L, :] = x_ref[...].astype(F32)
        ext_ref[SUBLANES + L:, :] = xa_ref[...].astype(F32)
        le = L + SUBLANES
        acc = jnp.zeros((le, ct), F32) + b_ref[...]
        for k in range(CONV_K):
            acc = acc + w_ref[k:k + 1, :] * ext_ref[pl.ds(SUBLANES - (CONV_K - 1) + k, le), :]
        sg = _sigmoid(acc)
        dsilu = sg * (1.0 + acc * (1.0 - sg))
        dc_ref[0:L, :] = d_ref[...].astype(F32) * dsilu[0:L]
        dc_ref[L:, :] = jnp.where(last, 0.0, da_ref[...].astype(F32)) * dsilu[L:]
        dc = dc_ref[0:L, :]
        dx = jnp.zeros((L, ct), F32)
        for k in range(CONV_K):
            dx = dx + w_ref[k:k + 1, :] * dc_ref[pl.ds(CONV_K - 1 - k, L), :]
            dw_ref[k:k + 1, :] += _colsum(dc * ext_ref[pl.ds(SUBLANES - (CONV_K - 1) + k, L), :])
        db_ref[0:1, :] += _colsum(dc)
        o_ref[...] = dx.astype(o_ref.dtype)

    return pl.pallas_call(
        body, name=name,
        grid=(width // ct, nbc),
        in_specs=[
            pl.BlockSpec((L, ct), lambda j, i: (i, cb0 + j)),
            pl.BlockSpec((SUBLANES, ct), lambda j, i: (jnp.maximum(i * hb - 1, 0), cb0 + j)),
            pl.BlockSpec((SUBLANES, ct), lambda j, i: (jnp.minimum((i + 1) * hb, last_hb), cb0 + j)),
            pl.BlockSpec((L, ct), lambda j, i: (i, j)),
            pl.BlockSpec((SUBLANES, ct), lambda j, i: (jnp.minimum((i + 1) * hb, last_hb), j)),
            pl.BlockSpec((SUBLANES, ct), lambda j, i: (0, cb0 + j)),
            pl.BlockSpec((1, ct), lambda j, i: (0, cb0 + j)),
        ],
        out_specs=[
            pl.BlockSpec((L, ct), lambda j, i: (i, j)),
            pl.BlockSpec((SUBLANES, ct), lambda j, i: (0, j)),
            pl.BlockSpec((SUBLANES, ct), lambda j, i: (0, j)),
        ],
        out_shape=[
            jax.ShapeDtypeStruct((t, width), BF16),
            jax.ShapeDtypeStruct((SUBLANES, width), F32),
            jax.ShapeDtypeStruct((SUBLANES, width), F32),
        ],
        scratch_shapes=[pltpu.VMEM((L + 2 * SUBLANES, ct), F32), pltpu.VMEM((L + SUBLANES, ct), F32)],
        compiler_params=_params(("parallel", "arbitrary")),
    )(proj, proj, proj, dxbc, dxbc, conv_w8, conv_b)


def _cumsum_rows(x, reverse=False):
    n = x.shape[0]
    row = lax.broadcasted_iota(jnp.int32, x.shape, 0)
    s = 1
    while s < n:
        if reverse:
            x = x + jnp.where(row < n - s, pltpu.roll(x, n - s, 0), 0.0)
        else:
            x = x + jnp.where(row >= s, pltpu.roll(x, s, 0), 0.0)
        s *= 2
    return x


def _ssd_scalars(dtr, dtb, alog):
    pre = dtr + dtb
    dt = jnp.maximum(pre, 0.0) + jnp.log(1.0 + jnp.exp(-jnp.abs(pre)))
    a = -jnp.exp(alog)
    acs = _cumsum_rows(dt * a)
    return pre, dt, a, acs


def _ssd_group_common(xs, dt_s, acs_s, e_ref, gw):
    L = xs.shape[0]
    e = e_ref[...]
    dt_x = _dot_exact01(dt_s, e)
    acs_x = _dot_exact01(acs_s, e)
    e_x = jnp.exp(acs_x)
    a_last = acs_x[L - 1:L, :]
    dec_x = jnp.exp(a_last - acs_x)
    return dt_x, acs_x, e_x, dec_x


def _decay_matrix(acs_x, acs_t, r, tri):
    col = acs_x[:, r * HEAD_DIM:r * HEAD_DIM + 1]
    rowv = acs_t[r * HEAD_DIM:r * HEAD_DIM + 1, :]
    return jnp.exp(jnp.where(tri, col - rowv, NEG_BIG))


def _ssd_fwd(xbc, proj, dt_raw, dtb, alog, dskip_x, normw, emat, bl, inner, z_col0):
    t = xbc.shape[0]
    L = CHUNK
    nc = t // bl // L
    G = GROUPS
    gw = inner // G
    hpg = gw // HEAD_DIM
    assert z_col0 % gw == 0
    zb0 = z_col0 // gw
    bb0 = inner // STATE
    cb0 = bb0 + G

    def body(xs_ref, b_ref, c_ref, z_ref, dtr_ref, dtb_ref, alog_ref, dsk_ref, nw_ref, e_ref,
             y_ref, yn_ref, st_ref, h_ref, dt_s, acs_s):
        c = pl.program_id(1)
        g = pl.program_id(2)

        @pl.when(g == 0)
        def _():
            _, dt, _, acs = _ssd_scalars(dtr_ref[...], dtb_ref[...], alog_ref[...])
            dt_s[...] = dt
            acs_s[...] = acs

        @pl.when(c == 0)
        def _():
            h_ref[g] = jnp.zeros((STATE, gw), F32)

        xs = xs_ref[...].astype(F32)
        bg = b_ref[...]
        cg = c_ref[...]
        dt_x, acs_x, e_x, dec_x = _ssd_group_common(xs, dt_s[...], acs_s[...], e_ref, gw)
        xdt = xs * dt_x
        cb = _dot(cg, bg, NT)
        acs_t = acs_x.T
        h = h_ref[g]
        st_ref[0, 0] = h
        tri = lax.broadcasted_iota(jnp.int32, (L, L), 0) >= lax.broadcasted_iota(jnp.int32, (L, L), 1)
        lane = lax.broadcasted_iota(jnp.int32, (L, gw), 1)
        y = _dot(cg, h) * e_x + dsk_ref[...] * xs
        for r in range(hpg):
            lm = _decay_matrix(acs_x, acs_t, r, tri)
            m = cb * lm
            xr = jnp.where((lane >= r * HEAD_DIM) & (lane < (r + 1) * HEAD_DIM), xdt, 0.0)
            y = y + _dot(m, xr)
        h_ref[g] = h * e_x[L - 1:L, :] + _dot(bg, xdt * dec_x, TN)
        y_ref[...] = y.astype(y_ref.dtype)
        yq = y_ref[...].astype(F32)
        z = z_ref[...].astype(F32)
        yg = yq * (z * _sigmoid(z))
        rs = lax.rsqrt(jnp.mean(yg * yg, axis=-1, keepdims=True) + RMS_EPS)
        yn_ref[...] = (yg * rs * nw_ref[...]).astype(yn_ref.dtype)

    return pl.pallas_call(
        body, name="ssd_fwd",
        grid=(bl, nc, G),
        in_specs=[
            pl.BlockSpec((L, gw), lambda b, c, g: (b * nc + c, g)),
            pl.BlockSpec((L, STATE), lambda b, c, g: (b * nc + c, bb0 + g)),
            pl.BlockSpec((L, STATE), lambda b, c, g: (b * nc + c, cb0 + g)),
            pl.BlockSpec((L, gw), lambda b, c, g: (b * nc + c, zb0 + g)),
            pl.BlockSpec((L, LANES), lambda b, c, g: (b * nc + c, 0)),
            pl.BlockSpec((1, LANES), lambda b, c, g: (0, 0)),
            pl.BlockSpec((1, LANES), lambda b, c, g: (0, 0)),
            pl.BlockSpec((1, gw), lambda b, c, g: (0, g)),
            pl.BlockSpec((1, gw), lambda b, c, g: (0, g)),
            pl.BlockSpec((LANES, gw), lambda b, c, g: (0, g)),
        ],
        out_specs=[
            pl.BlockSpec((L, gw), lambda b, c, g: (b * nc + c, g)),
            pl.BlockSpec((L, gw), lambda b, c, g: (b * nc + c, g)),
            pl.BlockSpec((1, 1, STATE, gw), lambda b, c, g: (b * nc + c, g, 0, 0)),
        ],
        out_shape=[
            jax.ShapeDtypeStruct((t, inner), BF16),
            jax.ShapeDtypeStruct((t, inner), BF16),
            jax.ShapeDtypeStruct((bl * nc, G, STATE, gw), F32),
        ],
        scratch_shapes=[pltpu.VMEM((G, STATE, gw), F32), pltpu.VMEM((L, LANES), F32), pltpu.VMEM((L, LANES), F32)],
        compiler_params=_params(("arbitrary", "arbitrary", "arbitrary")),
    )(xbc, xbc, xbc, proj, dt_raw, dtb, alog, dskip_x, normw, emat)


def _ssd_bwd(xbc, proj, dt_raw, y, dyn, states, dtb, alog, dskip_x, normw, emat, emat_t, bl, inner, z_col0):
    t = xbc.shape[0]
    L = CHUNK
    nc = t // bl // L
    G = GROUPS
    gw = inner // G
    hpg = gw // HEAD_DIM
    zb0 = z_col0 // gw
    bb0 = inner // STATE
    cb0 = bb0 + G

    def rc(j):
        return nc - 1 - j

    def body(xs_ref, b_ref, c_ref, z_ref, dtr_ref, y_ref, dyn_ref, st_ref, dtb_ref, alog_ref, dsk_ref,
             nw_ref, e_ref, et_ref,
             dxs_ref, db_ref, dc_ref, dz_ref, ddt_ref, dnw_ref, dsk_acc, dalog_acc, ddtb_acc,
             dh_ref, pre_s, dt_s, acs_s, wacs_s, wdt_s):
        b = pl.program_id(0)
        j = pl.program_id(1)
        g = pl.program_id(2)

        @pl.when((b == 0) & (j == 0) & (g == 0))
        def _():
            dsk_acc[...] = jnp.zeros_like(dsk_acc)
            dalog_acc[...] = jnp.zeros_like(dalog_acc)
            ddtb_acc[...] = jnp.zeros_like(ddtb_acc)

        @pl.when((b == 0) & (j == 0))
        def _():
            dnw_ref[g] = jnp.zeros((SUBLANES, gw), F32)

        @pl.when(g == 0)
        def _():
            pre, dt, _, acs = _ssd_scalars(dtr_ref[...], dtb_ref[...], alog_ref[...])
            pre_s[...] = pre
            dt_s[...] = dt
            acs_s[...] = acs
            wacs_s[...] = jnp.zeros_like(wacs_s)
            wdt_s[...] = jnp.zeros_like(wdt_s)

        @pl.when(j == 0)
        def _():
            dh_ref[g] = jnp.zeros((STATE, gw), F32)

        xs = xs_ref[...].astype(F32)
        bg = b_ref[...]
        cg = c_ref[...]
        dt_x, acs_x, e_x, dec_x = _ssd_group_common(xs, dt_s[...], acs_s[...], e_ref, gw)
        xdt = xs * dt_x
        xdt_b = xdt.astype(BF16)
        cb = _dot(cg, bg, NT)
        acs_t = acs_x.T
        h = st_ref[0, 0]
        hb16 = h.astype(BF16)
        dsk = dsk_ref[...]

        yv = y_ref[...].astype(F32)
        z = z_ref[...].astype(F32)
        sgz = _sigmoid(z)
        sz = z * sgz
        yg = yv * sz
        rs = lax.rsqrt(jnp.mean(yg * yg, axis=-1, keepdims=True) + RMS_EPS)
        yhat = yg * rs
        dyn_v = dyn_ref[...].astype(F32)
        dnw_ref[g] += _colsum(dyn_v * yhat)
        dyh = dyn_v * nw_ref[...]
        dyg = rs * (dyh - yhat * jnp.mean(dyh * yhat, axis=-1, keepdims=True))
        dy = dyg * sz
        dz_ref[...] = (dyg * yv * (sgz * (1.0 + z * (1.0 - sgz)))).astype(dz_ref.dtype)

        tri = lax.broadcasted_iota(jnp.int32, (L, L), 0) >= lax.broadcasted_iota(jnp.int32, (L, L), 1)
        lane = lax.broadcasted_iota(jnp.int32, (L, gw), 1)
        dy_b = dy.astype(BF16)
        dcb = jnp.zeros((L, L), F32)
        dxdt_d = jnp.zeros((L, gw), F32)
        ydiag = jnp.zeros((L, gw), F32)
        for r in range(hpg):
            lm = _decay_matrix(acs_x, acs_t, r, tri)
            m = (cb * lm).astype(BF16)
            sel = (lane >= r * HEAD_DIM) & (lane < (r + 1) * HEAD_DIM)
            dyr = jnp.where(sel, dy_b, jnp.zeros_like(dy_b))
            xr = jnp.where(sel, xdt_b, jnp.zeros_like(xdt_b))
            ydiag = ydiag + _dot(m, xr)
            dcb = dcb + _dot(dyr, xdt_b, NT) * lm
            dxdt_d = dxdt_d + _dot(m, dyr, TN)
        dh = dh_ref[g]
        dh16 = dh.astype(BF16)
        xdec_b = (xdt * dec_x).astype(BF16)
        bdh = _dot(bg, dh16)
        dxdt = dxdt_d + dec_x * bdh
        dcb16 = dcb.astype(BF16)
        dye = (dy * e_x).astype(BF16)
        db_ref[...] = (_dot(dcb16, cg, TN) + _dot(xdec_b, dh16, NT)).astype(db_ref.dtype)
        dc_ref[...] = (_dot(dcb16, bg) + _dot(dye, hb16, NT)).astype(dc_ref.dtype)
        dprev = _dot(cg, dye, TN)
        cd_row = e_x[L - 1:L, :]
        s_new = _dot(bg, xdec_b, TN)
        last_term = _colsum(dh16.astype(F32) * s_new) + _colsum(dh * h) * cd_row
        rowi = lax.broadcasted_iota(jnp.int32, (L, gw), 0)
        yoff = _dot(cg, hb16) * e_x
        wfold = (dy_b.astype(F32) * ydiag + dy * yoff - dxdt_d * xdt_b.astype(F32) - bdh * xdec_b.astype(F32)
                 + jnp.where(rowi == L - 1, last_term, 0.0))
        et = et_ref[...]
        wacs_s[...] += _dot_exact01(wfold, et)
        wdt_s[...] += _dot_exact01(dxdt * xs, et)
        dsk_acc[...] += _dot_exact01(jnp.broadcast_to(_colsum(dy * xs), (SUBLANES, gw)), et)
        dxs_ref[...] = (dsk * dy + dxdt * dt_x).astype(dxs_ref.dtype)
        dh_ref[g] = dprev + cd_row * dh

        @pl.when(g == G - 1)
        def _():
            a = -jnp.exp(alog_ref[...])
            dda = _cumsum_rows(wacs_s[...], reverse=True)
            ddt = wdt_s[...] + dda * a
            ddt_raw = ddt * _sigmoid(pre_s[...])
            ddt_ref[...] = ddt_raw
            dalog_acc[...] += _colsum(dda * dt_s[...]) * a
            ddtb_acc[...] += _colsum(ddt_raw)

    def cidx(b, j):
        return b * nc + rc(j)

    accs = lambda shape: pl.BlockSpec(shape, lambda b, j, g: tuple(0 for _ in shape))
    return pl.pallas_call(
        body, name="ssd_bwd",
        grid=(bl, nc, G),
        in_specs=[
            pl.BlockSpec((L, gw), lambda b, j, g: (cidx(b, j), g)),
            pl.BlockSpec((L, STATE), lambda b, j, g: (cidx(b, j), bb0 + g)),
            pl.BlockSpec((L, STATE), lambda b, j, g: (cidx(b, j), cb0 + g)),
            pl.BlockSpec((L, gw), lambda b, j, g: (cidx(b, j), zb0 + g)),
            pl.BlockSpec((L, LANES), lambda b, j, g: (cidx(b, j), 0)),
            pl.BlockSpec((L, gw), lambda b, j, g: (cidx(b, j), g)),
            pl.BlockSpec((L, gw), lambda b, j, g: (cidx(b, j), g)),
            pl.BlockSpec((1, 1, STATE, gw), lambda b, j, g: (cidx(b, j), g, 0, 0)),
            pl.BlockSpec((1, LANES), lambda b, j, g: (0, 0)),
            pl.BlockSpec((1, LANES), lambda b, j, g: (0, 0)),
            pl.BlockSpec((1, gw), lambda b, j, g: (0, g)),
            pl.BlockSpec((1, gw), lambda b, j, g: (0, g)),
            pl.BlockSpec((LANES, gw), lambda b, j, g: (0, g)),
            pl.BlockSpec((gw, LANES), lambda b, j, g: (g, 0)),
        ],
        out_specs=[
            pl.BlockSpec((L, gw), lambda b, j, g: (cidx(b, j), g)),
            pl.BlockSpec((L, STATE), lambda b, j, g: (cidx(b, j), g)),
            pl.BlockSpec((L, STATE), lambda b, j, g: (cidx(b, j), g)),
            pl.BlockSpec((L, gw), lambda b, j, g: (cidx(b, j), g)),
            pl.BlockSpec((L, LANES), lambda b, j, g: (cidx(b, j), 0)),
            accs((G, SUBLANES, gw)),
            accs((SUBLANES, LANES)),
            accs((SUBLANES, LANES)),
            accs((SUBLANES, LANES)),
        ],
        out_shape=[
            jax.ShapeDtypeStruct((t, inner), BF16),
            jax.ShapeDtypeStruct((t, G * STATE), BF16),
            jax.ShapeDtypeStruct((t, G * STATE), BF16),
            jax.ShapeDtypeStruct((t, inner), BF16),
            jax.ShapeDtypeStruct((t, LANES), F32),
            jax.ShapeDtypeStruct((G, SUBLANES, gw), F32),
            jax.ShapeDtypeStruct((SUBLANES, LANES), F32),
            jax.ShapeDtypeStruct((SUBLANES, LANES), F32),
            jax.ShapeDtypeStruct((SUBLANES, LANES), F32),
        ],
        scratch_shapes=[pltpu.VMEM((G, STATE, gw), F32)] + [pltpu.VMEM((L, LANES), F32)] * 5,
        compiler_params=_params(("arbitrary", "arbitrary", "arbitrary")),
    )(xbc, xbc, xbc, proj, dt_raw, y, dyn, states, dtb, alog, dskip_x, normw, emat, emat_t)


def _pool_window(u, w, anti):
    n = u.shape[0]
    row = lax.broadcasted_iota(jnp.int32, u.shape, 0)
    acc = u
    s = 1
    while s < w:
        if anti:
            acc = acc + jnp.where(row < n - s, pltpu.roll(acc, n - s, 0), 0.0)
        else:
            acc = acc + jnp.where(row >= s, pltpu.roll(acc, s, 0), 0.0)
        s *= 2
    return acc


def _pool_cnt(shape, w):
    row = lax.broadcasted_iota(jnp.int32, shape, 0)
    return jnp.minimum(row + 1, w).astype(F32)


def _pool_fwd(proj, wpg, bl, d, u_col0):
    t = proj.shape[0]
    s = t // bl
    pg = len(POOL_WINDOWS)
    cg = d // pg
    ub0 = u_col0 // d

    def body(u_ref, w_ref, o_ref):
        for gi, w in enumerate(POOL_WINDOWS):
            u = u_ref[:, gi * cg:(gi + 1) * cg].astype(F32)
            pooled = _pool_window(u, w, False) / _pool_cnt(u.shape, w) - u
            o_ref[:, gi * cg:(gi + 1) * cg] = _dot(pooled, w_ref[gi]).astype(o_ref.dtype)

    return pl.pallas_call(
        body, name="pool_fwd",
        grid=(bl,),
        in_specs=[pl.BlockSpec((s, d), lambda b: (b, ub0)), pl.BlockSpec((pg, cg, cg), lambda b: (0, 0, 0))],
        out_specs=pl.BlockSpec((s, d), lambda b: (b, 0)),
        out_shape=jax.ShapeDtypeStruct((t, d), BF16),
        compiler_params=_params(("parallel",)),
    )(proj, wpg)


def _pool_bwd(proj, dyp, wpg, bl, d, u_col0):
    t = proj.shape[0]
    s = t // bl
    pg = len(POOL_WINDOWS)
    cg = d // pg
    ub0 = u_col0 // d

    def body(u_ref, dy_ref, w_ref, du_ref, dw_ref):
        @pl.when(pl.program_id(0) == 0)
        def _():
            dw_ref[...] = jnp.zeros_like(dw_ref)

        for gi, w in enumerate(POOL_WINDOWS):
            u = u_ref[:, gi * cg:(gi + 1) * cg].astype(F32)
            cnt = _pool_cnt(u.shape, w)
            pooled = _pool_window(u, w, False) / cnt - u
            dy = dy_ref[:, gi * cg:(gi + 1) * cg]
            dw_ref[gi] += _dot(pooled, dy, TN)
            dp = _dot(dy, w_ref[gi], NT)
            du_ref[:, gi * cg:(gi + 1) * cg] = (_pool_window(dp / cnt, w, True) - dp).astype(du_ref.dtype)

    return pl.pallas_call(
        body, name="pool_bwd",
        grid=(bl,),
        in_specs=[pl.BlockSpec((s, d), lambda b: (b, ub0)), pl.BlockSpec((s, d), lambda b: (b, 0)),
                  pl.BlockSpec((pg, cg, cg), lambda b: (0, 0, 0))],
        out_specs=[pl.BlockSpec((s, d), lambda b: (b, 0)), pl.BlockSpec((pg, cg, cg), lambda b: (0, 0, 0))],
        out_shape=[jax.ShapeDtypeStruct((t, d), BF16), jax.ShapeDtypeStruct((pg, cg, cg), F32)],
        compiler_params=_params(("arbitrary",)),
    )(proj, dyp, wpg)


def _merge_fwd(proj, ypr, yssd, x, w_out, b_gates, pool_scale, d, lg_col0, tm):
    t = x.shape[0]
    lb0 = lg_col0 // (2 * d)

    def body(lg_ref, yp_ref, ys_ref, x_ref, w_ref, bg_ref, ps_ref, mg_ref, r1_ref):
        lg = lg_ref[...].astype(F32) + bg_ref[...]
        ga = _sigmoid(lg[:, :d])
        gb = _sigmoid(lg[:, d:])
        merged = ga * (yp_ref[...].astype(F32) * ps_ref[...]) + gb * ys_ref[...].astype(F32)
        mg_ref[...] = merged.astype(mg_ref.dtype)
        r1_ref[...] = ALPHA * x_ref[...] + _dot(mg_ref[...], w_ref[...])

    row = lambda w: pl.BlockSpec((tm, w), lambda i: (i, 0))
    full = lambda a: pl.BlockSpec(a.shape, lambda i: (0, 0))
    return pl.pallas_call(
        body, name="merge_fwd",
        grid=(t // tm,),
        in_specs=[pl.BlockSpec((tm, 2 * d), lambda i: (i, lb0)), row(d), row(d), row(d), full(w_out), full(b_gates),
                  full(pool_scale)],
        out_specs=[row(d), row(d)],
        out_shape=[jax.ShapeDtypeStruct((t, d), BF16), jax.ShapeDtypeStruct((t, d), F32)],
        compiler_params=_params(("parallel",)),
    )(proj, ypr, yssd, x, w_out, b_gates, pool_scale)


def _merge_bwd(dr1, proj, ypr, yssd, w_out, b_gates, pool_scale, d, lg_col0, tm):
    t = dr1.shape[0]
    lb0 = lg_col0 // (2 * d)

    def body(dr_ref, lg_ref, yp_ref, ys_ref, w_ref, bg_ref, ps_ref, dlg_ref, dyp_ref, dys_ref, dbg_ref, dps_ref):
        @pl.when(pl.program_id(0) == 0)
        def _():
            dbg_ref[...] = jnp.zeros_like(dbg_ref)
            dps_ref[...] = jnp.zeros_like(dps_ref)

        dm = _dot(dr_ref[...], w_ref[...], NT)
        lg = lg_ref[...].astype(F32) + bg_ref[...]
        ga = _sigmoid(lg[:, :d])
        gb = _sigmoid(lg[:, d:])
        ypr_v = yp_ref[...].astype(F32)
        ys_v = ys_ref[...].astype(F32)
        ps = ps_ref[...]
        dga = dm * ypr_v * ps
        dla = dga * ga * (1.0 - ga)
        dlb = dm * ys_v * gb * (1.0 - gb)
        dlg_ref[:, :d] = dla.astype(dlg_ref.dtype)
        dlg_ref[:, d:] = dlb.astype(dlg_ref.dtype)
        dyp_ref[...] = (dm * ga * ps).astype(dyp_ref.dtype)
        dys_ref[...] = (dm * gb).astype(dys_ref.dtype)
        dbg_ref[0:1, :d] += _colsum(dla)
        dbg_ref[0:1, d:] += _colsum(dlb)
        dps_ref[0:1, :] += _colsum(dm * ga * ypr_v)

    row = lambda w: pl.BlockSpec((tm, w), lambda i: (i, 0))
    full = lambda a: pl.BlockSpec(a.shape, lambda i: (0, 0))
    acc = lambda w: pl.BlockSpec((SUBLANES, w), lambda i: (0, 0))
    return pl.pallas_call(
        body, name="merge_bwd",
        grid=(t // tm,),
        in_specs=[row(d), pl.BlockSpec((tm, 2 * d), lambda i: (i, lb0)), row(d), row(d), full(w_out), full(b_gates),
                  full(pool_scale)],
        out_specs=[row(2 * d), row(d), row(d), acc(2 * d), acc(d)],
        out_shape=[jax.ShapeDtypeStruct((t, 2 * d), BF16), jax.ShapeDtypeStruct((t, d), BF16),
                   jax.ShapeDtypeStruct((t, d), BF16), jax.ShapeDtypeStruct((SUBLANES, 2 * d), F32),
                   jax.ShapeDtypeStruct((SUBLANES, d), F32)],
        compiler_params=_params(("arbitrary",)),
    )(dr1, proj, ypr, yssd, w_out, b_gates, pool_scale)


def _mlp_fwd(r1, target, w_up, w_down, ln1_g, ln1_b, ln2_g, ln2_b, tm):
    t, d = r1.shape
    nf, _, tf = w_up.shape
    ff = nf * tf

    def body(r1_ref, tg_ref, wu_ref, wd_ref, g1_ref, b1_ref, g2_ref, b2_ref,
             up_ref, h1_ref, dr2_ref, loss_ref, dg2_ref, db2_ref, h1f, acc):
        i = pl.program_id(0)
        f = pl.program_id(1)

        @pl.when((i == 0) & (f == 0))
        def _():
            loss_ref[...] = jnp.zeros_like(loss_ref)
            dg2_ref[...] = jnp.zeros_like(dg2_ref)
            db2_ref[...] = jnp.zeros_like(db2_ref)

        @pl.when(f == 0)
        def _():
            xhat, _ = _ln_fwd(r1_ref[...])
            h1 = xhat * g1_ref[...] + b1_ref[...]
            h1f[...] = h1
            h1_ref[...] = h1.astype(h1_ref.dtype)
            acc[...] = jnp.zeros_like(acc)

        up_ref[...] = _dot(h1_ref[...], wu_ref[0]).astype(up_ref.dtype)
        upq = jnp.maximum(up_ref[...].astype(F32), 0.0)
        acc[...] += _dot(upq * upq, wd_ref[...])

        @pl.when(f == nf - 1)
        def _():
            xhat, rstd = _ln_fwd(ALPHA * h1f[...] + acc[...])
            g2 = g2_ref[...]
            diff = xhat * g2 + b2_ref[...] - tg_ref[...]
            loss_ref[...] += 0.5 / d * jnp.sum(diff * diff)
            dh2 = diff * (1.0 / d)
            dg2_ref[0:1, :] += _colsum(dh2 * xhat)
            db2_ref[0:1, :] += _colsum(dh2)
            dr2_ref[...] = _ln_bwd(dh2, xhat, rstd, g2).astype(dr2_ref.dtype)

    row = pl.BlockSpec((tm, d), lambda i, f: (i, 0))
    vec = pl.BlockSpec((1, d), lambda i, f: (0, 0))
    acc8 = pl.BlockSpec((SUBLANES, d), lambda i, f: (0, 0))
    return pl.pallas_call(
        body, name="mlp_fwd",
        grid=(t // tm, nf),
        in_specs=[row, row, pl.BlockSpec((1, d, tf), lambda i, f: (f, 0, 0)), pl.BlockSpec((tf, d), lambda i, f: (f, 0)),
                  vec, vec, vec, vec],
        out_specs=[pl.BlockSpec((tm, tf), lambda i, f: (i, f)), row, row,
                   pl.BlockSpec((SUBLANES, LANES), lambda i, f: (0, 0)), acc8, acc8],
        out_shape=[jax.ShapeDtypeStruct((t, ff), BF16), jax.ShapeDtypeStruct((t, d), BF16),
                   jax.ShapeDtypeStruct((t, d), BF16), jax.ShapeDtypeStruct((SUBLANES, LANES), F32),
                   jax.ShapeDtypeStruct((SUBLANES, d), F32), jax.ShapeDtypeStruct((SUBLANES, d), F32)],
        scratch_shapes=[pltpu.VMEM((tm, d), F32), pltpu.VMEM((tm, d), F32)],
        compiler_params=_params(("arbitrary", "arbitrary")),
    )(r1, target, w_up, w_down, ln1_g, ln1_b, ln2_g, ln2_b)


def _mlp_bwd(dr2, up, r1, w_up, w_down, ln1_g, tm):
    t, d = r1.shape
    nf, _, tf = w_up.shape
    ff = nf * tf

    def body(dr2_ref, up_ref, r1_ref, wu_ref, wd_ref, g1_ref, dup_ref, dr1_ref, dg1_ref, db1_ref, acc):
        i = pl.program_id(0)
        f = pl.program_id(1)

        @pl.when((i == 0) & (f == 0))
        def _():
            dg1_ref[...] = jnp.zeros_like(dg1_ref)
            db1_ref[...] = jnp.zeros_like(db1_ref)

        @pl.when(f == 0)
        def _():
            acc[...] = jnp.zeros_like(acc)

        dact = _dot(dr2_ref[...], wd_ref[...], NT)
        dup_ref[...] = (dact * 2.0 * jnp.maximum(up_ref[...].astype(F32), 0.0)).astype(dup_ref.dtype)
        acc[...] += _dot(dup_ref[...], wu_ref[0], NT)

        @pl.when(f == nf - 1)
        def _():
            dh1 = acc[...] + ALPHA * dr2_ref[...].astype(F32)
            xhat, rstd = _ln_fwd(r1_ref[...])
            dg1_ref[0:1, :] += _colsum(dh1 * xhat)
            db1_ref[0:1, :] += _colsum(dh1)
            dr1_ref[...] = _ln_bwd(dh1, xhat, rstd, g1_ref[...]).astype(dr1_ref.dtype)

    row = pl.BlockSpec((tm, d), lambda i, f: (i, 0))
    acc8 = pl.BlockSpec((SUBLANES, d), lambda i, f: (0, 0))
    return pl.pallas_call(
        body, name="mlp_bwd",
        grid=(t // tm, nf),
        in_specs=[row, pl.BlockSpec((tm, tf), lambda i, f: (i, f)), row,
                  pl.BlockSpec((1, d, tf), lambda i, f: (f, 0, 0)), pl.BlockSpec((tf, d), lambda i, f: (f, 0)),
                  pl.BlockSpec((1, d), lambda i, f: (0, 0))],
        out_specs=[pl.BlockSpec((tm, tf), lambda i, f: (i, f)), row, acc8, acc8],
        out_shape=[jax.ShapeDtypeStruct((t, ff), BF16), jax.ShapeDtypeStruct((t, d), BF16),
                   jax.ShapeDtypeStruct((SUBLANES, d), F32), jax.ShapeDtypeStruct((SUBLANES, d), F32)],
        scratch_shapes=[pltpu.VMEM((tm, d), F32)],
        compiler_params=_params(("arbitrary", "arbitrary")),
    )(dr2, up, r1, w_up, w_down, ln1_g)


def _dx_kernel(segs, w_main, ddt, w_dt, dr1, tm, tk):
    t, d = dr1.shape
    nblk = [s.shape[1] // tk for s in segs]
    starts = [sum(nblk[:i]) for i in range(len(segs))]
    nk = sum(nblk)
    nseg = len(segs)

    def body(*refs):
        seg_refs = refs[:nseg]
        w_ref, ddt_ref, wdt_ref, dr1_ref, o_ref, acc = refs[nseg:]
        k = pl.program_id(1)

        @pl.when(k == 0)
        def _():
            acc[...] = ALPHA * dr1_ref[...].astype(F32) + _dot(ddt_ref[...], wdt_ref[...], NT)

        for si in range(nseg):
            @pl.when((k >= starts[si]) & (k < starts[si] + nblk[si]))
            def _(si=si):
                acc[...] += _dot(seg_refs[si][...], w_ref[...], NT)

        @pl.when(k == nk - 1)
        def _():
            o_ref[...] = acc[...]

    def seg_spec(si):
        return pl.BlockSpec((tm, tk), lambda i, k: (i, jnp.clip(k - starts[si], 0, nblk[si] - 1)))

    row = pl.BlockSpec((tm, d), lambda i, k: (i, 0))
    return pl.pallas_call(
        body, name="dx",
        grid=(t // tm, nk),
        in_specs=[seg_spec(si) for si in range(nseg)] + [
            pl.BlockSpec((d, tk), lambda i, k: (0, k)), pl.BlockSpec((tm, LANES), lambda i, k: (i, 0)),
            pl.BlockSpec((d, LANES), lambda i, k: (0, 0)), row],
        out_specs=row,
        out_shape=jax.ShapeDtypeStruct((t, d), F32),
        scratch_shapes=[pltpu.VMEM((tm, d), F32)],
        compiler_params=_params(("parallel", "arbitrary")),
    )(*segs, w_main, ddt, w_dt, dr1)


def _dims(d):
    inner = 2 * d
    heads = inner // HEAD_DIM
    cd = inner + 2 * GROUPS * STATE
    assert heads <= LANES and inner % (GROUPS * LANES) == 0 and d % (len(POOL_WINDOWS) * LANES) == 0
    o_z, o_xbc, o_dt, o_lg = d, d + inner, d + inner + cd, d + inner + cd + heads
    return inner, heads, cd, (o_z, o_xbc, o_dt, o_lg)


def _row(v, width=None):
    v = v.reshape(1, -1).astype(F32)
    if width is not None and v.shape[1] < width:
        v = jnp.pad(v, ((0, 0), (0, width - v.shape[1])))
    return v


def _local_step(x2, tgt2, w, bl):
    t, d = x2.shape
    inner, heads, cd, _ = _dims(d)
    gs = GROUPS * STATE
    nc = t // bl // CHUNK
    w_main, w_dt = _w_in_internal(w["w_in_blocks"], d)
    c_z, c_lg, c_u = cd, cd + inner, cd + inner + 2 * d
    conv_w8 = jnp.pad(w["conv_w"].astype(F32), ((0, SUBLANES - CONV_K), (0, 0)))
    conv_b = _row(w["conv_b"])
    dtb, alog = _row(w["dt_bias"], LANES), _row(w["a_log"], LANES)
    dskip_x = _row(jnp.repeat(w["d_skip"].reshape(-1), HEAD_DIM))
    normw = _row(w["ssd_norm_w"])
    col_head = lax.broadcasted_iota(jnp.int32, (LANES, inner), 1) // HEAD_DIM
    emat = (col_head == lax.broadcasted_iota(jnp.int32, (LANES, inner), 0)).astype(BF16)
    emat_t = emat.T
    w_ssd, wpg, w_out = w["w_ssd_proj"].astype(BF16), w["w_pool_group"].astype(BF16), w["w_out"].astype(BF16)
    w_up, w_down = w["w_up_blocks"].astype(BF16), w["w_down"].astype(BF16)
    w_main, w_dt = w_main.astype(BF16), w_dt.astype(BF16)
    b_gates, pool_scale = _row(w["b_gates"]), _row(w["pool_scale"])
    ln1_g, ln1_b, ln2_g, ln2_b = _row(w["ln1_g"]), _row(w["ln1_b"]), _row(w["ln2_g"]), _row(w["ln2_b"])

    tm = min(512, t)
    tk = min(1024, d)
    ct = min(1024, d)
    mm = functools.partial(_matmul, bm=1024, bn=tk, bk=1024)
    xb = x2.astype(BF16)

    proj = mm(xb, w_main, "nn", BF16, name="in_proj")
    dt_raw = mm(xb, w_dt, "nn", F32, name="in_proj_dt")
    xbc = _conv_fwd(proj, conv_w8, conv_b, nc, cd, ct)
    y, yn, states = _ssd_fwd(xbc, proj, dt_raw, dtb, alog, dskip_x, normw, emat, bl, inner, c_z)
    yssd = mm(yn, w_ssd, "nn", BF16, name="ssd_proj")
    ypr = _pool_fwd(proj, wpg, bl, d, c_u)
    merged, r1 = _merge_fwd(proj, ypr, yssd, x2, w_out, b_gates, pool_scale, d, c_lg, tm)
    up, h1, dr2, loss8, dg2, db2 = _mlp_fwd(r1, tgt2, w_up, w_down, ln1_g, ln1_b, ln2_g, ln2_b, tm)

    dup, dr1, dg1, db1 = _mlp_bwd(dr2, up, r1, w_up, w_down, ln1_g, tm)
    relu2 = lambda v: jnp.square(jnp.maximum(v, 0.0))
    g = {}
    g["w_down"] = mm(up, dr2, "tn", BF16, name="dw_down", a_fn=relu2)
    g["w_up"] = _matmul(h1, dup, "tn", BF16, bm=1024, bn=w_up.shape[2], bk=1024, name="dw_up", col_blocks=N_DEV)
    g["w_out"] = mm(merged, dr1, "tn", BF16, name="dw_out")
    dlg, dyp, dys, dbg, dps = _merge_bwd(dr1, proj, ypr, yssd, w_out, b_gates, pool_scale, d, c_lg, tm)
    du, dwpg = _pool_bwd(proj, dyp, wpg, bl, d, c_u)
    npg, cg = dwpg.shape[0], dwpg.shape[1]
    g["w_pool_group"] = dwpg.reshape(npg, N_DEV, cg // N_DEV, cg).transpose(1, 0, 2, 3).reshape(
        N_DEV, npg * cg // N_DEV, cg).astype(BF16)
    dyn = mm(dys, w_ssd, "nt", BF16, name="d_ssd_proj")
    g["w_ssd_proj"] = mm(yn, dys, "tn", BF16, name="dw_ssd_proj")
    dxs, dbm, dcm, dz, ddt, dnw, dsk, dalog, ddtb = _ssd_bwd(
        xbc, proj, dt_raw, y, dyn, states, dtb, alog, dskip_x, normw, emat, emat_t, bl, inner, c_z)
    dxs_p, dcw_x, dcb_x = _conv_bwd(proj, dxs, conv_w8, conv_b, nc, 0, inner, ct, "conv_bwd_x")
    dbm_p, dcw_b, dcb_b = _conv_bwd(proj, dbm, conv_w8, conv_b, nc, inner, gs, ct, "conv_bwd_b")
    dcm_p, dcw_c, dcb_c = _conv_bwd(proj, dcm, conv_w8, conv_b, nc, inner + gs, gs, ct, "conv_bwd_c")
    segs = [dxs_p, dbm_p, dcm_p, dz, dlg, du]
    grad_x = _dx_kernel(segs, w_main, ddt, w_dt, dr1, tm, tk)
    keys = [k for k, _, _ in _col_segments(d)]
    dws = {k: mm(xb, s, "tn", BF16, name="dw_in_" + k) for k, s in zip(keys, segs + [ddt])}
    g["w_in"] = _w_in_grad_blocks(dws, d, w["w_in_blocks"].shape[2])
    g["conv_w"] = jnp.concatenate([dcw_x, dcw_b, dcw_c], axis=1)[:CONV_K]
    g["conv_b"] = jnp.concatenate([dcb_x, dcb_b, dcb_c], axis=1)[0]
    g["b_gates"], g["pool_scale"] = dbg[0], dps[0]
    g["dt_bias"], g["a_log"], g["d_skip"] = ddtb[0, :heads], dalog[0, :heads], dsk[0, :heads]
    g["ssd_norm_w"] = dnw[:, 0, :].reshape(inner)
    g["ln1_g"], g["ln1_b"], g["ln2_g"], g["ln2_b"] = dg1[0], db1[0], dg2[0], db2[0]
    return loss8, grad_x, g


BIG = ("w_in", "w_ssd_proj", "w_pool_group", "w_out", "w_up", "w_down")
SMALL = ("b_gates", "conv_b", "dt_bias", "a_log", "d_skip", "ssd_norm_w", "pool_scale", "ln1_g", "ln1_b", "ln2_g",
         "ln2_b")
SMALL_PACK = SMALL + ("conv_w",)
NAMES = ("w_in", "b_gates", "conv_w", "conv_b", "dt_bias", "a_log", "d_skip", "ssd_norm_w", "w_ssd_proj",
         "w_pool_group", "pool_scale", "w_out", "ln1_g", "ln1_b", "w_up", "w_down", "ln2_g", "ln2_b")


def _size(shape):
    n = 1
    for s in shape:
        n *= s
    return n


def _rows128(v):
    v = v.astype(F32).reshape((-1, v.shape[-1]))
    n = v.shape[-1]
    v = jnp.pad(v, ((0, 0), (0, -n % LANES)))
    return v.reshape(-1, LANES)


def _pack_small(vals, extra):
    parts = [_rows128(vals[n]) for n in SMALL_PACK]
    parts.append(jnp.pad(extra.reshape(1, 1).astype(F32), ((0, 0), (0, LANES - 1))))
    buf = jnp.concatenate(parts, axis=0)
    return jnp.pad(buf, ((0, -buf.shape[0] % SUBLANES), (0, 0)))


def _unpack_small(buf, shapes):
    out, off = {}, 0
    for n in SMALL_PACK:
        lead, last = _size(shapes[n][:-1]), shapes[n][-1]
        per = -(-last // LANES)
        out[n] = buf[off:off + lead * per].reshape(lead, per * LANES)[:, :last].reshape(shapes[n])
        off += lead * per
    return out, buf[off, 0]


def _col_segments(d):
    inner, heads, cd, (o_z, o_xbc, o_dt, o_lg) = _dims(d)
    gs = GROUPS * STATE
    return [("xs", o_xbc, inner), ("B", o_xbc + inner, gs), ("C", o_xbc + inner + gs, gs), ("z", o_z, inner),
            ("lg", o_lg, 2 * d), ("u", 0, d), ("dt", o_dt, heads)]


def _cols_from_blocks(blocks, start, width, bw):
    parts, pos = [], start
    while pos < start + width:
        k, off = divmod(pos, bw)
        n = min(bw - off, start + width - pos)
        parts.append(blocks[k][:, off:off + n])
        pos += n
    return parts


def _w_in_internal(blocks, d):
    bw = blocks.shape[2]
    segs = _col_segments(d)
    heads = segs[-1][2]
    main = [p for _, s, w_ in segs[:-1] for p in _cols_from_blocks(blocks, s, w_, bw)]
    w_dt = jnp.concatenate(_cols_from_blocks(blocks, segs[-1][1], heads, bw), axis=1)
    return jnp.concatenate(main, axis=1), jnp.pad(w_dt, ((0, 0), (0, LANES - heads)))


def _w_in_grad_blocks(dws, d, bw):
    order = sorted(_col_segments(d), key=lambda s: s[1])
    blocks = []
    for k in range(N_DEV):
        lo, hi, parts = k * bw, (k + 1) * bw, []
        for key, s, w_ in order:
            a, b = max(lo, s), min(hi, s + w_)
            if a < b:
                parts.append(dws[key][:, a - s:b - s])
        blocks.append(jnp.concatenate(parts, axis=1))
    return jnp.stack(blocks)


def _mesh_pos():
    return lax.axis_index("x"), lax.axis_index("y"), lax.axis_index("c")


ANY = pl.BlockSpec(memory_space=pl.ANY)


def _all_gather(shards):
    nw = len(shards)

    def body(*refs):
        x_refs, out_refs = refs[:nw], refs[nw:2 * nw]
        send_sems, recv_sems, local_sems = refs[2 * nw:]
        x, y, c = _mesh_pos()
        me, sibling = (x, y, c), (x, y, 1 - c)
        chips = [(1 - x, y), (x, 1 - y), (1 - x, 1 - y)]

        def copy(wi, k, block, to, from_input=False):
            px, py, pc = block
            blk = out_refs[wi].at[4 * px + 2 * py + pc]
            return pltpu.make_async_remote_copy(
                src_ref=x_refs[wi] if from_input else blk, dst_ref=blk,
                send_sem=send_sems.at[7 * wi + k], recv_sem=recv_sems.at[7 * wi + k], device_id=to,
                device_id_type=MESH)

        mine = [pltpu.make_async_copy(x_refs[wi], out_refs[wi].at[4 * x + 2 * y + c], local_sems.at[wi])
                for wi in range(nw)]
        sends = []
        for wi in range(nw):
            mine[wi].start()
            sends.append(copy(wi, 0, me, sibling, True))
            sends += [copy(wi, 1 + j, me, (*chip, c), True) for j, chip in enumerate(chips)]
        for cp in sends:
            cp.start()
        passed = []
        for wi in range(nw):
            for j, chip in enumerate(chips):
                copy(wi, 1 + j, (*chip, c), me).wait_recv()
                passed.append(copy(wi, 4 + j, (*chip, c), sibling))
                passed[-1].start()
        for wi in range(nw):
            copy(wi, 0, sibling, me).wait_recv()
            for j, chip in enumerate(chips):
                copy(wi, 4 + j, (*chip, 1 - c), me).wait_recv()
        for cp in sends + passed:
            cp.wait_send()
        for cp in mine:
            cp.wait()

    return pl.pallas_call(
        body, name="all_gather_weights",
        in_specs=[ANY] * nw, out_specs=[ANY] * nw,
        out_shape=[jax.ShapeDtypeStruct((N_DEV,) + s.shape, s.dtype) for s in shards],
        scratch_shapes=[pltpu.SemaphoreType.DMA((7 * nw,)), pltpu.SemaphoreType.DMA((7 * nw,)),
                        pltpu.SemaphoreType.DMA((nw,))],
    )(*shards)


def _rs_sibling(parts):
    nw = len(parts)
    half = N_DEV // 2

    def body(*refs):
        p_refs, keep_refs, recv_refs = refs[:nw], refs[nw:2 * nw], refs[2 * nw:3 * nw]
        send_sems, recv_sems, local_sems = refs[3 * nw:]
        x, y, c = _mesh_pos()
        cps, loc = [], []
        for wi in range(nw):
            for q in range(half):
                cps.append(pltpu.make_async_remote_copy(
                    src_ref=p_refs[wi].at[2 * q + 1 - c], dst_ref=recv_refs[wi].at[q],
                    send_sem=send_sems.at[half * wi + q], recv_sem=recv_sems.at[half * wi + q],
                    device_id=(x, y, 1 - c), device_id_type=MESH))
                loc.append(pltpu.make_async_copy(p_refs[wi].at[2 * q + c], keep_refs[wi].at[q],
                                                 local_sems.at[half * wi + q]))
        for cp in cps + loc:
            cp.start()
        for cp in cps + loc:
            cp.wait()

    shapes = [jax.ShapeDtypeStruct((half,) + p.shape[1:], p.dtype) for p in parts]
    outs = pl.pallas_call(
        body, name="rs_sibling",
        in_specs=[ANY] * nw, out_specs=[ANY] * (2 * nw), out_shape=shapes + shapes,
        scratch_shapes=[pltpu.SemaphoreType.DMA((half * nw,)), pltpu.SemaphoreType.DMA((half * nw,)),
                        pltpu.SemaphoreType.DMA((half * nw,))],
    )(*parts)
    return outs[:nw], outs[nw:]


def _rs_chips(tbs):
    nw = len(tbs)

    def body(*refs):
        t_refs, o_refs = refs[:nw], refs[nw:2 * nw]
        send_sems, recv_sems, local_sems = refs[2 * nw:]
        x, y, c = _mesh_pos()
        p = 2 * x + y
        chips = [(1 - x, y), (x, 1 - y), (1 - x, 1 - y)]
        own = [pltpu.make_async_copy(t_refs[wi].at[p], o_refs[wi].at[p], local_sems.at[wi]) for wi in range(nw)]
        cps = []
        for wi in range(nw):
            own[wi].start()
            for j, (qx, qy) in enumerate(chips):
                cps.append(pltpu.make_async_remote_copy(
                    src_ref=t_refs[wi].at[2 * qx + qy], dst_ref=o_refs[wi].at[p], send_sem=send_sems.at[3 * wi + j],
                    recv_sem=recv_sems.at[3 * wi + j], device_id=(qx, qy, c), device_id_type=MESH))
        for cp in cps:
            cp.start()
        for wi in range(nw):
            for j, (qx, qy) in enumerate(chips):
                pltpu.make_async_remote_copy(
                    src_ref=t_refs[wi].at[p], dst_ref=o_refs[wi].at[2 * qx + qy], send_sem=send_sems.at[3 * wi + j],
                    recv_sem=recv_sems.at[3 * wi + j], device_id=(qx, qy, c), device_id_type=MESH).wait_recv()
        for cp in cps:
            cp.wait_send()
        for cp in own:
            cp.wait()

    return pl.pallas_call(
        body, name="rs_chips",
        in_specs=[ANY] * nw, out_specs=[ANY] * nw,
        out_shape=[jax.ShapeDtypeStruct(t_.shape, t_.dtype) for t_ in tbs],
        scratch_shapes=[pltpu.SemaphoreType.DMA((3 * nw,)), pltpu.SemaphoreType.DMA((3 * nw,)),
                        pltpu.SemaphoreType.DMA((nw,))],
    )(*tbs)


def _row_tile(rows, cap=256):
    if rows <= cap:
        return rows
    return max(t_ for t_ in range(SUBLANES, cap + 1, SUBLANES) if rows % t_ == 0)


def _add_pairs(a, b, name):
    n, r, c_ = a.shape
    tr = _row_tile(r)

    def body(a_ref, b_ref, o_ref):
        o_ref[...] = (a_ref[...].astype(F32) + b_ref[...].astype(F32)).astype(o_ref.dtype)

    spec = pl.BlockSpec((1, tr, c_), lambda q, i: (q, i, 0))
    return pl.pallas_call(
        body, name=name, grid=(n, r // tr), in_specs=[spec, spec], out_specs=spec,
        out_shape=jax.ShapeDtypeStruct(a.shape, BF16), compiler_params=_params(("parallel", "parallel")),
    )(a, b)


def _small_allreduce(vec, name):
    rows = vec.shape[0]

    def body(x_ref, o_ref, buf, send_sems, recv_sems):
        x, y, c = _mesh_pos()
        me = 4 * x + 2 * y + c
        buf[me] = x_ref[...]
        cps = []
        for k in range(1, N_DEV):
            peer = (1 - x if k & 4 else x, 1 - y if k & 2 else y, 1 - c if k & 1 else c)
            cps.append(pltpu.make_async_remote_copy(
                src_ref=x_ref, dst_ref=buf.at[me], send_sem=send_sems.at[k - 1], recv_sem=recv_sems.at[k - 1],
                device_id=peer, device_id_type=MESH))
        for cp in cps:
            cp.start()
        for k in range(1, N_DEV):
            px, py, pc = (1 - x if k & 4 else x, 1 - y if k & 2 else y, 1 - c if k & 1 else c)
            pltpu.make_async_remote_copy(
                src_ref=x_ref, dst_ref=buf.at[4 * px + 2 * py + pc], send_sem=send_sems.at[k - 1],
                recv_sem=recv_sems.at[k - 1], device_id=(px, py, pc), device_id_type=MESH).wait_recv()
        for cp in cps:
            cp.wait_send()
        acc = buf[0]
        for k in range(1, N_DEV):
            acc = acc + buf[k]
        o_ref[...] = acc

    vm = pl.BlockSpec(memory_space=pltpu.VMEM)
    return pl.pallas_call(
        body, name=name,
        in_specs=[vm], out_specs=vm,
        out_shape=jax.ShapeDtypeStruct(vec.shape, F32),
        scratch_shapes=[pltpu.VMEM((N_DEV, rows, LANES), F32), pltpu.SemaphoreType.DMA((N_DEV - 1,)),
                        pltpu.SemaphoreType.DMA((N_DEV - 1,))],
    )(vec)


def _adamw(gparts, w, m, v, name):
    n, r, c_ = gparts.shape
    tr = _row_tile(r)
    c1 = 1.0 / (1.0 - B1 ** STEP)
    c2 = 1.0 / (1.0 - B2 ** STEP)

    def body(g_ref, w_ref, m_ref, v_ref, go_ref, d_ref, mo_ref, vo_ref):
        g = g_ref[0].astype(F32)
        for q in range(1, n):
            g = g + g_ref[q].astype(F32)
        mn = B1 * m_ref[...] + (1.0 - B1) * g
        vn = B2 * v_ref[...] + (1.0 - B2) * (g * g)
        go_ref[...] = g
        mo_ref[...] = mn
        vo_ref[...] = vn
        d_ref[...] = -LR * ((mn * c1) / (jnp.sqrt(vn * c2) + ADAM_EPS) + WD * w_ref[...])

    spec = pl.BlockSpec((tr, c_), lambda i: (i, 0))
    out = jax.ShapeDtypeStruct((r, c_), F32)
    return pl.pallas_call(
        body, name=name, grid=(r // tr,),
        in_specs=[pl.BlockSpec((n, tr, c_), lambda i: (0, i, 0)), spec, spec, spec],
        out_specs=[spec] * 4, out_shape=[out] * 4, compiler_params=_params(("parallel",)),
    )(gparts, w, m, v)


def kernel(x, w_in, b_gates, conv_w, conv_b, dt_bias, a_log, d_skip, ssd_norm_w, w_ssd_proj, w_pool_group, pool_scale, w_out, ln1_g, ln1_b, w_up, w_down, ln2_g, ln2_b, loss_target, m_w_in, m_b_gates, m_conv_w, m_conv_b, m_dt_bias, m_a_log, m_d_skip, m_ssd_norm_w, m_w_ssd_proj, m_w_pool_group, m_pool_scale, m_w_out, m_ln1_g, m_ln1_b, m_w_up, m_w_down, m_ln2_g, m_ln2_b, v_w_in, v_b_gates, v_conv_w, v_conv_b, v_dt_bias, v_a_log, v_d_skip, v_ssd_norm_w, v_w_ssd_proj, v_w_pool_group, v_pool_scale, v_w_out, v_ln1_g, v_ln1_b, v_w_up, v_w_down, v_ln2_g, v_ln2_b):
    ws = (w_in, b_gates, conv_w, conv_b, dt_bias, a_log, d_skip, ssd_norm_w, w_ssd_proj, w_pool_group, pool_scale,
          w_out, ln1_g, ln1_b, w_up, w_down, ln2_g, ln2_b)
    ms = (m_w_in, m_b_gates, m_conv_w, m_conv_b, m_dt_bias, m_a_log, m_d_skip, m_ssd_norm_w, m_w_ssd_proj,
          m_w_pool_group, m_pool_scale, m_w_out, m_ln1_g, m_ln1_b, m_w_up, m_w_down, m_ln2_g, m_ln2_b)
    vs = (v_w_in, v_b_gates, v_conv_w, v_conv_b, v_dt_bias, v_a_log, v_d_skip, v_ssd_norm_w, v_w_ssd_proj,
          v_w_pool_group, v_pool_scale, v_w_out, v_ln1_g, v_ln1_b, v_w_up, v_w_down, v_ln2_g, v_ln2_b)
    w = {n: a[0] for n, a in zip(NAMES, ws)}
    m = {n: a[0] for n, a in zip(NAMES, ms)}
    v = {n: a[0] for n, a in zip(NAMES, vs)}
    out_shapes = {n: a.shape for n, a in zip(NAMES, ws)}
    bl, s, d = x.shape
    x2, tgt2 = x.reshape(bl * s, d), loss_target.reshape(bl * s, d)
    xi, yi, ci = _mesh_pos()
    me = 4 * xi + 2 * yi + ci
    zero = jnp.zeros((), F32)
    shapes = {n: w[n].shape for n in NAMES}
    shape2d = {n: (_size(shapes[n][:-1]), shapes[n][-1]) for n in BIG}
    cwl = shapes["conv_w"][1]

    conv_place = lax.dynamic_update_slice(jnp.zeros((CONV_K, N_DEV * cwl), F32), w["conv_w"], (0, me * cwl))
    conv_full = _small_allreduce(_rows128(conv_place), "gather_conv_w")
    conv_full = conv_full.reshape(CONV_K, N_DEV * cwl)

    gathered = dict(zip(BIG, _all_gather([w[n].astype(BF16).reshape(shape2d[n]) for n in BIG])))
    cg = shapes["w_pool_group"][2]
    full = {n: w[n] for n in SMALL}
    full["conv_w"] = conv_full
    full["w_in_blocks"] = gathered["w_in"]
    full["w_up_blocks"] = gathered["w_up"]
    full["w_ssd_proj"] = gathered["w_ssd_proj"].reshape(-1, d)
    full["w_out"] = gathered["w_out"].reshape(-1, d)
    full["w_down"] = gathered["w_down"].reshape(-1, d)
    full["w_pool_group"] = gathered["w_pool_group"].reshape(N_DEV, len(POOL_WINDOWS), cg // N_DEV, cg).transpose(
        1, 0, 2, 3).reshape(len(POOL_WINDOWS), cg, cg)
    loss8, grad_x, g = _local_step(x2, tgt2, full, bl)

    small_sum = _small_allreduce(_pack_small(g, loss8[0, 0]), "small_allreduce")
    ex_shapes = {n: shapes[n] for n in SMALL}
    ex_shapes["conv_w"] = (CONV_K, N_DEV * cwl)
    gsum, loss = _unpack_small(small_sum, ex_shapes)
    gsum["conv_w"] = lax.dynamic_slice(gsum["conv_w"], (0, me * cwl), (CONV_K, cwl))
    gs_pk = _pack_small(gsum, zero)
    ws_pk, ms_pk, vs_pk = (_pack_small(t_, zero) for t_ in (w, m, v))
    small_out = _adamw(gs_pk[None], ws_pk, ms_pk, vs_pk, "adamw_small")
    loc_shapes = {n: shapes[n] for n in SMALL_PACK}
    res = [_unpack_small(o, loc_shapes)[0] for o in small_out]

    parts = [g[n].reshape((N_DEV,) + shape2d[n]) for n in BIG]
    kept, recv = _rs_sibling(parts)
    chip_sums = [_add_pairs(a, b, "rs_add_" + n) for n, a, b in zip(BIG, kept, recv)]
    for n, gp in zip(BIG, _rs_chips(chip_sums)):
        outs = _adamw(gp, *(t_[n].reshape(shape2d[n]) for t_ in (w, m, v)), "adamw_" + n)
        for r_, o in zip(res, outs):
            r_[n] = o

    def ordered(r_):
        return [r_[n].reshape(out_shapes[n]) for n in NAMES]

    return (loss, grad_x.reshape(bl, s, d), *ordered(res[0]), *ordered(res[1]), *ordered(res[2]), *ordered(res[3]))
```

```python
import collections
import functools

import jax
import jax.numpy as jnp
from jax import lax
from jax.experimental import pallas as pl
from jax.experimental.pallas import tpu as pltpu

F32 = jnp.float32
BF16 = jnp.bfloat16
MESH = pl.DeviceIdType.MESH

HEAD_DIM = 64
STATE = 128
GROUPS = 8
CONV_K = 4
CHUNK = 256
POOL_WINDOWS = (2, 4, 8, 16)
ALPHA = 2.0 ** 0.25
LN_EPS = 1e-5
RMS_EPS = 1e-5
LR, B1, B2, ADAM_EPS, WD, STEP = 0.001, 0.9, 0.999, 1e-08, 0.01, 10
N_DEV = 8
LANES = 128
SUBLANES = 8
VMEM_LIMIT = 56 * 1024 * 1024
NEG_BIG = -1e30

NN = (((1,), (0,)), ((), ()))
NT = (((1,), (1,)), ((), ()))
TN = (((0,), (0,)), ((), ()))


def _dot(a, b, dims=NN):
    return lax.dot_general(a.astype(BF16), b.astype(BF16), dims, preferred_element_type=F32)


def _dot_exact01(q, e, dims=NN):
    hi = q.astype(BF16)
    r1 = q - hi.astype(F32)
    mid = r1.astype(BF16)
    lo = (r1 - mid.astype(F32)).astype(BF16)
    f = lambda p: lax.dot_general(p, e, dims, preferred_element_type=F32)
    return f(hi) + f(mid) + f(lo)


def _params(sem):
    return pltpu.CompilerParams(dimension_semantics=sem, vmem_limit_bytes=VMEM_LIMIT)


def _sigmoid(x):
    return 1.0 / (1.0 + jnp.exp(-x))


def _colsum(x):
    return jnp.sum(x, axis=0, keepdims=True)


def _ln_fwd(r):
    mu = jnp.mean(r, axis=-1, keepdims=True)
    xc = r - mu
    var = jnp.mean(xc * xc, axis=-1, keepdims=True)
    rstd = lax.rsqrt(var + LN_EPS)
    return xc * rstd, rstd


def _ln_bwd(dy, xhat, rstd, g):
    dxh = dy * g
    m1 = jnp.mean(dxh, axis=-1, keepdims=True)
    m2 = jnp.mean(dxh * xhat, axis=-1, keepdims=True)
    return rstd * (dxh - m1 - xhat * m2)


_Comm = collections.namedtuple("_Comm", "inputs out_shapes scratch start wait")
ANY = pl.BlockSpec(memory_space=pl.ANY)


def _fuse_comm(body, grid, n_in, n_out, comm):
    if comm is None:
        return body
    ci, co = len(comm.inputs), len(comm.out_shapes)

    def fused(*refs):
        ins, cins = refs[:n_in], refs[n_in:n_in + ci]
        o0 = n_in + ci
        outs, couts = refs[o0:o0 + n_out], refs[o0 + n_out:o0 + n_out + co]
        rest = refs[o0 + n_out + co:]
        scr, cscr = rest[:len(rest) - len(comm.scratch)], rest[len(rest) - len(comm.scratch):]
        ids = [pl.program_id(a) for a in range(len(grid))]
        first, last = ids[0] == 0, ids[0] == grid[0] - 1
        for a in range(1, len(grid)):
            first, last = first & (ids[a] == 0), last & (ids[a] == grid[a] - 1)

        @pl.when(first)
        def _():
            comm.start(cins, couts, cscr)

        body(*ins, *outs, *scr)

        @pl.when(last)
        def _():
            comm.wait(cins, couts, cscr)

    return fused


def _comm_specs(comm):
    if comm is None:
        return [], [], [], []
    return list(comm.inputs), [ANY] * len(comm.inputs), [ANY] * len(comm.out_shapes), list(comm.out_shapes)


def _run_comm(comm, name):
    ci, co = len(comm.inputs), len(comm.out_shapes)

    def body(*refs):
        comm.start(refs[:ci], refs[ci:ci + co], refs[ci + co:])
        comm.wait(refs[:ci], refs[ci:ci + co], refs[ci + co:])

    return pl.pallas_call(body, name=name, in_specs=[ANY] * ci, out_specs=[ANY] * co, out_shape=list(comm.out_shapes),
                          scratch_shapes=list(comm.scratch))(*comm.inputs)


def _matmul(a, b, mode, out_dtype, bm, bn, bk, name, a_fn=None, col_blocks=0, comm=None):
    if mode == "nn":
        (m, k), n, dims = a.shape, b.shape[1], NN
    elif mode == "nt":
        (m, k), n, dims = a.shape, b.shape[0], NT
    else:
        (k, m), n, dims = a.shape, b.shape[1], TN
    bm, bn, bk = min(bm, m), min(bn, n), min(bk, k)
    assert m % bm == 0 and n % bn == 0 and k % bk == 0, (name, m, n, k, bm, bn, bk)
    nk = k // bk
    if mode == "nn":
        a_spec = pl.BlockSpec((bm, bk), lambda i, j, kk: (i, kk))
        b_spec = pl.BlockSpec((bk, bn), lambda i, j, kk: (kk, j))
    elif mode == "nt":
        a_spec = pl.BlockSpec((bm, bk), lambda i, j, kk: (i, kk))
        b_spec = pl.BlockSpec((bn, bk), lambda i, j, kk: (j, kk))
    else:
        a_spec = pl.BlockSpec((bk, bm), lambda i, j, kk: (kk, i))
        b_spec = pl.BlockSpec((bk, bn), lambda i, j, kk: (kk, j))

    def body(a_ref, b_ref, o_ref, acc_ref):
        kk = pl.program_id(2)

        @pl.when(kk == 0)
        def _():
            acc_ref[...] = jnp.zeros_like(acc_ref)

        av = a_ref[...]
        if a_fn is not None:
            av = a_fn(av.astype(F32))
        acc_ref[...] += _dot(av, b_ref[...], dims)

        @pl.when(kk == nk - 1)
        def _():
            o_ref[...] = acc_ref[...].astype(o_ref.dtype).reshape(o_ref.shape)

    if col_blocks:
        assert n == col_blocks * bn, (name, n, col_blocks, bn)
        out_spec = pl.BlockSpec((1, bm, bn), lambda i, j, kk: (j, i, 0))
        out_shape = jax.ShapeDtypeStruct((col_blocks, m, bn), out_dtype)
    else:
        out_spec = pl.BlockSpec((bm, bn), lambda i, j, kk: (i, j))
        out_shape = jax.ShapeDtypeStruct((m, n), out_dtype)
    grid = (m // bm, n // bn, nk)
    c_in, c_in_specs, c_out_specs, c_out_shapes = _comm_specs(comm)
    res = pl.pallas_call(
        _fuse_comm(body, grid, 2, 1, comm), name=name,
        grid=grid,
        in_specs=[a_spec, b_spec] + c_in_specs,
        out_specs=[out_spec] + c_out_specs,
        out_shape=[out_shape] + c_out_shapes,
        scratch_shapes=[pltpu.VMEM((bm, bn), F32)] + (list(comm.scratch) if comm else []),
        compiler_params=_params(("arbitrary",) * 3 if comm else ("parallel", "parallel", "arbitrary")),
    )(a, b, *c_in)
    return res if comm else res[0]


CONV_STRIP = 16


def _conv_pre(ext_ref, w_ref, b_ref, r0, rows):
    acc = b_ref[...] + w_ref[0:1, :] * ext_ref[pl.ds(r0 + SUBLANES - (CONV_K - 1), rows), :]
    for k in range(1, CONV_K):
        acc = acc + w_ref[k:k + 1, :] * ext_ref[pl.ds(r0 + SUBLANES - (CONV_K - 1) + k, rows), :]
    return acc


def _conv_fwd(proj, conv_w8, conv_b, n_seq_chunks, cd, ct, L):
    t = proj.shape[0]
    nbc = t // L
    hb = L // SUBLANES

    def body(x_ref, halo_ref, w_ref, b_ref, o_ref, ext_ref):
        bc = pl.program_id(0)
        first = (bc % n_seq_chunks) == 0
        halo = halo_ref[...].astype(F32)
        ext_ref[0:SUBLANES, :] = jnp.where(first, 0.0, halo)
        ext_ref[SUBLANES:, :] = x_ref[...].astype(F32)
        for r0 in range(0, L, CONV_STRIP):
            acc = _conv_pre(ext_ref, w_ref, b_ref, r0, CONV_STRIP)
            o_ref[r0:r0 + CONV_STRIP, :] = (acc * _sigmoid(acc)).astype(o_ref.dtype)

    return pl.pallas_call(
        body, name="conv_fwd",
        grid=(nbc, cd // ct),
        in_specs=[
            pl.BlockSpec((L, ct), lambda i, j: (i, j)),
            pl.BlockSpec((SUBLANES, ct), lambda i, j: (jnp.maximum(i * hb - 1, 0), j)),
            pl.BlockSpec((SUBLANES, ct), lambda i, j: (0, j)),
            pl.BlockSpec((1, ct), lambda i, j: (0, j)),
        ],
        out_specs=pl.BlockSpec((L, ct), lambda i, j: (i, j)),
        out_shape=jax.ShapeDtypeStruct((t, cd), BF16),
        scratch_shapes=[pltpu.VMEM((L + SUBLANES, ct), F32)],
        compiler_params=_params(("parallel", "parallel")),
    )(proj, proj, conv_w8, conv_b)


def _conv_bwd(proj, dxbc, conv_w8, conv_b, n_seq_chunks, col0, width, ct, L, name):
    t = proj.shape[0]
    nbc = t // L
    hb = L // SUBLANES
    ct = min(ct, width)
    assert col0 % ct == 0 and width % ct == 0
    cb0 = col0 // ct
    last_hb = t // SUBLANES - 1

    def body(x_ref, xb_ref, xa_ref, d_ref, da_ref, w_ref, b_ref, o_ref, dw_ref, db_ref, ext_ref, dc_ref):
        bc = pl.program_id(1)
        first = (bc % n_seq_chunks) == 0
        last = (bc % n_seq_chunks) == n_seq_chunks - 1

        @pl.when(bc == 0)
        def _():
            dw_ref[...] = jnp.zeros_like(dw_ref)
            db_ref[...] = jnp.zeros_like(db_ref)

        ext_ref[0:SUBLANES, :] = jnp.where(first, 0.0, xb_ref[...].astype(F32))
        ext_ref[SUBLANES:SUBLANES + L, :] = x_ref[...].astype(F32)
        ext_ref[SUBLANES + L:, :] = xa_ref[...].astype(F32)

        def dsilu_at(r0, rows):
            acc = _conv_pre(ext_ref, w_ref, b_ref, r0, rows)
            sg = _sigmoid(acc)
            return sg * (1.0 + acc * (1.0 - sg))

        for r0 in range(0, L, CONV_STRIP):
            dc_ref[r0:r0 + CONV_STRIP, :] = d_ref[r0:r0 + CONV_STRIP, :].astype(F32) * dsilu_at(r0, CONV_STRIP)
        dc_ref[L:, :] = jnp.where(last, 0.0, da_ref[...].astype(F32)) * dsilu_at(L, SUBLANES)
        fold = lambda v: v[0:SUBLANES] + v[SUBLANES:CONV_STRIP]
        dws = [jnp.zeros((SUBLANES, ct), F32) for _ in range(CONV_K)]
        dbs = jnp.zeros((SUBLANES, ct), F32)
        for r0 in range(0, L, CONV_STRIP):
            dc = dc_ref[r0:r0 + CONV_STRIP, :]
            dx = w_ref[CONV_K - 1:CONV_K, :] * dc
            for k in range(CONV_K - 1):
                dx = dx + w_ref[k:k + 1, :] * dc_ref[pl.ds(r0 + CONV_K - 1 - k, CONV_STRIP), :]
            o_ref[r0:r0 + CONV_STRIP, :] = dx.astype(o_ref.dtype)
            for k in range(CONV_K):
                dws[k] = dws[k] + fold(dc * ext_ref[pl.ds(r0 + SUBLANES - (CONV_K - 1) + k, CONV_STRIP), :])
            dbs = dbs + fold(dc)
        for k in range(CONV_K):
            dw_ref[k:k + 1, :] += _colsum(dws[k])
        db_ref[0:1, :] += _colsum(dbs)

    return pl.pallas_call(
        body, name=name,
        grid=(width // ct, nbc),
        in_specs=[
            pl.BlockSpec((L, ct), lambda j, i: (i, cb0 + j)),
            pl.BlockSpec((SUBLANES, ct), lambda j, i: (jnp.maximum(i * hb - 1, 0), cb0 + j)),
            pl.BlockSpec((SUBLANES, ct), lambda j, i: (jnp.minimum((i + 1) * hb, last_hb), cb0 + j)),
            pl.BlockSpec((L, ct), lambda j, i: (i, j)),
            pl.BlockSpec((SUBLANES, ct), lambda j, i: (jnp.minimum((i + 1) * hb, last_hb), j)),
            pl.BlockSpec((SUBLANES, ct), lambda j, i: (0, cb0 + j)),
            pl.BlockSpec((1, ct), lambda j, i: (0, cb0 + j)),
        ],
        out_specs=[
            pl.BlockSpec((L, ct), lambda j, i: (i, j)),
            pl.BlockSpec((SUBLANES, ct), lambda j, i: (0, j)),
            pl.BlockSpec((SUBLANES, ct), lambda j, i: (0, j)),
        ],
        out_shape=[
            jax.ShapeDtypeStruct((t, width), BF16),
            jax.ShapeDtypeStruct((SUBLANES, width), F32),
            jax.ShapeDtypeStruct((SUBLANES, width), F32),
        ],
        scratch_shapes=[pltpu.VMEM((L + 2 * SUBLANES, ct), F32), pltpu.VMEM((L + SUBLANES, ct), F32)],
        compiler_params=_params(("parallel", "arbitrary")),
    )(proj, proj, proj, dxbc, dxbc, conv_w8, conv_b)


def _cumsum_rows(x, reverse=False):
    n = x.shape[0]
    row = lax.broadcasted_iota(jnp.int32, x.shape, 0)
    s = 1
    while s < n:
        if reverse:
            x = x + jnp.where(row < n - s, pltpu.roll(x, n - s, 0), 0.0)
        else:
            x = x + jnp.where(row >= s, pltpu.roll(x, s, 0), 0.0)
        s *= 2
    return x


def _ssd_scalars(dtr, dtb, alog):
    pre = dtr + dtb
    dt = jnp.maximum(pre, 0.0) + jnp.log(1.0 + jnp.exp(-jnp.abs(pre)))
    a = -jnp.exp(alog)
    acs = _cumsum_rows(dt * a)
    return pre, dt, a, acs


SSD_GROUPS_PER_STEP = 4


def _ssd_group_common(xs, dt_s, acs_s, e):
    L = xs.shape[0]
    dt_x = _dot_exact01(dt_s, e)
    acs_x = _dot_exact01(acs_s, e)
    e_x = jnp.exp(acs_x)
    a_last = acs_x[L - 1:L, :]
    dec_x = jnp.exp(a_last - acs_x)
    return dt_x, acs_x, e_x, dec_x


def _decay_matrix(acs_x, acs_t, r, tri):
    col = acs_x[:, r * HEAD_DIM:r * HEAD_DIM + 1]
    rowv = acs_t[r * HEAD_DIM:r * HEAD_DIM + 1, :]
    return jnp.exp(jnp.where(tri, col - rowv, NEG_BIG))


def _ssd_fwd(xbc, proj, dt_raw, dtb, alog, dskip_x, normw, emat, bl, inner, z_col0):
    t = xbc.shape[0]
    L = CHUNK
    nc = t // bl // L
    G = GROUPS
    gw = inner // G
    hpg = gw // HEAD_DIM
    assert z_col0 % gw == 0
    zb0 = z_col0 // gw
    bb0 = inner // STATE
    cb0 = bb0 + G

    P = SSD_GROUPS_PER_STEP
    assert G % P == 0 and bb0 % P == 0 and cb0 % P == 0 and zb0 % P == 0

    def body(xs_ref, b_ref, c_ref, z_ref, dtr_ref, dtb_ref, alog_ref, dsk_ref, nw_ref, e_ref,
             y_ref, yn_ref, st_ref, h_ref, dt_s, acs_s):
        c = pl.program_id(1)
        gb = pl.program_id(2)

        @pl.when(gb == 0)
        def _():
            _, dt, _, acs = _ssd_scalars(dtr_ref[...], dtb_ref[...], alog_ref[...])
            dt_s[...] = dt
            acs_s[...] = acs

        tri = lax.broadcasted_iota(jnp.int32, (L, L), 0) >= lax.broadcasted_iota(jnp.int32, (L, L), 1)
        lane = lax.broadcasted_iota(jnp.int32, (L, gw), 1)
        for gi in range(P):
            g = gb * P + gi
            cols = slice(gi * gw, (gi + 1) * gw)
            ncol = slice(gi * STATE, (gi + 1) * STATE)

            @pl.when(c == 0)
            def _():
                h_ref[g] = jnp.zeros((STATE, gw), F32)

            xs = xs_ref[:, cols].astype(F32)
            bg = b_ref[:, ncol]
            cg = c_ref[:, ncol]
            dt_x, acs_x, e_x, dec_x = _ssd_group_common(xs, dt_s[...], acs_s[...], e_ref[:, cols])
            xdt = xs * dt_x
            cb = _dot(cg, bg, NT)
            acs_t = acs_x.T
            h = h_ref[g]
            st_ref[0, gi] = h
            y = _dot(cg, h) * e_x + dsk_ref[:, cols] * xs
            for r in range(hpg):
                lm = _decay_matrix(acs_x, acs_t, r, tri)
                m = cb * lm
                xr = jnp.where((lane >= r * HEAD_DIM) & (lane < (r + 1) * HEAD_DIM), xdt, 0.0)
                y = y + _dot(m, xr)
            h_ref[g] = h * e_x[L - 1:L, :] + _dot(bg, xdt * dec_x, TN)
            yq = y.astype(y_ref.dtype)
            y_ref[:, cols] = yq
            z = z_ref[:, cols].astype(F32)
            yg = yq.astype(F32) * (z * _sigmoid(z))
            rs = lax.rsqrt(jnp.mean(yg * yg, axis=-1, keepdims=True) + RMS_EPS)
            yn_ref[:, cols] = (yg * rs * nw_ref[:, cols]).astype(yn_ref.dtype)

    return pl.pallas_call(
        body, name="ssd_fwd",
        grid=(bl, nc, G // P),
        in_specs=[
            pl.BlockSpec((L, P * gw), lambda b, c, g: (b * nc + c, g)),
            pl.BlockSpec((L, P * STATE), lambda b, c, g: (b * nc + c, bb0 // P + g)),
            pl.BlockSpec((L, P * STATE), lambda b, c, g: (b * nc + c, cb0 // P + g)),
            pl.BlockSpec((L, P * gw), lambda b, c, g: (b * nc + c, zb0 // P + g)),
            pl.BlockSpec((L, LANES), lambda b, c, g: (b * nc + c, 0)),
            pl.BlockSpec((1, LANES), lambda b, c, g: (0, 0)),
            pl.BlockSpec((1, LANES), lambda b, c, g: (0, 0)),
            pl.BlockSpec((1, P * gw), lambda b, c, g: (0, g)),
            pl.BlockSpec((1, P * gw), lambda b, c, g: (0, g)),
            pl.BlockSpec((LANES, P * gw), lambda b, c, g: (0, g)),
        ],
        out_specs=[
            pl.BlockSpec((L, P * gw), lambda b, c, g: (b * nc + c, g)),
            pl.BlockSpec((L, P * gw), lambda b, c, g: (b * nc + c, g)),
            pl.BlockSpec((1, P, STATE, gw), lambda b, c, g: (b * nc + c, g, 0, 0)),
        ],
        out_shape=[
            jax.ShapeDtypeStruct((t, inner), BF16),
            jax.ShapeDtypeStruct((t, inner), BF16),
            jax.ShapeDtypeStruct((bl * nc, G, STATE, gw), F32),
        ],
        scratch_shapes=[pltpu.VMEM((G, STATE, gw), F32), pltpu.VMEM((L, LANES), F32), pltpu.VMEM((L, LANES), F32)],
        compiler_params=_params(("arbitrary", "arbitrary", "arbitrary")),
    )(xbc, xbc, xbc, proj, dt_raw, dtb, alog, dskip_x, normw, emat)


def _ssd_bwd(xbc, proj, dt_raw, y, dyn, states, dtb, alog, dskip_x, normw, emat, emat_t, bl, inner, z_col0,
             comm=None):
    t = xbc.shape[0]
    L = CHUNK
    nc = t // bl // L
    G = GROUPS
    gw = inner // G
    hpg = gw // HEAD_DIM
    zb0 = z_col0 // gw
    bb0 = inner // STATE
    cb0 = bb0 + G
    P = SSD_GROUPS_PER_STEP

    def rc(j):
        return nc - 1 - j

    def body(xs_ref, b_ref, c_ref, z_ref, dtr_ref, y_ref, dyn_ref, st_ref, dtb_ref, alog_ref, dsk_ref,
             nw_ref, e_ref, et_ref,
             dxs_ref, db_ref, dc_ref, dz_ref, ddt_ref, dnw_ref, dsk_acc, dalog_acc, ddtb_acc,
             dh_ref, pre_s, dt_s, acs_s, wacs_s, wdt_s):
        b = pl.program_id(0)
        j = pl.program_id(1)
        gb = pl.program_id(2)

        @pl.when((b == 0) & (j == 0) & (gb == 0))
        def _():
            dsk_acc[...] = jnp.zeros_like(dsk_acc)
            dalog_acc[...] = jnp.zeros_like(dalog_acc)
            ddtb_acc[...] = jnp.zeros_like(ddtb_acc)

        @pl.when(gb == 0)
        def _():
            pre, dt, _, acs = _ssd_scalars(dtr_ref[...], dtb_ref[...], alog_ref[...])
            pre_s[...] = pre
            dt_s[...] = dt
            acs_s[...] = acs
            wacs_s[...] = jnp.zeros_like(wacs_s)
            wdt_s[...] = jnp.zeros_like(wdt_s)

        tri = lax.broadcasted_iota(jnp.int32, (L, L), 0) >= lax.broadcasted_iota(jnp.int32, (L, L), 1)
        lane = lax.broadcasted_iota(jnp.int32, (L, gw), 1)
        rowi = lax.broadcasted_iota(jnp.int32, (L, gw), 0)
        for gi in range(P):
            g = gb * P + gi
            cols = slice(gi * gw, (gi + 1) * gw)
            ncol = slice(gi * STATE, (gi + 1) * STATE)

            @pl.when((b == 0) & (j == 0))
            def _():
                dnw_ref[g] = jnp.zeros((SUBLANES, gw), F32)

            @pl.when(j == 0)
            def _():
                dh_ref[g] = jnp.zeros((STATE, gw), F32)

            xs = xs_ref[:, cols].astype(F32)
            bg = b_ref[:, ncol]
            cg = c_ref[:, ncol]
            dt_x, acs_x, e_x, dec_x = _ssd_group_common(xs, dt_s[...], acs_s[...], e_ref[:, cols])
            xdt = xs * dt_x
            xdt_b = xdt.astype(BF16)
            cb = _dot(cg, bg, NT)
            acs_t = acs_x.T
            h = st_ref[0, gi]
            hb16 = h.astype(BF16)
            dsk = dsk_ref[:, cols]

            yv = y_ref[:, cols].astype(F32)
            z = z_ref[:, cols].astype(F32)
            sgz = _sigmoid(z)
            sz = z * sgz
            yg = yv * sz
            rs = lax.rsqrt(jnp.mean(yg * yg, axis=-1, keepdims=True) + RMS_EPS)
            yhat = yg * rs
            dyn_v = dyn_ref[:, cols].astype(F32)
            dnw_ref[g] += _colsum(dyn_v * yhat)
            dyh = dyn_v * nw_ref[:, cols]
            dyg = rs * (dyh - yhat * jnp.mean(dyh * yhat, axis=-1, keepdims=True))
            dy = dyg * sz
            dz_ref[:, cols] = (dyg * yv * (sgz * (1.0 + z * (1.0 - sgz)))).astype(dz_ref.dtype)

            dy_b = dy.astype(BF16)
            dcb = jnp.zeros((L, L), F32)
            dxdt_d = jnp.zeros((L, gw), F32)
            ydiag = jnp.zeros((L, gw), F32)
            for r in range(hpg):
                lm = _decay_matrix(acs_x, acs_t, r, tri)
                m = (cb * lm).astype(BF16)
                sel = (lane >= r * HEAD_DIM) & (lane < (r + 1) * HEAD_DIM)
                dyr = jnp.where(sel, dy_b, jnp.zeros_like(dy_b))
                xr = jnp.where(sel, xdt_b, jnp.zeros_like(xdt_b))
                ydiag = ydiag + _dot(m, xr)
                dcb = dcb + _dot(dyr, xdt_b, NT) * lm
                dxdt_d = dxdt_d + _dot(m, dyr, TN)
            dh = dh_ref[g]
            dh16 = dh.astype(BF16)
            xdec_b = (xdt * dec_x).astype(BF16)
            bdh = _dot(bg, dh16)
            dxdt = dxdt_d + dec_x * bdh
            dcb16 = dcb.astype(BF16)
            dye = (dy * e_x).astype(BF16)
            db_ref[:, ncol] = (_dot(dcb16, cg, TN) + _dot(xdec_b, dh16, NT)).astype(db_ref.dtype)
            dc_ref[:, ncol] = (_dot(dcb16, bg) + _dot(dye, hb16, NT)).astype(dc_ref.dtype)
            dprev = _dot(cg, dye, TN)
            cd_row = e_x[L - 1:L, :]
            s_new = _dot(bg, xdec_b, TN)
            last_term = _colsum(dh16.astype(F32) * s_new) + _colsum(dh * h) * cd_row
            yoff = _dot(cg, hb16) * e_x
            wfold = (dy_b.astype(F32) * ydiag + dy * yoff - dxdt_d * xdt_b.astype(F32) - bdh * xdec_b.astype(F32)
                     + jnp.where(rowi == L - 1, last_term, 0.0))
            et = et_ref[cols, :]
            wacs_s[...] += _dot_exact01(wfold, et)
            wdt_s[...] += _dot_exact01(dxdt * xs, et)
            dsk_acc[...] += _dot_exact01(jnp.broadcast_to(_colsum(dy * xs), (SUBLANES, gw)), et)
            dxs_ref[:, cols] = (dsk * dy + dxdt * dt_x).astype(dxs_ref.dtype)
            dh_ref[g] = dprev + cd_row * dh

        @pl.when(gb == G // P - 1)
        def _():
            a = -jnp.exp(alog_ref[...])
            dda = _cumsum_rows(wacs_s[...], reverse=True)
            ddt = wdt_s[...] + dda * a
            ddt_raw = ddt * _sigmoid(pre_s[...])
            ddt_ref[...] = ddt_raw
            dalog_acc[...] += _colsum(dda * dt_s[...]) * a
            ddtb_acc[...] += _colsum(ddt_raw)

    def cidx(b, j):
        return b * nc + rc(j)

    accs = lambda shape: pl.BlockSpec(shape, lambda b, j, g: tuple(0 for _ in shape))
    grid = (bl, nc, G // P)
    c_in, c_in_specs, c_out_specs, c_out_shapes = _comm_specs(comm)
    return pl.pallas_call(
        _fuse_comm(body, grid, 14, 9, comm), name="ssd_bwd",
        grid=grid,
        in_specs=[
            pl.BlockSpec((L, P * gw), lambda b, j, g: (cidx(b, j), g)),
            pl.BlockSpec((L, P * STATE), lambda b, j, g: (cidx(b, j), bb0 // P + g)),
            pl.BlockSpec((L, P * STATE), lambda b, j, g: (cidx(b, j), cb0 // P + g)),
            pl.BlockSpec((L, P * gw), lambda b, j, g: (cidx(b, j), zb0 // P + g)),
            pl.BlockSpec((L, LANES), lambda b, j, g: (cidx(b, j), 0)),
            pl.BlockSpec((L, P * gw), lambda b, j, g: (cidx(b, j), g)),
            pl.BlockSpec((L, P * gw), lambda b, j, g: (cidx(b, j), g)),
            pl.BlockSpec((1, P, STATE, gw), lambda b, j, g: (cidx(b, j), g, 0, 0)),
            pl.BlockSpec((1, LANES), lambda b, j, g: (0, 0)),
            pl.BlockSpec((1, LANES), lambda b, j, g: (0, 0)),
            pl.BlockSpec((1, P * gw), lambda b, j, g: (0, g)),
            pl.BlockSpec((1, P * gw), lambda b, j, g: (0, g)),
            pl.BlockSpec((LANES, P * gw), lambda b, j, g: (0, g)),
            pl.BlockSpec((P * gw, LANES), lambda b, j, g: (g, 0)),
        ] + c_in_specs,
        out_specs=[
            pl.BlockSpec((L, P * gw), lambda b, j, g: (cidx(b, j), g)),
            pl.BlockSpec((L, P * STATE), lambda b, j, g: (cidx(b, j), g)),
            pl.BlockSpec((L, P * STATE), lambda b, j, g: (cidx(b, j), g)),
            pl.BlockSpec((L, P * gw), lambda b, j, g: (cidx(b, j), g)),
            pl.BlockSpec((L, LANES), lambda b, j, g: (cidx(b, j), 0)),
            accs((G, SUBLANES, gw)),
            accs((SUBLANES, LANES)),
            accs((SUBLANES, LANES)),
            accs((SUBLANES, LANES)),
        ] + c_out_specs,
        out_shape=[
            jax.ShapeDtypeStruct((t, inner), BF16),
            jax.ShapeDtypeStruct((t, G * STATE), BF16),
            jax.ShapeDtypeStruct((t, G * STATE), BF16),
            jax.ShapeDtypeStruct((t, inner), BF16),
            jax.ShapeDtypeStruct((t, LANES), F32),
            jax.ShapeDtypeStruct((G, SUBLANES, gw), F32),
            jax.ShapeDtypeStruct((SUBLANES, LANES), F32),
            jax.ShapeDtypeStruct((SUBLANES, LANES), F32),
            jax.ShapeDtypeStruct((SUBLANES, LANES), F32),
        ] + c_out_shapes,
        scratch_shapes=[pltpu.VMEM((G, STATE, gw), F32)] + [pltpu.VMEM((L, LANES), F32)] * 5
        + (list(comm.scratch) if comm else []),
        compiler_params=_params(("arbitrary", "arbitrary", "arbitrary")),
    )(xbc, xbc, xbc, proj, dt_raw, y, dyn, states, dtb, alog, dskip_x, normw, emat, emat_t, *c_in)


def _pool_window(u, w, anti):
    n = u.shape[0]
    row = lax.broadcasted_iota(jnp.int32, u.shape, 0)
    acc = u
    s = 1
    while s < w:
        if anti:
            acc = acc + jnp.where(row < n - s, pltpu.roll(acc, n - s, 0), 0.0)
        else:
            acc = acc + jnp.where(row >= s, pltpu.roll(acc, s, 0), 0.0)
        s *= 2
    return acc


def _pool_cnt(shape, w):
    row = lax.broadcasted_iota(jnp.int32, shape, 0)
    return jnp.minimum(row + 1, w).astype(F32)


def _pool_fwd(proj, wpg, bl, d, u_col0):
    t = proj.shape[0]
    s = t // bl
    pg = len(POOL_WINDOWS)
    cg = d // pg
    ub0 = u_col0 // d

    def body(u_ref, w_ref, o_ref):
        for gi, w in enumerate(POOL_WINDOWS):
            u = u_ref[:, gi * cg:(gi + 1) * cg].astype(F32)
            pooled = _pool_window(u, w, False) / _pool_cnt(u.shape, w) - u
            o_ref[:, gi * cg:(gi + 1) * cg] = _dot(pooled, w_ref[gi]).astype(o_ref.dtype)

    return pl.pallas_call(
        body, name="pool_fwd",
        grid=(bl,),
        in_specs=[pl.BlockSpec((s, d), lambda b: (b, ub0)), pl.BlockSpec((pg, cg, cg), lambda b: (0, 0, 0))],
        out_specs=pl.BlockSpec((s, d), lambda b: (b, 0)),
        out_shape=jax.ShapeDtypeStruct((t, d), BF16),
        compiler_params=_params(("parallel",)),
    )(proj, wpg)


def _pool_bwd(proj, dyp, wpg, bl, d, u_col0):
    t = proj.shape[0]
    s = t // bl
    pg = len(POOL_WINDOWS)
    cg = d // pg
    ub0 = u_col0 // d

    def body(u_ref, dy_ref, w_ref, du_ref, dw_ref):
        @pl.when(pl.program_id(0) == 0)
        def _():
            dw_ref[...] = jnp.zeros_like(dw_ref)

        for gi, w in enumerate(POOL_WINDOWS):
            u = u_ref[:, gi * cg:(gi + 1) * cg].astype(F32)
            cnt = _pool_cnt(u.shape, w)
            pooled = _pool_window(u, w, False) / cnt - u
            dy = dy_ref[:, gi * cg:(gi + 1) * cg]
            dw_ref[gi] += _dot(pooled, dy, TN)
            dp = _dot(dy, w_ref[gi], NT)
            du_ref[:, gi * cg:(gi + 1) * cg] = (_pool_window(dp / cnt, w, True) - dp).astype(du_ref.dtype)

    return pl.pallas_call(
        body, name="pool_bwd",
        grid=(bl,),
        in_specs=[pl.BlockSpec((s, d), lambda b: (b, ub0)), pl.BlockSpec((s, d), lambda b: (b, 0)),
                  pl.BlockSpec((pg, cg, cg), lambda b: (0, 0, 0))],
        out_specs=[pl.BlockSpec((s, d), lambda b: (b, 0)), pl.BlockSpec((pg, cg, cg), lambda b: (0, 0, 0))],
        out_shape=[jax.ShapeDtypeStruct((t, d), BF16), jax.ShapeDtypeStruct((pg, cg, cg), F32)],
        compiler_params=_params(("arbitrary",)),
    )(proj, dyp, wpg)


def _merge_fwd(proj, ypr, yssd, x, w_out, b_gates, pool_scale, d, lg_col0, tm):
    t = x.shape[0]
    lb0 = lg_col0 // (2 * d)

    def body(lg_ref, yp_ref, ys_ref, x_ref, w_ref, bg_ref, ps_ref, mg_ref, r1_ref):
        lg = lg_ref[...].astype(F32) + bg_ref[...]
        ga = _sigmoid(lg[:, :d])
        gb = _sigmoid(lg[:, d:])
        merged = ga * (yp_ref[...].astype(F32) * ps_ref[...]) + gb * ys_ref[...].astype(F32)
        mg_ref[...] = merged.astype(mg_ref.dtype)
        r1_ref[...] = ALPHA * x_ref[...] + _dot(mg_ref[...], w_ref[...])

    row = lambda w: pl.BlockSpec((tm, w), lambda i: (i, 0))
    full = lambda a: pl.BlockSpec(a.shape, lambda i: (0, 0))
    return pl.pallas_call(
        body, name="merge_fwd",
        grid=(t // tm,),
        in_specs=[pl.BlockSpec((tm, 2 * d), lambda i: (i, lb0)), row(d), row(d), row(d), full(w_out), full(b_gates),
                  full(pool_scale)],
        out_specs=[row(d), row(d)],
        out_shape=[jax.ShapeDtypeStruct((t, d), BF16), jax.ShapeDtypeStruct((t, d), F32)],
        compiler_params=_params(("parallel",)),
    )(proj, ypr, yssd, x, w_out, b_gates, pool_scale)


def _merge_bwd(dr1, proj, ypr, yssd, w_out, b_gates, pool_scale, d, lg_col0, tm):
    t = dr1.shape[0]
    lb0 = lg_col0 // (2 * d)

    def body(dr_ref, lg_ref, yp_ref, ys_ref, w_ref, bg_ref, ps_ref, dlg_ref, dyp_ref, dys_ref, dbg_ref, dps_ref):
        @pl.when(pl.program_id(0) == 0)
        def _():
            dbg_ref[...] = jnp.zeros_like(dbg_ref)
            dps_ref[...] = jnp.zeros_like(dps_ref)

        dm = _dot(dr_ref[...], w_ref[...], NT)
        lg = lg_ref[...].astype(F32) + bg_ref[...]
        ga = _sigmoid(lg[:, :d])
        gb = _sigmoid(lg[:, d:])
        ypr_v = yp_ref[...].astype(F32)
        ys_v = ys_ref[...].astype(F32)
        ps = ps_ref[...]
        dga = dm * ypr_v * ps
        dla = dga * ga * (1.0 - ga)
        dlb = dm * ys_v * gb * (1.0 - gb)
        dlg_ref[:, :d] = dla.astype(dlg_ref.dtype)
        dlg_ref[:, d:] = dlb.astype(dlg_ref.dtype)
        dyp_ref[...] = (dm * ga * ps).astype(dyp_ref.dtype)
        dys_ref[...] = (dm * gb).astype(dys_ref.dtype)
        dbg_ref[0:1, :d] += _colsum(dla)
        dbg_ref[0:1, d:] += _colsum(dlb)
        dps_ref[0:1, :] += _colsum(dm * ga * ypr_v)

    row = lambda w: pl.BlockSpec((tm, w), lambda i: (i, 0))
    full = lambda a: pl.BlockSpec(a.shape, lambda i: (0, 0))
    acc = lambda w: pl.BlockSpec((SUBLANES, w), lambda i: (0, 0))
    return pl.pallas_call(
        body, name="merge_bwd",
        grid=(t // tm,),
        in_specs=[row(d), pl.BlockSpec((tm, 2 * d), lambda i: (i, lb0)), row(d), row(d), full(w_out), full(b_gates),
                  full(pool_scale)],
        out_specs=[row(2 * d), row(d), row(d), acc(2 * d), acc(d)],
        out_shape=[jax.ShapeDtypeStruct((t, 2 * d), BF16), jax.ShapeDtypeStruct((t, d), BF16),
                   jax.ShapeDtypeStruct((t, d), BF16), jax.ShapeDtypeStruct((SUBLANES, 2 * d), F32),
                   jax.ShapeDtypeStruct((SUBLANES, d), F32)],
        compiler_params=_params(("arbitrary",)),
    )(dr1, proj, ypr, yssd, w_out, b_gates, pool_scale)


def _mlp_fwd(r1, target, w_up, w_down, ln1_g, ln1_b, ln2_g, ln2_b, tm):
    t, d = r1.shape
    nf, _, tf = w_up.shape
    ff = nf * tf

    def body(r1_ref, tg_ref, wu_ref, wd_ref, g1_ref, b1_ref, g2_ref, b2_ref,
             up_ref, h1_ref, dr2_ref, loss_ref, dg2_ref, db2_ref, h1f, acc):
        i = pl.program_id(0)
        f = pl.program_id(1)

        @pl.when((i == 0) & (f == 0))
        def _():
            loss_ref[...] = jnp.zeros_like(loss_ref)
            dg2_ref[...] = jnp.zeros_like(dg2_ref)
            db2_ref[...] = jnp.zeros_like(db2_ref)

        @pl.when(f == 0)
        def _():
            xhat, _ = _ln_fwd(r1_ref[...])
            h1 = xhat * g1_ref[...] + b1_ref[...]
            h1f[...] = h1
            h1_ref[...] = h1.astype(h1_ref.dtype)
            acc[...] = jnp.zeros_like(acc)

        up_ref[...] = _dot(h1_ref[...], wu_ref[0]).astype(up_ref.dtype)
        upq = jnp.maximum(up_ref[...].astype(F32), 0.0)
        acc[...] += _dot(upq * upq, wd_ref[...])

        @pl.when(f == nf - 1)
        def _():
            xhat, rstd = _ln_fwd(ALPHA * h1f[...] + acc[...])
            g2 = g2_ref[...]
            diff = xhat * g2 + b2_ref[...] - tg_ref[...]
            loss_ref[...] += 0.5 / d * jnp.sum(diff * diff)
            dh2 = diff * (1.0 / d)
            dg2_ref[0:1, :] += _colsum(dh2 * xhat)
            db2_ref[0:1, :] += _colsum(dh2)
            dr2_ref[...] = _ln_bwd(dh2, xhat, rstd, g2).astype(dr2_ref.dtype)

    row = pl.BlockSpec((tm, d), lambda i, f: (i, 0))
    vec = pl.BlockSpec((1, d), lambda i, f: (0, 0))
    acc8 = pl.BlockSpec((SUBLANES, d), lambda i, f: (0, 0))
    return pl.pallas_call(
        body, name="mlp_fwd",
        grid=(t // tm, nf),
        in_specs=[row, row, pl.BlockSpec((1, d, tf), lambda i, f: (f, 0, 0)), pl.BlockSpec((tf, d), lambda i, f: (f, 0)),
                  vec, vec, vec, vec],
        out_specs=[pl.BlockSpec((tm, tf), lambda i, f: (i, f)), row, row,
                   pl.BlockSpec((SUBLANES, LANES), lambda i, f: (0, 0)), acc8, acc8],
        out_shape=[jax.ShapeDtypeStruct((t, ff), BF16), jax.ShapeDtypeStruct((t, d), BF16),
                   jax.ShapeDtypeStruct((t, d), BF16), jax.ShapeDtypeStruct((SUBLANES, LANES), F32),
                   jax.ShapeDtypeStruct((SUBLANES, d), F32), jax.ShapeDtypeStruct((SUBLANES, d), F32)],
        scratch_shapes=[pltpu.VMEM((tm, d), F32), pltpu.VMEM((tm, d), F32)],
        compiler_params=_params(("arbitrary", "arbitrary")),
    )(r1, target, w_up, w_down, ln1_g, ln1_b, ln2_g, ln2_b)


def _mlp_bwd(dr2, up, r1, w_up, w_down, ln1_g, tm):
    t, d = r1.shape
    nf, _, tf = w_up.shape
    ff = nf * tf

    def body(dr2_ref, up_ref, r1_ref, wu_ref, wd_ref, g1_ref, dup_ref, dr1_ref, dg1_ref, db1_ref, acc):
        i = pl.program_id(0)
        f = pl.program_id(1)

        @pl.when((i == 0) & (f == 0))
        def _():
            dg1_ref[...] = jnp.zeros_like(dg1_ref)
            db1_ref[...] = jnp.zeros_like(db1_ref)

        @pl.when(f == 0)
        def _():
            acc[...] = jnp.zeros_like(acc)

        dact = _dot(dr2_ref[...], wd_ref[...], NT)
        dup_ref[...] = (dact * 2.0 * jnp.maximum(up_ref[...].astype(F32), 0.0)).astype(dup_ref.dtype)
        acc[...] += _dot(dup_ref[...], wu_ref[0], NT)

        @pl.when(f == nf - 1)
        def _():
            dh1 = acc[...] + ALPHA * dr2_ref[...].astype(F32)
            xhat, rstd = _ln_fwd(r1_ref[...])
            dg1_ref[0:1, :] += _colsum(dh1 * xhat)
            db1_ref[0:1, :] += _colsum(dh1)
            dr1_ref[...] = _ln_bwd(dh1, xhat, rstd, g1_ref[...]).astype(dr1_ref.dtype)

    row = pl.BlockSpec((tm, d), lambda i, f: (i, 0))
    acc8 = pl.BlockSpec((SUBLANES, d), lambda i, f: (0, 0))
    return pl.pallas_call(
        body, name="mlp_bwd",
        grid=(t // tm, nf),
        in_specs=[row, pl.BlockSpec((tm, tf), lambda i, f: (i, f)), row,
                  pl.BlockSpec((1, d, tf), lambda i, f: (f, 0, 0)), pl.BlockSpec((tf, d), lambda i, f: (f, 0)),
                  pl.BlockSpec((1, d), lambda i, f: (0, 0))],
        out_specs=[pl.BlockSpec((tm, tf), lambda i, f: (i, f)), row, acc8, acc8],
        out_shape=[jax.ShapeDtypeStruct((t, ff), BF16), jax.ShapeDtypeStruct((t, d), BF16),
                   jax.ShapeDtypeStruct((SUBLANES, d), F32), jax.ShapeDtypeStruct((SUBLANES, d), F32)],
        scratch_shapes=[pltpu.VMEM((tm, d), F32)],
        compiler_params=_params(("arbitrary", "arbitrary")),
    )(dr2, up, r1, w_up, w_down, ln1_g)


def _dx_kernel(segs, w_main, ddt, w_dt, dr1, tm, tk, comm=None):
    t, d = dr1.shape
    nblk = [s.shape[1] // tk for s in segs]
    starts = [sum(nblk[:i]) for i in range(len(segs))]
    nk = sum(nblk)
    nseg = len(segs)

    def body(*refs):
        seg_refs = refs[:nseg]
        w_ref, ddt_ref, wdt_ref, dr1_ref, o_ref, acc = refs[nseg:]
        k = pl.program_id(1)

        @pl.when(k == 0)
        def _():
            acc[...] = ALPHA * dr1_ref[...].astype(F32) + _dot(ddt_ref[...], wdt_ref[...], NT)

        for si in range(nseg):
            @pl.when((k >= starts[si]) & (k < starts[si] + nblk[si]))
            def _(si=si):
                acc[...] += _dot(seg_refs[si][...], w_ref[...], NT)

        @pl.when(k == nk - 1)
        def _():
            o_ref[...] = acc[...]

    def seg_spec(si):
        return pl.BlockSpec((tm, tk), lambda i, k: (i, jnp.clip(k - starts[si], 0, nblk[si] - 1)))

    row = pl.BlockSpec((tm, d), lambda i, k: (i, 0))
    grid = (t // tm, nk)
    c_in, c_in_specs, c_out_specs, c_out_shapes = _comm_specs(comm)
    return pl.pallas_call(
        _fuse_comm(body, grid, nseg + 4, 1, comm), name="dx",
        grid=grid,
        in_specs=[seg_spec(si) for si in range(nseg)] + [
            pl.BlockSpec((d, tk), lambda i, k: (0, k)), pl.BlockSpec((tm, LANES), lambda i, k: (i, 0)),
            pl.BlockSpec((d, LANES), lambda i, k: (0, 0)), row] + c_in_specs,
        out_specs=[row] + c_out_specs,
        out_shape=[jax.ShapeDtypeStruct((t, d), F32)] + c_out_shapes,
        scratch_shapes=[pltpu.VMEM((tm, d), F32)] + (list(comm.scratch) if comm else []),
        compiler_params=_params(("arbitrary", "arbitrary")),
    )(*segs, w_main, ddt, w_dt, dr1, *c_in)


def _dims(d):
    inner = 2 * d
    heads = inner // HEAD_DIM
    cd = inner + 2 * GROUPS * STATE
    assert heads <= LANES and inner % (GROUPS * LANES) == 0 and d % (len(POOL_WINDOWS) * LANES) == 0
    o_z, o_xbc, o_dt, o_lg = d, d + inner, d + inner + cd, d + inner + cd + heads
    return inner, heads, cd, (o_z, o_xbc, o_dt, o_lg)


def _row(v, width=None):
    v = v.reshape(1, -1).astype(F32)
    if width is not None and v.shape[1] < width:
        v = jnp.pad(v, ((0, 0), (0, width - v.shape[1])))
    return v


def _local_step(x2, tgt2, w, shards, core, bl):
    t, d = x2.shape
    inner, heads, cd, _ = _dims(d)
    gs = GROUPS * STATE
    nc = t // bl // CHUNK
    w_main, w_dt = _w_in_internal(w["w_in_blocks"], d)
    c_z, c_lg, c_u = cd, cd + inner, cd + inner + 2 * d
    conv_w8 = jnp.pad(w["conv_w"].astype(F32), ((0, SUBLANES - CONV_K), (0, 0)))
    conv_b = _row(w["conv_b"])
    dtb, alog = _row(w["dt_bias"], LANES), _row(w["a_log"], LANES)
    dskip_x = _row(jnp.repeat(w["d_skip"].reshape(-1), HEAD_DIM))
    normw = _row(w["ssd_norm_w"])
    col_head = lax.broadcasted_iota(jnp.int32, (LANES, inner), 1) // HEAD_DIM
    emat = (col_head == lax.broadcasted_iota(jnp.int32, (LANES, inner), 0)).astype(BF16)
    emat_t = emat.T
    w_main, w_dt = w_main.astype(BF16), w_dt.astype(BF16)
    b_gates, pool_scale = _row(w["b_gates"]), _row(w["pool_scale"])
    ln1_g, ln1_b, ln2_g, ln2_b = _row(w["ln1_g"]), _row(w["ln1_b"]), _row(w["ln2_g"]), _row(w["ln2_b"])

    tm = min(512, t)
    tk = min(1024, d)
    ct = min(512, d)
    rt = min(512, t // bl)
    nct = t // bl // rt
    mm = functools.partial(_matmul, bm=1024, bn=tk, bk=1024)
    xb = x2.astype(BF16)

    proj, *gathered = mm(xb, w_main, "nn", BF16, name="in_proj", comm=_all_gather_comm([shards[n] for n in OTHERS]))
    gathered = dict(zip(OTHERS, gathered))
    w_ssd, w_out, w_down = (gathered[n].reshape(-1, d) for n in ("w_ssd_proj", "w_out", "w_down"))
    w_up = gathered["w_up"]
    npg = len(POOL_WINDOWS)
    cg = d // npg
    wpg = gathered["w_pool_group"].reshape(N_DEV, npg, cg // N_DEV, cg).transpose(1, 0, 2, 3).reshape(npg, cg, cg)
    dt_raw = mm(xb, w_dt, "nn", F32, name="in_proj_dt")
    xbc = _conv_fwd(proj, conv_w8, conv_b, nct, cd, ct, rt)
    y, yn, states = _ssd_fwd(xbc, proj, dt_raw, dtb, alog, dskip_x, normw, emat, bl, inner, c_z)
    yssd = mm(yn, w_ssd, "nn", BF16, name="ssd_proj")
    ypr = _pool_fwd(proj, wpg, bl, d, c_u)
    merged, r1 = _merge_fwd(proj, ypr, yssd, x2, w_out, b_gates, pool_scale, d, c_lg, tm)
    tmm = min(1024, t)
    up, h1, dr2, loss8, dg2, db2 = _mlp_fwd(r1, tgt2, w_up, w_down, ln1_g, ln1_b, ln2_g, ln2_b, tmm)

    dup, dr1, dg1, db1 = _mlp_bwd(dr2, up, r1, w_up, w_down, ln1_g, tmm)
    relu2 = lambda v: jnp.square(jnp.maximum(v, 0.0))
    g = {}
    g["w_down"] = mm(up, dr2, "tn", BF16, name="dw_down", a_fn=relu2)
    g["w_up"] = _matmul(h1, dup, "tn", BF16, bm=1024, bn=w_up.shape[2], bk=1024, name="dw_up", col_blocks=N_DEV)
    g["w_out"] = mm(merged, dr1, "tn", BF16, name="dw_out")
    dlg, dyp, dys, dbg, dps = _merge_bwd(dr1, proj, ypr, yssd, w_out, b_gates, pool_scale, d, c_lg, tm)
    du, dwpg = _pool_bwd(proj, dyp, wpg, bl, d, c_u)
    g["w_pool_group"] = dwpg.reshape(npg, N_DEV, cg // N_DEV, cg).transpose(1, 0, 2, 3).reshape(
        N_DEV, npg * cg // N_DEV, cg).astype(BF16)
    dyn = mm(dys, w_ssd, "nt", BF16, name="d_ssd_proj")
    g["w_ssd_proj"] = mm(yn, dys, "tn", BF16, name="dw_ssd_proj")

    def chip_sums(names, tag):
        parts = [g.pop(n).reshape((N_DEV,) + shards_2d[n]) for n in names]
        recv = _run_comm(_rs_sibling_comm(parts), "rs_sibling_" + tag)
        return [_add_pairs(core, p, r, "rs_add_" + n) for n, p, r in zip(names, parts, recv)]

    shards_2d = {n: s.shape for n, s in shards.items()}
    shards_2d["w_in"] = w["w_in_blocks"].shape[1:]
    dxs, dbm, dcm, dz, ddt, dnw, dsk, dalog, ddtb, *recv_others = _ssd_bwd(
        xbc, proj, dt_raw, y, dyn, states, dtb, alog, dskip_x, normw, emat, emat_t, bl, inner, c_z,
        comm=_rs_chips_comm(chip_sums(OTHERS, "a")))
    dxs_p, dcw_x, dcb_x = _conv_bwd(proj, dxs, conv_w8, conv_b, nct, 0, inner, ct, rt, "conv_bwd_x")
    dbm_p, dcw_b, dcb_b = _conv_bwd(proj, dbm, conv_w8, conv_b, nct, inner, gs, ct, rt, "conv_bwd_b")
    dcm_p, dcw_c, dcb_c = _conv_bwd(proj, dcm, conv_w8, conv_b, nct, inner + gs, gs, ct, rt, "conv_bwd_c")
    segs = [dxs_p, dbm_p, dcm_p, dz, dlg, du]
    keys = [k for k, _, _ in _col_segments(d)]
    dws = {k: mm(xb, s, "tn", BF16, name="dw_in_" + k) for k, s in zip(keys, segs + [ddt])}
    g["w_in"] = _w_in_grad_blocks(dws, d, w["w_in_blocks"].shape[2])
    grad_x, recv_w_in = _dx_kernel(segs, w_main, ddt, w_dt, dr1, tm, tk, comm=_rs_chips_comm(chip_sums(["w_in"], "b")))
    recv = dict(zip(OTHERS, recv_others))
    recv["w_in"] = recv_w_in
    g["conv_w"] = jnp.concatenate([dcw_x, dcw_b, dcw_c], axis=1)[:CONV_K]
    g["conv_b"] = jnp.concatenate([dcb_x, dcb_b, dcb_c], axis=1)[0]
    g["b_gates"], g["pool_scale"] = dbg[0], dps[0]
    g["dt_bias"], g["a_log"], g["d_skip"] = ddtb[0, :heads], dalog[0, :heads], dsk[0, :heads]
    g["ssd_norm_w"] = dnw[:, 0, :].reshape(inner)
    g["ln1_g"], g["ln1_b"], g["ln2_g"], g["ln2_b"] = dg1[0], db1[0], dg2[0], db2[0]
    return loss8, grad_x, g, recv


BIG = ("w_in", "w_ssd_proj", "w_pool_group", "w_out", "w_up", "w_down")
OTHERS = BIG[1:]
SMALL = ("b_gates", "conv_b", "dt_bias", "a_log", "d_skip", "ssd_norm_w", "pool_scale", "ln1_g", "ln1_b", "ln2_g",
         "ln2_b")
SMALL_PACK = SMALL + ("conv_w",)
NAMES = ("w_in", "b_gates", "conv_w", "conv_b", "dt_bias", "a_log", "d_skip", "ssd_norm_w", "w_ssd_proj",
         "w_pool_group", "pool_scale", "w_out", "ln1_g", "ln1_b", "w_up", "w_down", "ln2_g", "ln2_b")


def _size(shape):
    n = 1
    for s in shape:
        n *= s
    return n


def _rows128(v):
    v = v.astype(F32).reshape((-1, v.shape[-1]))
    n = v.shape[-1]
    v = jnp.pad(v, ((0, 0), (0, -n % LANES)))
    return v.reshape(-1, LANES)


def _pack_small(vals, extra):
    parts = [_rows128(vals[n]) for n in SMALL_PACK]
    parts.append(jnp.pad(extra.reshape(1, 1).astype(F32), ((0, 0), (0, LANES - 1))))
    buf = jnp.concatenate(parts, axis=0)
    return jnp.pad(buf, ((0, -buf.shape[0] % SUBLANES), (0, 0)))


def _unpack_small(buf, shapes):
    out, off = {}, 0
    for n in SMALL_PACK:
        lead, last = _size(shapes[n][:-1]), shapes[n][-1]
        per = -(-last // LANES)
        out[n] = buf[off:off + lead * per].reshape(lead, per * LANES)[:, :last].reshape(shapes[n])
        off += lead * per
    return out, buf[off, 0]


def _col_segments(d):
    inner, heads, cd, (o_z, o_xbc, o_dt, o_lg) = _dims(d)
    gs = GROUPS * STATE
    return [("xs", o_xbc, inner), ("B", o_xbc + inner, gs), ("C", o_xbc + inner + gs, gs), ("z", o_z, inner),
            ("lg", o_lg, 2 * d), ("u", 0, d), ("dt", o_dt, heads)]


def _cols_from_blocks(blocks, start, width, bw):
    parts, pos = [], start
    while pos < start + width:
        k, off = divmod(pos, bw)
        n = min(bw - off, start + width - pos)
        parts.append(blocks[k][:, off:off + n])
        pos += n
    return parts


def _w_in_internal(blocks, d):
    bw = blocks.shape[2]
    segs = _col_segments(d)
    heads = segs[-1][2]
    main = [p for _, s, w_ in segs[:-1] for p in _cols_from_blocks(blocks, s, w_, bw)]
    w_dt = jnp.concatenate(_cols_from_blocks(blocks, segs[-1][1], heads, bw), axis=1)
    return jnp.concatenate(main, axis=1), jnp.pad(w_dt, ((0, 0), (0, LANES - heads)))


def _w_in_grad_blocks(dws, d, bw):
    order = sorted(_col_segments(d), key=lambda s: s[1])
    blocks = []
    for k in range(N_DEV):
        lo, hi, parts = k * bw, (k + 1) * bw, []
        for key, s, w_ in order:
            a, b = max(lo, s), min(hi, s + w_)
            if a < b:
                parts.append(dws[key][:, a - s:b - s])
        blocks.append(jnp.concatenate(parts, axis=1))
    return jnp.stack(blocks)


def _mesh_pos():
    return lax.axis_index("x"), lax.axis_index("y"), lax.axis_index("c")


def _all_gather_comm(shards):
    nw = len(shards)

    def setup(x_refs, out_refs, scr):
        send_sems, recv_sems, local_sems = scr
        x, y, c = _mesh_pos()
        me, sibling = (x, y, c), (x, y, 1 - c)
        chips = [(1 - x, y), (x, 1 - y), (1 - x, 1 - y)]

        def copy(wi, k, block, to, from_input=False):
            px, py, pc = block
            blk = out_refs[wi].at[4 * px + 2 * py + pc]
            return pltpu.make_async_remote_copy(
                src_ref=x_refs[wi] if from_input else blk, dst_ref=blk,
                send_sem=send_sems.at[7 * wi + k], recv_sem=recv_sems.at[7 * wi + k], device_id=to,
                device_id_type=MESH)

        mine = [pltpu.make_async_copy(x_refs[wi], out_refs[wi].at[4 * x + 2 * y + c], local_sems.at[wi])
                for wi in range(nw)]
        sends = []
        for wi in range(nw):
            sends.append(copy(wi, 0, me, sibling, True))
            sends += [copy(wi, 1 + j, me, (*chip, c), True) for j, chip in enumerate(chips)]
        return copy, mine, sends, me, sibling, chips, c

    def start(x_refs, out_refs, scr):
        _, mine, sends, _, _, _, _ = setup(x_refs, out_refs, scr)
        for cp in mine + sends:
            cp.start()

    def wait(x_refs, out_refs, scr):
        copy, mine, sends, me, sibling, chips, c = setup(x_refs, out_refs, scr)
        passed = []
        for wi in range(nw):
            for j, chip in enumerate(chips):
                copy(wi, 1 + j, (*chip, c), me).wait_recv()
                passed.append(copy(wi, 4 + j, (*chip, c), sibling))
                passed[-1].start()
        for wi in range(nw):
            copy(wi, 0, sibling, me).wait_recv()
            for j, chip in enumerate(chips):
                copy(wi, 4 + j, (*chip, 1 - c), me).wait_recv()
        for cp in sends + passed:
            cp.wait_send()
        for cp in mine:
            cp.wait()

    return _Comm(
        inputs=list(shards),
        out_shapes=[jax.ShapeDtypeStruct((N_DEV,) + s.shape, s.dtype) for s in shards],
        scratch=[pltpu.SemaphoreType.DMA((7 * nw,)), pltpu.SemaphoreType.DMA((7 * nw,)),
                 pltpu.SemaphoreType.DMA((nw,))],
        start=start, wait=wait)


def _rs_sibling_comm(parts):
    nw = len(parts)
    half = N_DEV // 2

    def copies(p_refs, recv_refs, scr):
        send_sems, recv_sems = scr
        x, y, c = _mesh_pos()
        return [pltpu.make_async_remote_copy(
            src_ref=p_refs[wi].at[2 * q + 1 - c], dst_ref=recv_refs[wi].at[q],
            send_sem=send_sems.at[half * wi + q], recv_sem=recv_sems.at[half * wi + q],
            device_id=(x, y, 1 - c), device_id_type=MESH) for wi in range(nw) for q in range(half)]

    def start(p_refs, recv_refs, scr):
        for cp in copies(p_refs, recv_refs, scr):
            cp.start()

    def wait(p_refs, recv_refs, scr):
        for cp in copies(p_refs, recv_refs, scr):
            cp.wait()

    return _Comm(
        inputs=list(parts),
        out_shapes=[jax.ShapeDtypeStruct((half,) + p.shape[1:], p.dtype) for p in parts],
        scratch=[pltpu.SemaphoreType.DMA((half * nw,)), pltpu.SemaphoreType.DMA((half * nw,))],
        start=start, wait=wait)


def _rs_chips_comm(tbs):
    nw = len(tbs)

    def copies(t_refs, o_refs, scr):
        send_sems, recv_sems, local_sems = scr
        x, y, c = _mesh_pos()
        p = 2 * x + y
        chips = [(1 - x, y), (x, 1 - y), (1 - x, 1 - y)]
        own = [pltpu.make_async_copy(t_refs[wi].at[p], o_refs[wi].at[p], local_sems.at[wi]) for wi in range(nw)]
        remote = [pltpu.make_async_remote_copy(
            src_ref=t_refs[wi].at[2 * qx + qy], dst_ref=o_refs[wi].at[p], send_sem=send_sems.at[3 * wi + j],
            recv_sem=recv_sems.at[3 * wi + j], device_id=(qx, qy, c), device_id_type=MESH)
            for wi in range(nw) for j, (qx, qy) in enumerate(chips)]
        arriving = [pltpu.make_async_remote_copy(
            src_ref=t_refs[wi].at[p], dst_ref=o_refs[wi].at[2 * qx + qy], send_sem=send_sems.at[3 * wi + j],
            recv_sem=recv_sems.at[3 * wi + j], device_id=(qx, qy, c), device_id_type=MESH)
            for wi in range(nw) for j, (qx, qy) in enumerate(chips)]
        return own, remote, arriving

    def start(t_refs, o_refs, scr):
        own, remote, _ = copies(t_refs, o_refs, scr)
        for cp in own + remote:
            cp.start()

    def wait(t_refs, o_refs, scr):
        own, remote, arriving = copies(t_refs, o_refs, scr)
        for cp in arriving:
            cp.wait_recv()
        for cp in remote:
            cp.wait_send()
        for cp in own:
            cp.wait()

    return _Comm(
        inputs=list(tbs),
        out_shapes=[jax.ShapeDtypeStruct(t_.shape, t_.dtype) for t_ in tbs],
        scratch=[pltpu.SemaphoreType.DMA((3 * nw,)), pltpu.SemaphoreType.DMA((3 * nw,)),
                 pltpu.SemaphoreType.DMA((nw,))],
        start=start, wait=wait)


def _row_tile(rows, cap=256):
    if rows <= cap:
        return rows
    return max(t_ for t_ in range(SUBLANES, cap + 1, SUBLANES) if rows % t_ == 0)


def _add_pairs(core, part, recv, name):
    n, r, c_ = recv.shape
    tr = _row_tile(r)

    def body(core_ref, a_ref, b_ref, o_ref):
        o_ref[...] = (a_ref[...].astype(F32) + b_ref[...].astype(F32)).astype(o_ref.dtype)

    spec = pl.BlockSpec((1, tr, c_), lambda q, i, core_ref: (q, i, 0))
    return pl.pallas_call(
        body, name=name,
        grid_spec=pltpu.PrefetchScalarGridSpec(
            num_scalar_prefetch=1, grid=(n, r // tr),
            in_specs=[pl.BlockSpec((1, tr, c_), lambda q, i, core_ref: (2 * q + core_ref[0], i, 0)), spec],
            out_specs=spec),
        out_shape=jax.ShapeDtypeStruct(recv.shape, BF16), compiler_params=_params(("parallel", "parallel")),
    )(core, part, recv)


def _small_allreduce(vec, name):
    rows = vec.shape[0]

    def body(x_ref, o_ref, buf, send_sems, recv_sems):
        x, y, c = _mesh_pos()
        me = 4 * x + 2 * y + c
        buf[me] = x_ref[...]
        cps = []
        for k in range(1, N_DEV):
            peer = (1 - x if k & 4 else x, 1 - y if k & 2 else y, 1 - c if k & 1 else c)
            cps.append(pltpu.make_async_remote_copy(
                src_ref=x_ref, dst_ref=buf.at[me], send_sem=send_sems.at[k - 1], recv_sem=recv_sems.at[k - 1],
                device_id=peer, device_id_type=MESH))
        for cp in cps:
            cp.start()
        for k in range(1, N_DEV):
            px, py, pc = (1 - x if k & 4 else x, 1 - y if k & 2 else y, 1 - c if k & 1 else c)
            pltpu.make_async_remote_copy(
                src_ref=x_ref, dst_ref=buf.at[4 * px + 2 * py + pc], send_sem=send_sems.at[k - 1],
                recv_sem=recv_sems.at[k - 1], device_id=(px, py, pc), device_id_type=MESH).wait_recv()
        for cp in cps:
            cp.wait_send()
        acc = buf[0]
        for k in range(1, N_DEV):
            acc = acc + buf[k]
        o_ref[...] = acc

    vm = pl.BlockSpec(memory_space=pltpu.VMEM)
    return pl.pallas_call(
        body, name=name,
        in_specs=[vm], out_specs=vm,
        out_shape=jax.ShapeDtypeStruct(vec.shape, F32),
        scratch_shapes=[pltpu.VMEM((N_DEV, rows, LANES), F32), pltpu.SemaphoreType.DMA((N_DEV - 1,)),
                        pltpu.SemaphoreType.DMA((N_DEV - 1,))],
    )(vec)


def _adamw(gparts, w, m, v, name):
    n, r, c_ = gparts.shape
    tr = _row_tile(r)
    c1 = 1.0 / (1.0 - B1 ** STEP)
    c2 = 1.0 / (1.0 - B2 ** STEP)

    def body(g_ref, w_ref, m_ref, v_ref, go_ref, d_ref, mo_ref, vo_ref):
        g = g_ref[0].astype(F32)
        for q in range(1, n):
            g = g + g_ref[q].astype(F32)
        mn = B1 * m_ref[...] + (1.0 - B1) * g
        vn = B2 * v_ref[...] + (1.0 - B2) * (g * g)
        go_ref[...] = g
        mo_ref[...] = mn
        vo_ref[...] = vn
        d_ref[...] = -LR * ((mn * c1) / (jnp.sqrt(vn * c2) + ADAM_EPS) + WD * w_ref[...])

    spec = pl.BlockSpec((tr, c_), lambda i: (i, 0))
    out = jax.ShapeDtypeStruct((r, c_), F32)
    return pl.pallas_call(
        body, name=name, grid=(r // tr,),
        in_specs=[pl.BlockSpec((n, tr, c_), lambda i: (0, i, 0)), spec, spec, spec],
        out_specs=[spec] * 4, out_shape=[out] * 4, compiler_params=_params(("parallel",)),
    )(gparts, w, m, v)


def kernel(x, w_in, b_gates, conv_w, conv_b, dt_bias, a_log, d_skip, ssd_norm_w, w_ssd_proj, w_pool_group, pool_scale, w_out, ln1_g, ln1_b, w_up, w_down, ln2_g, ln2_b, loss_target, m_w_in, m_b_gates, m_conv_w, m_conv_b, m_dt_bias, m_a_log, m_d_skip, m_ssd_norm_w, m_w_ssd_proj, m_w_pool_group, m_pool_scale, m_w_out, m_ln1_g, m_ln1_b, m_w_up, m_w_down, m_ln2_g, m_ln2_b, v_w_in, v_b_gates, v_conv_w, v_conv_b, v_dt_bias, v_a_log, v_d_skip, v_ssd_norm_w, v_w_ssd_proj, v_w_pool_group, v_pool_scale, v_w_out, v_ln1_g, v_ln1_b, v_w_up, v_w_down, v_ln2_g, v_ln2_b):
    ws = (w_in, b_gates, conv_w, conv_b, dt_bias, a_log, d_skip, ssd_norm_w, w_ssd_proj, w_pool_group, pool_scale,
          w_out, ln1_g, ln1_b, w_up, w_down, ln2_g, ln2_b)
    ms = (m_w_in, m_b_gates, m_conv_w, m_conv_b, m_dt_bias, m_a_log, m_d_skip, m_ssd_norm_w, m_w_ssd_proj,
          m_w_pool_group, m_pool_scale, m_w_out, m_ln1_g, m_ln1_b, m_w_up, m_w_down, m_ln2_g, m_ln2_b)
    vs = (v_w_in, v_b_gates, v_conv_w, v_conv_b, v_dt_bias, v_a_log, v_d_skip, v_ssd_norm_w, v_w_ssd_proj,
          v_w_pool_group, v_pool_scale, v_w_out, v_ln1_g, v_ln1_b, v_w_up, v_w_down, v_ln2_g, v_ln2_b)
    w = {n: a[0] for n, a in zip(NAMES, ws)}
    m = {n: a[0] for n, a in zip(NAMES, ms)}
    v = {n: a[0] for n, a in zip(NAMES, vs)}
    out_shapes = {n: a.shape for n, a in zip(NAMES, ws)}
    bl, s, d = x.shape
    x2, tgt2 = x.reshape(bl * s, d), loss_target.reshape(bl * s, d)
    xi, yi, ci = _mesh_pos()
    me = 4 * xi + 2 * yi + ci
    zero = jnp.zeros((), F32)
    shapes = {n: w[n].shape for n in NAMES}
    shape2d = {n: (_size(shapes[n][:-1]), shapes[n][-1]) for n in BIG}
    cwl = shapes["conv_w"][1]

    conv_place = lax.dynamic_update_slice(jnp.zeros((CONV_K, N_DEV * cwl), F32), w["conv_w"], (0, me * cwl))
    conv_full = _small_allreduce(_rows128(conv_place), "gather_conv_w")
    conv_full = conv_full.reshape(CONV_K, N_DEV * cwl)

    shards = {n: w[n].astype(BF16).reshape(shape2d[n]) for n in BIG}
    full = {n: w[n] for n in SMALL}
    full["conv_w"] = conv_full
    full["w_in_blocks"] = _run_comm(_all_gather_comm([shards.pop("w_in")]), "all_gather_w_in")[0]
    loss8, grad_x, g, recv = _local_step(x2, tgt2, full, shards, ci.astype(jnp.int32).reshape(1), bl)

    small_sum = _small_allreduce(_pack_small(g, loss8[0, 0]), "small_allreduce")
    ex_shapes = {n: shapes[n] for n in SMALL}
    ex_shapes["conv_w"] = (CONV_K, N_DEV * cwl)
    gsum, loss = _unpack_small(small_sum, ex_shapes)
    gsum["conv_w"] = lax.dynamic_slice(gsum["conv_w"], (0, me * cwl), (CONV_K, cwl))
    gs_pk = _pack_small(gsum, zero)
    ws_pk, ms_pk, vs_pk = (_pack_small(t_, zero) for t_ in (w, m, v))
    small_out = _adamw(gs_pk[None], ws_pk, ms_pk, vs_pk, "adamw_small")
    loc_shapes = {n: shapes[n] for n in SMALL_PACK}
    res = [_unpack_small(o, loc_shapes)[0] for o in small_out]

    for n in BIG:
        outs = _adamw(recv[n], *(t_[n].reshape(shape2d[n]) for t_ in (w, m, v)), "adamw_" + n)
        for r_, o in zip(res, outs):
            r_[n] = o

    def ordered(r_):
        return [r_[n].reshape(out_shapes[n]) for n in NAMES]

    return (loss, grad_x.reshape(bl, s, d), *ordered(res[0]), *ordered(res[1]), *ordered(res[2]), *ordered(res[3]))
```

```python
import collections
import functools

import jax
import jax.numpy as jnp
from jax import lax
from jax.experimental import pallas as pl
from jax.experimental.pallas import tpu as pltpu

F32 = jnp.float32
BF16 = jnp.bfloat16
MESH = pl.DeviceIdType.MESH

HEAD_DIM = 64
STATE = 128
GROUPS = 8
CONV_K = 4
CHUNK = 256
POOL_WINDOWS = (2, 4, 8, 16)
ALPHA = 2.0 ** 0.25
LN_EPS = 1e-5
RMS_EPS = 1e-5
LR, B1, B2, ADAM_EPS, WD, STEP = 0.001, 0.9, 0.999, 1e-08, 0.01, 10
N_DEV = 8
LANES = 128
SUBLANES = 8
VMEM_LIMIT = 56 * 1024 * 1024
NEG_BIG = -1e30

NN = (((1,), (0,)), ((), ()))
NT = (((1,), (1,)), ((), ()))
TN = (((0,), (0,)), ((), ()))


def _dot(a, b, dims=NN):
    return lax.dot_general(a.astype(BF16), b.astype(BF16), dims, preferred_element_type=F32)


def _dot_exact01(q, e, dims=NN):
    hi = q.astype(BF16)
    r1 = q - hi.astype(F32)
    mid = r1.astype(BF16)
    lo = (r1 - mid.astype(F32)).astype(BF16)
    f = lambda p: lax.dot_general(p, e, dims, preferred_element_type=F32)
    return f(hi) + f(mid) + f(lo)


def _params(sem):
    return pltpu.CompilerParams(dimension_semantics=sem, vmem_limit_bytes=VMEM_LIMIT)


def _sigmoid(x):
    return 1.0 / (1.0 + jnp.exp(-x))


def _colsum(x):
    return jnp.sum(x, axis=0, keepdims=True)


def _ln_fwd(r):
    mu = jnp.mean(r, axis=-1, keepdims=True)
    xc = r - mu
    var = jnp.mean(xc * xc, axis=-1, keepdims=True)
    rstd = lax.rsqrt(var + LN_EPS)
    return xc * rstd, rstd


def _ln_bwd(dy, xhat, rstd, g):
    dxh = dy * g
    m1 = jnp.mean(dxh, axis=-1, keepdims=True)
    m2 = jnp.mean(dxh * xhat, axis=-1, keepdims=True)
    return rstd * (dxh - m1 - xhat * m2)


_Comm = collections.namedtuple("_Comm", "inputs out_shapes scratch start wait")
ANY = pl.BlockSpec(memory_space=pl.ANY)


def _fuse_comm(body, grid, n_in, n_out, comm):
    if comm is None:
        return body
    ci, co = len(comm.inputs), len(comm.out_shapes)

    def fused(*refs):
        ins, cins = refs[:n_in], refs[n_in:n_in + ci]
        o0 = n_in + ci
        outs, couts = refs[o0:o0 + n_out], refs[o0 + n_out:o0 + n_out + co]
        rest = refs[o0 + n_out + co:]
        scr, cscr = rest[:len(rest) - len(comm.scratch)], rest[len(rest) - len(comm.scratch):]
        ids = [pl.program_id(a) for a in range(len(grid))]
        first, last = ids[0] == 0, ids[0] == grid[0] - 1
        for a in range(1, len(grid)):
            first, last = first & (ids[a] == 0), last & (ids[a] == grid[a] - 1)

        @pl.when(first)
        def _():
            comm.start(cins, couts, cscr)

        body(*ins, *outs, *scr)

        @pl.when(last)
        def _():
            comm.wait(cins, couts, cscr)

    return fused


def _comm_specs(comm):
    if comm is None:
        return [], [], [], []
    return list(comm.inputs), [ANY] * len(comm.inputs), [ANY] * len(comm.out_shapes), list(comm.out_shapes)


def _run_comm(comm, name):
    ci, co = len(comm.inputs), len(comm.out_shapes)

    def body(*refs):
        comm.start(refs[:ci], refs[ci:ci + co], refs[ci + co:])
        comm.wait(refs[:ci], refs[ci:ci + co], refs[ci + co:])

    return pl.pallas_call(body, name=name, in_specs=[ANY] * ci, out_specs=[ANY] * co, out_shape=list(comm.out_shapes),
                          scratch_shapes=list(comm.scratch))(*comm.inputs)


def _matmul(a, b, mode, out_dtype, bm, bn, bk, name, a_fn=None, col_blocks=0, comm=None):
    if mode == "nn":
        (m, k), n, dims = a.shape, b.shape[1], NN
    elif mode == "nt":
        (m, k), n, dims = a.shape, b.shape[0], NT
    else:
        (k, m), n, dims = a.shape, b.shape[1], TN
    bm, bn, bk = min(bm, m), min(bn, n), min(bk, k)
    assert m % bm == 0 and n % bn == 0 and k % bk == 0, (name, m, n, k, bm, bn, bk)
    nk = k // bk
    if mode == "nn":
        a_spec = pl.BlockSpec((bm, bk), lambda i, j, kk: (i, kk))
        b_spec = pl.BlockSpec((bk, bn), lambda i, j, kk: (kk, j))
    elif mode == "nt":
        a_spec = pl.BlockSpec((bm, bk), lambda i, j, kk: (i, kk))
        b_spec = pl.BlockSpec((bn, bk), lambda i, j, kk: (j, kk))
    else:
        a_spec = pl.BlockSpec((bk, bm), lambda i, j, kk: (kk, i))
        b_spec = pl.BlockSpec((bk, bn), lambda i, j, kk: (kk, j))

    def body(a_ref, b_ref, o_ref, acc_ref):
        kk = pl.program_id(2)

        @pl.when(kk == 0)
        def _():
            acc_ref[...] = jnp.zeros_like(acc_ref)

        av = a_ref[...]
        if a_fn is not None:
            av = a_fn(av.astype(F32))
        acc_ref[...] += _dot(av, b_ref[...], dims)

        @pl.when(kk == nk - 1)
        def _():
            o_ref[...] = acc_ref[...].astype(o_ref.dtype).reshape(o_ref.shape)

    if col_blocks:
        assert n == col_blocks * bn, (name, n, col_blocks, bn)
        out_spec = pl.BlockSpec((1, bm, bn), lambda i, j, kk: (j, i, 0))
        out_shape = jax.ShapeDtypeStruct((col_blocks, m, bn), out_dtype)
    else:
        out_spec = pl.BlockSpec((bm, bn), lambda i, j, kk: (i, j))
        out_shape = jax.ShapeDtypeStruct((m, n), out_dtype)
    grid = (m // bm, n // bn, nk)
    c_in, c_in_specs, c_out_specs, c_out_shapes = _comm_specs(comm)
    res = pl.pallas_call(
        _fuse_comm(body, grid, 2, 1, comm), name=name,
        grid=grid,
        in_specs=[a_spec, b_spec] + c_in_specs,
        out_specs=[out_spec] + c_out_specs,
        out_shape=[out_shape] + c_out_shapes,
        scratch_shapes=[pltpu.VMEM((bm, bn), F32)] + (list(comm.scratch) if comm else []),
        compiler_params=_params(("arbitrary",) * 3 if comm else ("parallel", "parallel", "arbitrary")),
    )(a, b, *c_in)
    return res if comm else res[0]


CONV_STRIP = 16


def _conv_pre(ext_ref, w_ref, b_ref, r0, rows):
    acc = b_ref[...] + w_ref[0:1, :] * ext_ref[pl.ds(r0 + SUBLANES - (CONV_K - 1), rows), :]
    for k in range(1, CONV_K):
        acc = acc + w_ref[k:k + 1, :] * ext_ref[pl.ds(r0 + SUBLANES - (CONV_K - 1) + k, rows), :]
    return acc


def _in_proj(xb, w_main, conv_w8, conv_b, cd, seq_len, bm, bn, comm):
    t, d = xb.shape
    pw = w_main.shape[1]
    bm, bn = min(bm, seq_len), min(bn, d)
    assert t % bm == 0 and seq_len % bm == 0 and pw % bn == 0 and cd % bn == 0
    ncj = cd // bn
    tiles_per_seq = seq_len // bm

    def body(x_ref, w_ref, cw_ref, cb_ref, p_ref, xbc_ref, ext_ref, carry_ref):
        i = pl.program_id(0)
        j = pl.program_id(1)
        pq = _dot(x_ref[...], w_ref[...]).astype(BF16)
        p_ref[...] = pq

        @pl.when(j < ncj)
        def _():
            first = (i % tiles_per_seq) == 0
            jc = jnp.minimum(j, ncj - 1)
            ext_ref[0:SUBLANES, :] = jnp.where(first, 0.0, carry_ref[jc])
            ext_ref[SUBLANES:, :] = pq.astype(F32)
            carry_ref[jc] = ext_ref[bm:bm + SUBLANES, :]
            for r0 in range(0, bm, CONV_STRIP):
                acc = _conv_pre(ext_ref, cw_ref, cb_ref, r0, CONV_STRIP)
                xbc_ref[r0:r0 + CONV_STRIP, :] = (acc * _sigmoid(acc)).astype(xbc_ref.dtype)

    grid = (t // bm, pw // bn)
    conv_col = lambda i, j: (0, jnp.minimum(j, ncj - 1))
    c_in, c_in_specs, c_out_specs, c_out_shapes = _comm_specs(comm)
    return pl.pallas_call(
        _fuse_comm(body, grid, 4, 2, comm), name="in_proj",
        grid=grid,
        in_specs=[pl.BlockSpec((bm, d), lambda i, j: (i, 0)), pl.BlockSpec((d, bn), lambda i, j: (0, j)),
                  pl.BlockSpec((SUBLANES, bn), conv_col), pl.BlockSpec((1, bn), conv_col)] + c_in_specs,
        out_specs=[pl.BlockSpec((bm, bn), lambda i, j: (i, j)),
                   pl.BlockSpec((bm, bn), lambda i, j: (i, jnp.minimum(j, ncj - 1)))] + c_out_specs,
        out_shape=[jax.ShapeDtypeStruct((t, pw), BF16), jax.ShapeDtypeStruct((t, cd), BF16)] + c_out_shapes,
        scratch_shapes=[pltpu.VMEM((bm + SUBLANES, bn), F32), pltpu.VMEM((ncj, SUBLANES, bn), F32)]
        + (list(comm.scratch) if comm else []),
        compiler_params=_params(("arbitrary", "arbitrary")),
    )(xb, w_main, conv_w8, conv_b, *c_in)


def _conv_bwd(proj, dxbc, conv_w8, conv_b, n_seq_chunks, col0, width, ct, L, name):
    t = proj.shape[0]
    nbc = t // L
    hb = L // SUBLANES
    ct = min(ct, width)
    assert col0 % ct == 0 and width % ct == 0
    cb0 = col0 // ct
    last_hb = t // SUBLANES - 1

    def body(x_ref, xb_ref, xa_ref, d_ref, da_ref, w_ref, b_ref, o_ref, dw_ref, db_ref, ext_ref, dc_ref):
        bc = pl.program_id(1)
        first = (bc % n_seq_chunks) == 0
        last = (bc % n_seq_chunks) == n_seq_chunks - 1

        @pl.when(bc == 0)
        def _():
            dw_ref[...] = jnp.zeros_like(dw_ref)
            db_ref[...] = jnp.zeros_like(db_ref)

        ext_ref[0:SUBLANES, :] = jnp.where(first, 0.0, xb_ref[...].astype(F32))
        ext_ref[SUBLANES:SUBLANES + L, :] = x_ref[...].astype(F32)
        ext_ref[SUBLANES + L:, :] = xa_ref[...].astype(F32)

        def dsilu_at(r0, rows):
            acc = _conv_pre(ext_ref, w_ref, b_ref, r0, rows)
            sg = _sigmoid(acc)
            return sg * (1.0 + acc * (1.0 - sg))

        for r0 in range(0, L, CONV_STRIP):
            dc_ref[r0:r0 + CONV_STRIP, :] = d_ref[r0:r0 + CONV_STRIP, :].astype(F32) * dsilu_at(r0, CONV_STRIP)
        dc_ref[L:, :] = jnp.where(last, 0.0, da_ref[...].astype(F32)) * dsilu_at(L, SUBLANES)
        fold = lambda v: v[0:SUBLANES] + v[SUBLANES:CONV_STRIP]
        dws = [jnp.zeros((SUBLANES, ct), F32) for _ in range(CONV_K)]
        dbs = jnp.zeros((SUBLANES, ct), F32)
        for r0 in range(0, L, CONV_STRIP):
            xq = ext_ref[r0 + SUBLANES:r0 + SUBLANES + CONV_STRIP, :]
            dx = jnp.zeros((CONV_STRIP, ct), F32)
            for k in range(CONV_K):
                dck = dc_ref[pl.ds(r0 + CONV_K - 1 - k, CONV_STRIP), :]
                dx = dx + w_ref[k:k + 1, :] * dck
                dws[k] = dws[k] + fold(dck * xq)
                if k == CONV_K - 1:
                    dbs = dbs + fold(dck)
            o_ref[r0:r0 + CONV_STRIP, :] = dx.astype(o_ref.dtype)
        for k in range(CONV_K):
            dw_ref[k:k + 1, :] += _colsum(dws[k])
        db_ref[0:1, :] += _colsum(dbs)

    return pl.pallas_call(
        body, name=name,
        grid=(width // ct, nbc),
        in_specs=[
            pl.BlockSpec((L, ct), lambda j, i: (i, cb0 + j)),
            pl.BlockSpec((SUBLANES, ct), lambda j, i: (jnp.maximum(i * hb - 1, 0), cb0 + j)),
            pl.BlockSpec((SUBLANES, ct), lambda j, i: (jnp.minimum((i + 1) * hb, last_hb), cb0 + j)),
            pl.BlockSpec((L, ct), lambda j, i: (i, j)),
            pl.BlockSpec((SUBLANES, ct), lambda j, i: (jnp.minimum((i + 1) * hb, last_hb), j)),
            pl.BlockSpec((SUBLANES, ct), lambda j, i: (0, cb0 + j)),
            pl.BlockSpec((1, ct), lambda j, i: (0, cb0 + j)),
        ],
        out_specs=[
            pl.BlockSpec((L, ct), lambda j, i: (i, j)),
            pl.BlockSpec((SUBLANES, ct), lambda j, i: (0, j)),
            pl.BlockSpec((SUBLANES, ct), lambda j, i: (0, j)),
        ],
        out_shape=[
            jax.ShapeDtypeStruct((t, width), BF16),
            jax.ShapeDtypeStruct((SUBLANES, width), F32),
            jax.ShapeDtypeStruct((SUBLANES, width), F32),
        ],
        scratch_shapes=[pltpu.VMEM((L + 2 * SUBLANES, ct), F32), pltpu.VMEM((L + SUBLANES, ct), F32)],
        compiler_params=_params(("parallel", "arbitrary")),
    )(proj, proj, proj, dxbc, dxbc, conv_w8, conv_b)


def _cumsum_rows(x, reverse=False):
    n = x.shape[0]
    row = lax.broadcasted_iota(jnp.int32, x.shape, 0)
    s = 1
    while s < n:
        if reverse:
            x = x + jnp.where(row < n - s, pltpu.roll(x, n - s, 0), 0.0)
        else:
            x = x + jnp.where(row >= s, pltpu.roll(x, s, 0), 0.0)
        s *= 2
    return x


def _ssd_scalars(dtr, dtb, alog):
    pre = dtr + dtb
    dt = jnp.maximum(pre, 0.0) + jnp.log(1.0 + jnp.exp(-jnp.abs(pre)))
    a = -jnp.exp(alog)
    acs = _cumsum_rows(dt * a) * LOG2E
    return pre, dt, a, acs


SSD_GROUPS_PER_STEP = 4
LOG2E = 1.4426950408889634


def _ssd_group_common(xs, dt_s, acs_s, e):
    L = xs.shape[0]
    dt_x = _dot_exact01(dt_s, e)
    acs_x = _dot_exact01(acs_s, e)
    e_x = jnp.exp2(acs_x)
    a_last = acs_x[L - 1:L, :]
    dec_x = jnp.exp2(a_last - acs_x)
    return dt_x, acs_x, e_x, dec_x


def _decay_matrix(acs_x, acs_t, r, tri):
    col = acs_x[:, r * HEAD_DIM:r * HEAD_DIM + 1]
    rowv = acs_t[r * HEAD_DIM:r * HEAD_DIM + 1, :]
    return jnp.exp2(jnp.where(tri, col - rowv, NEG_BIG))


def _head_mask(r, gw, dtype):
    lane = lax.broadcasted_iota(jnp.int32, (1, gw), 1)
    return ((lane >= r * HEAD_DIM) & (lane < (r + 1) * HEAD_DIM)).astype(dtype)


def _ssd_fwd(xbc, proj, dt_raw, dtb, alog, dskip_x, normw, emat, bl, inner, z_col0):
    t = xbc.shape[0]
    L = CHUNK
    nc = t // bl // L
    G = GROUPS
    gw = inner // G
    hpg = gw // HEAD_DIM
    assert z_col0 % gw == 0
    zb0 = z_col0 // gw
    bb0 = inner // STATE
    cb0 = bb0 + G

    P = SSD_GROUPS_PER_STEP
    assert G % P == 0 and bb0 % P == 0 and cb0 % P == 0 and zb0 % P == 0

    def body(xs_ref, b_ref, c_ref, z_ref, dtr_ref, dtb_ref, alog_ref, dsk_ref, nw_ref, e_ref,
             y_ref, yn_ref, st_ref, h_ref, dt_s, acs_s):
        c = pl.program_id(1)
        gb = pl.program_id(2)

        @pl.when(gb == 0)
        def _():
            _, dt, _, acs = _ssd_scalars(dtr_ref[...], dtb_ref[...], alog_ref[...])
            dt_s[...] = dt
            acs_s[...] = acs

        tri = lax.broadcasted_iota(jnp.int32, (L, L), 0) >= lax.broadcasted_iota(jnp.int32, (L, L), 1)
        for gi in range(P):
            g = gb * P + gi
            cols = slice(gi * gw, (gi + 1) * gw)
            ncol = slice(gi * STATE, (gi + 1) * STATE)

            @pl.when(c == 0)
            def _():
                h_ref[g] = jnp.zeros((STATE, gw), F32)

            xs = xs_ref[:, cols].astype(F32)
            bg = b_ref[:, ncol]
            cg = c_ref[:, ncol]
            dt_x, acs_x, e_x, dec_x = _ssd_group_common(xs, dt_s[...], acs_s[...], e_ref[:, cols])
            xdt = xs * dt_x
            xdt_b = xdt.astype(BF16)
            cb = _dot(cg, bg, NT)
            acs_t = acs_x.T
            h = h_ref[g]
            st_ref[0, gi] = h
            y = _dot(cg, h) * e_x + dsk_ref[:, cols] * xs
            for r in range(hpg):
                lm = _decay_matrix(acs_x, acs_t, r, tri)
                m = cb * lm
                y = y + _dot(m, xdt_b * _head_mask(r, gw, BF16))
            h_ref[g] = h * e_x[L - 1:L, :] + _dot(bg, xdt * dec_x, TN)
            yq = y.astype(y_ref.dtype)
            y_ref[:, cols] = yq
            z = z_ref[:, cols].astype(F32)
            yg = yq.astype(F32) * (z * _sigmoid(z))
            rs = lax.rsqrt(jnp.mean(yg * yg, axis=-1, keepdims=True) + RMS_EPS)
            yn_ref[:, cols] = (yg * rs * nw_ref[:, cols]).astype(yn_ref.dtype)

    return pl.pallas_call(
        body, name="ssd_fwd",
        grid=(bl, nc, G // P),
        in_specs=[
            pl.BlockSpec((L, P * gw), lambda b, c, g: (b * nc + c, g)),
            pl.BlockSpec((L, P * STATE), lambda b, c, g: (b * nc + c, bb0 // P + g)),
            pl.BlockSpec((L, P * STATE), lambda b, c, g: (b * nc + c, cb0 // P + g)),
            pl.BlockSpec((L, P * gw), lambda b, c, g: (b * nc + c, zb0 // P + g)),
            pl.BlockSpec((L, LANES), lambda b, c, g: (b * nc + c, 0)),
            pl.BlockSpec((1, LANES), lambda b, c, g: (0, 0)),
            pl.BlockSpec((1, LANES), lambda b, c, g: (0, 0)),
            pl.BlockSpec((1, P * gw), lambda b, c, g: (0, g)),
            pl.BlockSpec((1, P * gw), lambda b, c, g: (0, g)),
            pl.BlockSpec((LANES, P * gw), lambda b, c, g: (0, g)),
        ],
        out_specs=[
            pl.BlockSpec((L, P * gw), lambda b, c, g: (b * nc + c, g)),
            pl.BlockSpec((L, P * gw), lambda b, c, g: (b * nc + c, g)),
            pl.BlockSpec((1, P, STATE, gw), lambda b, c, g: (b * nc + c, g, 0, 0)),
        ],
        out_shape=[
            jax.ShapeDtypeStruct((t, inner), BF16),
            jax.ShapeDtypeStruct((t, inner), BF16),
            jax.ShapeDtypeStruct((bl * nc, G, STATE, gw), F32),
        ],
        scratch_shapes=[pltpu.VMEM((G, STATE, gw), F32), pltpu.VMEM((L, LANES), F32), pltpu.VMEM((L, LANES), F32)],
        compiler_params=_params(("arbitrary", "arbitrary", "arbitrary")),
    )(xbc, xbc, xbc, proj, dt_raw, dtb, alog, dskip_x, normw, emat)


def _ssd_bwd(xbc, proj, dt_raw, y, dyn, states, dtb, alog, dskip_x, normw, emat, emat_t, bl, inner, z_col0,
             comm=None):
    t = xbc.shape[0]
    L = CHUNK
    nc = t // bl // L
    G = GROUPS
    gw = inner // G
    hpg = gw // HEAD_DIM
    zb0 = z_col0 // gw
    bb0 = inner // STATE
    cb0 = bb0 + G
    P = SSD_GROUPS_PER_STEP

    def rc(j):
        return nc - 1 - j

    def body(xs_ref, b_ref, c_ref, z_ref, dtr_ref, y_ref, dyn_ref, st_ref, dtb_ref, alog_ref, dsk_ref,
             nw_ref, e_ref, et_ref,
             dxs_ref, db_ref, dc_ref, dz_ref, ddt_ref, dnw_ref, dsk_acc, dalog_acc, ddtb_acc,
             dh_ref, pre_s, dt_s, acs_s, wacs_s, wdt_s):
        b = pl.program_id(0)
        j = pl.program_id(1)
        gb = pl.program_id(2)

        @pl.when((b == 0) & (j == 0) & (gb == 0))
        def _():
            dsk_acc[...] = jnp.zeros_like(dsk_acc)
            dalog_acc[...] = jnp.zeros_like(dalog_acc)
            ddtb_acc[...] = jnp.zeros_like(ddtb_acc)

        @pl.when(gb == 0)
        def _():
            pre, dt, _, acs = _ssd_scalars(dtr_ref[...], dtb_ref[...], alog_ref[...])
            pre_s[...] = pre
            dt_s[...] = dt
            acs_s[...] = acs
            wacs_s[...] = jnp.zeros_like(wacs_s)
            wdt_s[...] = jnp.zeros_like(wdt_s)

        tri = lax.broadcasted_iota(jnp.int32, (L, L), 0) >= lax.broadcasted_iota(jnp.int32, (L, L), 1)
        rowi = lax.broadcasted_iota(jnp.int32, (L, gw), 0)
        for gi in range(P):
            g = gb * P + gi
            cols = slice(gi * gw, (gi + 1) * gw)
            ncol = slice(gi * STATE, (gi + 1) * STATE)

            @pl.when((b == 0) & (j == 0))
            def _():
                dnw_ref[g] = jnp.zeros((SUBLANES, gw), F32)

            @pl.when(j == 0)
            def _():
                dh_ref[g] = jnp.zeros((STATE, gw), F32)

            xs = xs_ref[:, cols].astype(F32)
            bg = b_ref[:, ncol]
            cg = c_ref[:, ncol]
            dt_x, acs_x, e_x, dec_x = _ssd_group_common(xs, dt_s[...], acs_s[...], e_ref[:, cols])
            xdt = xs * dt_x
            xdt_b = xdt.astype(BF16)
            cb = _dot(cg, bg, NT)
            acs_t = acs_x.T
            h = st_ref[0, gi]
            hb16 = h.astype(BF16)
            dsk = dsk_ref[:, cols]

            yv = y_ref[:, cols].astype(F32)
            z = z_ref[:, cols].astype(F32)
            sgz = _sigmoid(z)
            sz = z * sgz
            yg = yv * sz
            rs = lax.rsqrt(jnp.mean(yg * yg, axis=-1, keepdims=True) + RMS_EPS)
            yhat = yg * rs
            dyn_v = dyn_ref[:, cols].astype(F32)
            dnw_ref[g] += _colsum(dyn_v * yhat)
            dyh = dyn_v * nw_ref[:, cols]
            dyg = rs * (dyh - yhat * jnp.mean(dyh * yhat, axis=-1, keepdims=True))
            dy = dyg * sz
            dz_ref[:, cols] = (dyg * yv * (sgz * (1.0 + z * (1.0 - sgz)))).astype(dz_ref.dtype)

            dy_b = dy.astype(BF16)
            dcb = jnp.zeros((L, L), F32)
            dxdt_d = jnp.zeros((L, gw), F32)
            ydiag = jnp.zeros((L, gw), F32)
            for r in range(hpg):
                lm = _decay_matrix(acs_x, acs_t, r, tri)
                m = (cb * lm).astype(BF16)
                hm = _head_mask(r, gw, BF16)
                dyr = dy_b * hm
                xr = xdt_b * hm
                ydiag = ydiag + _dot(m, xr)
                dcb = dcb + _dot(dyr, xdt_b, NT) * lm
                dxdt_d = dxdt_d + _dot(m, dyr, TN)
            dh = dh_ref[g]
            dh16 = dh.astype(BF16)
            xdec_b = (xdt * dec_x).astype(BF16)
            bdh = _dot(bg, dh16)
            dxdt = dxdt_d + dec_x * bdh
            dcb16 = dcb.astype(BF16)
            dye = (dy * e_x).astype(BF16)
            db_ref[:, ncol] = (_dot(dcb16, cg, TN) + _dot(xdec_b, dh16, NT)).astype(db_ref.dtype)
            dc_ref[:, ncol] = (_dot(dcb16, bg) + _dot(dye, hb16, NT)).astype(dc_ref.dtype)
            dprev = _dot(cg, dye, TN)
            cd_row = e_x[L - 1:L, :]
            s_new = _dot(bg, xdec_b, TN)
            last_term = _colsum(dh16.astype(F32) * s_new) + _colsum(dh * h) * cd_row
            yoff = _dot(cg, hb16) * e_x
            wfold = (dy_b.astype(F32) * ydiag + dy * yoff - dxdt_d * xdt_b.astype(F32) - bdh * xdec_b.astype(F32)
                     + jnp.where(rowi == L - 1, last_term, 0.0))
            et = et_ref[cols, :]
            wacs_s[...] += _dot_exact01(wfold, et)
            wdt_s[...] += _dot_exact01(dxdt * xs, et)
            dsk_acc[...] += _dot_exact01(jnp.broadcast_to(_colsum(dy * xs), (SUBLANES, gw)), et)
            dxs_ref[:, cols] = (dsk * dy + dxdt * dt_x).astype(dxs_ref.dtype)
            dh_ref[g] = dprev + cd_row * dh

        @pl.when(gb == G // P - 1)
        def _():
            a = -jnp.exp(alog_ref[...])
            dda = _cumsum_rows(wacs_s[...], reverse=True)
            ddt = wdt_s[...] + dda * a
            ddt_raw = ddt * _sigmoid(pre_s[...])
            ddt_ref[...] = ddt_raw
            dalog_acc[...] += _colsum(dda * dt_s[...]) * a
            ddtb_acc[...] += _colsum(ddt_raw)

    def cidx(b, j):
        return b * nc + rc(j)

    accs = lambda shape: pl.BlockSpec(shape, lambda b, j, g: tuple(0 for _ in shape))
    grid = (bl, nc, G // P)
    c_in, c_in_specs, c_out_specs, c_out_shapes = _comm_specs(comm)
    return pl.pallas_call(
        _fuse_comm(body, grid, 14, 9, comm), name="ssd_bwd",
        grid=grid,
        in_specs=[
            pl.BlockSpec((L, P * gw), lambda b, j, g: (cidx(b, j), g)),
            pl.BlockSpec((L, P * STATE), lambda b, j, g: (cidx(b, j), bb0 // P + g)),
            pl.BlockSpec((L, P * STATE), lambda b, j, g: (cidx(b, j), cb0 // P + g)),
            pl.BlockSpec((L, P * gw), lambda b, j, g: (cidx(b, j), zb0 // P + g)),
            pl.BlockSpec((L, LANES), lambda b, j, g: (cidx(b, j), 0)),
            pl.BlockSpec((L, P * gw), lambda b, j, g: (cidx(b, j), g)),
            pl.BlockSpec((L, P * gw), lambda b, j, g: (cidx(b, j), g)),
            pl.BlockSpec((1, P, STATE, gw), lambda b, j, g: (cidx(b, j), g, 0, 0)),
            pl.BlockSpec((1, LANES), lambda b, j, g: (0, 0)),
            pl.BlockSpec((1, LANES), lambda b, j, g: (0, 0)),
            pl.BlockSpec((1, P * gw), lambda b, j, g: (0, g)),
            pl.BlockSpec((1, P * gw), lambda b, j, g: (0, g)),
            pl.BlockSpec((LANES, P * gw), lambda b, j, g: (0, g)),
            pl.BlockSpec((P * gw, LANES), lambda b, j, g: (g, 0)),
        ] + c_in_specs,
        out_specs=[
            pl.BlockSpec((L, P * gw), lambda b, j, g: (cidx(b, j), g)),
            pl.BlockSpec((L, P * STATE), lambda b, j, g: (cidx(b, j), g)),
            pl.BlockSpec((L, P * STATE), lambda b, j, g: (cidx(b, j), g)),
            pl.BlockSpec((L, P * gw), lambda b, j, g: (cidx(b, j), g)),
            pl.BlockSpec((L, LANES), lambda b, j, g: (cidx(b, j), 0)),
            accs((G, SUBLANES, gw)),
            accs((SUBLANES, LANES)),
            accs((SUBLANES, LANES)),
            accs((SUBLANES, LANES)),
        ] + c_out_specs,
        out_shape=[
            jax.ShapeDtypeStruct((t, inner), BF16),
            jax.ShapeDtypeStruct((t, G * STATE), BF16),
            jax.ShapeDtypeStruct((t, G * STATE), BF16),
            jax.ShapeDtypeStruct((t, inner), BF16),
            jax.ShapeDtypeStruct((t, LANES), F32),
            jax.ShapeDtypeStruct((G, SUBLANES, gw), F32),
            jax.ShapeDtypeStruct((SUBLANES, LANES), F32),
            jax.ShapeDtypeStruct((SUBLANES, LANES), F32),
            jax.ShapeDtypeStruct((SUBLANES, LANES), F32),
        ] + c_out_shapes,
        scratch_shapes=[pltpu.VMEM((G, STATE, gw), F32)] + [pltpu.VMEM((L, LANES), F32)] * 5
        + (list(comm.scratch) if comm else []),
        compiler_params=_params(("arbitrary", "arbitrary", "arbitrary")),
    )(xbc, xbc, xbc, proj, dt_raw, y, dyn, states, dtb, alog, dskip_x, normw, emat, emat_t, *c_in)


def _pool_window(u, w, anti):
    n = u.shape[0]
    row = lax.broadcasted_iota(jnp.int32, u.shape, 0)
    acc = u
    s = 1
    while s < w:
        if anti:
            acc = acc + jnp.where(row < n - s, pltpu.roll(acc, n - s, 0), 0.0)
        else:
            acc = acc + jnp.where(row >= s, pltpu.roll(acc, s, 0), 0.0)
        s *= 2
    return acc


def _pool_cnt(shape, w):
    row = lax.broadcasted_iota(jnp.int32, shape, 0)
    return jnp.minimum(row + 1, w).astype(F32)


def _pool_fwd(proj, wpg, bl, d, u_col0):
    t = proj.shape[0]
    s = t // bl
    pg = len(POOL_WINDOWS)
    cg = d // pg
    ub0 = u_col0 // d

    def body(u_ref, w_ref, o_ref):
        for gi, w in enumerate(POOL_WINDOWS):
            u = u_ref[:, gi * cg:(gi + 1) * cg].astype(F32)
            pooled = _pool_window(u, w, False) / _pool_cnt(u.shape, w) - u
            o_ref[:, gi * cg:(gi + 1) * cg] = _dot(pooled, w_ref[gi]).astype(o_ref.dtype)

    return pl.pallas_call(
        body, name="pool_fwd",
        grid=(bl,),
        in_specs=[pl.BlockSpec((s, d), lambda b: (b, ub0)), pl.BlockSpec((pg, cg, cg), lambda b: (0, 0, 0))],
        out_specs=pl.BlockSpec((s, d), lambda b: (b, 0)),
        out_shape=jax.ShapeDtypeStruct((t, d), BF16),
        compiler_params=_params(("parallel",)),
    )(proj, wpg)


def _pool_bwd(proj, dyp, wpg, bl, d, u_col0):
    t = proj.shape[0]
    s = t // bl
    pg = len(POOL_WINDOWS)
    cg = d // pg
    ub0 = u_col0 // d

    def body(u_ref, dy_ref, w_ref, du_ref, dw_ref):
        @pl.when(pl.program_id(0) == 0)
        def _():
            dw_ref[...] = jnp.zeros_like(dw_ref)

        for gi, w in enumerate(POOL_WINDOWS):
            u = u_ref[:, gi * cg:(gi + 1) * cg].astype(F32)
            cnt = _pool_cnt(u.shape, w)
            pooled = _pool_window(u, w, False) / cnt - u
            dy = dy_ref[:, gi * cg:(gi + 1) * cg]
            dw_ref[gi] += _dot(pooled, dy, TN)
            dp = _dot(dy, w_ref[gi], NT)
            du_ref[:, gi * cg:(gi + 1) * cg] = (_pool_window(dp / cnt, w, True) - dp).astype(du_ref.dtype)

    return pl.pallas_call(
        body, name="pool_bwd",
        grid=(bl,),
        in_specs=[pl.BlockSpec((s, d), lambda b: (b, ub0)), pl.BlockSpec((s, d), lambda b: (b, 0)),
                  pl.BlockSpec((pg, cg, cg), lambda b: (0, 0, 0))],
        out_specs=[pl.BlockSpec((s, d), lambda b: (b, 0)), pl.BlockSpec((pg, cg, cg), lambda b: (0, 0, 0))],
        out_shape=[jax.ShapeDtypeStruct((t, d), BF16), jax.ShapeDtypeStruct((pg, cg, cg), F32)],
        compiler_params=_params(("arbitrary",)),
    )(proj, dyp, wpg)


def _merge_fwd(proj, ypr, yssd, x, w_out, b_gates, pool_scale, d, lg_col0, tm):
    t = x.shape[0]
    lb0 = lg_col0 // (2 * d)

    def body(lg_ref, yp_ref, ys_ref, x_ref, w_ref, bg_ref, ps_ref, mg_ref, r1_ref):
        lg = lg_ref[...].astype(F32) + bg_ref[...]
        ga = _sigmoid(lg[:, :d])
        gb = _sigmoid(lg[:, d:])
        merged = ga * (yp_ref[...].astype(F32) * ps_ref[...]) + gb * ys_ref[...].astype(F32)
        mg_ref[...] = merged.astype(mg_ref.dtype)
        r1_ref[...] = ALPHA * x_ref[...] + _dot(mg_ref[...], w_ref[...])

    row = lambda w: pl.BlockSpec((tm, w), lambda i: (i, 0))
    full = lambda a: pl.BlockSpec(a.shape, lambda i: (0, 0))
    return pl.pallas_call(
        body, name="merge_fwd",
        grid=(t // tm,),
        in_specs=[pl.BlockSpec((tm, 2 * d), lambda i: (i, lb0)), row(d), row(d), row(d), full(w_out), full(b_gates),
                  full(pool_scale)],
        out_specs=[row(d), row(d)],
        out_shape=[jax.ShapeDtypeStruct((t, d), BF16), jax.ShapeDtypeStruct((t, d), F32)],
        compiler_params=_params(("parallel",)),
    )(proj, ypr, yssd, x, w_out, b_gates, pool_scale)


def _merge_bwd(dr1, proj, ypr, yssd, w_out, b_gates, pool_scale, d, lg_col0, tm):
    t = dr1.shape[0]
    lb0 = lg_col0 // (2 * d)

    def body(dr_ref, lg_ref, yp_ref, ys_ref, w_ref, bg_ref, ps_ref, dlg_ref, dyp_ref, dys_ref, dbg_ref, dps_ref):
        @pl.when(pl.program_id(0) == 0)
        def _():
            dbg_ref[...] = jnp.zeros_like(dbg_ref)
            dps_ref[...] = jnp.zeros_like(dps_ref)

        dm = _dot(dr_ref[...], w_ref[...], NT)
        lg = lg_ref[...].astype(F32) + bg_ref[...]
        ga = _sigmoid(lg[:, :d])
        gb = _sigmoid(lg[:, d:])
        ypr_v = yp_ref[...].astype(F32)
        ys_v = ys_ref[...].astype(F32)
        ps = ps_ref[...]
        dga = dm * ypr_v * ps
        dla = dga * ga * (1.0 - ga)
        dlb = dm * ys_v * gb * (1.0 - gb)
        dlg_ref[:, :d] = dla.astype(dlg_ref.dtype)
        dlg_ref[:, d:] = dlb.astype(dlg_ref.dtype)
        dyp_ref[...] = (dm * ga * ps).astype(dyp_ref.dtype)
        dys_ref[...] = (dm * gb).astype(dys_ref.dtype)
        dbg_ref[0:1, :d] += _colsum(dla)
        dbg_ref[0:1, d:] += _colsum(dlb)
        dps_ref[0:1, :] += _colsum(dm * ga * ypr_v)

    row = lambda w: pl.BlockSpec((tm, w), lambda i: (i, 0))
    full = lambda a: pl.BlockSpec(a.shape, lambda i: (0, 0))
    acc = lambda w: pl.BlockSpec((SUBLANES, w), lambda i: (0, 0))
    return pl.pallas_call(
        body, name="merge_bwd",
        grid=(t // tm,),
        in_specs=[row(d), pl.BlockSpec((tm, 2 * d), lambda i: (i, lb0)), row(d), row(d), full(w_out), full(b_gates),
                  full(pool_scale)],
        out_specs=[row(2 * d), row(d), row(d), acc(2 * d), acc(d)],
        out_shape=[jax.ShapeDtypeStruct((t, 2 * d), BF16), jax.ShapeDtypeStruct((t, d), BF16),
                   jax.ShapeDtypeStruct((t, d), BF16), jax.ShapeDtypeStruct((SUBLANES, 2 * d), F32),
                   jax.ShapeDtypeStruct((SUBLANES, d), F32)],
        compiler_params=_params(("arbitrary",)),
    )(dr1, proj, ypr, yssd, w_out, b_gates, pool_scale)


def _mlp_fwd(r1, target, w_up, w_down, ln1_g, ln1_b, ln2_g, ln2_b, tm):
    t, d = r1.shape
    nf, _, tf = w_up.shape
    ff = nf * tf

    def body(r1_ref, tg_ref, wu_ref, wd_ref, g1_ref, b1_ref, g2_ref, b2_ref,
             up_ref, h1_ref, dr2_ref, loss_ref, dg2_ref, db2_ref, h1f, acc):
        i = pl.program_id(0)
        f = pl.program_id(1)

        @pl.when((i == 0) & (f == 0))
        def _():
            loss_ref[...] = jnp.zeros_like(loss_ref)
            dg2_ref[...] = jnp.zeros_like(dg2_ref)
            db2_ref[...] = jnp.zeros_like(db2_ref)

        @pl.when(f == 0)
        def _():
            xhat, _ = _ln_fwd(r1_ref[...])
            h1 = xhat * g1_ref[...] + b1_ref[...]
            h1f[...] = h1
            h1_ref[...] = h1.astype(h1_ref.dtype)
            acc[...] = jnp.zeros_like(acc)

        up_ref[...] = _dot(h1_ref[...], wu_ref[0]).astype(up_ref.dtype)
        upq = jnp.maximum(up_ref[...].astype(F32), 0.0)
        acc[...] += _dot(upq * upq, wd_ref[...])

        @pl.when(f == nf - 1)
        def _():
            xhat, rstd = _ln_fwd(ALPHA * h1f[...] + acc[...])
            g2 = g2_ref[...]
            diff = xhat * g2 + b2_ref[...] - tg_ref[...]
            loss_ref[...] += 0.5 / d * jnp.sum(diff * diff)
            dh2 = diff * (1.0 / d)
            dg2_ref[0:1, :] += _colsum(dh2 * xhat)
            db2_ref[0:1, :] += _colsum(dh2)
            dr2_ref[...] = _ln_bwd(dh2, xhat, rstd, g2).astype(dr2_ref.dtype)

    row = pl.BlockSpec((tm, d), lambda i, f: (i, 0))
    vec = pl.BlockSpec((1, d), lambda i, f: (0, 0))
    acc8 = pl.BlockSpec((SUBLANES, d), lambda i, f: (0, 0))
    return pl.pallas_call(
        body, name="mlp_fwd",
        grid=(t // tm, nf),
        in_specs=[row, row, pl.BlockSpec((1, d, tf), lambda i, f: (f, 0, 0)), pl.BlockSpec((tf, d), lambda i, f: (f, 0)),
                  vec, vec, vec, vec],
        out_specs=[pl.BlockSpec((tm, tf), lambda i, f: (i, f)), row, row,
                   pl.BlockSpec((SUBLANES, LANES), lambda i, f: (0, 0)), acc8, acc8],
        out_shape=[jax.ShapeDtypeStruct((t, ff), BF16), jax.ShapeDtypeStruct((t, d), BF16),
                   jax.ShapeDtypeStruct((t, d), BF16), jax.ShapeDtypeStruct((SUBLANES, LANES), F32),
                   jax.ShapeDtypeStruct((SUBLANES, d), F32), jax.ShapeDtypeStruct((SUBLANES, d), F32)],
        scratch_shapes=[pltpu.VMEM((tm, d), F32), pltpu.VMEM((tm, d), F32)],
        compiler_params=_params(("arbitrary", "arbitrary")),
    )(r1, target, w_up, w_down, ln1_g, ln1_b, ln2_g, ln2_b)


def _mlp_bwd(dr2, up, r1, w_up, w_down, ln1_g, tm):
    t, d = r1.shape
    nf, _, tf = w_up.shape
    ff = nf * tf

    def body(dr2_ref, up_ref, r1_ref, wu_ref, wd_ref, g1_ref, dup_ref, dr1_ref, dg1_ref, db1_ref, acc):
        i = pl.program_id(0)
        f = pl.program_id(1)

        @pl.when((i == 0) & (f == 0))
        def _():
            dg1_ref[...] = jnp.zeros_like(dg1_ref)
            db1_ref[...] = jnp.zeros_like(db1_ref)

        @pl.when(f == 0)
        def _():
            acc[...] = jnp.zeros_like(acc)

        dact = _dot(dr2_ref[...], wd_ref[...], NT)
        dup_ref[...] = (dact * 2.0 * jnp.maximum(up_ref[...].astype(F32), 0.0)).astype(dup_ref.dtype)
        acc[...] += _dot(dup_ref[...], wu_ref[0], NT)

        @pl.when(f == nf - 1)
        def _():
            dh1 = acc[...] + ALPHA * dr2_ref[...].astype(F32)
            xhat, rstd = _ln_fwd(r1_ref[...])
            dg1_ref[0:1, :] += _colsum(dh1 * xhat)
            db1_ref[0:1, :] += _colsum(dh1)
            dr1_ref[...] = _ln_bwd(dh1, xhat, rstd, g1_ref[...]).astype(dr1_ref.dtype)

    row = pl.BlockSpec((tm, d), lambda i, f: (i, 0))
    acc8 = pl.BlockSpec((SUBLANES, d), lambda i, f: (0, 0))
    return pl.pallas_call(
        body, name="mlp_bwd",
        grid=(t // tm, nf),
        in_specs=[row, pl.BlockSpec((tm, tf), lambda i, f: (i, f)), row,
                  pl.BlockSpec((1, d, tf), lambda i, f: (f, 0, 0)), pl.BlockSpec((tf, d), lambda i, f: (f, 0)),
                  pl.BlockSpec((1, d), lambda i, f: (0, 0))],
        out_specs=[pl.BlockSpec((tm, tf), lambda i, f: (i, f)), row, acc8, acc8],
        out_shape=[jax.ShapeDtypeStruct((t, ff), BF16), jax.ShapeDtypeStruct((t, d), BF16),
                   jax.ShapeDtypeStruct((SUBLANES, d), F32), jax.ShapeDtypeStruct((SUBLANES, d), F32)],
        scratch_shapes=[pltpu.VMEM((tm, d), F32)],
        compiler_params=_params(("arbitrary", "arbitrary")),
    )(dr2, up, r1, w_up, w_down, ln1_g)


def _dx_kernel(segs, w_main, ddt, w_dt, dr1, tm, tk, comm=None):
    t, d = dr1.shape
    nblk = [s.shape[1] // tk for s in segs]
    starts = [sum(nblk[:i]) for i in range(len(segs))]
    nk = sum(nblk)
    nseg = len(segs)

    def body(*refs):
        seg_refs = refs[:nseg]
        w_ref, ddt_ref, wdt_ref, dr1_ref, o_ref, acc = refs[nseg:]
        k = pl.program_id(1)

        @pl.when(k == 0)
        def _():
            acc[...] = ALPHA * dr1_ref[...].astype(F32) + _dot(ddt_ref[...], wdt_ref[...], NT)

        for si in range(nseg):
            @pl.when((k >= starts[si]) & (k < starts[si] + nblk[si]))
            def _(si=si):
                acc[...] += _dot(seg_refs[si][...], w_ref[...], NT)

        @pl.when(k == nk - 1)
        def _():
            o_ref[...] = acc[...]

    def seg_spec(si):
        return pl.BlockSpec((tm, tk), lambda i, k: (i, jnp.clip(k - starts[si], 0, nblk[si] - 1)))

    row = pl.BlockSpec((tm, d), lambda i, k: (i, 0))
    grid = (t // tm, nk)
    c_in, c_in_specs, c_out_specs, c_out_shapes = _comm_specs(comm)
    return pl.pallas_call(
        _fuse_comm(body, grid, nseg + 4, 1, comm), name="dx",
        grid=grid,
        in_specs=[seg_spec(si) for si in range(nseg)] + [
            pl.BlockSpec((d, tk), lambda i, k: (0, k)), pl.BlockSpec((tm, LANES), lambda i, k: (i, 0)),
            pl.BlockSpec((d, LANES), lambda i, k: (0, 0)), row] + c_in_specs,
        out_specs=[row] + c_out_specs,
        out_shape=[jax.ShapeDtypeStruct((t, d), F32)] + c_out_shapes,
        scratch_shapes=[pltpu.VMEM((tm, d), F32)] + (list(comm.scratch) if comm else []),
        compiler_params=_params(("arbitrary", "arbitrary")),
    )(*segs, w_main, ddt, w_dt, dr1, *c_in)


def _dims(d):
    inner = 2 * d
    heads = inner // HEAD_DIM
    cd = inner + 2 * GROUPS * STATE
    assert heads <= LANES and inner % (GROUPS * LANES) == 0 and d % (len(POOL_WINDOWS) * LANES) == 0
    o_z, o_xbc, o_dt, o_lg = d, d + inner, d + inner + cd, d + inner + cd + heads
    return inner, heads, cd, (o_z, o_xbc, o_dt, o_lg)


def _row(v, width=None):
    v = v.reshape(1, -1).astype(F32)
    if width is not None and v.shape[1] < width:
        v = jnp.pad(v, ((0, 0), (0, width - v.shape[1])))
    return v


def _local_step(x2, tgt2, w, shards, core, bl):
    t, d = x2.shape
    inner, heads, cd, _ = _dims(d)
    gs = GROUPS * STATE
    nc = t // bl // CHUNK
    w_main, w_dt = _w_in_internal(w["w_in_blocks"], d)
    c_z, c_lg, c_u = cd, cd + inner, cd + inner + 2 * d
    conv_w8 = jnp.pad(w["conv_w"].astype(F32), ((0, SUBLANES - CONV_K), (0, 0)))
    conv_b = _row(w["conv_b"])
    dtb, alog = _row(w["dt_bias"], LANES), _row(w["a_log"], LANES)
    dskip_x = _row(jnp.repeat(w["d_skip"].reshape(-1), HEAD_DIM))
    normw = _row(w["ssd_norm_w"])
    col_head = lax.broadcasted_iota(jnp.int32, (LANES, inner), 1) // HEAD_DIM
    emat = (col_head == lax.broadcasted_iota(jnp.int32, (LANES, inner), 0)).astype(BF16)
    emat_t = emat.T
    w_main, w_dt = w_main.astype(BF16), w_dt.astype(BF16)
    b_gates, pool_scale = _row(w["b_gates"]), _row(w["pool_scale"])
    ln1_g, ln1_b, ln2_g, ln2_b = _row(w["ln1_g"]), _row(w["ln1_b"]), _row(w["ln2_g"]), _row(w["ln2_b"])

    tm = min(512, t)
    tk = min(1024, d)
    ct = min(512, d)
    rt = min(512, t // bl)
    nct = t // bl // rt
    mm = functools.partial(_matmul, bm=1024, bn=tk, bk=1024)
    xb = x2.astype(BF16)

    proj, xbc, *gathered = _in_proj(xb, w_main, conv_w8, conv_b, cd, t // bl, 1024, tk,
                                    _all_gather_comm([shards[n] for n in OTHERS]))
    gathered = dict(zip(OTHERS, gathered))
    w_ssd, w_out, w_down = (gathered[n].reshape(-1, d) for n in ("w_ssd_proj", "w_out", "w_down"))
    w_up = gathered["w_up"]
    npg = len(POOL_WINDOWS)
    cg = d // npg
    wpg = gathered["w_pool_group"].reshape(N_DEV, npg, cg // N_DEV, cg).transpose(1, 0, 2, 3).reshape(npg, cg, cg)
    dt_raw = mm(xb, w_dt, "nn", F32, name="in_proj_dt")
    y, yn, states = _ssd_fwd(xbc, proj, dt_raw, dtb, alog, dskip_x, normw, emat, bl, inner, c_z)
    yssd = mm(yn, w_ssd, "nn", BF16, name="ssd_proj")
    ypr = _pool_fwd(proj, wpg, bl, d, c_u)
    merged, r1 = _merge_fwd(proj, ypr, yssd, x2, w_out, b_gates, pool_scale, d, c_lg, tm)
    tmm = min(1024, t)
    up, h1, dr2, loss8, dg2, db2 = _mlp_fwd(r1, tgt2, w_up, w_down, ln1_g, ln1_b, ln2_g, ln2_b, tmm)

    dup, dr1, dg1, db1 = _mlp_bwd(dr2, up, r1, w_up, w_down, ln1_g, tmm)
    relu2 = lambda v: jnp.square(jnp.maximum(v, 0.0))
    g = {}
    g["w_down"] = mm(up, dr2, "tn", BF16, name="dw_down", a_fn=relu2)
    g["w_up"] = _matmul(h1, dup, "tn", BF16, bm=1024, bn=w_up.shape[2], bk=1024, name="dw_up", col_blocks=N_DEV)
    g["w_out"] = mm(merged, dr1, "tn", BF16, name="dw_out")
    dlg, dyp, dys, dbg, dps = _merge_bwd(dr1, proj, ypr, yssd, w_out, b_gates, pool_scale, d, c_lg, tm)
    du, dwpg = _pool_bwd(proj, dyp, wpg, bl, d, c_u)
    g["w_pool_group"] = dwpg.reshape(npg, N_DEV, cg // N_DEV, cg).transpose(1, 0, 2, 3).reshape(
        N_DEV, npg * cg // N_DEV, cg).astype(BF16)
    dyn = mm(dys, w_ssd, "nt", BF16, name="d_ssd_proj")
    g["w_ssd_proj"] = mm(yn, dys, "tn", BF16, name="dw_ssd_proj")

    def chip_sums(names, tag):
        parts = [g.pop(n).reshape((N_DEV,) + shards_2d[n]) for n in names]
        recv = _run_comm(_rs_sibling_comm(parts), "rs_sibling_" + tag)
        return [_add_pairs(core, p, r, "rs_add_" + n) for n, p, r in zip(names, parts, recv)]

    shards_2d = {n: s.shape for n, s in shards.items()}
    shards_2d["w_in"] = w["w_in_blocks"].shape[1:]
    dxs, dbm, dcm, dz, ddt, dnw, dsk, dalog, ddtb, *recv_others = _ssd_bwd(
        xbc, proj, dt_raw, y, dyn, states, dtb, alog, dskip_x, normw, emat, emat_t, bl, inner, c_z,
        comm=_rs_chips_comm(chip_sums(OTHERS, "a")))
    dxs_p, dcw_x, dcb_x = _conv_bwd(proj, dxs, conv_w8, conv_b, nct, 0, inner, ct, rt, "conv_bwd_x")
    dbm_p, dcw_b, dcb_b = _conv_bwd(proj, dbm, conv_w8, conv_b, nct, inner, gs, ct, rt, "conv_bwd_b")
    dcm_p, dcw_c, dcb_c = _conv_bwd(proj, dcm, conv_w8, conv_b, nct, inner + gs, gs, ct, rt, "conv_bwd_c")
    segs = [dxs_p, dbm_p, dcm_p, dz, dlg, du]
    keys = [k for k, _, _ in _col_segments(d)]
    dws = {k: mm(xb, s, "tn", BF16, name="dw_in_" + k) for k, s in zip(keys, segs + [ddt])}
    g["w_in"] = _w_in_grad_blocks(dws, d, w["w_in_blocks"].shape[2])
    grad_x, recv_w_in = _dx_kernel(segs, w_main, ddt, w_dt, dr1, tmm, min(512, d),
                                   comm=_rs_chips_comm(chip_sums(["w_in"], "b")))
    recv = dict(zip(OTHERS, recv_others))
    recv["w_in"] = recv_w_in
    g["conv_w"] = jnp.concatenate([dcw_x, dcw_b, dcw_c], axis=1)[:CONV_K]
    g["conv_b"] = jnp.concatenate([dcb_x, dcb_b, dcb_c], axis=1)[0]
    g["b_gates"], g["pool_scale"] = dbg[0], dps[0]
    g["dt_bias"], g["a_log"], g["d_skip"] = ddtb[0, :heads], dalog[0, :heads], dsk[0, :heads]
    g["ssd_norm_w"] = dnw[:, 0, :].reshape(inner)
    g["ln1_g"], g["ln1_b"], g["ln2_g"], g["ln2_b"] = dg1[0], db1[0], dg2[0], db2[0]
    return loss8, grad_x, g, recv


BIG = ("w_in", "w_ssd_proj", "w_pool_group", "w_out", "w_up", "w_down")
OTHERS = BIG[1:]
SMALL = ("b_gates", "conv_b", "dt_bias", "a_log", "d_skip", "ssd_norm_w", "pool_scale", "ln1_g", "ln1_b", "ln2_g",
         "ln2_b")
SMALL_PACK = SMALL + ("conv_w",)
NAMES = ("w_in", "b_gates", "conv_w", "conv_b", "dt_bias", "a_log", "d_skip", "ssd_norm_w", "w_ssd_proj",
         "w_pool_group", "pool_scale", "w_out", "ln1_g", "ln1_b", "w_up", "w_down", "ln2_g", "ln2_b")


def _size(shape):
    n = 1
    for s in shape:
        n *= s
    return n


def _rows128(v):
    v = v.astype(F32).reshape((-1, v.shape[-1]))
    n = v.shape[-1]
    v = jnp.pad(v, ((0, 0), (0, -n % LANES)))
    return v.reshape(-1, LANES)


def _pack_small(vals, extra):
    parts = [_rows128(vals[n]) for n in SMALL_PACK]
    parts.append(jnp.pad(extra.reshape(1, 1).astype(F32), ((0, 0), (0, LANES - 1))))
    buf = jnp.concatenate(parts, axis=0)
    return jnp.pad(buf, ((0, -buf.shape[0] % SUBLANES), (0, 0)))


def _unpack_small(buf, shapes):
    out, off = {}, 0
    for n in SMALL_PACK:
        lead, last = _size(shapes[n][:-1]), shapes[n][-1]
        per = -(-last // LANES)
        out[n] = buf[off:off + lead * per].reshape(lead, per * LANES)[:, :last].reshape(shapes[n])
        off += lead * per
    return out, buf[off, 0]


def _col_segments(d):
    inner, heads, cd, (o_z, o_xbc, o_dt, o_lg) = _dims(d)
    gs = GROUPS * STATE
    return [("xs", o_xbc, inner), ("B", o_xbc + inner, gs), ("C", o_xbc + inner + gs, gs), ("z", o_z, inner),
            ("lg", o_lg, 2 * d), ("u", 0, d), ("dt", o_dt, heads)]


def _cols_from_blocks(blocks, start, width, bw):
    parts, pos = [], start
    while pos < start + width:
        k, off = divmod(pos, bw)
        n = min(bw - off, start + width - pos)
        parts.append(blocks[k][:, off:off + n])
        pos += n
    return parts


def _w_in_internal(blocks, d):
    bw = blocks.shape[2]
    segs = _col_segments(d)
    heads = segs[-1][2]
    main = [p for _, s, w_ in segs[:-1] for p in _cols_from_blocks(blocks, s, w_, bw)]
    w_dt = jnp.concatenate(_cols_from_blocks(blocks, segs[-1][1], heads, bw), axis=1)
    return jnp.concatenate(main, axis=1), jnp.pad(w_dt, ((0, 0), (0, LANES - heads)))


def _w_in_grad_blocks(dws, d, bw):
    order = sorted(_col_segments(d), key=lambda s: s[1])
    blocks = []
    for k in range(N_DEV):
        lo, hi, parts = k * bw, (k + 1) * bw, []
        for key, s, w_ in order:
            a, b = max(lo, s), min(hi, s + w_)
            if a < b:
                parts.append(dws[key][:, a - s:b - s])
        blocks.append(jnp.concatenate(parts, axis=1))
    return jnp.stack(blocks)


def _mesh_pos():
    return lax.axis_index("x"), lax.axis_index("y"), lax.axis_index("c")


def _all_gather_comm(shards):
    nw = len(shards)

    def setup(x_refs, out_refs, scr):
        send_sems, recv_sems, local_sems = scr
        x, y, c = _mesh_pos()
        me, sibling = (x, y, c), (x, y, 1 - c)
        chips = [(1 - x, y), (x, 1 - y), (1 - x, 1 - y)]

        def copy(wi, k, block, to, from_input=False):
            px, py, pc = block
            blk = out_refs[wi].at[4 * px + 2 * py + pc]
            return pltpu.make_async_remote_copy(
                src_ref=x_refs[wi] if from_input else blk, dst_ref=blk,
                send_sem=send_sems.at[7 * wi + k], recv_sem=recv_sems.at[7 * wi + k], device_id=to,
                device_id_type=MESH)

        mine = [pltpu.make_async_copy(x_refs[wi], out_refs[wi].at[4 * x + 2 * y + c], local_sems.at[wi])
                for wi in range(nw)]
        sends = []
        for wi in range(nw):
            sends.append(copy(wi, 0, me, sibling, True))
            sends += [copy(wi, 1 + j, me, (*chip, c), True) for j, chip in enumerate(chips)]
        return copy, mine, sends, me, sibling, chips, c

    def start(x_refs, out_refs, scr):
        _, mine, sends, _, _, _, _ = setup(x_refs, out_refs, scr)
        for cp in mine + sends:
            cp.start()

    def wait(x_refs, out_refs, scr):
        copy, mine, sends, me, sibling, chips, c = setup(x_refs, out_refs, scr)
        passed = []
        for wi in range(nw):
            for j, chip in enumerate(chips):
                copy(wi, 1 + j, (*chip, c), me).wait_recv()
                passed.append(copy(wi, 4 + j, (*chip, c), sibling))
                passed[-1].start()
        for wi in range(nw):
            copy(wi, 0, sibling, me).wait_recv()
            for j, chip in enumerate(chips):
                copy(wi, 4 + j, (*chip, 1 - c), me).wait_recv()
        for cp in sends + passed:
            cp.wait_send()
        for cp in mine:
            cp.wait()

    return _Comm(
        inputs=list(shards),
        out_shapes=[jax.ShapeDtypeStruct((N_DEV,) + s.shape, s.dtype) for s in shards],
        scratch=[pltpu.SemaphoreType.DMA((7 * nw,)), pltpu.SemaphoreType.DMA((7 * nw,)),
                 pltpu.SemaphoreType.DMA((nw,))],
        start=start, wait=wait)


def _rs_sibling_comm(parts):
    nw = len(parts)
    half = N_DEV // 2

    def copies(p_refs, recv_refs, scr):
        send_sems, recv_sems = scr
        x, y, c = _mesh_pos()
        return [pltpu.make_async_remote_copy(
            src_ref=p_refs[wi].at[2 * q + 1 - c], dst_ref=recv_refs[wi].at[q],
            send_sem=send_sems.at[half * wi + q], recv_sem=recv_sems.at[half * wi + q],
            device_id=(x, y, 1 - c), device_id_type=MESH) for wi in range(nw) for q in range(half)]

    def start(p_refs, recv_refs, scr):
        for cp in copies(p_refs, recv_refs, scr):
            cp.start()

    def wait(p_refs, recv_refs, scr):
        for cp in copies(p_refs, recv_refs, scr):
            cp.wait()

    return _Comm(
        inputs=list(parts),
        out_shapes=[jax.ShapeDtypeStruct((half,) + p.shape[1:], p.dtype) for p in parts],
        scratch=[pltpu.SemaphoreType.DMA((half * nw,)), pltpu.SemaphoreType.DMA((half * nw,))],
        start=start, wait=wait)


def _rs_chips_comm(tbs):
    nw = len(tbs)

    def copies(t_refs, o_refs, scr):
        send_sems, recv_sems, local_sems = scr
        x, y, c = _mesh_pos()
        p = 2 * x + y
        chips = [(1 - x, y), (x, 1 - y), (1 - x, 1 - y)]
        own = [pltpu.make_async_copy(t_refs[wi].at[p], o_refs[wi].at[p], local_sems.at[wi]) for wi in range(nw)]
        remote = [pltpu.make_async_remote_copy(
            src_ref=t_refs[wi].at[2 * qx + qy], dst_ref=o_refs[wi].at[p], send_sem=send_sems.at[3 * wi + j],
            recv_sem=recv_sems.at[3 * wi + j], device_id=(qx, qy, c), device_id_type=MESH)
            for wi in range(nw) for j, (qx, qy) in enumerate(chips)]
        arriving = [pltpu.make_async_remote_copy(
            src_ref=t_refs[wi].at[p], dst_ref=o_refs[wi].at[2 * qx + qy], send_sem=send_sems.at[3 * wi + j],
            recv_sem=recv_sems.at[3 * wi + j], device_id=(qx, qy, c), device_id_type=MESH)
            for wi in range(nw) for j, (qx, qy) in enumerate(chips)]
        return own, remote, arriving

    def start(t_refs, o_refs, scr):
        own, remote, _ = copies(t_refs, o_refs, scr)
        for cp in own + remote:
            cp.start()

    def wait(t_refs, o_refs, scr):
        own, remote, arriving = copies(t_refs, o_refs, scr)
        for cp in arriving:
            cp.wait_recv()
        for cp in remote:
            cp.wait_send()
        for cp in own:
            cp.wait()

    return _Comm(
        inputs=list(tbs),
        out_shapes=[jax.ShapeDtypeStruct(t_.shape, t_.dtype) for t_ in tbs],
        scratch=[pltpu.SemaphoreType.DMA((3 * nw,)), pltpu.SemaphoreType.DMA((3 * nw,)),
                 pltpu.SemaphoreType.DMA((nw,))],
        start=start, wait=wait)


def _row_tile(rows, cap=256):
    if rows <= cap:
        return rows
    return max(t_ for t_ in range(SUBLANES, cap + 1, SUBLANES) if rows % t_ == 0)


def _add_pairs(core, part, recv, name):
    n, r, c_ = recv.shape
    tr = _row_tile(r)

    def body(core_ref, a_ref, b_ref, o_ref):
        o_ref[...] = (a_ref[...].astype(F32) + b_ref[...].astype(F32)).astype(o_ref.dtype)

    spec = pl.BlockSpec((1, tr, c_), lambda q, i, core_ref: (q, i, 0))
    return pl.pallas_call(
        body, name=name,
        grid_spec=pltpu.PrefetchScalarGridSpec(
            num_scalar_prefetch=1, grid=(n, r // tr),
            in_specs=[pl.BlockSpec((1, tr, c_), lambda q, i, core_ref: (2 * q + core_ref[0], i, 0)), spec],
            out_specs=spec),
        out_shape=jax.ShapeDtypeStruct(recv.shape, BF16), compiler_params=_params(("parallel", "parallel")),
    )(core, part, recv)


def _small_allreduce(vec, name):
    rows = vec.shape[0]

    def body(x_ref, o_ref, buf, send_sems, recv_sems):
        x, y, c = _mesh_pos()
        me = 4 * x + 2 * y + c
        buf[me] = x_ref[...]
        cps = []
        for k in range(1, N_DEV):
            peer = (1 - x if k & 4 else x, 1 - y if k & 2 else y, 1 - c if k & 1 else c)
            cps.append(pltpu.make_async_remote_copy(
                src_ref=x_ref, dst_ref=buf.at[me], send_sem=send_sems.at[k - 1], recv_sem=recv_sems.at[k - 1],
                device_id=peer, device_id_type=MESH))
        for cp in cps:
            cp.start()
        for k in range(1, N_DEV):
            px, py, pc = (1 - x if k & 4 else x, 1 - y if k & 2 else y, 1 - c if k & 1 else c)
            pltpu.make_async_remote_copy(
                src_ref=x_ref, dst_ref=buf.at[4 * px + 2 * py + pc], send_sem=send_sems.at[k - 1],
                recv_sem=recv_sems.at[k - 1], device_id=(px, py, pc), device_id_type=MESH).wait_recv()
        for cp in cps:
            cp.wait_send()
        acc = buf[0]
        for k in range(1, N_DEV):
            acc = acc + buf[k]
        o_ref[...] = acc

    vm = pl.BlockSpec(memory_space=pltpu.VMEM)
    return pl.pallas_call(
        body, name=name,
        in_specs=[vm], out_specs=vm,
        out_shape=jax.ShapeDtypeStruct(vec.shape, F32),
        scratch_shapes=[pltpu.VMEM((N_DEV, rows, LANES), F32), pltpu.SemaphoreType.DMA((N_DEV - 1,)),
                        pltpu.SemaphoreType.DMA((N_DEV - 1,))],
    )(vec)


def _adamw(gparts, w, m, v, name):
    n, r, c_ = gparts.shape
    tr = _row_tile(r)
    c1 = 1.0 / (1.0 - B1 ** STEP)
    c2 = 1.0 / (1.0 - B2 ** STEP)

    def body(g_ref, w_ref, m_ref, v_ref, go_ref, d_ref, mo_ref, vo_ref):
        g = g_ref[0].astype(F32)
        for q in range(1, n):
            g = g + g_ref[q].astype(F32)
        mn = B1 * m_ref[...] + (1.0 - B1) * g
        vn = B2 * v_ref[...] + (1.0 - B2) * (g * g)
        go_ref[...] = g
        mo_ref[...] = mn
        vo_ref[...] = vn
        d_ref[...] = -LR * ((mn * c1) / (jnp.sqrt(vn * c2) + ADAM_EPS) + WD * w_ref[...])

    spec = pl.BlockSpec((tr, c_), lambda i: (i, 0))
    out = jax.ShapeDtypeStruct((r, c_), F32)
    return pl.pallas_call(
        body, name=name, grid=(r // tr,),
        in_specs=[pl.BlockSpec((n, tr, c_), lambda i: (0, i, 0)), spec, spec, spec],
        out_specs=[spec] * 4, out_shape=[out] * 4, compiler_params=_params(("parallel",)),
    )(gparts, w, m, v)


def kernel(x, w_in, b_gates, conv_w, conv_b, dt_bias, a_log, d_skip, ssd_norm_w, w_ssd_proj, w_pool_group, pool_scale, w_out, ln1_g, ln1_b, w_up, w_down, ln2_g, ln2_b, loss_target, m_w_in, m_b_gates, m_conv_w, m_conv_b, m_dt_bias, m_a_log, m_d_skip, m_ssd_norm_w, m_w_ssd_proj, m_w_pool_group, m_pool_scale, m_w_out, m_ln1_g, m_ln1_b, m_w_up, m_w_down, m_ln2_g, m_ln2_b, v_w_in, v_b_gates, v_conv_w, v_conv_b, v_dt_bias, v_a_log, v_d_skip, v_ssd_norm_w, v_w_ssd_proj, v_w_pool_group, v_pool_scale, v_w_out, v_ln1_g, v_ln1_b, v_w_up, v_w_down, v_ln2_g, v_ln2_b):
    ws = (w_in, b_gates, conv_w, conv_b, dt_bias, a_log, d_skip, ssd_norm_w, w_ssd_proj, w_pool_group, pool_scale,
          w_out, ln1_g, ln1_b, w_up, w_down, ln2_g, ln2_b)
    ms = (m_w_in, m_b_gates, m_conv_w, m_conv_b, m_dt_bias, m_a_log, m_d_skip, m_ssd_norm_w, m_w_ssd_proj,
          m_w_pool_group, m_pool_scale, m_w_out, m_ln1_g, m_ln1_b, m_w_up, m_w_down, m_ln2_g, m_ln2_b)
    vs = (v_w_in, v_b_gates, v_conv_w, v_conv_b, v_dt_bias, v_a_log, v_d_skip, v_ssd_norm_w, v_w_ssd_proj,
          v_w_pool_group, v_pool_scale, v_w_out, v_ln1_g, v_ln1_b, v_w_up, v_w_down, v_ln2_g, v_ln2_b)
    w = {n: a[0] for n, a in zip(NAMES, ws)}
    m = {n: a[0] for n, a in zip(NAMES, ms)}
    v = {n: a[0] for n, a in zip(NAMES, vs)}
    out_shapes = {n: a.shape for n, a in zip(NAMES, ws)}
    bl, s, d = x.shape
    x2, tgt2 = x.reshape(bl * s, d), loss_target.reshape(bl * s, d)
    xi, yi, ci = _mesh_pos()
    me = 4 * xi + 2 * yi + ci
    zero = jnp.zeros((), F32)
    shapes = {n: w[n].shape for n in NAMES}
    shape2d = {n: (_size(shapes[n][:-1]), shapes[n][-1]) for n in BIG}
    cwl = shapes["conv_w"][1]

    conv_place = lax.dynamic_update_slice(jnp.zeros((CONV_K, N_DEV * cwl), F32), w["conv_w"], (0, me * cwl))
    conv_full = _small_allreduce(_rows128(conv_place), "gather_conv_w")
    conv_full = conv_full.reshape(CONV_K, N_DEV * cwl)

    shards = {n: w[n].astype(BF16).reshape(shape2d[n]) for n in BIG}
    full = {n: w[n] for n in SMALL}
    full["conv_w"] = conv_full
    full["w_in_blocks"] = _run_comm(_all_gather_comm([shards.pop("w_in")]), "all_gather_w_in")[0]
    loss8, grad_x, g, recv = _local_step(x2, tgt2, full, shards, ci.astype(jnp.int32).reshape(1), bl)

    small_sum = _small_allreduce(_pack_small(g, loss8[0, 0]), "small_allreduce")
    ex_shapes = {n: shapes[n] for n in SMALL}
    ex_shapes["conv_w"] = (CONV_K, N_DEV * cwl)
    gsum, loss = _unpack_small(small_sum, ex_shapes)
    gsum["conv_w"] = lax.dynamic_slice(gsum["conv_w"], (0, me * cwl), (CONV_K, cwl))
    gs_pk = _pack_small(gsum, zero)
    ws_pk, ms_pk, vs_pk = (_pack_small(t_, zero) for t_ in (w, m, v))
    small_out = _adamw(gs_pk[None], ws_pk, ms_pk, vs_pk, "adamw_small")
    loc_shapes = {n: shapes[n] for n in SMALL_PACK}
    res = [_unpack_small(o, loc_shapes)[0] for o in small_out]

    for n in BIG:
        outs = _adamw(recv[n], *(t_[n].reshape(shape2d[n]) for t_ in (w, m, v)), "adamw_" + n)
        for r_, o in zip(res, outs):
            r_[n] = o

    def ordered(r_):
        return [r_[n].reshape(out_shapes[n]) for n in NAMES]

    return (loss, grad_x.reshape(bl, s, d), *ordered(res[0]), *ordered(res[1]), *ordered(res[2]), *ordered(res[3]))
```

```python
import collections
import functools

import jax
import jax.numpy as jnp
from jax import lax
from jax.experimental import pallas as pl
from jax.experimental.pallas import tpu as pltpu

F32 = jnp.float32
BF16 = jnp.bfloat16
MESH = pl.DeviceIdType.MESH

HEAD_DIM = 64
STATE = 128
GROUPS = 8
CONV_K = 4
CHUNK = 256
POOL_WINDOWS = (2, 4, 8, 16)
ALPHA = 2.0 ** 0.25
LN_EPS = 1e-5
RMS_EPS = 1e-5
LR, B1, B2, ADAM_EPS, WD, STEP = 0.001, 0.9, 0.999, 1e-08, 0.01, 10
N_DEV = 8
LANES = 128
SUBLANES = 8
VMEM_LIMIT = 56 * 1024 * 1024
NEG_BIG = -1e30

NN = (((1,), (0,)), ((), ()))
NT = (((1,), (1,)), ((), ()))
TN = (((0,), (0,)), ((), ()))


def _dot(a, b, dims=NN):
    return lax.dot_general(a.astype(BF16), b.astype(BF16), dims, preferred_element_type=F32)


def _dot_exact01(q, e, dims=NN):
    hi = q.astype(BF16)
    r1 = q - hi.astype(F32)
    mid = r1.astype(BF16)
    lo = (r1 - mid.astype(F32)).astype(BF16)
    f = lambda p: lax.dot_general(p, e, dims, preferred_element_type=F32)
    return f(hi) + f(mid) + f(lo)


def _params(sem):
    return pltpu.CompilerParams(dimension_semantics=sem, vmem_limit_bytes=VMEM_LIMIT)


def _sigmoid(x):
    return 1.0 / (1.0 + jnp.exp(-x))


def _colsum(x):
    return jnp.sum(x, axis=0, keepdims=True)


def _ln_fwd(r):
    mu = jnp.mean(r, axis=-1, keepdims=True)
    xc = r - mu
    var = jnp.mean(xc * xc, axis=-1, keepdims=True)
    rstd = lax.rsqrt(var + LN_EPS)
    return xc * rstd, rstd


def _ln_bwd(dy, xhat, rstd, g):
    dxh = dy * g
    m1 = jnp.mean(dxh, axis=-1, keepdims=True)
    m2 = jnp.mean(dxh * xhat, axis=-1, keepdims=True)
    return rstd * (dxh - m1 - xhat * m2)


_Comm = collections.namedtuple("_Comm", "inputs out_shapes scratch start wait")
ANY = pl.BlockSpec(memory_space=pl.ANY)


def _fuse_comm(body, grid, n_in, n_out, comm):
    if comm is None:
        return body
    ci, co = len(comm.inputs), len(comm.out_shapes)

    def fused(*refs):
        ins, cins = refs[:n_in], refs[n_in:n_in + ci]
        o0 = n_in + ci
        outs, couts = refs[o0:o0 + n_out], refs[o0 + n_out:o0 + n_out + co]
        rest = refs[o0 + n_out + co:]
        scr, cscr = rest[:len(rest) - len(comm.scratch)], rest[len(rest) - len(comm.scratch):]
        ids = [pl.program_id(a) for a in range(len(grid))]
        first, last = ids[0] == 0, ids[0] == grid[0] - 1
        for a in range(1, len(grid)):
            first, last = first & (ids[a] == 0), last & (ids[a] == grid[a] - 1)

        @pl.when(first)
        def _():
            comm.start(cins, couts, cscr)

        body(*ins, *outs, *scr)

        @pl.when(last)
        def _():
            comm.wait(cins, couts, cscr)

    return fused


def _comm_specs(comm):
    if comm is None:
        return [], [], [], []
    return list(comm.inputs), [ANY] * len(comm.inputs), [ANY] * len(comm.out_shapes), list(comm.out_shapes)


def _run_comm(comm, name):
    ci, co = len(comm.inputs), len(comm.out_shapes)

    def body(*refs):
        comm.start(refs[:ci], refs[ci:ci + co], refs[ci + co:])
        comm.wait(refs[:ci], refs[ci:ci + co], refs[ci + co:])

    return pl.pallas_call(body, name=name, in_specs=[ANY] * ci, out_specs=[ANY] * co, out_shape=list(comm.out_shapes),
                          scratch_shapes=list(comm.scratch))(*comm.inputs)


def _matmul(a, b, mode, out_dtype, bm, bn, bk, name, a_fn=None, col_blocks=0, comm=None):
    if mode == "nn":
        (m, k), n, dims = a.shape, b.shape[1], NN
    elif mode == "nt":
        (m, k), n, dims = a.shape, b.shape[0], NT
    else:
        (k, m), n, dims = a.shape, b.shape[1], TN
    bm, bn, bk = min(bm, m), min(bn, n), min(bk, k)
    assert m % bm == 0 and n % bn == 0 and k % bk == 0, (name, m, n, k, bm, bn, bk)
    nk = k // bk
    if mode == "nn":
        a_spec = pl.BlockSpec((bm, bk), lambda i, j, kk: (i, kk))
        b_spec = pl.BlockSpec((bk, bn), lambda i, j, kk: (kk, j))
    elif mode == "nt":
        a_spec = pl.BlockSpec((bm, bk), lambda i, j, kk: (i, kk))
        b_spec = pl.BlockSpec((bn, bk), lambda i, j, kk: (j, kk))
    else:
        a_spec = pl.BlockSpec((bk, bm), lambda i, j, kk: (kk, i))
        b_spec = pl.BlockSpec((bk, bn), lambda i, j, kk: (kk, j))

    def body(a_ref, b_ref, o_ref, acc_ref):
        kk = pl.program_id(2)

        @pl.when(kk == 0)
        def _():
            acc_ref[...] = jnp.zeros_like(acc_ref)

        av = a_ref[...]
        if a_fn is not None:
            av = a_fn(av.astype(F32))
        acc_ref[...] += _dot(av, b_ref[...], dims)

        @pl.when(kk == nk - 1)
        def _():
            if col_blocks:
                for s in range(bn // slab):
                    o_ref[s] = acc_ref[:, s * slab:(s + 1) * slab].astype(o_ref.dtype)
            else:
                o_ref[...] = acc_ref[...].astype(o_ref.dtype)

    if col_blocks:
        slab = n // col_blocks
        assert n % col_blocks == 0 and bn % slab == 0, (name, n, col_blocks, bn)
        out_spec = pl.BlockSpec((bn // slab, bm, slab), lambda i, j, kk: (j, i, 0))
        out_shape = jax.ShapeDtypeStruct((col_blocks, m, slab), out_dtype)
    else:
        out_spec = pl.BlockSpec((bm, bn), lambda i, j, kk: (i, j))
        out_shape = jax.ShapeDtypeStruct((m, n), out_dtype)
    grid = (m // bm, n // bn, nk)
    c_in, c_in_specs, c_out_specs, c_out_shapes = _comm_specs(comm)
    res = pl.pallas_call(
        _fuse_comm(body, grid, 2, 1, comm), name=name,
        grid=grid,
        in_specs=[a_spec, b_spec] + c_in_specs,
        out_specs=[out_spec] + c_out_specs,
        out_shape=[out_shape] + c_out_shapes,
        scratch_shapes=[pltpu.VMEM((bm, bn), F32)] + (list(comm.scratch) if comm else []),
        compiler_params=_params(("arbitrary",) * 3 if comm else ("parallel", "parallel", "arbitrary")),
    )(a, b, *c_in)
    return res if comm else res[0]


CONV_STRIP = 16


def _conv_pre(ext_ref, w_ref, b_ref, r0, rows):
    acc = b_ref[...] + w_ref[0:1, :] * ext_ref[pl.ds(r0 + SUBLANES - (CONV_K - 1), rows), :]
    for k in range(1, CONV_K):
        acc = acc + w_ref[k:k + 1, :] * ext_ref[pl.ds(r0 + SUBLANES - (CONV_K - 1) + k, rows), :]
    return acc


def _in_proj(xb, w_main, conv_w8, conv_b, cd, seq_len, bm, bn, comm):
    t, d = xb.shape
    pw = w_main.shape[1]
    bm, bn = min(bm, seq_len), min(bn, d)
    assert t % bm == 0 and seq_len % bm == 0 and pw % bn == 0 and cd % bn == 0
    ncj = cd // bn
    tiles_per_seq = seq_len // bm

    def body(x_ref, w_ref, cw_ref, cb_ref, p_ref, xbc_ref, ext_ref, carry_ref):
        i = pl.program_id(0)
        j = pl.program_id(1)

        def conv_previous():
            ext = ext_ref.at[(j + 1) % 2]
            for r0 in range(0, bm, CONV_STRIP):
                acc = _conv_pre(ext, cw_ref, cb_ref, r0, CONV_STRIP)
                xbc_ref[r0:r0 + CONV_STRIP, :] = (acc * _sigmoid(acc)).astype(xbc_ref.dtype)

        def project(stash):
            pq = _dot(x_ref[...], w_ref[...]).astype(BF16)
            p_ref[...] = pq
            if stash:
                ext = ext_ref.at[j % 2]
                jc = jnp.minimum(j, ncj - 1)
                ext[0:SUBLANES, :] = jnp.where((i % tiles_per_seq) == 0, 0.0, carry_ref[jc])
                ext[SUBLANES:, :] = pq.astype(F32)
                carry_ref[jc] = ext[bm:bm + SUBLANES, :]

        @pl.when(j == 0)
        def _():
            project(True)

        @pl.when((j >= 1) & (j < ncj))
        def _():
            conv_previous()
            project(True)

        @pl.when(j == ncj)
        def _():
            conv_previous()
            project(False)

        @pl.when(j > ncj)
        def _():
            project(False)

    assert pw // bn > ncj
    grid = (t // bm, pw // bn)
    conv_col = lambda i, j: (0, jnp.clip(j - 1, 0, ncj - 1))
    c_in, c_in_specs, c_out_specs, c_out_shapes = _comm_specs(comm)
    return pl.pallas_call(
        _fuse_comm(body, grid, 4, 2, comm), name="in_proj",
        grid=grid,
        in_specs=[pl.BlockSpec((bm, d), lambda i, j: (i, 0)), pl.BlockSpec((d, bn), lambda i, j: (0, j)),
                  pl.BlockSpec((SUBLANES, bn), conv_col), pl.BlockSpec((1, bn), conv_col)] + c_in_specs,
        out_specs=[pl.BlockSpec((bm, bn), lambda i, j: (i, j)),
                   pl.BlockSpec((bm, bn), lambda i, j: (i, jnp.clip(j - 1, 0, ncj - 1)))] + c_out_specs,
        out_shape=[jax.ShapeDtypeStruct((t, pw), BF16), jax.ShapeDtypeStruct((t, cd), BF16)] + c_out_shapes,
        scratch_shapes=[pltpu.VMEM((2, bm + SUBLANES, bn), F32), pltpu.VMEM((ncj, SUBLANES, bn), F32)]
        + (list(comm.scratch) if comm else []),
        compiler_params=_params(("arbitrary", "arbitrary")),
    )(xb, w_main, conv_w8, conv_b, *c_in)


def _conv_bwd(proj, dxbc, conv_w8, conv_b, n_seq_chunks, col0, width, ct, L, name):
    t = proj.shape[0]
    nbc = t // L
    hb = L // SUBLANES
    ct = min(ct, width)
    assert col0 % ct == 0 and width % ct == 0
    cb0 = col0 // ct
    last_hb = t // SUBLANES - 1

    def body(x_ref, xb_ref, xa_ref, d_ref, da_ref, w_ref, b_ref, o_ref, dw_ref, db_ref, ext_ref, dc_ref):
        bc = pl.program_id(1)
        first = (bc % n_seq_chunks) == 0
        last = (bc % n_seq_chunks) == n_seq_chunks - 1

        @pl.when(bc == 0)
        def _():
            dw_ref[...] = jnp.zeros_like(dw_ref)
            db_ref[...] = jnp.zeros_like(db_ref)

        ext_ref[0:SUBLANES, :] = jnp.where(first, 0.0, xb_ref[...].astype(F32))
        ext_ref[SUBLANES:SUBLANES + L, :] = x_ref[...].astype(F32)
        ext_ref[SUBLANES + L:, :] = xa_ref[...].astype(F32)

        def dsilu_at(r0, rows):
            acc = _conv_pre(ext_ref, w_ref, b_ref, r0, rows)
            sg = _sigmoid(acc)
            return sg * (1.0 + acc * (1.0 - sg))

        for r0 in range(0, L, CONV_STRIP):
            dc_ref[r0:r0 + CONV_STRIP, :] = d_ref[r0:r0 + CONV_STRIP, :].astype(F32) * dsilu_at(r0, CONV_STRIP)
        dc_ref[L:, :] = jnp.where(last, 0.0, da_ref[...].astype(F32)) * dsilu_at(L, SUBLANES)
        fold = lambda v: v[0:SUBLANES] + v[SUBLANES:CONV_STRIP]
        dws = [jnp.zeros((SUBLANES, ct), F32) for _ in range(CONV_K)]
        dbs = jnp.zeros((SUBLANES, ct), F32)
        for r0 in range(0, L, CONV_STRIP):
            dc = dc_ref[r0:r0 + CONV_STRIP, :]
            dx = w_ref[CONV_K - 1:CONV_K, :] * dc
            for k in range(CONV_K - 1):
                dx = dx + w_ref[k:k + 1, :] * dc_ref[pl.ds(r0 + CONV_K - 1 - k, CONV_STRIP), :]
            o_ref[r0:r0 + CONV_STRIP, :] = dx.astype(o_ref.dtype)
            for k in range(CONV_K):
                dws[k] = dws[k] + fold(dc * ext_ref[pl.ds(r0 + SUBLANES - (CONV_K - 1) + k, CONV_STRIP), :])
            dbs = dbs + fold(dc)
        for k in range(CONV_K):
            dw_ref[k:k + 1, :] += _colsum(dws[k])
        db_ref[0:1, :] += _colsum(dbs)

    return pl.pallas_call(
        body, name=name,
        grid=(width // ct, nbc),
        in_specs=[
            pl.BlockSpec((L, ct), lambda j, i: (i, cb0 + j)),
            pl.BlockSpec((SUBLANES, ct), lambda j, i: (jnp.maximum(i * hb - 1, 0), cb0 + j)),
            pl.BlockSpec((SUBLANES, ct), lambda j, i: (jnp.minimum((i + 1) * hb, last_hb), cb0 + j)),
            pl.BlockSpec((L, ct), lambda j, i: (i, j)),
            pl.BlockSpec((SUBLANES, ct), lambda j, i: (jnp.minimum((i + 1) * hb, last_hb), j)),
            pl.BlockSpec((SUBLANES, ct), lambda j, i: (0, cb0 + j)),
            pl.BlockSpec((1, ct), lambda j, i: (0, cb0 + j)),
        ],
        out_specs=[
            pl.BlockSpec((L, ct), lambda j, i: (i, j)),
            pl.BlockSpec((SUBLANES, ct), lambda j, i: (0, j)),
            pl.BlockSpec((SUBLANES, ct), lambda j, i: (0, j)),
        ],
        out_shape=[
            jax.ShapeDtypeStruct((t, width), BF16),
            jax.ShapeDtypeStruct((SUBLANES, width), F32),
            jax.ShapeDtypeStruct((SUBLANES, width), F32),
        ],
        scratch_shapes=[pltpu.VMEM((L + 2 * SUBLANES, ct), F32), pltpu.VMEM((L + SUBLANES, ct), F32)],
        compiler_params=_params(("parallel", "arbitrary")),
    )(proj, proj, proj, dxbc, dxbc, conv_w8, conv_b)


def _cumsum_rows(x, reverse=False):
    n = x.shape[0]
    row = lax.broadcasted_iota(jnp.int32, x.shape, 0)
    s = 1
    while s < n:
        if reverse:
            x = x + jnp.where(row < n - s, pltpu.roll(x, n - s, 0), 0.0)
        else:
            x = x + jnp.where(row >= s, pltpu.roll(x, s, 0), 0.0)
        s *= 2
    return x


def _ssd_scalars(dtr, dtb, alog):
    pre = dtr + dtb
    dt = jnp.maximum(pre, 0.0) + jnp.log(1.0 + jnp.exp(-jnp.abs(pre)))
    a = -jnp.exp(alog)
    acs = _cumsum_rows(dt * a) * LOG2E
    return pre, dt, a, acs


SSD_GROUPS_PER_STEP = 8
LOG2E = 1.4426950408889634


def _ssd_group_common(xs, dt_s, acs_s, e):
    L = xs.shape[0]
    dt_x = _dot_exact01(dt_s, e)
    acs_x = _dot_exact01(acs_s, e)
    e_x = jnp.exp2(acs_x)
    a_last = acs_x[L - 1:L, :]
    dec_x = jnp.exp2(a_last - acs_x)
    return dt_x, acs_x, e_x, dec_x


def _decay_matrix(acs_x, acs_t, r, tri):
    col = acs_x[:, r * HEAD_DIM:r * HEAD_DIM + 1]
    rowv = acs_t[r * HEAD_DIM:r * HEAD_DIM + 1, :]
    return jnp.exp2(jnp.where(tri, col - rowv, NEG_BIG))


def _head_mask(r, gw, dtype):
    lane = lax.broadcasted_iota(jnp.int32, (1, gw), 1)
    return ((lane >= r * HEAD_DIM) & (lane < (r + 1) * HEAD_DIM)).astype(dtype)


def _ssd_fwd(xbc, proj, dt_raw, dtb, alog, dskip_x, normw, emat, bl, inner, z_col0):
    t = xbc.shape[0]
    L = CHUNK
    nc = t // bl // L
    G = GROUPS
    gw = inner // G
    hpg = gw // HEAD_DIM
    assert z_col0 % gw == 0
    zb0 = z_col0 // gw
    bb0 = inner // STATE
    cb0 = bb0 + G

    P = SSD_GROUPS_PER_STEP
    assert G % P == 0 and bb0 % P == 0 and cb0 % P == 0 and zb0 % P == 0

    def body(xs_ref, b_ref, c_ref, z_ref, dtr_ref, dtb_ref, alog_ref, dsk_ref, nw_ref, e_ref,
             y_ref, yn_ref, st_ref, h_ref, dt_s, acs_s):
        c = pl.program_id(1)
        gb = pl.program_id(2)

        @pl.when(gb == 0)
        def _():
            _, dt, _, acs = _ssd_scalars(dtr_ref[...], dtb_ref[...], alog_ref[...])
            dt_s[...] = dt
            acs_s[...] = acs

        tri = lax.broadcasted_iota(jnp.int32, (L, L), 0) >= lax.broadcasted_iota(jnp.int32, (L, L), 1)
        lane = lax.broadcasted_iota(jnp.int32, (L, gw), 1)
        for gi in range(P):
            g = gb * P + gi
            cols = slice(gi * gw, (gi + 1) * gw)
            ncol = slice(gi * STATE, (gi + 1) * STATE)

            @pl.when(c == 0)
            def _():
                h_ref[g] = jnp.zeros((STATE, gw), F32)

            xs = xs_ref[:, cols].astype(F32)
            bg = b_ref[:, ncol]
            cg = c_ref[:, ncol]
            dt_x, acs_x, e_x, dec_x = _ssd_group_common(xs, dt_s[...], acs_s[...], e_ref[:, cols])
            xdt = xs * dt_x
            cb = _dot(cg, bg, NT)
            acs_t = acs_x.T
            h = h_ref[g]
            st_ref[0, gi] = h
            y = _dot(cg, h) * e_x + dsk_ref[:, cols] * xs
            for r in range(hpg):
                lm = _decay_matrix(acs_x, acs_t, r, tri)
                m = cb * lm
                xr = jnp.where((lane >= r * HEAD_DIM) & (lane < (r + 1) * HEAD_DIM), xdt, 0.0)
                y = y + _dot(m, xr)
            h_ref[g] = h * e_x[L - 1:L, :] + _dot(bg, xdt * dec_x, TN)
            yq = y.astype(y_ref.dtype)
            y_ref[:, cols] = yq
            z = z_ref[:, cols].astype(F32)
            yg = yq.astype(F32) * (z * _sigmoid(z))
            rs = lax.rsqrt(jnp.mean(yg * yg, axis=-1, keepdims=True) + RMS_EPS)
            yn_ref[:, cols] = (yg * rs * nw_ref[:, cols]).astype(yn_ref.dtype)

    return pl.pallas_call(
        body, name="ssd_fwd",
        grid=(bl, nc, G // P),
        in_specs=[
            pl.BlockSpec((L, P * gw), lambda b, c, g: (b * nc + c, g)),
            pl.BlockSpec((L, P * STATE), lambda b, c, g: (b * nc + c, bb0 // P + g)),
            pl.BlockSpec((L, P * STATE), lambda b, c, g: (b * nc + c, cb0 // P + g)),
            pl.BlockSpec((L, P * gw), lambda b, c, g: (b * nc + c, zb0 // P + g)),
            pl.BlockSpec((L, LANES), lambda b, c, g: (b * nc + c, 0)),
            pl.BlockSpec((1, LANES), lambda b, c, g: (0, 0)),
            pl.BlockSpec((1, LANES), lambda b, c, g: (0, 0)),
            pl.BlockSpec((1, P * gw), lambda b, c, g: (0, g)),
            pl.BlockSpec((1, P * gw), lambda b, c, g: (0, g)),
            pl.BlockSpec((LANES, P * gw), lambda b, c, g: (0, g)),
        ],
        out_specs=[
            pl.BlockSpec((L, P * gw), lambda b, c, g: (b * nc + c, g)),
            pl.BlockSpec((L, P * gw), lambda b, c, g: (b * nc + c, g)),
            pl.BlockSpec((1, P, STATE, gw), lambda b, c, g: (b * nc + c, g, 0, 0)),
        ],
        out_shape=[
            jax.ShapeDtypeStruct((t, inner), BF16),
            jax.ShapeDtypeStruct((t, inner), BF16),
            jax.ShapeDtypeStruct((bl * nc, G, STATE, gw), F32),
        ],
        scratch_shapes=[pltpu.VMEM((G, STATE, gw), F32), pltpu.VMEM((L, LANES), F32), pltpu.VMEM((L, LANES), F32)],
        compiler_params=_params(("arbitrary", "arbitrary", "arbitrary")),
    )(xbc, xbc, xbc, proj, dt_raw, dtb, alog, dskip_x, normw, emat)


def _ssd_bwd(xbc, proj, dt_raw, y, dyn, states, dtb, alog, dskip_x, normw, emat, emat_t, bl, inner, z_col0,
             comm=None):
    t = xbc.shape[0]
    L = CHUNK
    nc = t // bl // L
    G = GROUPS
    gw = inner // G
    hpg = gw // HEAD_DIM
    zb0 = z_col0 // gw
    bb0 = inner // STATE
    cb0 = bb0 + G
    P = SSD_GROUPS_PER_STEP

    def rc(j):
        return nc - 1 - j

    def body(xs_ref, b_ref, c_ref, z_ref, dtr_ref, y_ref, dyn_ref, st_ref, dtb_ref, alog_ref, dsk_ref,
             nw_ref, e_ref, et_ref,
             dxs_ref, db_ref, dc_ref, dz_ref, ddt_ref, dnw_ref, dsk_acc, dalog_acc, ddtb_acc,
             dh_ref, pre_s, dt_s, acs_s, wacs_s, wdt_s):
        b = pl.program_id(0)
        j = pl.program_id(1)
        gb = pl.program_id(2)

        @pl.when((b == 0) & (j == 0) & (gb == 0))
        def _():
            dsk_acc[...] = jnp.zeros_like(dsk_acc)
            dalog_acc[...] = jnp.zeros_like(dalog_acc)
            ddtb_acc[...] = jnp.zeros_like(ddtb_acc)

        @pl.when(gb == 0)
        def _():
            pre, dt, _, acs = _ssd_scalars(dtr_ref[...], dtb_ref[...], alog_ref[...])
            pre_s[...] = pre
            dt_s[...] = dt
            acs_s[...] = acs
            wacs_s[...] = jnp.zeros_like(wacs_s)
            wdt_s[...] = jnp.zeros_like(wdt_s)

        tri = lax.broadcasted_iota(jnp.int32, (L, L), 0) >= lax.broadcasted_iota(jnp.int32, (L, L), 1)
        rowi = lax.broadcasted_iota(jnp.int32, (L, gw), 0)
        for gi in range(P):
            g = gb * P + gi
            cols = slice(gi * gw, (gi + 1) * gw)
            ncol = slice(gi * STATE, (gi + 1) * STATE)

            @pl.when((b == 0) & (j == 0))
            def _():
                dnw_ref[g] = jnp.zeros((SUBLANES, gw), F32)

            @pl.when(j == 0)
            def _():
                dh_ref[g] = jnp.zeros((STATE, gw), F32)

            xs = xs_ref[:, cols].astype(F32)
            bg = b_ref[:, ncol]
            cg = c_ref[:, ncol]
            dt_x, acs_x, e_x, dec_x = _ssd_group_common(xs, dt_s[...], acs_s[...], e_ref[:, cols])
            xdt = xs * dt_x
            xdt_b = xdt.astype(BF16)
            cb = _dot(cg, bg, NT)
            acs_t = acs_x.T
            h = st_ref[0, gi]
            hb16 = h.astype(BF16)
            dsk = dsk_ref[:, cols]

            yv = y_ref[:, cols].astype(F32)
            z = z_ref[:, cols].astype(F32)
            sgz = _sigmoid(z)
            sz = z * sgz
            yg = yv * sz
            rs = lax.rsqrt(jnp.mean(yg * yg, axis=-1, keepdims=True) + RMS_EPS)
            yhat = yg * rs
            dyn_v = dyn_ref[:, cols].astype(F32)
            dnw_ref[g] += _colsum(dyn_v * yhat)
            dyh = dyn_v * nw_ref[:, cols]
            dyg = rs * (dyh - yhat * jnp.mean(dyh * yhat, axis=-1, keepdims=True))
            dy = dyg * sz
            dz_ref[:, cols] = (dyg * yv * (sgz * (1.0 + z * (1.0 - sgz)))).astype(dz_ref.dtype)

            dy_b = dy.astype(BF16)
            dcb = jnp.zeros((L, L), F32)
            dxdt_d = jnp.zeros((L, gw), F32)
            ydiag = jnp.zeros((L, gw), F32)
            for r in range(hpg):
                lm = _decay_matrix(acs_x, acs_t, r, tri)
                m = (cb * lm).astype(BF16)
                hm = _head_mask(r, gw, BF16)
                dyr = dy_b * hm
                xr = xdt_b * hm
                ydiag = ydiag + _dot(m, xr)
                dcb = dcb + _dot(dyr, xdt_b, NT) * lm
                dxdt_d = dxdt_d + _dot(m, dyr, TN)
            dh = dh_ref[g]
            dh16 = dh.astype(BF16)
            xdec_b = (xdt * dec_x).astype(BF16)
            bdh = _dot(bg, dh16)
            dxdt = dxdt_d + dec_x * bdh
            dcb16 = dcb.astype(BF16)
            dye = (dy * e_x).astype(BF16)
            db_ref[:, ncol] = (_dot(dcb16, cg, TN) + _dot(xdec_b, dh16, NT)).astype(db_ref.dtype)
            dc_ref[:, ncol] = (_dot(dcb16, bg) + _dot(dye, hb16, NT)).astype(dc_ref.dtype)
            dprev = _dot(cg, dye, TN)
            cd_row = e_x[L - 1:L, :]
            s_new = _dot(bg, xdec_b, TN)
            last_term = _colsum(dh16.astype(F32) * s_new) + _colsum(dh * h) * cd_row
            yoff = _dot(cg, hb16) * e_x
            wfold = (dy_b.astype(F32) * ydiag + dy * yoff - dxdt_d * xdt_b.astype(F32) - bdh * xdec_b.astype(F32)
                     + jnp.where(rowi == L - 1, last_term, 0.0))
            et = et_ref[cols, :]
            wacs_s[...] += _dot_exact01(wfold, et)
            wdt_s[...] += _dot_exact01(dxdt * xs, et)
            dsk_acc[...] += _dot_exact01(jnp.broadcast_to(_colsum(dy * xs), (SUBLANES, gw)), et)
            dxs_ref[:, cols] = (dsk * dy + dxdt * dt_x).astype(dxs_ref.dtype)
            dh_ref[g] = dprev + cd_row * dh

        @pl.when(gb == G // P - 1)
        def _():
            a = -jnp.exp(alog_ref[...])
            dda = _cumsum_rows(wacs_s[...], reverse=True)
            ddt = wdt_s[...] + dda * a
            ddt_raw = ddt * _sigmoid(pre_s[...])
            ddt_ref[...] = ddt_raw
            dalog_acc[...] += _colsum(dda * dt_s[...]) * a
            ddtb_acc[...] += _colsum(ddt_raw)

    def cidx(b, j):
        return b * nc + rc(j)

    accs = lambda shape: pl.BlockSpec(shape, lambda b, j, g: tuple(0 for _ in shape))
    grid = (bl, nc, G // P)
    c_in, c_in_specs, c_out_specs, c_out_shapes = _comm_specs(comm)
    return pl.pallas_call(
        _fuse_comm(body, grid, 14, 9, comm), name="ssd_bwd",
        grid=grid,
        in_specs=[
            pl.BlockSpec((L, P * gw), lambda b, j, g: (cidx(b, j), g)),
            pl.BlockSpec((L, P * STATE), lambda b, j, g: (cidx(b, j), bb0 // P + g)),
            pl.BlockSpec((L, P * STATE), lambda b, j, g: (cidx(b, j), cb0 // P + g)),
            pl.BlockSpec((L, P * gw), lambda b, j, g: (cidx(b, j), zb0 // P + g)),
            pl.BlockSpec((L, LANES), lambda b, j, g: (cidx(b, j), 0)),
            pl.BlockSpec((L, P * gw), lambda b, j, g: (cidx(b, j), g)),
            pl.BlockSpec((L, P * gw), lambda b, j, g: (cidx(b, j), g)),
            pl.BlockSpec((1, P, STATE, gw), lambda b, j, g: (cidx(b, j), g, 0, 0)),
            pl.BlockSpec((1, LANES), lambda b, j, g: (0, 0)),
            pl.BlockSpec((1, LANES), lambda b, j, g: (0, 0)),
            pl.BlockSpec((1, P * gw), lambda b, j, g: (0, g)),
            pl.BlockSpec((1, P * gw), lambda b, j, g: (0, g)),
            pl.BlockSpec((LANES, P * gw), lambda b, j, g: (0, g)),
            pl.BlockSpec((P * gw, LANES), lambda b, j, g: (g, 0)),
        ] + c_in_specs,
        out_specs=[
            pl.BlockSpec((L, P * gw), lambda b, j, g: (cidx(b, j), g)),
            pl.BlockSpec((L, P * STATE), lambda b, j, g: (cidx(b, j), g)),
            pl.BlockSpec((L, P * STATE), lambda b, j, g: (cidx(b, j), g)),
            pl.BlockSpec((L, P * gw), lambda b, j, g: (cidx(b, j), g)),
            pl.BlockSpec((L, LANES), lambda b, j, g: (cidx(b, j), 0)),
            accs((G, SUBLANES, gw)),
            accs((SUBLANES, LANES)),
            accs((SUBLANES, LANES)),
            accs((SUBLANES, LANES)),
        ] + c_out_specs,
        out_shape=[
            jax.ShapeDtypeStruct((t, inner), BF16),
            jax.ShapeDtypeStruct((t, G * STATE), BF16),
            jax.ShapeDtypeStruct((t, G * STATE), BF16),
            jax.ShapeDtypeStruct((t, inner), BF16),
            jax.ShapeDtypeStruct((t, LANES), F32),
            jax.ShapeDtypeStruct((G, SUBLANES, gw), F32),
            jax.ShapeDtypeStruct((SUBLANES, LANES), F32),
            jax.ShapeDtypeStruct((SUBLANES, LANES), F32),
            jax.ShapeDtypeStruct((SUBLANES, LANES), F32),
        ] + c_out_shapes,
        scratch_shapes=[pltpu.VMEM((G, STATE, gw), F32)] + [pltpu.VMEM((L, LANES), F32)] * 5
        + (list(comm.scratch) if comm else []),
        compiler_params=_params(("arbitrary", "arbitrary", "arbitrary")),
    )(xbc, xbc, xbc, proj, dt_raw, y, dyn, states, dtb, alog, dskip_x, normw, emat, emat_t, *c_in)


def _pool_window(u, w, anti):
    n = u.shape[0]
    row = lax.broadcasted_iota(jnp.int32, u.shape, 0)
    acc = u
    s = 1
    while s < w:
        if anti:
            acc = acc + jnp.where(row < n - s, pltpu.roll(acc, n - s, 0), 0.0)
        else:
            acc = acc + jnp.where(row >= s, pltpu.roll(acc, s, 0), 0.0)
        s *= 2
    return acc


def _pool_cnt(shape, w):
    row = lax.broadcasted_iota(jnp.int32, shape, 0)
    return jnp.minimum(row + 1, w).astype(F32)


def _pool_fwd(proj, wpg, bl, d, u_col0):
    t = proj.shape[0]
    s = t // bl
    pg = len(POOL_WINDOWS)
    cg = d // pg
    ub0 = u_col0 // d

    def body(u_ref, w_ref, o_ref):
        for gi, w in enumerate(POOL_WINDOWS):
            u = u_ref[:, gi * cg:(gi + 1) * cg].astype(F32)
            pooled = _pool_window(u, w, False) / _pool_cnt(u.shape, w) - u
            o_ref[:, gi * cg:(gi + 1) * cg] = _dot(pooled, w_ref[gi]).astype(o_ref.dtype)

    return pl.pallas_call(
        body, name="pool_fwd",
        grid=(bl,),
        in_specs=[pl.BlockSpec((s, d), lambda b: (b, ub0)), pl.BlockSpec((pg, cg, cg), lambda b: (0, 0, 0))],
        out_specs=pl.BlockSpec((s, d), lambda b: (b, 0)),
        out_shape=jax.ShapeDtypeStruct((t, d), BF16),
        compiler_params=_params(("parallel",)),
    )(proj, wpg)


def _pool_bwd(proj, dyp, wpg, bl, d, u_col0):
    t = proj.shape[0]
    s = t // bl
    pg = len(POOL_WINDOWS)
    cg = d // pg
    ub0 = u_col0 // d

    def body(u_ref, dy_ref, w_ref, du_ref, dw_ref):
        @pl.when(pl.program_id(0) == 0)
        def _():
            dw_ref[...] = jnp.zeros_like(dw_ref)

        for gi, w in enumerate(POOL_WINDOWS):
            u = u_ref[:, gi * cg:(gi + 1) * cg].astype(F32)
            cnt = _pool_cnt(u.shape, w)
            pooled = _pool_window(u, w, False) / cnt - u
            dy = dy_ref[:, gi * cg:(gi + 1) * cg]
            dw_ref[gi] += _dot(pooled, dy, TN)
            dp = _dot(dy, w_ref[gi], NT)
            du_ref[:, gi * cg:(gi + 1) * cg] = (_pool_window(dp / cnt, w, True) - dp).astype(du_ref.dtype)

    return pl.pallas_call(
        body, name="pool_bwd",
        grid=(bl,),
        in_specs=[pl.BlockSpec((s, d), lambda b: (b, ub0)), pl.BlockSpec((s, d), lambda b: (b, 0)),
                  pl.BlockSpec((pg, cg, cg), lambda b: (0, 0, 0))],
        out_specs=[pl.BlockSpec((s, d), lambda b: (b, 0)), pl.BlockSpec((pg, cg, cg), lambda b: (0, 0, 0))],
        out_shape=[jax.ShapeDtypeStruct((t, d), BF16), jax.ShapeDtypeStruct((pg, cg, cg), F32)],
        compiler_params=_params(("arbitrary",)),
    )(proj, dyp, wpg)


def _merge_fwd(proj, ypr, yssd, x, w_out, b_gates, pool_scale, d, lg_col0, tm):
    t = x.shape[0]
    lb0 = lg_col0 // (2 * d)

    def body(lg_ref, yp_ref, ys_ref, x_ref, w_ref, bg_ref, ps_ref, mg_ref, r1_ref):
        lg = lg_ref[...].astype(F32) + bg_ref[...]
        ga = _sigmoid(lg[:, :d])
        gb = _sigmoid(lg[:, d:])
        merged = ga * (yp_ref[...].astype(F32) * ps_ref[...]) + gb * ys_ref[...].astype(F32)
        mg_ref[...] = merged.astype(mg_ref.dtype)
        r1_ref[...] = ALPHA * x_ref[...] + _dot(mg_ref[...], w_ref[...])

    row = lambda w: pl.BlockSpec((tm, w), lambda i: (i, 0))
    full = lambda a: pl.BlockSpec(a.shape, lambda i: (0, 0))
    return pl.pallas_call(
        body, name="merge_fwd",
        grid=(t // tm,),
        in_specs=[pl.BlockSpec((tm, 2 * d), lambda i: (i, lb0)), row(d), row(d), row(d), full(w_out), full(b_gates),
                  full(pool_scale)],
        out_specs=[row(d), row(d)],
        out_shape=[jax.ShapeDtypeStruct((t, d), BF16), jax.ShapeDtypeStruct((t, d), F32)],
        compiler_params=_params(("parallel",)),
    )(proj, ypr, yssd, x, w_out, b_gates, pool_scale)


def _merge_bwd(dr1, proj, ypr, yssd, w_out, b_gates, pool_scale, d, lg_col0, tm):
    t = dr1.shape[0]
    lb0 = lg_col0 // (2 * d)

    def body(dr_ref, lg_ref, yp_ref, ys_ref, w_ref, bg_ref, ps_ref, dlg_ref, dyp_ref, dys_ref, dbg_ref, dps_ref):
        @pl.when(pl.program_id(0) == 0)
        def _():
            dbg_ref[...] = jnp.zeros_like(dbg_ref)
            dps_ref[...] = jnp.zeros_like(dps_ref)

        dm = _dot(dr_ref[...], w_ref[...], NT)
        lg = lg_ref[...].astype(F32) + bg_ref[...]
        ga = _sigmoid(lg[:, :d])
        gb = _sigmoid(lg[:, d:])
        ypr_v = yp_ref[...].astype(F32)
        ys_v = ys_ref[...].astype(F32)
        ps = ps_ref[...]
        dga = dm * ypr_v * ps
        dla = dga * ga * (1.0 - ga)
        dlb = dm * ys_v * gb * (1.0 - gb)
        dlg_ref[:, :d] = dla.astype(dlg_ref.dtype)
        dlg_ref[:, d:] = dlb.astype(dlg_ref.dtype)
        dyp_ref[...] = (dm * ga * ps).astype(dyp_ref.dtype)
        dys_ref[...] = (dm * gb).astype(dys_ref.dtype)
        dbg_ref[0:1, :d] += _colsum(dla)
        dbg_ref[0:1, d:] += _colsum(dlb)
        dps_ref[0:1, :] += _colsum(dm * ga * ypr_v)

    row = lambda w: pl.BlockSpec((tm, w), lambda i: (i, 0))
    full = lambda a: pl.BlockSpec(a.shape, lambda i: (0, 0))
    acc = lambda w: pl.BlockSpec((SUBLANES, w), lambda i: (0, 0))
    return pl.pallas_call(
        body, name="merge_bwd",
        grid=(t // tm,),
        in_specs=[row(d), pl.BlockSpec((tm, 2 * d), lambda i: (i, lb0)), row(d), row(d), full(w_out), full(b_gates),
                  full(pool_scale)],
        out_specs=[row(2 * d), row(d), row(d), acc(2 * d), acc(d)],
        out_shape=[jax.ShapeDtypeStruct((t, 2 * d), BF16), jax.ShapeDtypeStruct((t, d), BF16),
                   jax.ShapeDtypeStruct((t, d), BF16), jax.ShapeDtypeStruct((SUBLANES, 2 * d), F32),
                   jax.ShapeDtypeStruct((SUBLANES, d), F32)],
        compiler_params=_params(("arbitrary",)),
    )(dr1, proj, ypr, yssd, w_out, b_gates, pool_scale)


def _mlp_fwd(r1, target, w_up, w_down, ln1_g, ln1_b, ln2_g, ln2_b, tm):
    t, d = r1.shape
    nf, _, tf = w_up.shape
    ff = nf * tf

    def body(r1_ref, tg_ref, wu_ref, wd_ref, g1_ref, b1_ref, g2_ref, b2_ref,
             up_ref, h1_ref, dr2_ref, loss_ref, dg2_ref, db2_ref, h1f, acc):
        i = pl.program_id(0)
        f = pl.program_id(1)

        @pl.when((i == 0) & (f == 0))
        def _():
            loss_ref[...] = jnp.zeros_like(loss_ref)
            dg2_ref[...] = jnp.zeros_like(dg2_ref)
            db2_ref[...] = jnp.zeros_like(db2_ref)

        @pl.when(f == 0)
        def _():
            xhat, _ = _ln_fwd(r1_ref[...])
            h1 = xhat * g1_ref[...] + b1_ref[...]
            h1f[...] = h1
            h1_ref[...] = h1.astype(h1_ref.dtype)
            acc[...] = jnp.zeros_like(acc)

        up_ref[...] = _dot(h1_ref[...], wu_ref[0]).astype(up_ref.dtype)
        upq = jnp.maximum(up_ref[...].astype(F32), 0.0)
        acc[...] += _dot(upq * upq, wd_ref[...])

        @pl.when(f == nf - 1)
        def _():
            xhat, rstd = _ln_fwd(ALPHA * h1f[...] + acc[...])
            g2 = g2_ref[...]
            diff = xhat * g2 + b2_ref[...] - tg_ref[...]
            loss_ref[...] += 0.5 / d * jnp.sum(diff * diff)
            dh2 = diff * (1.0 / d)
            dg2_ref[0:1, :] += _colsum(dh2 * xhat)
            db2_ref[0:1, :] += _colsum(dh2)
            dr2_ref[...] = _ln_bwd(dh2, xhat, rstd, g2).astype(dr2_ref.dtype)

    row = pl.BlockSpec((tm, d), lambda i, f: (i, 0))
    vec = pl.BlockSpec((1, d), lambda i, f: (0, 0))
    acc8 = pl.BlockSpec((SUBLANES, d), lambda i, f: (0, 0))
    return pl.pallas_call(
        body, name="mlp_fwd",
        grid=(t // tm, nf),
        in_specs=[row, row, pl.BlockSpec((1, d, tf), lambda i, f: (f, 0, 0)), pl.BlockSpec((tf, d), lambda i, f: (f, 0)),
                  vec, vec, vec, vec],
        out_specs=[pl.BlockSpec((tm, tf), lambda i, f: (i, f)), row, row,
                   pl.BlockSpec((SUBLANES, LANES), lambda i, f: (0, 0)), acc8, acc8],
        out_shape=[jax.ShapeDtypeStruct((t, ff), BF16), jax.ShapeDtypeStruct((t, d), BF16),
                   jax.ShapeDtypeStruct((t, d), BF16), jax.ShapeDtypeStruct((SUBLANES, LANES), F32),
                   jax.ShapeDtypeStruct((SUBLANES, d), F32), jax.ShapeDtypeStruct((SUBLANES, d), F32)],
        scratch_shapes=[pltpu.VMEM((tm, d), F32), pltpu.VMEM((tm, d), F32)],
        compiler_params=_params(("arbitrary", "arbitrary")),
    )(r1, target, w_up, w_down, ln1_g, ln1_b, ln2_g, ln2_b)


def _mlp_bwd(dr2, up, r1, w_up, w_down, ln1_g, tm):
    t, d = r1.shape
    nf, _, tf = w_up.shape
    ff = nf * tf

    def body(dr2_ref, up_ref, r1_ref, wu_ref, wd_ref, g1_ref, dup_ref, dr1_ref, dg1_ref, db1_ref, acc):
        i = pl.program_id(0)
        f = pl.program_id(1)

        @pl.when((i == 0) & (f == 0))
        def _():
            dg1_ref[...] = jnp.zeros_like(dg1_ref)
            db1_ref[...] = jnp.zeros_like(db1_ref)

        @pl.when(f == 0)
        def _():
            acc[...] = jnp.zeros_like(acc)

        dact = _dot(dr2_ref[...], wd_ref[...], NT)
        dup_ref[...] = (dact * 2.0 * jnp.maximum(up_ref[...].astype(F32), 0.0)).astype(dup_ref.dtype)
        acc[...] += _dot(dup_ref[...], wu_ref[0], NT)

        @pl.when(f == nf - 1)
        def _():
            dh1 = acc[...] + ALPHA * dr2_ref[...].astype(F32)
            xhat, rstd = _ln_fwd(r1_ref[...])
            dg1_ref[0:1, :] += _colsum(dh1 * xhat)
            db1_ref[0:1, :] += _colsum(dh1)
            dr1_ref[...] = _ln_bwd(dh1, xhat, rstd, g1_ref[...]).astype(dr1_ref.dtype)

    row = pl.BlockSpec((tm, d), lambda i, f: (i, 0))
    acc8 = pl.BlockSpec((SUBLANES, d), lambda i, f: (0, 0))
    return pl.pallas_call(
        body, name="mlp_bwd",
        grid=(t // tm, nf),
        in_specs=[row, pl.BlockSpec((tm, tf), lambda i, f: (i, f)), row,
                  pl.BlockSpec((1, d, tf), lambda i, f: (f, 0, 0)), pl.BlockSpec((tf, d), lambda i, f: (f, 0)),
                  pl.BlockSpec((1, d), lambda i, f: (0, 0))],
        out_specs=[pl.BlockSpec((tm, tf), lambda i, f: (i, f)), row, acc8, acc8],
        out_shape=[jax.ShapeDtypeStruct((t, ff), BF16), jax.ShapeDtypeStruct((t, d), BF16),
                   jax.ShapeDtypeStruct((SUBLANES, d), F32), jax.ShapeDtypeStruct((SUBLANES, d), F32)],
        scratch_shapes=[pltpu.VMEM((tm, d), F32)],
        compiler_params=_params(("arbitrary", "arbitrary")),
    )(dr2, up, r1, w_up, w_down, ln1_g)


def _dx_kernel(segs, w_main, ddt, w_dt, dr1, tm, tk, comm=None):
    t, d = dr1.shape
    nblk = [s.shape[1] // tk for s in segs]
    starts = [sum(nblk[:i]) for i in range(len(segs))]
    nk = sum(nblk)
    nseg = len(segs)

    def body(*refs):
        seg_refs = refs[:nseg]
        w_ref, ddt_ref, wdt_ref, dr1_ref, o_ref, acc = refs[nseg:]
        k = pl.program_id(1)

        @pl.when(k == 0)
        def _():
            acc[...] = ALPHA * dr1_ref[...].astype(F32) + _dot(ddt_ref[...], wdt_ref[...], NT)

        for si in range(nseg):
            @pl.when((k >= starts[si]) & (k < starts[si] + nblk[si]))
            def _(si=si):
                acc[...] += _dot(seg_refs[si][...], w_ref[...], NT)

        @pl.when(k == nk - 1)
        def _():
            o_ref[...] = acc[...]

    def seg_spec(si):
        return pl.BlockSpec((tm, tk), lambda i, k: (i, jnp.clip(k - starts[si], 0, nblk[si] - 1)))

    row = pl.BlockSpec((tm, d), lambda i, k: (i, 0))
    grid = (t // tm, nk)
    c_in, c_in_specs, c_out_specs, c_out_shapes = _comm_specs(comm)
    return pl.pallas_call(
        _fuse_comm(body, grid, nseg + 4, 1, comm), name="dx",
        grid=grid,
        in_specs=[seg_spec(si) for si in range(nseg)] + [
            pl.BlockSpec((d, tk), lambda i, k: (0, k)), pl.BlockSpec((tm, LANES), lambda i, k: (i, 0)),
            pl.BlockSpec((d, LANES), lambda i, k: (0, 0)), row] + c_in_specs,
        out_specs=[row] + c_out_specs,
        out_shape=[jax.ShapeDtypeStruct((t, d), F32)] + c_out_shapes,
        scratch_shapes=[pltpu.VMEM((tm, d), F32)] + (list(comm.scratch) if comm else []),
        compiler_params=_params(("arbitrary", "arbitrary")),
    )(*segs, w_main, ddt, w_dt, dr1, *c_in)


def _dims(d):
    inner = 2 * d
    heads = inner // HEAD_DIM
    cd = inner + 2 * GROUPS * STATE
    assert heads <= LANES and inner % (GROUPS * LANES) == 0 and d % (len(POOL_WINDOWS) * LANES) == 0
    o_z, o_xbc, o_dt, o_lg = d, d + inner, d + inner + cd, d + inner + cd + heads
    return inner, heads, cd, (o_z, o_xbc, o_dt, o_lg)


def _row(v, width=None):
    v = v.reshape(1, -1).astype(F32)
    if width is not None and v.shape[1] < width:
        v = jnp.pad(v, ((0, 0), (0, width - v.shape[1])))
    return v


def _local_step(x2, tgt2, w, shards, core, bl):
    t, d = x2.shape
    inner, heads, cd, _ = _dims(d)
    gs = GROUPS * STATE
    nc = t // bl // CHUNK
    w_main, w_dt = _w_in_internal(w["w_in_blocks"], d)
    c_z, c_lg, c_u = cd, cd + inner, cd + inner + 2 * d
    conv_w8 = jnp.pad(w["conv_w"].astype(F32), ((0, SUBLANES - CONV_K), (0, 0)))
    conv_b = _row(w["conv_b"])
    dtb, alog = _row(w["dt_bias"], LANES), _row(w["a_log"], LANES)
    dskip_x = _row(jnp.repeat(w["d_skip"].reshape(-1), HEAD_DIM))
    normw = _row(w["ssd_norm_w"])
    col_head = lax.broadcasted_iota(jnp.int32, (LANES, inner), 1) // HEAD_DIM
    emat = (col_head == lax.broadcasted_iota(jnp.int32, (LANES, inner), 0)).astype(BF16)
    emat_t = emat.T
    w_main, w_dt = w_main.astype(BF16), w_dt.astype(BF16)
    b_gates, pool_scale = _row(w["b_gates"]), _row(w["pool_scale"])
    ln1_g, ln1_b, ln2_g, ln2_b = _row(w["ln1_g"]), _row(w["ln1_b"]), _row(w["ln2_g"]), _row(w["ln2_b"])

    tm = min(512, t)
    tk = min(1024, d)
    ct = min(512, d)
    rt = min(512, t // bl)
    nct = t // bl // rt
    mm = functools.partial(_matmul, bm=1024, bn=tk, bk=1024)
    xb = x2.astype(BF16)

    proj, xbc, *gathered = _in_proj(xb, w_main, conv_w8, conv_b, cd, t // bl, 1024, tk,
                                    _all_gather_comm([shards[n] for n in OTHERS]))
    gathered = dict(zip(OTHERS, gathered))
    w_ssd, w_out, w_down = (gathered[n].reshape(-1, d) for n in ("w_ssd_proj", "w_out", "w_down"))
    w_up = gathered["w_up"]
    npg = len(POOL_WINDOWS)
    cg = d // npg
    wpg = gathered["w_pool_group"].reshape(N_DEV, npg, cg // N_DEV, cg).transpose(1, 0, 2, 3).reshape(npg, cg, cg)
    dt_raw = mm(xb, w_dt, "nn", F32, name="in_proj_dt")
    y, yn, states = _ssd_fwd(xbc, proj, dt_raw, dtb, alog, dskip_x, normw, emat, bl, inner, c_z)
    yssd = mm(yn, w_ssd, "nn", BF16, name="ssd_proj")
    ypr = _pool_fwd(proj, wpg, bl, d, c_u)
    merged, r1 = _merge_fwd(proj, ypr, yssd, x2, w_out, b_gates, pool_scale, d, c_lg, tm)
    tmm = min(1024, t)
    up, h1, dr2, loss8, dg2, db2 = _mlp_fwd(r1, tgt2, w_up, w_down, ln1_g, ln1_b, ln2_g, ln2_b, tmm)

    dup, dr1, dg1, db1 = _mlp_bwd(dr2, up, r1, w_up, w_down, ln1_g, tmm)
    relu2 = lambda v: jnp.square(jnp.maximum(v, 0.0))
    g = {}
    g["w_down"] = mm(up, dr2, "tn", BF16, name="dw_down", a_fn=relu2)
    g["w_up"] = mm(h1, dup, "tn", BF16, name="dw_up", col_blocks=N_DEV)
    g["w_out"] = mm(merged, dr1, "tn", BF16, name="dw_out")
    dlg, dyp, dys, dbg, dps = _merge_bwd(dr1, proj, ypr, yssd, w_out, b_gates, pool_scale, d, c_lg, tm)
    du, dwpg = _pool_bwd(proj, dyp, wpg, bl, d, c_u)
    g["w_pool_group"] = dwpg.reshape(npg, N_DEV, cg // N_DEV, cg).transpose(1, 0, 2, 3).reshape(
        N_DEV, npg * cg // N_DEV, cg).astype(BF16)
    dyn = mm(dys, w_ssd, "nt", BF16, name="d_ssd_proj")
    g["w_ssd_proj"] = mm(yn, dys, "tn", BF16, name="dw_ssd_proj")

    def chip_sums(names, tag):
        parts = [g.pop(n).reshape((N_DEV,) + shards_2d[n]) for n in names]
        recv = _run_comm(_rs_sibling_comm(parts), "rs_sibling_" + tag)
        return [_add_pairs(core, p, r, "rs_add_" + n) for n, p, r in zip(names, parts, recv)]

    shards_2d = {n: s.shape for n, s in shards.items()}
    shards_2d["w_in"] = w["w_in_blocks"].shape[1:]
    dxs, dbm, dcm, dz, ddt, dnw, dsk, dalog, ddtb, *recv_others = _ssd_bwd(
        xbc, proj, dt_raw, y, dyn, states, dtb, alog, dskip_x, normw, emat, emat_t, bl, inner, c_z,
        comm=_rs_chips_comm(chip_sums(OTHERS, "a")))
    dxs_p, dcw_x, dcb_x = _conv_bwd(proj, dxs, conv_w8, conv_b, nct, 0, inner, ct, rt, "conv_bwd_x")
    dbm_p, dcw_b, dcb_b = _conv_bwd(proj, dbm, conv_w8, conv_b, nct, inner, gs, ct, rt, "conv_bwd_b")
    dcm_p, dcw_c, dcb_c = _conv_bwd(proj, dcm, conv_w8, conv_b, nct, inner + gs, gs, ct, rt, "conv_bwd_c")
    segs = [dxs_p, dbm_p, dcm_p, dz, dlg, du]
    keys = [k for k, _, _ in _col_segments(d)]
    dws = {k: mm(xb, s, "tn", BF16, name="dw_in_" + k) for k, s in zip(keys, segs + [ddt])}
    g["w_in"] = _w_in_grad_blocks(dws, d, w["w_in_blocks"].shape[2])
    grad_x, recv_w_in = _dx_kernel(segs, w_main, ddt, w_dt, dr1, tmm, min(512, d),
                                   comm=_rs_chips_comm(chip_sums(["w_in"], "b")))
    recv = dict(zip(OTHERS, recv_others))
    recv["w_in"] = recv_w_in
    g["conv_w"] = jnp.concatenate([dcw_x, dcw_b, dcw_c], axis=1)[:CONV_K]
    g["conv_b"] = jnp.concatenate([dcb_x, dcb_b, dcb_c], axis=1)[0]
    g["b_gates"], g["pool_scale"] = dbg[0], dps[0]
    g["dt_bias"], g["a_log"], g["d_skip"] = ddtb[0, :heads], dalog[0, :heads], dsk[0, :heads]
    g["ssd_norm_w"] = dnw[:, 0, :].reshape(inner)
    g["ln1_g"], g["ln1_b"], g["ln2_g"], g["ln2_b"] = dg1[0], db1[0], dg2[0], db2[0]
    return loss8, grad_x, g, recv


BIG = ("w_in", "w_ssd_proj", "w_pool_group", "w_out", "w_up", "w_down")
OTHERS = BIG[1:]
SMALL = ("b_gates", "conv_b", "dt_bias", "a_log", "d_skip", "ssd_norm_w", "pool_scale", "ln1_g", "ln1_b", "ln2_g",
         "ln2_b")
SMALL_PACK = SMALL + ("conv_w",)
NAMES = ("w_in", "b_gates", "conv_w", "conv_b", "dt_bias", "a_log", "d_skip", "ssd_norm_w", "w_ssd_proj",
         "w_pool_group", "pool_scale", "w_out", "ln1_g", "ln1_b", "w_up", "w_down", "ln2_g", "ln2_b")


def _size(shape):
    n = 1
    for s in shape:
        n *= s
    return n


def _rows128(v):
    v = v.astype(F32).reshape((-1, v.shape[-1]))
    n = v.shape[-1]
    v = jnp.pad(v, ((0, 0), (0, -n % LANES)))
    return v.reshape(-1, LANES)


def _pack_small(vals, extra):
    parts = [_rows128(vals[n]) for n in SMALL_PACK]
    parts.append(jnp.pad(extra.reshape(1, 1).astype(F32), ((0, 0), (0, LANES - 1))))
    buf = jnp.concatenate(parts, axis=0)
    return jnp.pad(buf, ((0, -buf.shape[0] % SUBLANES), (0, 0)))


def _unpack_small(buf, shapes):
    out, off = {}, 0
    for n in SMALL_PACK:
        lead, last = _size(shapes[n][:-1]), shapes[n][-1]
        per = -(-last // LANES)
        out[n] = buf[off:off + lead * per].reshape(lead, per * LANES)[:, :last].reshape(shapes[n])
        off += lead * per
    return out, buf[off, 0]


def _col_segments(d):
    inner, heads, cd, (o_z, o_xbc, o_dt, o_lg) = _dims(d)
    gs = GROUPS * STATE
    return [("xs", o_xbc, inner), ("B", o_xbc + inner, gs), ("C", o_xbc + inner + gs, gs), ("z", o_z, inner),
            ("lg", o_lg, 2 * d), ("u", 0, d), ("dt", o_dt, heads)]


def _cols_from_blocks(blocks, start, width, bw):
    parts, pos = [], start
    while pos < start + width:
        k, off = divmod(pos, bw)
        n = min(bw - off, start + width - pos)
        parts.append(blocks[k][:, off:off + n])
        pos += n
    return parts


def _w_in_internal(blocks, d):
    bw = blocks.shape[2]
    segs = _col_segments(d)
    heads = segs[-1][2]
    main = [p for _, s, w_ in segs[:-1] for p in _cols_from_blocks(blocks, s, w_, bw)]
    w_dt = jnp.concatenate(_cols_from_blocks(blocks, segs[-1][1], heads, bw), axis=1)
    return jnp.concatenate(main, axis=1), jnp.pad(w_dt, ((0, 0), (0, LANES - heads)))


def _w_in_grad_blocks(dws, d, bw):
    order = sorted(_col_segments(d), key=lambda s: s[1])
    blocks = []
    for k in range(N_DEV):
        lo, hi, parts = k * bw, (k + 1) * bw, []
        for key, s, w_ in order:
            a, b = max(lo, s), min(hi, s + w_)
            if a < b:
                parts.append(dws[key][:, a - s:b - s])
        blocks.append(jnp.concatenate(parts, axis=1))
    return jnp.stack(blocks)


def _mesh_pos():
    return lax.axis_index("x"), lax.axis_index("y"), lax.axis_index("c")


def _all_gather_comm(shards):
    nw = len(shards)

    def setup(x_refs, out_refs, scr):
        send_sems, recv_sems, local_sems = scr
        x, y, c = _mesh_pos()
        me, sibling = (x, y, c), (x, y, 1 - c)
        chips = [(1 - x, y), (x, 1 - y), (1 - x, 1 - y)]

        def copy(wi, k, block, to, from_input=False):
            px, py, pc = block
            blk = out_refs[wi].at[4 * px + 2 * py + pc]
            return pltpu.make_async_remote_copy(
                src_ref=x_refs[wi] if from_input else blk, dst_ref=blk,
                send_sem=send_sems.at[7 * wi + k], recv_sem=recv_sems.at[7 * wi + k], device_id=to,
                device_id_type=MESH)

        mine = [pltpu.make_async_copy(x_refs[wi], out_refs[wi].at[4 * x + 2 * y + c], local_sems.at[wi])
                for wi in range(nw)]
        sends = []
        for wi in range(nw):
            sends.append(copy(wi, 0, me, sibling, True))
            sends += [copy(wi, 1 + j, me, (*chip, c), True) for j, chip in enumerate(chips)]
        return copy, mine, sends, me, sibling, chips, c

    def start(x_refs, out_refs, scr):
        _, mine, sends, _, _, _, _ = setup(x_refs, out_refs, scr)
        for cp in mine + sends:
            cp.start()

    def wait(x_refs, out_refs, scr):
        copy, mine, sends, me, sibling, chips, c = setup(x_refs, out_refs, scr)
        passed = []
        for wi in range(nw):
            for j, chip in enumerate(chips):
                copy(wi, 1 + j, (*chip, c), me).wait_recv()
                passed.append(copy(wi, 4 + j, (*chip, c), sibling))
                passed[-1].start()
        for wi in range(nw):
            copy(wi, 0, sibling, me).wait_recv()
            for j, chip in enumerate(chips):
                copy(wi, 4 + j, (*chip, 1 - c), me).wait_recv()
        for cp in sends + passed:
            cp.wait_send()
        for cp in mine:
            cp.wait()

    return _Comm(
        inputs=list(shards),
        out_shapes=[jax.ShapeDtypeStruct((N_DEV,) + s.shape, s.dtype) for s in shards],
        scratch=[pltpu.SemaphoreType.DMA((7 * nw,)), pltpu.SemaphoreType.DMA((7 * nw,)),
                 pltpu.SemaphoreType.DMA((nw,))],
        start=start, wait=wait)


def _rs_sibling_comm(parts):
    nw = len(parts)
    half = N_DEV // 2

    def copies(p_refs, recv_refs, scr):
        send_sems, recv_sems = scr
        x, y, c = _mesh_pos()
        return [pltpu.make_async_remote_copy(
            src_ref=p_refs[wi].at[2 * q + 1 - c], dst_ref=recv_refs[wi].at[q],
            send_sem=send_sems.at[half * wi + q], recv_sem=recv_sems.at[half * wi + q],
            device_id=(x, y, 1 - c), device_id_type=MESH) for wi in range(nw) for q in range(half)]

    def start(p_refs, recv_refs, scr):
        for cp in copies(p_refs, recv_refs, scr):
            cp.start()

    def wait(p_refs, recv_refs, scr):
        for cp in copies(p_refs, recv_refs, scr):
            cp.wait()

    return _Comm(
        inputs=list(parts),
        out_shapes=[jax.ShapeDtypeStruct((half,) + p.shape[1:], p.dtype) for p in parts],
        scratch=[pltpu.SemaphoreType.DMA((half * nw,)), pltpu.SemaphoreType.DMA((half * nw,))],
        start=start, wait=wait)


def _rs_chips_comm(tbs):
    nw = len(tbs)

    def copies(t_refs, o_refs, scr):
        send_sems, recv_sems, local_sems = scr
        x, y, c = _mesh_pos()
        p = 2 * x + y
        chips = [(1 - x, y), (x, 1 - y), (1 - x, 1 - y)]
        own = [pltpu.make_async_copy(t_refs[wi].at[p], o_refs[wi].at[p], local_sems.at[wi]) for wi in range(nw)]
        remote = [pltpu.make_async_remote_copy(
            src_ref=t_refs[wi].at[2 * qx + qy], dst_ref=o_refs[wi].at[p], send_sem=send_sems.at[3 * wi + j],
            recv_sem=recv_sems.at[3 * wi + j], device_id=(qx, qy, c), device_id_type=MESH)
            for wi in range(nw) for j, (qx, qy) in enumerate(chips)]
        arriving = [pltpu.make_async_remote_copy(
            src_ref=t_refs[wi].at[p], dst_ref=o_refs[wi].at[2 * qx + qy], send_sem=send_sems.at[3 * wi + j],
            recv_sem=recv_sems.at[3 * wi + j], device_id=(qx, qy, c), device_id_type=MESH)
            for wi in range(nw) for j, (qx, qy) in enumerate(chips)]
        return own, remote, arriving

    def start(t_refs, o_refs, scr):
        own, remote, _ = copies(t_refs, o_refs, scr)
        for cp in own + remote:
            cp.start()

    def wait(t_refs, o_refs, scr):
        own, remote, arriving = copies(t_refs, o_refs, scr)
        for cp in arriving:
            cp.wait_recv()
        for cp in remote:
            cp.wait_send()
        for cp in own:
            cp.wait()

    return _Comm(
        inputs=list(tbs),
        out_shapes=[jax.ShapeDtypeStruct(t_.shape, t_.dtype) for t_ in tbs],
        scratch=[pltpu.SemaphoreType.DMA((3 * nw,)), pltpu.SemaphoreType.DMA((3 * nw,)),
                 pltpu.SemaphoreType.DMA((nw,))],
        start=start, wait=wait)


def _row_tile(rows, cap=256):
    if rows <= cap:
        return rows
    return max(t_ for t_ in range(SUBLANES, cap + 1, SUBLANES) if rows % t_ == 0)


def _add_pairs(core, part, recv, name):
    n, r, c_ = recv.shape
    tr = _row_tile(r)

    def body(core_ref, a_ref, b_ref, o_ref):
        o_ref[...] = (a_ref[...].astype(F32) + b_ref[...].astype(F32)).astype(o_ref.dtype)

    spec = pl.BlockSpec((1, tr, c_), lambda q, i, core_ref: (q, i, 0))
    return pl.pallas_call(
        body, name=name,
        grid_spec=pltpu.PrefetchScalarGridSpec(
            num_scalar_prefetch=1, grid=(n, r // tr),
            in_specs=[pl.BlockSpec((1, tr, c_), lambda q, i, core_ref: (2 * q + core_ref[0], i, 0)), spec],
            out_specs=spec),
        out_shape=jax.ShapeDtypeStruct(recv.shape, BF16), compiler_params=_params(("parallel", "parallel")),
    )(core, part, recv)


def _small_allreduce(vec, name):
    rows = vec.shape[0]

    def body(x_ref, o_ref, buf, send_sems, recv_sems):
        x, y, c = _mesh_pos()
        me = 4 * x + 2 * y + c
        buf[me] = x_ref[...]
        cps = []
        for k in range(1, N_DEV):
            peer = (1 - x if k & 4 else x, 1 - y if k & 2 else y, 1 - c if k & 1 else c)
            cps.append(pltpu.make_async_remote_copy(
                src_ref=x_ref, dst_ref=buf.at[me], send_sem=send_sems.at[k - 1], recv_sem=recv_sems.at[k - 1],
                device_id=peer, device_id_type=MESH))
        for cp in cps:
            cp.start()
        for k in range(1, N_DEV):
            px, py, pc = (1 - x if k & 4 else x, 1 - y if k & 2 else y, 1 - c if k & 1 else c)
            pltpu.make_async_remote_copy(
                src_ref=x_ref, dst_ref=buf.at[4 * px + 2 * py + pc], send_sem=send_sems.at[k - 1],
                recv_sem=recv_sems.at[k - 1], device_id=(px, py, pc), device_id_type=MESH).wait_recv()
        for cp in cps:
            cp.wait_send()
        acc = buf[0]
        for k in range(1, N_DEV):
            acc = acc + buf[k]
        o_ref[...] = acc

    vm = pl.BlockSpec(memory_space=pltpu.VMEM)
    return pl.pallas_call(
        body, name=name,
        in_specs=[vm], out_specs=vm,
        out_shape=jax.ShapeDtypeStruct(vec.shape, F32),
        scratch_shapes=[pltpu.VMEM((N_DEV, rows, LANES), F32), pltpu.SemaphoreType.DMA((N_DEV - 1,)),
                        pltpu.SemaphoreType.DMA((N_DEV - 1,))],
    )(vec)


def _adamw(gparts, w, m, v, name):
    n, r, c_ = gparts.shape
    tr = _row_tile(r)
    c1 = 1.0 / (1.0 - B1 ** STEP)
    c2 = 1.0 / (1.0 - B2 ** STEP)

    def body(g_ref, w_ref, m_ref, v_ref, go_ref, d_ref, mo_ref, vo_ref):
        g = g_ref[0].astype(F32)
        for q in range(1, n):
            g = g + g_ref[q].astype(F32)
        mn = B1 * m_ref[...] + (1.0 - B1) * g
        vn = B2 * v_ref[...] + (1.0 - B2) * (g * g)
        go_ref[...] = g
        mo_ref[...] = mn
        vo_ref[...] = vn
        d_ref[...] = -LR * ((mn * c1) / (jnp.sqrt(vn * c2) + ADAM_EPS) + WD * w_ref[...])

    spec = pl.BlockSpec((tr, c_), lambda i: (i, 0))
    out = jax.ShapeDtypeStruct((r, c_), F32)
    return pl.pallas_call(
        body, name=name, grid=(r // tr,),
        in_specs=[pl.BlockSpec((n, tr, c_), lambda i: (0, i, 0)), spec, spec, spec],
        out_specs=[spec] * 4, out_shape=[out] * 4, compiler_params=_params(("parallel",)),
    )(gparts, w, m, v)


def kernel(x, w_in, b_gates, conv_w, conv_b, dt_bias, a_log, d_skip, ssd_norm_w, w_ssd_proj, w_pool_group, pool_scale, w_out, ln1_g, ln1_b, w_up, w_down, ln2_g, ln2_b, loss_target, m_w_in, m_b_gates, m_conv_w, m_conv_b, m_dt_bias, m_a_log, m_d_skip, m_ssd_norm_w, m_w_ssd_proj, m_w_pool_group, m_pool_scale, m_w_out, m_ln1_g, m_ln1_b, m_w_up, m_w_down, m_ln2_g, m_ln2_b, v_w_in, v_b_gates, v_conv_w, v_conv_b, v_dt_bias, v_a_log, v_d_skip, v_ssd_norm_w, v_w_ssd_proj, v_w_pool_group, v_pool_scale, v_w_out, v_ln1_g, v_ln1_b, v_w_up, v_w_down, v_ln2_g, v_ln2_b):
    ws = (w_in, b_gates, conv_w, conv_b, dt_bias, a_log, d_skip, ssd_norm_w, w_ssd_proj, w_pool_group, pool_scale,
          w_out, ln1_g, ln1_b, w_up, w_down, ln2_g, ln2_b)
    ms = (m_w_in, m_b_gates, m_conv_w, m_conv_b, m_dt_bias, m_a_log, m_d_skip, m_ssd_norm_w, m_w_ssd_proj,
          m_w_pool_group, m_pool_scale, m_w_out, m_ln1_g, m_ln1_b, m_w_up, m_w_down, m_ln2_g, m_ln2_b)
    vs = (v_w_in, v_b_gates, v_conv_w, v_conv_b, v_dt_bias, v_a_log, v_d_skip, v_ssd_norm_w, v_w_ssd_proj,
          v_w_pool_group, v_pool_scale, v_w_out, v_ln1_g, v_ln1_b, v_w_up, v_w_down, v_ln2_g, v_ln2_b)
    w = {n: a[0] for n, a in zip(NAMES, ws)}
    m = {n: a[0] for n, a in zip(NAMES, ms)}
    v = {n: a[0] for n, a in zip(NAMES, vs)}
    out_shapes = {n: a.shape for n, a in zip(NAMES, ws)}
    bl, s, d = x.shape
    x2, tgt2 = x.reshape(bl * s, d), loss_target.reshape(bl * s, d)
    xi, yi, ci = _mesh_pos()
    me = 4 * xi + 2 * yi + ci
    zero = jnp.zeros((), F32)
    shapes = {n: w[n].shape for n in NAMES}
    shape2d = {n: (_size(shapes[n][:-1]), shapes[n][-1]) for n in BIG}
    cwl = shapes["conv_w"][1]

    conv_place = lax.dynamic_update_slice(jnp.zeros((CONV_K, N_DEV * cwl), F32), w["conv_w"], (0, me * cwl))
    conv_full = _small_allreduce(_rows128(conv_place), "gather_conv_w")
    conv_full = conv_full.reshape(CONV_K, N_DEV * cwl)

    shards = {n: w[n].astype(BF16).reshape(shape2d[n]) for n in BIG}
    full = {n: w[n] for n in SMALL}
    full["conv_w"] = conv_full
    full["w_in_blocks"] = _run_comm(_all_gather_comm([shards.pop("w_in")]), "all_gather_w_in")[0]
    loss8, grad_x, g, recv = _local_step(x2, tgt2, full, shards, ci.astype(jnp.int32).reshape(1), bl)

    small_sum = _small_allreduce(_pack_small(g, loss8[0, 0]), "small_allreduce")
    ex_shapes = {n: shapes[n] for n in SMALL}
    ex_shapes["conv_w"] = (CONV_K, N_DEV * cwl)
    gsum, loss = _unpack_small(small_sum, ex_shapes)
    gsum["conv_w"] = lax.dynamic_slice(gsum["conv_w"], (0, me * cwl), (CONV_K, cwl))
    gs_pk = _pack_small(gsum, zero)
    ws_pk, ms_pk, vs_pk = (_pack_small(t_, zero) for t_ in (w, m, v))
    small_out = _adamw(gs_pk[None], ws_pk, ms_pk, vs_pk, "adamw_small")
    loc_shapes = {n: shapes[n] for n in SMALL_PACK}
    res = [_unpack_small(o, loc_shapes)[0] for o in small_out]

    for n in BIG:
        outs = _adamw(recv[n], *(t_[n].reshape(shape2d[n]) for t_ in (w, m, v)), "adamw_" + n)
        for r_, o in zip(res, outs):
            r_[n] = o

    def ordered(r_):
        return [r_[n].reshape(out_shapes[n]) for n in NAMES]

    return (loss, grad_x.reshape(bl, s, d), *ordered(res[0]), *ordered(res[1]), *ordered(res[2]), *ordered(res[3]))
```

```python
import collections
import functools

import jax
import jax.numpy as jnp
from jax import lax
from jax.experimental import pallas as pl
from jax.experimental.pallas import tpu as pltpu

F32 = jnp.float32
BF16 = jnp.bfloat16
MESH = pl.DeviceIdType.MESH

HEAD_DIM = 64
STATE = 128
GROUPS = 8
CONV_K = 4
CHUNK = 256
POOL_WINDOWS = (2, 4, 8, 16)
ALPHA = 2.0 ** 0.25
LN_EPS = 1e-5
RMS_EPS = 1e-5
LR, B1, B2, ADAM_EPS, WD, STEP = 0.001, 0.9, 0.999, 1e-08, 0.01, 10
N_DEV = 8
LANES = 128
SUBLANES = 8
VMEM_LIMIT = 56 * 1024 * 1024
NEG_BIG = -1e30

NN = (((1,), (0,)), ((), ()))
NT = (((1,), (1,)), ((), ()))
TN = (((0,), (0,)), ((), ()))


def _dot(a, b, dims=NN):
    return lax.dot_general(a.astype(BF16), b.astype(BF16), dims, preferred_element_type=F32)


def _dot_exact01(q, e, dims=NN):
    hi = q.astype(BF16)
    r1 = q - hi.astype(F32)
    mid = r1.astype(BF16)
    lo = (r1 - mid.astype(F32)).astype(BF16)
    f = lambda p: lax.dot_general(p, e, dims, preferred_element_type=F32)
    return f(hi) + f(mid) + f(lo)


def _params(sem):
    return pltpu.CompilerParams(dimension_semantics=sem, vmem_limit_bytes=VMEM_LIMIT)


def _sigmoid(x):
    return 1.0 / (1.0 + jnp.exp(-x))


def _colsum(x):
    return jnp.sum(x, axis=0, keepdims=True)


def _ln_fwd(r):
    mu = jnp.mean(r, axis=-1, keepdims=True)
    xc = r - mu
    var = jnp.mean(xc * xc, axis=-1, keepdims=True)
    rstd = lax.rsqrt(var + LN_EPS)
    return xc * rstd, rstd


def _ln_bwd(dy, xhat, rstd, g):
    dxh = dy * g
    m1 = jnp.mean(dxh, axis=-1, keepdims=True)
    m2 = jnp.mean(dxh * xhat, axis=-1, keepdims=True)
    return rstd * (dxh - m1 - xhat * m2)


_Comm = collections.namedtuple("_Comm", "inputs out_shapes scratch start wait")
ANY = pl.BlockSpec(memory_space=pl.ANY)


def _fuse_comm(body, grid, n_in, n_out, comm):
    if comm is None:
        return body
    ci, co = len(comm.inputs), len(comm.out_shapes)

    def fused(*refs):
        ins, cins = refs[:n_in], refs[n_in:n_in + ci]
        o0 = n_in + ci
        outs, couts = refs[o0:o0 + n_out], refs[o0 + n_out:o0 + n_out + co]
        rest = refs[o0 + n_out + co:]
        scr, cscr = rest[:len(rest) - len(comm.scratch)], rest[len(rest) - len(comm.scratch):]
        ids = [pl.program_id(a) for a in range(len(grid))]
        first, last = ids[0] == 0, ids[0] == grid[0] - 1
        for a in range(1, len(grid)):
            first, last = first & (ids[a] == 0), last & (ids[a] == grid[a] - 1)

        @pl.when(first)
        def _():
            comm.start(cins, couts, cscr)

        body(*ins, *outs, *scr)

        @pl.when(last)
        def _():
            comm.wait(cins, couts, cscr)

    return fused


def _comm_specs(comm):
    if comm is None:
        return [], [], [], []
    return list(comm.inputs), [ANY] * len(comm.inputs), [ANY] * len(comm.out_shapes), list(comm.out_shapes)


def _run_comm(comm, name):
    ci, co = len(comm.inputs), len(comm.out_shapes)

    def body(*refs):
        comm.start(refs[:ci], refs[ci:ci + co], refs[ci + co:])
        comm.wait(refs[:ci], refs[ci:ci + co], refs[ci + co:])

    return pl.pallas_call(body, name=name, in_specs=[ANY] * ci, out_specs=[ANY] * co, out_shape=list(comm.out_shapes),
                          scratch_shapes=list(comm.scratch))(*comm.inputs)


def _matmul(a, b, mode, out_dtype, bm, bn, bk, name, a_fn=None, col_blocks=0, comm=None):
    if mode == "nn":
        (m, k), n, dims = a.shape, b.shape[1], NN
    elif mode == "nt":
        (m, k), n, dims = a.shape, b.shape[0], NT
    else:
        (k, m), n, dims = a.shape, b.shape[1], TN
    bm, bn, bk = min(bm, m), min(bn, n), min(bk, k)
    assert m % bm == 0 and n % bn == 0 and k % bk == 0, (name, m, n, k, bm, bn, bk)
    nk = k // bk
    if mode == "nn":
        a_spec = pl.BlockSpec((bm, bk), lambda i, j, kk: (i, kk))
        b_spec = pl.BlockSpec((bk, bn), lambda i, j, kk: (kk, j))
    elif mode == "nt":
        a_spec = pl.BlockSpec((bm, bk), lambda i, j, kk: (i, kk))
        b_spec = pl.BlockSpec((bn, bk), lambda i, j, kk: (j, kk))
    else:
        a_spec = pl.BlockSpec((bk, bm), lambda i, j, kk: (kk, i))
        b_spec = pl.BlockSpec((bk, bn), lambda i, j, kk: (kk, j))

    def body(a_ref, b_ref, o_ref, acc_ref):
        kk = pl.program_id(2)

        @pl.when(kk == 0)
        def _():
            acc_ref[...] = jnp.zeros_like(acc_ref)

        av = a_ref[...]
        if a_fn is not None:
            av = a_fn(av.astype(F32))
        acc_ref[...] += _dot(av, b_ref[...], dims)

        @pl.when(kk == nk - 1)
        def _():
            if col_blocks:
                for s in range(bn // slab):
                    o_ref[s] = acc_ref[:, s * slab:(s + 1) * slab].astype(o_ref.dtype)
            else:
                o_ref[...] = acc_ref[...].astype(o_ref.dtype)

    if col_blocks:
        slab = n // col_blocks
        assert n % col_blocks == 0 and bn % slab == 0, (name, n, col_blocks, bn)
        out_spec = pl.BlockSpec((bn // slab, bm, slab), lambda i, j, kk: (j, i, 0))
        out_shape = jax.ShapeDtypeStruct((col_blocks, m, slab), out_dtype)
    else:
        out_spec = pl.BlockSpec((bm, bn), lambda i, j, kk: (i, j))
        out_shape = jax.ShapeDtypeStruct((m, n), out_dtype)
    grid = (m // bm, n // bn, nk)
    c_in, c_in_specs, c_out_specs, c_out_shapes = _comm_specs(comm)
    res = pl.pallas_call(
        _fuse_comm(body, grid, 2, 1, comm), name=name,
        grid=grid,
        in_specs=[a_spec, b_spec] + c_in_specs,
        out_specs=[out_spec] + c_out_specs,
        out_shape=[out_shape] + c_out_shapes,
        scratch_shapes=[pltpu.VMEM((bm, bn), F32)] + (list(comm.scratch) if comm else []),
        compiler_params=_params(("arbitrary",) * 3 if comm else ("parallel", "parallel", "arbitrary")),
    )(a, b, *c_in)
    return res if comm else res[0]


CONV_STRIP = 16


def _conv_pre(ext_ref, w_ref, b_ref, r0, rows):
    acc = b_ref[...] + w_ref[0:1, :] * ext_ref[pl.ds(r0 + SUBLANES - (CONV_K - 1), rows), :]
    for k in range(1, CONV_K):
        acc = acc + w_ref[k:k + 1, :] * ext_ref[pl.ds(r0 + SUBLANES - (CONV_K - 1) + k, rows), :]
    return acc


def _in_proj(xb, w_main, conv_w8, conv_b, cd, seq_len, bm, bn, comm):
    t, d = xb.shape
    pw = w_main.shape[1]
    bm, bn = min(bm, seq_len), min(bn, d)
    assert t % bm == 0 and seq_len % bm == 0 and pw % bn == 0 and cd % bn == 0
    ncj = cd // bn
    tiles_per_seq = seq_len // bm

    def body(x_ref, w_ref, cw_ref, cb_ref, p_ref, xbc_ref, dsl_ref, ext_ref, carry_ref):
        i = pl.program_id(0)
        j = pl.program_id(1)

        def conv_previous():
            ext = ext_ref.at[(j + 1) % 2]
            for r0 in range(0, bm, CONV_STRIP):
                acc = _conv_pre(ext, cw_ref, cb_ref, r0, CONV_STRIP)
                sg = _sigmoid(acc)
                xbc_ref[r0:r0 + CONV_STRIP, :] = (acc * sg).astype(xbc_ref.dtype)
                dsl_ref[r0:r0 + CONV_STRIP, :] = (sg * (1.0 + acc * (1.0 - sg))).astype(dsl_ref.dtype)

        def project(stash):
            pq = _dot(x_ref[...], w_ref[...]).astype(BF16)
            p_ref[...] = pq
            if stash:
                ext = ext_ref.at[j % 2]
                jc = jnp.minimum(j, ncj - 1)
                ext[0:SUBLANES, :] = jnp.where((i % tiles_per_seq) == 0, 0.0, carry_ref[jc])
                ext[SUBLANES:, :] = pq.astype(F32)
                carry_ref[jc] = ext[bm:bm + SUBLANES, :]

        @pl.when(j == 0)
        def _():
            project(True)

        @pl.when((j >= 1) & (j < ncj))
        def _():
            conv_previous()
            project(True)

        @pl.when(j == ncj)
        def _():
            conv_previous()
            project(False)

        @pl.when(j > ncj)
        def _():
            project(False)

    assert pw // bn > ncj
    grid = (t // bm, pw // bn)
    conv_col = lambda i, j: (0, jnp.clip(j - 1, 0, ncj - 1))
    c_in, c_in_specs, c_out_specs, c_out_shapes = _comm_specs(comm)
    conv_tile = pl.BlockSpec((bm, bn), lambda i, j: (i, jnp.clip(j - 1, 0, ncj - 1)))
    conv_out = jax.ShapeDtypeStruct((t, cd), BF16)
    return pl.pallas_call(
        _fuse_comm(body, grid, 4, 3, comm), name="in_proj",
        grid=grid,
        in_specs=[pl.BlockSpec((bm, d), lambda i, j: (i, 0)), pl.BlockSpec((d, bn), lambda i, j: (0, j)),
                  pl.BlockSpec((SUBLANES, bn), conv_col), pl.BlockSpec((1, bn), conv_col)] + c_in_specs,
        out_specs=[pl.BlockSpec((bm, bn), lambda i, j: (i, j)), conv_tile, conv_tile] + c_out_specs,
        out_shape=[jax.ShapeDtypeStruct((t, pw), BF16), conv_out, conv_out] + c_out_shapes,
        scratch_shapes=[pltpu.VMEM((2, bm + SUBLANES, bn), F32), pltpu.VMEM((ncj, SUBLANES, bn), F32)]
        + (list(comm.scratch) if comm else []),
        compiler_params=_params(("arbitrary", "arbitrary")),
    )(xb, w_main, conv_w8, conv_b, *c_in)


def _conv_bwd(proj, dsilu, dxbc, conv_w8, n_seq_chunks, col0, width, ct, L, name):
    t = proj.shape[0]
    nbc = t // L
    hb = L // SUBLANES
    ct = min(ct, width)
    assert col0 % ct == 0 and width % ct == 0
    cb0 = col0 // ct
    last_hb = t // SUBLANES - 1

    def body(x_ref, xb_ref, s_ref, sa_ref, d_ref, da_ref, w_ref, o_ref, dw_ref, db_ref, ext_ref, dc_ref):
        bc = pl.program_id(1)
        first = (bc % n_seq_chunks) == 0
        last = (bc % n_seq_chunks) == n_seq_chunks - 1

        @pl.when(bc == 0)
        def _():
            dw_ref[...] = jnp.zeros_like(dw_ref)
            db_ref[...] = jnp.zeros_like(db_ref)

        ext_ref[0:SUBLANES, :] = jnp.where(first, 0.0, xb_ref[...].astype(F32))
        ext_ref[SUBLANES:, :] = x_ref[...].astype(F32)
        for r0 in range(0, L, CONV_STRIP):
            rows = slice(r0, r0 + CONV_STRIP)
            dc_ref[rows, :] = d_ref[rows, :].astype(F32) * s_ref[rows, :].astype(F32)
        dc_ref[L:, :] = jnp.where(last, 0.0, da_ref[...].astype(F32)) * sa_ref[...].astype(F32)
        fold = lambda v: v[0:SUBLANES] + v[SUBLANES:CONV_STRIP]
        dws = [jnp.zeros((SUBLANES, ct), F32) for _ in range(CONV_K)]
        dbs = jnp.zeros((SUBLANES, ct), F32)
        for r0 in range(0, L, CONV_STRIP):
            dc = dc_ref[r0:r0 + CONV_STRIP, :]
            dx = w_ref[CONV_K - 1:CONV_K, :] * dc
            for k in range(CONV_K - 1):
                dx = dx + w_ref[k:k + 1, :] * dc_ref[pl.ds(r0 + CONV_K - 1 - k, CONV_STRIP), :]
            o_ref[r0:r0 + CONV_STRIP, :] = dx.astype(o_ref.dtype)
            for k in range(CONV_K):
                dws[k] = dws[k] + fold(dc * ext_ref[pl.ds(r0 + SUBLANES - (CONV_K - 1) + k, CONV_STRIP), :])
            dbs = dbs + fold(dc)
        for k in range(CONV_K):
            dw_ref[k:k + 1, :] += _colsum(dws[k])
        db_ref[0:1, :] += _colsum(dbs)

    return pl.pallas_call(
        body, name=name,
        grid=(width // ct, nbc),
        in_specs=[
            pl.BlockSpec((L, ct), lambda j, i: (i, cb0 + j)),
            pl.BlockSpec((SUBLANES, ct), lambda j, i: (jnp.maximum(i * hb - 1, 0), cb0 + j)),
            pl.BlockSpec((L, ct), lambda j, i: (i, cb0 + j)),
            pl.BlockSpec((SUBLANES, ct), lambda j, i: (jnp.minimum((i + 1) * hb, last_hb), cb0 + j)),
            pl.BlockSpec((L, ct), lambda j, i: (i, j)),
            pl.BlockSpec((SUBLANES, ct), lambda j, i: (jnp.minimum((i + 1) * hb, last_hb), j)),
            pl.BlockSpec((SUBLANES, ct), lambda j, i: (0, cb0 + j)),
        ],
        out_specs=[
            pl.BlockSpec((L, ct), lambda j, i: (i, j)),
            pl.BlockSpec((SUBLANES, ct), lambda j, i: (0, j)),
            pl.BlockSpec((SUBLANES, ct), lambda j, i: (0, j)),
        ],
        out_shape=[
            jax.ShapeDtypeStruct((t, width), BF16),
            jax.ShapeDtypeStruct((SUBLANES, width), F32),
            jax.ShapeDtypeStruct((SUBLANES, width), F32),
        ],
        scratch_shapes=[pltpu.VMEM((L + SUBLANES, ct), F32), pltpu.VMEM((L + SUBLANES, ct), F32)],
        compiler_params=_params(("parallel", "arbitrary")),
    )(proj, proj, dsilu, dsilu, dxbc, dxbc, conv_w8)


def _cumsum_rows(x, reverse=False):
    n = x.shape[0]
    row = lax.broadcasted_iota(jnp.int32, x.shape, 0)
    s = 1
    while s < n:
        if reverse:
            x = x + jnp.where(row < n - s, pltpu.roll(x, n - s, 0), 0.0)
        else:
            x = x + jnp.where(row >= s, pltpu.roll(x, s, 0), 0.0)
        s *= 2
    return x


def _ssd_scalars(dtr, dtb, alog):
    pre = dtr + dtb
    dt = jnp.maximum(pre, 0.0) + jnp.log(1.0 + jnp.exp(-jnp.abs(pre)))
    a = -jnp.exp(alog)
    acs = _cumsum_rows(dt * a) * LOG2E
    n = acs.shape[0]
    return pre, dt, a, acs, jnp.exp2(acs), jnp.exp2(acs[n - 1:n, :] - acs)


LOG2E = 1.4426950408889634


def _dot_2piece(q, e):
    hi = q.astype(BF16)
    mid = (q - hi.astype(F32)).astype(BF16)
    return lax.dot_general(jnp.concatenate([hi, mid], axis=1), jnp.concatenate([e, e], axis=0), NN,
                           preferred_element_type=F32)


def _ssd_group_common(dt_s, e_s, dec_s, e):
    return _dot_2piece(dt_s, e), _dot_2piece(e_s, e), _dot_2piece(dec_s, e)


def _decay_matrix(acs, acs_t, h, tri):
    return jnp.exp2(jnp.where(tri, acs[:, h:h + 1] - acs_t[h:h + 1, :], NEG_BIG))


def _head_mask(r, gw, dtype):
    lane = lax.broadcasted_iota(jnp.int32, (1, gw), 1)
    return ((lane >= r * HEAD_DIM) & (lane < (r + 1) * HEAD_DIM)).astype(dtype)


def _ssd_fwd(xbc, proj, dt_raw, dtb, alog, dskip_x, normw, emat, bl, inner, z_col0):
    t = xbc.shape[0]
    L = CHUNK
    nc = t // bl // L
    G = GROUPS
    gw = inner // G
    hpg = gw // HEAD_DIM
    assert z_col0 % gw == 0
    zb0 = z_col0 // gw
    bb0 = inner // STATE
    cb0 = bb0 + G

    P = G
    assert bb0 % P == 0 and cb0 % P == 0 and zb0 % P == 0

    def body(xs_ref, b_ref, c_ref, z_ref, dtr_ref, dtb_ref, alog_ref, dsk_ref, nw_ref, e_ref,
             y_ref, yn_ref, st_ref, h_ref):
        c = pl.program_id(1)
        _, dt_s, _, acs, e_s, dec_s = _ssd_scalars(dtr_ref[...], dtb_ref[...], alog_ref[...])
        acs_t = acs.T
        tri = lax.broadcasted_iota(jnp.int32, (L, L), 0) >= lax.broadcasted_iota(jnp.int32, (L, L), 1)
        lane = lax.broadcasted_iota(jnp.int32, (L, gw), 1)
        for g in range(G):
            cols = slice(g * gw, (g + 1) * gw)
            ncol = slice(g * STATE, (g + 1) * STATE)

            @pl.when(c == 0)
            def _():
                h_ref[g] = jnp.zeros((STATE, gw), F32)

            xs = xs_ref[:, cols].astype(F32)
            bg = b_ref[:, ncol]
            cg = c_ref[:, ncol]
            dt_x, e_x, dec_x = _ssd_group_common(dt_s, e_s, dec_s, e_ref[:, cols])
            xdt = xs * dt_x
            cb = _dot(cg, bg, NT)
            h = h_ref[g]
            st_ref[0, g] = h
            y = _dot(cg, h) * e_x + dsk_ref[:, cols] * xs
            for r in range(hpg):
                m = cb * _decay_matrix(acs, acs_t, g * hpg + r, tri)
                xr = jnp.where((lane >= r * HEAD_DIM) & (lane < (r + 1) * HEAD_DIM), xdt, 0.0)
                y = y + _dot(m, xr)
            h_ref[g] = h * e_x[L - 1:L, :] + _dot(bg, xdt * dec_x, TN)
            yq = y.astype(y_ref.dtype)
            y_ref[:, cols] = yq
            z = z_ref[:, cols].astype(F32)
            yg = yq.astype(F32) * (z * _sigmoid(z))
            rs = lax.rsqrt(jnp.mean(yg * yg, axis=-1, keepdims=True) + RMS_EPS)
            yn_ref[:, cols] = (yg * rs * nw_ref[:, cols]).astype(yn_ref.dtype)

    return pl.pallas_call(
        body, name="ssd_fwd",
        grid=(bl, nc, G // P),
        in_specs=[
            pl.BlockSpec((L, P * gw), lambda b, c, g: (b * nc + c, g)),
            pl.BlockSpec((L, P * STATE), lambda b, c, g: (b * nc + c, bb0 // P + g)),
            pl.BlockSpec((L, P * STATE), lambda b, c, g: (b * nc + c, cb0 // P + g)),
            pl.BlockSpec((L, P * gw), lambda b, c, g: (b * nc + c, zb0 // P + g)),
            pl.BlockSpec((L, LANES), lambda b, c, g: (b * nc + c, 0)),
            pl.BlockSpec((1, LANES), lambda b, c, g: (0, 0)),
            pl.BlockSpec((1, LANES), lambda b, c, g: (0, 0)),
            pl.BlockSpec((1, P * gw), lambda b, c, g: (0, g)),
            pl.BlockSpec((1, P * gw), lambda b, c, g: (0, g)),
            pl.BlockSpec((LANES, P * gw), lambda b, c, g: (0, g)),
        ],
        out_specs=[
            pl.BlockSpec((L, P * gw), lambda b, c, g: (b * nc + c, g)),
            pl.BlockSpec((L, P * gw), lambda b, c, g: (b * nc + c, g)),
            pl.BlockSpec((1, P, STATE, gw), lambda b, c, g: (b * nc + c, g, 0, 0)),
        ],
        out_shape=[
            jax.ShapeDtypeStruct((t, inner), BF16),
            jax.ShapeDtypeStruct((t, inner), BF16),
            jax.ShapeDtypeStruct((bl * nc, G, STATE, gw), F32),
        ],
        scratch_shapes=[pltpu.VMEM((G, STATE, gw), F32)],
        compiler_params=_params(("arbitrary", "arbitrary", "arbitrary")),
    )(xbc, xbc, xbc, proj, dt_raw, dtb, alog, dskip_x, normw, emat)


def _ssd_bwd(xbc, proj, dt_raw, y, dyn, states, dtb, alog, dskip_x, normw, emat, emat_t, bl, inner, z_col0,
             comm=None):
    t = xbc.shape[0]
    L = CHUNK
    nc = t // bl // L
    G = GROUPS
    gw = inner // G
    hpg = gw // HEAD_DIM
    zb0 = z_col0 // gw
    bb0 = inner // STATE
    cb0 = bb0 + G
    P = G

    def rc(j):
        return nc - 1 - j

    def body(xs_ref, b_ref, c_ref, z_ref, dtr_ref, y_ref, dyn_ref, st_ref, dtb_ref, alog_ref, dsk_ref,
             nw_ref, e_ref, et_ref,
             dxs_ref, db_ref, dc_ref, dz_ref, ddt_ref, dnw_ref, dsk_acc, dalog_acc, ddtb_acc,
             dh_ref):
        b = pl.program_id(0)
        j = pl.program_id(1)

        @pl.when((b == 0) & (j == 0))
        def _():
            dsk_acc[...] = jnp.zeros_like(dsk_acc)
            dalog_acc[...] = jnp.zeros_like(dalog_acc)
            ddtb_acc[...] = jnp.zeros_like(ddtb_acc)

        pre, dt_s, a_row, acs, e_s, dec_s = _ssd_scalars(dtr_ref[...], dtb_ref[...], alog_ref[...])
        acs_t = acs.T
        wacs = jnp.zeros((L, LANES), F32)
        wdt = jnp.zeros((L, LANES), F32)
        tri = lax.broadcasted_iota(jnp.int32, (L, L), 0) >= lax.broadcasted_iota(jnp.int32, (L, L), 1)
        rowi = lax.broadcasted_iota(jnp.int32, (L, gw), 0)
        for g in range(G):
            cols = slice(g * gw, (g + 1) * gw)
            ncol = slice(g * STATE, (g + 1) * STATE)

            @pl.when((b == 0) & (j == 0))
            def _():
                dnw_ref[g] = jnp.zeros((SUBLANES, gw), F32)

            @pl.when(j == 0)
            def _():
                dh_ref[g] = jnp.zeros((STATE, gw), F32)

            xs = xs_ref[:, cols].astype(F32)
            bg = b_ref[:, ncol]
            cg = c_ref[:, ncol]
            dt_x, e_x, dec_x = _ssd_group_common(dt_s, e_s, dec_s, e_ref[:, cols])
            xdt = xs * dt_x
            xdt_b = xdt.astype(BF16)
            cb = _dot(cg, bg, NT)
            h = st_ref[0, g]
            hb16 = h.astype(BF16)
            dsk = dsk_ref[:, cols]

            yv = y_ref[:, cols].astype(F32)
            z = z_ref[:, cols].astype(F32)
            sgz = _sigmoid(z)
            sz = z * sgz
            yg = yv * sz
            rs = lax.rsqrt(jnp.mean(yg * yg, axis=-1, keepdims=True) + RMS_EPS)
            yhat = yg * rs
            dyn_v = dyn_ref[:, cols].astype(F32)
            dnw_ref[g] += _colsum(dyn_v * yhat)
            dyh = dyn_v * nw_ref[:, cols]
            dyg = rs * (dyh - yhat * jnp.mean(dyh * yhat, axis=-1, keepdims=True))
            dy = dyg * sz
            dz_ref[:, cols] = (dyg * yv * (sgz * (1.0 + z * (1.0 - sgz)))).astype(dz_ref.dtype)

            dy_b = dy.astype(BF16)
            dcb = jnp.zeros((L, L), F32)
            dxdt_d = jnp.zeros((L, gw), F32)
            ydiag = jnp.zeros((L, gw), F32)
            for r in range(hpg):
                lm = _decay_matrix(acs, acs_t, g * hpg + r, tri)
                m = (cb * lm).astype(BF16)
                hm = _head_mask(r, gw, BF16)
                dyr = dy_b * hm
                xr = xdt_b * hm
                ydiag = ydiag + _dot(m, xr)
                dcb = dcb + _dot(dyr, xdt_b, NT) * lm
                dxdt_d = dxdt_d + _dot(m, dyr, TN)
            dh = dh_ref[g]
            dh16 = dh.astype(BF16)
            xdec_b = (xdt * dec_x).astype(BF16)
            bdh = _dot(bg, dh16)
            dxdt = dxdt_d + dec_x * bdh
            dcb16 = dcb.astype(BF16)
            dye = (dy * e_x).astype(BF16)
            db_ref[:, ncol] = (_dot(dcb16, cg, TN) + _dot(xdec_b, dh16, NT)).astype(db_ref.dtype)
            dc_ref[:, ncol] = (_dot(dcb16, bg) + _dot(dye, hb16, NT)).astype(dc_ref.dtype)
            dprev = _dot(cg, dye, TN)
            cd_row = e_x[L - 1:L, :]
            s_new = _dot(bg, xdec_b, TN)
            last_term = _colsum(dh16.astype(F32) * s_new) + _colsum(dh * h) * cd_row
            yoff = _dot(cg, hb16) * e_x
            wfold = (dy_b.astype(F32) * ydiag + dy * yoff - dxdt_d * xdt_b.astype(F32) - bdh * xdec_b.astype(F32)
                     + jnp.where(rowi == L - 1, last_term, 0.0))
            et = et_ref[cols, :]
            wacs = wacs + _dot_2piece(wfold, et)
            wdt = wdt + _dot_2piece(dxdt * xs, et)
            dsk_acc[...] += _dot_exact01(jnp.broadcast_to(_colsum(dy * xs), (SUBLANES, gw)), et)
            dxs_ref[:, cols] = (dsk * dy + dxdt * dt_x).astype(dxs_ref.dtype)
            dh_ref[g] = dprev + cd_row * dh

        dda = _cumsum_rows(wacs, reverse=True)
        ddt_raw = (wdt + dda * a_row) * _sigmoid(pre)
        ddt_ref[...] = ddt_raw
        dalog_acc[...] += _colsum(dda * dt_s) * a_row
        ddtb_acc[...] += _colsum(ddt_raw)

    def cidx(b, j):
        return b * nc + rc(j)

    accs = lambda shape: pl.BlockSpec(shape, lambda b, j, g: tuple(0 for _ in shape))
    grid = (bl, nc, G // P)
    c_in, c_in_specs, c_out_specs, c_out_shapes = _comm_specs(comm)
    return pl.pallas_call(
        _fuse_comm(body, grid, 14, 9, comm), name="ssd_bwd",
        grid=grid,
        in_specs=[
            pl.BlockSpec((L, P * gw), lambda b, j, g: (cidx(b, j), g)),
            pl.BlockSpec((L, P * STATE), lambda b, j, g: (cidx(b, j), bb0 // P + g)),
            pl.BlockSpec((L, P * STATE), lambda b, j, g: (cidx(b, j), cb0 // P + g)),
            pl.BlockSpec((L, P * gw), lambda b, j, g: (cidx(b, j), zb0 // P + g)),
            pl.BlockSpec((L, LANES), lambda b, j, g: (cidx(b, j), 0)),
            pl.BlockSpec((L, P * gw), lambda b, j, g: (cidx(b, j), g)),
            pl.BlockSpec((L, P * gw), lambda b, j, g: (cidx(b, j), g)),
            pl.BlockSpec((1, P, STATE, gw), lambda b, j, g: (cidx(b, j), g, 0, 0)),
            pl.BlockSpec((1, LANES), lambda b, j, g: (0, 0)),
            pl.BlockSpec((1, LANES), lambda b, j, g: (0, 0)),
            pl.BlockSpec((1, P * gw), lambda b, j, g: (0, g)),
            pl.BlockSpec((1, P * gw), lambda b, j, g: (0, g)),
            pl.BlockSpec((LANES, P * gw), lambda b, j, g: (0, g)),
            pl.BlockSpec((P * gw, LANES), lambda b, j, g: (g, 0)),
        ] + c_in_specs,
        out_specs=[
            pl.BlockSpec((L, P * gw), lambda b, j, g: (cidx(b, j), g)),
            pl.BlockSpec((L, P * STATE), lambda b, j, g: (cidx(b, j), g)),
            pl.BlockSpec((L, P * STATE), lambda b, j, g: (cidx(b, j), g)),
            pl.BlockSpec((L, P * gw), lambda b, j, g: (cidx(b, j), g)),
            pl.BlockSpec((L, LANES), lambda b, j, g: (cidx(b, j), 0)),
            accs((G, SUBLANES, gw)),
            accs((SUBLANES, LANES)),
            accs((SUBLANES, LANES)),
            accs((SUBLANES, LANES)),
        ] + c_out_specs,
        out_shape=[
            jax.ShapeDtypeStruct((t, inner), BF16),
            jax.ShapeDtypeStruct((t, G * STATE), BF16),
            jax.ShapeDtypeStruct((t, G * STATE), BF16),
            jax.ShapeDtypeStruct((t, inner), BF16),
            jax.ShapeDtypeStruct((t, LANES), F32),
            jax.ShapeDtypeStruct((G, SUBLANES, gw), F32),
            jax.ShapeDtypeStruct((SUBLANES, LANES), F32),
            jax.ShapeDtypeStruct((SUBLANES, LANES), F32),
            jax.ShapeDtypeStruct((SUBLANES, LANES), F32),
        ] + c_out_shapes,
        scratch_shapes=[pltpu.VMEM((G, STATE, gw), F32)]
        + (list(comm.scratch) if comm else []),
        compiler_params=_params(("arbitrary", "arbitrary", "arbitrary")),
    )(xbc, xbc, xbc, proj, dt_raw, y, dyn, states, dtb, alog, dskip_x, normw, emat, emat_t, *c_in)


def _pool_window(u, w, anti):
    n = u.shape[0]
    row = lax.broadcasted_iota(jnp.int32, u.shape, 0)
    acc = u
    s = 1
    while s < w:
        if anti:
            acc = acc + jnp.where(row < n - s, pltpu.roll(acc, n - s, 0), 0.0)
        else:
            acc = acc + jnp.where(row >= s, pltpu.roll(acc, s, 0), 0.0)
        s *= 2
    return acc


def _pool_cnt(shape, w):
    row = lax.broadcasted_iota(jnp.int32, shape, 0)
    return jnp.minimum(row + 1, w).astype(F32)


def _pool_fwd(proj, wpg, bl, d, u_col0):
    t = proj.shape[0]
    s = t // bl
    pg = len(POOL_WINDOWS)
    cg = d // pg
    ub0 = u_col0 // d

    def body(u_ref, w_ref, o_ref):
        for gi, w in enumerate(POOL_WINDOWS):
            u = u_ref[:, gi * cg:(gi + 1) * cg].astype(F32)
            pooled = _pool_window(u, w, False) / _pool_cnt(u.shape, w) - u
            o_ref[:, gi * cg:(gi + 1) * cg] = _dot(pooled, w_ref[gi]).astype(o_ref.dtype)

    return pl.pallas_call(
        body, name="pool_fwd",
        grid=(bl,),
        in_specs=[pl.BlockSpec((s, d), lambda b: (b, ub0)), pl.BlockSpec((pg, cg, cg), lambda b: (0, 0, 0))],
        out_specs=pl.BlockSpec((s, d), lambda b: (b, 0)),
        out_shape=jax.ShapeDtypeStruct((t, d), BF16),
        compiler_params=_params(("parallel",)),
    )(proj, wpg)


def _pool_bwd(proj, dyp, wpg, bl, d, u_col0):
    t = proj.shape[0]
    s = t // bl
    pg = len(POOL_WINDOWS)
    cg = d // pg
    ub0 = u_col0 // d

    def body(u_ref, dy_ref, w_ref, du_ref, dw_ref):
        @pl.when(pl.program_id(0) == 0)
        def _():
            dw_ref[...] = jnp.zeros_like(dw_ref)

        for gi, w in enumerate(POOL_WINDOWS):
            u = u_ref[:, gi * cg:(gi + 1) * cg].astype(F32)
            cnt = _pool_cnt(u.shape, w)
            pooled = _pool_window(u, w, False) / cnt - u
            dy = dy_ref[:, gi * cg:(gi + 1) * cg]
            dw_ref[gi] += _dot(pooled, dy, TN)
            dp = _dot(dy, w_ref[gi], NT)
            du_ref[:, gi * cg:(gi + 1) * cg] = (_pool_window(dp / cnt, w, True) - dp).astype(du_ref.dtype)

    return pl.pallas_call(
        body, name="pool_bwd",
        grid=(bl,),
        in_specs=[pl.BlockSpec((s, d), lambda b: (b, ub0)), pl.BlockSpec((s, d), lambda b: (b, 0)),
                  pl.BlockSpec((pg, cg, cg), lambda b: (0, 0, 0))],
        out_specs=[pl.BlockSpec((s, d), lambda b: (b, 0)), pl.BlockSpec((pg, cg, cg), lambda b: (0, 0, 0))],
        out_shape=[jax.ShapeDtypeStruct((t, d), BF16), jax.ShapeDtypeStruct((pg, cg, cg), F32)],
        compiler_params=_params(("arbitrary",)),
    )(proj, dyp, wpg)


def _merge_fwd(proj, ypr, yssd, x, w_out, b_gates, pool_scale, d, lg_col0, tm):
    t = x.shape[0]
    lb0 = lg_col0 // (2 * d)

    def body(lg_ref, yp_ref, ys_ref, x_ref, w_ref, bg_ref, ps_ref, mg_ref, r1_ref):
        lg = lg_ref[...].astype(F32) + bg_ref[...]
        ga = _sigmoid(lg[:, :d])
        gb = _sigmoid(lg[:, d:])
        merged = ga * (yp_ref[...].astype(F32) * ps_ref[...]) + gb * ys_ref[...].astype(F32)
        mg_ref[...] = merged.astype(mg_ref.dtype)
        r1_ref[...] = ALPHA * x_ref[...] + _dot(mg_ref[...], w_ref[...])

    row = lambda w: pl.BlockSpec((tm, w), lambda i: (i, 0))
    full = lambda a: pl.BlockSpec(a.shape, lambda i: (0, 0))
    return pl.pallas_call(
        body, name="merge_fwd",
        grid=(t // tm,),
        in_specs=[pl.BlockSpec((tm, 2 * d), lambda i: (i, lb0)), row(d), row(d), row(d), full(w_out), full(b_gates),
                  full(pool_scale)],
        out_specs=[row(d), row(d)],
        out_shape=[jax.ShapeDtypeStruct((t, d), BF16), jax.ShapeDtypeStruct((t, d), F32)],
        compiler_params=_params(("parallel",)),
    )(proj, ypr, yssd, x, w_out, b_gates, pool_scale)


def _merge_bwd(dr1, proj, ypr, yssd, w_out, b_gates, pool_scale, d, lg_col0, tm):
    t = dr1.shape[0]
    lb0 = lg_col0 // (2 * d)

    def body(dr_ref, lg_ref, yp_ref, ys_ref, w_ref, bg_ref, ps_ref, dlg_ref, dyp_ref, dys_ref, dbg_ref, dps_ref):
        @pl.when(pl.program_id(0) == 0)
        def _():
            dbg_ref[...] = jnp.zeros_like(dbg_ref)
            dps_ref[...] = jnp.zeros_like(dps_ref)

        dm = _dot(dr_ref[...], w_ref[...], NT)
        lg = lg_ref[...].astype(F32) + bg_ref[...]
        ga = _sigmoid(lg[:, :d])
        gb = _sigmoid(lg[:, d:])
        ypr_v = yp_ref[...].astype(F32)
        ys_v = ys_ref[...].astype(F32)
        ps = ps_ref[...]
        dga = dm * ypr_v * ps
        dla = dga * ga * (1.0 - ga)
        dlb = dm * ys_v * gb * (1.0 - gb)
        dlg_ref[:, :d] = dla.astype(dlg_ref.dtype)
        dlg_ref[:, d:] = dlb.astype(dlg_ref.dtype)
        dyp_ref[...] = (dm * ga * ps).astype(dyp_ref.dtype)
        dys_ref[...] = (dm * gb).astype(dys_ref.dtype)
        dbg_ref[0:1, :d] += _colsum(dla)
        dbg_ref[0:1, d:] += _colsum(dlb)
        dps_ref[0:1, :] += _colsum(dm * ga * ypr_v)

    row = lambda w: pl.BlockSpec((tm, w), lambda i: (i, 0))
    full = lambda a: pl.BlockSpec(a.shape, lambda i: (0, 0))
    acc = lambda w: pl.BlockSpec((SUBLANES, w), lambda i: (0, 0))
    return pl.pallas_call(
        body, name="merge_bwd",
        grid=(t // tm,),
        in_specs=[row(d), pl.BlockSpec((tm, 2 * d), lambda i: (i, lb0)), row(d), row(d), full(w_out), full(b_gates),
                  full(pool_scale)],
        out_specs=[row(2 * d), row(d), row(d), acc(2 * d), acc(d)],
        out_shape=[jax.ShapeDtypeStruct((t, 2 * d), BF16), jax.ShapeDtypeStruct((t, d), BF16),
                   jax.ShapeDtypeStruct((t, d), BF16), jax.ShapeDtypeStruct((SUBLANES, 2 * d), F32),
                   jax.ShapeDtypeStruct((SUBLANES, d), F32)],
        compiler_params=_params(("arbitrary",)),
    )(dr1, proj, ypr, yssd, w_out, b_gates, pool_scale)


def _mlp_fwd(r1, target, w_up, w_down, ln1_g, ln1_b, ln2_g, ln2_b, tm):
    t, d = r1.shape
    nf, _, tf = w_up.shape
    ff = nf * tf

    def body(r1_ref, tg_ref, wu_ref, wd_ref, g1_ref, b1_ref, g2_ref, b2_ref,
             up_ref, h1_ref, dr2_ref, loss_ref, dg2_ref, db2_ref, h1f, acc):
        i = pl.program_id(0)
        f = pl.program_id(1)

        @pl.when((i == 0) & (f == 0))
        def _():
            loss_ref[...] = jnp.zeros_like(loss_ref)
            dg2_ref[...] = jnp.zeros_like(dg2_ref)
            db2_ref[...] = jnp.zeros_like(db2_ref)

        @pl.when(f == 0)
        def _():
            xhat, _ = _ln_fwd(r1_ref[...])
            h1 = xhat * g1_ref[...] + b1_ref[...]
            h1f[...] = h1
            h1_ref[...] = h1.astype(h1_ref.dtype)
            acc[...] = jnp.zeros_like(acc)

        up_ref[...] = _dot(h1_ref[...], wu_ref[0]).astype(up_ref.dtype)
        upq = jnp.maximum(up_ref[...].astype(F32), 0.0)
        acc[...] += _dot(upq * upq, wd_ref[...])

        @pl.when(f == nf - 1)
        def _():
            xhat, rstd = _ln_fwd(ALPHA * h1f[...] + acc[...])
            g2 = g2_ref[...]
            diff = xhat * g2 + b2_ref[...] - tg_ref[...]
            loss_ref[...] += 0.5 / d * jnp.sum(diff * diff)
            dh2 = diff * (1.0 / d)
            dg2_ref[0:1, :] += _colsum(dh2 * xhat)
            db2_ref[0:1, :] += _colsum(dh2)
            dr2_ref[...] = _ln_bwd(dh2, xhat, rstd, g2).astype(dr2_ref.dtype)

    row = pl.BlockSpec((tm, d), lambda i, f: (i, 0))
    vec = pl.BlockSpec((1, d), lambda i, f: (0, 0))
    acc8 = pl.BlockSpec((SUBLANES, d), lambda i, f: (0, 0))
    return pl.pallas_call(
        body, name="mlp_fwd",
        grid=(t // tm, nf),
        in_specs=[row, row, pl.BlockSpec((1, d, tf), lambda i, f: (f, 0, 0)), pl.BlockSpec((tf, d), lambda i, f: (f, 0)),
                  vec, vec, vec, vec],
        out_specs=[pl.BlockSpec((tm, tf), lambda i, f: (i, f)), row, row,
                   pl.BlockSpec((SUBLANES, LANES), lambda i, f: (0, 0)), acc8, acc8],
        out_shape=[jax.ShapeDtypeStruct((t, ff), BF16), jax.ShapeDtypeStruct((t, d), BF16),
                   jax.ShapeDtypeStruct((t, d), BF16), jax.ShapeDtypeStruct((SUBLANES, LANES), F32),
                   jax.ShapeDtypeStruct((SUBLANES, d), F32), jax.ShapeDtypeStruct((SUBLANES, d), F32)],
        scratch_shapes=[pltpu.VMEM((tm, d), F32), pltpu.VMEM((tm, d), F32)],
        compiler_params=_params(("arbitrary", "arbitrary")),
    )(r1, target, w_up, w_down, ln1_g, ln1_b, ln2_g, ln2_b)


def _mlp_bwd(dr2, up, r1, w_up, w_down, ln1_g, tm):
    t, d = r1.shape
    nf, _, tf = w_up.shape
    ff = nf * tf

    def body(dr2_ref, up_ref, r1_ref, wu_ref, wd_ref, g1_ref, dup_ref, dr1_ref, dg1_ref, db1_ref, acc):
        i = pl.program_id(0)
        f = pl.program_id(1)

        @pl.when((i == 0) & (f == 0))
        def _():
            dg1_ref[...] = jnp.zeros_like(dg1_ref)
            db1_ref[...] = jnp.zeros_like(db1_ref)

        @pl.when(f == 0)
        def _():
            acc[...] = jnp.zeros_like(acc)

        dact = _dot(dr2_ref[...], wd_ref[...], NT)
        dup_ref[...] = (dact * 2.0 * jnp.maximum(up_ref[...].astype(F32), 0.0)).astype(dup_ref.dtype)
        acc[...] += _dot(dup_ref[...], wu_ref[0], NT)

        @pl.when(f == nf - 1)
        def _():
            dh1 = acc[...] + ALPHA * dr2_ref[...].astype(F32)
            xhat, rstd = _ln_fwd(r1_ref[...])
            dg1_ref[0:1, :] += _colsum(dh1 * xhat)
            db1_ref[0:1, :] += _colsum(dh1)
            dr1_ref[...] = _ln_bwd(dh1, xhat, rstd, g1_ref[...]).astype(dr1_ref.dtype)

    row = pl.BlockSpec((tm, d), lambda i, f: (i, 0))
    acc8 = pl.BlockSpec((SUBLANES, d), lambda i, f: (0, 0))
    return pl.pallas_call(
        body, name="mlp_bwd",
        grid=(t // tm, nf),
        in_specs=[row, pl.BlockSpec((tm, tf), lambda i, f: (i, f)), row,
                  pl.BlockSpec((1, d, tf), lambda i, f: (f, 0, 0)), pl.BlockSpec((tf, d), lambda i, f: (f, 0)),
                  pl.BlockSpec((1, d), lambda i, f: (0, 0))],
        out_specs=[pl.BlockSpec((tm, tf), lambda i, f: (i, f)), row, acc8, acc8],
        out_shape=[jax.ShapeDtypeStruct((t, ff), BF16), jax.ShapeDtypeStruct((t, d), BF16),
                   jax.ShapeDtypeStruct((SUBLANES, d), F32), jax.ShapeDtypeStruct((SUBLANES, d), F32)],
        scratch_shapes=[pltpu.VMEM((tm, d), F32)],
        compiler_params=_params(("arbitrary", "arbitrary")),
    )(dr2, up, r1, w_up, w_down, ln1_g)


def _dx_kernel(segs, w_main, ddt, w_dt, dr1, tm, tk, comm=None):
    t, d = dr1.shape
    nblk = [s.shape[1] // tk for s in segs]
    starts = [sum(nblk[:i]) for i in range(len(segs))]
    nk = sum(nblk)
    nseg = len(segs)

    def body(*refs):
        seg_refs = refs[:nseg]
        w_ref, ddt_ref, wdt_ref, dr1_ref, o_ref, acc = refs[nseg:]
        k = pl.program_id(1)

        @pl.when(k == 0)
        def _():
            acc[...] = ALPHA * dr1_ref[...].astype(F32) + _dot(ddt_ref[...], wdt_ref[...], NT)

        for si in range(nseg):
            @pl.when((k >= starts[si]) & (k < starts[si] + nblk[si]))
            def _(si=si):
                acc[...] += _dot(seg_refs[si][...], w_ref[...], NT)

        @pl.when(k == nk - 1)
        def _():
            o_ref[...] = acc[...]

    def seg_spec(si):
        return pl.BlockSpec((tm, tk), lambda i, k: (i, jnp.clip(k - starts[si], 0, nblk[si] - 1)))

    row = pl.BlockSpec((tm, d), lambda i, k: (i, 0))
    grid = (t // tm, nk)
    c_in, c_in_specs, c_out_specs, c_out_shapes = _comm_specs(comm)
    return pl.pallas_call(
        _fuse_comm(body, grid, nseg + 4, 1, comm), name="dx",
        grid=grid,
        in_specs=[seg_spec(si) for si in range(nseg)] + [
            pl.BlockSpec((d, tk), lambda i, k: (0, k)), pl.BlockSpec((tm, LANES), lambda i, k: (i, 0)),
            pl.BlockSpec((d, LANES), lambda i, k: (0, 0)), row] + c_in_specs,
        out_specs=[row] + c_out_specs,
        out_shape=[jax.ShapeDtypeStruct((t, d), F32)] + c_out_shapes,
        scratch_shapes=[pltpu.VMEM((tm, d), F32)] + (list(comm.scratch) if comm else []),
        compiler_params=_params(("arbitrary", "arbitrary")),
    )(*segs, w_main, ddt, w_dt, dr1, *c_in)


def _dims(d):
    inner = 2 * d
    heads = inner // HEAD_DIM
    cd = inner + 2 * GROUPS * STATE
    assert heads <= LANES and inner % (GROUPS * LANES) == 0 and d % (len(POOL_WINDOWS) * LANES) == 0
    o_z, o_xbc, o_dt, o_lg = d, d + inner, d + inner + cd, d + inner + cd + heads
    return inner, heads, cd, (o_z, o_xbc, o_dt, o_lg)


def _row(v, width=None):
    v = v.reshape(1, -1).astype(F32)
    if width is not None and v.shape[1] < width:
        v = jnp.pad(v, ((0, 0), (0, width - v.shape[1])))
    return v


def _local_step(x2, tgt2, w, shards, core, bl):
    t, d = x2.shape
    inner, heads, cd, _ = _dims(d)
    gs = GROUPS * STATE
    nc = t // bl // CHUNK
    w_main, w_dt = _w_in_internal(w["w_in_blocks"], d)
    c_z, c_lg, c_u = cd, cd + inner, cd + inner + 2 * d
    conv_w8 = jnp.pad(w["conv_w"].astype(F32), ((0, SUBLANES - CONV_K), (0, 0)))
    conv_b = _row(w["conv_b"])
    dtb, alog = _row(w["dt_bias"], LANES), _row(w["a_log"], LANES)
    dskip_x = _row(jnp.repeat(w["d_skip"].reshape(-1), HEAD_DIM))
    normw = _row(w["ssd_norm_w"])
    col_head = lax.broadcasted_iota(jnp.int32, (LANES, inner), 1) // HEAD_DIM
    emat = (col_head == lax.broadcasted_iota(jnp.int32, (LANES, inner), 0)).astype(BF16)
    emat_t = emat.T
    w_main, w_dt = w_main.astype(BF16), w_dt.astype(BF16)
    b_gates, pool_scale = _row(w["b_gates"]), _row(w["pool_scale"])
    ln1_g, ln1_b, ln2_g, ln2_b = _row(w["ln1_g"]), _row(w["ln1_b"]), _row(w["ln2_g"]), _row(w["ln2_b"])

    tm = min(512, t)
    tk = min(1024, d)
    ct = min(512, d)
    rt = min(512, t // bl)
    nct = t // bl // rt
    mm = functools.partial(_matmul, bm=1024, bn=tk, bk=1024)
    xb = x2.astype(BF16)

    proj, xbc, dsl, *gathered = _in_proj(xb, w_main, conv_w8, conv_b, cd, t // bl, 1024, tk,
                                    _all_gather_comm([shards[n] for n in OTHERS]))
    gathered = dict(zip(OTHERS, gathered))
    w_ssd, w_out, w_down = (gathered[n].reshape(-1, d) for n in ("w_ssd_proj", "w_out", "w_down"))
    w_up = gathered["w_up"]
    npg = len(POOL_WINDOWS)
    cg = d // npg
    wpg = gathered["w_pool_group"].reshape(N_DEV, npg, cg // N_DEV, cg).transpose(1, 0, 2, 3).reshape(npg, cg, cg)
    dt_raw = mm(xb, w_dt, "nn", F32, name="in_proj_dt")
    y, yn, states = _ssd_fwd(xbc, proj, dt_raw, dtb, alog, dskip_x, normw, emat, bl, inner, c_z)
    yssd = mm(yn, w_ssd, "nn", BF16, name="ssd_proj")
    ypr = _pool_fwd(proj, wpg, bl, d, c_u)
    merged, r1 = _merge_fwd(proj, ypr, yssd, x2, w_out, b_gates, pool_scale, d, c_lg, tm)
    tmm = min(1024, t)
    up, h1, dr2, loss8, dg2, db2 = _mlp_fwd(r1, tgt2, w_up, w_down, ln1_g, ln1_b, ln2_g, ln2_b, tmm)

    dup, dr1, dg1, db1 = _mlp_bwd(dr2, up, r1, w_up, w_down, ln1_g, tmm)
    relu2 = lambda v: jnp.square(jnp.maximum(v, 0.0))
    g = {}
    g["w_down"] = mm(up, dr2, "tn", BF16, name="dw_down", a_fn=relu2)
    g["w_up"] = mm(h1, dup, "tn", BF16, name="dw_up", col_blocks=N_DEV)
    g["w_out"] = mm(merged, dr1, "tn", BF16, name="dw_out")
    dlg, dyp, dys, dbg, dps = _merge_bwd(dr1, proj, ypr, yssd, w_out, b_gates, pool_scale, d, c_lg, tm)
    du, dwpg = _pool_bwd(proj, dyp, wpg, bl, d, c_u)
    g["w_pool_group"] = dwpg.reshape(npg, N_DEV, cg // N_DEV, cg).transpose(1, 0, 2, 3).reshape(
        N_DEV, npg * cg // N_DEV, cg).astype(BF16)
    dyn = mm(dys, w_ssd, "nt", BF16, name="d_ssd_proj")
    g["w_ssd_proj"] = mm(yn, dys, "tn", BF16, name="dw_ssd_proj")

    def chip_sums(names, tag):
        parts = [g.pop(n).reshape((N_DEV,) + shards_2d[n]) for n in names]
        recv = _run_comm(_rs_sibling_comm(parts), "rs_sibling_" + tag)
        return [_add_pairs(core, p, r, "rs_add_" + n) for n, p, r in zip(names, parts, recv)]

    shards_2d = {n: s.shape for n, s in shards.items()}
    shards_2d["w_in"] = w["w_in_blocks"].shape[1:]
    dxs, dbm, dcm, dz, ddt, dnw, dsk, dalog, ddtb, *recv_others = _ssd_bwd(
        xbc, proj, dt_raw, y, dyn, states, dtb, alog, dskip_x, normw, emat, emat_t, bl, inner, c_z,
        comm=_rs_chips_comm(chip_sums(OTHERS, "a")))
    dxs_p, dcw_x, dcb_x = _conv_bwd(proj, dsl, dxs, conv_w8, nct, 0, inner, ct, rt, "conv_bwd_x")
    dbm_p, dcw_b, dcb_b = _conv_bwd(proj, dsl, dbm, conv_w8, nct, inner, gs, ct, rt, "conv_bwd_b")
    dcm_p, dcw_c, dcb_c = _conv_bwd(proj, dsl, dcm, conv_w8, nct, inner + gs, gs, ct, rt, "conv_bwd_c")
    segs = [dxs_p, dbm_p, dcm_p, dz, dlg, du]
    keys = [k for k, _, _ in _col_segments(d)]
    dws = {k: mm(xb, s, "tn", BF16, name="dw_in_" + k) for k, s in zip(keys, segs + [ddt])}
    g["w_in"] = _w_in_grad_blocks(dws, d, w["w_in_blocks"].shape[2])
    grad_x, recv_w_in = _dx_kernel(segs, w_main, ddt, w_dt, dr1, tmm, min(512, d),
                                   comm=_rs_chips_comm(chip_sums(["w_in"], "b")))
    recv = dict(zip(OTHERS, recv_others))
    recv["w_in"] = recv_w_in
    g["conv_w"] = jnp.concatenate([dcw_x, dcw_b, dcw_c], axis=1)[:CONV_K]
    g["conv_b"] = jnp.concatenate([dcb_x, dcb_b, dcb_c], axis=1)[0]
    g["b_gates"], g["pool_scale"] = dbg[0], dps[0]
    g["dt_bias"], g["a_log"], g["d_skip"] = ddtb[0, :heads], dalog[0, :heads], dsk[0, :heads]
    g["ssd_norm_w"] = dnw[:, 0, :].reshape(inner)
    g["ln1_g"], g["ln1_b"], g["ln2_g"], g["ln2_b"] = dg1[0], db1[0], dg2[0], db2[0]
    return loss8, grad_x, g, recv


BIG = ("w_in", "w_ssd_proj", "w_pool_group", "w_out", "w_up", "w_down")
OTHERS = BIG[1:]
SMALL = ("b_gates", "conv_b", "dt_bias", "a_log", "d_skip", "ssd_norm_w", "pool_scale", "ln1_g", "ln1_b", "ln2_g",
         "ln2_b")
SMALL_PACK = SMALL + ("conv_w",)
NAMES = ("w_in", "b_gates", "conv_w", "conv_b", "dt_bias", "a_log", "d_skip", "ssd_norm_w", "w_ssd_proj",
         "w_pool_group", "pool_scale", "w_out", "ln1_g", "ln1_b", "w_up", "w_down", "ln2_g", "ln2_b")


def _size(shape):
    n = 1
    for s in shape:
        n *= s
    return n


def _rows128(v):
    v = v.astype(F32).reshape((-1, v.shape[-1]))
    n = v.shape[-1]
    v = jnp.pad(v, ((0, 0), (0, -n % LANES)))
    return v.reshape(-1, LANES)


def _pack_small(vals, extra):
    parts = [_rows128(vals[n]) for n in SMALL_PACK]
    parts.append(jnp.pad(extra.reshape(1, 1).astype(F32), ((0, 0), (0, LANES - 1))))
    buf = jnp.concatenate(parts, axis=0)
    return jnp.pad(buf, ((0, -buf.shape[0] % SUBLANES), (0, 0)))


def _unpack_small(buf, shapes):
    out, off = {}, 0
    for n in SMALL_PACK:
        lead, last = _size(shapes[n][:-1]), shapes[n][-1]
        per = -(-last // LANES)
        out[n] = buf[off:off + lead * per].reshape(lead, per * LANES)[:, :last].reshape(shapes[n])
        off += lead * per
    return out, buf[off, 0]


def _col_segments(d):
    inner, heads, cd, (o_z, o_xbc, o_dt, o_lg) = _dims(d)
    gs = GROUPS * STATE
    return [("xs", o_xbc, inner), ("B", o_xbc + inner, gs), ("C", o_xbc + inner + gs, gs), ("z", o_z, inner),
            ("lg", o_lg, 2 * d), ("u", 0, d), ("dt", o_dt, heads)]


def _cols_from_blocks(blocks, start, width, bw):
    parts, pos = [], start
    while pos < start + width:
        k, off = divmod(pos, bw)
        n = min(bw - off, start + width - pos)
        parts.append(blocks[k][:, off:off + n])
        pos += n
    return parts


def _w_in_internal(blocks, d):
    bw = blocks.shape[2]
    segs = _col_segments(d)
    heads = segs[-1][2]
    main = [p for _, s, w_ in segs[:-1] for p in _cols_from_blocks(blocks, s, w_, bw)]
    w_dt = jnp.concatenate(_cols_from_blocks(blocks, segs[-1][1], heads, bw), axis=1)
    return jnp.concatenate(main, axis=1), jnp.pad(w_dt, ((0, 0), (0, LANES - heads)))


def _w_in_grad_blocks(dws, d, bw):
    order = sorted(_col_segments(d), key=lambda s: s[1])
    blocks = []
    for k in range(N_DEV):
        lo, hi, parts = k * bw, (k + 1) * bw, []
        for key, s, w_ in order:
            a, b = max(lo, s), min(hi, s + w_)
            if a < b:
                parts.append(dws[key][:, a - s:b - s])
        blocks.append(jnp.concatenate(parts, axis=1))
    return jnp.stack(blocks)


def _mesh_pos():
    return lax.axis_index("x"), lax.axis_index("y"), lax.axis_index("c")


def _all_gather_comm(shards):
    nw = len(shards)

    def setup(x_refs, out_refs, scr):
        send_sems, recv_sems, local_sems = scr
        x, y, c = _mesh_pos()
        me, sibling = (x, y, c), (x, y, 1 - c)
        chips = [(1 - x, y), (x, 1 - y), (1 - x, 1 - y)]

        def copy(wi, k, block, to, from_input=False):
            px, py, pc = block
            blk = out_refs[wi].at[4 * px + 2 * py + pc]
            return pltpu.make_async_remote_copy(
                src_ref=x_refs[wi] if from_input else blk, dst_ref=blk,
                send_sem=send_sems.at[7 * wi + k], recv_sem=recv_sems.at[7 * wi + k], device_id=to,
                device_id_type=MESH)

        mine = [pltpu.make_async_copy(x_refs[wi], out_refs[wi].at[4 * x + 2 * y + c], local_sems.at[wi])
                for wi in range(nw)]
        sends = []
        for wi in range(nw):
            sends.append(copy(wi, 0, me, sibling, True))
            sends += [copy(wi, 1 + j, me, (*chip, c), True) for j, chip in enumerate(chips)]
        return copy, mine, sends, me, sibling, chips, c

    def start(x_refs, out_refs, scr):
        _, mine, sends, _, _, _, _ = setup(x_refs, out_refs, scr)
        for cp in mine + sends:
            cp.start()

    def wait(x_refs, out_refs, scr):
        copy, mine, sends, me, sibling, chips, c = setup(x_refs, out_refs, scr)
        passed = []
        for wi in range(nw):
            for j, chip in enumerate(chips):
                copy(wi, 1 + j, (*chip, c), me).wait_recv()
                passed.append(copy(wi, 4 + j, (*chip, c), sibling))
                passed[-1].start()
        for wi in range(nw):
            copy(wi, 0, sibling, me).wait_recv()
            for j, chip in enumerate(chips):
                copy(wi, 4 + j, (*chip, 1 - c), me).wait_recv()
        for cp in sends + passed:
            cp.wait_send()
        for cp in mine:
            cp.wait()

    return _Comm(
        inputs=list(shards),
        out_shapes=[jax.ShapeDtypeStruct((N_DEV,) + s.shape, s.dtype) for s in shards],
        scratch=[pltpu.SemaphoreType.DMA((7 * nw,)), pltpu.SemaphoreType.DMA((7 * nw,)),
                 pltpu.SemaphoreType.DMA((nw,))],
        start=start, wait=wait)


def _rs_sibling_comm(parts):
    nw = len(parts)
    half = N_DEV // 2

    def copies(p_refs, recv_refs, scr):
        send_sems, recv_sems = scr
        x, y, c = _mesh_pos()
        return [pltpu.make_async_remote_copy(
            src_ref=p_refs[wi].at[2 * q + 1 - c], dst_ref=recv_refs[wi].at[q],
            send_sem=send_sems.at[half * wi + q], recv_sem=recv_sems.at[half * wi + q],
            device_id=(x, y, 1 - c), device_id_type=MESH) for wi in range(nw) for q in range(half)]

    def start(p_refs, recv_refs, scr):
        for cp in copies(p_refs, recv_refs, scr):
            cp.start()

    def wait(p_refs, recv_refs, scr):
        for cp in copies(p_refs, recv_refs, scr):
            cp.wait()

    return _Comm(
        inputs=list(parts),
        out_shapes=[jax.ShapeDtypeStruct((half,) + p.shape[1:], p.dtype) for p in parts],
        scratch=[pltpu.SemaphoreType.DMA((half * nw,)), pltpu.SemaphoreType.DMA((half * nw,))],
        start=start, wait=wait)


def _rs_chips_comm(tbs):
    nw = len(tbs)

    def copies(t_refs, o_refs, scr):
        send_sems, recv_sems, local_sems = scr
        x, y, c = _mesh_pos()
        p = 2 * x + y
        chips = [(1 - x, y), (x, 1 - y), (1 - x, 1 - y)]
        own = [pltpu.make_async_copy(t_refs[wi].at[p], o_refs[wi].at[p], local_sems.at[wi]) for wi in range(nw)]
        remote = [pltpu.make_async_remote_copy(
            src_ref=t_refs[wi].at[2 * qx + qy], dst_ref=o_refs[wi].at[p], send_sem=send_sems.at[3 * wi + j],
            recv_sem=recv_sems.at[3 * wi + j], device_id=(qx, qy, c), device_id_type=MESH)
            for wi in range(nw) for j, (qx, qy) in enumerate(chips)]
        arriving = [pltpu.make_async_remote_copy(
            src_ref=t_refs[wi].at[p], dst_ref=o_refs[wi].at[2 * qx + qy], send_sem=send_sems.at[3 * wi + j],
            recv_sem=recv_sems.at[3 * wi + j], device_id=(qx, qy, c), device_id_type=MESH)
            for wi in range(nw) for j, (qx, qy) in enumerate(chips)]
        return own, remote, arriving

    def start(t_refs, o_refs, scr):
        own, remote, _ = copies(t_refs, o_refs, scr)
        for cp in own + remote:
            cp.start()

    def wait(t_refs, o_refs, scr):
        own, remote, arriving = copies(t_refs, o_refs, scr)
        for cp in arriving:
            cp.wait_recv()
        for cp in remote:
            cp.wait_send()
        for cp in own:
            cp.wait()

    return _Comm(
        inputs=list(tbs),
        out_shapes=[jax.ShapeDtypeStruct(t_.shape, t_.dtype) for t_ in tbs],
        scratch=[pltpu.SemaphoreType.DMA((3 * nw,)), pltpu.SemaphoreType.DMA((3 * nw,)),
                 pltpu.SemaphoreType.DMA((nw,))],
        start=start, wait=wait)


def _row_tile(rows, cap=256):
    if rows <= cap:
        return rows
    return max(t_ for t_ in range(SUBLANES, cap + 1, SUBLANES) if rows % t_ == 0)


def _add_pairs(core, part, recv, name):
    n, r, c_ = recv.shape
    tr = _row_tile(r)

    def body(core_ref, a_ref, b_ref, o_ref):
        o_ref[...] = (a_ref[...].astype(F32) + b_ref[...].astype(F32)).astype(o_ref.dtype)

    spec = pl.BlockSpec((1, tr, c_), lambda q, i, core_ref: (q, i, 0))
    return pl.pallas_call(
        body, name=name,
        grid_spec=pltpu.PrefetchScalarGridSpec(
            num_scalar_prefetch=1, grid=(n, r // tr),
            in_specs=[pl.BlockSpec((1, tr, c_), lambda q, i, core_ref: (2 * q + core_ref[0], i, 0)), spec],
            out_specs=spec),
        out_shape=jax.ShapeDtypeStruct(recv.shape, BF16), compiler_params=_params(("parallel", "parallel")),
    )(core, part, recv)


def _small_allreduce(vec, name):
    rows = vec.shape[0]

    def body(x_ref, o_ref, buf, send_sems, recv_sems):
        x, y, c = _mesh_pos()
        me = 4 * x + 2 * y + c
        buf[me] = x_ref[...]
        cps = []
        for k in range(1, N_DEV):
            peer = (1 - x if k & 4 else x, 1 - y if k & 2 else y, 1 - c if k & 1 else c)
            cps.append(pltpu.make_async_remote_copy(
                src_ref=x_ref, dst_ref=buf.at[me], send_sem=send_sems.at[k - 1], recv_sem=recv_sems.at[k - 1],
                device_id=peer, device_id_type=MESH))
        for cp in cps:
            cp.start()
        for k in range(1, N_DEV):
            px, py, pc = (1 - x if k & 4 else x, 1 - y if k & 2 else y, 1 - c if k & 1 else c)
            pltpu.make_async_remote_copy(
                src_ref=x_ref, dst_ref=buf.at[4 * px + 2 * py + pc], send_sem=send_sems.at[k - 1],
                recv_sem=recv_sems.at[k - 1], device_id=(px, py, pc), device_id_type=MESH).wait_recv()
        for cp in cps:
            cp.wait_send()
        acc = buf[0]
        for k in range(1, N_DEV):
            acc = acc + buf[k]
        o_ref[...] = acc

    vm = pl.BlockSpec(memory_space=pltpu.VMEM)
    return pl.pallas_call(
        body, name=name,
        in_specs=[vm], out_specs=vm,
        out_shape=jax.ShapeDtypeStruct(vec.shape, F32),
        scratch_shapes=[pltpu.VMEM((N_DEV, rows, LANES), F32), pltpu.SemaphoreType.DMA((N_DEV - 1,)),
                        pltpu.SemaphoreType.DMA((N_DEV - 1,))],
    )(vec)


def _adamw(gparts, w, m, v, name):
    n, r, c_ = gparts.shape
    tr = _row_tile(r)
    c1 = 1.0 / (1.0 - B1 ** STEP)
    c2 = 1.0 / (1.0 - B2 ** STEP)

    def body(g_ref, w_ref, m_ref, v_ref, go_ref, d_ref, mo_ref, vo_ref):
        g = g_ref[0].astype(F32)
        for q in range(1, n):
            g = g + g_ref[q].astype(F32)
        mn = B1 * m_ref[...] + (1.0 - B1) * g
        vn = B2 * v_ref[...] + (1.0 - B2) * (g * g)
        go_ref[...] = g
        mo_ref[...] = mn
        vo_ref[...] = vn
        d_ref[...] = -LR * ((mn * c1) / (jnp.sqrt(vn * c2) + ADAM_EPS) + WD * w_ref[...])

    spec = pl.BlockSpec((tr, c_), lambda i: (i, 0))
    out = jax.ShapeDtypeStruct((r, c_), F32)
    return pl.pallas_call(
        body, name=name, grid=(r // tr,),
        in_specs=[pl.BlockSpec((n, tr, c_), lambda i: (0, i, 0)), spec, spec, spec],
        out_specs=[spec] * 4, out_shape=[out] * 4, compiler_params=_params(("parallel",)),
    )(gparts, w, m, v)


def kernel(x, w_in, b_gates, conv_w, conv_b, dt_bias, a_log, d_skip, ssd_norm_w, w_ssd_proj, w_pool_group, pool_scale, w_out, ln1_g, ln1_b, w_up, w_down, ln2_g, ln2_b, loss_target, m_w_in, m_b_gates, m_conv_w, m_conv_b, m_dt_bias, m_a_log, m_d_skip, m_ssd_norm_w, m_w_ssd_proj, m_w_pool_group, m_pool_scale, m_w_out, m_ln1_g, m_ln1_b, m_w_up, m_w_down, m_ln2_g, m_ln2_b, v_w_in, v_b_gates, v_conv_w, v_conv_b, v_dt_bias, v_a_log, v_d_skip, v_ssd_norm_w, v_w_ssd_proj, v_w_pool_group, v_pool_scale, v_w_out, v_ln1_g, v_ln1_b, v_w_up, v_w_down, v_ln2_g, v_ln2_b):
    ws = (w_in, b_gates, conv_w, conv_b, dt_bias, a_log, d_skip, ssd_norm_w, w_ssd_proj, w_pool_group, pool_scale,
          w_out, ln1_g, ln1_b, w_up, w_down, ln2_g, ln2_b)
    ms = (m_w_in, m_b_gates, m_conv_w, m_conv_b, m_dt_bias, m_a_log, m_d_skip, m_ssd_norm_w, m_w_ssd_proj,
          m_w_pool_group, m_pool_scale, m_w_out, m_ln1_g, m_ln1_b, m_w_up, m_w_down, m_ln2_g, m_ln2_b)
    vs = (v_w_in, v_b_gates, v_conv_w, v_conv_b, v_dt_bias, v_a_log, v_d_skip, v_ssd_norm_w, v_w_ssd_proj,
          v_w_pool_group, v_pool_scale, v_w_out, v_ln1_g, v_ln1_b, v_w_up, v_w_down, v_ln2_g, v_ln2_b)
    w = {n: a[0] for n, a in zip(NAMES, ws)}
    m = {n: a[0] for n, a in zip(NAMES, ms)}
    v = {n: a[0] for n, a in zip(NAMES, vs)}
    out_shapes = {n: a.shape for n, a in zip(NAMES, ws)}
    bl, s, d = x.shape
    x2, tgt2 = x.reshape(bl * s, d), loss_target.reshape(bl * s, d)
    xi, yi, ci = _mesh_pos()
    me = 4 * xi + 2 * yi + ci
    zero = jnp.zeros((), F32)
    shapes = {n: w[n].shape for n in NAMES}
    shape2d = {n: (_size(shapes[n][:-1]), shapes[n][-1]) for n in BIG}
    cwl = shapes["conv_w"][1]

    conv_place = lax.dynamic_update_slice(jnp.zeros((CONV_K, N_DEV * cwl), F32), w["conv_w"], (0, me * cwl))
    conv_full = _small_allreduce(_rows128(conv_place), "gather_conv_w")
    conv_full = conv_full.reshape(CONV_K, N_DEV * cwl)

    shards = {n: w[n].astype(BF16).reshape(shape2d[n]) for n in BIG}
    full = {n: w[n] for n in SMALL}
    full["conv_w"] = conv_full
    full["w_in_blocks"] = _run_comm(_all_gather_comm([shards.pop("w_in")]), "all_gather_w_in")[0]
    loss8, grad_x, g, recv = _local_step(x2, tgt2, full, shards, ci.astype(jnp.int32).reshape(1), bl)

    small_sum = _small_allreduce(_pack_small(g, loss8[0, 0]), "small_allreduce")
    ex_shapes = {n: shapes[n] for n in SMALL}
    ex_shapes["conv_w"] = (CONV_K, N_DEV * cwl)
    gsum, loss = _unpack_small(small_sum, ex_shapes)
    gsum["conv_w"] = lax.dynamic_slice(gsum["conv_w"], (0, me * cwl), (CONV_K, cwl))
    gs_pk = _pack_small(gsum, zero)
    ws_pk, ms_pk, vs_pk = (_pack_small(t_, zero) for t_ in (w, m, v))
    small_out = _adamw(gs_pk[None], ws_pk, ms_pk, vs_pk, "adamw_small")
    loc_shapes = {n: shapes[n] for n in SMALL_PACK}
    res = [_unpack_small(o, loc_shapes)[0] for o in small_out]

    for n in BIG:
        outs = _adamw(recv[n], *(t_[n].reshape(shape2d[n]) for t_ in (w, m, v)), "adamw_" + n)
        for r_, o in zip(res, outs):
            r_[n] = o

    def ordered(r_):
        return [r_[n].reshape(out_shapes[n]) for n in NAMES]

    return (loss, grad_x.reshape(bl, s, d), *ordered(res[0]), *ordered(res[1]), *ordered(res[2]), *ordered(res[3]))
```

```python
import collections
import functools

import jax
import jax.numpy as jnp
from jax import lax
from jax.experimental import pallas as pl
from jax.experimental.pallas import tpu as pltpu

F32 = jnp.float32
BF16 = jnp.bfloat16
MESH = pl.DeviceIdType.MESH

HEAD_DIM = 64
STATE = 128
GROUPS = 8
CONV_K = 4
CHUNK = 256
POOL_WINDOWS = (2, 4, 8, 16)
ALPHA = 2.0 ** 0.25
LN_EPS = 1e-5
RMS_EPS = 1e-5
LR, B1, B2, ADAM_EPS, WD, STEP = 0.001, 0.9, 0.999, 1e-08, 0.01, 10
N_DEV = 8
LANES = 128
SUBLANES = 8
VMEM_LIMIT = 56 * 1024 * 1024
NEG_BIG = -1e30

NN = (((1,), (0,)), ((), ()))
NT = (((1,), (1,)), ((), ()))
TN = (((0,), (0,)), ((), ()))


def _dot(a, b, dims=NN):
    return lax.dot_general(a.astype(BF16), b.astype(BF16), dims, preferred_element_type=F32)


def _dot_exact01(q, e, dims=NN):
    hi = q.astype(BF16)
    r1 = q - hi.astype(F32)
    mid = r1.astype(BF16)
    lo = (r1 - mid.astype(F32)).astype(BF16)
    f = lambda p: lax.dot_general(p, e, dims, preferred_element_type=F32)
    return f(hi) + f(mid) + f(lo)


def _params(sem):
    return pltpu.CompilerParams(dimension_semantics=sem, vmem_limit_bytes=VMEM_LIMIT)


def _sigmoid(x):
    return 1.0 / (1.0 + jnp.exp(-x))


def _colsum(x):
    return jnp.sum(x, axis=0, keepdims=True)


def _ln_fwd(r):
    mu = jnp.mean(r, axis=-1, keepdims=True)
    xc = r - mu
    var = jnp.mean(xc * xc, axis=-1, keepdims=True)
    rstd = lax.rsqrt(var + LN_EPS)
    return xc * rstd, rstd


def _ln_bwd(dy, xhat, rstd, g):
    dxh = dy * g
    m1 = jnp.mean(dxh, axis=-1, keepdims=True)
    m2 = jnp.mean(dxh * xhat, axis=-1, keepdims=True)
    return rstd * (dxh - m1 - xhat * m2)


_Comm = collections.namedtuple("_Comm", "inputs out_shapes scratch start wait")
ANY = pl.BlockSpec(memory_space=pl.ANY)


def _fuse_comm(body, grid, n_in, n_out, comm):
    if comm is None:
        return body
    ci, co = len(comm.inputs), len(comm.out_shapes)

    def fused(*refs):
        ins, cins = refs[:n_in], refs[n_in:n_in + ci]
        o0 = n_in + ci
        outs, couts = refs[o0:o0 + n_out], refs[o0 + n_out:o0 + n_out + co]
        rest = refs[o0 + n_out + co:]
        scr, cscr = rest[:len(rest) - len(comm.scratch)], rest[len(rest) - len(comm.scratch):]
        ids = [pl.program_id(a) for a in range(len(grid))]
        first, last = ids[0] == 0, ids[0] == grid[0] - 1
        for a in range(1, len(grid)):
            first, last = first & (ids[a] == 0), last & (ids[a] == grid[a] - 1)

        @pl.when(first)
        def _():
            comm.start(cins, couts, cscr)

        body(*ins, *outs, *scr)

        @pl.when(last)
        def _():
            comm.wait(cins, couts, cscr)

    return fused


def _comm_specs(comm):
    if comm is None:
        return [], [], [], []
    return list(comm.inputs), [ANY] * len(comm.inputs), [ANY] * len(comm.out_shapes), list(comm.out_shapes)


def _run_comm(comm, name):
    ci, co = len(comm.inputs), len(comm.out_shapes)

    def body(*refs):
        comm.start(refs[:ci], refs[ci:ci + co], refs[ci + co:])
        comm.wait(refs[:ci], refs[ci:ci + co], refs[ci + co:])

    return pl.pallas_call(body, name=name, in_specs=[ANY] * ci, out_specs=[ANY] * co, out_shape=list(comm.out_shapes),
                          scratch_shapes=list(comm.scratch))(*comm.inputs)


def _matmul(a, b, mode, out_dtype, bm, bn, bk, name, a_fn=None, col_blocks=0, comm=None):
    if mode == "nn":
        (m, k), n, dims = a.shape, b.shape[1], NN
    elif mode == "nt":
        (m, k), n, dims = a.shape, b.shape[0], NT
    else:
        (k, m), n, dims = a.shape, b.shape[1], TN
    bm, bn, bk = min(bm, m), min(bn, n), min(bk, k)
    assert m % bm == 0 and n % bn == 0 and k % bk == 0, (name, m, n, k, bm, bn, bk)
    nk = k // bk
    if mode == "nn":
        a_spec = pl.BlockSpec((bm, bk), lambda i, j, kk: (i, kk))
        b_spec = pl.BlockSpec((bk, bn), lambda i, j, kk: (kk, j))
    elif mode == "nt":
        a_spec = pl.BlockSpec((bm, bk), lambda i, j, kk: (i, kk))
        b_spec = pl.BlockSpec((bn, bk), lambda i, j, kk: (j, kk))
    else:
        a_spec = pl.BlockSpec((bk, bm), lambda i, j, kk: (kk, i))
        b_spec = pl.BlockSpec((bk, bn), lambda i, j, kk: (kk, j))

    def body(a_ref, b_ref, o_ref, acc_ref):
        kk = pl.program_id(2)

        @pl.when(kk == 0)
        def _():
            acc_ref[...] = jnp.zeros_like(acc_ref)

        av = a_ref[...]
        if a_fn is not None:
            av = a_fn(av.astype(F32))
        acc_ref[...] += _dot(av, b_ref[...], dims)

        @pl.when(kk == nk - 1)
        def _():
            if col_blocks:
                for s in range(bn // slab):
                    o_ref[s] = acc_ref[:, s * slab:(s + 1) * slab].astype(o_ref.dtype)
            else:
                o_ref[...] = acc_ref[...].astype(o_ref.dtype)

    if col_blocks:
        slab = n // col_blocks
        assert n % col_blocks == 0 and bn % slab == 0, (name, n, col_blocks, bn)
        out_spec = pl.BlockSpec((bn // slab, bm, slab), lambda i, j, kk: (j, i, 0))
        out_shape = jax.ShapeDtypeStruct((col_blocks, m, slab), out_dtype)
    else:
        out_spec = pl.BlockSpec((bm, bn), lambda i, j, kk: (i, j))
        out_shape = jax.ShapeDtypeStruct((m, n), out_dtype)
    grid = (m // bm, n // bn, nk)
    c_in, c_in_specs, c_out_specs, c_out_shapes = _comm_specs(comm)
    res = pl.pallas_call(
        _fuse_comm(body, grid, 2, 1, comm), name=name,
        grid=grid,
        in_specs=[a_spec, b_spec] + c_in_specs,
        out_specs=[out_spec] + c_out_specs,
        out_shape=[out_shape] + c_out_shapes,
        scratch_shapes=[pltpu.VMEM((bm, bn), F32)] + (list(comm.scratch) if comm else []),
        compiler_params=_params(("arbitrary",) * 3 if comm else ("parallel", "parallel", "arbitrary")),
    )(a, b, *c_in)
    return res if comm else res[0]


CONV_STRIP = 16


def _conv_pre(ext_ref, w_ref, b_ref, r0, rows):
    acc = b_ref[...] + w_ref[0:1, :] * ext_ref[pl.ds(r0 + SUBLANES - (CONV_K - 1), rows), :]
    for k in range(1, CONV_K):
        acc = acc + w_ref[k:k + 1, :] * ext_ref[pl.ds(r0 + SUBLANES - (CONV_K - 1) + k, rows), :]
    return acc


def _in_proj(xb, w_main, conv_w8, conv_b, cd, seq_len, bm, bn, comm):
    t, d = xb.shape
    pw = w_main.shape[1]
    bm, bn = min(bm, seq_len), min(bn, d)
    assert t % bm == 0 and seq_len % bm == 0 and pw % bn == 0 and cd % bn == 0
    ncj = cd // bn
    tiles_per_seq = seq_len // bm

    def body(x_ref, w_ref, cw_ref, cb_ref, p_ref, xbc_ref, dsl_ref, ext_ref, carry_ref):
        i = pl.program_id(0)
        j = pl.program_id(1)

        def conv_previous():
            ext = ext_ref.at[(j + 1) % 2]
            for r0 in range(0, bm, CONV_STRIP):
                acc = _conv_pre(ext, cw_ref, cb_ref, r0, CONV_STRIP)
                sg = _sigmoid(acc)
                xbc_ref[r0:r0 + CONV_STRIP, :] = (acc * sg).astype(xbc_ref.dtype)
                dsl_ref[r0:r0 + CONV_STRIP, :] = (sg * (1.0 + acc * (1.0 - sg))).astype(dsl_ref.dtype)

        def project(stash):
            pq = _dot(x_ref[...], w_ref[...]).astype(BF16)
            p_ref[...] = pq
            if stash:
                ext = ext_ref.at[j % 2]
                jc = jnp.minimum(j, ncj - 1)
                ext[0:SUBLANES, :] = jnp.where((i % tiles_per_seq) == 0, 0.0, carry_ref[jc])
                ext[SUBLANES:, :] = pq.astype(F32)
                carry_ref[jc] = ext[bm:bm + SUBLANES, :]

        @pl.when(j == 0)
        def _():
            project(True)

        @pl.when((j >= 1) & (j < ncj))
        def _():
            conv_previous()
            project(True)

        @pl.when(j == ncj)
        def _():
            conv_previous()
            project(False)

        @pl.when(j > ncj)
        def _():
            project(False)

    assert pw // bn > ncj
    grid = (t // bm, pw // bn)
    conv_col = lambda i, j: (0, jnp.clip(j - 1, 0, ncj - 1))
    c_in, c_in_specs, c_out_specs, c_out_shapes = _comm_specs(comm)
    conv_tile = pl.BlockSpec((bm, bn), lambda i, j: (i, jnp.clip(j - 1, 0, ncj - 1)))
    conv_out = jax.ShapeDtypeStruct((t, cd), BF16)
    return pl.pallas_call(
        _fuse_comm(body, grid, 4, 3, comm), name="in_proj",
        grid=grid,
        in_specs=[pl.BlockSpec((bm, d), lambda i, j: (i, 0)), pl.BlockSpec((d, bn), lambda i, j: (0, j)),
                  pl.BlockSpec((SUBLANES, bn), conv_col), pl.BlockSpec((1, bn), conv_col)] + c_in_specs,
        out_specs=[pl.BlockSpec((bm, bn), lambda i, j: (i, j)), conv_tile, conv_tile] + c_out_specs,
        out_shape=[jax.ShapeDtypeStruct((t, pw), BF16), conv_out, conv_out] + c_out_shapes,
        scratch_shapes=[pltpu.VMEM((2, bm + SUBLANES, bn), F32), pltpu.VMEM((ncj, SUBLANES, bn), F32)]
        + (list(comm.scratch) if comm else []),
        compiler_params=_params(("arbitrary", "arbitrary")),
    )(xb, w_main, conv_w8, conv_b, *c_in)


def _conv_bwd(proj, dsilu, dxbc, conv_w8, n_seq_chunks, col0, width, ct, L, name):
    t = proj.shape[0]
    nbc = t // L
    hb = L // SUBLANES
    ct = min(ct, width)
    assert col0 % ct == 0 and width % ct == 0
    cb0 = col0 // ct
    last_hb = t // SUBLANES - 1

    def body(x_ref, xb_ref, s_ref, sa_ref, d_ref, da_ref, w_ref, o_ref, dw_ref, db_ref, ext_ref, dc_ref):
        bc = pl.program_id(1)
        first = (bc % n_seq_chunks) == 0
        last = (bc % n_seq_chunks) == n_seq_chunks - 1

        @pl.when(bc == 0)
        def _():
            dw_ref[...] = jnp.zeros_like(dw_ref)
            db_ref[...] = jnp.zeros_like(db_ref)

        ext_ref[0:SUBLANES, :] = jnp.where(first, 0.0, xb_ref[...].astype(F32))
        ext_ref[SUBLANES:, :] = x_ref[...].astype(F32)
        for r0 in range(0, L, CONV_STRIP):
            rows = slice(r0, r0 + CONV_STRIP)
            dc_ref[rows, :] = d_ref[rows, :].astype(F32) * s_ref[rows, :].astype(F32)
        dc_ref[L:, :] = jnp.where(last, 0.0, da_ref[...].astype(F32)) * sa_ref[...].astype(F32)
        fold = lambda v: v[0:SUBLANES] + v[SUBLANES:CONV_STRIP]
        dws = [jnp.zeros((SUBLANES, ct), F32) for _ in range(CONV_K)]
        dbs = jnp.zeros((SUBLANES, ct), F32)
        for r0 in range(0, L, CONV_STRIP):
            dc = dc_ref[r0:r0 + CONV_STRIP, :]
            dx = w_ref[CONV_K - 1:CONV_K, :] * dc
            for k in range(CONV_K - 1):
                dx = dx + w_ref[k:k + 1, :] * dc_ref[pl.ds(r0 + CONV_K - 1 - k, CONV_STRIP), :]
            o_ref[r0:r0 + CONV_STRIP, :] = dx.astype(o_ref.dtype)
            for k in range(CONV_K):
                dws[k] = dws[k] + fold(dc * ext_ref[pl.ds(r0 + SUBLANES - (CONV_K - 1) + k, CONV_STRIP), :])
            dbs = dbs + fold(dc)
        for k in range(CONV_K):
            dw_ref[k:k + 1, :] += _colsum(dws[k])
        db_ref[0:1, :] += _colsum(dbs)

    return pl.pallas_call(
        body, name=name,
        grid=(width // ct, nbc),
        in_specs=[
            pl.BlockSpec((L, ct), lambda j, i: (i, cb0 + j)),
            pl.BlockSpec((SUBLANES, ct), lambda j, i: (jnp.maximum(i * hb - 1, 0), cb0 + j)),
            pl.BlockSpec((L, ct), lambda j, i: (i, cb0 + j)),
            pl.BlockSpec((SUBLANES, ct), lambda j, i: (jnp.minimum((i + 1) * hb, last_hb), cb0 + j)),
            pl.BlockSpec((L, ct), lambda j, i: (i, j)),
            pl.BlockSpec((SUBLANES, ct), lambda j, i: (jnp.minimum((i + 1) * hb, last_hb), j)),
            pl.BlockSpec((SUBLANES, ct), lambda j, i: (0, cb0 + j)),
        ],
        out_specs=[
            pl.BlockSpec((L, ct), lambda j, i: (i, j)),
            pl.BlockSpec((SUBLANES, ct), lambda j, i: (0, j)),
            pl.BlockSpec((SUBLANES, ct), lambda j, i: (0, j)),
        ],
        out_shape=[
            jax.ShapeDtypeStruct((t, width), BF16),
            jax.ShapeDtypeStruct((SUBLANES, width), F32),
            jax.ShapeDtypeStruct((SUBLANES, width), F32),
        ],
        scratch_shapes=[pltpu.VMEM((L + SUBLANES, ct), F32), pltpu.VMEM((L + SUBLANES, ct), F32)],
        compiler_params=_params(("parallel", "arbitrary")),
    )(proj, proj, dsilu, dsilu, dxbc, dxbc, conv_w8)


def _cumsum_rows(x, reverse=False):
    n = x.shape[0]
    row = lax.broadcasted_iota(jnp.int32, x.shape, 0)
    s = 1
    while s < n:
        if reverse:
            x = x + jnp.where(row < n - s, pltpu.roll(x, n - s, 0), 0.0)
        else:
            x = x + jnp.where(row >= s, pltpu.roll(x, s, 0), 0.0)
        s *= 2
    return x


def _ssd_scalars(dtr, dtb, alog):
    pre = dtr + dtb
    dt = jnp.maximum(pre, 0.0) + jnp.log(1.0 + jnp.exp(-jnp.abs(pre)))
    a = -jnp.exp(alog)
    acs = _cumsum_rows(dt * a) * LOG2E
    n = acs.shape[0]
    return pre, dt, a, acs, jnp.exp2(acs), jnp.exp2(acs[n - 1:n, :] - acs)


LOG2E = 1.4426950408889634


def _dot_2piece(q, e):
    hi = q.astype(BF16)
    mid = (q - hi.astype(F32)).astype(BF16)
    return lax.dot_general(jnp.concatenate([hi, mid], axis=1), jnp.concatenate([e, e], axis=0), NN,
                           preferred_element_type=F32)


def _ssd_group_common(dt_s, e_s, dec_s, e):
    return _dot_2piece(dt_s, e), _dot_2piece(e_s, e), _dot_2piece(dec_s, e)


def _decay_matrix(acs, acs_t, h, tri):
    return jnp.exp2(jnp.where(tri, acs[:, h:h + 1] - acs_t[h:h + 1, :], NEG_BIG))


def _head_mask(r, gw, dtype):
    lane = lax.broadcasted_iota(jnp.int32, (1, gw), 1)
    return ((lane >= r * HEAD_DIM) & (lane < (r + 1) * HEAD_DIM)).astype(dtype)


def _ssd_fwd(xbc, proj, dt_raw, dtb, alog, dskip_x, normw, emat, bl, inner, z_col0):
    t = xbc.shape[0]
    L = CHUNK
    nc = t // bl // L
    G = GROUPS
    gw = inner // G
    hpg = gw // HEAD_DIM
    assert z_col0 % gw == 0
    zb0 = z_col0 // gw
    bb0 = inner // STATE
    cb0 = bb0 + G

    P = G
    assert bb0 % P == 0 and cb0 % P == 0 and zb0 % P == 0

    def body(xs_ref, b_ref, c_ref, z_ref, dtr_ref, dtb_ref, alog_ref, dsk_ref, nw_ref, e_ref,
             y_ref, yn_ref, st_ref, h_ref):
        c = pl.program_id(1)
        _, dt_s, _, acs, e_s, dec_s = _ssd_scalars(dtr_ref[...], dtb_ref[...], alog_ref[...])
        acs_t = acs.T
        tri = lax.broadcasted_iota(jnp.int32, (L, L), 0) >= lax.broadcasted_iota(jnp.int32, (L, L), 1)
        lane = lax.broadcasted_iota(jnp.int32, (L, gw), 1)
        for g in range(G):
            cols = slice(g * gw, (g + 1) * gw)
            ncol = slice(g * STATE, (g + 1) * STATE)

            @pl.when(c == 0)
            def _():
                h_ref[g] = jnp.zeros((STATE, gw), F32)

            xs = xs_ref[:, cols].astype(F32)
            bg = b_ref[:, ncol]
            cg = c_ref[:, ncol]
            dt_x, e_x, dec_x = _ssd_group_common(dt_s, e_s, dec_s, e_ref[:, cols])
            xdt = xs * dt_x
            cb = _dot(cg, bg, NT)
            h = h_ref[g]
            st_ref[0, g] = h
            y = _dot(cg, h) * e_x + dsk_ref[:, cols] * xs
            for r in range(hpg):
                m = cb * _decay_matrix(acs, acs_t, g * hpg + r, tri)
                xr = jnp.where((lane >= r * HEAD_DIM) & (lane < (r + 1) * HEAD_DIM), xdt, 0.0)
                y = y + _dot(m, xr)
            h_ref[g] = h * e_x[L - 1:L, :] + _dot(bg, xdt * dec_x, TN)
            yq = y.astype(y_ref.dtype)
            y_ref[:, cols] = yq
            z = z_ref[:, cols].astype(F32)
            yg = yq.astype(F32) * (z * _sigmoid(z))
            rs = lax.rsqrt(jnp.mean(yg * yg, axis=-1, keepdims=True) + RMS_EPS)
            yn_ref[:, cols] = (yg * rs * nw_ref[:, cols]).astype(yn_ref.dtype)

    return pl.pallas_call(
        body, name="ssd_fwd",
        grid=(bl, nc, G // P),
        in_specs=[
            pl.BlockSpec((L, P * gw), lambda b, c, g: (b * nc + c, g)),
            pl.BlockSpec((L, P * STATE), lambda b, c, g: (b * nc + c, bb0 // P + g)),
            pl.BlockSpec((L, P * STATE), lambda b, c, g: (b * nc + c, cb0 // P + g)),
            pl.BlockSpec((L, P * gw), lambda b, c, g: (b * nc + c, zb0 // P + g)),
            pl.BlockSpec((L, LANES), lambda b, c, g: (b * nc + c, 0)),
            pl.BlockSpec((1, LANES), lambda b, c, g: (0, 0)),
            pl.BlockSpec((1, LANES), lambda b, c, g: (0, 0)),
            pl.BlockSpec((1, P * gw), lambda b, c, g: (0, g)),
            pl.BlockSpec((1, P * gw), lambda b, c, g: (0, g)),
            pl.BlockSpec((LANES, P * gw), lambda b, c, g: (0, g)),
        ],
        out_specs=[
            pl.BlockSpec((L, P * gw), lambda b, c, g: (b * nc + c, g)),
            pl.BlockSpec((L, P * gw), lambda b, c, g: (b * nc + c, g)),
            pl.BlockSpec((1, P, STATE, gw), lambda b, c, g: (b * nc + c, g, 0, 0)),
        ],
        out_shape=[
            jax.ShapeDtypeStruct((t, inner), BF16),
            jax.ShapeDtypeStruct((t, inner), BF16),
            jax.ShapeDtypeStruct((bl * nc, G, STATE, gw), F32),
        ],
        scratch_shapes=[pltpu.VMEM((G, STATE, gw), F32)],
        compiler_params=_params(("arbitrary", "arbitrary", "arbitrary")),
    )(xbc, xbc, xbc, proj, dt_raw, dtb, alog, dskip_x, normw, emat)


def _ssd_bwd(xbc, proj, dt_raw, y, dyn, states, dtb, alog, dskip_x, normw, emat, emat_t, bl, inner, z_col0,
             comm=None):
    t = xbc.shape[0]
    L = CHUNK
    nc = t // bl // L
    G = GROUPS
    gw = inner // G
    hpg = gw // HEAD_DIM
    zb0 = z_col0 // gw
    bb0 = inner // STATE
    cb0 = bb0 + G
    P = G

    def rc(j):
        return nc - 1 - j

    def body(xs_ref, b_ref, c_ref, z_ref, dtr_ref, y_ref, dyn_ref, st_ref, dtb_ref, alog_ref, dsk_ref,
             nw_ref, e_ref, et_ref,
             dxs_ref, db_ref, dc_ref, dz_ref, ddt_ref, dnw_ref, dsk_acc, dalog_acc, ddtb_acc,
             dh_ref):
        b = pl.program_id(0)
        j = pl.program_id(1)

        @pl.when((b == 0) & (j == 0))
        def _():
            dsk_acc[...] = jnp.zeros_like(dsk_acc)
            dalog_acc[...] = jnp.zeros_like(dalog_acc)
            ddtb_acc[...] = jnp.zeros_like(ddtb_acc)

        pre, dt_s, a_row, acs, e_s, dec_s = _ssd_scalars(dtr_ref[...], dtb_ref[...], alog_ref[...])
        acs_t = acs.T
        wacs = jnp.zeros((L, LANES), F32)
        wdt = jnp.zeros((L, LANES), F32)
        tri = lax.broadcasted_iota(jnp.int32, (L, L), 0) >= lax.broadcasted_iota(jnp.int32, (L, L), 1)
        rowi = lax.broadcasted_iota(jnp.int32, (L, gw), 0)
        for g in range(G):
            cols = slice(g * gw, (g + 1) * gw)
            ncol = slice(g * STATE, (g + 1) * STATE)

            @pl.when((b == 0) & (j == 0))
            def _():
                dnw_ref[g] = jnp.zeros((SUBLANES, gw), F32)

            @pl.when(j == 0)
            def _():
                dh_ref[g] = jnp.zeros((STATE, gw), F32)

            xs = xs_ref[:, cols].astype(F32)
            bg = b_ref[:, ncol]
            cg = c_ref[:, ncol]
            dt_x, e_x, dec_x = _ssd_group_common(dt_s, e_s, dec_s, e_ref[:, cols])
            xdt = xs * dt_x
            xdt_b = xdt.astype(BF16)
            cb = _dot(cg, bg, NT)
            h = st_ref[0, g]
            hb16 = h.astype(BF16)
            dsk = dsk_ref[:, cols]

            yv = y_ref[:, cols].astype(F32)
            z = z_ref[:, cols].astype(F32)
            sgz = _sigmoid(z)
            sz = z * sgz
            yg = yv * sz
            rs = lax.rsqrt(jnp.mean(yg * yg, axis=-1, keepdims=True) + RMS_EPS)
            yhat = yg * rs
            dyn_v = dyn_ref[:, cols].astype(F32)
            dnw_ref[g] += _colsum(dyn_v * yhat)
            dyh = dyn_v * nw_ref[:, cols]
            dyg = rs * (dyh - yhat * jnp.mean(dyh * yhat, axis=-1, keepdims=True))
            dy = dyg * sz
            dz_ref[:, cols] = (dyg * yv * (sgz * (1.0 + z * (1.0 - sgz)))).astype(dz_ref.dtype)

            dy_b = dy.astype(BF16)
            dcb = jnp.zeros((L, L), F32)
            dxdt_d = jnp.zeros((L, gw), F32)
            ydiag = jnp.zeros((L, gw), F32)
            for r in range(hpg):
                lm = _decay_matrix(acs, acs_t, g * hpg + r, tri)
                m = (cb * lm).astype(BF16)
                hm = _head_mask(r, gw, BF16)
                dyr = dy_b * hm
                xr = xdt_b * hm
                ydiag = ydiag + _dot(m, xr)
                dcb = dcb + _dot(dyr, xdt_b, NT) * lm
                dxdt_d = dxdt_d + _dot(m, dyr, TN)
            dh = dh_ref[g]
            dh16 = dh.astype(BF16)
            xdec_b = (xdt * dec_x).astype(BF16)
            bdh = _dot(bg, dh16)
            dxdt = dxdt_d + dec_x * bdh
            dcb16 = dcb.astype(BF16)
            dye = (dy * e_x).astype(BF16)
            db_ref[:, ncol] = (_dot(dcb16, cg, TN) + _dot(xdec_b, dh16, NT)).astype(db_ref.dtype)
            dc_ref[:, ncol] = (_dot(dcb16, bg) + _dot(dye, hb16, NT)).astype(dc_ref.dtype)
            dprev = _dot(cg, dye, TN)
            cd_row = e_x[L - 1:L, :]
            s_new = _dot(bg, xdec_b, TN)
            last_term = _colsum(dh16.astype(F32) * s_new) + _colsum(dh * h) * cd_row
            yoff = _dot(cg, hb16) * e_x
            wfold = (dy_b.astype(F32) * ydiag + dy * yoff - dxdt_d * xdt_b.astype(F32) - bdh * xdec_b.astype(F32)
                     + jnp.where(rowi == L - 1, last_term, 0.0))
            et = et_ref[cols, :]
            wacs = wacs + _dot_2piece(wfold, et)
            wdt = wdt + _dot_2piece(dxdt * xs, et)
            dsk_acc[...] += _dot_exact01(jnp.broadcast_to(_colsum(dy * xs), (SUBLANES, gw)), et)
            dxs_ref[:, cols] = (dsk * dy + dxdt * dt_x).astype(dxs_ref.dtype)
            dh_ref[g] = dprev + cd_row * dh

        dda = _cumsum_rows(wacs, reverse=True)
        ddt_raw = (wdt + dda * a_row) * _sigmoid(pre)
        ddt_ref[...] = ddt_raw
        dalog_acc[...] += _colsum(dda * dt_s) * a_row
        ddtb_acc[...] += _colsum(ddt_raw)

    def cidx(b, j):
        return b * nc + rc(j)

    accs = lambda shape: pl.BlockSpec(shape, lambda b, j, g: tuple(0 for _ in shape))
    grid = (bl, nc, G // P)
    c_in, c_in_specs, c_out_specs, c_out_shapes = _comm_specs(comm)
    return pl.pallas_call(
        _fuse_comm(body, grid, 14, 9, comm), name="ssd_bwd",
        grid=grid,
        in_specs=[
            pl.BlockSpec((L, P * gw), lambda b, j, g: (cidx(b, j), g)),
            pl.BlockSpec((L, P * STATE), lambda b, j, g: (cidx(b, j), bb0 // P + g)),
            pl.BlockSpec((L, P * STATE), lambda b, j, g: (cidx(b, j), cb0 // P + g)),
            pl.BlockSpec((L, P * gw), lambda b, j, g: (cidx(b, j), zb0 // P + g)),
            pl.BlockSpec((L, LANES), lambda b, j, g: (cidx(b, j), 0)),
            pl.BlockSpec((L, P * gw), lambda b, j, g: (cidx(b, j), g)),
            pl.BlockSpec((L, P * gw), lambda b, j, g: (cidx(b, j), g)),
            pl.BlockSpec((1, P, STATE, gw), lambda b, j, g: (cidx(b, j), g, 0, 0)),
            pl.BlockSpec((1, LANES), lambda b, j, g: (0, 0)),
            pl.BlockSpec((1, LANES), lambda b, j, g: (0, 0)),
            pl.BlockSpec((1, P * gw), lambda b, j, g: (0, g)),
            pl.BlockSpec((1, P * gw), lambda b, j, g: (0, g)),
            pl.BlockSpec((LANES, P * gw), lambda b, j, g: (0, g)),
            pl.BlockSpec((P * gw, LANES), lambda b, j, g: (g, 0)),
        ] + c_in_specs,
        out_specs=[
            pl.BlockSpec((L, P * gw), lambda b, j, g: (cidx(b, j), g)),
            pl.BlockSpec((L, P * STATE), lambda b, j, g: (cidx(b, j), g)),
            pl.BlockSpec((L, P * STATE), lambda b, j, g: (cidx(b, j), g)),
            pl.BlockSpec((L, P * gw), lambda b, j, g: (cidx(b, j), g)),
            pl.BlockSpec((L, LANES), lambda b, j, g: (cidx(b, j), 0)),
            accs((G, SUBLANES, gw)),
            accs((SUBLANES, LANES)),
            accs((SUBLANES, LANES)),
            accs((SUBLANES, LANES)),
        ] + c_out_specs,
        out_shape=[
            jax.ShapeDtypeStruct((t, inner), BF16),
            jax.ShapeDtypeStruct((t, G * STATE), BF16),
            jax.ShapeDtypeStruct((t, G * STATE), BF16),
            jax.ShapeDtypeStruct((t, inner), BF16),
            jax.ShapeDtypeStruct((t, LANES), F32),
            jax.ShapeDtypeStruct((G, SUBLANES, gw), F32),
            jax.ShapeDtypeStruct((SUBLANES, LANES), F32),
            jax.ShapeDtypeStruct((SUBLANES, LANES), F32),
            jax.ShapeDtypeStruct((SUBLANES, LANES), F32),
        ] + c_out_shapes,
        scratch_shapes=[pltpu.VMEM((G, STATE, gw), F32)]
        + (list(comm.scratch) if comm else []),
        compiler_params=_params(("arbitrary", "arbitrary", "arbitrary")),
    )(xbc, xbc, xbc, proj, dt_raw, y, dyn, states, dtb, alog, dskip_x, normw, emat, emat_t, *c_in)


def _pool_window(u, w, anti):
    n = u.shape[0]
    row = lax.broadcasted_iota(jnp.int32, u.shape, 0)
    acc = u
    s = 1
    while s < w:
        if anti:
            acc = acc + jnp.where(row < n - s, pltpu.roll(acc, n - s, 0), 0.0)
        else:
            acc = acc + jnp.where(row >= s, pltpu.roll(acc, s, 0), 0.0)
        s *= 2
    return acc


def _pool_cnt(shape, w):
    row = lax.broadcasted_iota(jnp.int32, shape, 0)
    return jnp.minimum(row + 1, w).astype(F32)


def _pool_fwd(proj, wpg, bl, d, u_col0):
    t = proj.shape[0]
    s = t // bl
    pg = len(POOL_WINDOWS)
    cg = d // pg
    ub0 = u_col0 // d

    def body(u_ref, w_ref, o_ref):
        for gi, w in enumerate(POOL_WINDOWS):
            u = u_ref[:, gi * cg:(gi + 1) * cg].astype(F32)
            pooled = _pool_window(u, w, False) / _pool_cnt(u.shape, w) - u
            o_ref[:, gi * cg:(gi + 1) * cg] = _dot(pooled, w_ref[gi]).astype(o_ref.dtype)

    return pl.pallas_call(
        body, name="pool_fwd",
        grid=(bl,),
        in_specs=[pl.BlockSpec((s, d), lambda b: (b, ub0)), pl.BlockSpec((pg, cg, cg), lambda b: (0, 0, 0))],
        out_specs=pl.BlockSpec((s, d), lambda b: (b, 0)),
        out_shape=jax.ShapeDtypeStruct((t, d), BF16),
        compiler_params=_params(("parallel",)),
    )(proj, wpg)


def _pool_bwd(proj, dyp, wpg, bl, d, u_col0):
    t = proj.shape[0]
    s = t // bl
    pg = len(POOL_WINDOWS)
    cg = d // pg
    ub0 = u_col0 // d

    def body(u_ref, dy_ref, w_ref, du_ref, dw_ref):
        @pl.when(pl.program_id(0) == 0)
        def _():
            dw_ref[...] = jnp.zeros_like(dw_ref)

        for gi, w in enumerate(POOL_WINDOWS):
            u = u_ref[:, gi * cg:(gi + 1) * cg].astype(F32)
            cnt = _pool_cnt(u.shape, w)
            pooled = _pool_window(u, w, False) / cnt - u
            dy = dy_ref[:, gi * cg:(gi + 1) * cg]
            dw_ref[gi] += _dot(pooled, dy, TN)
            dp = _dot(dy, w_ref[gi], NT)
            du_ref[:, gi * cg:(gi + 1) * cg] = (_pool_window(dp / cnt, w, True) - dp).astype(du_ref.dtype)

    return pl.pallas_call(
        body, name="pool_bwd",
        grid=(bl,),
        in_specs=[pl.BlockSpec((s, d), lambda b: (b, ub0)), pl.BlockSpec((s, d), lambda b: (b, 0)),
                  pl.BlockSpec((pg, cg, cg), lambda b: (0, 0, 0))],
        out_specs=[pl.BlockSpec((s, d), lambda b: (b, 0)), pl.BlockSpec((pg, cg, cg), lambda b: (0, 0, 0))],
        out_shape=[jax.ShapeDtypeStruct((t, d), BF16), jax.ShapeDtypeStruct((pg, cg, cg), F32)],
        compiler_params=_params(("arbitrary",)),
    )(proj, dyp, wpg)


def _merge_fwd(proj, ypr, yssd, x, w_out, b_gates, pool_scale, d, lg_col0, tm):
    t = x.shape[0]
    lb0 = lg_col0 // (2 * d)

    def body(lg_ref, yp_ref, ys_ref, x_ref, w_ref, bg_ref, ps_ref, mg_ref, r1_ref):
        lg = lg_ref[...].astype(F32) + bg_ref[...]
        ga = _sigmoid(lg[:, :d])
        gb = _sigmoid(lg[:, d:])
        merged = ga * (yp_ref[...].astype(F32) * ps_ref[...]) + gb * ys_ref[...].astype(F32)
        mg_ref[...] = merged.astype(mg_ref.dtype)
        r1_ref[...] = ALPHA * x_ref[...] + _dot(mg_ref[...], w_ref[...])

    row = lambda w: pl.BlockSpec((tm, w), lambda i: (i, 0))
    full = lambda a: pl.BlockSpec(a.shape, lambda i: (0, 0))
    return pl.pallas_call(
        body, name="merge_fwd",
        grid=(t // tm,),
        in_specs=[pl.BlockSpec((tm, 2 * d), lambda i: (i, lb0)), row(d), row(d), row(d), full(w_out), full(b_gates),
                  full(pool_scale)],
        out_specs=[row(d), row(d)],
        out_shape=[jax.ShapeDtypeStruct((t, d), BF16), jax.ShapeDtypeStruct((t, d), F32)],
        compiler_params=_params(("parallel",)),
    )(proj, ypr, yssd, x, w_out, b_gates, pool_scale)


def _merge_bwd(dr1, proj, ypr, yssd, w_out, b_gates, pool_scale, d, lg_col0, tm):
    t = dr1.shape[0]
    lb0 = lg_col0 // (2 * d)

    def body(dr_ref, lg_ref, yp_ref, ys_ref, w_ref, bg_ref, ps_ref, dlg_ref, dyp_ref, dys_ref, dbg_ref, dps_ref):
        @pl.when(pl.program_id(0) == 0)
        def _():
            dbg_ref[...] = jnp.zeros_like(dbg_ref)
            dps_ref[...] = jnp.zeros_like(dps_ref)

        dm = _dot(dr_ref[...], w_ref[...], NT)
        lg = lg_ref[...].astype(F32) + bg_ref[...]
        ga = _sigmoid(lg[:, :d])
        gb = _sigmoid(lg[:, d:])
        ypr_v = yp_ref[...].astype(F32)
        ys_v = ys_ref[...].astype(F32)
        ps = ps_ref[...]
        dga = dm * ypr_v * ps
        dla = dga * ga * (1.0 - ga)
        dlb = dm * ys_v * gb * (1.0 - gb)
        dlg_ref[:, :d] = dla.astype(dlg_ref.dtype)
        dlg_ref[:, d:] = dlb.astype(dlg_ref.dtype)
        dyp_ref[...] = (dm * ga * ps).astype(dyp_ref.dtype)
        dys_ref[...] = (dm * gb).astype(dys_ref.dtype)
        dbg_ref[0:1, :d] += _colsum(dla)
        dbg_ref[0:1, d:] += _colsum(dlb)
        dps_ref[0:1, :] += _colsum(dm * ga * ypr_v)

    row = lambda w: pl.BlockSpec((tm, w), lambda i: (i, 0))
    full = lambda a: pl.BlockSpec(a.shape, lambda i: (0, 0))
    acc = lambda w: pl.BlockSpec((SUBLANES, w), lambda i: (0, 0))
    return pl.pallas_call(
        body, name="merge_bwd",
        grid=(t // tm,),
        in_specs=[row(d), pl.BlockSpec((tm, 2 * d), lambda i: (i, lb0)), row(d), row(d), full(w_out), full(b_gates),
                  full(pool_scale)],
        out_specs=[row(2 * d), row(d), row(d), acc(2 * d), acc(d)],
        out_shape=[jax.ShapeDtypeStruct((t, 2 * d), BF16), jax.ShapeDtypeStruct((t, d), BF16),
                   jax.ShapeDtypeStruct((t, d), BF16), jax.ShapeDtypeStruct((SUBLANES, 2 * d), F32),
                   jax.ShapeDtypeStruct((SUBLANES, d), F32)],
        compiler_params=_params(("arbitrary",)),
    )(dr1, proj, ypr, yssd, w_out, b_gates, pool_scale)


MLP_SLABS_PER_STEP = 2


def _mlp_fwd(r1, target, w_up, w_down, ln1_g, ln1_b, ln2_g, ln2_b, tm):
    t, d = r1.shape
    ns, _, sw = w_up.shape
    spb = MLP_SLABS_PER_STEP
    assert ns % spb == 0
    nf, tf, ff = ns // spb, spb * sw, ns * sw

    def body(r1_ref, tg_ref, wu_ref, wd_ref, g1_ref, b1_ref, g2_ref, b2_ref,
             up_ref, h1_ref, dr2_ref, loss_ref, dg2_ref, db2_ref, h1f, acc):
        i = pl.program_id(0)
        f = pl.program_id(1)

        @pl.when((i == 0) & (f == 0))
        def _():
            loss_ref[...] = jnp.zeros_like(loss_ref)
            dg2_ref[...] = jnp.zeros_like(dg2_ref)
            db2_ref[...] = jnp.zeros_like(db2_ref)

        @pl.when(f == 0)
        def _():
            xhat, _ = _ln_fwd(r1_ref[...])
            h1 = xhat * g1_ref[...] + b1_ref[...]
            h1f[...] = h1
            h1_ref[...] = h1.astype(h1_ref.dtype)
            acc[...] = jnp.zeros_like(acc)

        for s in range(spb):
            up_ref[:, s * sw:(s + 1) * sw] = _dot(h1_ref[...], wu_ref[s]).astype(up_ref.dtype)
        upq = jnp.maximum(up_ref[...].astype(F32), 0.0)
        acc[...] += _dot(upq * upq, wd_ref[...])

        @pl.when(f == nf - 1)
        def _():
            xhat, rstd = _ln_fwd(ALPHA * h1f[...] + acc[...])
            g2 = g2_ref[...]
            diff = xhat * g2 + b2_ref[...] - tg_ref[...]
            loss_ref[...] += 0.5 / d * jnp.sum(diff * diff)
            dh2 = diff * (1.0 / d)
            dg2_ref[0:1, :] += _colsum(dh2 * xhat)
            db2_ref[0:1, :] += _colsum(dh2)
            dr2_ref[...] = _ln_bwd(dh2, xhat, rstd, g2).astype(dr2_ref.dtype)

    row = pl.BlockSpec((tm, d), lambda i, f: (i, 0))
    vec = pl.BlockSpec((1, d), lambda i, f: (0, 0))
    acc8 = pl.BlockSpec((SUBLANES, d), lambda i, f: (0, 0))
    return pl.pallas_call(
        body, name="mlp_fwd",
        grid=(t // tm, nf),
        in_specs=[row, row, pl.BlockSpec((spb, d, sw), lambda i, f: (f, 0, 0)), pl.BlockSpec((tf, d), lambda i, f: (f, 0)),
                  vec, vec, vec, vec],
        out_specs=[pl.BlockSpec((tm, tf), lambda i, f: (i, f)), row, row,
                   pl.BlockSpec((SUBLANES, LANES), lambda i, f: (0, 0)), acc8, acc8],
        out_shape=[jax.ShapeDtypeStruct((t, ff), BF16), jax.ShapeDtypeStruct((t, d), BF16),
                   jax.ShapeDtypeStruct((t, d), BF16), jax.ShapeDtypeStruct((SUBLANES, LANES), F32),
                   jax.ShapeDtypeStruct((SUBLANES, d), F32), jax.ShapeDtypeStruct((SUBLANES, d), F32)],
        scratch_shapes=[pltpu.VMEM((tm, d), F32), pltpu.VMEM((tm, d), F32)],
        compiler_params=_params(("arbitrary", "arbitrary")),
    )(r1, target, w_up, w_down, ln1_g, ln1_b, ln2_g, ln2_b)


def _mlp_bwd(dr2, up, r1, w_up, w_down, ln1_g, tm):
    t, d = r1.shape
    ns, _, sw = w_up.shape
    spb = MLP_SLABS_PER_STEP
    assert ns % spb == 0
    nf, tf, ff = ns // spb, spb * sw, ns * sw

    def body(dr2_ref, up_ref, r1_ref, wu_ref, wd_ref, g1_ref, dup_ref, dr1_ref, dg1_ref, db1_ref, acc):
        i = pl.program_id(0)
        f = pl.program_id(1)

        @pl.when((i == 0) & (f == 0))
        def _():
            dg1_ref[...] = jnp.zeros_like(dg1_ref)
            db1_ref[...] = jnp.zeros_like(db1_ref)

        @pl.when(f == 0)
        def _():
            acc[...] = jnp.zeros_like(acc)

        dact = _dot(dr2_ref[...], wd_ref[...], NT)
        dup_ref[...] = (dact * 2.0 * jnp.maximum(up_ref[...].astype(F32), 0.0)).astype(dup_ref.dtype)
        for s in range(spb):
            acc[...] += _dot(dup_ref[:, s * sw:(s + 1) * sw], wu_ref[s], NT)

        @pl.when(f == nf - 1)
        def _():
            dh1 = acc[...] + ALPHA * dr2_ref[...].astype(F32)
            xhat, rstd = _ln_fwd(r1_ref[...])
            dg1_ref[0:1, :] += _colsum(dh1 * xhat)
            db1_ref[0:1, :] += _colsum(dh1)
            dr1_ref[...] = _ln_bwd(dh1, xhat, rstd, g1_ref[...]).astype(dr1_ref.dtype)

    row = pl.BlockSpec((tm, d), lambda i, f: (i, 0))
    acc8 = pl.BlockSpec((SUBLANES, d), lambda i, f: (0, 0))
    return pl.pallas_call(
        body, name="mlp_bwd",
        grid=(t // tm, nf),
        in_specs=[row, pl.BlockSpec((tm, tf), lambda i, f: (i, f)), row,
                  pl.BlockSpec((spb, d, sw), lambda i, f: (f, 0, 0)), pl.BlockSpec((tf, d), lambda i, f: (f, 0)),
                  pl.BlockSpec((1, d), lambda i, f: (0, 0))],
        out_specs=[pl.BlockSpec((tm, tf), lambda i, f: (i, f)), row, acc8, acc8],
        out_shape=[jax.ShapeDtypeStruct((t, ff), BF16), jax.ShapeDtypeStruct((t, d), BF16),
                   jax.ShapeDtypeStruct((SUBLANES, d), F32), jax.ShapeDtypeStruct((SUBLANES, d), F32)],
        scratch_shapes=[pltpu.VMEM((tm, d), F32)],
        compiler_params=_params(("arbitrary", "arbitrary")),
    )(dr2, up, r1, w_up, w_down, ln1_g)


def _dx_kernel(segs, w_main, ddt, w_dt, dr1, tm, tk, comm=None):
    t, d = dr1.shape
    nblk = [s.shape[1] // tk for s in segs]
    starts = [sum(nblk[:i]) for i in range(len(segs))]
    nk = sum(nblk)
    nseg = len(segs)

    def body(*refs):
        seg_refs = refs[:nseg]
        w_ref, ddt_ref, wdt_ref, dr1_ref, o_ref, acc = refs[nseg:]
        k = pl.program_id(1)

        @pl.when(k == 0)
        def _():
            acc[...] = ALPHA * dr1_ref[...].astype(F32) + _dot(ddt_ref[...], wdt_ref[...], NT)

        for si in range(nseg):
            @pl.when((k >= starts[si]) & (k < starts[si] + nblk[si]))
            def _(si=si):
                acc[...] += _dot(seg_refs[si][...], w_ref[...], NT)

        @pl.when(k == nk - 1)
        def _():
            o_ref[...] = acc[...]

    def seg_spec(si):
        return pl.BlockSpec((tm, tk), lambda i, k: (i, jnp.clip(k - starts[si], 0, nblk[si] - 1)))

    row = pl.BlockSpec((tm, d), lambda i, k: (i, 0))
    grid = (t // tm, nk)
    c_in, c_in_specs, c_out_specs, c_out_shapes = _comm_specs(comm)
    return pl.pallas_call(
        _fuse_comm(body, grid, nseg + 4, 1, comm), name="dx",
        grid=grid,
        in_specs=[seg_spec(si) for si in range(nseg)] + [
            pl.BlockSpec((d, tk), lambda i, k: (0, k)), pl.BlockSpec((tm, LANES), lambda i, k: (i, 0)),
            pl.BlockSpec((d, LANES), lambda i, k: (0, 0)), row] + c_in_specs,
        out_specs=[row] + c_out_specs,
        out_shape=[jax.ShapeDtypeStruct((t, d), F32)] + c_out_shapes,
        scratch_shapes=[pltpu.VMEM((tm, d), F32)] + (list(comm.scratch) if comm else []),
        compiler_params=_params(("arbitrary", "arbitrary")),
    )(*segs, w_main, ddt, w_dt, dr1, *c_in)


def _dims(d):
    inner = 2 * d
    heads = inner // HEAD_DIM
    cd = inner + 2 * GROUPS * STATE
    assert heads <= LANES and inner % (GROUPS * LANES) == 0 and d % (len(POOL_WINDOWS) * LANES) == 0
    o_z, o_xbc, o_dt, o_lg = d, d + inner, d + inner + cd, d + inner + cd + heads
    return inner, heads, cd, (o_z, o_xbc, o_dt, o_lg)


def _row(v, width=None):
    v = v.reshape(1, -1).astype(F32)
    if width is not None and v.shape[1] < width:
        v = jnp.pad(v, ((0, 0), (0, width - v.shape[1])))
    return v


def _local_step(x2, tgt2, w, shards, core, bl):
    t, d = x2.shape
    inner, heads, cd, _ = _dims(d)
    gs = GROUPS * STATE
    nc = t // bl // CHUNK
    w_main, w_dt = _w_in_internal(w["w_in_blocks"], d)
    c_z, c_lg, c_u = cd, cd + inner, cd + inner + 2 * d
    conv_w8 = jnp.pad(w["conv_w"].astype(F32), ((0, SUBLANES - CONV_K), (0, 0)))
    conv_b = _row(w["conv_b"])
    dtb, alog = _row(w["dt_bias"], LANES), _row(w["a_log"], LANES)
    dskip_x = _row(jnp.repeat(w["d_skip"].reshape(-1), HEAD_DIM))
    normw = _row(w["ssd_norm_w"])
    col_head = lax.broadcasted_iota(jnp.int32, (LANES, inner), 1) // HEAD_DIM
    emat = (col_head == lax.broadcasted_iota(jnp.int32, (LANES, inner), 0)).astype(BF16)
    emat_t = emat.T
    w_main, w_dt = w_main.astype(BF16), w_dt.astype(BF16)
    b_gates, pool_scale = _row(w["b_gates"]), _row(w["pool_scale"])
    ln1_g, ln1_b, ln2_g, ln2_b = _row(w["ln1_g"]), _row(w["ln1_b"]), _row(w["ln2_g"]), _row(w["ln2_b"])

    tm = min(512, t)
    tk = min(1024, d)
    ct = min(512, d)
    rt = min(512, t // bl)
    nct = t // bl // rt
    mm = functools.partial(_matmul, bm=1024, bn=tk, bk=1024)
    xb = x2.astype(BF16)

    proj, xbc, dsl, *gathered = _in_proj(xb, w_main, conv_w8, conv_b, cd, t // bl, 1024, tk,
                                    _all_gather_comm([shards[n] for n in OTHERS]))
    gathered = dict(zip(OTHERS, gathered))
    w_ssd, w_out, w_down = (gathered[n].reshape(-1, d) for n in ("w_ssd_proj", "w_out", "w_down"))
    w_up = gathered["w_up"]
    npg = len(POOL_WINDOWS)
    cg = d // npg
    wpg = gathered["w_pool_group"].reshape(N_DEV, npg, cg // N_DEV, cg).transpose(1, 0, 2, 3).reshape(npg, cg, cg)
    dt_raw = mm(xb, w_dt, "nn", F32, name="in_proj_dt")
    y, yn, states = _ssd_fwd(xbc, proj, dt_raw, dtb, alog, dskip_x, normw, emat, bl, inner, c_z)
    yssd = mm(yn, w_ssd, "nn", BF16, name="ssd_proj")
    ypr = _pool_fwd(proj, wpg, bl, d, c_u)
    merged, r1 = _merge_fwd(proj, ypr, yssd, x2, w_out, b_gates, pool_scale, d, c_lg, tm)
    tmm = min(1024, t)
    up, h1, dr2, loss8, dg2, db2 = _mlp_fwd(r1, tgt2, w_up, w_down, ln1_g, ln1_b, ln2_g, ln2_b, tmm)

    dup, dr1, dg1, db1 = _mlp_bwd(dr2, up, r1, w_up, w_down, ln1_g, tmm)
    relu2 = lambda v: jnp.square(jnp.maximum(v, 0.0))
    g = {}
    g["w_down"] = mm(up, dr2, "tn", BF16, name="dw_down", a_fn=relu2)
    g["w_up"] = mm(h1, dup, "tn", BF16, name="dw_up", col_blocks=N_DEV)
    g["w_out"] = mm(merged, dr1, "tn", BF16, name="dw_out")
    dlg, dyp, dys, dbg, dps = _merge_bwd(dr1, proj, ypr, yssd, w_out, b_gates, pool_scale, d, c_lg, tm)
    du, dwpg = _pool_bwd(proj, dyp, wpg, bl, d, c_u)
    g["w_pool_group"] = dwpg.reshape(npg, N_DEV, cg // N_DEV, cg).transpose(1, 0, 2, 3).reshape(
        N_DEV, npg * cg // N_DEV, cg).astype(BF16)
    dyn = mm(dys, w_ssd, "nt", BF16, name="d_ssd_proj")
    g["w_ssd_proj"] = mm(yn, dys, "tn", BF16, name="dw_ssd_proj")

    def chip_sums(names, tag):
        parts = [g.pop(n).reshape((N_DEV,) + shards_2d[n]) for n in names]
        recv = _run_comm(_rs_sibling_comm(parts), "rs_sibling_" + tag)
        return [_add_pairs(core, p, r, "rs_add_" + n) for n, p, r in zip(names, parts, recv)]

    shards_2d = {n: s.shape for n, s in shards.items()}
    shards_2d["w_in"] = w["w_in_blocks"].shape[1:]
    dxs, dbm, dcm, dz, ddt, dnw, dsk, dalog, ddtb, *recv_others = _ssd_bwd(
        xbc, proj, dt_raw, y, dyn, states, dtb, alog, dskip_x, normw, emat, emat_t, bl, inner, c_z,
        comm=_rs_chips_comm(chip_sums(OTHERS, "a")))
    dxs_p, dcw_x, dcb_x = _conv_bwd(proj, dsl, dxs, conv_w8, nct, 0, inner, ct, rt, "conv_bwd_x")
    dbm_p, dcw_b, dcb_b = _conv_bwd(proj, dsl, dbm, conv_w8, nct, inner, gs, ct, rt, "conv_bwd_b")
    dcm_p, dcw_c, dcb_c = _conv_bwd(proj, dsl, dcm, conv_w8, nct, inner + gs, gs, ct, rt, "conv_bwd_c")
    segs = [dxs_p, dbm_p, dcm_p, dz, dlg, du]
    keys = [k for k, _, _ in _col_segments(d)]
    dws = {k: mm(xb, s, "tn", BF16, name="dw_in_" + k) for k, s in zip(keys, segs + [ddt])}
    g["w_in"] = _w_in_grad_blocks(dws, d, w["w_in_blocks"].shape[2])
    grad_x, recv_w_in = _dx_kernel(segs, w_main, ddt, w_dt, dr1, tmm, tk,
                                   comm=_rs_chips_comm(chip_sums(["w_in"], "b")))
    recv = dict(zip(OTHERS, recv_others))
    recv["w_in"] = recv_w_in
    g["conv_w"] = jnp.concatenate([dcw_x, dcw_b, dcw_c], axis=1)[:CONV_K]
    g["conv_b"] = jnp.concatenate([dcb_x, dcb_b, dcb_c], axis=1)[0]
    g["b_gates"], g["pool_scale"] = dbg[0], dps[0]
    g["dt_bias"], g["a_log"], g["d_skip"] = ddtb[0, :heads], dalog[0, :heads], dsk[0, :heads]
    g["ssd_norm_w"] = dnw[:, 0, :].reshape(inner)
    g["ln1_g"], g["ln1_b"], g["ln2_g"], g["ln2_b"] = dg1[0], db1[0], dg2[0], db2[0]
    return loss8, grad_x, g, recv


BIG = ("w_in", "w_ssd_proj", "w_pool_group", "w_out", "w_up", "w_down")
OTHERS = BIG[1:]
SMALL = ("b_gates", "conv_b", "dt_bias", "a_log", "d_skip", "ssd_norm_w", "pool_scale", "ln1_g", "ln1_b", "ln2_g",
         "ln2_b")
SMALL_PACK = SMALL + ("conv_w",)
NAMES = ("w_in", "b_gates", "conv_w", "conv_b", "dt_bias", "a_log", "d_skip", "ssd_norm_w", "w_ssd_proj",
         "w_pool_group", "pool_scale", "w_out", "ln1_g", "ln1_b", "w_up", "w_down", "ln2_g", "ln2_b")


def _size(shape):
    n = 1
    for s in shape:
        n *= s
    return n


def _rows128(v):
    v = v.astype(F32).reshape((-1, v.shape[-1]))
    n = v.shape[-1]
    v = jnp.pad(v, ((0, 0), (0, -n % LANES)))
    return v.reshape(-1, LANES)


def _pack_small(vals, extra):
    parts = [_rows128(vals[n]) for n in SMALL_PACK]
    parts.append(jnp.pad(extra.reshape(1, 1).astype(F32), ((0, 0), (0, LANES - 1))))
    buf = jnp.concatenate(parts, axis=0)
    return jnp.pad(buf, ((0, -buf.shape[0] % SUBLANES), (0, 0)))


def _unpack_small(buf, shapes):
    out, off = {}, 0
    for n in SMALL_PACK:
        lead, last = _size(shapes[n][:-1]), shapes[n][-1]
        per = -(-last // LANES)
        out[n] = buf[off:off + lead * per].reshape(lead, per * LANES)[:, :last].reshape(shapes[n])
        off += lead * per
    return out, buf[off, 0]


def _col_segments(d):
    inner, heads, cd, (o_z, o_xbc, o_dt, o_lg) = _dims(d)
    gs = GROUPS * STATE
    return [("xs", o_xbc, inner), ("B", o_xbc + inner, gs), ("C", o_xbc + inner + gs, gs), ("z", o_z, inner),
            ("lg", o_lg, 2 * d), ("u", 0, d), ("dt", o_dt, heads)]


def _cols_from_blocks(blocks, start, width, bw):
    parts, pos = [], start
    while pos < start + width:
        k, off = divmod(pos, bw)
        n = min(bw - off, start + width - pos)
        parts.append(blocks[k][:, off:off + n])
        pos += n
    return parts


def _w_in_internal(blocks, d):
    bw = blocks.shape[2]
    segs = _col_segments(d)
    heads = segs[-1][2]
    main = [p for _, s, w_ in segs[:-1] for p in _cols_from_blocks(blocks, s, w_, bw)]
    w_dt = jnp.concatenate(_cols_from_blocks(blocks, segs[-1][1], heads, bw), axis=1)
    return jnp.concatenate(main, axis=1), jnp.pad(w_dt, ((0, 0), (0, LANES - heads)))


def _w_in_grad_blocks(dws, d, bw):
    order = sorted(_col_segments(d), key=lambda s: s[1])
    blocks = []
    for k in range(N_DEV):
        lo, hi, parts = k * bw, (k + 1) * bw, []
        for key, s, w_ in order:
            a, b = max(lo, s), min(hi, s + w_)
            if a < b:
                parts.append(dws[key][:, a - s:b - s])
        blocks.append(jnp.concatenate(parts, axis=1))
    return jnp.stack(blocks)


def _mesh_pos():
    return lax.axis_index("x"), lax.axis_index("y"), lax.axis_index("c")


def _all_gather_comm(shards):
    nw = len(shards)

    def setup(x_refs, out_refs, scr):
        send_sems, recv_sems, local_sems = scr
        x, y, c = _mesh_pos()
        me, sibling = (x, y, c), (x, y, 1 - c)
        chips = [(1 - x, y), (x, 1 - y), (1 - x, 1 - y)]

        def copy(wi, k, block, to, from_input=False):
            px, py, pc = block
            blk = out_refs[wi].at[4 * px + 2 * py + pc]
            return pltpu.make_async_remote_copy(
                src_ref=x_refs[wi] if from_input else blk, dst_ref=blk,
                send_sem=send_sems.at[7 * wi + k], recv_sem=recv_sems.at[7 * wi + k], device_id=to,
                device_id_type=MESH)

        mine = [pltpu.make_async_copy(x_refs[wi], out_refs[wi].at[4 * x + 2 * y + c], local_sems.at[wi])
                for wi in range(nw)]
        sends = []
        for wi in range(nw):
            sends.append(copy(wi, 0, me, sibling, True))
            sends += [copy(wi, 1 + j, me, (*chip, c), True) for j, chip in enumerate(chips)]
        return copy, mine, sends, me, sibling, chips, c

    def start(x_refs, out_refs, scr):
        _, mine, sends, _, _, _, _ = setup(x_refs, out_refs, scr)
        for cp in mine + sends:
            cp.start()

    def wait(x_refs, out_refs, scr):
        copy, mine, sends, me, sibling, chips, c = setup(x_refs, out_refs, scr)
        passed = []
        for wi in range(nw):
            for j, chip in enumerate(chips):
                copy(wi, 1 + j, (*chip, c), me).wait_recv()
                passed.append(copy(wi, 4 + j, (*chip, c), sibling))
                passed[-1].start()
        for wi in range(nw):
            copy(wi, 0, sibling, me).wait_recv()
            for j, chip in enumerate(chips):
                copy(wi, 4 + j, (*chip, 1 - c), me).wait_recv()
        for cp in sends + passed:
            cp.wait_send()
        for cp in mine:
            cp.wait()

    return _Comm(
        inputs=list(shards),
        out_shapes=[jax.ShapeDtypeStruct((N_DEV,) + s.shape, s.dtype) for s in shards],
        scratch=[pltpu.SemaphoreType.DMA((7 * nw,)), pltpu.SemaphoreType.DMA((7 * nw,)),
                 pltpu.SemaphoreType.DMA((nw,))],
        start=start, wait=wait)


def _rs_sibling_comm(parts):
    nw = len(parts)
    half = N_DEV // 2

    def copies(p_refs, recv_refs, scr):
        send_sems, recv_sems = scr
        x, y, c = _mesh_pos()
        return [pltpu.make_async_remote_copy(
            src_ref=p_refs[wi].at[2 * q + 1 - c], dst_ref=recv_refs[wi].at[q],
            send_sem=send_sems.at[half * wi + q], recv_sem=recv_sems.at[half * wi + q],
            device_id=(x, y, 1 - c), device_id_type=MESH) for wi in range(nw) for q in range(half)]

    def start(p_refs, recv_refs, scr):
        for cp in copies(p_refs, recv_refs, scr):
            cp.start()

    def wait(p_refs, recv_refs, scr):
        for cp in copies(p_refs, recv_refs, scr):
            cp.wait()

    return _Comm(
        inputs=list(parts),
        out_shapes=[jax.ShapeDtypeStruct((half,) + p.shape[1:], p.dtype) for p in parts],
        scratch=[pltpu.SemaphoreType.DMA((half * nw,)), pltpu.SemaphoreType.DMA((half * nw,))],
        start=start, wait=wait)


def _rs_chips_comm(tbs):
    nw = len(tbs)

    def copies(t_refs, o_refs, scr):
        send_sems, recv_sems, local_sems = scr
        x, y, c = _mesh_pos()
        p = 2 * x + y
        chips = [(1 - x, y), (x, 1 - y), (1 - x, 1 - y)]
        own = [pltpu.make_async_copy(t_refs[wi].at[p], o_refs[wi].at[p], local_sems.at[wi]) for wi in range(nw)]
        remote = [pltpu.make_async_remote_copy(
            src_ref=t_refs[wi].at[2 * qx + qy], dst_ref=o_refs[wi].at[p], send_sem=send_sems.at[3 * wi + j],
            recv_sem=recv_sems.at[3 * wi + j], device_id=(qx, qy, c), device_id_type=MESH)
            for wi in range(nw) for j, (qx, qy) in enumerate(chips)]
        arriving = [pltpu.make_async_remote_copy(
            src_ref=t_refs[wi].at[p], dst_ref=o_refs[wi].at[2 * qx + qy], send_sem=send_sems.at[3 * wi + j],
            recv_sem=recv_sems.at[3 * wi + j], device_id=(qx, qy, c), device_id_type=MESH)
            for wi in range(nw) for j, (qx, qy) in enumerate(chips)]
        return own, remote, arriving

    def start(t_refs, o_refs, scr):
        own, remote, _ = copies(t_refs, o_refs, scr)
        for cp in own + remote:
            cp.start()

    def wait(t_refs, o_refs, scr):
        own, remote, arriving = copies(t_refs, o_refs, scr)
        for cp in arriving:
            cp.wait_recv()
        for cp in remote:
            cp.wait_send()
        for cp in own:
            cp.wait()

    return _Comm(
        inputs=list(tbs),
        out_shapes=[jax.ShapeDtypeStruct(t_.shape, t_.dtype) for t_ in tbs],
        scratch=[pltpu.SemaphoreType.DMA((3 * nw,)), pltpu.SemaphoreType.DMA((3 * nw,)),
                 pltpu.SemaphoreType.DMA((nw,))],
        start=start, wait=wait)


def _row_tile(rows, cap=256):
    if rows <= cap:
        return rows
    return max(t_ for t_ in range(SUBLANES, cap + 1, SUBLANES) if rows % t_ == 0)


def _add_pairs(core, part, recv, name):
    n, r, c_ = recv.shape
    tr = _row_tile(r)

    def body(core_ref, a_ref, b_ref, o_ref):
        o_ref[...] = (a_ref[...].astype(F32) + b_ref[...].astype(F32)).astype(o_ref.dtype)

    spec = pl.BlockSpec((1, tr, c_), lambda q, i, core_ref: (q, i, 0))
    return pl.pallas_call(
        body, name=name,
        grid_spec=pltpu.PrefetchScalarGridSpec(
            num_scalar_prefetch=1, grid=(n, r // tr),
            in_specs=[pl.BlockSpec((1, tr, c_), lambda q, i, core_ref: (2 * q + core_ref[0], i, 0)), spec],
            out_specs=spec),
        out_shape=jax.ShapeDtypeStruct(recv.shape, BF16), compiler_params=_params(("parallel", "parallel")),
    )(core, part, recv)


def _small_allreduce(vec, name):
    rows = vec.shape[0]

    def body(x_ref, o_ref, buf, send_sems, recv_sems):
        x, y, c = _mesh_pos()
        me = 4 * x + 2 * y + c
        buf[me] = x_ref[...]
        cps = []
        for k in range(1, N_DEV):
            peer = (1 - x if k & 4 else x, 1 - y if k & 2 else y, 1 - c if k & 1 else c)
            cps.append(pltpu.make_async_remote_copy(
                src_ref=x_ref, dst_ref=buf.at[me], send_sem=send_sems.at[k - 1], recv_sem=recv_sems.at[k - 1],
                device_id=peer, device_id_type=MESH))
        for cp in cps:
            cp.start()
        for k in range(1, N_DEV):
            px, py, pc = (1 - x if k & 4 else x, 1 - y if k & 2 else y, 1 - c if k & 1 else c)
            pltpu.make_async_remote_copy(
                src_ref=x_ref, dst_ref=buf.at[4 * px + 2 * py + pc], send_sem=send_sems.at[k - 1],
                recv_sem=recv_sems.at[k - 1], device_id=(px, py, pc), device_id_type=MESH).wait_recv()
        for cp in cps:
            cp.wait_send()
        acc = buf[0]
        for k in range(1, N_DEV):
            acc = acc + buf[k]
        o_ref[...] = acc

    vm = pl.BlockSpec(memory_space=pltpu.VMEM)
    return pl.pallas_call(
        body, name=name,
        in_specs=[vm], out_specs=vm,
        out_shape=jax.ShapeDtypeStruct(vec.shape, F32),
        scratch_shapes=[pltpu.VMEM((N_DEV, rows, LANES), F32), pltpu.SemaphoreType.DMA((N_DEV - 1,)),
                        pltpu.SemaphoreType.DMA((N_DEV - 1,))],
    )(vec)


def _adamw(gparts, w, m, v, name):
    n, r, c_ = gparts.shape
    tr = _row_tile(r)
    c1 = 1.0 / (1.0 - B1 ** STEP)
    c2 = 1.0 / (1.0 - B2 ** STEP)

    def body(g_ref, w_ref, m_ref, v_ref, go_ref, d_ref, mo_ref, vo_ref):
        g = g_ref[0].astype(F32)
        for q in range(1, n):
            g = g + g_ref[q].astype(F32)
        mn = B1 * m_ref[...] + (1.0 - B1) * g
        vn = B2 * v_ref[...] + (1.0 - B2) * (g * g)
        go_ref[...] = g
        mo_ref[...] = mn
        vo_ref[...] = vn
        d_ref[...] = -LR * ((mn * c1) / (jnp.sqrt(vn * c2) + ADAM_EPS) + WD * w_ref[...])

    spec = pl.BlockSpec((tr, c_), lambda i: (i, 0))
    out = jax.ShapeDtypeStruct((r, c_), F32)
    return pl.pallas_call(
        body, name=name, grid=(r // tr,),
        in_specs=[pl.BlockSpec((n, tr, c_), lambda i: (0, i, 0)), spec, spec, spec],
        out_specs=[spec] * 4, out_shape=[out] * 4, compiler_params=_params(("parallel",)),
    )(gparts, w, m, v)


def kernel(x, w_in, b_gates, conv_w, conv_b, dt_bias, a_log, d_skip, ssd_norm_w, w_ssd_proj, w_pool_group, pool_scale, w_out, ln1_g, ln1_b, w_up, w_down, ln2_g, ln2_b, loss_target, m_w_in, m_b_gates, m_conv_w, m_conv_b, m_dt_bias, m_a_log, m_d_skip, m_ssd_norm_w, m_w_ssd_proj, m_w_pool_group, m_pool_scale, m_w_out, m_ln1_g, m_ln1_b, m_w_up, m_w_down, m_ln2_g, m_ln2_b, v_w_in, v_b_gates, v_conv_w, v_conv_b, v_dt_bias, v_a_log, v_d_skip, v_ssd_norm_w, v_w_ssd_proj, v_w_pool_group, v_pool_scale, v_w_out, v_ln1_g, v_ln1_b, v_w_up, v_w_down, v_ln2_g, v_ln2_b):
    ws = (w_in, b_gates, conv_w, conv_b, dt_bias, a_log, d_skip, ssd_norm_w, w_ssd_proj, w_pool_group, pool_scale,
          w_out, ln1_g, ln1_b, w_up, w_down, ln2_g, ln2_b)
    ms = (m_w_in, m_b_gates, m_conv_w, m_conv_b, m_dt_bias, m_a_log, m_d_skip, m_ssd_norm_w, m_w_ssd_proj,
          m_w_pool_group, m_pool_scale, m_w_out, m_ln1_g, m_ln1_b, m_w_up, m_w_down, m_ln2_g, m_ln2_b)
    vs = (v_w_in, v_b_gates, v_conv_w, v_conv_b, v_dt_bias, v_a_log, v_d_skip, v_ssd_norm_w, v_w_ssd_proj,
          v_w_pool_group, v_pool_scale, v_w_out, v_ln1_g, v_ln1_b, v_w_up, v_w_down, v_ln2_g, v_ln2_b)
    w = {n: a[0] for n, a in zip(NAMES, ws)}
    m = {n: a[0] for n, a in zip(NAMES, ms)}
    v = {n: a[0] for n, a in zip(NAMES, vs)}
    out_shapes = {n: a.shape for n, a in zip(NAMES, ws)}
    bl, s, d = x.shape
    x2, tgt2 = x.reshape(bl * s, d), loss_target.reshape(bl * s, d)
    xi, yi, ci = _mesh_pos()
    me = 4 * xi + 2 * yi + ci
    zero = jnp.zeros((), F32)
    shapes = {n: w[n].shape for n in NAMES}
    shape2d = {n: (_size(shapes[n][:-1]), shapes[n][-1]) for n in BIG}
    cwl = shapes["conv_w"][1]

    conv_place = lax.dynamic_update_slice(jnp.zeros((CONV_K, N_DEV * cwl), F32), w["conv_w"], (0, me * cwl))
    conv_full = _small_allreduce(_rows128(conv_place), "gather_conv_w")
    conv_full = conv_full.reshape(CONV_K, N_DEV * cwl)

    shards = {n: w[n].astype(BF16).reshape(shape2d[n]) for n in BIG}
    full = {n: w[n] for n in SMALL}
    full["conv_w"] = conv_full
    full["w_in_blocks"] = _run_comm(_all_gather_comm([shards.pop("w_in")]), "all_gather_w_in")[0]
    loss8, grad_x, g, recv = _local_step(x2, tgt2, full, shards, ci.astype(jnp.int32).reshape(1), bl)

    small_sum = _small_allreduce(_pack_small(g, loss8[0, 0]), "small_allreduce")
    ex_shapes = {n: shapes[n] for n in SMALL}
    ex_shapes["conv_w"] = (CONV_K, N_DEV * cwl)
    gsum, loss = _unpack_small(small_sum, ex_shapes)
    gsum["conv_w"] = lax.dynamic_slice(gsum["conv_w"], (0, me * cwl), (CONV_K, cwl))
    gs_pk = _pack_small(gsum, zero)
    ws_pk, ms_pk, vs_pk = (_pack_small(t_, zero) for t_ in (w, m, v))
    small_out = _adamw(gs_pk[None], ws_pk, ms_pk, vs_pk, "adamw_small")
    loc_shapes = {n: shapes[n] for n in SMALL_PACK}
    res = [_unpack_small(o, loc_shapes)[0] for o in small_out]

    for n in BIG:
        outs = _adamw(recv[n], *(t_[n].reshape(shape2d[n]) for t_ in (w, m, v)), "adamw_" + n)
        for r_, o in zip(res, outs):
            r_[n] = o

    def ordered(r_):
        return [r_[n].reshape(out_shapes[n]) for n in NAMES]

    return (loss, grad_x.reshape(bl, s, d), *ordered(res[0]), *ordered(res[1]), *ordered(res[2]), *ordered(res[3]))
```

```python
import collections
import functools

import jax
import jax.numpy as jnp
from jax import lax
from jax.experimental import pallas as pl
from jax.experimental.pallas import tpu as pltpu

F32 = jnp.float32
BF16 = jnp.bfloat16
MESH = pl.DeviceIdType.MESH

HEAD_DIM = 64
STATE = 128
GROUPS = 8
CONV_K = 4
CHUNK = 256
POOL_WINDOWS = (2, 4, 8, 16)
ALPHA = 2.0 ** 0.25
LN_EPS = 1e-5
RMS_EPS = 1e-5
LR, B1, B2, ADAM_EPS, WD, STEP = 0.001, 0.9, 0.999, 1e-08, 0.01, 10
N_DEV = 8
LANES = 128
SUBLANES = 8
VMEM_LIMIT = 56 * 1024 * 1024
NEG_BIG = -1e30

NN = (((1,), (0,)), ((), ()))
NT = (((1,), (1,)), ((), ()))
TN = (((0,), (0,)), ((), ()))


def _dot(a, b, dims=NN):
    return lax.dot_general(a.astype(BF16), b.astype(BF16), dims, preferred_element_type=F32)


def _dot_exact01(q, e, dims=NN):
    hi = q.astype(BF16)
    r1 = q - hi.astype(F32)
    mid = r1.astype(BF16)
    lo = (r1 - mid.astype(F32)).astype(BF16)
    f = lambda p: lax.dot_general(p, e, dims, preferred_element_type=F32)
    return f(hi) + f(mid) + f(lo)


def _params(sem):
    return pltpu.CompilerParams(dimension_semantics=sem, vmem_limit_bytes=VMEM_LIMIT)


def _sigmoid(x):
    return 1.0 / (1.0 + jnp.exp(-x))


def _colsum(x):
    return jnp.sum(x, axis=0, keepdims=True)


def _ln_fwd(r):
    mu = jnp.mean(r, axis=-1, keepdims=True)
    xc = r - mu
    var = jnp.mean(xc * xc, axis=-1, keepdims=True)
    rstd = lax.rsqrt(var + LN_EPS)
    return xc * rstd, rstd


def _ln_bwd(dy, xhat, rstd, g):
    dxh = dy * g
    m1 = jnp.mean(dxh, axis=-1, keepdims=True)
    m2 = jnp.mean(dxh * xhat, axis=-1, keepdims=True)
    return rstd * (dxh - m1 - xhat * m2)


_Comm = collections.namedtuple("_Comm", "inputs out_shapes scratch start wait")
ANY = pl.BlockSpec(memory_space=pl.ANY)


def _fuse_comm(body, grid, n_in, n_out, comm):
    if comm is None:
        return body
    ci, co = len(comm.inputs), len(comm.out_shapes)

    def fused(*refs):
        ins, cins = refs[:n_in], refs[n_in:n_in + ci]
        o0 = n_in + ci
        outs, couts = refs[o0:o0 + n_out], refs[o0 + n_out:o0 + n_out + co]
        rest = refs[o0 + n_out + co:]
        scr, cscr = rest[:len(rest) - len(comm.scratch)], rest[len(rest) - len(comm.scratch):]
        ids = [pl.program_id(a) for a in range(len(grid))]
        first, last = ids[0] == 0, ids[0] == grid[0] - 1
        for a in range(1, len(grid)):
            first, last = first & (ids[a] == 0), last & (ids[a] == grid[a] - 1)

        @pl.when(first)
        def _():
            comm.start(cins, couts, cscr)

        body(*ins, *outs, *scr)

        @pl.when(last)
        def _():
            comm.wait(cins, couts, cscr)

    return fused


def _comm_specs(comm):
    if comm is None:
        return [], [], [], []
    return list(comm.inputs), [ANY] * len(comm.inputs), [ANY] * len(comm.out_shapes), list(comm.out_shapes)


def _run_comm(comm, name):
    ci, co = len(comm.inputs), len(comm.out_shapes)

    def body(*refs):
        comm.start(refs[:ci], refs[ci:ci + co], refs[ci + co:])
        comm.wait(refs[:ci], refs[ci:ci + co], refs[ci + co:])

    return pl.pallas_call(body, name=name, in_specs=[ANY] * ci, out_specs=[ANY] * co, out_shape=list(comm.out_shapes),
                          scratch_shapes=list(comm.scratch))(*comm.inputs)


def _matmul(a, b, mode, out_dtype, bm, bn, bk, name, a_fn=None, col_blocks=0, comm=None):
    if mode == "nn":
        (m, k), n, dims = a.shape, b.shape[1], NN
    elif mode == "nt":
        (m, k), n, dims = a.shape, b.shape[0], NT
    else:
        (k, m), n, dims = a.shape, b.shape[1], TN
    bm, bn, bk = min(bm, m), min(bn, n), min(bk, k)
    assert m % bm == 0 and n % bn == 0 and k % bk == 0, (name, m, n, k, bm, bn, bk)
    nk = k // bk
    if mode == "nn":
        a_spec = pl.BlockSpec((bm, bk), lambda i, j, kk: (i, kk))
        b_spec = pl.BlockSpec((bk, bn), lambda i, j, kk: (kk, j))
    elif mode == "nt":
        a_spec = pl.BlockSpec((bm, bk), lambda i, j, kk: (i, kk))
        b_spec = pl.BlockSpec((bn, bk), lambda i, j, kk: (j, kk))
    else:
        a_spec = pl.BlockSpec((bk, bm), lambda i, j, kk: (kk, i))
        b_spec = pl.BlockSpec((bk, bn), lambda i, j, kk: (kk, j))

    def body(a_ref, b_ref, o_ref, acc_ref):
        kk = pl.program_id(2)
        av = a_ref[...]
        if a_fn is not None:
            av = a_fn(av.astype(F32))
        prod = _dot(av, b_ref[...], dims)

        def emit(total):
            if col_blocks:
                for s in range(bn // slab):
                    o_ref[s] = total[:, s * slab:(s + 1) * slab].astype(o_ref.dtype)
            else:
                o_ref[...] = total.astype(o_ref.dtype)

        if nk == 1:
            emit(prod)
        else:
            @pl.when(kk == 0)
            def _():
                acc_ref[...] = prod

            @pl.when((kk > 0) & (kk < nk - 1))
            def _():
                acc_ref[...] += prod

            @pl.when(kk == nk - 1)
            def _():
                emit(acc_ref[...] + prod)

    if col_blocks:
        slab = n // col_blocks
        assert n % col_blocks == 0 and bn % slab == 0, (name, n, col_blocks, bn)
        out_spec = pl.BlockSpec((bn // slab, bm, slab), lambda i, j, kk: (j, i, 0))
        out_shape = jax.ShapeDtypeStruct((col_blocks, m, slab), out_dtype)
    else:
        out_spec = pl.BlockSpec((bm, bn), lambda i, j, kk: (i, j))
        out_shape = jax.ShapeDtypeStruct((m, n), out_dtype)
    grid = (m // bm, n // bn, nk)
    c_in, c_in_specs, c_out_specs, c_out_shapes = _comm_specs(comm)
    res = pl.pallas_call(
        _fuse_comm(body, grid, 2, 1, comm), name=name,
        grid=grid,
        in_specs=[a_spec, b_spec] + c_in_specs,
        out_specs=[out_spec] + c_out_specs,
        out_shape=[out_shape] + c_out_shapes,
        scratch_shapes=[pltpu.VMEM((bm, bn), F32)] + (list(comm.scratch) if comm else []),
        compiler_params=_params(("arbitrary",) * 3 if comm else ("parallel", "parallel", "arbitrary")),
    )(a, b, *c_in)
    return res if comm else res[0]


CONV_STRIP = 16
CONV_COLS = 512


def _conv_pre(ext_ref, w_ref, b_ref, r0, rows, cols=slice(None)):
    acc = b_ref[:, cols] + w_ref[0:1, cols] * ext_ref[pl.ds(r0 + SUBLANES - (CONV_K - 1), rows), cols]
    for k in range(1, CONV_K):
        acc = acc + w_ref[k:k + 1, cols] * ext_ref[pl.ds(r0 + SUBLANES - (CONV_K - 1) + k, rows), cols]
    return acc


def _in_proj(xb, w_main, conv_w8, conv_b, cd, seq_len, bm, bn, comm):
    t, d = xb.shape
    pw = w_main.shape[1]
    bm, bn = min(bm, seq_len), min(bn, d)
    assert t % bm == 0 and seq_len % bm == 0 and pw % bn == 0 and cd % bn == 0
    ncj = cd // bn
    tiles_per_seq = seq_len // bm

    def body(x_ref, w_ref, cw_ref, cb_ref, p_ref, xbc_ref, dsl_ref, ext_ref, carry_ref):
        i = pl.program_id(0)
        j = pl.program_id(1)

        def conv_previous():
            ext = ext_ref.at[(j + 1) % 2]
            cw = min(bn, CONV_COLS)
            for c0 in range(0, bn, cw):
                cols = slice(c0, c0 + cw)
                for r0 in range(0, bm, CONV_STRIP):
                    rows = slice(r0, r0 + CONV_STRIP)
                    acc = _conv_pre(ext, cw_ref, cb_ref, r0, CONV_STRIP, cols)
                    sg = _sigmoid(acc)
                    xbc_ref[rows, cols] = (acc * sg).astype(xbc_ref.dtype)
                    dsl_ref[rows, cols] = (sg * (1.0 + acc * (1.0 - sg))).astype(dsl_ref.dtype)

        def project(stash):
            pq = _dot(x_ref[...], w_ref[...]).astype(BF16)
            p_ref[...] = pq
            if stash:
                ext = ext_ref.at[j % 2]
                jc = jnp.minimum(j, ncj - 1)
                ext[0:SUBLANES, :] = jnp.where((i % tiles_per_seq) == 0, 0.0, carry_ref[jc])
                ext[SUBLANES:, :] = pq.astype(F32)
                carry_ref[jc] = ext[bm:bm + SUBLANES, :]

        @pl.when(j == 0)
        def _():
            project(True)

        @pl.when((j >= 1) & (j < ncj))
        def _():
            conv_previous()
            project(True)

        @pl.when(j == ncj)
        def _():
            conv_previous()
            project(False)

        @pl.when(j > ncj)
        def _():
            project(False)

    assert pw // bn > ncj
    grid = (t // bm, pw // bn)
    conv_col = lambda i, j: (0, jnp.clip(j - 1, 0, ncj - 1))
    c_in, c_in_specs, c_out_specs, c_out_shapes = _comm_specs(comm)
    conv_tile = pl.BlockSpec((bm, bn), lambda i, j: (i, jnp.clip(j - 1, 0, ncj - 1)))
    conv_out = jax.ShapeDtypeStruct((t, cd), BF16)
    return pl.pallas_call(
        _fuse_comm(body, grid, 4, 3, comm), name="in_proj",
        grid=grid,
        in_specs=[pl.BlockSpec((bm, d), lambda i, j: (i, 0)), pl.BlockSpec((d, bn), lambda i, j: (0, j)),
                  pl.BlockSpec((SUBLANES, bn), conv_col), pl.BlockSpec((1, bn), conv_col)] + c_in_specs,
        out_specs=[pl.BlockSpec((bm, bn), lambda i, j: (i, j)), conv_tile, conv_tile] + c_out_specs,
        out_shape=[jax.ShapeDtypeStruct((t, pw), BF16), conv_out, conv_out] + c_out_shapes,
        scratch_shapes=[pltpu.VMEM((2, bm + SUBLANES, bn), F32), pltpu.VMEM((ncj, SUBLANES, bn), F32)]
        + (list(comm.scratch) if comm else []),
        compiler_params=_params(("arbitrary", "arbitrary")),
    )(xb, w_main, conv_w8, conv_b, *c_in)


def _conv_bwd(proj, dsilu, dxbc, conv_w8, n_seq_chunks, col0, width, ct, L, name):
    t = proj.shape[0]
    nbc = t // L
    hb = L // SUBLANES
    ct = min(ct, width)
    assert col0 % ct == 0 and width % ct == 0
    cb0 = col0 // ct
    last_hb = t // SUBLANES - 1

    def body(x_ref, xb_ref, s_ref, sa_ref, d_ref, da_ref, w_ref, o_ref, dw_ref, db_ref, ext_ref, dc_ref):
        bc = pl.program_id(1)
        first = (bc % n_seq_chunks) == 0
        last = (bc % n_seq_chunks) == n_seq_chunks - 1

        @pl.when(bc == 0)
        def _():
            dw_ref[...] = jnp.zeros_like(dw_ref)
            db_ref[...] = jnp.zeros_like(db_ref)

        ext_ref[0:SUBLANES, :] = jnp.where(first, 0.0, xb_ref[...].astype(F32))
        ext_ref[SUBLANES:, :] = x_ref[...].astype(F32)
        for r0 in range(0, L, CONV_STRIP):
            rows = slice(r0, r0 + CONV_STRIP)
            dc_ref[rows, :] = d_ref[rows, :].astype(F32) * s_ref[rows, :].astype(F32)
        dc_ref[L:, :] = jnp.where(last, 0.0, da_ref[...].astype(F32)) * sa_ref[...].astype(F32)
        fold = lambda v: v[0:SUBLANES] + v[SUBLANES:CONV_STRIP]
        dws = [jnp.zeros((SUBLANES, ct), F32) for _ in range(CONV_K)]
        dbs = jnp.zeros((SUBLANES, ct), F32)
        for r0 in range(0, L, CONV_STRIP):
            dc = dc_ref[r0:r0 + CONV_STRIP, :]
            dx = w_ref[CONV_K - 1:CONV_K, :] * dc
            for k in range(CONV_K - 1):
                dx = dx + w_ref[k:k + 1, :] * dc_ref[pl.ds(r0 + CONV_K - 1 - k, CONV_STRIP), :]
            o_ref[r0:r0 + CONV_STRIP, :] = dx.astype(o_ref.dtype)
            for k in range(CONV_K):
                dws[k] = dws[k] + fold(dc * ext_ref[pl.ds(r0 + SUBLANES - (CONV_K - 1) + k, CONV_STRIP), :])
            dbs = dbs + fold(dc)
        for k in range(CONV_K):
            dw_ref[k:k + 1, :] += _colsum(dws[k])
        db_ref[0:1, :] += _colsum(dbs)

    return pl.pallas_call(
        body, name=name,
        grid=(width // ct, nbc),
        in_specs=[
            pl.BlockSpec((L, ct), lambda j, i: (i, cb0 + j)),
            pl.BlockSpec((SUBLANES, ct), lambda j, i: (jnp.maximum(i * hb - 1, 0), cb0 + j)),
            pl.BlockSpec((L, ct), lambda j, i: (i, cb0 + j)),
            pl.BlockSpec((SUBLANES, ct), lambda j, i: (jnp.minimum((i + 1) * hb, last_hb), cb0 + j)),
            pl.BlockSpec((L, ct), lambda j, i: (i, j)),
            pl.BlockSpec((SUBLANES, ct), lambda j, i: (jnp.minimum((i + 1) * hb, last_hb), j)),
            pl.BlockSpec((SUBLANES, ct), lambda j, i: (0, cb0 + j)),
        ],
        out_specs=[
            pl.BlockSpec((L, ct), lambda j, i: (i, j)),
            pl.BlockSpec((SUBLANES, ct), lambda j, i: (0, j)),
            pl.BlockSpec((SUBLANES, ct), lambda j, i: (0, j)),
        ],
        out_shape=[
            jax.ShapeDtypeStruct((t, width), BF16),
            jax.ShapeDtypeStruct((SUBLANES, width), F32),
            jax.ShapeDtypeStruct((SUBLANES, width), F32),
        ],
        scratch_shapes=[pltpu.VMEM((L + SUBLANES, ct), F32), pltpu.VMEM((L + SUBLANES, ct), F32)],
        compiler_params=_params(("parallel", "arbitrary")),
    )(proj, proj, dsilu, dsilu, dxbc, dxbc, conv_w8)


def _cumsum_rows(x, reverse=False):
    n = x.shape[0]
    row = lax.broadcasted_iota(jnp.int32, x.shape, 0)
    s = 1
    while s < n:
        if reverse:
            x = x + jnp.where(row < n - s, pltpu.roll(x, n - s, 0), 0.0)
        else:
            x = x + jnp.where(row >= s, pltpu.roll(x, s, 0), 0.0)
        s *= 2
    return x


def _ssd_scalars(dtr, dtb, alog):
    pre = dtr + dtb
    dt = jnp.maximum(pre, 0.0) + jnp.log(1.0 + jnp.exp(-jnp.abs(pre)))
    a = -jnp.exp(alog)
    acs = _cumsum_rows(dt * a) * LOG2E
    n = acs.shape[0]
    return pre, dt, a, acs, jnp.exp2(acs), jnp.exp2(acs[n - 1:n, :] - acs)


LOG2E = 1.4426950408889634


def _dot_2piece(q, e):
    hi = q.astype(BF16)
    mid = (q - hi.astype(F32)).astype(BF16)
    return lax.dot_general(jnp.concatenate([hi, mid], axis=1), jnp.concatenate([e, e], axis=0), NN,
                           preferred_element_type=F32)


def _ssd_group_common(dt_s, e_s, dec_s, e):
    return _dot_2piece(dt_s, e), _dot_2piece(e_s, e), _dot_2piece(dec_s, e)


def _decay_matrix(acs, acs_t, h, tri):
    return jnp.exp2(jnp.where(tri, acs[:, h:h + 1] - acs_t[h:h + 1, :], NEG_BIG))


def _head_mask(r, gw, dtype):
    lane = lax.broadcasted_iota(jnp.int32, (1, gw), 1)
    return ((lane >= r * HEAD_DIM) & (lane < (r + 1) * HEAD_DIM)).astype(dtype)


def _ssd_fwd(xbc, proj, dt_raw, dtb, alog, dskip_x, normw, emat, bl, inner, z_col0):
    t = xbc.shape[0]
    L = CHUNK
    nc = t // bl // L
    G = GROUPS
    gw = inner // G
    hpg = gw // HEAD_DIM
    assert z_col0 % gw == 0
    zb0 = z_col0 // gw
    bb0 = inner // STATE
    cb0 = bb0 + G

    P = G
    assert bb0 % P == 0 and cb0 % P == 0 and zb0 % P == 0

    def body(xs_ref, b_ref, c_ref, z_ref, dtr_ref, dtb_ref, alog_ref, dsk_ref, nw_ref, e_ref,
             y_ref, yn_ref, st_ref, h_ref):
        c = pl.program_id(1)
        _, dt_s, _, acs, e_s, dec_s = _ssd_scalars(dtr_ref[...], dtb_ref[...], alog_ref[...])
        acs_t = acs.T
        tri = lax.broadcasted_iota(jnp.int32, (L, L), 0) >= lax.broadcasted_iota(jnp.int32, (L, L), 1)
        lane = lax.broadcasted_iota(jnp.int32, (L, gw), 1)
        for g in range(G):
            cols = slice(g * gw, (g + 1) * gw)
            ncol = slice(g * STATE, (g + 1) * STATE)

            @pl.when(c == 0)
            def _():
                h_ref[g] = jnp.zeros((STATE, gw), F32)

            xs = xs_ref[:, cols].astype(F32)
            bg = b_ref[:, ncol]
            cg = c_ref[:, ncol]
            dt_x, e_x, dec_x = _ssd_group_common(dt_s, e_s, dec_s, e_ref[:, cols])
            xdt = xs * dt_x
            cb = _dot(cg, bg, NT)
            h = h_ref[g]
            st_ref[0, g] = h
            y = _dot(cg, h) * e_x + dsk_ref[:, cols] * xs
            for r in range(hpg):
                m = cb * _decay_matrix(acs, acs_t, g * hpg + r, tri)
                xr = jnp.where((lane >= r * HEAD_DIM) & (lane < (r + 1) * HEAD_DIM), xdt, 0.0)
                y = y + _dot(m, xr)
            h_ref[g] = h * e_x[L - 1:L, :] + _dot(bg, xdt * dec_x, TN)
            yq = y.astype(y_ref.dtype)
            y_ref[:, cols] = yq
            z = z_ref[:, cols].astype(F32)
            yg = yq.astype(F32) * (z * _sigmoid(z))
            rs = lax.rsqrt(jnp.mean(yg * yg, axis=-1, keepdims=True) + RMS_EPS)
            yn_ref[:, cols] = (yg * rs * nw_ref[:, cols]).astype(yn_ref.dtype)

    return pl.pallas_call(
        body, name="ssd_fwd",
        grid=(bl, nc, G // P),
        in_specs=[
            pl.BlockSpec((L, P * gw), lambda b, c, g: (b * nc + c, g)),
            pl.BlockSpec((L, P * STATE), lambda b, c, g: (b * nc + c, bb0 // P + g)),
            pl.BlockSpec((L, P * STATE), lambda b, c, g: (b * nc + c, cb0 // P + g)),
            pl.BlockSpec((L, P * gw), lambda b, c, g: (b * nc + c, zb0 // P + g)),
            pl.BlockSpec((L, LANES), lambda b, c, g: (b * nc + c, 0)),
            pl.BlockSpec((1, LANES), lambda b, c, g: (0, 0)),
            pl.BlockSpec((1, LANES), lambda b, c, g: (0, 0)),
            pl.BlockSpec((1, P * gw), lambda b, c, g: (0, g)),
            pl.BlockSpec((1, P * gw), lambda b, c, g: (0, g)),
            pl.BlockSpec((LANES, P * gw), lambda b, c, g: (0, g)),
        ],
        out_specs=[
            pl.BlockSpec((L, P * gw), lambda b, c, g: (b * nc + c, g)),
            pl.BlockSpec((L, P * gw), lambda b, c, g: (b * nc + c, g)),
            pl.BlockSpec((1, P, STATE, gw), lambda b, c, g: (b * nc + c, g, 0, 0)),
        ],
        out_shape=[
            jax.ShapeDtypeStruct((t, inner), BF16),
            jax.ShapeDtypeStruct((t, inner), BF16),
            jax.ShapeDtypeStruct((bl * nc, G, STATE, gw), F32),
        ],
        scratch_shapes=[pltpu.VMEM((G, STATE, gw), F32)],
        compiler_params=_params(("arbitrary", "arbitrary", "arbitrary")),
    )(xbc, xbc, xbc, proj, dt_raw, dtb, alog, dskip_x, normw, emat)


def _ssd_bwd(xbc, proj, dt_raw, y, dyn, states, dtb, alog, dskip_x, normw, emat, emat_t, bl, inner, z_col0,
             comm=None):
    t = xbc.shape[0]
    L = CHUNK
    nc = t // bl // L
    G = GROUPS
    gw = inner // G
    hpg = gw // HEAD_DIM
    zb0 = z_col0 // gw
    bb0 = inner // STATE
    cb0 = bb0 + G
    P = G

    def rc(j):
        return nc - 1 - j

    def body(xs_ref, b_ref, c_ref, z_ref, dtr_ref, y_ref, dyn_ref, st_ref, dtb_ref, alog_ref, dsk_ref,
             nw_ref, e_ref, et_ref,
             dxs_ref, db_ref, dc_ref, dz_ref, ddt_ref, dnw_ref, dsk_acc, dalog_acc, ddtb_acc,
             dh_ref):
        b = pl.program_id(0)
        j = pl.program_id(1)

        @pl.when((b == 0) & (j == 0))
        def _():
            dsk_acc[...] = jnp.zeros_like(dsk_acc)
            dalog_acc[...] = jnp.zeros_like(dalog_acc)
            ddtb_acc[...] = jnp.zeros_like(ddtb_acc)

        pre, dt_s, a_row, acs, e_s, dec_s = _ssd_scalars(dtr_ref[...], dtb_ref[...], alog_ref[...])
        acs_t = acs.T
        wacs = jnp.zeros((L, LANES), F32)
        wdt = jnp.zeros((L, LANES), F32)
        tri = lax.broadcasted_iota(jnp.int32, (L, L), 0) >= lax.broadcasted_iota(jnp.int32, (L, L), 1)
        rowi = lax.broadcasted_iota(jnp.int32, (L, gw), 0)
        for g in range(G):
            cols = slice(g * gw, (g + 1) * gw)
            ncol = slice(g * STATE, (g + 1) * STATE)

            @pl.when((b == 0) & (j == 0))
            def _():
                dnw_ref[g] = jnp.zeros((SUBLANES, gw), F32)

            @pl.when(j == 0)
            def _():
                dh_ref[g] = jnp.zeros((STATE, gw), F32)

            xs = xs_ref[:, cols].astype(F32)
            bg = b_ref[:, ncol]
            cg = c_ref[:, ncol]
            dt_x, e_x, dec_x = _ssd_group_common(dt_s, e_s, dec_s, e_ref[:, cols])
            xdt = xs * dt_x
            xdt_b = xdt.astype(BF16)
            cb = _dot(cg, bg, NT)
            h = st_ref[0, g]
            hb16 = h.astype(BF16)
            dsk = dsk_ref[:, cols]

            yv = y_ref[:, cols].astype(F32)
            z = z_ref[:, cols].astype(F32)
            sgz = _sigmoid(z)
            sz = z * sgz
            yg = yv * sz
            rs = lax.rsqrt(jnp.mean(yg * yg, axis=-1, keepdims=True) + RMS_EPS)
            yhat = yg * rs
            dyn_v = dyn_ref[:, cols].astype(F32)
            dnw_ref[g] += _colsum(dyn_v * yhat)
            dyh = dyn_v * nw_ref[:, cols]
            dyg = rs * (dyh - yhat * jnp.mean(dyh * yhat, axis=-1, keepdims=True))
            dy = dyg * sz
            dz_ref[:, cols] = (dyg * yv * (sgz * (1.0 + z * (1.0 - sgz)))).astype(dz_ref.dtype)

            dy_b = dy.astype(BF16)
            dcb = jnp.zeros((L, L), F32)
            dxdt_d = jnp.zeros((L, gw), F32)
            ydiag = jnp.zeros((L, gw), F32)
            for r in range(hpg):
                lm = _decay_matrix(acs, acs_t, g * hpg + r, tri)
                m = (cb * lm).astype(BF16)
                hm = _head_mask(r, gw, BF16)
                dyr = dy_b * hm
                xr = xdt_b * hm
                ydiag = ydiag + _dot(m, xr)
                dcb = dcb + _dot(dyr, xdt_b, NT) * lm
                dxdt_d = dxdt_d + _dot(m, dyr, TN)
            dh = dh_ref[g]
            dh16 = dh.astype(BF16)
            xdec_b = (xdt * dec_x).astype(BF16)
            bdh = _dot(bg, dh16)
            dxdt = dxdt_d + dec_x * bdh
            dcb16 = dcb.astype(BF16)
            dye = (dy * e_x).astype(BF16)
            db_ref[:, ncol] = (_dot(dcb16, cg, TN) + _dot(xdec_b, dh16, NT)).astype(db_ref.dtype)
            dc_ref[:, ncol] = (_dot(dcb16, bg) + _dot(dye, hb16, NT)).astype(dc_ref.dtype)
            dprev = _dot(cg, dye, TN)
            cd_row = e_x[L - 1:L, :]
            s_new = _dot(bg, xdec_b, TN)
            last_term = _colsum(dh16.astype(F32) * s_new) + _colsum(dh * h) * cd_row
            yoff = _dot(cg, hb16) * e_x
            wfold = (dy_b.astype(F32) * ydiag + dy * yoff - dxdt_d * xdt_b.astype(F32) - bdh * xdec_b.astype(F32)
                     + jnp.where(rowi == L - 1, last_term, 0.0))
            et = et_ref[cols, :]
            wacs = wacs + _dot_2piece(wfold, et)
            wdt = wdt + _dot_2piece(dxdt * xs, et)
            dsk_acc[...] += _dot_exact01(jnp.broadcast_to(_colsum(dy * xs), (SUBLANES, gw)), et)
            dxs_ref[:, cols] = (dsk * dy + dxdt * dt_x).astype(dxs_ref.dtype)
            dh_ref[g] = dprev + cd_row * dh

        dda = _cumsum_rows(wacs, reverse=True)
        ddt_raw = (wdt + dda * a_row) * _sigmoid(pre)
        ddt_ref[...] = ddt_raw
        dalog_acc[...] += _colsum(dda * dt_s) * a_row
        ddtb_acc[...] += _colsum(ddt_raw)

    def cidx(b, j):
        return b * nc + rc(j)

    accs = lambda shape: pl.BlockSpec(shape, lambda b, j, g: tuple(0 for _ in shape))
    grid = (bl, nc, G // P)
    c_in, c_in_specs, c_out_specs, c_out_shapes = _comm_specs(comm)
    return pl.pallas_call(
        _fuse_comm(body, grid, 14, 9, comm), name="ssd_bwd",
        grid=grid,
        in_specs=[
            pl.BlockSpec((L, P * gw), lambda b, j, g: (cidx(b, j), g)),
            pl.BlockSpec((L, P * STATE), lambda b, j, g: (cidx(b, j), bb0 // P + g)),
            pl.BlockSpec((L, P * STATE), lambda b, j, g: (cidx(b, j), cb0 // P + g)),
            pl.BlockSpec((L, P * gw), lambda b, j, g: (cidx(b, j), zb0 // P + g)),
            pl.BlockSpec((L, LANES), lambda b, j, g: (cidx(b, j), 0)),
            pl.BlockSpec((L, P * gw), lambda b, j, g: (cidx(b, j), g)),
            pl.BlockSpec((L, P * gw), lambda b, j, g: (cidx(b, j), g)),
            pl.BlockSpec((1, P, STATE, gw), lambda b, j, g: (cidx(b, j), g, 0, 0)),
            pl.BlockSpec((1, LANES), lambda b, j, g: (0, 0)),
            pl.BlockSpec((1, LANES), lambda b, j, g: (0, 0)),
            pl.BlockSpec((1, P * gw), lambda b, j, g: (0, g)),
            pl.BlockSpec((1, P * gw), lambda b, j, g: (0, g)),
            pl.BlockSpec((LANES, P * gw), lambda b, j, g: (0, g)),
            pl.BlockSpec((P * gw, LANES), lambda b, j, g: (g, 0)),
        ] + c_in_specs,
        out_specs=[
            pl.BlockSpec((L, P * gw), lambda b, j, g: (cidx(b, j), g)),
            pl.BlockSpec((L, P * STATE), lambda b, j, g: (cidx(b, j), g)),
            pl.BlockSpec((L, P * STATE), lambda b, j, g: (cidx(b, j), g)),
            pl.BlockSpec((L, P * gw), lambda b, j, g: (cidx(b, j), g)),
            pl.BlockSpec((L, LANES), lambda b, j, g: (cidx(b, j), 0)),
            accs((G, SUBLANES, gw)),
            accs((SUBLANES, LANES)),
            accs((SUBLANES, LANES)),
            accs((SUBLANES, LANES)),
        ] + c_out_specs,
        out_shape=[
            jax.ShapeDtypeStruct((t, inner), BF16),
            jax.ShapeDtypeStruct((t, G * STATE), BF16),
            jax.ShapeDtypeStruct((t, G * STATE), BF16),
            jax.ShapeDtypeStruct((t, inner), BF16),
            jax.ShapeDtypeStruct((t, LANES), F32),
            jax.ShapeDtypeStruct((G, SUBLANES, gw), F32),
            jax.ShapeDtypeStruct((SUBLANES, LANES), F32),
            jax.ShapeDtypeStruct((SUBLANES, LANES), F32),
            jax.ShapeDtypeStruct((SUBLANES, LANES), F32),
        ] + c_out_shapes,
        scratch_shapes=[pltpu.VMEM((G, STATE, gw), F32)]
        + (list(comm.scratch) if comm else []),
        compiler_params=_params(("arbitrary", "arbitrary", "arbitrary")),
    )(xbc, xbc, xbc, proj, dt_raw, y, dyn, states, dtb, alog, dskip_x, normw, emat, emat_t, *c_in)


def _pool_window(u, w, anti):
    n = u.shape[0]
    row = lax.broadcasted_iota(jnp.int32, u.shape, 0)
    acc = u
    s = 1
    while s < w:
        if anti:
            acc = acc + jnp.where(row < n - s, pltpu.roll(acc, n - s, 0), 0.0)
        else:
            acc = acc + jnp.where(row >= s, pltpu.roll(acc, s, 0), 0.0)
        s *= 2
    return acc


def _pool_cnt(shape, w):
    row = lax.broadcasted_iota(jnp.int32, shape, 0)
    return jnp.minimum(row + 1, w).astype(F32)


def _pool_fwd(proj, wpg, bl, d, u_col0):
    t = proj.shape[0]
    s = t // bl
    pg = len(POOL_WINDOWS)
    cg = d // pg
    ub0 = u_col0 // d

    def body(u_ref, w_ref, o_ref):
        for gi, w in enumerate(POOL_WINDOWS):
            u = u_ref[:, gi * cg:(gi + 1) * cg].astype(F32)
            pooled = _pool_window(u, w, False) / _pool_cnt(u.shape, w) - u
            o_ref[:, gi * cg:(gi + 1) * cg] = _dot(pooled, w_ref[gi]).astype(o_ref.dtype)

    return pl.pallas_call(
        body, name="pool_fwd",
        grid=(bl,),
        in_specs=[pl.BlockSpec((s, d), lambda b: (b, ub0)), pl.BlockSpec((pg, cg, cg), lambda b: (0, 0, 0))],
        out_specs=pl.BlockSpec((s, d), lambda b: (b, 0)),
        out_shape=jax.ShapeDtypeStruct((t, d), BF16),
        compiler_params=_params(("parallel",)),
    )(proj, wpg)


def _pool_bwd(proj, dyp, wpg, bl, d, u_col0):
    t = proj.shape[0]
    s = t // bl
    pg = len(POOL_WINDOWS)
    cg = d // pg
    ub0 = u_col0 // d

    def body(u_ref, dy_ref, w_ref, du_ref, dw_ref):
        @pl.when(pl.program_id(0) == 0)
        def _():
            dw_ref[...] = jnp.zeros_like(dw_ref)

        for gi, w in enumerate(POOL_WINDOWS):
            u = u_ref[:, gi * cg:(gi + 1) * cg].astype(F32)
            cnt = _pool_cnt(u.shape, w)
            pooled = _pool_window(u, w, False) / cnt - u
            dy = dy_ref[:, gi * cg:(gi + 1) * cg]
            dw_ref[gi] += _dot(pooled, dy, TN)
            dp = _dot(dy, w_ref[gi], NT)
            du_ref[:, gi * cg:(gi + 1) * cg] = (_pool_window(dp / cnt, w, True) - dp).astype(du_ref.dtype)

    return pl.pallas_call(
        body, name="pool_bwd",
        grid=(bl,),
        in_specs=[pl.BlockSpec((s, d), lambda b: (b, ub0)), pl.BlockSpec((s, d), lambda b: (b, 0)),
                  pl.BlockSpec((pg, cg, cg), lambda b: (0, 0, 0))],
        out_specs=[pl.BlockSpec((s, d), lambda b: (b, 0)), pl.BlockSpec((pg, cg, cg), lambda b: (0, 0, 0))],
        out_shape=[jax.ShapeDtypeStruct((t, d), BF16), jax.ShapeDtypeStruct((pg, cg, cg), F32)],
        compiler_params=_params(("arbitrary",)),
    )(proj, dyp, wpg)


def _merge_fwd(proj, ypr, yssd, x, w_out, b_gates, pool_scale, d, lg_col0, tm):
    t = x.shape[0]
    lb0 = lg_col0 // (2 * d)

    def body(lg_ref, yp_ref, ys_ref, x_ref, w_ref, bg_ref, ps_ref, mg_ref, r1_ref):
        lg = lg_ref[...].astype(F32) + bg_ref[...]
        ga = _sigmoid(lg[:, :d])
        gb = _sigmoid(lg[:, d:])
        merged = ga * (yp_ref[...].astype(F32) * ps_ref[...]) + gb * ys_ref[...].astype(F32)
        mg_ref[...] = merged.astype(mg_ref.dtype)
        r1_ref[...] = ALPHA * x_ref[...] + _dot(mg_ref[...], w_ref[...])

    row = lambda w: pl.BlockSpec((tm, w), lambda i: (i, 0))
    full = lambda a: pl.BlockSpec(a.shape, lambda i: (0, 0))
    return pl.pallas_call(
        body, name="merge_fwd",
        grid=(t // tm,),
        in_specs=[pl.BlockSpec((tm, 2 * d), lambda i: (i, lb0)), row(d), row(d), row(d), full(w_out), full(b_gates),
                  full(pool_scale)],
        out_specs=[row(d), row(d)],
        out_shape=[jax.ShapeDtypeStruct((t, d), BF16), jax.ShapeDtypeStruct((t, d), F32)],
        compiler_params=_params(("parallel",)),
    )(proj, ypr, yssd, x, w_out, b_gates, pool_scale)


def _merge_bwd(dr1, proj, ypr, yssd, w_out, b_gates, pool_scale, d, lg_col0, tm):
    t = dr1.shape[0]
    lb0 = lg_col0 // (2 * d)

    def body(dr_ref, lg_ref, yp_ref, ys_ref, w_ref, bg_ref, ps_ref, dlg_ref, dyp_ref, dys_ref, dbg_ref, dps_ref):
        @pl.when(pl.program_id(0) == 0)
        def _():
            dbg_ref[...] = jnp.zeros_like(dbg_ref)
            dps_ref[...] = jnp.zeros_like(dps_ref)

        dm = _dot(dr_ref[...], w_ref[...], NT)
        lg = lg_ref[...].astype(F32) + bg_ref[...]
        ga = _sigmoid(lg[:, :d])
        gb = _sigmoid(lg[:, d:])
        ypr_v = yp_ref[...].astype(F32)
        ys_v = ys_ref[...].astype(F32)
        ps = ps_ref[...]
        dga = dm * ypr_v * ps
        dla = dga * ga * (1.0 - ga)
        dlb = dm * ys_v * gb * (1.0 - gb)
        dlg_ref[:, :d] = dla.astype(dlg_ref.dtype)
        dlg_ref[:, d:] = dlb.astype(dlg_ref.dtype)
        dyp_ref[...] = (dm * ga * ps).astype(dyp_ref.dtype)
        dys_ref[...] = (dm * gb).astype(dys_ref.dtype)
        dbg_ref[0:1, :d] += _colsum(dla)
        dbg_ref[0:1, d:] += _colsum(dlb)
        dps_ref[0:1, :] += _colsum(dm * ga * ypr_v)

    row = lambda w: pl.BlockSpec((tm, w), lambda i: (i, 0))
    full = lambda a: pl.BlockSpec(a.shape, lambda i: (0, 0))
    acc = lambda w: pl.BlockSpec((SUBLANES, w), lambda i: (0, 0))
    return pl.pallas_call(
        body, name="merge_bwd",
        grid=(t // tm,),
        in_specs=[row(d), pl.BlockSpec((tm, 2 * d), lambda i: (i, lb0)), row(d), row(d), full(w_out), full(b_gates),
                  full(pool_scale)],
        out_specs=[row(2 * d), row(d), row(d), acc(2 * d), acc(d)],
        out_shape=[jax.ShapeDtypeStruct((t, 2 * d), BF16), jax.ShapeDtypeStruct((t, d), BF16),
                   jax.ShapeDtypeStruct((t, d), BF16), jax.ShapeDtypeStruct((SUBLANES, 2 * d), F32),
                   jax.ShapeDtypeStruct((SUBLANES, d), F32)],
        compiler_params=_params(("arbitrary",)),
    )(dr1, proj, ypr, yssd, w_out, b_gates, pool_scale)


MLP_SLABS_PER_STEP = 2


def _mlp_fwd(r1, target, w_up, w_down, ln1_g, ln1_b, ln2_g, ln2_b, tm):
    t, d = r1.shape
    ns, _, sw = w_up.shape
    spb = MLP_SLABS_PER_STEP
    assert ns % spb == 0
    nf, tf, ff = ns // spb, spb * sw, ns * sw

    def body(r1_ref, tg_ref, wu_ref, wd_ref, g1_ref, b1_ref, g2_ref, b2_ref,
             up_ref, h1_ref, dr2_ref, loss_ref, dg2_ref, db2_ref, h1f, acc):
        i = pl.program_id(0)
        f = pl.program_id(1)

        @pl.when((i == 0) & (f == 0))
        def _():
            loss_ref[...] = jnp.zeros_like(loss_ref)
            dg2_ref[...] = jnp.zeros_like(dg2_ref)
            db2_ref[...] = jnp.zeros_like(db2_ref)

        @pl.when(f == 0)
        def _():
            xhat, _ = _ln_fwd(r1_ref[...])
            h1 = xhat * g1_ref[...] + b1_ref[...]
            h1f[...] = h1
            h1_ref[...] = h1.astype(h1_ref.dtype)
            acc[...] = jnp.zeros_like(acc)

        for s in range(spb):
            up_ref[:, s * sw:(s + 1) * sw] = _dot(h1_ref[...], wu_ref[s]).astype(up_ref.dtype)
        upq = jnp.maximum(up_ref[...].astype(F32), 0.0)
        acc[...] += _dot(upq * upq, wd_ref[...])

        @pl.when(f == nf - 1)
        def _():
            xhat, rstd = _ln_fwd(ALPHA * h1f[...] + acc[...])
            g2 = g2_ref[...]
            diff = xhat * g2 + b2_ref[...] - tg_ref[...]
            loss_ref[...] += 0.5 / d * jnp.sum(diff * diff)
            dh2 = diff * (1.0 / d)
            dg2_ref[0:1, :] += _colsum(dh2 * xhat)
            db2_ref[0:1, :] += _colsum(dh2)
            dr2_ref[...] = _ln_bwd(dh2, xhat, rstd, g2).astype(dr2_ref.dtype)

    row = pl.BlockSpec((tm, d), lambda i, f: (i, 0))
    vec = pl.BlockSpec((1, d), lambda i, f: (0, 0))
    acc8 = pl.BlockSpec((SUBLANES, d), lambda i, f: (0, 0))
    return pl.pallas_call(
        body, name="mlp_fwd",
        grid=(t // tm, nf),
        in_specs=[row, row, pl.BlockSpec((spb, d, sw), lambda i, f: (f, 0, 0)), pl.BlockSpec((tf, d), lambda i, f: (f, 0)),
                  vec, vec, vec, vec],
        out_specs=[pl.BlockSpec((tm, tf), lambda i, f: (i, f)), row, row,
                   pl.BlockSpec((SUBLANES, LANES), lambda i, f: (0, 0)), acc8, acc8],
        out_shape=[jax.ShapeDtypeStruct((t, ff), BF16), jax.ShapeDtypeStruct((t, d), BF16),
                   jax.ShapeDtypeStruct((t, d), BF16), jax.ShapeDtypeStruct((SUBLANES, LANES), F32),
                   jax.ShapeDtypeStruct((SUBLANES, d), F32), jax.ShapeDtypeStruct((SUBLANES, d), F32)],
        scratch_shapes=[pltpu.VMEM((tm, d), F32), pltpu.VMEM((tm, d), F32)],
        compiler_params=_params(("arbitrary", "arbitrary")),
    )(r1, target, w_up, w_down, ln1_g, ln1_b, ln2_g, ln2_b)


def _mlp_bwd(dr2, up, r1, w_up, w_down, ln1_g, tm):
    t, d = r1.shape
    ns, _, sw = w_up.shape
    spb = MLP_SLABS_PER_STEP
    assert ns % spb == 0
    nf, tf, ff = ns // spb, spb * sw, ns * sw

    def body(dr2_ref, up_ref, r1_ref, wu_ref, wd_ref, g1_ref, dup_ref, dr1_ref, dg1_ref, db1_ref, acc):
        i = pl.program_id(0)
        f = pl.program_id(1)

        @pl.when((i == 0) & (f == 0))
        def _():
            dg1_ref[...] = jnp.zeros_like(dg1_ref)
            db1_ref[...] = jnp.zeros_like(db1_ref)

        @pl.when(f == 0)
        def _():
            acc[...] = jnp.zeros_like(acc)

        dact = _dot(dr2_ref[...], wd_ref[...], NT)
        dup_ref[...] = (dact * 2.0 * jnp.maximum(up_ref[...].astype(F32), 0.0)).astype(dup_ref.dtype)
        for s in range(spb):
            acc[...] += _dot(dup_ref[:, s * sw:(s + 1) * sw], wu_ref[s], NT)

        @pl.when(f == nf - 1)
        def _():
            dh1 = acc[...] + ALPHA * dr2_ref[...].astype(F32)
            xhat, rstd = _ln_fwd(r1_ref[...])
            dg1_ref[0:1, :] += _colsum(dh1 * xhat)
            db1_ref[0:1, :] += _colsum(dh1)
            dr1_ref[...] = _ln_bwd(dh1, xhat, rstd, g1_ref[...]).astype(dr1_ref.dtype)

    row = pl.BlockSpec((tm, d), lambda i, f: (i, 0))
    acc8 = pl.BlockSpec((SUBLANES, d), lambda i, f: (0, 0))
    return pl.pallas_call(
        body, name="mlp_bwd",
        grid=(t // tm, nf),
        in_specs=[row, pl.BlockSpec((tm, tf), lambda i, f: (i, f)), row,
                  pl.BlockSpec((spb, d, sw), lambda i, f: (f, 0, 0)), pl.BlockSpec((tf, d), lambda i, f: (f, 0)),
                  pl.BlockSpec((1, d), lambda i, f: (0, 0))],
        out_specs=[pl.BlockSpec((tm, tf), lambda i, f: (i, f)), row, acc8, acc8],
        out_shape=[jax.ShapeDtypeStruct((t, ff), BF16), jax.ShapeDtypeStruct((t, d), BF16),
                   jax.ShapeDtypeStruct((SUBLANES, d), F32), jax.ShapeDtypeStruct((SUBLANES, d), F32)],
        scratch_shapes=[pltpu.VMEM((tm, d), F32)],
        compiler_params=_params(("arbitrary", "arbitrary")),
    )(dr2, up, r1, w_up, w_down, ln1_g)


def _dx_kernel(segs, w_main, ddt, w_dt, dr1, tm, tk, comm=None):
    t, d = dr1.shape
    nblk = [s.shape[1] // tk for s in segs]
    starts = [sum(nblk[:i]) for i in range(len(segs))]
    nk = sum(nblk)
    nseg = len(segs)

    def body(*refs):
        seg_refs = refs[:nseg]
        w_ref, ddt_ref, wdt_ref, dr1_ref, o_ref, acc = refs[nseg:]
        k = pl.program_id(1)

        @pl.when(k == 0)
        def _():
            acc[...] = ALPHA * dr1_ref[...].astype(F32) + _dot(ddt_ref[...], wdt_ref[...], NT)

        for si in range(nseg):
            @pl.when((k >= starts[si]) & (k < starts[si] + nblk[si]))
            def _(si=si):
                acc[...] += _dot(seg_refs[si][...], w_ref[...], NT)

        @pl.when(k == nk - 1)
        def _():
            o_ref[...] = acc[...]

    def seg_spec(si):
        return pl.BlockSpec((tm, tk), lambda i, k: (i, jnp.clip(k - starts[si], 0, nblk[si] - 1)))

    row = pl.BlockSpec((tm, d), lambda i, k: (i, 0))
    grid = (t // tm, nk)
    c_in, c_in_specs, c_out_specs, c_out_shapes = _comm_specs(comm)
    return pl.pallas_call(
        _fuse_comm(body, grid, nseg + 4, 1, comm), name="dx",
        grid=grid,
        in_specs=[seg_spec(si) for si in range(nseg)] + [
            pl.BlockSpec((d, tk), lambda i, k: (0, k)), pl.BlockSpec((tm, LANES), lambda i, k: (i, 0)),
            pl.BlockSpec((d, LANES), lambda i, k: (0, 0)), row] + c_in_specs,
        out_specs=[row] + c_out_specs,
        out_shape=[jax.ShapeDtypeStruct((t, d), F32)] + c_out_shapes,
        scratch_shapes=[pltpu.VMEM((tm, d), F32)] + (list(comm.scratch) if comm else []),
        compiler_params=_params(("arbitrary", "arbitrary")),
    )(*segs, w_main, ddt, w_dt, dr1, *c_in)


def _dims(d):
    inner = 2 * d
    heads = inner // HEAD_DIM
    cd = inner + 2 * GROUPS * STATE
    assert heads <= LANES and inner % (GROUPS * LANES) == 0 and d % (len(POOL_WINDOWS) * LANES) == 0
    o_z, o_xbc, o_dt, o_lg = d, d + inner, d + inner + cd, d + inner + cd + heads
    return inner, heads, cd, (o_z, o_xbc, o_dt, o_lg)


def _row(v, width=None):
    v = v.reshape(1, -1).astype(F32)
    if width is not None and v.shape[1] < width:
        v = jnp.pad(v, ((0, 0), (0, width - v.shape[1])))
    return v


def _local_step(x2, tgt2, w, shards, core, bl):
    t, d = x2.shape
    inner, heads, cd, _ = _dims(d)
    gs = GROUPS * STATE
    nc = t // bl // CHUNK
    w_main, w_dt = _w_in_internal(w["w_in_blocks"], d)
    c_z, c_lg, c_u = cd, cd + inner, cd + inner + 2 * d
    conv_w8 = jnp.pad(w["conv_w"].astype(F32), ((0, SUBLANES - CONV_K), (0, 0)))
    conv_b = _row(w["conv_b"])
    dtb, alog = _row(w["dt_bias"], LANES), _row(w["a_log"], LANES)
    dskip_x = _row(jnp.repeat(w["d_skip"].reshape(-1), HEAD_DIM))
    normw = _row(w["ssd_norm_w"])
    col_head = lax.broadcasted_iota(jnp.int32, (LANES, inner), 1) // HEAD_DIM
    emat = (col_head == lax.broadcasted_iota(jnp.int32, (LANES, inner), 0)).astype(BF16)
    emat_t = emat.T
    w_main, w_dt = w_main.astype(BF16), w_dt.astype(BF16)
    b_gates, pool_scale = _row(w["b_gates"]), _row(w["pool_scale"])
    ln1_g, ln1_b, ln2_g, ln2_b = _row(w["ln1_g"]), _row(w["ln1_b"]), _row(w["ln2_g"]), _row(w["ln2_b"])

    tm = min(512, t)
    tk = min(1024, d)
    ct = min(512, d)
    rt = min(512, t // bl)
    nct = t // bl // rt
    mm = functools.partial(_matmul, bm=1024, bn=tk, bk=1024)
    mmt = functools.partial(_matmul, bm=1024, bn=tk, bk=2048)
    xb = x2.astype(BF16)

    proj, xbc, dsl, *gathered = _in_proj(xb, w_main, conv_w8, conv_b, cd, t // bl, 1024, tk,
                                    _all_gather_comm([shards[n] for n in OTHERS]))
    gathered = dict(zip(OTHERS, gathered))
    w_ssd, w_out, w_down = (gathered[n].reshape(-1, d) for n in ("w_ssd_proj", "w_out", "w_down"))
    w_up = gathered["w_up"]
    npg = len(POOL_WINDOWS)
    cg = d // npg
    wpg = gathered["w_pool_group"].reshape(N_DEV, npg, cg // N_DEV, cg).transpose(1, 0, 2, 3).reshape(npg, cg, cg)
    dt_raw = mm(xb, w_dt, "nn", F32, name="in_proj_dt")
    y, yn, states = _ssd_fwd(xbc, proj, dt_raw, dtb, alog, dskip_x, normw, emat, bl, inner, c_z)
    yssd = mmt(yn, w_ssd, "nn", BF16, name="ssd_proj")
    ypr = _pool_fwd(proj, wpg, bl, d, c_u)
    merged, r1 = _merge_fwd(proj, ypr, yssd, x2, w_out, b_gates, pool_scale, d, c_lg, tm)
    tmm = min(1024, t)
    up, h1, dr2, loss8, dg2, db2 = _mlp_fwd(r1, tgt2, w_up, w_down, ln1_g, ln1_b, ln2_g, ln2_b, tmm)

    dup, dr1, dg1, db1 = _mlp_bwd(dr2, up, r1, w_up, w_down, ln1_g, tmm)
    relu2 = lambda v: jnp.square(jnp.maximum(v, 0.0))
    g = {}
    g["w_down"] = mmt(up, dr2, "tn", BF16, name="dw_down", a_fn=relu2)
    g["w_up"] = mmt(h1, dup, "tn", BF16, name="dw_up", col_blocks=N_DEV)
    g["w_out"] = mmt(merged, dr1, "tn", BF16, name="dw_out")
    dlg, dyp, dys, dbg, dps = _merge_bwd(dr1, proj, ypr, yssd, w_out, b_gates, pool_scale, d, c_lg, tm)
    du, dwpg = _pool_bwd(proj, dyp, wpg, bl, d, c_u)
    g["w_pool_group"] = dwpg.reshape(npg, N_DEV, cg // N_DEV, cg).transpose(1, 0, 2, 3).reshape(
        N_DEV, npg * cg // N_DEV, cg).astype(BF16)
    dyn = mm(dys, w_ssd, "nt", BF16, name="d_ssd_proj")
    g["w_ssd_proj"] = mmt(yn, dys, "tn", BF16, name="dw_ssd_proj")

    def chip_sums(names, tag):
        parts = [g.pop(n).reshape((N_DEV,) + shards_2d[n]) for n in names]
        recv = _run_comm(_rs_sibling_comm(parts), "rs_sibling_" + tag)
        return [_add_pairs(core, p, r, "rs_add_" + n) for n, p, r in zip(names, parts, recv)]

    shards_2d = {n: s.shape for n, s in shards.items()}
    shards_2d["w_in"] = w["w_in_blocks"].shape[1:]
    dxs, dbm, dcm, dz, ddt, dnw, dsk, dalog, ddtb, *recv_others = _ssd_bwd(
        xbc, proj, dt_raw, y, dyn, states, dtb, alog, dskip_x, normw, emat, emat_t, bl, inner, c_z,
        comm=_rs_chips_comm(chip_sums(OTHERS, "a")))
    dxs_p, dcw_x, dcb_x = _conv_bwd(proj, dsl, dxs, conv_w8, nct, 0, inner, ct, rt, "conv_bwd_x")
    dbm_p, dcw_b, dcb_b = _conv_bwd(proj, dsl, dbm, conv_w8, nct, inner, gs, ct, rt, "conv_bwd_b")
    dcm_p, dcw_c, dcb_c = _conv_bwd(proj, dsl, dcm, conv_w8, nct, inner + gs, gs, ct, rt, "conv_bwd_c")
    segs = [dxs_p, dbm_p, dcm_p, dz, dlg, du]
    keys = [k for k, _, _ in _col_segments(d)]
    dws = {k: mmt(xb, s, "tn", BF16, name="dw_in_" + k) for k, s in zip(keys, segs + [ddt])}
    g["w_in"] = _w_in_grad_blocks(dws, d, w["w_in_blocks"].shape[2])
    grad_x, recv_w_in = _dx_kernel(segs, w_main, ddt, w_dt, dr1, tmm, tk,
                                   comm=_rs_chips_comm(chip_sums(["w_in"], "b")))
    recv = dict(zip(OTHERS, recv_others))
    recv["w_in"] = recv_w_in
    g["conv_w"] = jnp.concatenate([dcw_x, dcw_b, dcw_c], axis=1)[:CONV_K]
    g["conv_b"] = jnp.concatenate([dcb_x, dcb_b, dcb_c], axis=1)[0]
    g["b_gates"], g["pool_scale"] = dbg[0], dps[0]
    g["dt_bias"], g["a_log"], g["d_skip"] = ddtb[0, :heads], dalog[0, :heads], dsk[0, :heads]
    g["ssd_norm_w"] = dnw[:, 0, :].reshape(inner)
    g["ln1_g"], g["ln1_b"], g["ln2_g"], g["ln2_b"] = dg1[0], db1[0], dg2[0], db2[0]
    return loss8, grad_x, g, recv


BIG = ("w_in", "w_ssd_proj", "w_pool_group", "w_out", "w_up", "w_down")
OTHERS = BIG[1:]
SMALL = ("b_gates", "conv_b", "dt_bias", "a_log", "d_skip", "ssd_norm_w", "pool_scale", "ln1_g", "ln1_b", "ln2_g",
         "ln2_b")
SMALL_PACK = SMALL + ("conv_w",)
NAMES = ("w_in", "b_gates", "conv_w", "conv_b", "dt_bias", "a_log", "d_skip", "ssd_norm_w", "w_ssd_proj",
         "w_pool_group", "pool_scale", "w_out", "ln1_g", "ln1_b", "w_up", "w_down", "ln2_g", "ln2_b")


def _size(shape):
    n = 1
    for s in shape:
        n *= s
    return n


def _rows128(v):
    v = v.astype(F32).reshape((-1, v.shape[-1]))
    n = v.shape[-1]
    v = jnp.pad(v, ((0, 0), (0, -n % LANES)))
    return v.reshape(-1, LANES)


def _pack_small(vals, extra):
    parts = [_rows128(vals[n]) for n in SMALL_PACK]
    parts.append(jnp.pad(extra.reshape(1, 1).astype(F32), ((0, 0), (0, LANES - 1))))
    buf = jnp.concatenate(parts, axis=0)
    return jnp.pad(buf, ((0, -buf.shape[0] % SUBLANES), (0, 0)))


def _unpack_small(buf, shapes):
    out, off = {}, 0
    for n in SMALL_PACK:
        lead, last = _size(shapes[n][:-1]), shapes[n][-1]
        per = -(-last // LANES)
        out[n] = buf[off:off + lead * per].reshape(lead, per * LANES)[:, :last].reshape(shapes[n])
        off += lead * per
    return out, buf[off, 0]


def _col_segments(d):
    inner, heads, cd, (o_z, o_xbc, o_dt, o_lg) = _dims(d)
    gs = GROUPS * STATE
    return [("xs", o_xbc, inner), ("B", o_xbc + inner, gs), ("C", o_xbc + inner + gs, gs), ("z", o_z, inner),
            ("lg", o_lg, 2 * d), ("u", 0, d), ("dt", o_dt, heads)]


def _cols_from_blocks(blocks, start, width, bw):
    parts, pos = [], start
    while pos < start + width:
        k, off = divmod(pos, bw)
        n = min(bw - off, start + width - pos)
        parts.append(blocks[k][:, off:off + n])
        pos += n
    return parts


def _w_in_internal(blocks, d):
    bw = blocks.shape[2]
    segs = _col_segments(d)
    heads = segs[-1][2]
    main = [p for _, s, w_ in segs[:-1] for p in _cols_from_blocks(blocks, s, w_, bw)]
    w_dt = jnp.concatenate(_cols_from_blocks(blocks, segs[-1][1], heads, bw), axis=1)
    return jnp.concatenate(main, axis=1), jnp.pad(w_dt, ((0, 0), (0, LANES - heads)))


def _w_in_grad_blocks(dws, d, bw):
    order = sorted(_col_segments(d), key=lambda s: s[1])
    blocks = []
    for k in range(N_DEV):
        lo, hi, parts = k * bw, (k + 1) * bw, []
        for key, s, w_ in order:
            a, b = max(lo, s), min(hi, s + w_)
            if a < b:
                parts.append(dws[key][:, a - s:b - s])
        blocks.append(jnp.concatenate(parts, axis=1))
    return jnp.stack(blocks)


def _mesh_pos():
    return lax.axis_index("x"), lax.axis_index("y"), lax.axis_index("c")


def _all_gather_comm(shards):
    nw = len(shards)

    def setup(x_refs, out_refs, scr):
        send_sems, recv_sems, local_sems = scr
        x, y, c = _mesh_pos()
        me, sibling = (x, y, c), (x, y, 1 - c)
        chips = [(1 - x, y), (x, 1 - y), (1 - x, 1 - y)]

        def copy(wi, k, block, to, from_input=False):
            px, py, pc = block
            blk = out_refs[wi].at[4 * px + 2 * py + pc]
            return pltpu.make_async_remote_copy(
                src_ref=x_refs[wi] if from_input else blk, dst_ref=blk,
                send_sem=send_sems.at[7 * wi + k], recv_sem=recv_sems.at[7 * wi + k], device_id=to,
                device_id_type=MESH)

        mine = [pltpu.make_async_copy(x_refs[wi], out_refs[wi].at[4 * x + 2 * y + c], local_sems.at[wi])
                for wi in range(nw)]
        sends = []
        for wi in range(nw):
            sends.append(copy(wi, 0, me, sibling, True))
            sends += [copy(wi, 1 + j, me, (*chip, c), True) for j, chip in enumerate(chips)]
        return copy, mine, sends, me, sibling, chips, c

    def start(x_refs, out_refs, scr):
        _, mine, sends, _, _, _, _ = setup(x_refs, out_refs, scr)
        for cp in mine + sends:
            cp.start()

    def wait(x_refs, out_refs, scr):
        copy, mine, sends, me, sibling, chips, c = setup(x_refs, out_refs, scr)
        passed = []
        for wi in range(nw):
            for j, chip in enumerate(chips):
                copy(wi, 1 + j, (*chip, c), me).wait_recv()
                passed.append(copy(wi, 4 + j, (*chip, c), sibling))
                passed[-1].start()
        for wi in range(nw):
            copy(wi, 0, sibling, me).wait_recv()
            for j, chip in enumerate(chips):
                copy(wi, 4 + j, (*chip, 1 - c), me).wait_recv()
        for cp in sends + passed:
            cp.wait_send()
        for cp in mine:
            cp.wait()

    return _Comm(
        inputs=list(shards),
        out_shapes=[jax.ShapeDtypeStruct((N_DEV,) + s.shape, s.dtype) for s in shards],
        scratch=[pltpu.SemaphoreType.DMA((7 * nw,)), pltpu.SemaphoreType.DMA((7 * nw,)),
                 pltpu.SemaphoreType.DMA((nw,))],
        start=start, wait=wait)


def _rs_sibling_comm(parts):
    nw = len(parts)
    half = N_DEV // 2

    def copies(p_refs, recv_refs, scr):
        send_sems, recv_sems = scr
        x, y, c = _mesh_pos()
        return [pltpu.make_async_remote_copy(
            src_ref=p_refs[wi].at[2 * q + 1 - c], dst_ref=recv_refs[wi].at[q],
            send_sem=send_sems.at[half * wi + q], recv_sem=recv_sems.at[half * wi + q],
            device_id=(x, y, 1 - c), device_id_type=MESH) for wi in range(nw) for q in range(half)]

    def start(p_refs, recv_refs, scr):
        for cp in copies(p_refs, recv_refs, scr):
            cp.start()

    def wait(p_refs, recv_refs, scr):
        for cp in copies(p_refs, recv_refs, scr):
            cp.wait()

    return _Comm(
        inputs=list(parts),
        out_shapes=[jax.ShapeDtypeStruct((half,) + p.shape[1:], p.dtype) for p in parts],
        scratch=[pltpu.SemaphoreType.DMA((half * nw,)), pltpu.SemaphoreType.DMA((half * nw,))],
        start=start, wait=wait)


def _rs_chips_comm(tbs):
    nw = len(tbs)

    def copies(t_refs, o_refs, scr):
        send_sems, recv_sems, local_sems = scr
        x, y, c = _mesh_pos()
        p = 2 * x + y
        chips = [(1 - x, y), (x, 1 - y), (1 - x, 1 - y)]
        own = [pltpu.make_async_copy(t_refs[wi].at[p], o_refs[wi].at[p], local_sems.at[wi]) for wi in range(nw)]
        remote = [pltpu.make_async_remote_copy(
            src_ref=t_refs[wi].at[2 * qx + qy], dst_ref=o_refs[wi].at[p], send_sem=send_sems.at[3 * wi + j],
            recv_sem=recv_sems.at[3 * wi + j], device_id=(qx, qy, c), device_id_type=MESH)
            for wi in range(nw) for j, (qx, qy) in enumerate(chips)]
        arriving = [pltpu.make_async_remote_copy(
            src_ref=t_refs[wi].at[p], dst_ref=o_refs[wi].at[2 * qx + qy], send_sem=send_sems.at[3 * wi + j],
            recv_sem=recv_sems.at[3 * wi + j], device_id=(qx, qy, c), device_id_type=MESH)
            for wi in range(nw) for j, (qx, qy) in enumerate(chips)]
        return own, remote, arriving

    def start(t_refs, o_refs, scr):
        own, remote, _ = copies(t_refs, o_refs, scr)
        for cp in own + remote:
            cp.start()

    def wait(t_refs, o_refs, scr):
        own, remote, arriving = copies(t_refs, o_refs, scr)
        for cp in arriving:
            cp.wait_recv()
        for cp in remote:
            cp.wait_send()
        for cp in own:
            cp.wait()

    return _Comm(
        inputs=list(tbs),
        out_shapes=[jax.ShapeDtypeStruct(t_.shape, t_.dtype) for t_ in tbs],
        scratch=[pltpu.SemaphoreType.DMA((3 * nw,)), pltpu.SemaphoreType.DMA((3 * nw,)),
                 pltpu.SemaphoreType.DMA((nw,))],
        start=start, wait=wait)


def _row_tile(rows, cap=256):
    if rows <= cap:
        return rows
    return max(t_ for t_ in range(SUBLANES, cap + 1, SUBLANES) if rows % t_ == 0)


def _add_pairs(core, part, recv, name):
    n, r, c_ = recv.shape
    tr = _row_tile(r)

    def body(core_ref, a_ref, b_ref, o_ref):
        o_ref[...] = (a_ref[...].astype(F32) + b_ref[...].astype(F32)).astype(o_ref.dtype)

    spec = pl.BlockSpec((1, tr, c_), lambda q, i, core_ref: (q, i, 0))
    return pl.pallas_call(
        body, name=name,
        grid_spec=pltpu.PrefetchScalarGridSpec(
            num_scalar_prefetch=1, grid=(n, r // tr),
            in_specs=[pl.BlockSpec((1, tr, c_), lambda q, i, core_ref: (2 * q + core_ref[0], i, 0)), spec],
            out_specs=spec),
        out_shape=jax.ShapeDtypeStruct(recv.shape, BF16), compiler_params=_params(("parallel", "parallel")),
    )(core, part, recv)


def _small_allreduce(vec, name):
    rows = vec.shape[0]

    def body(x_ref, o_ref, buf, send_sems, recv_sems):
        x, y, c = _mesh_pos()
        me = 4 * x + 2 * y + c
        buf[me] = x_ref[...]
        cps = []
        for k in range(1, N_DEV):
            peer = (1 - x if k & 4 else x, 1 - y if k & 2 else y, 1 - c if k & 1 else c)
            cps.append(pltpu.make_async_remote_copy(
                src_ref=x_ref, dst_ref=buf.at[me], send_sem=send_sems.at[k - 1], recv_sem=recv_sems.at[k - 1],
                device_id=peer, device_id_type=MESH))
        for cp in cps:
            cp.start()
        for k in range(1, N_DEV):
            px, py, pc = (1 - x if k & 4 else x, 1 - y if k & 2 else y, 1 - c if k & 1 else c)
            pltpu.make_async_remote_copy(
                src_ref=x_ref, dst_ref=buf.at[4 * px + 2 * py + pc], send_sem=send_sems.at[k - 1],
                recv_sem=recv_sems.at[k - 1], device_id=(px, py, pc), device_id_type=MESH).wait_recv()
        for cp in cps:
            cp.wait_send()
        acc = buf[0]
        for k in range(1, N_DEV):
            acc = acc + buf[k]
        o_ref[...] = acc

    vm = pl.BlockSpec(memory_space=pltpu.VMEM)
    return pl.pallas_call(
        body, name=name,
        in_specs=[vm], out_specs=vm,
        out_shape=jax.ShapeDtypeStruct(vec.shape, F32),
        scratch_shapes=[pltpu.VMEM((N_DEV, rows, LANES), F32), pltpu.SemaphoreType.DMA((N_DEV - 1,)),
                        pltpu.SemaphoreType.DMA((N_DEV - 1,))],
    )(vec)


def _adamw(gparts, w, m, v, name):
    n, r, c_ = gparts.shape
    tr = _row_tile(r)
    c1 = 1.0 / (1.0 - B1 ** STEP)
    c2 = 1.0 / (1.0 - B2 ** STEP)

    def body(g_ref, w_ref, m_ref, v_ref, go_ref, d_ref, mo_ref, vo_ref):
        g = g_ref[0].astype(F32)
        for q in range(1, n):
            g = g + g_ref[q].astype(F32)
        mn = B1 * m_ref[...] + (1.0 - B1) * g
        vn = B2 * v_ref[...] + (1.0 - B2) * (g * g)
        go_ref[...] = g
        mo_ref[...] = mn
        vo_ref[...] = vn
        d_ref[...] = -LR * ((mn * c1) / (jnp.sqrt(vn * c2) + ADAM_EPS) + WD * w_ref[...])

    spec = pl.BlockSpec((tr, c_), lambda i: (i, 0))
    out = jax.ShapeDtypeStruct((r, c_), F32)
    return pl.pallas_call(
        body, name=name, grid=(r // tr,),
        in_specs=[pl.BlockSpec((n, tr, c_), lambda i: (0, i, 0)), spec, spec, spec],
        out_specs=[spec] * 4, out_shape=[out] * 4, compiler_params=_params(("parallel",)),
    )(gparts, w, m, v)


def kernel(x, w_in, b_gates, conv_w, conv_b, dt_bias, a_log, d_skip, ssd_norm_w, w_ssd_proj, w_pool_group, pool_scale, w_out, ln1_g, ln1_b, w_up, w_down, ln2_g, ln2_b, loss_target, m_w_in, m_b_gates, m_conv_w, m_conv_b, m_dt_bias, m_a_log, m_d_skip, m_ssd_norm_w, m_w_ssd_proj, m_w_pool_group, m_pool_scale, m_w_out, m_ln1_g, m_ln1_b, m_w_up, m_w_down, m_ln2_g, m_ln2_b, v_w_in, v_b_gates, v_conv_w, v_conv_b, v_dt_bias, v_a_log, v_d_skip, v_ssd_norm_w, v_w_ssd_proj, v_w_pool_group, v_pool_scale, v_w_out, v_ln1_g, v_ln1_b, v_w_up, v_w_down, v_ln2_g, v_ln2_b):
    ws = (w_in, b_gates, conv_w, conv_b, dt_bias, a_log, d_skip, ssd_norm_w, w_ssd_proj, w_pool_group, pool_scale,
          w_out, ln1_g, ln1_b, w_up, w_down, ln2_g, ln2_b)
    ms = (m_w_in, m_b_gates, m_conv_w, m_conv_b, m_dt_bias, m_a_log, m_d_skip, m_ssd_norm_w, m_w_ssd_proj,
          m_w_pool_group, m_pool_scale, m_w_out, m_ln1_g, m_ln1_b, m_w_up, m_w_down, m_ln2_g, m_ln2_b)
    vs = (v_w_in, v_b_gates, v_conv_w, v_conv_b, v_dt_bias, v_a_log, v_d_skip, v_ssd_norm_w, v_w_ssd_proj,
          v_w_pool_group, v_pool_scale, v_w_out, v_ln1_g, v_ln1_b, v_w_up, v_w_down, v_ln2_g, v_ln2_b)
    w = {n: a[0] for n, a in zip(NAMES, ws)}
    m = {n: a[0] for n, a in zip(NAMES, ms)}
    v = {n: a[0] for n, a in zip(NAMES, vs)}
    out_shapes = {n: a.shape for n, a in zip(NAMES, ws)}
    bl, s, d = x.shape
    x2, tgt2 = x.reshape(bl * s, d), loss_target.reshape(bl * s, d)
    xi, yi, ci = _mesh_pos()
    me = 4 * xi + 2 * yi + ci
    zero = jnp.zeros((), F32)
    shapes = {n: w[n].shape for n in NAMES}
    shape2d = {n: (_size(shapes[n][:-1]), shapes[n][-1]) for n in BIG}
    cwl = shapes["conv_w"][1]

    conv_place = lax.dynamic_update_slice(jnp.zeros((CONV_K, N_DEV * cwl), F32), w["conv_w"], (0, me * cwl))
    conv_full = _small_allreduce(_rows128(conv_place), "gather_conv_w")
    conv_full = conv_full.reshape(CONV_K, N_DEV * cwl)

    shards = {n: w[n].astype(BF16).reshape(shape2d[n]) for n in BIG}
    full = {n: w[n] for n in SMALL}
    full["conv_w"] = conv_full
    full["w_in_blocks"] = _run_comm(_all_gather_comm([shards.pop("w_in")]), "all_gather_w_in")[0]
    loss8, grad_x, g, recv = _local_step(x2, tgt2, full, shards, ci.astype(jnp.int32).reshape(1), bl)

    small_sum = _small_allreduce(_pack_small(g, loss8[0, 0]), "small_allreduce")
    ex_shapes = {n: shapes[n] for n in SMALL}
    ex_shapes["conv_w"] = (CONV_K, N_DEV * cwl)
    gsum, loss = _unpack_small(small_sum, ex_shapes)
    gsum["conv_w"] = lax.dynamic_slice(gsum["conv_w"], (0, me * cwl), (CONV_K, cwl))
    gs_pk = _pack_small(gsum, zero)
    ws_pk, ms_pk, vs_pk = (_pack_small(t_, zero) for t_ in (w, m, v))
    small_out = _adamw(gs_pk[None], ws_pk, ms_pk, vs_pk, "adamw_small")
    loc_shapes = {n: shapes[n] for n in SMALL_PACK}
    res = [_unpack_small(o, loc_shapes)[0] for o in small_out]

    for n in BIG:
        outs = _adamw(recv[n], *(t_[n].reshape(shape2d[n]) for t_ in (w, m, v)), "adamw_" + n)
        for r_, o in zip(res, outs):
            r_[n] = o

    def ordered(r_):
        return [r_[n].reshape(out_shapes[n]) for n in NAMES]

    return (loss, grad_x.reshape(bl, s, d), *ordered(res[0]), *ordered(res[1]), *ordered(res[2]), *ordered(res[3]))
```

```python
import collections
import functools

import jax
import jax.numpy as jnp
from jax import lax
from jax.experimental import pallas as pl
from jax.experimental.pallas import tpu as pltpu

F32 = jnp.float32
BF16 = jnp.bfloat16
MESH = pl.DeviceIdType.MESH

HEAD_DIM = 64
STATE = 128
GROUPS = 8
CONV_K = 4
CHUNK = 256
POOL_WINDOWS = (2, 4, 8, 16)
ALPHA = 2.0 ** 0.25
LN_EPS = 1e-5
RMS_EPS = 1e-5
LR, B1, B2, ADAM_EPS, WD, STEP = 0.001, 0.9, 0.999, 1e-08, 0.01, 10
N_DEV = 8
LANES = 128
SUBLANES = 8
VMEM_LIMIT = 56 * 1024 * 1024
NEG_BIG = -1e30

NN = (((1,), (0,)), ((), ()))
NT = (((1,), (1,)), ((), ()))
TN = (((0,), (0,)), ((), ()))


def _dot(a, b, dims=NN):
    return lax.dot_general(a.astype(BF16), b.astype(BF16), dims, preferred_element_type=F32)


def _dot_exact01(q, e, dims=NN):
    hi = q.astype(BF16)
    r1 = q - hi.astype(F32)
    mid = r1.astype(BF16)
    lo = (r1 - mid.astype(F32)).astype(BF16)
    f = lambda p: lax.dot_general(p, e, dims, preferred_element_type=F32)
    return f(hi) + f(mid) + f(lo)


def _params(sem):
    return pltpu.CompilerParams(dimension_semantics=sem, vmem_limit_bytes=VMEM_LIMIT)


def _sigmoid(x):
    return 1.0 / (1.0 + jnp.exp(-x))


def _colsum(x):
    return jnp.sum(x, axis=0, keepdims=True)


def _ln_fwd(r):
    mu = jnp.mean(r, axis=-1, keepdims=True)
    xc = r - mu
    var = jnp.mean(xc * xc, axis=-1, keepdims=True)
    rstd = lax.rsqrt(var + LN_EPS)
    return xc * rstd, rstd


def _ln_bwd(dy, xhat, rstd, g):
    dxh = dy * g
    m1 = jnp.mean(dxh, axis=-1, keepdims=True)
    m2 = jnp.mean(dxh * xhat, axis=-1, keepdims=True)
    return rstd * (dxh - m1 - xhat * m2)


_Comm = collections.namedtuple("_Comm", "inputs out_shapes scratch start wait")
ANY = pl.BlockSpec(memory_space=pl.ANY)


def _fuse_comm(body, grid, n_in, n_out, comm):
    if comm is None:
        return body
    ci, co = len(comm.inputs), len(comm.out_shapes)

    def fused(*refs):
        ins, cins = refs[:n_in], refs[n_in:n_in + ci]
        o0 = n_in + ci
        outs, couts = refs[o0:o0 + n_out], refs[o0 + n_out:o0 + n_out + co]
        rest = refs[o0 + n_out + co:]
        scr, cscr = rest[:len(rest) - len(comm.scratch)], rest[len(rest) - len(comm.scratch):]
        ids = [pl.program_id(a) for a in range(len(grid))]
        first, last = ids[0] == 0, ids[0] == grid[0] - 1
        for a in range(1, len(grid)):
            first, last = first & (ids[a] == 0), last & (ids[a] == grid[a] - 1)

        @pl.when(first)
        def _():
            comm.start(cins, couts, cscr)

        body(*ins, *outs, *scr)

        @pl.when(last)
        def _():
            comm.wait(cins, couts, cscr)

    return fused


def _comm_specs(comm):
    if comm is None:
        return [], [], [], []
    return list(comm.inputs), [ANY] * len(comm.inputs), [ANY] * len(comm.out_shapes), list(comm.out_shapes)


def _run_comm(comm, name):
    ci, co = len(comm.inputs), len(comm.out_shapes)

    def body(*refs):
        comm.start(refs[:ci], refs[ci:ci + co], refs[ci + co:])
        comm.wait(refs[:ci], refs[ci:ci + co], refs[ci + co:])

    return pl.pallas_call(body, name=name, in_specs=[ANY] * ci, out_specs=[ANY] * co, out_shape=list(comm.out_shapes),
                          scratch_shapes=list(comm.scratch))(*comm.inputs)


def _matmul(a, b, mode, out_dtype, bm, bn, bk, name, a_fn=None, col_blocks=0, comm=None):
    if mode == "nn":
        (m, k), n, dims = a.shape, b.shape[1], NN
    elif mode == "nt":
        (m, k), n, dims = a.shape, b.shape[0], NT
    else:
        (k, m), n, dims = a.shape, b.shape[1], TN
    bm, bn, bk = min(bm, m), min(bn, n), min(bk, k)
    assert m % bm == 0 and n % bn == 0 and k % bk == 0, (name, m, n, k, bm, bn, bk)
    nk = k // bk
    if mode == "nn":
        a_spec = pl.BlockSpec((bm, bk), lambda i, j, kk: (i, kk))
        b_spec = pl.BlockSpec((bk, bn), lambda i, j, kk: (kk, j))
    elif mode == "nt":
        a_spec = pl.BlockSpec((bm, bk), lambda i, j, kk: (i, kk))
        b_spec = pl.BlockSpec((bn, bk), lambda i, j, kk: (j, kk))
    else:
        a_spec = pl.BlockSpec((bk, bm), lambda i, j, kk: (kk, i))
        b_spec = pl.BlockSpec((bk, bn), lambda i, j, kk: (kk, j))

    def body(a_ref, b_ref, o_ref, acc_ref):
        kk = pl.program_id(2)
        av = a_ref[...]
        if a_fn is not None:
            av = a_fn(av.astype(F32))
        prod = _dot(av, b_ref[...], dims)

        def emit(total):
            if col_blocks:
                for s in range(bn // slab):
                    o_ref[s] = total[:, s * slab:(s + 1) * slab].astype(o_ref.dtype)
            else:
                o_ref[...] = total.astype(o_ref.dtype)

        if nk == 1:
            emit(prod)
        else:
            @pl.when(kk == 0)
            def _():
                acc_ref[...] = prod

            @pl.when((kk > 0) & (kk < nk - 1))
            def _():
                acc_ref[...] += prod

            @pl.when(kk == nk - 1)
            def _():
                emit(acc_ref[...] + prod)

    if col_blocks:
        slab = n // col_blocks
        assert n % col_blocks == 0 and bn % slab == 0, (name, n, col_blocks, bn)
        out_spec = pl.BlockSpec((bn // slab, bm, slab), lambda i, j, kk: (j, i, 0))
        out_shape = jax.ShapeDtypeStruct((col_blocks, m, slab), out_dtype)
    else:
        out_spec = pl.BlockSpec((bm, bn), lambda i, j, kk: (i, j))
        out_shape = jax.ShapeDtypeStruct((m, n), out_dtype)
    grid = (m // bm, n // bn, nk)
    c_in, c_in_specs, c_out_specs, c_out_shapes = _comm_specs(comm)
    res = pl.pallas_call(
        _fuse_comm(body, grid, 2, 1, comm), name=name,
        grid=grid,
        in_specs=[a_spec, b_spec] + c_in_specs,
        out_specs=[out_spec] + c_out_specs,
        out_shape=[out_shape] + c_out_shapes,
        scratch_shapes=[pltpu.VMEM((bm, bn), F32)] + (list(comm.scratch) if comm else []),
        compiler_params=_params(("arbitrary",) * 3 if comm else ("parallel", "parallel", "arbitrary")),
    )(a, b, *c_in)
    return res if comm else res[0]


CONV_STRIP = 16
CONV_COLS = 512


def _conv_pre(ext_ref, w_ref, b_ref, r0, rows, cols=slice(None)):
    acc = b_ref[:, cols] + w_ref[0:1, cols] * ext_ref[pl.ds(r0 + SUBLANES - (CONV_K - 1), rows), cols]
    for k in range(1, CONV_K):
        acc = acc + w_ref[k:k + 1, cols] * ext_ref[pl.ds(r0 + SUBLANES - (CONV_K - 1) + k, rows), cols]
    return acc


def _in_proj(xb, w_main, conv_w8, conv_b, cd, seq_len, bm, bn, comm):
    t, d = xb.shape
    pw = w_main.shape[1]
    bm, bn = min(bm, seq_len), min(bn, d)
    assert t % bm == 0 and seq_len % bm == 0 and pw % bn == 0 and cd % bn == 0
    ncj = cd // bn
    tiles_per_seq = seq_len // bm

    def body(x_ref, w_ref, cw_ref, cb_ref, p_ref, xbc_ref, dsl_ref, ext_ref, carry_ref):
        i = pl.program_id(0)
        j = pl.program_id(1)

        def conv_previous():
            ext = ext_ref.at[(j + 1) % 2]
            cw = min(bn, CONV_COLS)
            for c0 in range(0, bn, cw):
                cols = slice(c0, c0 + cw)
                for r0 in range(0, bm, CONV_STRIP):
                    rows = slice(r0, r0 + CONV_STRIP)
                    acc = _conv_pre(ext, cw_ref, cb_ref, r0, CONV_STRIP, cols)
                    sg = _sigmoid(acc)
                    xbc_ref[rows, cols] = (acc * sg).astype(xbc_ref.dtype)
                    dsl_ref[rows, cols] = (sg * (1.0 + acc * (1.0 - sg))).astype(dsl_ref.dtype)

        def project(stash):
            pq = _dot(x_ref[...], w_ref[...]).astype(BF16)
            p_ref[...] = pq
            if stash:
                ext = ext_ref.at[j % 2]
                jc = jnp.minimum(j, ncj - 1)
                ext[0:SUBLANES, :] = jnp.where((i % tiles_per_seq) == 0, 0.0, carry_ref[jc])
                ext[SUBLANES:, :] = pq.astype(F32)
                carry_ref[jc] = ext[bm:bm + SUBLANES, :]

        @pl.when(j == 0)
        def _():
            project(True)

        @pl.when((j >= 1) & (j < ncj))
        def _():
            conv_previous()
            project(True)

        @pl.when(j == ncj)
        def _():
            conv_previous()
            project(False)

        @pl.when(j > ncj)
        def _():
            project(False)

    assert pw // bn > ncj
    grid = (t // bm, pw // bn)
    conv_col = lambda i, j: (0, jnp.clip(j - 1, 0, ncj - 1))
    c_in, c_in_specs, c_out_specs, c_out_shapes = _comm_specs(comm)
    conv_tile = pl.BlockSpec((bm, bn), lambda i, j: (i, jnp.clip(j - 1, 0, ncj - 1)))
    conv_out = jax.ShapeDtypeStruct((t, cd), BF16)
    return pl.pallas_call(
        _fuse_comm(body, grid, 4, 3, comm), name="in_proj",
        grid=grid,
        in_specs=[pl.BlockSpec((bm, d), lambda i, j: (i, 0)), pl.BlockSpec((d, bn), lambda i, j: (0, j)),
                  pl.BlockSpec((SUBLANES, bn), conv_col), pl.BlockSpec((1, bn), conv_col)] + c_in_specs,
        out_specs=[pl.BlockSpec((bm, bn), lambda i, j: (i, j)), conv_tile, conv_tile] + c_out_specs,
        out_shape=[jax.ShapeDtypeStruct((t, pw), BF16), conv_out, conv_out] + c_out_shapes,
        scratch_shapes=[pltpu.VMEM((2, bm + SUBLANES, bn), F32), pltpu.VMEM((ncj, SUBLANES, bn), F32)]
        + (list(comm.scratch) if comm else []),
        compiler_params=_params(("arbitrary", "arbitrary")),
    )(xb, w_main, conv_w8, conv_b, *c_in)


def _conv_bwd(proj, dsilu, dxbc, conv_w8, n_seq_chunks, col0, width, ct, L, name):
    t = proj.shape[0]
    nbc = t // L
    hb = L // SUBLANES
    ct = min(ct, width)
    assert col0 % ct == 0 and width % ct == 0
    cb0 = col0 // ct
    last_hb = t // SUBLANES - 1

    def body(x_ref, xb_ref, s_ref, sa_ref, d_ref, da_ref, w_ref, o_ref, dw_ref, db_ref, ext_ref, dc_ref):
        bc = pl.program_id(1)
        first = (bc % n_seq_chunks) == 0
        last = (bc % n_seq_chunks) == n_seq_chunks - 1

        @pl.when(bc == 0)
        def _():
            dw_ref[...] = jnp.zeros_like(dw_ref)
            db_ref[...] = jnp.zeros_like(db_ref)

        ext_ref[0:SUBLANES, :] = jnp.where(first, 0.0, xb_ref[...].astype(F32))
        ext_ref[SUBLANES:, :] = x_ref[...].astype(F32)
        for r0 in range(0, L, CONV_STRIP):
            rows = slice(r0, r0 + CONV_STRIP)
            dc_ref[rows, :] = d_ref[rows, :].astype(F32) * s_ref[rows, :].astype(F32)
        dc_ref[L:, :] = jnp.where(last, 0.0, da_ref[...].astype(F32)) * sa_ref[...].astype(F32)
        fold = lambda v: v[0:SUBLANES] + v[SUBLANES:CONV_STRIP]
        dws = [jnp.zeros((SUBLANES, ct), F32) for _ in range(CONV_K)]
        dbs = jnp.zeros((SUBLANES, ct), F32)
        for r0 in range(0, L, CONV_STRIP):
            dc = dc_ref[r0:r0 + CONV_STRIP, :]
            dx = w_ref[CONV_K - 1:CONV_K, :] * dc
            for k in range(CONV_K - 1):
                dx = dx + w_ref[k:k + 1, :] * dc_ref[pl.ds(r0 + CONV_K - 1 - k, CONV_STRIP), :]
            o_ref[r0:r0 + CONV_STRIP, :] = dx.astype(o_ref.dtype)
            for k in range(CONV_K):
                dws[k] = dws[k] + fold(dc * ext_ref[pl.ds(r0 + SUBLANES - (CONV_K - 1) + k, CONV_STRIP), :])
            dbs = dbs + fold(dc)
        for k in range(CONV_K):
            dw_ref[k:k + 1, :] += _colsum(dws[k])
        db_ref[0:1, :] += _colsum(dbs)

    return pl.pallas_call(
        body, name=name,
        grid=(width // ct, nbc),
        in_specs=[
            pl.BlockSpec((L, ct), lambda j, i: (i, cb0 + j)),
            pl.BlockSpec((SUBLANES, ct), lambda j, i: (jnp.maximum(i * hb - 1, 0), cb0 + j)),
            pl.BlockSpec((L, ct), lambda j, i: (i, cb0 + j)),
            pl.BlockSpec((SUBLANES, ct), lambda j, i: (jnp.minimum((i + 1) * hb, last_hb), cb0 + j)),
            pl.BlockSpec((L, ct), lambda j, i: (i, j)),
            pl.BlockSpec((SUBLANES, ct), lambda j, i: (jnp.minimum((i + 1) * hb, last_hb), j)),
            pl.BlockSpec((SUBLANES, ct), lambda j, i: (0, cb0 + j)),
        ],
        out_specs=[
            pl.BlockSpec((L, ct), lambda j, i: (i, j)),
            pl.BlockSpec((SUBLANES, ct), lambda j, i: (0, j)),
            pl.BlockSpec((SUBLANES, ct), lambda j, i: (0, j)),
        ],
        out_shape=[
            jax.ShapeDtypeStruct((t, width), BF16),
            jax.ShapeDtypeStruct((SUBLANES, width), F32),
            jax.ShapeDtypeStruct((SUBLANES, width), F32),
        ],
        scratch_shapes=[pltpu.VMEM((L + SUBLANES, ct), F32), pltpu.VMEM((L + SUBLANES, ct), F32)],
        compiler_params=_params(("parallel", "arbitrary")),
    )(proj, proj, dsilu, dsilu, dxbc, dxbc, conv_w8)


def _cumsum_rows(x, reverse=False):
    n = x.shape[0]
    row = lax.broadcasted_iota(jnp.int32, x.shape, 0)
    s = 1
    while s < n:
        if reverse:
            x = x + jnp.where(row < n - s, pltpu.roll(x, n - s, 0), 0.0)
        else:
            x = x + jnp.where(row >= s, pltpu.roll(x, s, 0), 0.0)
        s *= 2
    return x


def _ssd_scalars(dtr, dtb, alog):
    pre = dtr + dtb
    dt = jnp.maximum(pre, 0.0) + jnp.log(1.0 + jnp.exp(-jnp.abs(pre)))
    a = -jnp.exp(alog)
    acs = _cumsum_rows(dt * a) * LOG2E
    n = acs.shape[0]
    return pre, dt, a, acs, jnp.exp2(acs), jnp.exp2(acs[n - 1:n, :] - acs)


LOG2E = 1.4426950408889634


def _dot_2piece(q, e):
    hi = q.astype(BF16)
    mid = (q - hi.astype(F32)).astype(BF16)
    return lax.dot_general(jnp.concatenate([hi, mid], axis=1), jnp.concatenate([e, e], axis=0), NN,
                           preferred_element_type=F32)


def _ssd_group_common(dt_s, e_s, dec_s, e):
    return _dot_2piece(dt_s, e), _dot_2piece(e_s, e), _dot_2piece(dec_s, e)


def _decay_matrix(acs, acs_t, h, tri):
    return jnp.exp2(jnp.where(tri, acs[:, h:h + 1] - acs_t[h:h + 1, :], NEG_BIG))


def _head_mask(r, gw, dtype):
    lane = lax.broadcasted_iota(jnp.int32, (1, gw), 1)
    return ((lane >= r * HEAD_DIM) & (lane < (r + 1) * HEAD_DIM)).astype(dtype)


def _ssd_fwd(xbc, proj, dt_raw, dtb, alog, dskip_x, normw, emat, bl, inner, z_col0):
    t = xbc.shape[0]
    L = CHUNK
    nc = t // bl // L
    G = GROUPS
    gw = inner // G
    hpg = gw // HEAD_DIM
    assert z_col0 % gw == 0
    zb0 = z_col0 // gw
    bb0 = inner // STATE
    cb0 = bb0 + G

    P = G
    assert bb0 % P == 0 and cb0 % P == 0 and zb0 % P == 0

    def body(xs_ref, b_ref, c_ref, z_ref, dtr_ref, dtb_ref, alog_ref, dsk_ref, nw_ref, e_ref,
             y_ref, yn_ref, st_ref, h_ref):
        c = pl.program_id(1)
        _, dt_s, _, acs, e_s, dec_s = _ssd_scalars(dtr_ref[...], dtb_ref[...], alog_ref[...])
        acs_t = acs.T
        tri = lax.broadcasted_iota(jnp.int32, (L, L), 0) >= lax.broadcasted_iota(jnp.int32, (L, L), 1)
        lane = lax.broadcasted_iota(jnp.int32, (L, gw), 1)
        for g in range(G):
            cols = slice(g * gw, (g + 1) * gw)
            ncol = slice(g * STATE, (g + 1) * STATE)

            @pl.when(c == 0)
            def _():
                h_ref[g] = jnp.zeros((STATE, gw), F32)

            xs = xs_ref[:, cols].astype(F32)
            bg = b_ref[:, ncol]
            cg = c_ref[:, ncol]
            dt_x, e_x, dec_x = _ssd_group_common(dt_s, e_s, dec_s, e_ref[:, cols])
            xdt = xs * dt_x
            cb = _dot(cg, bg, NT)
            h = h_ref[g]
            st_ref[0, g] = h
            y = _dot(cg, h) * e_x + dsk_ref[:, cols] * xs
            for r in range(hpg):
                m = cb * _decay_matrix(acs, acs_t, g * hpg + r, tri)
                xr = jnp.where((lane >= r * HEAD_DIM) & (lane < (r + 1) * HEAD_DIM), xdt, 0.0)
                y = y + _dot(m, xr)
            h_ref[g] = h * e_x[L - 1:L, :] + _dot(bg, xdt * dec_x, TN)
            yq = y.astype(y_ref.dtype)
            y_ref[:, cols] = yq
            z = z_ref[:, cols].astype(F32)
            yg = yq.astype(F32) * (z * _sigmoid(z))
            rs = lax.rsqrt(jnp.mean(yg * yg, axis=-1, keepdims=True) + RMS_EPS)
            yn_ref[:, cols] = (yg * rs * nw_ref[:, cols]).astype(yn_ref.dtype)

    return pl.pallas_call(
        body, name="ssd_fwd",
        grid=(bl, nc, G // P),
        in_specs=[
            pl.BlockSpec((L, P * gw), lambda b, c, g: (b * nc + c, g)),
            pl.BlockSpec((L, P * STATE), lambda b, c, g: (b * nc + c, bb0 // P + g)),
            pl.BlockSpec((L, P * STATE), lambda b, c, g: (b * nc + c, cb0 // P + g)),
            pl.BlockSpec((L, P * gw), lambda b, c, g: (b * nc + c, zb0 // P + g)),
            pl.BlockSpec((L, LANES), lambda b, c, g: (b * nc + c, 0)),
            pl.BlockSpec((1, LANES), lambda b, c, g: (0, 0)),
            pl.BlockSpec((1, LANES), lambda b, c, g: (0, 0)),
            pl.BlockSpec((1, P * gw), lambda b, c, g: (0, g)),
            pl.BlockSpec((1, P * gw), lambda b, c, g: (0, g)),
            pl.BlockSpec((LANES, P * gw), lambda b, c, g: (0, g)),
        ],
        out_specs=[
            pl.BlockSpec((L, P * gw), lambda b, c, g: (b * nc + c, g)),
            pl.BlockSpec((L, P * gw), lambda b, c, g: (b * nc + c, g)),
            pl.BlockSpec((1, P, STATE, gw), lambda b, c, g: (b * nc + c, g, 0, 0)),
        ],
        out_shape=[
            jax.ShapeDtypeStruct((t, inner), BF16),
            jax.ShapeDtypeStruct((t, inner), BF16),
            jax.ShapeDtypeStruct((bl * nc, G, STATE, gw), F32),
        ],
        scratch_shapes=[pltpu.VMEM((G, STATE, gw), F32)],
        compiler_params=_params(("arbitrary", "arbitrary", "arbitrary")),
    )(xbc, xbc, xbc, proj, dt_raw, dtb, alog, dskip_x, normw, emat)


def _ssd_bwd(xbc, proj, dt_raw, y, dyn, states, dtb, alog, dskip_x, normw, emat, emat_t, bl, inner, z_col0,
             comm=None):
    t = xbc.shape[0]
    L = CHUNK
    nc = t // bl // L
    G = GROUPS
    gw = inner // G
    hpg = gw // HEAD_DIM
    zb0 = z_col0 // gw
    bb0 = inner // STATE
    cb0 = bb0 + G
    P = G

    def rc(j):
        return nc - 1 - j

    def body(xs_ref, b_ref, c_ref, z_ref, dtr_ref, y_ref, dyn_ref, st_ref, dtb_ref, alog_ref, dsk_ref,
             nw_ref, e_ref, et_ref,
             dxs_ref, db_ref, dc_ref, dz_ref, ddt_ref, dnw_ref, dsk_acc, dalog_acc, ddtb_acc,
             dh_ref):
        b = pl.program_id(0)
        j = pl.program_id(1)

        @pl.when((b == 0) & (j == 0))
        def _():
            dsk_acc[...] = jnp.zeros_like(dsk_acc)
            dalog_acc[...] = jnp.zeros_like(dalog_acc)
            ddtb_acc[...] = jnp.zeros_like(ddtb_acc)

        pre, dt_s, a_row, acs, e_s, dec_s = _ssd_scalars(dtr_ref[...], dtb_ref[...], alog_ref[...])
        acs_t = acs.T
        wacs = jnp.zeros((L, LANES), F32)
        wdt = jnp.zeros((L, LANES), F32)
        tri = lax.broadcasted_iota(jnp.int32, (L, L), 0) >= lax.broadcasted_iota(jnp.int32, (L, L), 1)
        rowi = lax.broadcasted_iota(jnp.int32, (L, gw), 0)
        for g in range(G):
            cols = slice(g * gw, (g + 1) * gw)
            ncol = slice(g * STATE, (g + 1) * STATE)

            @pl.when((b == 0) & (j == 0))
            def _():
                dnw_ref[g] = jnp.zeros((SUBLANES, gw), F32)

            @pl.when(j == 0)
            def _():
                dh_ref[g] = jnp.zeros((STATE, gw), F32)

            xs = xs_ref[:, cols].astype(F32)
            bg = b_ref[:, ncol]
            cg = c_ref[:, ncol]
            dt_x, e_x, dec_x = _ssd_group_common(dt_s, e_s, dec_s, e_ref[:, cols])
            xdt = xs * dt_x
            xdt_b = xdt.astype(BF16)
            cb = _dot(cg, bg, NT)
            h = st_ref[0, g]
            hb16 = h.astype(BF16)
            dsk = dsk_ref[:, cols]

            yv = y_ref[:, cols].astype(F32)
            z = z_ref[:, cols].astype(F32)
            sgz = _sigmoid(z)
            sz = z * sgz
            yg = yv * sz
            rs = lax.rsqrt(jnp.mean(yg * yg, axis=-1, keepdims=True) + RMS_EPS)
            yhat = yg * rs
            dyn_v = dyn_ref[:, cols].astype(F32)
            dnw_ref[g] += _colsum(dyn_v * yhat)
            dyh = dyn_v * nw_ref[:, cols]
            dyg = rs * (dyh - yhat * jnp.mean(dyh * yhat, axis=-1, keepdims=True))
            dy = dyg * sz
            dz_ref[:, cols] = (dyg * yv * (sgz * (1.0 + z * (1.0 - sgz)))).astype(dz_ref.dtype)

            dy_b = dy.astype(BF16)
            dcb = jnp.zeros((L, L), F32)
            dxdt_d = jnp.zeros((L, gw), F32)
            ydiag = jnp.zeros((L, gw), F32)
            for r in range(hpg):
                lm = _decay_matrix(acs, acs_t, g * hpg + r, tri)
                m = (cb * lm).astype(BF16)
                hm = _head_mask(r, gw, BF16)
                dyr = dy_b * hm
                xr = xdt_b * hm
                ydiag = ydiag + _dot(m, xr)
                dcb = dcb + _dot(dyr, xdt_b, NT) * lm
                dxdt_d = dxdt_d + _dot(m, dyr, TN)
            dh = dh_ref[g]
            dh16 = dh.astype(BF16)
            xdec_b = (xdt * dec_x).astype(BF16)
            bdh = _dot(bg, dh16)
            dxdt = dxdt_d + dec_x * bdh
            dcb16 = dcb.astype(BF16)
            dye = (dy * e_x).astype(BF16)
            db_ref[:, ncol] = (_dot(dcb16, cg, TN) + _dot(xdec_b, dh16, NT)).astype(db_ref.dtype)
            dc_ref[:, ncol] = (_dot(dcb16, bg) + _dot(dye, hb16, NT)).astype(dc_ref.dtype)
            dprev = _dot(cg, dye, TN)
            cd_row = e_x[L - 1:L, :]
            s_new = _dot(bg, xdec_b, TN)
            last_term = _colsum(dh16.astype(F32) * s_new) + _colsum(dh * h) * cd_row
            yoff = _dot(cg, hb16) * e_x
            wfold = (dy_b.astype(F32) * ydiag + dy * yoff - dxdt_d * xdt_b.astype(F32) - bdh * xdec_b.astype(F32)
                     + jnp.where(rowi == L - 1, last_term, 0.0))
            et = et_ref[cols, :]
            wacs = wacs + _dot_2piece(wfold, et)
            wdt = wdt + _dot_2piece(dxdt * xs, et)
            dsk_acc[...] += _dot_exact01(jnp.broadcast_to(_colsum(dy * xs), (SUBLANES, gw)), et)
            dxs_ref[:, cols] = (dsk * dy + dxdt * dt_x).astype(dxs_ref.dtype)
            dh_ref[g] = dprev + cd_row * dh

        dda = _cumsum_rows(wacs, reverse=True)
        ddt_raw = (wdt + dda * a_row) * _sigmoid(pre)
        ddt_ref[...] = ddt_raw
        dalog_acc[...] += _colsum(dda * dt_s) * a_row
        ddtb_acc[...] += _colsum(ddt_raw)

    def cidx(b, j):
        return b * nc + rc(j)

    accs = lambda shape: pl.BlockSpec(shape, lambda b, j, g: tuple(0 for _ in shape))
    grid = (bl, nc, G // P)
    c_in, c_in_specs, c_out_specs, c_out_shapes = _comm_specs(comm)
    return pl.pallas_call(
        _fuse_comm(body, grid, 14, 9, comm), name="ssd_bwd",
        grid=grid,
        in_specs=[
            pl.BlockSpec((L, P * gw), lambda b, j, g: (cidx(b, j), g)),
            pl.BlockSpec((L, P * STATE), lambda b, j, g: (cidx(b, j), bb0 // P + g)),
            pl.BlockSpec((L, P * STATE), lambda b, j, g: (cidx(b, j), cb0 // P + g)),
            pl.BlockSpec((L, P * gw), lambda b, j, g: (cidx(b, j), zb0 // P + g)),
            pl.BlockSpec((L, LANES), lambda b, j, g: (cidx(b, j), 0)),
            pl.BlockSpec((L, P * gw), lambda b, j, g: (cidx(b, j), g)),
            pl.BlockSpec((L, P * gw), lambda b, j, g: (cidx(b, j), g)),
            pl.BlockSpec((1, P, STATE, gw), lambda b, j, g: (cidx(b, j), g, 0, 0)),
            pl.BlockSpec((1, LANES), lambda b, j, g: (0, 0)),
            pl.BlockSpec((1, LANES), lambda b, j, g: (0, 0)),
            pl.BlockSpec((1, P * gw), lambda b, j, g: (0, g)),
            pl.BlockSpec((1, P * gw), lambda b, j, g: (0, g)),
            pl.BlockSpec((LANES, P * gw), lambda b, j, g: (0, g)),
            pl.BlockSpec((P * gw, LANES), lambda b, j, g: (g, 0)),
        ] + c_in_specs,
        out_specs=[
            pl.BlockSpec((L, P * gw), lambda b, j, g: (cidx(b, j), g)),
            pl.BlockSpec((L, P * STATE), lambda b, j, g: (cidx(b, j), g)),
            pl.BlockSpec((L, P * STATE), lambda b, j, g: (cidx(b, j), g)),
            pl.BlockSpec((L, P * gw), lambda b, j, g: (cidx(b, j), g)),
            pl.BlockSpec((L, LANES), lambda b, j, g: (cidx(b, j), 0)),
            accs((G, SUBLANES, gw)),
            accs((SUBLANES, LANES)),
            accs((SUBLANES, LANES)),
            accs((SUBLANES, LANES)),
        ] + c_out_specs,
        out_shape=[
            jax.ShapeDtypeStruct((t, inner), BF16),
            jax.ShapeDtypeStruct((t, G * STATE), BF16),
            jax.ShapeDtypeStruct((t, G * STATE), BF16),
            jax.ShapeDtypeStruct((t, inner), BF16),
            jax.ShapeDtypeStruct((t, LANES), F32),
            jax.ShapeDtypeStruct((G, SUBLANES, gw), F32),
            jax.ShapeDtypeStruct((SUBLANES, LANES), F32),
            jax.ShapeDtypeStruct((SUBLANES, LANES), F32),
            jax.ShapeDtypeStruct((SUBLANES, LANES), F32),
        ] + c_out_shapes,
        scratch_shapes=[pltpu.VMEM((G, STATE, gw), F32)]
        + (list(comm.scratch) if comm else []),
        compiler_params=_params(("arbitrary", "arbitrary", "arbitrary")),
    )(xbc, xbc, xbc, proj, dt_raw, y, dyn, states, dtb, alog, dskip_x, normw, emat, emat_t, *c_in)


def _pool_window(u, w, anti):
    n = u.shape[0]
    row = lax.broadcasted_iota(jnp.int32, u.shape, 0)
    acc = u
    s = 1
    while s < w:
        if anti:
            acc = acc + jnp.where(row < n - s, pltpu.roll(acc, n - s, 0), 0.0)
        else:
            acc = acc + jnp.where(row >= s, pltpu.roll(acc, s, 0), 0.0)
        s *= 2
    return acc


def _pool_cnt(shape, w):
    row = lax.broadcasted_iota(jnp.int32, shape, 0)
    return jnp.minimum(row + 1, w).astype(F32)


def _pool_fwd(proj, wpg, bl, d, u_col0):
    t = proj.shape[0]
    s = t // bl
    pg = len(POOL_WINDOWS)
    cg = d // pg
    ub0 = u_col0 // d

    def body(u_ref, w_ref, o_ref):
        for gi, w in enumerate(POOL_WINDOWS):
            u = u_ref[:, gi * cg:(gi + 1) * cg].astype(F32)
            pooled = _pool_window(u, w, False) / _pool_cnt(u.shape, w) - u
            o_ref[:, gi * cg:(gi + 1) * cg] = _dot(pooled, w_ref[gi]).astype(o_ref.dtype)

    return pl.pallas_call(
        body, name="pool_fwd",
        grid=(bl,),
        in_specs=[pl.BlockSpec((s, d), lambda b: (b, ub0)), pl.BlockSpec((pg, cg, cg), lambda b: (0, 0, 0))],
        out_specs=pl.BlockSpec((s, d), lambda b: (b, 0)),
        out_shape=jax.ShapeDtypeStruct((t, d), BF16),
        compiler_params=_params(("parallel",)),
    )(proj, wpg)


def _pool_bwd(proj, dyp, wpg, bl, d, u_col0):
    t = proj.shape[0]
    s = t // bl
    pg = len(POOL_WINDOWS)
    cg = d // pg
    ub0 = u_col0 // d

    def body(u_ref, dy_ref, w_ref, du_ref, dw_ref):
        @pl.when(pl.program_id(0) == 0)
        def _():
            dw_ref[...] = jnp.zeros_like(dw_ref)

        for gi, w in enumerate(POOL_WINDOWS):
            u = u_ref[:, gi * cg:(gi + 1) * cg].astype(F32)
            cnt = _pool_cnt(u.shape, w)
            pooled = _pool_window(u, w, False) / cnt - u
            dy = dy_ref[:, gi * cg:(gi + 1) * cg]
            dw_ref[gi] += _dot(pooled, dy, TN)
            dp = _dot(dy, w_ref[gi], NT)
            du_ref[:, gi * cg:(gi + 1) * cg] = (_pool_window(dp / cnt, w, True) - dp).astype(du_ref.dtype)

    return pl.pallas_call(
        body, name="pool_bwd",
        grid=(bl,),
        in_specs=[pl.BlockSpec((s, d), lambda b: (b, ub0)), pl.BlockSpec((s, d), lambda b: (b, 0)),
                  pl.BlockSpec((pg, cg, cg), lambda b: (0, 0, 0))],
        out_specs=[pl.BlockSpec((s, d), lambda b: (b, 0)), pl.BlockSpec((pg, cg, cg), lambda b: (0, 0, 0))],
        out_shape=[jax.ShapeDtypeStruct((t, d), BF16), jax.ShapeDtypeStruct((pg, cg, cg), F32)],
        compiler_params=_params(("arbitrary",)),
    )(proj, dyp, wpg)


def _merge_fwd(proj, ypr, yssd, x, w_out, b_gates, pool_scale, d, lg_col0, tm):
    t = x.shape[0]
    lb0 = lg_col0 // (2 * d)

    def body(lg_ref, yp_ref, ys_ref, x_ref, w_ref, bg_ref, ps_ref, mg_ref, r1_ref):
        lg = lg_ref[...].astype(F32) + bg_ref[...]
        ga = _sigmoid(lg[:, :d])
        gb = _sigmoid(lg[:, d:])
        merged = ga * (yp_ref[...].astype(F32) * ps_ref[...]) + gb * ys_ref[...].astype(F32)
        mg_ref[...] = merged.astype(mg_ref.dtype)
        r1_ref[...] = ALPHA * x_ref[...] + _dot(mg_ref[...], w_ref[...])

    row = lambda w: pl.BlockSpec((tm, w), lambda i: (i, 0))
    full = lambda a: pl.BlockSpec(a.shape, lambda i: (0, 0))
    return pl.pallas_call(
        body, name="merge_fwd",
        grid=(t // tm,),
        in_specs=[pl.BlockSpec((tm, 2 * d), lambda i: (i, lb0)), row(d), row(d), row(d), full(w_out), full(b_gates),
                  full(pool_scale)],
        out_specs=[row(d), row(d)],
        out_shape=[jax.ShapeDtypeStruct((t, d), BF16), jax.ShapeDtypeStruct((t, d), F32)],
        compiler_params=_params(("parallel",)),
    )(proj, ypr, yssd, x, w_out, b_gates, pool_scale)


def _merge_bwd(dr1, proj, ypr, yssd, w_out, b_gates, pool_scale, d, lg_col0, tm):
    t = dr1.shape[0]
    lb0 = lg_col0 // (2 * d)

    def body(dr_ref, lg_ref, yp_ref, ys_ref, w_ref, bg_ref, ps_ref, dlg_ref, dyp_ref, dys_ref, dbg_ref, dps_ref):
        @pl.when(pl.program_id(0) == 0)
        def _():
            dbg_ref[...] = jnp.zeros_like(dbg_ref)
            dps_ref[...] = jnp.zeros_like(dps_ref)

        dm = _dot(dr_ref[...], w_ref[...], NT)
        lg = lg_ref[...].astype(F32) + bg_ref[...]
        ga = _sigmoid(lg[:, :d])
        gb = _sigmoid(lg[:, d:])
        ypr_v = yp_ref[...].astype(F32)
        ys_v = ys_ref[...].astype(F32)
        ps = ps_ref[...]
        dga = dm * ypr_v * ps
        dla = dga * ga * (1.0 - ga)
        dlb = dm * ys_v * gb * (1.0 - gb)
        dlg_ref[:, :d] = dla.astype(dlg_ref.dtype)
        dlg_ref[:, d:] = dlb.astype(dlg_ref.dtype)
        dyp_ref[...] = (dm * ga * ps).astype(dyp_ref.dtype)
        dys_ref[...] = (dm * gb).astype(dys_ref.dtype)
        dbg_ref[0:1, :d] += _colsum(dla)
        dbg_ref[0:1, d:] += _colsum(dlb)
        dps_ref[0:1, :] += _colsum(dm * ga * ypr_v)

    row = lambda w: pl.BlockSpec((tm, w), lambda i: (i, 0))
    full = lambda a: pl.BlockSpec(a.shape, lambda i: (0, 0))
    acc = lambda w: pl.BlockSpec((SUBLANES, w), lambda i: (0, 0))
    return pl.pallas_call(
        body, name="merge_bwd",
        grid=(t // tm,),
        in_specs=[row(d), pl.BlockSpec((tm, 2 * d), lambda i: (i, lb0)), row(d), row(d), full(w_out), full(b_gates),
                  full(pool_scale)],
        out_specs=[row(2 * d), row(d), row(d), acc(2 * d), acc(d)],
        out_shape=[jax.ShapeDtypeStruct((t, 2 * d), BF16), jax.ShapeDtypeStruct((t, d), BF16),
                   jax.ShapeDtypeStruct((t, d), BF16), jax.ShapeDtypeStruct((SUBLANES, 2 * d), F32),
                   jax.ShapeDtypeStruct((SUBLANES, d), F32)],
        compiler_params=_params(("arbitrary",)),
    )(dr1, proj, ypr, yssd, w_out, b_gates, pool_scale)


MLP_SLABS_PER_STEP = 2


def _mlp_fwd(r1, target, w_up, w_down, ln1_g, ln1_b, ln2_g, ln2_b, tm):
    t, d = r1.shape
    ns, _, sw = w_up.shape
    spb = MLP_SLABS_PER_STEP
    assert ns % spb == 0
    nf, tf, ff = ns // spb, spb * sw, ns * sw

    def body(r1_ref, tg_ref, wu_ref, wd_ref, g1_ref, b1_ref, g2_ref, b2_ref,
             up_ref, h1_ref, dr2_ref, loss_ref, dg2_ref, db2_ref, h1f, acc):
        i = pl.program_id(0)
        f = pl.program_id(1)

        @pl.when((i == 0) & (f == 0))
        def _():
            loss_ref[...] = jnp.zeros_like(loss_ref)
            dg2_ref[...] = jnp.zeros_like(dg2_ref)
            db2_ref[...] = jnp.zeros_like(db2_ref)

        @pl.when(f == 0)
        def _():
            xhat, _ = _ln_fwd(r1_ref[...])
            h1 = xhat * g1_ref[...] + b1_ref[...]
            h1f[...] = h1
            h1_ref[...] = h1.astype(h1_ref.dtype)
            acc[...] = jnp.zeros_like(acc)

        for s in range(spb):
            up_ref[:, s * sw:(s + 1) * sw] = _dot(h1_ref[...], wu_ref[s]).astype(up_ref.dtype)
        upq = jnp.maximum(up_ref[...].astype(F32), 0.0)
        acc[...] += _dot(upq * upq, wd_ref[...])

        @pl.when(f == nf - 1)
        def _():
            xhat, rstd = _ln_fwd(ALPHA * h1f[...] + acc[...])
            g2 = g2_ref[...]
            diff = xhat * g2 + b2_ref[...] - tg_ref[...]
            loss_ref[...] += 0.5 / d * jnp.sum(diff * diff)
            dh2 = diff * (1.0 / d)
            dg2_ref[0:1, :] += _colsum(dh2 * xhat)
            db2_ref[0:1, :] += _colsum(dh2)
            dr2_ref[...] = _ln_bwd(dh2, xhat, rstd, g2).astype(dr2_ref.dtype)

    row = pl.BlockSpec((tm, d), lambda i, f: (i, 0))
    vec = pl.BlockSpec((1, d), lambda i, f: (0, 0))
    acc8 = pl.BlockSpec((SUBLANES, d), lambda i, f: (0, 0))
    return pl.pallas_call(
        body, name="mlp_fwd",
        grid=(t // tm, nf),
        in_specs=[row, row, pl.BlockSpec((spb, d, sw), lambda i, f: (f, 0, 0)), pl.BlockSpec((tf, d), lambda i, f: (f, 0)),
                  vec, vec, vec, vec],
        out_specs=[pl.BlockSpec((tm, tf), lambda i, f: (i, f)), row, row,
                   pl.BlockSpec((SUBLANES, LANES), lambda i, f: (0, 0)), acc8, acc8],
        out_shape=[jax.ShapeDtypeStruct((t, ff), BF16), jax.ShapeDtypeStruct((t, d), BF16),
                   jax.ShapeDtypeStruct((t, d), BF16), jax.ShapeDtypeStruct((SUBLANES, LANES), F32),
                   jax.ShapeDtypeStruct((SUBLANES, d), F32), jax.ShapeDtypeStruct((SUBLANES, d), F32)],
        scratch_shapes=[pltpu.VMEM((tm, d), F32), pltpu.VMEM((tm, d), F32)],
        compiler_params=_params(("arbitrary", "arbitrary")),
    )(r1, target, w_up, w_down, ln1_g, ln1_b, ln2_g, ln2_b)


def _mlp_bwd(dr2, up, r1, w_up, w_down, ln1_g, tm):
    t, d = r1.shape
    ns, _, sw = w_up.shape
    spb = MLP_SLABS_PER_STEP
    assert ns % spb == 0
    nf, tf, ff = ns // spb, spb * sw, ns * sw

    def body(dr2_ref, up_ref, r1_ref, wu_ref, wd_ref, g1_ref, dup_ref, dr1_ref, dg1_ref, db1_ref, acc):
        i = pl.program_id(0)
        f = pl.program_id(1)

        @pl.when((i == 0) & (f == 0))
        def _():
            dg1_ref[...] = jnp.zeros_like(dg1_ref)
            db1_ref[...] = jnp.zeros_like(db1_ref)

        @pl.when(f == 0)
        def _():
            acc[...] = jnp.zeros_like(acc)

        dact = _dot(dr2_ref[...], wd_ref[...], NT)
        dup_ref[...] = (dact * 2.0 * jnp.maximum(up_ref[...].astype(F32), 0.0)).astype(dup_ref.dtype)
        for s in range(spb):
            acc[...] += _dot(dup_ref[:, s * sw:(s + 1) * sw], wu_ref[s], NT)

        @pl.when(f == nf - 1)
        def _():
            dh1 = acc[...] + ALPHA * dr2_ref[...].astype(F32)
            xhat, rstd = _ln_fwd(r1_ref[...])
            dg1_ref[0:1, :] += _colsum(dh1 * xhat)
            db1_ref[0:1, :] += _colsum(dh1)
            dr1_ref[...] = _ln_bwd(dh1, xhat, rstd, g1_ref[...]).astype(dr1_ref.dtype)

    row = pl.BlockSpec((tm, d), lambda i, f: (i, 0))
    acc8 = pl.BlockSpec((SUBLANES, d), lambda i, f: (0, 0))
    return pl.pallas_call(
        body, name="mlp_bwd",
        grid=(t // tm, nf),
        in_specs=[row, pl.BlockSpec((tm, tf), lambda i, f: (i, f)), row,
                  pl.BlockSpec((spb, d, sw), lambda i, f: (f, 0, 0)), pl.BlockSpec((tf, d), lambda i, f: (f, 0)),
                  pl.BlockSpec((1, d), lambda i, f: (0, 0))],
        out_specs=[pl.BlockSpec((tm, tf), lambda i, f: (i, f)), row, acc8, acc8],
        out_shape=[jax.ShapeDtypeStruct((t, ff), BF16), jax.ShapeDtypeStruct((t, d), BF16),
                   jax.ShapeDtypeStruct((SUBLANES, d), F32), jax.ShapeDtypeStruct((SUBLANES, d), F32)],
        scratch_shapes=[pltpu.VMEM((tm, d), F32)],
        compiler_params=_params(("arbitrary", "arbitrary")),
    )(dr2, up, r1, w_up, w_down, ln1_g)


def _dx_kernel(segs, w_main, ddt, w_dt, dr1, tm, tk, comm=None):
    t, d = dr1.shape
    nblk = [s.shape[1] // tk for s in segs]
    starts = [sum(nblk[:i]) for i in range(len(segs))]
    nk = sum(nblk)
    nseg = len(segs)

    def body(*refs):
        seg_refs = refs[:nseg]
        w_ref, ddt_ref, wdt_ref, dr1_ref, o_ref, acc = refs[nseg:]
        k = pl.program_id(1)

        @pl.when(k == 0)
        def _():
            acc[...] = ALPHA * dr1_ref[...].astype(F32) + _dot(ddt_ref[...], wdt_ref[...], NT)

        for si in range(nseg):
            @pl.when((k >= starts[si]) & (k < starts[si] + nblk[si]))
            def _(si=si):
                acc[...] += _dot(seg_refs[si][...], w_ref[...], NT)

        @pl.when(k == nk - 1)
        def _():
            o_ref[...] = acc[...]

    def seg_spec(si):
        return pl.BlockSpec((tm, tk), lambda i, k: (i, jnp.clip(k - starts[si], 0, nblk[si] - 1)))

    row = pl.BlockSpec((tm, d), lambda i, k: (i, 0))
    grid = (t // tm, nk)
    c_in, c_in_specs, c_out_specs, c_out_shapes = _comm_specs(comm)
    return pl.pallas_call(
        _fuse_comm(body, grid, nseg + 4, 1, comm), name="dx",
        grid=grid,
        in_specs=[seg_spec(si) for si in range(nseg)] + [
            pl.BlockSpec((d, tk), lambda i, k: (0, k)), pl.BlockSpec((tm, LANES), lambda i, k: (i, 0)),
            pl.BlockSpec((d, LANES), lambda i, k: (0, 0)), row] + c_in_specs,
        out_specs=[row] + c_out_specs,
        out_shape=[jax.ShapeDtypeStruct((t, d), F32)] + c_out_shapes,
        scratch_shapes=[pltpu.VMEM((tm, d), F32)] + (list(comm.scratch) if comm else []),
        compiler_params=_params(("arbitrary", "arbitrary")),
    )(*segs, w_main, ddt, w_dt, dr1, *c_in)


def _dims(d):
    inner = 2 * d
    heads = inner // HEAD_DIM
    cd = inner + 2 * GROUPS * STATE
    assert heads <= LANES and inner % (GROUPS * LANES) == 0 and d % (len(POOL_WINDOWS) * LANES) == 0
    o_z, o_xbc, o_dt, o_lg = d, d + inner, d + inner + cd, d + inner + cd + heads
    return inner, heads, cd, (o_z, o_xbc, o_dt, o_lg)


def _row(v, width=None):
    v = v.reshape(1, -1).astype(F32)
    if width is not None and v.shape[1] < width:
        v = jnp.pad(v, ((0, 0), (0, width - v.shape[1])))
    return v


def _local_step(x2, tgt2, w, shards, core, bl):
    t, d = x2.shape
    inner, heads, cd, _ = _dims(d)
    gs = GROUPS * STATE
    nc = t // bl // CHUNK
    w_main, w_dt = _w_in_internal(w["w_in_blocks"], d)
    c_z, c_lg, c_u = cd, cd + inner, cd + inner + 2 * d
    conv_w8 = jnp.pad(w["conv_w"].astype(F32), ((0, SUBLANES - CONV_K), (0, 0)))
    conv_b = _row(w["conv_b"])
    dtb, alog = _row(w["dt_bias"], LANES), _row(w["a_log"], LANES)
    dskip_x = _row(jnp.repeat(w["d_skip"].reshape(-1), HEAD_DIM))
    normw = _row(w["ssd_norm_w"])
    col_head = lax.broadcasted_iota(jnp.int32, (LANES, inner), 1) // HEAD_DIM
    emat = (col_head == lax.broadcasted_iota(jnp.int32, (LANES, inner), 0)).astype(BF16)
    emat_t = emat.T
    w_main, w_dt = w_main.astype(BF16), w_dt.astype(BF16)
    b_gates, pool_scale = _row(w["b_gates"]), _row(w["pool_scale"])
    ln1_g, ln1_b, ln2_g, ln2_b = _row(w["ln1_g"]), _row(w["ln1_b"]), _row(w["ln2_g"]), _row(w["ln2_b"])

    tm = min(512, t)
    tk = min(1024, d)
    ct = min(512, d)
    rt = min(512, t // bl)
    nct = t // bl // rt
    mm = functools.partial(_matmul, bm=1024, bn=tk, bk=1024)
    mmt = functools.partial(_matmul, bm=1024, bn=tk, bk=2048)
    xb = x2.astype(BF16)

    proj, xbc, dsl, *gathered = _in_proj(xb, w_main, conv_w8, conv_b, cd, t // bl, 1024, tk,
                                    _all_gather_comm([shards[n] for n in OTHERS]))
    gathered = dict(zip(OTHERS, gathered))
    w_ssd, w_out, w_down = (gathered[n].reshape(-1, d) for n in ("w_ssd_proj", "w_out", "w_down"))
    w_up = gathered["w_up"]
    npg = len(POOL_WINDOWS)
    cg = d // npg
    wpg = gathered["w_pool_group"].reshape(N_DEV, npg, cg // N_DEV, cg).transpose(1, 0, 2, 3).reshape(npg, cg, cg)
    dt_raw = mm(xb, w_dt, "nn", F32, name="in_proj_dt")
    y, yn, states = _ssd_fwd(xbc, proj, dt_raw, dtb, alog, dskip_x, normw, emat, bl, inner, c_z)
    yssd = mmt(yn, w_ssd, "nn", BF16, name="ssd_proj")
    ypr = _pool_fwd(proj, wpg, bl, d, c_u)
    merged, r1 = _merge_fwd(proj, ypr, yssd, x2, w_out, b_gates, pool_scale, d, c_lg, tm)
    tmm = min(1024, t)
    up, h1, dr2, loss8, dg2, db2 = _mlp_fwd(r1, tgt2, w_up, w_down, ln1_g, ln1_b, ln2_g, ln2_b, tmm)

    dup, dr1, dg1, db1 = _mlp_bwd(dr2, up, r1, w_up, w_down, ln1_g, tmm)
    relu2 = lambda v: jnp.square(jnp.maximum(v, 0.0))
    g = {}
    g["w_down"] = mmt(up, dr2, "tn", BF16, name="dw_down", a_fn=relu2)
    g["w_up"] = mmt(h1, dup, "tn", BF16, name="dw_up", col_blocks=N_DEV)
    g["w_out"] = mmt(merged, dr1, "tn", BF16, name="dw_out")
    dlg, dyp, dys, dbg, dps = _merge_bwd(dr1, proj, ypr, yssd, w_out, b_gates, pool_scale, d, c_lg, tm)
    du, dwpg = _pool_bwd(proj, dyp, wpg, bl, d, c_u)
    g["w_pool_group"] = dwpg.reshape(npg, N_DEV, cg // N_DEV, cg).transpose(1, 0, 2, 3).reshape(
        N_DEV, npg * cg // N_DEV, cg).astype(BF16)
    dyn = mm(dys, w_ssd, "nt", BF16, name="d_ssd_proj")
    g["w_ssd_proj"] = mmt(yn, dys, "tn", BF16, name="dw_ssd_proj")

    def chip_sums(names, tag):
        parts = [g.pop(n).reshape((N_DEV,) + shards_2d[n]) for n in names]
        recv = _run_comm(_rs_sibling_comm(parts), "rs_sibling_" + tag)
        return [_add_pairs(core, p, r, "rs_add_" + n) for n, p, r in zip(names, parts, recv)]

    shards_2d = {n: s.shape for n, s in shards.items()}
    shards_2d["w_in"] = w["w_in_blocks"].shape[1:]
    dxs, dbm, dcm, dz, ddt, dnw, dsk, dalog, ddtb, *recv_others = _ssd_bwd(
        xbc, proj, dt_raw, y, dyn, states, dtb, alog, dskip_x, normw, emat, emat_t, bl, inner, c_z,
        comm=_rs_chips_comm(chip_sums(OTHERS, "a")))
    dxs_p, dcw_x, dcb_x = _conv_bwd(proj, dsl, dxs, conv_w8, nct, 0, inner, ct, rt, "conv_bwd_x")
    dbm_p, dcw_b, dcb_b = _conv_bwd(proj, dsl, dbm, conv_w8, nct, inner, gs, ct, rt, "conv_bwd_b")
    dcm_p, dcw_c, dcb_c = _conv_bwd(proj, dsl, dcm, conv_w8, nct, inner + gs, gs, ct, rt, "conv_bwd_c")
    segs = [dxs_p, dbm_p, dcm_p, dz, dlg, du]
    keys = [k for k, _, _ in _col_segments(d)]
    dws = {k: mmt(xb, s, "tn", BF16, name="dw_in_" + k) for k, s in zip(keys, segs + [ddt])}
    g["w_in"] = _w_in_grad_blocks(dws, d, w["w_in_blocks"].shape[2])
    grad_x, recv_w_in = _dx_kernel(segs, w_main, ddt, w_dt, dr1, tmm, tk,
                                   comm=_rs_chips_comm(chip_sums(["w_in"], "b")))
    recv = dict(zip(OTHERS, recv_others))
    recv["w_in"] = recv_w_in
    g["conv_w"] = jnp.concatenate([dcw_x, dcw_b, dcw_c], axis=1)[:CONV_K]
    g["conv_b"] = jnp.concatenate([dcb_x, dcb_b, dcb_c], axis=1)[0]
    g["b_gates"], g["pool_scale"] = dbg[0], dps[0]
    g["dt_bias"], g["a_log"], g["d_skip"] = ddtb[0, :heads], dalog[0, :heads], dsk[0, :heads]
    g["ssd_norm_w"] = dnw[:, 0, :].reshape(inner)
    g["ln1_g"], g["ln1_b"], g["ln2_g"], g["ln2_b"] = dg1[0], db1[0], dg2[0], db2[0]
    return loss8, grad_x, g, recv


BIG = ("w_in", "w_ssd_proj", "w_pool_group", "w_out", "w_up", "w_down")
OTHERS = BIG[1:]
SMALL = ("b_gates", "conv_b", "dt_bias", "a_log", "d_skip", "ssd_norm_w", "pool_scale", "ln1_g", "ln1_b", "ln2_g",
         "ln2_b")
SMALL_PACK = SMALL + ("conv_w",)
NAMES = ("w_in", "b_gates", "conv_w", "conv_b", "dt_bias", "a_log", "d_skip", "ssd_norm_w", "w_ssd_proj",
         "w_pool_group", "pool_scale", "w_out", "ln1_g", "ln1_b", "w_up", "w_down", "ln2_g", "ln2_b")


def _size(shape):
    n = 1
    for s in shape:
        n *= s
    return n


def _rows128(v):
    v = v.astype(F32).reshape((-1, v.shape[-1]))
    n = v.shape[-1]
    v = jnp.pad(v, ((0, 0), (0, -n % LANES)))
    return v.reshape(-1, LANES)


def _pack_small(vals, extra):
    parts = [_rows128(vals[n]) for n in SMALL_PACK]
    parts.append(jnp.pad(extra.reshape(1, 1).astype(F32), ((0, 0), (0, LANES - 1))))
    buf = jnp.concatenate(parts, axis=0)
    return jnp.pad(buf, ((0, -buf.shape[0] % SUBLANES), (0, 0)))


def _unpack_small(buf, shapes):
    out, off = {}, 0
    for n in SMALL_PACK:
        lead, last = _size(shapes[n][:-1]), shapes[n][-1]
        per = -(-last // LANES)
        out[n] = buf[off:off + lead * per].reshape(lead, per * LANES)[:, :last].reshape(shapes[n])
        off += lead * per
    return out, buf[off, 0]


def _col_segments(d):
    inner, heads, cd, (o_z, o_xbc, o_dt, o_lg) = _dims(d)
    gs = GROUPS * STATE
    return [("xs", o_xbc, inner), ("B", o_xbc + inner, gs), ("C", o_xbc + inner + gs, gs), ("z", o_z, inner),
            ("lg", o_lg, 2 * d), ("u", 0, d), ("dt", o_dt, heads)]


def _cols_from_blocks(blocks, start, width, bw):
    parts, pos = [], start
    while pos < start + width:
        k, off = divmod(pos, bw)
        n = min(bw - off, start + width - pos)
        parts.append(blocks[k][:, off:off + n])
        pos += n
    return parts


def _w_in_internal(blocks, d):
    bw = blocks.shape[2]
    segs = _col_segments(d)
    heads = segs[-1][2]
    main = [p for _, s, w_ in segs[:-1] for p in _cols_from_blocks(blocks, s, w_, bw)]
    w_dt = jnp.concatenate(_cols_from_blocks(blocks, segs[-1][1], heads, bw), axis=1)
    return jnp.concatenate(main, axis=1), jnp.pad(w_dt, ((0, 0), (0, LANES - heads)))


def _w_in_grad_blocks(dws, d, bw):
    order = sorted(_col_segments(d), key=lambda s: s[1])
    blocks = []
    for k in range(N_DEV):
        lo, hi, parts = k * bw, (k + 1) * bw, []
        for key, s, w_ in order:
            a, b = max(lo, s), min(hi, s + w_)
            if a < b:
                parts.append(dws[key][:, a - s:b - s])
        blocks.append(jnp.concatenate(parts, axis=1))
    return jnp.stack(blocks)


def _mesh_pos():
    return lax.axis_index("x"), lax.axis_index("y"), lax.axis_index("c")


def _all_gather_comm(shards):
    nw = len(shards)

    def setup(x_refs, out_refs, scr):
        send_sems, recv_sems, local_sems = scr
        x, y, c = _mesh_pos()
        me, sibling = (x, y, c), (x, y, 1 - c)
        chips = [(1 - x, y), (x, 1 - y), (1 - x, 1 - y)]

        def copy(wi, k, block, to, from_input=False):
            px, py, pc = block
            blk = out_refs[wi].at[4 * px + 2 * py + pc]
            return pltpu.make_async_remote_copy(
                src_ref=x_refs[wi] if from_input else blk, dst_ref=blk,
                send_sem=send_sems.at[7 * wi + k], recv_sem=recv_sems.at[7 * wi + k], device_id=to,
                device_id_type=MESH)

        mine = [pltpu.make_async_copy(x_refs[wi], out_refs[wi].at[4 * x + 2 * y + c], local_sems.at[wi])
                for wi in range(nw)]
        sends = []
        for wi in range(nw):
            sends.append(copy(wi, 0, me, sibling, True))
            sends += [copy(wi, 1 + j, me, (*chip, c), True) for j, chip in enumerate(chips)]
        return copy, mine, sends, me, sibling, chips, c

    def start(x_refs, out_refs, scr):
        _, mine, sends, _, _, _, _ = setup(x_refs, out_refs, scr)
        for cp in mine + sends:
            cp.start()

    def wait(x_refs, out_refs, scr):
        copy, mine, sends, me, sibling, chips, c = setup(x_refs, out_refs, scr)
        passed = []
        for wi in range(nw):
            for j, chip in enumerate(chips):
                copy(wi, 1 + j, (*chip, c), me).wait_recv()
                passed.append(copy(wi, 4 + j, (*chip, c), sibling))
                passed[-1].start()
        for wi in range(nw):
            copy(wi, 0, sibling, me).wait_recv()
            for j, chip in enumerate(chips):
                copy(wi, 4 + j, (*chip, 1 - c), me).wait_recv()
        for cp in sends + passed:
            cp.wait_send()
        for cp in mine:
            cp.wait()

    return _Comm(
        inputs=list(shards),
        out_shapes=[jax.ShapeDtypeStruct((N_DEV,) + s.shape, s.dtype) for s in shards],
        scratch=[pltpu.SemaphoreType.DMA((7 * nw,)), pltpu.SemaphoreType.DMA((7 * nw,)),
                 pltpu.SemaphoreType.DMA((nw,))],
        start=start, wait=wait)


def _rs_sibling_comm(parts):
    nw = len(parts)
    half = N_DEV // 2

    def copies(p_refs, recv_refs, scr):
        send_sems, recv_sems = scr
        x, y, c = _mesh_pos()
        return [pltpu.make_async_remote_copy(
            src_ref=p_refs[wi].at[2 * q + 1 - c], dst_ref=recv_refs[wi].at[q],
            send_sem=send_sems.at[half * wi + q], recv_sem=recv_sems.at[half * wi + q],
            device_id=(x, y, 1 - c), device_id_type=MESH) for wi in range(nw) for q in range(half)]

    def start(p_refs, recv_refs, scr):
        for cp in copies(p_refs, recv_refs, scr):
            cp.start()

    def wait(p_refs, recv_refs, scr):
        for cp in copies(p_refs, recv_refs, scr):
            cp.wait()

    return _Comm(
        inputs=list(parts),
        out_shapes=[jax.ShapeDtypeStruct((half,) + p.shape[1:], p.dtype) for p in parts],
        scratch=[pltpu.SemaphoreType.DMA((half * nw,)), pltpu.SemaphoreType.DMA((half * nw,))],
        start=start, wait=wait)


def _rs_chips_comm(tbs):
    nw = len(tbs)

    def copies(t_refs, o_refs, scr):
        send_sems, recv_sems, local_sems = scr
        x, y, c = _mesh_pos()
        p = 2 * x + y
        chips = [(1 - x, y), (x, 1 - y), (1 - x, 1 - y)]
        own = [pltpu.make_async_copy(t_refs[wi].at[p], o_refs[wi].at[p], local_sems.at[wi]) for wi in range(nw)]
        remote = [pltpu.make_async_remote_copy(
            src_ref=t_refs[wi].at[2 * qx + qy], dst_ref=o_refs[wi].at[p], send_sem=send_sems.at[3 * wi + j],
            recv_sem=recv_sems.at[3 * wi + j], device_id=(qx, qy, c), device_id_type=MESH)
            for wi in range(nw) for j, (qx, qy) in enumerate(chips)]
        arriving = [pltpu.make_async_remote_copy(
            src_ref=t_refs[wi].at[p], dst_ref=o_refs[wi].at[2 * qx + qy], send_sem=send_sems.at[3 * wi + j],
            recv_sem=recv_sems.at[3 * wi + j], device_id=(qx, qy, c), device_id_type=MESH)
            for wi in range(nw) for j, (qx, qy) in enumerate(chips)]
        return own, remote, arriving

    def start(t_refs, o_refs, scr):
        own, remote, _ = copies(t_refs, o_refs, scr)
        for cp in own + remote:
            cp.start()

    def wait(t_refs, o_refs, scr):
        own, remote, arriving = copies(t_refs, o_refs, scr)
        for cp in arriving:
            cp.wait_recv()
        for cp in remote:
            cp.wait_send()
        for cp in own:
            cp.wait()

    return _Comm(
        inputs=list(tbs),
        out_shapes=[jax.ShapeDtypeStruct(t_.shape, t_.dtype) for t_ in tbs],
        scratch=[pltpu.SemaphoreType.DMA((3 * nw,)), pltpu.SemaphoreType.DMA((3 * nw,)),
                 pltpu.SemaphoreType.DMA((nw,))],
        start=start, wait=wait)


def _row_tile(rows, cap=256):
    if rows <= cap:
        return rows
    return max(t_ for t_ in range(SUBLANES, cap + 1, SUBLANES) if rows % t_ == 0)


def _add_pairs(core, part, recv, name):
    n, r, c_ = recv.shape
    tr = _row_tile(r)

    def body(core_ref, a_ref, b_ref, o_ref):
        o_ref[...] = (a_ref[...].astype(F32) + b_ref[...].astype(F32)).astype(o_ref.dtype)

    spec = pl.BlockSpec((1, tr, c_), lambda q, i, core_ref: (q, i, 0))
    return pl.pallas_call(
        body, name=name,
        grid_spec=pltpu.PrefetchScalarGridSpec(
            num_scalar_prefetch=1, grid=(n, r // tr),
            in_specs=[pl.BlockSpec((1, tr, c_), lambda q, i, core_ref: (2 * q + core_ref[0], i, 0)), spec],
            out_specs=spec),
        out_shape=jax.ShapeDtypeStruct(recv.shape, BF16), compiler_params=_params(("parallel", "parallel")),
    )(core, part, recv)


def _small_allreduce(vec, name):
    rows = vec.shape[0]

    def body(x_ref, o_ref, buf, send_sems, recv_sems):
        x, y, c = _mesh_pos()
        me = 4 * x + 2 * y + c
        buf[me] = x_ref[...]
        cps = []
        for k in range(1, N_DEV):
            peer = (1 - x if k & 4 else x, 1 - y if k & 2 else y, 1 - c if k & 1 else c)
            cps.append(pltpu.make_async_remote_copy(
                src_ref=x_ref, dst_ref=buf.at[me], send_sem=send_sems.at[k - 1], recv_sem=recv_sems.at[k - 1],
                device_id=peer, device_id_type=MESH))
        for cp in cps:
            cp.start()
        for k in range(1, N_DEV):
            px, py, pc = (1 - x if k & 4 else x, 1 - y if k & 2 else y, 1 - c if k & 1 else c)
            pltpu.make_async_remote_copy(
                src_ref=x_ref, dst_ref=buf.at[4 * px + 2 * py + pc], send_sem=send_sems.at[k - 1],
                recv_sem=recv_sems.at[k - 1], device_id=(px, py, pc), device_id_type=MESH).wait_recv()
        for cp in cps:
            cp.wait_send()
        acc = buf[0]
        for k in range(1, N_DEV):
            acc = acc + buf[k]
        o_ref[...] = acc

    vm = pl.BlockSpec(memory_space=pltpu.VMEM)
    return pl.pallas_call(
        body, name=name,
        in_specs=[vm], out_specs=vm,
        out_shape=jax.ShapeDtypeStruct(vec.shape, F32),
        scratch_shapes=[pltpu.VMEM((N_DEV, rows, LANES), F32), pltpu.SemaphoreType.DMA((N_DEV - 1,)),
                        pltpu.SemaphoreType.DMA((N_DEV - 1,))],
    )(vec)


def _adamw(gparts, w, m, v, name):
    n, r, c_ = gparts.shape
    tr = _row_tile(r)
    c1 = 1.0 / (1.0 - B1 ** STEP)
    c2 = 1.0 / (1.0 - B2 ** STEP)

    def body(g_ref, w_ref, m_ref, v_ref, go_ref, d_ref, mo_ref, vo_ref):
        g = g_ref[0].astype(F32)
        for q in range(1, n):
            g = g + g_ref[q].astype(F32)
        mn = B1 * m_ref[...] + (1.0 - B1) * g
        vn = B2 * v_ref[...] + (1.0 - B2) * (g * g)
        go_ref[...] = g
        mo_ref[...] = mn
        vo_ref[...] = vn
        d_ref[...] = -LR * ((mn * c1) / (jnp.sqrt(vn * c2) + ADAM_EPS) + WD * w_ref[...])

    spec = pl.BlockSpec((tr, c_), lambda i: (i, 0))
    out = jax.ShapeDtypeStruct((r, c_), F32)
    return pl.pallas_call(
        body, name=name, grid=(r // tr,),
        in_specs=[pl.BlockSpec((n, tr, c_), lambda i: (0, i, 0)), spec, spec, spec],
        out_specs=[spec] * 4, out_shape=[out] * 4, compiler_params=_params(("parallel",)),
    )(gparts, w, m, v)


def kernel(x, w_in, b_gates, conv_w, conv_b, dt_bias, a_log, d_skip, ssd_norm_w, w_ssd_proj, w_pool_group, pool_scale, w_out, ln1_g, ln1_b, w_up, w_down, ln2_g, ln2_b, loss_target, m_w_in, m_b_gates, m_conv_w, m_conv_b, m_dt_bias, m_a_log, m_d_skip, m_ssd_norm_w, m_w_ssd_proj, m_w_pool_group, m_pool_scale, m_w_out, m_ln1_g, m_ln1_b, m_w_up, m_w_down, m_ln2_g, m_ln2_b, v_w_in, v_b_gates, v_conv_w, v_conv_b, v_dt_bias, v_a_log, v_d_skip, v_ssd_norm_w, v_w_ssd_proj, v_w_pool_group, v_pool_scale, v_w_out, v_ln1_g, v_ln1_b, v_w_up, v_w_down, v_ln2_g, v_ln2_b):
    ws = (w_in, b_gates, conv_w, conv_b, dt_bias, a_log, d_skip, ssd_norm_w, w_ssd_proj, w_pool_group, pool_scale,
          w_out, ln1_g, ln1_b, w_up, w_down, ln2_g, ln2_b)
    ms = (m_w_in, m_b_gates, m_conv_w, m_conv_b, m_dt_bias, m_a_log, m_d_skip, m_ssd_norm_w, m_w_ssd_proj,
          m_w_pool_group, m_pool_scale, m_w_out, m_ln1_g, m_ln1_b, m_w_up, m_w_down, m_ln2_g, m_ln2_b)
    vs = (v_w_in, v_b_gates, v_conv_w, v_conv_b, v_dt_bias, v_a_log, v_d_skip, v_ssd_norm_w, v_w_ssd_proj,
          v_w_pool_group, v_pool_scale, v_w_out, v_ln1_g, v_ln1_b, v_w_up, v_w_down, v_ln2_g, v_ln2_b)
    w = {n: a[0] for n, a in zip(NAMES, ws)}
    m = {n: a[0] for n, a in zip(NAMES, ms)}
    v = {n: a[0] for n, a in zip(NAMES, vs)}
    out_shapes = {n: a.shape for n, a in zip(NAMES, ws)}
    bl, s, d = x.shape
    x2, tgt2 = x.reshape(bl * s, d), loss_target.reshape(bl * s, d)
    xi, yi, ci = _mesh_pos()
    me = 4 * xi + 2 * yi + ci
    zero = jnp.zeros((), F32)
    shapes = {n: w[n].shape for n in NAMES}
    shape2d = {n: (_size(shapes[n][:-1]), shapes[n][-1]) for n in BIG}
    cwl = shapes["conv_w"][1]

    shards = {n: w[n].astype(BF16).reshape(shape2d[n]) for n in BIG}
    full = {n: w[n] for n in SMALL}
    full["w_in_blocks"], conv_blocks = _run_comm(_all_gather_comm([shards.pop("w_in"), w["conv_w"]]),
                                                 "all_gather_w_in")
    full["conv_w"] = conv_blocks.transpose(1, 0, 2).reshape(CONV_K, N_DEV * cwl)
    loss8, grad_x, g, recv = _local_step(x2, tgt2, full, shards, ci.astype(jnp.int32).reshape(1), bl)

    small_sum = _small_allreduce(_pack_small(g, loss8[0, 0]), "small_allreduce")
    ex_shapes = {n: shapes[n] for n in SMALL}
    ex_shapes["conv_w"] = (CONV_K, N_DEV * cwl)
    gsum, loss = _unpack_small(small_sum, ex_shapes)
    gsum["conv_w"] = lax.dynamic_slice(gsum["conv_w"], (0, me * cwl), (CONV_K, cwl))
    gs_pk = _pack_small(gsum, zero)
    ws_pk, ms_pk, vs_pk = (_pack_small(t_, zero) for t_ in (w, m, v))
    small_out = _adamw(gs_pk[None], ws_pk, ms_pk, vs_pk, "adamw_small")
    loc_shapes = {n: shapes[n] for n in SMALL_PACK}
    res = [_unpack_small(o, loc_shapes)[0] for o in small_out]

    for n in BIG:
        outs = _adamw(recv[n], *(t_[n].reshape(shape2d[n]) for t_ in (w, m, v)), "adamw_" + n)
        for r_, o in zip(res, outs):
            r_[n] = o

    def ordered(r_):
        return [r_[n].reshape(out_shapes[n]) for n in NAMES]

    return (loss, grad_x.reshape(bl, s, d), *ordered(res[0]), *ordered(res[1]), *ordered(res[2]), *ordered(res[3]))
```

```python
import collections
import functools

import jax
import jax.numpy as jnp
from jax import lax
from jax.experimental import pallas as pl
from jax.experimental.pallas import tpu as pltpu

F32 = jnp.float32
BF16 = jnp.bfloat16
MESH = pl.DeviceIdType.MESH

HEAD_DIM = 64
STATE = 128
GROUPS = 8
CONV_K = 4
CHUNK = 256
POOL_WINDOWS = (2, 4, 8, 16)
ALPHA = 2.0 ** 0.25
LN_EPS = 1e-5
RMS_EPS = 1e-5
LR, B1, B2, ADAM_EPS, WD, STEP = 0.001, 0.9, 0.999, 1e-08, 0.01, 10
N_DEV = 8
LANES = 128
SUBLANES = 8
VMEM_LIMIT = 56 * 1024 * 1024
NEG_BIG = -1e30

NN = (((1,), (0,)), ((), ()))
NT = (((1,), (1,)), ((), ()))
TN = (((0,), (0,)), ((), ()))


def _dot(a, b, dims=NN):
    return lax.dot_general(a.astype(BF16), b.astype(BF16), dims, preferred_element_type=F32)


def _dot_exact01(q, e, dims=NN):
    hi = q.astype(BF16)
    r1 = q - hi.astype(F32)
    mid = r1.astype(BF16)
    lo = (r1 - mid.astype(F32)).astype(BF16)
    f = lambda p: lax.dot_general(p, e, dims, preferred_element_type=F32)
    return f(hi) + f(mid) + f(lo)


def _params(sem):
    return pltpu.CompilerParams(dimension_semantics=sem, vmem_limit_bytes=VMEM_LIMIT)


def _sigmoid(x):
    return 1.0 / (1.0 + jnp.exp(-x))


def _colsum(x):
    return jnp.sum(x, axis=0, keepdims=True)


def _ln_fwd(r):
    mu = jnp.mean(r, axis=-1, keepdims=True)
    xc = r - mu
    var = jnp.mean(xc * xc, axis=-1, keepdims=True)
    rstd = lax.rsqrt(var + LN_EPS)
    return xc * rstd, rstd


def _ln_bwd(dy, xhat, rstd, g):
    dxh = dy * g
    m1 = jnp.mean(dxh, axis=-1, keepdims=True)
    m2 = jnp.mean(dxh * xhat, axis=-1, keepdims=True)
    return rstd * (dxh - m1 - xhat * m2)


_Comm = collections.namedtuple("_Comm", "inputs out_shapes scratch start wait")
ANY = pl.BlockSpec(memory_space=pl.ANY)


def _fuse_comm(body, grid, n_in, n_out, comm):
    if comm is None:
        return body
    ci, co = len(comm.inputs), len(comm.out_shapes)

    def fused(*refs):
        ins, cins = refs[:n_in], refs[n_in:n_in + ci]
        o0 = n_in + ci
        outs, couts = refs[o0:o0 + n_out], refs[o0 + n_out:o0 + n_out + co]
        rest = refs[o0 + n_out + co:]
        scr, cscr = rest[:len(rest) - len(comm.scratch)], rest[len(rest) - len(comm.scratch):]
        ids = [pl.program_id(a) for a in range(len(grid))]
        first, last = ids[0] == 0, ids[0] == grid[0] - 1
        for a in range(1, len(grid)):
            first, last = first & (ids[a] == 0), last & (ids[a] == grid[a] - 1)

        @pl.when(first)
        def _():
            comm.start(cins, couts, cscr)

        body(*ins, *outs, *scr)

        @pl.when(last)
        def _():
            comm.wait(cins, couts, cscr)

    return fused


def _comm_specs(comm):
    if comm is None:
        return [], [], [], []
    return list(comm.inputs), [ANY] * len(comm.inputs), [ANY] * len(comm.out_shapes), list(comm.out_shapes)


def _run_comm(comm, name):
    ci, co = len(comm.inputs), len(comm.out_shapes)

    def body(*refs):
        comm.start(refs[:ci], refs[ci:ci + co], refs[ci + co:])
        comm.wait(refs[:ci], refs[ci:ci + co], refs[ci + co:])

    return pl.pallas_call(body, name=name, in_specs=[ANY] * ci, out_specs=[ANY] * co, out_shape=list(comm.out_shapes),
                          scratch_shapes=list(comm.scratch))(*comm.inputs)


def _matmul(a, b, mode, out_dtype, bm, bn, bk, name, a_fn=None, col_blocks=0, comm=None):
    if mode == "nn":
        (m, k), n, dims = a.shape, b.shape[1], NN
    elif mode == "nt":
        (m, k), n, dims = a.shape, b.shape[0], NT
    else:
        (k, m), n, dims = a.shape, b.shape[1], TN
    bm, bn, bk = min(bm, m), min(bn, n), min(bk, k)
    assert m % bm == 0 and n % bn == 0 and k % bk == 0, (name, m, n, k, bm, bn, bk)
    nk = k // bk
    if mode == "nn":
        a_spec = pl.BlockSpec((bm, bk), lambda i, j, kk: (i, kk))
        b_spec = pl.BlockSpec((bk, bn), lambda i, j, kk: (kk, j))
    elif mode == "nt":
        a_spec = pl.BlockSpec((bm, bk), lambda i, j, kk: (i, kk))
        b_spec = pl.BlockSpec((bn, bk), lambda i, j, kk: (j, kk))
    else:
        a_spec = pl.BlockSpec((bk, bm), lambda i, j, kk: (kk, i))
        b_spec = pl.BlockSpec((bk, bn), lambda i, j, kk: (kk, j))

    def body(a_ref, b_ref, o_ref, acc_ref):
        kk = pl.program_id(2)
        av = a_ref[...]
        if a_fn is not None:
            av = a_fn(av.astype(F32))
        prod = _dot(av, b_ref[...], dims)

        def emit(total):
            if col_blocks:
                for s in range(bn // slab):
                    o_ref[s] = total[:, s * slab:(s + 1) * slab].astype(o_ref.dtype)
            else:
                o_ref[...] = total.astype(o_ref.dtype)

        if nk == 1:
            emit(prod)
        else:
            @pl.when(kk == 0)
            def _():
                acc_ref[...] = prod

            @pl.when((kk > 0) & (kk < nk - 1))
            def _():
                acc_ref[...] += prod

            @pl.when(kk == nk - 1)
            def _():
                emit(acc_ref[...] + prod)

    if col_blocks:
        slab = n // col_blocks
        assert n % col_blocks == 0 and bn % slab == 0, (name, n, col_blocks, bn)
        out_spec = pl.BlockSpec((bn // slab, bm, slab), lambda i, j, kk: (j, i, 0))
        out_shape = jax.ShapeDtypeStruct((col_blocks, m, slab), out_dtype)
    else:
        out_spec = pl.BlockSpec((bm, bn), lambda i, j, kk: (i, j))
        out_shape = jax.ShapeDtypeStruct((m, n), out_dtype)
    grid = (m // bm, n // bn, nk)
    c_in, c_in_specs, c_out_specs, c_out_shapes = _comm_specs(comm)
    res = pl.pallas_call(
        _fuse_comm(body, grid, 2, 1, comm), name=name,
        grid=grid,
        in_specs=[a_spec, b_spec] + c_in_specs,
        out_specs=[out_spec] + c_out_specs,
        out_shape=[out_shape] + c_out_shapes,
        scratch_shapes=[pltpu.VMEM((bm, bn), F32)] + (list(comm.scratch) if comm else []),
        compiler_params=_params(("arbitrary",) * 3 if comm else ("parallel", "parallel", "arbitrary")),
    )(a, b, *c_in)
    return res if comm else res[0]


CONV_STRIP = 16
CONV_COLS = 512


def _conv_pre(ext_ref, w_ref, b_ref, r0, rows, cols=slice(None)):
    acc = b_ref[:, cols] + w_ref[0:1, cols] * ext_ref[pl.ds(r0 + SUBLANES - (CONV_K - 1), rows), cols]
    for k in range(1, CONV_K):
        acc = acc + w_ref[k:k + 1, cols] * ext_ref[pl.ds(r0 + SUBLANES - (CONV_K - 1) + k, rows), cols]
    return acc


def _in_proj(xb, w_main, conv_w8, conv_b, cd, seq_len, bm, bn, comm):
    t, d = xb.shape
    pw = w_main.shape[1]
    bm, bn = min(bm, seq_len), min(bn, d)
    assert t % bm == 0 and seq_len % bm == 0 and pw % bn == 0 and cd % bn == 0
    ncj = cd // bn
    tiles_per_seq = seq_len // bm

    def body(x_ref, w_ref, cw_ref, cb_ref, p_ref, xbc_ref, dsl_ref, ext_ref, carry_ref):
        i = pl.program_id(0)
        j = pl.program_id(1)
        pq = _dot(x_ref[...], w_ref[...]).astype(BF16)
        p_ref[...] = pq

        @pl.when(j < ncj)
        def _():
            jc = jnp.minimum(j, ncj - 1)
            ext_ref[0:SUBLANES, :] = jnp.where((i % tiles_per_seq) == 0, 0.0, carry_ref[jc])
            ext_ref[SUBLANES:, :] = pq.astype(F32)
            carry_ref[jc] = ext_ref[bm:bm + SUBLANES, :]
            cw = min(bn, CONV_COLS)
            for c0 in range(0, bn, cw):
                cols = slice(c0, c0 + cw)
                for r0 in range(0, bm, CONV_STRIP):
                    rows = slice(r0, r0 + CONV_STRIP)
                    acc = _conv_pre(ext_ref, cw_ref, cb_ref, r0, CONV_STRIP, cols)
                    sg = _sigmoid(acc)
                    xbc_ref[rows, cols] = (acc * sg).astype(xbc_ref.dtype)
                    dsl_ref[rows, cols] = (sg * (1.0 + acc * (1.0 - sg))).astype(dsl_ref.dtype)

    grid = (t // bm, pw // bn)
    conv_col = lambda i, j: (0, jnp.minimum(j, ncj - 1))
    c_in, c_in_specs, c_out_specs, c_out_shapes = _comm_specs(comm)
    conv_tile = pl.BlockSpec((bm, bn), lambda i, j: (i, jnp.minimum(j, ncj - 1)))
    conv_out = jax.ShapeDtypeStruct((t, cd), BF16)
    return pl.pallas_call(
        _fuse_comm(body, grid, 4, 3, comm), name="in_proj",
        grid=grid,
        in_specs=[pl.BlockSpec((bm, d), lambda i, j: (i, 0)), pl.BlockSpec((d, bn), lambda i, j: (0, j)),
                  pl.BlockSpec((SUBLANES, bn), conv_col), pl.BlockSpec((1, bn), conv_col)] + c_in_specs,
        out_specs=[pl.BlockSpec((bm, bn), lambda i, j: (i, j)), conv_tile, conv_tile] + c_out_specs,
        out_shape=[jax.ShapeDtypeStruct((t, pw), BF16), conv_out, conv_out] + c_out_shapes,
        scratch_shapes=[pltpu.VMEM((bm + SUBLANES, bn), F32), pltpu.VMEM((ncj, SUBLANES, bn), F32)]
        + (list(comm.scratch) if comm else []),
        compiler_params=_params(("arbitrary", "arbitrary")),
    )(xb, w_main, conv_w8, conv_b, *c_in)


def _conv_bwd(proj, dsilu, dxbc, conv_w8, n_seq_chunks, col0, width, ct, L, name):
    t = proj.shape[0]
    nbc = t // L
    hb = L // SUBLANES
    ct = min(ct, width)
    assert col0 % ct == 0 and width % ct == 0
    cb0 = col0 // ct
    last_hb = t // SUBLANES - 1

    def body(x_ref, xb_ref, s_ref, sa_ref, d_ref, da_ref, w_ref, o_ref, dw_ref, db_ref, ext_ref, dc_ref):
        bc = pl.program_id(1)
        first = (bc % n_seq_chunks) == 0
        last = (bc % n_seq_chunks) == n_seq_chunks - 1

        @pl.when(bc == 0)
        def _():
            dw_ref[...] = jnp.zeros_like(dw_ref)
            db_ref[...] = jnp.zeros_like(db_ref)

        ext_ref[0:SUBLANES, :] = jnp.where(first, 0.0, xb_ref[...].astype(F32))
        ext_ref[SUBLANES:, :] = x_ref[...].astype(F32)
        for r0 in range(0, L, CONV_STRIP):
            rows = slice(r0, r0 + CONV_STRIP)
            dc_ref[rows, :] = d_ref[rows, :].astype(F32) * s_ref[rows, :].astype(F32)
        dc_ref[L:, :] = jnp.where(last, 0.0, da_ref[...].astype(F32)) * sa_ref[...].astype(F32)
        fold = lambda v: v[0:SUBLANES] + v[SUBLANES:CONV_STRIP]
        dws = [jnp.zeros((SUBLANES, ct), F32) for _ in range(CONV_K)]
        dbs = jnp.zeros((SUBLANES, ct), F32)
        for r0 in range(0, L, CONV_STRIP):
            dc = dc_ref[r0:r0 + CONV_STRIP, :]
            dx = w_ref[CONV_K - 1:CONV_K, :] * dc
            for k in range(CONV_K - 1):
                dx = dx + w_ref[k:k + 1, :] * dc_ref[pl.ds(r0 + CONV_K - 1 - k, CONV_STRIP), :]
            o_ref[r0:r0 + CONV_STRIP, :] = dx.astype(o_ref.dtype)
            for k in range(CONV_K):
                dws[k] = dws[k] + fold(dc * ext_ref[pl.ds(r0 + SUBLANES - (CONV_K - 1) + k, CONV_STRIP), :])
            dbs = dbs + fold(dc)
        for k in range(CONV_K):
            dw_ref[k:k + 1, :] += _colsum(dws[k])
        db_ref[0:1, :] += _colsum(dbs)

    return pl.pallas_call(
        body, name=name,
        grid=(width // ct, nbc),
        in_specs=[
            pl.BlockSpec((L, ct), lambda j, i: (i, cb0 + j)),
            pl.BlockSpec((SUBLANES, ct), lambda j, i: (jnp.maximum(i * hb - 1, 0), cb0 + j)),
            pl.BlockSpec((L, ct), lambda j, i: (i, cb0 + j)),
            pl.BlockSpec((SUBLANES, ct), lambda j, i: (jnp.minimum((i + 1) * hb, last_hb), cb0 + j)),
            pl.BlockSpec((L, ct), lambda j, i: (i, j)),
            pl.BlockSpec((SUBLANES, ct), lambda j, i: (jnp.minimum((i + 1) * hb, last_hb), j)),
            pl.BlockSpec((SUBLANES, ct), lambda j, i: (0, cb0 + j)),
        ],
        out_specs=[
            pl.BlockSpec((L, ct), lambda j, i: (i, j)),
            pl.BlockSpec((SUBLANES, ct), lambda j, i: (0, j)),
            pl.BlockSpec((SUBLANES, ct), lambda j, i: (0, j)),
        ],
        out_shape=[
            jax.ShapeDtypeStruct((t, width), BF16),
            jax.ShapeDtypeStruct((SUBLANES, width), F32),
            jax.ShapeDtypeStruct((SUBLANES, width), F32),
        ],
        scratch_shapes=[pltpu.VMEM((L + SUBLANES, ct), F32), pltpu.VMEM((L + SUBLANES, ct), F32)],
        compiler_params=_params(("parallel", "arbitrary")),
    )(proj, proj, dsilu, dsilu, dxbc, dxbc, conv_w8)


def _cumsum_rows(x, reverse=False):
    n = x.shape[0]
    row = lax.broadcasted_iota(jnp.int32, x.shape, 0)
    s = 1
    while s < n:
        if reverse:
            x = x + jnp.where(row < n - s, pltpu.roll(x, n - s, 0), 0.0)
        else:
            x = x + jnp.where(row >= s, pltpu.roll(x, s, 0), 0.0)
        s *= 2
    return x


def _ssd_scalars(dtr, dtb, alog):
    pre = dtr + dtb
    dt = jnp.maximum(pre, 0.0) + jnp.log(1.0 + jnp.exp(-jnp.abs(pre)))
    a = -jnp.exp(alog)
    acs = _cumsum_rows(dt * a) * LOG2E
    n = acs.shape[0]
    return pre, dt, a, acs, jnp.exp2(acs), jnp.exp2(acs[n - 1:n, :] - acs)


LOG2E = 1.4426950408889634


def _dot_2piece(q, e):
    hi = q.astype(BF16)
    mid = (q - hi.astype(F32)).astype(BF16)
    return lax.dot_general(jnp.concatenate([hi, mid], axis=1), jnp.concatenate([e, e], axis=0), NN,
                           preferred_element_type=F32)


def _ssd_group_common(dt_s, e_s, dec_s, e):
    return _dot_2piece(dt_s, e), _dot_2piece(e_s, e), _dot_2piece(dec_s, e)


def _decay_matrix(acs, acs_t, h, tri):
    return jnp.exp2(jnp.where(tri, acs[:, h:h + 1] - acs_t[h:h + 1, :], NEG_BIG))


def _head_mask(r, gw, dtype):
    lane = lax.broadcasted_iota(jnp.int32, (1, gw), 1)
    return ((lane >= r * HEAD_DIM) & (lane < (r + 1) * HEAD_DIM)).astype(dtype)


def _ssd_fwd(xbc, proj, dt_raw, dtb, alog, dskip_x, normw, emat, bl, inner, z_col0):
    t = xbc.shape[0]
    L = CHUNK
    nc = t // bl // L
    G = GROUPS
    gw = inner // G
    hpg = gw // HEAD_DIM
    assert z_col0 % gw == 0
    zb0 = z_col0 // gw
    bb0 = inner // STATE
    cb0 = bb0 + G

    P = G
    assert bb0 % P == 0 and cb0 % P == 0 and zb0 % P == 0

    def body(xs_ref, b_ref, c_ref, z_ref, dtr_ref, dtb_ref, alog_ref, dsk_ref, nw_ref, e_ref,
             y_ref, yn_ref, st_ref, h_ref):
        c = pl.program_id(1)
        _, dt_s, _, acs, e_s, dec_s = _ssd_scalars(dtr_ref[...], dtb_ref[...], alog_ref[...])
        acs_t = acs.T
        tri = lax.broadcasted_iota(jnp.int32, (L, L), 0) >= lax.broadcasted_iota(jnp.int32, (L, L), 1)
        lane = lax.broadcasted_iota(jnp.int32, (L, gw), 1)
        for g in range(G):
            cols = slice(g * gw, (g + 1) * gw)
            ncol = slice(g * STATE, (g + 1) * STATE)

            @pl.when(c == 0)
            def _():
                h_ref[g] = jnp.zeros((STATE, gw), F32)

            xs = xs_ref[:, cols].astype(F32)
            bg = b_ref[:, ncol]
            cg = c_ref[:, ncol]
            dt_x, e_x, dec_x = _ssd_group_common(dt_s, e_s, dec_s, e_ref[:, cols])
            xdt = xs * dt_x
            cb = _dot(cg, bg, NT)
            h = h_ref[g]
            st_ref[0, g] = h
            y = _dot(cg, h) * e_x + dsk_ref[:, cols] * xs
            for r in range(hpg):
                m = cb * _decay_matrix(acs, acs_t, g * hpg + r, tri)
                xr = jnp.where((lane >= r * HEAD_DIM) & (lane < (r + 1) * HEAD_DIM), xdt, 0.0)
                y = y + _dot(m, xr)
            h_ref[g] = h * e_x[L - 1:L, :] + _dot(bg, xdt * dec_x, TN)
            yq = y.astype(y_ref.dtype)
            y_ref[:, cols] = yq
            z = z_ref[:, cols].astype(F32)
            yg = yq.astype(F32) * (z * _sigmoid(z))
            rs = lax.rsqrt(jnp.mean(yg * yg, axis=-1, keepdims=True) + RMS_EPS)
            yn_ref[:, cols] = (yg * rs * nw_ref[:, cols]).astype(yn_ref.dtype)

    return pl.pallas_call(
        body, name="ssd_fwd",
        grid=(bl, nc, G // P),
        in_specs=[
            pl.BlockSpec((L, P * gw), lambda b, c, g: (b * nc + c, g)),
            pl.BlockSpec((L, P * STATE), lambda b, c, g: (b * nc + c, bb0 // P + g)),
            pl.BlockSpec((L, P * STATE), lambda b, c, g: (b * nc + c, cb0 // P + g)),
            pl.BlockSpec((L, P * gw), lambda b, c, g: (b * nc + c, zb0 // P + g)),
            pl.BlockSpec((L, LANES), lambda b, c, g: (b * nc + c, 0)),
            pl.BlockSpec((1, LANES), lambda b, c, g: (0, 0)),
            pl.BlockSpec((1, LANES), lambda b, c, g: (0, 0)),
            pl.BlockSpec((1, P * gw), lambda b, c, g: (0, g)),
            pl.BlockSpec((1, P * gw), lambda b, c, g: (0, g)),
            pl.BlockSpec((LANES, P * gw), lambda b, c, g: (0, g)),
        ],
        out_specs=[
            pl.BlockSpec((L, P * gw), lambda b, c, g: (b * nc + c, g)),
            pl.BlockSpec((L, P * gw), lambda b, c, g: (b * nc + c, g)),
            pl.BlockSpec((1, P, STATE, gw), lambda b, c, g: (b * nc + c, g, 0, 0)),
        ],
        out_shape=[
            jax.ShapeDtypeStruct((t, inner), BF16),
            jax.ShapeDtypeStruct((t, inner), BF16),
            jax.ShapeDtypeStruct((bl * nc, G, STATE, gw), F32),
        ],
        scratch_shapes=[pltpu.VMEM((G, STATE, gw), F32)],
        compiler_params=_params(("arbitrary", "arbitrary", "arbitrary")),
    )(xbc, xbc, xbc, proj, dt_raw, dtb, alog, dskip_x, normw, emat)


def _ssd_bwd(xbc, proj, dt_raw, y, dyn, states, dtb, alog, dskip_x, normw, emat, emat_t, bl, inner, z_col0,
             comm=None):
    t = xbc.shape[0]
    L = CHUNK
    nc = t // bl // L
    G = GROUPS
    gw = inner // G
    hpg = gw // HEAD_DIM
    zb0 = z_col0 // gw
    bb0 = inner // STATE
    cb0 = bb0 + G
    P = G

    def rc(j):
        return nc - 1 - j

    def body(xs_ref, b_ref, c_ref, z_ref, dtr_ref, y_ref, dyn_ref, st_ref, dtb_ref, alog_ref, dsk_ref,
             nw_ref, e_ref, et_ref,
             dxs_ref, db_ref, dc_ref, dz_ref, ddt_ref, dnw_ref, dsk_acc, dalog_acc, ddtb_acc,
             dh_ref):
        b = pl.program_id(0)
        j = pl.program_id(1)

        @pl.when((b == 0) & (j == 0))
        def _():
            dsk_acc[...] = jnp.zeros_like(dsk_acc)
            dalog_acc[...] = jnp.zeros_like(dalog_acc)
            ddtb_acc[...] = jnp.zeros_like(ddtb_acc)

        pre, dt_s, a_row, acs, e_s, dec_s = _ssd_scalars(dtr_ref[...], dtb_ref[...], alog_ref[...])
        acs_t = acs.T
        wacs = jnp.zeros((L, LANES), F32)
        wdt = jnp.zeros((L, LANES), F32)
        tri = lax.broadcasted_iota(jnp.int32, (L, L), 0) >= lax.broadcasted_iota(jnp.int32, (L, L), 1)
        rowi = lax.broadcasted_iota(jnp.int32, (L, gw), 0)
        for g in range(G):
            cols = slice(g * gw, (g + 1) * gw)
            ncol = slice(g * STATE, (g + 1) * STATE)

            @pl.when((b == 0) & (j == 0))
            def _():
                dnw_ref[g] = jnp.zeros((SUBLANES, gw), F32)

            @pl.when(j == 0)
            def _():
                dh_ref[g] = jnp.zeros((STATE, gw), F32)

            xs = xs_ref[:, cols].astype(F32)
            bg = b_ref[:, ncol]
            cg = c_ref[:, ncol]
            dt_x, e_x, dec_x = _ssd_group_common(dt_s, e_s, dec_s, e_ref[:, cols])
            xdt = xs * dt_x
            xdt_b = xdt.astype(BF16)
            cb = _dot(cg, bg, NT)
            h = st_ref[0, g]
            hb16 = h.astype(BF16)
            dsk = dsk_ref[:, cols]

            yv = y_ref[:, cols].astype(F32)
            z = z_ref[:, cols].astype(F32)
            sgz = _sigmoid(z)
            sz = z * sgz
            yg = yv * sz
            rs = lax.rsqrt(jnp.mean(yg * yg, axis=-1, keepdims=True) + RMS_EPS)
            yhat = yg * rs
            dyn_v = dyn_ref[:, cols].astype(F32)
            dnw_ref[g] += _colsum(dyn_v * yhat)
            dyh = dyn_v * nw_ref[:, cols]
            dyg = rs * (dyh - yhat * jnp.mean(dyh * yhat, axis=-1, keepdims=True))
            dy = dyg * sz
            dz_ref[:, cols] = (dyg * yv * (sgz * (1.0 + z * (1.0 - sgz)))).astype(dz_ref.dtype)

            dy_b = dy.astype(BF16)
            dcb = jnp.zeros((L, L), F32)
            dxdt_d = jnp.zeros((L, gw), F32)
            ydiag = jnp.zeros((L, gw), F32)
            for r in range(hpg):
                lm = _decay_matrix(acs, acs_t, g * hpg + r, tri)
                m = (cb * lm).astype(BF16)
                hm = _head_mask(r, gw, BF16)
                dyr = dy_b * hm
                xr = xdt_b * hm
                ydiag = ydiag + _dot(m, xr)
                dcb = dcb + _dot(dyr, xdt_b, NT) * lm
                dxdt_d = dxdt_d + _dot(m, dyr, TN)
            dh = dh_ref[g]
            dh16 = dh.astype(BF16)
            xdec_b = (xdt * dec_x).astype(BF16)
            bdh = _dot(bg, dh16)
            dxdt = dxdt_d + dec_x * bdh
            dcb16 = dcb.astype(BF16)
            dye = (dy * e_x).astype(BF16)
            db_ref[:, ncol] = (_dot(dcb16, cg, TN) + _dot(xdec_b, dh16, NT)).astype(db_ref.dtype)
            dc_ref[:, ncol] = (_dot(dcb16, bg) + _dot(dye, hb16, NT)).astype(dc_ref.dtype)
            dprev = _dot(cg, dye, TN)
            cd_row = e_x[L - 1:L, :]
            s_new = _dot(bg, xdec_b, TN)
            last_term = _colsum(dh16.astype(F32) * s_new) + _colsum(dh * h) * cd_row
            yoff = _dot(cg, hb16) * e_x
            wfold = (dy_b.astype(F32) * ydiag + dy * yoff - dxdt_d * xdt_b.astype(F32) - bdh * xdec_b.astype(F32)
                     + jnp.where(rowi == L - 1, last_term, 0.0))
            et = et_ref[cols, :]
            wacs = wacs + _dot_2piece(wfold, et)
            wdt = wdt + _dot_2piece(dxdt * xs, et)
            dsk_acc[...] += _dot_exact01(jnp.broadcast_to(_colsum(dy * xs), (SUBLANES, gw)), et)
            dxs_ref[:, cols] = (dsk * dy + dxdt * dt_x).astype(dxs_ref.dtype)
            dh_ref[g] = dprev + cd_row * dh

        dda = _cumsum_rows(wacs, reverse=True)
        ddt_raw = (wdt + dda * a_row) * _sigmoid(pre)
        ddt_ref[...] = ddt_raw
        dalog_acc[...] += _colsum(dda * dt_s) * a_row
        ddtb_acc[...] += _colsum(ddt_raw)

    def cidx(b, j):
        return b * nc + rc(j)

    accs = lambda shape: pl.BlockSpec(shape, lambda b, j, g: tuple(0 for _ in shape))
    grid = (bl, nc, G // P)
    c_in, c_in_specs, c_out_specs, c_out_shapes = _comm_specs(comm)
    return pl.pallas_call(
        _fuse_comm(body, grid, 14, 9, comm), name="ssd_bwd",
        grid=grid,
        in_specs=[
            pl.BlockSpec((L, P * gw), lambda b, j, g: (cidx(b, j), g)),
            pl.BlockSpec((L, P * STATE), lambda b, j, g: (cidx(b, j), bb0 // P + g)),
            pl.BlockSpec((L, P * STATE), lambda b, j, g: (cidx(b, j), cb0 // P + g)),
            pl.BlockSpec((L, P * gw), lambda b, j, g: (cidx(b, j), zb0 // P + g)),
            pl.BlockSpec((L, LANES), lambda b, j, g: (cidx(b, j), 0)),
            pl.BlockSpec((L, P * gw), lambda b, j, g: (cidx(b, j), g)),
            pl.BlockSpec((L, P * gw), lambda b, j, g: (cidx(b, j), g)),
            pl.BlockSpec((1, P, STATE, gw), lambda b, j, g: (cidx(b, j), g, 0, 0)),
            pl.BlockSpec((1, LANES), lambda b, j, g: (0, 0)),
            pl.BlockSpec((1, LANES), lambda b, j, g: (0, 0)),
            pl.BlockSpec((1, P * gw), lambda b, j, g: (0, g)),
            pl.BlockSpec((1, P * gw), lambda b, j, g: (0, g)),
            pl.BlockSpec((LANES, P * gw), lambda b, j, g: (0, g)),
            pl.BlockSpec((P * gw, LANES), lambda b, j, g: (g, 0)),
        ] + c_in_specs,
        out_specs=[
            pl.BlockSpec((L, P * gw), lambda b, j, g: (cidx(b, j), g)),
            pl.BlockSpec((L, P * STATE), lambda b, j, g: (cidx(b, j), g)),
            pl.BlockSpec((L, P * STATE), lambda b, j, g: (cidx(b, j), g)),
            pl.BlockSpec((L, P * gw), lambda b, j, g: (cidx(b, j), g)),
            pl.BlockSpec((L, LANES), lambda b, j, g: (cidx(b, j), 0)),
            accs((G, SUBLANES, gw)),
            accs((SUBLANES, LANES)),
            accs((SUBLANES, LANES)),
            accs((SUBLANES, LANES)),
        ] + c_out_specs,
        out_shape=[
            jax.ShapeDtypeStruct((t, inner), BF16),
            jax.ShapeDtypeStruct((t, G * STATE), BF16),
            jax.ShapeDtypeStruct((t, G * STATE), BF16),
            jax.ShapeDtypeStruct((t, inner), BF16),
            jax.ShapeDtypeStruct((t, LANES), F32),
            jax.ShapeDtypeStruct((G, SUBLANES, gw), F32),
            jax.ShapeDtypeStruct((SUBLANES, LANES), F32),
            jax.ShapeDtypeStruct((SUBLANES, LANES), F32),
            jax.ShapeDtypeStruct((SUBLANES, LANES), F32),
        ] + c_out_shapes,
        scratch_shapes=[pltpu.VMEM((G, STATE, gw), F32)]
        + (list(comm.scratch) if comm else []),
        compiler_params=_params(("arbitrary", "arbitrary", "arbitrary")),
    )(xbc, xbc, xbc, proj, dt_raw, y, dyn, states, dtb, alog, dskip_x, normw, emat, emat_t, *c_in)


def _pool_window(u, w, anti):
    n = u.shape[0]
    row = lax.broadcasted_iota(jnp.int32, u.shape, 0)
    acc = u
    s = 1
    while s < w:
        if anti:
            acc = acc + jnp.where(row < n - s, pltpu.roll(acc, n - s, 0), 0.0)
        else:
            acc = acc + jnp.where(row >= s, pltpu.roll(acc, s, 0), 0.0)
        s *= 2
    return acc


def _pool_cnt(shape, w):
    row = lax.broadcasted_iota(jnp.int32, shape, 0)
    return jnp.minimum(row + 1, w).astype(F32)


def _pool_fwd(proj, wpg, bl, d, u_col0):
    t = proj.shape[0]
    s = t // bl
    pg = len(POOL_WINDOWS)
    cg = d // pg
    ub0 = u_col0 // d

    def body(u_ref, w_ref, o_ref):
        for gi, w in enumerate(POOL_WINDOWS):
            u = u_ref[:, gi * cg:(gi + 1) * cg].astype(F32)
            pooled = _pool_window(u, w, False) / _pool_cnt(u.shape, w) - u
            o_ref[:, gi * cg:(gi + 1) * cg] = _dot(pooled, w_ref[gi]).astype(o_ref.dtype)

    return pl.pallas_call(
        body, name="pool_fwd",
        grid=(bl,),
        in_specs=[pl.BlockSpec((s, d), lambda b: (b, ub0)), pl.BlockSpec((pg, cg, cg), lambda b: (0, 0, 0))],
        out_specs=pl.BlockSpec((s, d), lambda b: (b, 0)),
        out_shape=jax.ShapeDtypeStruct((t, d), BF16),
        compiler_params=_params(("parallel",)),
    )(proj, wpg)


def _pool_bwd(proj, dyp, wpg, bl, d, u_col0):
    t = proj.shape[0]
    s = t // bl
    pg = len(POOL_WINDOWS)
    cg = d // pg
    ub0 = u_col0 // d

    def body(u_ref, dy_ref, w_ref, du_ref, dw_ref):
        @pl.when(pl.program_id(0) == 0)
        def _():
            dw_ref[...] = jnp.zeros_like(dw_ref)

        for gi, w in enumerate(POOL_WINDOWS):
            u = u_ref[:, gi * cg:(gi + 1) * cg].astype(F32)
            cnt = _pool_cnt(u.shape, w)
            pooled = _pool_window(u, w, False) / cnt - u
            dy = dy_ref[:, gi * cg:(gi + 1) * cg]
            dw_ref[gi] += _dot(pooled, dy, TN)
            dp = _dot(dy, w_ref[gi], NT)
            du_ref[:, gi * cg:(gi + 1) * cg] = (_pool_window(dp / cnt, w, True) - dp).astype(du_ref.dtype)

    return pl.pallas_call(
        body, name="pool_bwd",
        grid=(bl,),
        in_specs=[pl.BlockSpec((s, d), lambda b: (b, ub0)), pl.BlockSpec((s, d), lambda b: (b, 0)),
                  pl.BlockSpec((pg, cg, cg), lambda b: (0, 0, 0))],
        out_specs=[pl.BlockSpec((s, d), lambda b: (b, 0)), pl.BlockSpec((pg, cg, cg), lambda b: (0, 0, 0))],
        out_shape=[jax.ShapeDtypeStruct((t, d), BF16), jax.ShapeDtypeStruct((pg, cg, cg), F32)],
        compiler_params=_params(("arbitrary",)),
    )(proj, dyp, wpg)


def _merge_fwd(proj, ypr, yssd, x, w_out, b_gates, pool_scale, d, lg_col0, tm):
    t = x.shape[0]
    lb0 = lg_col0 // (2 * d)

    def body(lg_ref, yp_ref, ys_ref, x_ref, w_ref, bg_ref, ps_ref, mg_ref, r1_ref):
        lg = lg_ref[...].astype(F32) + bg_ref[...]
        ga = _sigmoid(lg[:, :d])
        gb = _sigmoid(lg[:, d:])
        merged = ga * (yp_ref[...].astype(F32) * ps_ref[...]) + gb * ys_ref[...].astype(F32)
        mg_ref[...] = merged.astype(mg_ref.dtype)
        r1_ref[...] = ALPHA * x_ref[...] + _dot(mg_ref[...], w_ref[...])

    row = lambda w: pl.BlockSpec((tm, w), lambda i: (i, 0))
    full = lambda a: pl.BlockSpec(a.shape, lambda i: (0, 0))
    return pl.pallas_call(
        body, name="merge_fwd",
        grid=(t // tm,),
        in_specs=[pl.BlockSpec((tm, 2 * d), lambda i: (i, lb0)), row(d), row(d), row(d), full(w_out), full(b_gates),
                  full(pool_scale)],
        out_specs=[row(d), row(d)],
        out_shape=[jax.ShapeDtypeStruct((t, d), BF16), jax.ShapeDtypeStruct((t, d), F32)],
        compiler_params=_params(("parallel",)),
    )(proj, ypr, yssd, x, w_out, b_gates, pool_scale)


def _merge_bwd(dr1, proj, ypr, yssd, w_out, b_gates, pool_scale, d, lg_col0, tm):
    t = dr1.shape[0]
    lb0 = lg_col0 // (2 * d)

    def body(dr_ref, lg_ref, yp_ref, ys_ref, w_ref, bg_ref, ps_ref, dlg_ref, dyp_ref, dys_ref, dbg_ref, dps_ref):
        @pl.when(pl.program_id(0) == 0)
        def _():
            dbg_ref[...] = jnp.zeros_like(dbg_ref)
            dps_ref[...] = jnp.zeros_like(dps_ref)

        dm = _dot(dr_ref[...], w_ref[...], NT)
        lg = lg_ref[...].astype(F32) + bg_ref[...]
        ga = _sigmoid(lg[:, :d])
        gb = _sigmoid(lg[:, d:])
        ypr_v = yp_ref[...].astype(F32)
        ys_v = ys_ref[...].astype(F32)
        ps = ps_ref[...]
        dga = dm * ypr_v * ps
        dla = dga * ga * (1.0 - ga)
        dlb = dm * ys_v * gb * (1.0 - gb)
        dlg_ref[:, :d] = dla.astype(dlg_ref.dtype)
        dlg_ref[:, d:] = dlb.astype(dlg_ref.dtype)
        dyp_ref[...] = (dm * ga * ps).astype(dyp_ref.dtype)
        dys_ref[...] = (dm * gb).astype(dys_ref.dtype)
        dbg_ref[0:1, :d] += _colsum(dla)
        dbg_ref[0:1, d:] += _colsum(dlb)
        dps_ref[0:1, :] += _colsum(dm * ga * ypr_v)

    row = lambda w: pl.BlockSpec((tm, w), lambda i: (i, 0))
    full = lambda a: pl.BlockSpec(a.shape, lambda i: (0, 0))
    acc = lambda w: pl.BlockSpec((SUBLANES, w), lambda i: (0, 0))
    return pl.pallas_call(
        body, name="merge_bwd",
        grid=(t // tm,),
        in_specs=[row(d), pl.BlockSpec((tm, 2 * d), lambda i: (i, lb0)), row(d), row(d), full(w_out), full(b_gates),
                  full(pool_scale)],
        out_specs=[row(2 * d), row(d), row(d), acc(2 * d), acc(d)],
        out_shape=[jax.ShapeDtypeStruct((t, 2 * d), BF16), jax.ShapeDtypeStruct((t, d), BF16),
                   jax.ShapeDtypeStruct((t, d), BF16), jax.ShapeDtypeStruct((SUBLANES, 2 * d), F32),
                   jax.ShapeDtypeStruct((SUBLANES, d), F32)],
        compiler_params=_params(("arbitrary",)),
    )(dr1, proj, ypr, yssd, w_out, b_gates, pool_scale)


MLP_SLABS_PER_STEP = 2


def _mlp_fwd(r1, target, w_up, w_down, ln1_g, ln1_b, ln2_g, ln2_b, tm):
    t, d = r1.shape
    ns, _, sw = w_up.shape
    spb = MLP_SLABS_PER_STEP
    assert ns % spb == 0
    nf, tf, ff = ns // spb, spb * sw, ns * sw

    def body(r1_ref, tg_ref, wu_ref, wd_ref, g1_ref, b1_ref, g2_ref, b2_ref,
             up_ref, h1_ref, dr2_ref, loss_ref, dg2_ref, db2_ref, h1f, acc):
        i = pl.program_id(0)
        f = pl.program_id(1)

        @pl.when((i == 0) & (f == 0))
        def _():
            loss_ref[...] = jnp.zeros_like(loss_ref)
            dg2_ref[...] = jnp.zeros_like(dg2_ref)
            db2_ref[...] = jnp.zeros_like(db2_ref)

        @pl.when(f == 0)
        def _():
            xhat, _ = _ln_fwd(r1_ref[...])
            h1 = xhat * g1_ref[...] + b1_ref[...]
            h1f[...] = h1
            h1_ref[...] = h1.astype(h1_ref.dtype)
            acc[...] = jnp.zeros_like(acc)

        for s in range(spb):
            up_ref[:, s * sw:(s + 1) * sw] = _dot(h1_ref[...], wu_ref[s]).astype(up_ref.dtype)
        upq = jnp.maximum(up_ref[...].astype(F32), 0.0)
        acc[...] += _dot(upq * upq, wd_ref[...])

        @pl.when(f == nf - 1)
        def _():
            xhat, rstd = _ln_fwd(ALPHA * h1f[...] + acc[...])
            g2 = g2_ref[...]
            diff = xhat * g2 + b2_ref[...] - tg_ref[...]
            loss_ref[...] += 0.5 / d * jnp.sum(diff * diff)
            dh2 = diff * (1.0 / d)
            dg2_ref[0:1, :] += _colsum(dh2 * xhat)
            db2_ref[0:1, :] += _colsum(dh2)
            dr2_ref[...] = _ln_bwd(dh2, xhat, rstd, g2).astype(dr2_ref.dtype)

    row = pl.BlockSpec((tm, d), lambda i, f: (i, 0))
    vec = pl.BlockSpec((1, d), lambda i, f: (0, 0))
    acc8 = pl.BlockSpec((SUBLANES, d), lambda i, f: (0, 0))
    return pl.pallas_call(
        body, name="mlp_fwd",
        grid=(t // tm, nf),
        in_specs=[row, row, pl.BlockSpec((spb, d, sw), lambda i, f: (f, 0, 0)), pl.BlockSpec((tf, d), lambda i, f: (f, 0)),
                  vec, vec, vec, vec],
        out_specs=[pl.BlockSpec((tm, tf), lambda i, f: (i, f)), row, row,
                   pl.BlockSpec((SUBLANES, LANES), lambda i, f: (0, 0)), acc8, acc8],
        out_shape=[jax.ShapeDtypeStruct((t, ff), BF16), jax.ShapeDtypeStruct((t, d), BF16),
                   jax.ShapeDtypeStruct((t, d), BF16), jax.ShapeDtypeStruct((SUBLANES, LANES), F32),
                   jax.ShapeDtypeStruct((SUBLANES, d), F32), jax.ShapeDtypeStruct((SUBLANES, d), F32)],
        scratch_shapes=[pltpu.VMEM((tm, d), F32), pltpu.VMEM((tm, d), F32)],
        compiler_params=_params(("arbitrary", "arbitrary")),
    )(r1, target, w_up, w_down, ln1_g, ln1_b, ln2_g, ln2_b)


def _mlp_bwd(dr2, up, r1, w_up, w_down, ln1_g, tm):
    t, d = r1.shape
    ns, _, sw = w_up.shape
    spb = MLP_SLABS_PER_STEP
    assert ns % spb == 0
    nf, tf, ff = ns // spb, spb * sw, ns * sw

    def body(dr2_ref, up_ref, r1_ref, wu_ref, wd_ref, g1_ref, dup_ref, dr1_ref, dg1_ref, db1_ref, acc):
        i = pl.program_id(0)
        f = pl.program_id(1)

        @pl.when((i == 0) & (f == 0))
        def _():
            dg1_ref[...] = jnp.zeros_like(dg1_ref)
            db1_ref[...] = jnp.zeros_like(db1_ref)

        @pl.when(f == 0)
        def _():
            acc[...] = jnp.zeros_like(acc)

        dact = _dot(dr2_ref[...], wd_ref[...], NT)
        dup_ref[...] = (dact * 2.0 * jnp.maximum(up_ref[...].astype(F32), 0.0)).astype(dup_ref.dtype)
        for s in range(spb):
            acc[...] += _dot(dup_ref[:, s * sw:(s + 1) * sw], wu_ref[s], NT)

        @pl.when(f == nf - 1)
        def _():
            dh1 = acc[...] + ALPHA * dr2_ref[...].astype(F32)
            xhat, rstd = _ln_fwd(r1_ref[...])
            dg1_ref[0:1, :] += _colsum(dh1 * xhat)
            db1_ref[0:1, :] += _colsum(dh1)
            dr1_ref[...] = _ln_bwd(dh1, xhat, rstd, g1_ref[...]).astype(dr1_ref.dtype)

    row = pl.BlockSpec((tm, d), lambda i, f: (i, 0))
    acc8 = pl.BlockSpec((SUBLANES, d), lambda i, f: (0, 0))
    return pl.pallas_call(
        body, name="mlp_bwd",
        grid=(t // tm, nf),
        in_specs=[row, pl.BlockSpec((tm, tf), lambda i, f: (i, f)), row,
                  pl.BlockSpec((spb, d, sw), lambda i, f: (f, 0, 0)), pl.BlockSpec((tf, d), lambda i, f: (f, 0)),
                  pl.BlockSpec((1, d), lambda i, f: (0, 0))],
        out_specs=[pl.BlockSpec((tm, tf), lambda i, f: (i, f)), row, acc8, acc8],
        out_shape=[jax.ShapeDtypeStruct((t, ff), BF16), jax.ShapeDtypeStruct((t, d), BF16),
                   jax.ShapeDtypeStruct((SUBLANES, d), F32), jax.ShapeDtypeStruct((SUBLANES, d), F32)],
        scratch_shapes=[pltpu.VMEM((tm, d), F32)],
        compiler_params=_params(("arbitrary", "arbitrary")),
    )(dr2, up, r1, w_up, w_down, ln1_g)


def _dx_kernel(segs, w_main, ddt, w_dt, dr1, tm, tk, comm=None):
    t, d = dr1.shape
    nblk = [s.shape[1] // tk for s in segs]
    starts = [sum(nblk[:i]) for i in range(len(segs))]
    nk = sum(nblk)
    nseg = len(segs)

    def body(*refs):
        seg_refs = refs[:nseg]
        w_ref, ddt_ref, wdt_ref, dr1_ref, o_ref, acc = refs[nseg:]
        k = pl.program_id(1)

        @pl.when(k == 0)
        def _():
            acc[...] = ALPHA * dr1_ref[...].astype(F32) + _dot(ddt_ref[...], wdt_ref[...], NT)

        for si in range(nseg):
            @pl.when((k >= starts[si]) & (k < starts[si] + nblk[si]))
            def _(si=si):
                acc[...] += _dot(seg_refs[si][...], w_ref[...], NT)

        @pl.when(k == nk - 1)
        def _():
            o_ref[...] = acc[...]

    def seg_spec(si):
        return pl.BlockSpec((tm, tk), lambda i, k: (i, jnp.clip(k - starts[si], 0, nblk[si] - 1)))

    row = pl.BlockSpec((tm, d), lambda i, k: (i, 0))
    grid = (t // tm, nk)
    c_in, c_in_specs, c_out_specs, c_out_shapes = _comm_specs(comm)
    return pl.pallas_call(
        _fuse_comm(body, grid, nseg + 4, 1, comm), name="dx",
        grid=grid,
        in_specs=[seg_spec(si) for si in range(nseg)] + [
            pl.BlockSpec((d, tk), lambda i, k: (0, k)), pl.BlockSpec((tm, LANES), lambda i, k: (i, 0)),
            pl.BlockSpec((d, LANES), lambda i, k: (0, 0)), row] + c_in_specs,
        out_specs=[row] + c_out_specs,
        out_shape=[jax.ShapeDtypeStruct((t, d), F32)] + c_out_shapes,
        scratch_shapes=[pltpu.VMEM((tm, d), F32)] + (list(comm.scratch) if comm else []),
        compiler_params=_params(("arbitrary", "arbitrary")),
    )(*segs, w_main, ddt, w_dt, dr1, *c_in)


def _dims(d):
    inner = 2 * d
    heads = inner // HEAD_DIM
    cd = inner + 2 * GROUPS * STATE
    assert heads <= LANES and inner % (GROUPS * LANES) == 0 and d % (len(POOL_WINDOWS) * LANES) == 0
    o_z, o_xbc, o_dt, o_lg = d, d + inner, d + inner + cd, d + inner + cd + heads
    return inner, heads, cd, (o_z, o_xbc, o_dt, o_lg)


def _row(v, width=None):
    v = v.reshape(1, -1).astype(F32)
    if width is not None and v.shape[1] < width:
        v = jnp.pad(v, ((0, 0), (0, width - v.shape[1])))
    return v


def _local_step(x2, tgt2, w, shards, core, bl):
    t, d = x2.shape
    inner, heads, cd, _ = _dims(d)
    gs = GROUPS * STATE
    nc = t // bl // CHUNK
    w_main, w_dt = _w_in_internal(w["w_in_blocks"], d)
    c_z, c_lg, c_u = cd, cd + inner, cd + inner + 2 * d
    conv_w8 = jnp.pad(w["conv_w"].astype(F32), ((0, SUBLANES - CONV_K), (0, 0)))
    conv_b = _row(w["conv_b"])
    dtb, alog = _row(w["dt_bias"], LANES), _row(w["a_log"], LANES)
    dskip_x = _row(jnp.repeat(w["d_skip"].reshape(-1), HEAD_DIM))
    normw = _row(w["ssd_norm_w"])
    col_head = lax.broadcasted_iota(jnp.int32, (LANES, inner), 1) // HEAD_DIM
    emat = (col_head == lax.broadcasted_iota(jnp.int32, (LANES, inner), 0)).astype(BF16)
    emat_t = emat.T
    w_main, w_dt = w_main.astype(BF16), w_dt.astype(BF16)
    b_gates, pool_scale = _row(w["b_gates"]), _row(w["pool_scale"])
    ln1_g, ln1_b, ln2_g, ln2_b = _row(w["ln1_g"]), _row(w["ln1_b"]), _row(w["ln2_g"]), _row(w["ln2_b"])

    tm = min(512, t)
    tk = min(1024, d)
    ct = min(512, d)
    rt = min(512, t // bl)
    nct = t // bl // rt
    mm = functools.partial(_matmul, bm=1024, bn=tk, bk=1024)
    mmt = functools.partial(_matmul, bm=1024, bn=tk, bk=2048)
    xb = x2.astype(BF16)

    proj, xbc, dsl, *gathered = _in_proj(xb, w_main, conv_w8, conv_b, cd, t // bl, 1024, tk,
                                    _all_gather_comm([shards[n] for n in OTHERS]))
    gathered = dict(zip(OTHERS, gathered))
    w_ssd, w_out, w_down = (gathered[n].reshape(-1, d) for n in ("w_ssd_proj", "w_out", "w_down"))
    w_up = gathered["w_up"]
    npg = len(POOL_WINDOWS)
    cg = d // npg
    wpg = gathered["w_pool_group"].reshape(N_DEV, npg, cg // N_DEV, cg).transpose(1, 0, 2, 3).reshape(npg, cg, cg)
    dt_raw = mm(xb, w_dt, "nn", F32, name="in_proj_dt")
    y, yn, states = _ssd_fwd(xbc, proj, dt_raw, dtb, alog, dskip_x, normw, emat, bl, inner, c_z)
    yssd = mmt(yn, w_ssd, "nn", BF16, name="ssd_proj")
    ypr = _pool_fwd(proj, wpg, bl, d, c_u)
    merged, r1 = _merge_fwd(proj, ypr, yssd, x2, w_out, b_gates, pool_scale, d, c_lg, tm)
    tmm = min(1024, t)
    up, h1, dr2, loss8, dg2, db2 = _mlp_fwd(r1, tgt2, w_up, w_down, ln1_g, ln1_b, ln2_g, ln2_b, tmm)

    dup, dr1, dg1, db1 = _mlp_bwd(dr2, up, r1, w_up, w_down, ln1_g, tmm)
    relu2 = lambda v: jnp.square(jnp.maximum(v, 0.0))
    g = {}
    g["w_down"] = mmt(up, dr2, "tn", BF16, name="dw_down", a_fn=relu2)
    g["w_up"] = mmt(h1, dup, "tn", BF16, name="dw_up", col_blocks=N_DEV)
    g["w_out"] = mmt(merged, dr1, "tn", BF16, name="dw_out")
    dlg, dyp, dys, dbg, dps = _merge_bwd(dr1, proj, ypr, yssd, w_out, b_gates, pool_scale, d, c_lg, tm)
    du, dwpg = _pool_bwd(proj, dyp, wpg, bl, d, c_u)
    g["w_pool_group"] = dwpg.reshape(npg, N_DEV, cg // N_DEV, cg).transpose(1, 0, 2, 3).reshape(
        N_DEV, npg * cg // N_DEV, cg).astype(BF16)
    dyn = mm(dys, w_ssd, "nt", BF16, name="d_ssd_proj")
    g["w_ssd_proj"] = mmt(yn, dys, "tn", BF16, name="dw_ssd_proj")

    def chip_sums(names, tag):
        parts = [g.pop(n).reshape((N_DEV,) + shards_2d[n]) for n in names]
        recv = _run_comm(_rs_sibling_comm(parts), "rs_sibling_" + tag)
        return [_add_pairs(core, p, r, "rs_add_" + n) for n, p, r in zip(names, parts, recv)]

    shards_2d = {n: s.shape for n, s in shards.items()}
    shards_2d["w_in"] = w["w_in_blocks"].shape[1:]
    dxs, dbm, dcm, dz, ddt, dnw, dsk, dalog, ddtb, *recv_others = _ssd_bwd(
        xbc, proj, dt_raw, y, dyn, states, dtb, alog, dskip_x, normw, emat, emat_t, bl, inner, c_z,
        comm=_rs_chips_comm(chip_sums(OTHERS, "a")))
    dxs_p, dcw_x, dcb_x = _conv_bwd(proj, dsl, dxs, conv_w8, nct, 0, inner, ct, rt, "conv_bwd_x")
    dbm_p, dcw_b, dcb_b = _conv_bwd(proj, dsl, dbm, conv_w8, nct, inner, gs, ct, rt, "conv_bwd_b")
    dcm_p, dcw_c, dcb_c = _conv_bwd(proj, dsl, dcm, conv_w8, nct, inner + gs, gs, ct, rt, "conv_bwd_c")
    segs = [dxs_p, dbm_p, dcm_p, dz, dlg, du]
    keys = [k for k, _, _ in _col_segments(d)]
    dws = {k: mmt(xb, s, "tn", BF16, name="dw_in_" + k) for k, s in zip(keys, segs + [ddt])}
    g["w_in"] = _w_in_grad_blocks(dws, d, w["w_in_blocks"].shape[2])
    grad_x, recv_w_in = _dx_kernel(segs, w_main, ddt, w_dt, dr1, tmm, tk,
                                   comm=_rs_chips_comm(chip_sums(["w_in"], "b")))
    recv = dict(zip(OTHERS, recv_others))
    recv["w_in"] = recv_w_in
    g["conv_w"] = jnp.concatenate([dcw_x, dcw_b, dcw_c], axis=1)[:CONV_K]
    g["conv_b"] = jnp.concatenate([dcb_x, dcb_b, dcb_c], axis=1)[0]
    g["b_gates"], g["pool_scale"] = dbg[0], dps[0]
    g["dt_bias"], g["a_log"], g["d_skip"] = ddtb[0, :heads], dalog[0, :heads], dsk[0, :heads]
    g["ssd_norm_w"] = dnw[:, 0, :].reshape(inner)
    g["ln1_g"], g["ln1_b"], g["ln2_g"], g["ln2_b"] = dg1[0], db1[0], dg2[0], db2[0]
    return loss8, grad_x, g, recv


BIG = ("w_in", "w_ssd_proj", "w_pool_group", "w_out", "w_up", "w_down")
OTHERS = BIG[1:]
SMALL = ("b_gates", "conv_b", "dt_bias", "a_log", "d_skip", "ssd_norm_w", "pool_scale", "ln1_g", "ln1_b", "ln2_g",
         "ln2_b")
SMALL_PACK = SMALL + ("conv_w",)
NAMES = ("w_in", "b_gates", "conv_w", "conv_b", "dt_bias", "a_log", "d_skip", "ssd_norm_w", "w_ssd_proj",
         "w_pool_group", "pool_scale", "w_out", "ln1_g", "ln1_b", "w_up", "w_down", "ln2_g", "ln2_b")


def _size(shape):
    n = 1
    for s in shape:
        n *= s
    return n


def _rows128(v):
    v = v.astype(F32).reshape((-1, v.shape[-1]))
    n = v.shape[-1]
    v = jnp.pad(v, ((0, 0), (0, -n % LANES)))
    return v.reshape(-1, LANES)


def _pack_small(vals, extra):
    parts = [_rows128(vals[n]) for n in SMALL_PACK]
    parts.append(jnp.pad(extra.reshape(1, 1).astype(F32), ((0, 0), (0, LANES - 1))))
    buf = jnp.concatenate(parts, axis=0)
    return jnp.pad(buf, ((0, -buf.shape[0] % SUBLANES), (0, 0)))


def _unpack_small(buf, shapes):
    out, off = {}, 0
    for n in SMALL_PACK:
        lead, last = _size(shapes[n][:-1]), shapes[n][-1]
        per = -(-last // LANES)
        out[n] = buf[off:off + lead * per].reshape(lead, per * LANES)[:, :last].reshape(shapes[n])
        off += lead * per
    return out, buf[off, 0]


def _col_segments(d):
    inner, heads, cd, (o_z, o_xbc, o_dt, o_lg) = _dims(d)
    gs = GROUPS * STATE
    return [("xs", o_xbc, inner), ("B", o_xbc + inner, gs), ("C", o_xbc + inner + gs, gs), ("z", o_z, inner),
            ("lg", o_lg, 2 * d), ("u", 0, d), ("dt", o_dt, heads)]


def _cols_from_blocks(blocks, start, width, bw):
    parts, pos = [], start
    while pos < start + width:
        k, off = divmod(pos, bw)
        n = min(bw - off, start + width - pos)
        parts.append(blocks[k][:, off:off + n])
        pos += n
    return parts


def _w_in_internal(blocks, d):
    bw = blocks.shape[2]
    segs = _col_segments(d)
    heads = segs[-1][2]
    main = [p for _, s, w_ in segs[:-1] for p in _cols_from_blocks(blocks, s, w_, bw)]
    w_dt = jnp.concatenate(_cols_from_blocks(blocks, segs[-1][1], heads, bw), axis=1)
    return jnp.concatenate(main, axis=1), jnp.pad(w_dt, ((0, 0), (0, LANES - heads)))


def _w_in_grad_blocks(dws, d, bw):
    order = sorted(_col_segments(d), key=lambda s: s[1])
    blocks = []
    for k in range(N_DEV):
        lo, hi, parts = k * bw, (k + 1) * bw, []
        for key, s, w_ in order:
            a, b = max(lo, s), min(hi, s + w_)
            if a < b:
                parts.append(dws[key][:, a - s:b - s])
        blocks.append(jnp.concatenate(parts, axis=1))
    return jnp.stack(blocks)


def _mesh_pos():
    return lax.axis_index("x"), lax.axis_index("y"), lax.axis_index("c")


def _all_gather_comm(shards):
    nw = len(shards)

    def setup(x_refs, out_refs, scr):
        send_sems, recv_sems, local_sems = scr
        x, y, c = _mesh_pos()
        me, sibling = (x, y, c), (x, y, 1 - c)
        chips = [(1 - x, y), (x, 1 - y), (1 - x, 1 - y)]

        def copy(wi, k, block, to, from_input=False):
            px, py, pc = block
            blk = out_refs[wi].at[4 * px + 2 * py + pc]
            return pltpu.make_async_remote_copy(
                src_ref=x_refs[wi] if from_input else blk, dst_ref=blk,
                send_sem=send_sems.at[7 * wi + k], recv_sem=recv_sems.at[7 * wi + k], device_id=to,
                device_id_type=MESH)

        mine = [pltpu.make_async_copy(x_refs[wi], out_refs[wi].at[4 * x + 2 * y + c], local_sems.at[wi])
                for wi in range(nw)]
        sends = []
        for wi in range(nw):
            sends.append(copy(wi, 0, me, sibling, True))
            sends += [copy(wi, 1 + j, me, (*chip, c), True) for j, chip in enumerate(chips)]
        return copy, mine, sends, me, sibling, chips, c

    def start(x_refs, out_refs, scr):
        _, mine, sends, _, _, _, _ = setup(x_refs, out_refs, scr)
        for cp in mine + sends:
            cp.start()

    def wait(x_refs, out_refs, scr):
        copy, mine, sends, me, sibling, chips, c = setup(x_refs, out_refs, scr)
        passed = []
        for wi in range(nw):
            for j, chip in enumerate(chips):
                copy(wi, 1 + j, (*chip, c), me).wait_recv()
                passed.append(copy(wi, 4 + j, (*chip, c), sibling))
                passed[-1].start()
        for wi in range(nw):
            copy(wi, 0, sibling, me).wait_recv()
            for j, chip in enumerate(chips):
                copy(wi, 4 + j, (*chip, 1 - c), me).wait_recv()
        for cp in sends + passed:
            cp.wait_send()
        for cp in mine:
            cp.wait()

    return _Comm(
        inputs=list(shards),
        out_shapes=[jax.ShapeDtypeStruct((N_DEV,) + s.shape, s.dtype) for s in shards],
        scratch=[pltpu.SemaphoreType.DMA((7 * nw,)), pltpu.SemaphoreType.DMA((7 * nw,)),
                 pltpu.SemaphoreType.DMA((nw,))],
        start=start, wait=wait)


def _rs_sibling_comm(parts):
    nw = len(parts)
    half = N_DEV // 2

    def copies(p_refs, recv_refs, scr):
        send_sems, recv_sems = scr
        x, y, c = _mesh_pos()
        return [pltpu.make_async_remote_copy(
            src_ref=p_refs[wi].at[2 * q + 1 - c], dst_ref=recv_refs[wi].at[q],
            send_sem=send_sems.at[half * wi + q], recv_sem=recv_sems.at[half * wi + q],
            device_id=(x, y, 1 - c), device_id_type=MESH) for wi in range(nw) for q in range(half)]

    def start(p_refs, recv_refs, scr):
        for cp in copies(p_refs, recv_refs, scr):
            cp.start()

    def wait(p_refs, recv_refs, scr):
        for cp in copies(p_refs, recv_refs, scr):
            cp.wait()

    return _Comm(
        inputs=list(parts),
        out_shapes=[jax.ShapeDtypeStruct((half,) + p.shape[1:], p.dtype) for p in parts],
        scratch=[pltpu.SemaphoreType.DMA((half * nw,)), pltpu.SemaphoreType.DMA((half * nw,))],
        start=start, wait=wait)


def _rs_chips_comm(tbs):
    nw = len(tbs)

    def copies(t_refs, o_refs, scr):
        send_sems, recv_sems, local_sems = scr
        x, y, c = _mesh_pos()
        p = 2 * x + y
        chips = [(1 - x, y), (x, 1 - y), (1 - x, 1 - y)]
        own = [pltpu.make_async_copy(t_refs[wi].at[p], o_refs[wi].at[p], local_sems.at[wi]) for wi in range(nw)]
        remote = [pltpu.make_async_remote_copy(
            src_ref=t_refs[wi].at[2 * qx + qy], dst_ref=o_refs[wi].at[p], send_sem=send_sems.at[3 * wi + j],
            recv_sem=recv_sems.at[3 * wi + j], device_id=(qx, qy, c), device_id_type=MESH)
            for wi in range(nw) for j, (qx, qy) in enumerate(chips)]
        arriving = [pltpu.make_async_remote_copy(
            src_ref=t_refs[wi].at[p], dst_ref=o_refs[wi].at[2 * qx + qy], send_sem=send_sems.at[3 * wi + j],
            recv_sem=recv_sems.at[3 * wi + j], device_id=(qx, qy, c), device_id_type=MESH)
            for wi in range(nw) for j, (qx, qy) in enumerate(chips)]
        return own, remote, arriving

    def start(t_refs, o_refs, scr):
        own, remote, _ = copies(t_refs, o_refs, scr)
        for cp in own + remote:
            cp.start()

    def wait(t_refs, o_refs, scr):
        own, remote, arriving = copies(t_refs, o_refs, scr)
        for cp in arriving:
            cp.wait_recv()
        for cp in remote:
            cp.wait_send()
        for cp in own:
            cp.wait()

    return _Comm(
        inputs=list(tbs),
        out_shapes=[jax.ShapeDtypeStruct(t_.shape, t_.dtype) for t_ in tbs],
        scratch=[pltpu.SemaphoreType.DMA((3 * nw,)), pltpu.SemaphoreType.DMA((3 * nw,)),
                 pltpu.SemaphoreType.DMA((nw,))],
        start=start, wait=wait)


def _row_tile(rows, cap=256):
    if rows <= cap:
        return rows
    return max(t_ for t_ in range(SUBLANES, cap + 1, SUBLANES) if rows % t_ == 0)


def _add_pairs(core, part, recv, name):
    n, r, c_ = recv.shape
    tr = _row_tile(r)

    def body(core_ref, a_ref, b_ref, o_ref):
        o_ref[...] = (a_ref[...].astype(F32) + b_ref[...].astype(F32)).astype(o_ref.dtype)

    spec = pl.BlockSpec((1, tr, c_), lambda q, i, core_ref: (q, i, 0))
    return pl.pallas_call(
        body, name=name,
        grid_spec=pltpu.PrefetchScalarGridSpec(
            num_scalar_prefetch=1, grid=(n, r // tr),
            in_specs=[pl.BlockSpec((1, tr, c_), lambda q, i, core_ref: (2 * q + core_ref[0], i, 0)), spec],
            out_specs=spec),
        out_shape=jax.ShapeDtypeStruct(recv.shape, BF16), compiler_params=_params(("parallel", "parallel")),
    )(core, part, recv)


def _small_allreduce(vec, name):
    rows = vec.shape[0]

    def body(x_ref, o_ref, buf, send_sems, recv_sems):
        x, y, c = _mesh_pos()
        me = 4 * x + 2 * y + c
        buf[me] = x_ref[...]
        cps = []
        for k in range(1, N_DEV):
            peer = (1 - x if k & 4 else x, 1 - y if k & 2 else y, 1 - c if k & 1 else c)
            cps.append(pltpu.make_async_remote_copy(
                src_ref=x_ref, dst_ref=buf.at[me], send_sem=send_sems.at[k - 1], recv_sem=recv_sems.at[k - 1],
                device_id=peer, device_id_type=MESH))
        for cp in cps:
            cp.start()
        for k in range(1, N_DEV):
            px, py, pc = (1 - x if k & 4 else x, 1 - y if k & 2 else y, 1 - c if k & 1 else c)
            pltpu.make_async_remote_copy(
                src_ref=x_ref, dst_ref=buf.at[4 * px + 2 * py + pc], send_sem=send_sems.at[k - 1],
                recv_sem=recv_sems.at[k - 1], device_id=(px, py, pc), device_id_type=MESH).wait_recv()
        for cp in cps:
            cp.wait_send()
        acc = buf[0]
        for k in range(1, N_DEV):
            acc = acc + buf[k]
        o_ref[...] = acc

    vm = pl.BlockSpec(memory_space=pltpu.VMEM)
    return pl.pallas_call(
        body, name=name,
        in_specs=[vm], out_specs=vm,
        out_shape=jax.ShapeDtypeStruct(vec.shape, F32),
        scratch_shapes=[pltpu.VMEM((N_DEV, rows, LANES), F32), pltpu.SemaphoreType.DMA((N_DEV - 1,)),
                        pltpu.SemaphoreType.DMA((N_DEV - 1,))],
    )(vec)


def _adamw(gparts, w, m, v, name):
    n, r, c_ = gparts.shape
    tr = _row_tile(r)
    c1 = 1.0 / (1.0 - B1 ** STEP)
    c2 = 1.0 / (1.0 - B2 ** STEP)

    def body(g_ref, w_ref, m_ref, v_ref, go_ref, d_ref, mo_ref, vo_ref):
        g = g_ref[0].astype(F32)
        for q in range(1, n):
            g = g + g_ref[q].astype(F32)
        mn = B1 * m_ref[...] + (1.0 - B1) * g
        vn = B2 * v_ref[...] + (1.0 - B2) * (g * g)
        go_ref[...] = g
        mo_ref[...] = mn
        vo_ref[...] = vn
        d_ref[...] = -LR * ((mn * c1) / (jnp.sqrt(vn * c2) + ADAM_EPS) + WD * w_ref[...])

    spec = pl.BlockSpec((tr, c_), lambda i: (i, 0))
    out = jax.ShapeDtypeStruct((r, c_), F32)
    return pl.pallas_call(
        body, name=name, grid=(r // tr,),
        in_specs=[pl.BlockSpec((n, tr, c_), lambda i: (0, i, 0)), spec, spec, spec],
        out_specs=[spec] * 4, out_shape=[out] * 4, compiler_params=_params(("parallel",)),
    )(gparts, w, m, v)


def kernel(x, w_in, b_gates, conv_w, conv_b, dt_bias, a_log, d_skip, ssd_norm_w, w_ssd_proj, w_pool_group, pool_scale, w_out, ln1_g, ln1_b, w_up, w_down, ln2_g, ln2_b, loss_target, m_w_in, m_b_gates, m_conv_w, m_conv_b, m_dt_bias, m_a_log, m_d_skip, m_ssd_norm_w, m_w_ssd_proj, m_w_pool_group, m_pool_scale, m_w_out, m_ln1_g, m_ln1_b, m_w_up, m_w_down, m_ln2_g, m_ln2_b, v_w_in, v_b_gates, v_conv_w, v_conv_b, v_dt_bias, v_a_log, v_d_skip, v_ssd_norm_w, v_w_ssd_proj, v_w_pool_group, v_pool_scale, v_w_out, v_ln1_g, v_ln1_b, v_w_up, v_w_down, v_ln2_g, v_ln2_b):
    ws = (w_in, b_gates, conv_w, conv_b, dt_bias, a_log, d_skip, ssd_norm_w, w_ssd_proj, w_pool_group, pool_scale,
          w_out, ln1_g, ln1_b, w_up, w_down, ln2_g, ln2_b)
    ms = (m_w_in, m_b_gates, m_conv_w, m_conv_b, m_dt_bias, m_a_log, m_d_skip, m_ssd_norm_w, m_w_ssd_proj,
          m_w_pool_group, m_pool_scale, m_w_out, m_ln1_g, m_ln1_b, m_w_up, m_w_down, m_ln2_g, m_ln2_b)
    vs = (v_w_in, v_b_gates, v_conv_w, v_conv_b, v_dt_bias, v_a_log, v_d_skip, v_ssd_norm_w, v_w_ssd_proj,
          v_w_pool_group, v_pool_scale, v_w_out, v_ln1_g, v_ln1_b, v_w_up, v_w_down, v_ln2_g, v_ln2_b)
    w = {n: a[0] for n, a in zip(NAMES, ws)}
    m = {n: a[0] for n, a in zip(NAMES, ms)}
    v = {n: a[0] for n, a in zip(NAMES, vs)}
    out_shapes = {n: a.shape for n, a in zip(NAMES, ws)}
    bl, s, d = x.shape
    x2, tgt2 = x.reshape(bl * s, d), loss_target.reshape(bl * s, d)
    xi, yi, ci = _mesh_pos()
    me = 4 * xi + 2 * yi + ci
    zero = jnp.zeros((), F32)
    shapes = {n: w[n].shape for n in NAMES}
    shape2d = {n: (_size(shapes[n][:-1]), shapes[n][-1]) for n in BIG}
    cwl = shapes["conv_w"][1]

    shards = {n: w[n].astype(BF16).reshape(shape2d[n]) for n in BIG}
    full = {n: w[n] for n in SMALL}
    full["w_in_blocks"], conv_blocks = _run_comm(_all_gather_comm([shards.pop("w_in"), w["conv_w"]]),
                                                 "all_gather_w_in")
    full["conv_w"] = conv_blocks.transpose(1, 0, 2).reshape(CONV_K, N_DEV * cwl)
    loss8, grad_x, g, recv = _local_step(x2, tgt2, full, shards, ci.astype(jnp.int32).reshape(1), bl)

    small_sum = _small_allreduce(_pack_small(g, loss8[0, 0]), "small_allreduce")
    ex_shapes = {n: shapes[n] for n in SMALL}
    ex_shapes["conv_w"] = (CONV_K, N_DEV * cwl)
    gsum, loss = _unpack_small(small_sum, ex_shapes)
    gsum["conv_w"] = lax.dynamic_slice(gsum["conv_w"], (0, me * cwl), (CONV_K, cwl))
    gs_pk = _pack_small(gsum, zero)
    ws_pk, ms_pk, vs_pk = (_pack_small(t_, zero) for t_ in (w, m, v))
    small_out = _adamw(gs_pk[None], ws_pk, ms_pk, vs_pk, "adamw_small")
    loc_shapes = {n: shapes[n] for n in SMALL_PACK}
    res = [_unpack_small(o, loc_shapes)[0] for o in small_out]

    for n in BIG:
        outs = _adamw(recv[n], *(t_[n].reshape(shape2d[n]) for t_ in (w, m, v)), "adamw_" + n)
        for r_, o in zip(res, outs):
            r_[n] = o

    def ordered(r_):
        return [r_[n].reshape(out_shapes[n]) for n in NAMES]

    return (loss, grad_x.reshape(bl, s, d), *ordered(res[0]), *ordered(res[1]), *ordered(res[2]), *ordered(res[3]))
```

```python
import collections
import functools

import jax
import jax.numpy as jnp
from jax import lax
from jax.experimental import pallas as pl
from jax.experimental.pallas import tpu as pltpu

F32 = jnp.float32
BF16 = jnp.bfloat16
MESH = pl.DeviceIdType.MESH

HEAD_DIM = 64
STATE = 128
GROUPS = 8
CONV_K = 4
CHUNK = 256
POOL_WINDOWS = (2, 4, 8, 16)
ALPHA = 2.0 ** 0.25
LN_EPS = 1e-5
RMS_EPS = 1e-5
LR, B1, B2, ADAM_EPS, WD, STEP = 0.001, 0.9, 0.999, 1e-08, 0.01, 10
N_DEV = 8
LANES = 128
SUBLANES = 8
VMEM_LIMIT = 56 * 1024 * 1024
NEG_BIG = -1e30

NN = (((1,), (0,)), ((), ()))
NT = (((1,), (1,)), ((), ()))
TN = (((0,), (0,)), ((), ()))


def _dot(a, b, dims=NN):
    return lax.dot_general(a.astype(BF16), b.astype(BF16), dims, preferred_element_type=F32)


def _dot_exact01(q, e, dims=NN):
    hi = q.astype(BF16)
    r1 = q - hi.astype(F32)
    mid = r1.astype(BF16)
    lo = (r1 - mid.astype(F32)).astype(BF16)
    f = lambda p: lax.dot_general(p, e, dims, preferred_element_type=F32)
    return f(hi) + f(mid) + f(lo)


def _params(sem):
    return pltpu.CompilerParams(dimension_semantics=sem, vmem_limit_bytes=VMEM_LIMIT)


def _sigmoid(x):
    return 1.0 / (1.0 + jnp.exp(-x))


def _colsum(x):
    return jnp.sum(x, axis=0, keepdims=True)


def _ln_fwd(r):
    mu = jnp.mean(r, axis=-1, keepdims=True)
    xc = r - mu
    var = jnp.mean(xc * xc, axis=-1, keepdims=True)
    rstd = lax.rsqrt(var + LN_EPS)
    return xc * rstd, rstd


def _ln_bwd(dy, xhat, rstd, g):
    dxh = dy * g
    m1 = jnp.mean(dxh, axis=-1, keepdims=True)
    m2 = jnp.mean(dxh * xhat, axis=-1, keepdims=True)
    return rstd * (dxh - m1 - xhat * m2)


_Comm = collections.namedtuple("_Comm", "inputs out_shapes scratch start wait")
ANY = pl.BlockSpec(memory_space=pl.ANY)


def _fuse_comm(body, grid, n_in, n_out, comm):
    if comm is None:
        return body
    ci, co = len(comm.inputs), len(comm.out_shapes)

    def fused(*refs):
        ins, cins = refs[:n_in], refs[n_in:n_in + ci]
        o0 = n_in + ci
        outs, couts = refs[o0:o0 + n_out], refs[o0 + n_out:o0 + n_out + co]
        rest = refs[o0 + n_out + co:]
        scr, cscr = rest[:len(rest) - len(comm.scratch)], rest[len(rest) - len(comm.scratch):]
        ids = [pl.program_id(a) for a in range(len(grid))]
        first, last = ids[0] == 0, ids[0] == grid[0] - 1
        for a in range(1, len(grid)):
            first, last = first & (ids[a] == 0), last & (ids[a] == grid[a] - 1)

        @pl.when(first)
        def _():
            comm.start(cins, couts, cscr)

        body(*ins, *outs, *scr)

        @pl.when(last)
        def _():
            comm.wait(cins, couts, cscr)

    return fused


def _comm_specs(comm):
    if comm is None:
        return [], [], [], []
    return list(comm.inputs), [ANY] * len(comm.inputs), [ANY] * len(comm.out_shapes), list(comm.out_shapes)


def _run_comm(comm, name):
    ci, co = len(comm.inputs), len(comm.out_shapes)

    def body(*refs):
        comm.start(refs[:ci], refs[ci:ci + co], refs[ci + co:])
        comm.wait(refs[:ci], refs[ci:ci + co], refs[ci + co:])

    return pl.pallas_call(body, name=name, in_specs=[ANY] * ci, out_specs=[ANY] * co, out_shape=list(comm.out_shapes),
                          scratch_shapes=list(comm.scratch))(*comm.inputs)


def _matmul(a, b, mode, out_dtype, bm, bn, bk, name, a_fn=None, col_blocks=0, comm=None):
    if mode == "nn":
        (m, k), n, dims = a.shape, b.shape[1], NN
    elif mode == "nt":
        (m, k), n, dims = a.shape, b.shape[0], NT
    else:
        (k, m), n, dims = a.shape, b.shape[1], TN
    bm, bn, bk = min(bm, m), min(bn, n), min(bk, k)
    assert m % bm == 0 and n % bn == 0 and k % bk == 0, (name, m, n, k, bm, bn, bk)
    nk = k // bk
    if mode == "nn":
        a_spec = pl.BlockSpec((bm, bk), lambda i, j, kk: (i, kk))
        b_spec = pl.BlockSpec((bk, bn), lambda i, j, kk: (kk, j))
    elif mode == "nt":
        a_spec = pl.BlockSpec((bm, bk), lambda i, j, kk: (i, kk))
        b_spec = pl.BlockSpec((bn, bk), lambda i, j, kk: (j, kk))
    else:
        a_spec = pl.BlockSpec((bk, bm), lambda i, j, kk: (kk, i))
        b_spec = pl.BlockSpec((bk, bn), lambda i, j, kk: (kk, j))

    def body(a_ref, b_ref, o_ref, acc_ref):
        kk = pl.program_id(2)
        av = a_ref[...]
        if a_fn is not None:
            av = a_fn(av.astype(F32))
        prod = _dot(av, b_ref[...], dims)

        def emit(total):
            if col_blocks:
                for s in range(bn // slab):
                    o_ref[s] = total[:, s * slab:(s + 1) * slab].astype(o_ref.dtype)
            else:
                o_ref[...] = total.astype(o_ref.dtype)

        if nk == 1:
            emit(prod)
        else:
            @pl.when(kk == 0)
            def _():
                acc_ref[...] = prod

            @pl.when((kk > 0) & (kk < nk - 1))
            def _():
                acc_ref[...] += prod

            @pl.when(kk == nk - 1)
            def _():
                emit(acc_ref[...] + prod)

    if col_blocks:
        slab = n // col_blocks
        assert n % col_blocks == 0 and bn % slab == 0, (name, n, col_blocks, bn)
        out_spec = pl.BlockSpec((bn // slab, bm, slab), lambda i, j, kk: (j, i, 0))
        out_shape = jax.ShapeDtypeStruct((col_blocks, m, slab), out_dtype)
    else:
        out_spec = pl.BlockSpec((bm, bn), lambda i, j, kk: (i, j))
        out_shape = jax.ShapeDtypeStruct((m, n), out_dtype)
    grid = (m // bm, n // bn, nk)
    c_in, c_in_specs, c_out_specs, c_out_shapes = _comm_specs(comm)
    res = pl.pallas_call(
        _fuse_comm(body, grid, 2, 1, comm), name=name,
        grid=grid,
        in_specs=[a_spec, b_spec] + c_in_specs,
        out_specs=[out_spec] + c_out_specs,
        out_shape=[out_shape] + c_out_shapes,
        scratch_shapes=[pltpu.VMEM((bm, bn), F32)] + (list(comm.scratch) if comm else []),
        compiler_params=_params(("arbitrary",) * 3 if comm else ("parallel", "parallel", "arbitrary")),
    )(a, b, *c_in)
    return res if comm else res[0]


CONV_STRIP = 16
CONV_COLS = 512


def _conv_pre(ext_ref, w_ref, b_ref, r0, rows, cols=slice(None)):
    n = rows + SUBLANES
    win = ext_ref[pl.ds(r0, n), cols]
    acc = b_ref[:, cols] + w_ref[CONV_K - 1:CONV_K, cols] * win[SUBLANES:, :]
    for k in range(CONV_K - 1):
        off = SUBLANES - (CONV_K - 1) + k
        acc = acc + w_ref[k:k + 1, cols] * pltpu.roll(win, n - off, 0)[0:rows, :]
    return acc


def _in_proj(xb, w_main, conv_w8, conv_b, cd, seq_len, bm, bn, comm):
    t, d = xb.shape
    pw = w_main.shape[1]
    bm, bn = min(bm, seq_len), min(bn, d)
    assert t % bm == 0 and seq_len % bm == 0 and pw % bn == 0 and cd % bn == 0
    ncj = cd // bn
    tiles_per_seq = seq_len // bm

    def body(x_ref, w_ref, cw_ref, cb_ref, p_ref, xbc_ref, dsl_ref, ext_ref, carry_ref):
        i = pl.program_id(0)
        j = pl.program_id(1)
        pq = _dot(x_ref[...], w_ref[...]).astype(BF16)
        p_ref[...] = pq

        @pl.when(j < ncj)
        def _():
            jc = jnp.minimum(j, ncj - 1)
            ext_ref[0:SUBLANES, :] = jnp.where((i % tiles_per_seq) == 0, 0.0, carry_ref[jc])
            ext_ref[SUBLANES:, :] = pq.astype(F32)
            carry_ref[jc] = ext_ref[bm:bm + SUBLANES, :]
            cw = min(bn, CONV_COLS)
            for c0 in range(0, bn, cw):
                cols = slice(c0, c0 + cw)
                for r0 in range(0, bm, CONV_STRIP):
                    rows = slice(r0, r0 + CONV_STRIP)
                    acc = _conv_pre(ext_ref, cw_ref, cb_ref, r0, CONV_STRIP, cols)
                    sg = _sigmoid(acc)
                    xbc_ref[rows, cols] = (acc * sg).astype(xbc_ref.dtype)
                    dsl_ref[rows, cols] = (sg * (1.0 + acc * (1.0 - sg))).astype(dsl_ref.dtype)

    grid = (t // bm, pw // bn)
    conv_col = lambda i, j: (0, jnp.minimum(j, ncj - 1))
    c_in, c_in_specs, c_out_specs, c_out_shapes = _comm_specs(comm)
    conv_tile = pl.BlockSpec((bm, bn), lambda i, j: (i, jnp.minimum(j, ncj - 1)))
    conv_out = jax.ShapeDtypeStruct((t, cd), BF16)
    return pl.pallas_call(
        _fuse_comm(body, grid, 4, 3, comm), name="in_proj",
        grid=grid,
        in_specs=[pl.BlockSpec((bm, d), lambda i, j: (i, 0)), pl.BlockSpec((d, bn), lambda i, j: (0, j)),
                  pl.BlockSpec((SUBLANES, bn), conv_col), pl.BlockSpec((1, bn), conv_col)] + c_in_specs,
        out_specs=[pl.BlockSpec((bm, bn), lambda i, j: (i, j)), conv_tile, conv_tile] + c_out_specs,
        out_shape=[jax.ShapeDtypeStruct((t, pw), BF16), conv_out, conv_out] + c_out_shapes,
        scratch_shapes=[pltpu.VMEM((bm + SUBLANES, bn), F32), pltpu.VMEM((ncj, SUBLANES, bn), F32)]
        + (list(comm.scratch) if comm else []),
        compiler_params=_params(("arbitrary", "arbitrary")),
    )(xb, w_main, conv_w8, conv_b, *c_in)


def _conv_bwd(proj, dsilu, dxbc, conv_w8, n_seq_chunks, col0, width, ct, L, name):
    t = proj.shape[0]
    nbc = t // L
    hb = L // SUBLANES
    ct = min(ct, width)
    assert col0 % ct == 0 and width % ct == 0
    cb0 = col0 // ct
    last_hb = t // SUBLANES - 1

    def body(x_ref, xb_ref, s_ref, sa_ref, d_ref, da_ref, w_ref, o_ref, dw_ref, db_ref, ext_ref, dc_ref):
        bc = pl.program_id(1)
        first = (bc % n_seq_chunks) == 0
        last = (bc % n_seq_chunks) == n_seq_chunks - 1

        @pl.when(bc == 0)
        def _():
            dw_ref[...] = jnp.zeros_like(dw_ref)
            db_ref[...] = jnp.zeros_like(db_ref)

        ext_ref[0:SUBLANES, :] = jnp.where(first, 0.0, xb_ref[...].astype(F32))
        ext_ref[SUBLANES:, :] = x_ref[...].astype(F32)
        for r0 in range(0, L, CONV_STRIP):
            rows = slice(r0, r0 + CONV_STRIP)
            dc_ref[rows, :] = d_ref[rows, :].astype(F32) * s_ref[rows, :].astype(F32)
        dc_ref[L:, :] = jnp.where(last, 0.0, da_ref[...].astype(F32)) * sa_ref[...].astype(F32)
        fold = lambda v: v[0:SUBLANES] + v[SUBLANES:CONV_STRIP]
        dws = [jnp.zeros((SUBLANES, ct), F32) for _ in range(CONV_K)]
        dbs = jnp.zeros((SUBLANES, ct), F32)
        for r0 in range(0, L, CONV_STRIP):
            dc = dc_ref[r0:r0 + CONV_STRIP, :]
            dx = w_ref[CONV_K - 1:CONV_K, :] * dc
            for k in range(CONV_K - 1):
                dx = dx + w_ref[k:k + 1, :] * dc_ref[pl.ds(r0 + CONV_K - 1 - k, CONV_STRIP), :]
            o_ref[r0:r0 + CONV_STRIP, :] = dx.astype(o_ref.dtype)
            for k in range(CONV_K):
                dws[k] = dws[k] + fold(dc * ext_ref[pl.ds(r0 + SUBLANES - (CONV_K - 1) + k, CONV_STRIP), :])
            dbs = dbs + fold(dc)
        for k in range(CONV_K):
            dw_ref[k:k + 1, :] += _colsum(dws[k])
        db_ref[0:1, :] += _colsum(dbs)

    return pl.pallas_call(
        body, name=name,
        grid=(width // ct, nbc),
        in_specs=[
            pl.BlockSpec((L, ct), lambda j, i: (i, cb0 + j)),
            pl.BlockSpec((SUBLANES, ct), lambda j, i: (jnp.maximum(i * hb - 1, 0), cb0 + j)),
            pl.BlockSpec((L, ct), lambda j, i: (i, cb0 + j)),
            pl.BlockSpec((SUBLANES, ct), lambda j, i: (jnp.minimum((i + 1) * hb, last_hb), cb0 + j)),
            pl.BlockSpec((L, ct), lambda j, i: (i, j)),
            pl.BlockSpec((SUBLANES, ct), lambda j, i: (jnp.minimum((i + 1) * hb, last_hb), j)),
            pl.BlockSpec((SUBLANES, ct), lambda j, i: (0, cb0 + j)),
        ],
        out_specs=[
            pl.BlockSpec((L, ct), lambda j, i: (i, j)),
            pl.BlockSpec((SUBLANES, ct), lambda j, i: (0, j)),
            pl.BlockSpec((SUBLANES, ct), lambda j, i: (0, j)),
        ],
        out_shape=[
            jax.ShapeDtypeStruct((t, width), BF16),
            jax.ShapeDtypeStruct((SUBLANES, width), F32),
            jax.ShapeDtypeStruct((SUBLANES, width), F32),
        ],
        scratch_shapes=[pltpu.VMEM((L + SUBLANES, ct), F32), pltpu.VMEM((L + SUBLANES, ct), F32)],
        compiler_params=_params(("parallel", "arbitrary")),
    )(proj, proj, dsilu, dsilu, dxbc, dxbc, conv_w8)


def _cumsum_rows(x, reverse=False):
    n = x.shape[0]
    row = lax.broadcasted_iota(jnp.int32, x.shape, 0)
    s = 1
    while s < n:
        if reverse:
            x = x + jnp.where(row < n - s, pltpu.roll(x, n - s, 0), 0.0)
        else:
            x = x + jnp.where(row >= s, pltpu.roll(x, s, 0), 0.0)
        s *= 2
    return x


def _ssd_scalars(dtr, dtb, alog):
    pre = dtr + dtb
    dt = jnp.maximum(pre, 0.0) + jnp.log(1.0 + jnp.exp(-jnp.abs(pre)))
    a = -jnp.exp(alog)
    acs = _cumsum_rows(dt * a) * LOG2E
    n = acs.shape[0]
    return pre, dt, a, acs, jnp.exp2(acs), jnp.exp2(acs[n - 1:n, :] - acs)


LOG2E = 1.4426950408889634


def _dot_2piece(q, e):
    hi = q.astype(BF16)
    mid = (q - hi.astype(F32)).astype(BF16)
    return lax.dot_general(jnp.concatenate([hi, mid], axis=1), jnp.concatenate([e, e], axis=0), NN,
                           preferred_element_type=F32)


def _ssd_group_common(dt_s, e_s, dec_s, e):
    return _dot_2piece(dt_s, e), _dot_2piece(e_s, e), _dot_2piece(dec_s, e)


def _decay_matrix(acs, acs_t, h, tri):
    return jnp.exp2(jnp.where(tri, acs[:, h:h + 1] - acs_t[h:h + 1, :], NEG_BIG))


def _head_mask(r, gw, dtype):
    lane = lax.broadcasted_iota(jnp.int32, (1, gw), 1)
    return ((lane >= r * HEAD_DIM) & (lane < (r + 1) * HEAD_DIM)).astype(dtype)


def _ssd_fwd(xbc, proj, dt_raw, dtb, alog, dskip_x, normw, emat, bl, inner, z_col0):
    t = xbc.shape[0]
    L = CHUNK
    nc = t // bl // L
    G = GROUPS
    gw = inner // G
    hpg = gw // HEAD_DIM
    assert z_col0 % gw == 0
    zb0 = z_col0 // gw
    bb0 = inner // STATE
    cb0 = bb0 + G

    P = G
    assert bb0 % P == 0 and cb0 % P == 0 and zb0 % P == 0

    def body(xs_ref, b_ref, c_ref, z_ref, dtr_ref, dtb_ref, alog_ref, dsk_ref, nw_ref, e_ref,
             y_ref, yn_ref, st_ref, h_ref):
        c = pl.program_id(1)
        _, dt_s, _, acs, e_s, dec_s = _ssd_scalars(dtr_ref[...], dtb_ref[...], alog_ref[...])
        acs_t = acs.T
        tri = lax.broadcasted_iota(jnp.int32, (L, L), 0) >= lax.broadcasted_iota(jnp.int32, (L, L), 1)
        lane = lax.broadcasted_iota(jnp.int32, (L, gw), 1)
        for g in range(G):
            cols = slice(g * gw, (g + 1) * gw)
            ncol = slice(g * STATE, (g + 1) * STATE)

            @pl.when(c == 0)
            def _():
                h_ref[g] = jnp.zeros((STATE, gw), F32)

            xs = xs_ref[:, cols].astype(F32)
            bg = b_ref[:, ncol]
            cg = c_ref[:, ncol]
            dt_x, e_x, dec_x = _ssd_group_common(dt_s, e_s, dec_s, e_ref[:, cols])
            xdt = xs * dt_x
            cb = _dot(cg, bg, NT)
            h = h_ref[g]
            st_ref[0, g] = h
            y = _dot(cg, h) * e_x + dsk_ref[:, cols] * xs
            for r in range(hpg):
                m = cb * _decay_matrix(acs, acs_t, g * hpg + r, tri)
                xr = jnp.where((lane >= r * HEAD_DIM) & (lane < (r + 1) * HEAD_DIM), xdt, 0.0)
                y = y + _dot(m, xr)
            h_ref[g] = h * e_x[L - 1:L, :] + _dot(bg, xdt * dec_x, TN)
            yq = y.astype(y_ref.dtype)
            y_ref[:, cols] = yq
            z = z_ref[:, cols].astype(F32)
            yg = yq.astype(F32) * (z * _sigmoid(z))
            rs = lax.rsqrt(jnp.mean(yg * yg, axis=-1, keepdims=True) + RMS_EPS)
            yn_ref[:, cols] = (yg * rs * nw_ref[:, cols]).astype(yn_ref.dtype)

    return pl.pallas_call(
        body, name="ssd_fwd",
        grid=(bl, nc, G // P),
        in_specs=[
            pl.BlockSpec((L, P * gw), lambda b, c, g: (b * nc + c, g)),
            pl.BlockSpec((L, P * STATE), lambda b, c, g: (b * nc + c, bb0 // P + g)),
            pl.BlockSpec((L, P * STATE), lambda b, c, g: (b * nc + c, cb0 // P + g)),
            pl.BlockSpec((L, P * gw), lambda b, c, g: (b * nc + c, zb0 // P + g)),
            pl.BlockSpec((L, LANES), lambda b, c, g: (b * nc + c, 0)),
            pl.BlockSpec((1, LANES), lambda b, c, g: (0, 0)),
            pl.BlockSpec((1, LANES), lambda b, c, g: (0, 0)),
            pl.BlockSpec((1, P * gw), lambda b, c, g: (0, g)),
            pl.BlockSpec((1, P * gw), lambda b, c, g: (0, g)),
            pl.BlockSpec((LANES, P * gw), lambda b, c, g: (0, g)),
        ],
        out_specs=[
            pl.BlockSpec((L, P * gw), lambda b, c, g: (b * nc + c, g)),
            pl.BlockSpec((L, P * gw), lambda b, c, g: (b * nc + c, g)),
            pl.BlockSpec((1, P, STATE, gw), lambda b, c, g: (b * nc + c, g, 0, 0)),
        ],
        out_shape=[
            jax.ShapeDtypeStruct((t, inner), BF16),
            jax.ShapeDtypeStruct((t, inner), BF16),
            jax.ShapeDtypeStruct((bl * nc, G, STATE, gw), F32),
        ],
        scratch_shapes=[pltpu.VMEM((G, STATE, gw), F32)],
        compiler_params=_params(("arbitrary", "arbitrary", "arbitrary")),
    )(xbc, xbc, xbc, proj, dt_raw, dtb, alog, dskip_x, normw, emat)


def _ssd_bwd(xbc, proj, dt_raw, y, dyn, states, dtb, alog, dskip_x, normw, emat, emat_t, bl, inner, z_col0,
             comm=None):
    t = xbc.shape[0]
    L = CHUNK
    nc = t // bl // L
    G = GROUPS
    gw = inner // G
    hpg = gw // HEAD_DIM
    zb0 = z_col0 // gw
    bb0 = inner // STATE
    cb0 = bb0 + G
    P = G

    def rc(j):
        return nc - 1 - j

    def body(xs_ref, b_ref, c_ref, z_ref, dtr_ref, y_ref, dyn_ref, st_ref, dtb_ref, alog_ref, dsk_ref,
             nw_ref, e_ref, et_ref,
             dxs_ref, db_ref, dc_ref, dz_ref, ddt_ref, dnw_ref, dsk_acc, dalog_acc, ddtb_acc,
             dh_ref):
        b = pl.program_id(0)
        j = pl.program_id(1)

        @pl.when((b == 0) & (j == 0))
        def _():
            dsk_acc[...] = jnp.zeros_like(dsk_acc)
            dalog_acc[...] = jnp.zeros_like(dalog_acc)
            ddtb_acc[...] = jnp.zeros_like(ddtb_acc)

        pre, dt_s, a_row, acs, e_s, dec_s = _ssd_scalars(dtr_ref[...], dtb_ref[...], alog_ref[...])
        acs_t = acs.T
        wacs = jnp.zeros((L, LANES), F32)
        wdt = jnp.zeros((L, LANES), F32)
        tri = lax.broadcasted_iota(jnp.int32, (L, L), 0) >= lax.broadcasted_iota(jnp.int32, (L, L), 1)
        rowi = lax.broadcasted_iota(jnp.int32, (L, gw), 0)
        for g in range(G):
            cols = slice(g * gw, (g + 1) * gw)
            ncol = slice(g * STATE, (g + 1) * STATE)

            @pl.when((b == 0) & (j == 0))
            def _():
                dnw_ref[g] = jnp.zeros((SUBLANES, gw), F32)

            @pl.when(j == 0)
            def _():
                dh_ref[g] = jnp.zeros((STATE, gw), F32)

            xs = xs_ref[:, cols].astype(F32)
            bg = b_ref[:, ncol]
            cg = c_ref[:, ncol]
            dt_x, e_x, dec_x = _ssd_group_common(dt_s, e_s, dec_s, e_ref[:, cols])
            xdt = xs * dt_x
            xdt_b = xdt.astype(BF16)
            cb = _dot(cg, bg, NT)
            h = st_ref[0, g]
            hb16 = h.astype(BF16)
            dsk = dsk_ref[:, cols]

            yv = y_ref[:, cols].astype(F32)
            z = z_ref[:, cols].astype(F32)
            sgz = _sigmoid(z)
            sz = z * sgz
            yg = yv * sz
            rs = lax.rsqrt(jnp.mean(yg * yg, axis=-1, keepdims=True) + RMS_EPS)
            yhat = yg * rs
            dyn_v = dyn_ref[:, cols].astype(F32)
            dnw_ref[g] += _colsum(dyn_v * yhat)
            dyh = dyn_v * nw_ref[:, cols]
            dyg = rs * (dyh - yhat * jnp.mean(dyh * yhat, axis=-1, keepdims=True))
            dy = dyg * sz
            dz_ref[:, cols] = (dyg * yv * (sgz * (1.0 + z * (1.0 - sgz)))).astype(dz_ref.dtype)

            dy_b = dy.astype(BF16)
            dcb = jnp.zeros((L, L), F32)
            dxdt_d = jnp.zeros((L, gw), F32)
            ydiag = jnp.zeros((L, gw), F32)
            for r in range(hpg):
                lm = _decay_matrix(acs, acs_t, g * hpg + r, tri)
                m = (cb * lm).astype(BF16)
                hm = _head_mask(r, gw, BF16)
                dyr = dy_b * hm
                xr = xdt_b * hm
                ydiag = ydiag + _dot(m, xr)
                dcb = dcb + _dot(dyr, xdt_b, NT) * lm
                dxdt_d = dxdt_d + _dot(m, dyr, TN)
            dh = dh_ref[g]
            dh16 = dh.astype(BF16)
            xdec_b = (xdt * dec_x).astype(BF16)
            bdh = _dot(bg, dh16)
            dxdt = dxdt_d + dec_x * bdh
            dcb16 = dcb.astype(BF16)
            dye = (dy * e_x).astype(BF16)
            db_ref[:, ncol] = (_dot(dcb16, cg, TN) + _dot(xdec_b, dh16, NT)).astype(db_ref.dtype)
            dc_ref[:, ncol] = (_dot(dcb16, bg) + _dot(dye, hb16, NT)).astype(dc_ref.dtype)
            dprev = _dot(cg, dye, TN)
            cd_row = e_x[L - 1:L, :]
            s_new = _dot(bg, xdec_b, TN)
            last_term = _colsum(dh16.astype(F32) * s_new) + _colsum(dh * h) * cd_row
            yoff = _dot(cg, hb16) * e_x
            wfold = (dy_b.astype(F32) * ydiag + dy * yoff - dxdt_d * xdt_b.astype(F32) - bdh * xdec_b.astype(F32)
                     + jnp.where(rowi == L - 1, last_term, 0.0))
            et = et_ref[cols, :]
            wacs = wacs + _dot_2piece(wfold, et)
            wdt = wdt + _dot_2piece(dxdt * xs, et)
            dsk_acc[...] += _dot_exact01(jnp.broadcast_to(_colsum(dy * xs), (SUBLANES, gw)), et)
            dxs_ref[:, cols] = (dsk * dy + dxdt * dt_x).astype(dxs_ref.dtype)
            dh_ref[g] = dprev + cd_row * dh

        dda = _cumsum_rows(wacs, reverse=True)
        ddt_raw = (wdt + dda * a_row) * _sigmoid(pre)
        ddt_ref[...] = ddt_raw
        dalog_acc[...] += _colsum(dda * dt_s) * a_row
        ddtb_acc[...] += _colsum(ddt_raw)

    def cidx(b, j):
        return b * nc + rc(j)

    accs = lambda shape: pl.BlockSpec(shape, lambda b, j, g: tuple(0 for _ in shape))
    grid = (bl, nc, G // P)
    c_in, c_in_specs, c_out_specs, c_out_shapes = _comm_specs(comm)
    return pl.pallas_call(
        _fuse_comm(body, grid, 14, 9, comm), name="ssd_bwd",
        grid=grid,
        in_specs=[
            pl.BlockSpec((L, P * gw), lambda b, j, g: (cidx(b, j), g)),
            pl.BlockSpec((L, P * STATE), lambda b, j, g: (cidx(b, j), bb0 // P + g)),
            pl.BlockSpec((L, P * STATE), lambda b, j, g: (cidx(b, j), cb0 // P + g)),
            pl.BlockSpec((L, P * gw), lambda b, j, g: (cidx(b, j), zb0 // P + g)),
            pl.BlockSpec((L, LANES), lambda b, j, g: (cidx(b, j), 0)),
            pl.BlockSpec((L, P * gw), lambda b, j, g: (cidx(b, j), g)),
            pl.BlockSpec((L, P * gw), lambda b, j, g: (cidx(b, j), g)),
            pl.BlockSpec((1, P, STATE, gw), lambda b, j, g: (cidx(b, j), g, 0, 0)),
            pl.BlockSpec((1, LANES), lambda b, j, g: (0, 0)),
            pl.BlockSpec((1, LANES), lambda b, j, g: (0, 0)),
            pl.BlockSpec((1, P * gw), lambda b, j, g: (0, g)),
            pl.BlockSpec((1, P * gw), lambda b, j, g: (0, g)),
            pl.BlockSpec((LANES, P * gw), lambda b, j, g: (0, g)),
            pl.BlockSpec((P * gw, LANES), lambda b, j, g: (g, 0)),
        ] + c_in_specs,
        out_specs=[
            pl.BlockSpec((L, P * gw), lambda b, j, g: (cidx(b, j), g)),
            pl.BlockSpec((L, P * STATE), lambda b, j, g: (cidx(b, j), g)),
            pl.BlockSpec((L, P * STATE), lambda b, j, g: (cidx(b, j), g)),
            pl.BlockSpec((L, P * gw), lambda b, j, g: (cidx(b, j), g)),
            pl.BlockSpec((L, LANES), lambda b, j, g: (cidx(b, j), 0)),
            accs((G, SUBLANES, gw)),
            accs((SUBLANES, LANES)),
            accs((SUBLANES, LANES)),
            accs((SUBLANES, LANES)),
        ] + c_out_specs,
        out_shape=[
            jax.ShapeDtypeStruct((t, inner), BF16),
            jax.ShapeDtypeStruct((t, G * STATE), BF16),
            jax.ShapeDtypeStruct((t, G * STATE), BF16),
            jax.ShapeDtypeStruct((t, inner), BF16),
            jax.ShapeDtypeStruct((t, LANES), F32),
            jax.ShapeDtypeStruct((G, SUBLANES, gw), F32),
            jax.ShapeDtypeStruct((SUBLANES, LANES), F32),
            jax.ShapeDtypeStruct((SUBLANES, LANES), F32),
            jax.ShapeDtypeStruct((SUBLANES, LANES), F32),
        ] + c_out_shapes,
        scratch_shapes=[pltpu.VMEM((G, STATE, gw), F32)]
        + (list(comm.scratch) if comm else []),
        compiler_params=_params(("arbitrary", "arbitrary", "arbitrary")),
    )(xbc, xbc, xbc, proj, dt_raw, y, dyn, states, dtb, alog, dskip_x, normw, emat, emat_t, *c_in)


def _pool_window(u, w, anti):
    n = u.shape[0]
    row = lax.broadcasted_iota(jnp.int32, u.shape, 0)
    acc = u
    s = 1
    while s < w:
        if anti:
            acc = acc + jnp.where(row < n - s, pltpu.roll(acc, n - s, 0), 0.0)
        else:
            acc = acc + jnp.where(row >= s, pltpu.roll(acc, s, 0), 0.0)
        s *= 2
    return acc


def _pool_cnt(shape, w):
    row = lax.broadcasted_iota(jnp.int32, shape, 0)
    return jnp.minimum(row + 1, w).astype(F32)


def _pool_fwd(proj, wpg, bl, d, u_col0):
    t = proj.shape[0]
    s = t // bl
    pg = len(POOL_WINDOWS)
    cg = d // pg
    ub0 = u_col0 // d

    def body(u_ref, w_ref, o_ref):
        for gi, w in enumerate(POOL_WINDOWS):
            u = u_ref[:, gi * cg:(gi + 1) * cg].astype(F32)
            pooled = _pool_window(u, w, False) / _pool_cnt(u.shape, w) - u
            o_ref[:, gi * cg:(gi + 1) * cg] = _dot(pooled, w_ref[gi]).astype(o_ref.dtype)

    return pl.pallas_call(
        body, name="pool_fwd",
        grid=(bl,),
        in_specs=[pl.BlockSpec((s, d), lambda b: (b, ub0)), pl.BlockSpec((pg, cg, cg), lambda b: (0, 0, 0))],
        out_specs=pl.BlockSpec((s, d), lambda b: (b, 0)),
        out_shape=jax.ShapeDtypeStruct((t, d), BF16),
        compiler_params=_params(("parallel",)),
    )(proj, wpg)


def _pool_bwd(proj, dyp, wpg, bl, d, u_col0):
    t = proj.shape[0]
    s = t // bl
    pg = len(POOL_WINDOWS)
    cg = d // pg
    ub0 = u_col0 // d

    def body(u_ref, dy_ref, w_ref, du_ref, dw_ref):
        @pl.when(pl.program_id(0) == 0)
        def _():
            dw_ref[...] = jnp.zeros_like(dw_ref)

        for gi, w in enumerate(POOL_WINDOWS):
            u = u_ref[:, gi * cg:(gi + 1) * cg].astype(F32)
            cnt = _pool_cnt(u.shape, w)
            pooled = _pool_window(u, w, False) / cnt - u
            dy = dy_ref[:, gi * cg:(gi + 1) * cg]
            dw_ref[gi] += _dot(pooled, dy, TN)
            dp = _dot(dy, w_ref[gi], NT)
            du_ref[:, gi * cg:(gi + 1) * cg] = (_pool_window(dp / cnt, w, True) - dp).astype(du_ref.dtype)

    return pl.pallas_call(
        body, name="pool_bwd",
        grid=(bl,),
        in_specs=[pl.BlockSpec((s, d), lambda b: (b, ub0)), pl.BlockSpec((s, d), lambda b: (b, 0)),
                  pl.BlockSpec((pg, cg, cg), lambda b: (0, 0, 0))],
        out_specs=[pl.BlockSpec((s, d), lambda b: (b, 0)), pl.BlockSpec((pg, cg, cg), lambda b: (0, 0, 0))],
        out_shape=[jax.ShapeDtypeStruct((t, d), BF16), jax.ShapeDtypeStruct((pg, cg, cg), F32)],
        compiler_params=_params(("arbitrary",)),
    )(proj, dyp, wpg)


def _merge_fwd(proj, ypr, yssd, x, w_out, b_gates, pool_scale, d, lg_col0, tm):
    t = x.shape[0]
    lb0 = lg_col0 // (2 * d)

    def body(lg_ref, yp_ref, ys_ref, x_ref, w_ref, bg_ref, ps_ref, mg_ref, r1_ref):
        lg = lg_ref[...].astype(F32) + bg_ref[...]
        ga = _sigmoid(lg[:, :d])
        gb = _sigmoid(lg[:, d:])
        merged = ga * (yp_ref[...].astype(F32) * ps_ref[...]) + gb * ys_ref[...].astype(F32)
        mg_ref[...] = merged.astype(mg_ref.dtype)
        r1_ref[...] = ALPHA * x_ref[...] + _dot(mg_ref[...], w_ref[...])

    row = lambda w: pl.BlockSpec((tm, w), lambda i: (i, 0))
    full = lambda a: pl.BlockSpec(a.shape, lambda i: (0, 0))
    return pl.pallas_call(
        body, name="merge_fwd",
        grid=(t // tm,),
        in_specs=[pl.BlockSpec((tm, 2 * d), lambda i: (i, lb0)), row(d), row(d), row(d), full(w_out), full(b_gates),
                  full(pool_scale)],
        out_specs=[row(d), row(d)],
        out_shape=[jax.ShapeDtypeStruct((t, d), BF16), jax.ShapeDtypeStruct((t, d), F32)],
        compiler_params=_params(("parallel",)),
    )(proj, ypr, yssd, x, w_out, b_gates, pool_scale)


def _merge_bwd(dr1, proj, ypr, yssd, w_out, b_gates, pool_scale, d, lg_col0, tm):
    t = dr1.shape[0]
    lb0 = lg_col0 // (2 * d)

    def body(dr_ref, lg_ref, yp_ref, ys_ref, w_ref, bg_ref, ps_ref, dlg_ref, dyp_ref, dys_ref, dbg_ref, dps_ref):
        @pl.when(pl.program_id(0) == 0)
        def _():
            dbg_ref[...] = jnp.zeros_like(dbg_ref)
            dps_ref[...] = jnp.zeros_like(dps_ref)

        dm = _dot(dr_ref[...], w_ref[...], NT)
        lg = lg_ref[...].astype(F32) + bg_ref[...]
        ga = _sigmoid(lg[:, :d])
        gb = _sigmoid(lg[:, d:])
        ypr_v = yp_ref[...].astype(F32)
        ys_v = ys_ref[...].astype(F32)
        ps = ps_ref[...]
        dga = dm * ypr_v * ps
        dla = dga * ga * (1.0 - ga)
        dlb = dm * ys_v * gb * (1.0 - gb)
        dlg_ref[:, :d] = dla.astype(dlg_ref.dtype)
        dlg_ref[:, d:] = dlb.astype(dlg_ref.dtype)
        dyp_ref[...] = (dm * ga * ps).astype(dyp_ref.dtype)
        dys_ref[...] = (dm * gb).astype(dys_ref.dtype)
        dbg_ref[0:1, :d] += _colsum(dla)
        dbg_ref[0:1, d:] += _colsum(dlb)
        dps_ref[0:1, :] += _colsum(dm * ga * ypr_v)

    row = lambda w: pl.BlockSpec((tm, w), lambda i: (i, 0))
    full = lambda a: pl.BlockSpec(a.shape, lambda i: (0, 0))
    acc = lambda w: pl.BlockSpec((SUBLANES, w), lambda i: (0, 0))
    return pl.pallas_call(
        body, name="merge_bwd",
        grid=(t // tm,),
        in_specs=[row(d), pl.BlockSpec((tm, 2 * d), lambda i: (i, lb0)), row(d), row(d), full(w_out), full(b_gates),
                  full(pool_scale)],
        out_specs=[row(2 * d), row(d), row(d), acc(2 * d), acc(d)],
        out_shape=[jax.ShapeDtypeStruct((t, 2 * d), BF16), jax.ShapeDtypeStruct((t, d), BF16),
                   jax.ShapeDtypeStruct((t, d), BF16), jax.ShapeDtypeStruct((SUBLANES, 2 * d), F32),
                   jax.ShapeDtypeStruct((SUBLANES, d), F32)],
        compiler_params=_params(("arbitrary",)),
    )(dr1, proj, ypr, yssd, w_out, b_gates, pool_scale)


MLP_SLABS_PER_STEP = 2


def _mlp_fwd(r1, target, w_up, w_down, ln1_g, ln1_b, ln2_g, ln2_b, tm):
    t, d = r1.shape
    ns, _, sw = w_up.shape
    spb = MLP_SLABS_PER_STEP
    assert ns % spb == 0
    nf, tf, ff = ns // spb, spb * sw, ns * sw

    def body(r1_ref, tg_ref, wu_ref, wd_ref, g1_ref, b1_ref, g2_ref, b2_ref,
             up_ref, h1_ref, dr2_ref, loss_ref, dg2_ref, db2_ref, h1f, acc):
        i = pl.program_id(0)
        f = pl.program_id(1)

        @pl.when((i == 0) & (f == 0))
        def _():
            loss_ref[...] = jnp.zeros_like(loss_ref)
            dg2_ref[...] = jnp.zeros_like(dg2_ref)
            db2_ref[...] = jnp.zeros_like(db2_ref)

        @pl.when(f == 0)
        def _():
            xhat, _ = _ln_fwd(r1_ref[...])
            h1 = xhat * g1_ref[...] + b1_ref[...]
            h1f[...] = h1
            h1_ref[...] = h1.astype(h1_ref.dtype)
            acc[...] = jnp.zeros_like(acc)

        for s in range(spb):
            up_ref[:, s * sw:(s + 1) * sw] = _dot(h1_ref[...], wu_ref[s]).astype(up_ref.dtype)
        upq = jnp.maximum(up_ref[...].astype(F32), 0.0)
        acc[...] += _dot(upq * upq, wd_ref[...])

        @pl.when(f == nf - 1)
        def _():
            xhat, rstd = _ln_fwd(ALPHA * h1f[...] + acc[...])
            g2 = g2_ref[...]
            diff = xhat * g2 + b2_ref[...] - tg_ref[...]
            loss_ref[...] += 0.5 / d * jnp.sum(diff * diff)
            dh2 = diff * (1.0 / d)
            dg2_ref[0:1, :] += _colsum(dh2 * xhat)
            db2_ref[0:1, :] += _colsum(dh2)
            dr2_ref[...] = _ln_bwd(dh2, xhat, rstd, g2).astype(dr2_ref.dtype)

    row = pl.BlockSpec((tm, d), lambda i, f: (i, 0))
    vec = pl.BlockSpec((1, d), lambda i, f: (0, 0))
    acc8 = pl.BlockSpec((SUBLANES, d), lambda i, f: (0, 0))
    return pl.pallas_call(
        body, name="mlp_fwd",
        grid=(t // tm, nf),
        in_specs=[row, row, pl.BlockSpec((spb, d, sw), lambda i, f: (f, 0, 0)), pl.BlockSpec((tf, d), lambda i, f: (f, 0)),
                  vec, vec, vec, vec],
        out_specs=[pl.BlockSpec((tm, tf), lambda i, f: (i, f)), row, row,
                   pl.BlockSpec((SUBLANES, LANES), lambda i, f: (0, 0)), acc8, acc8],
        out_shape=[jax.ShapeDtypeStruct((t, ff), BF16), jax.ShapeDtypeStruct((t, d), BF16),
                   jax.ShapeDtypeStruct((t, d), BF16), jax.ShapeDtypeStruct((SUBLANES, LANES), F32),
                   jax.ShapeDtypeStruct((SUBLANES, d), F32), jax.ShapeDtypeStruct((SUBLANES, d), F32)],
        scratch_shapes=[pltpu.VMEM((tm, d), F32), pltpu.VMEM((tm, d), F32)],
        compiler_params=_params(("arbitrary", "arbitrary")),
    )(r1, target, w_up, w_down, ln1_g, ln1_b, ln2_g, ln2_b)


def _mlp_bwd(dr2, up, r1, w_up, w_down, ln1_g, tm):
    t, d = r1.shape
    ns, _, sw = w_up.shape
    spb = MLP_SLABS_PER_STEP
    assert ns % spb == 0
    nf, tf, ff = ns // spb, spb * sw, ns * sw

    def body(dr2_ref, up_ref, r1_ref, wu_ref, wd_ref, g1_ref, dup_ref, dr1_ref, dg1_ref, db1_ref, acc):
        i = pl.program_id(0)
        f = pl.program_id(1)

        @pl.when((i == 0) & (f == 0))
        def _():
            dg1_ref[...] = jnp.zeros_like(dg1_ref)
            db1_ref[...] = jnp.zeros_like(db1_ref)

        @pl.when(f == 0)
        def _():
            acc[...] = jnp.zeros_like(acc)

        dact = _dot(dr2_ref[...], wd_ref[...], NT)
        dup_ref[...] = (dact * 2.0 * jnp.maximum(up_ref[...].astype(F32), 0.0)).astype(dup_ref.dtype)
        for s in range(spb):
            acc[...] += _dot(dup_ref[:, s * sw:(s + 1) * sw], wu_ref[s], NT)

        @pl.when(f == nf - 1)
        def _():
            dh1 = acc[...] + ALPHA * dr2_ref[...].astype(F32)
            xhat, rstd = _ln_fwd(r1_ref[...])
            dg1_ref[0:1, :] += _colsum(dh1 * xhat)
            db1_ref[0:1, :] += _colsum(dh1)
            dr1_ref[...] = _ln_bwd(dh1, xhat, rstd, g1_ref[...]).astype(dr1_ref.dtype)

    row = pl.BlockSpec((tm, d), lambda i, f: (i, 0))
    acc8 = pl.BlockSpec((SUBLANES, d), lambda i, f: (0, 0))
    return pl.pallas_call(
        body, name="mlp_bwd",
        grid=(t // tm, nf),
        in_specs=[row, pl.BlockSpec((tm, tf), lambda i, f: (i, f)), row,
                  pl.BlockSpec((spb, d, sw), lambda i, f: (f, 0, 0)), pl.BlockSpec((tf, d), lambda i, f: (f, 0)),
                  pl.BlockSpec((1, d), lambda i, f: (0, 0))],
        out_specs=[pl.BlockSpec((tm, tf), lambda i, f: (i, f)), row, acc8, acc8],
        out_shape=[jax.ShapeDtypeStruct((t, ff), BF16), jax.ShapeDtypeStruct((t, d), BF16),
                   jax.ShapeDtypeStruct((SUBLANES, d), F32), jax.ShapeDtypeStruct((SUBLANES, d), F32)],
        scratch_shapes=[pltpu.VMEM((tm, d), F32)],
        compiler_params=_params(("arbitrary", "arbitrary")),
    )(dr2, up, r1, w_up, w_down, ln1_g)


def _dx_kernel(segs, w_main, ddt, w_dt, dr1, tm, tk, comm=None):
    t, d = dr1.shape
    nblk = [s.shape[1] // tk for s in segs]
    starts = [sum(nblk[:i]) for i in range(len(segs))]
    nk = sum(nblk)
    nseg = len(segs)

    def body(*refs):
        seg_refs = refs[:nseg]
        w_ref, ddt_ref, wdt_ref, dr1_ref, o_ref, acc = refs[nseg:]
        k = pl.program_id(1)

        @pl.when(k == 0)
        def _():
            acc[...] = ALPHA * dr1_ref[...].astype(F32) + _dot(ddt_ref[...], wdt_ref[...], NT)

        for si in range(nseg):
            @pl.when((k >= starts[si]) & (k < starts[si] + nblk[si]))
            def _(si=si):
                acc[...] += _dot(seg_refs[si][...], w_ref[...], NT)

        @pl.when(k == nk - 1)
        def _():
            o_ref[...] = acc[...]

    def seg_spec(si):
        return pl.BlockSpec((tm, tk), lambda i, k: (i, jnp.clip(k - starts[si], 0, nblk[si] - 1)))

    row = pl.BlockSpec((tm, d), lambda i, k: (i, 0))
    grid = (t // tm, nk)
    c_in, c_in_specs, c_out_specs, c_out_shapes = _comm_specs(comm)
    return pl.pallas_call(
        _fuse_comm(body, grid, nseg + 4, 1, comm), name="dx",
        grid=grid,
        in_specs=[seg_spec(si) for si in range(nseg)] + [
            pl.BlockSpec((d, tk), lambda i, k: (0, k)), pl.BlockSpec((tm, LANES), lambda i, k: (i, 0)),
            pl.BlockSpec((d, LANES), lambda i, k: (0, 0)), row] + c_in_specs,
        out_specs=[row] + c_out_specs,
        out_shape=[jax.ShapeDtypeStruct((t, d), F32)] + c_out_shapes,
        scratch_shapes=[pltpu.VMEM((tm, d), F32)] + (list(comm.scratch) if comm else []),
        compiler_params=_params(("arbitrary", "arbitrary")),
    )(*segs, w_main, ddt, w_dt, dr1, *c_in)


def _dims(d):
    inner = 2 * d
    heads = inner // HEAD_DIM
    cd = inner + 2 * GROUPS * STATE
    assert heads <= LANES and inner % (GROUPS * LANES) == 0 and d % (len(POOL_WINDOWS) * LANES) == 0
    o_z, o_xbc, o_dt, o_lg = d, d + inner, d + inner + cd, d + inner + cd + heads
    return inner, heads, cd, (o_z, o_xbc, o_dt, o_lg)


def _row(v, width=None):
    v = v.reshape(1, -1).astype(F32)
    if width is not None and v.shape[1] < width:
        v = jnp.pad(v, ((0, 0), (0, width - v.shape[1])))
    return v


def _local_step(x2, tgt2, w, shards, core, bl):
    t, d = x2.shape
    inner, heads, cd, _ = _dims(d)
    gs = GROUPS * STATE
    nc = t // bl // CHUNK
    w_main, w_dt = _w_in_internal(w["w_in_blocks"], d)
    c_z, c_lg, c_u = cd, cd + inner, cd + inner + 2 * d
    conv_w8 = jnp.pad(w["conv_w"].astype(F32), ((0, SUBLANES - CONV_K), (0, 0)))
    conv_b = _row(w["conv_b"])
    dtb, alog = _row(w["dt_bias"], LANES), _row(w["a_log"], LANES)
    dskip_x = _row(jnp.repeat(w["d_skip"].reshape(-1), HEAD_DIM))
    normw = _row(w["ssd_norm_w"])
    col_head = lax.broadcasted_iota(jnp.int32, (LANES, inner), 1) // HEAD_DIM
    emat = (col_head == lax.broadcasted_iota(jnp.int32, (LANES, inner), 0)).astype(BF16)
    emat_t = emat.T
    w_main, w_dt = w_main.astype(BF16), w_dt.astype(BF16)
    b_gates, pool_scale = _row(w["b_gates"]), _row(w["pool_scale"])
    ln1_g, ln1_b, ln2_g, ln2_b = _row(w["ln1_g"]), _row(w["ln1_b"]), _row(w["ln2_g"]), _row(w["ln2_b"])

    tm = min(512, t)
    tk = min(1024, d)
    ct = min(512, d)
    rt = min(512, t // bl)
    nct = t // bl // rt
    mm = functools.partial(_matmul, bm=1024, bn=tk, bk=1024)
    mmt = functools.partial(_matmul, bm=1024, bn=tk, bk=2048)
    xb = x2.astype(BF16)

    proj, xbc, dsl, *gathered = _in_proj(xb, w_main, conv_w8, conv_b, cd, t // bl, 1024, tk,
                                    _all_gather_comm([shards[n] for n in OTHERS]))
    gathered = dict(zip(OTHERS, gathered))
    w_ssd, w_out, w_down = (gathered[n].reshape(-1, d) for n in ("w_ssd_proj", "w_out", "w_down"))
    w_up = gathered["w_up"]
    npg = len(POOL_WINDOWS)
    cg = d // npg
    wpg = gathered["w_pool_group"].reshape(N_DEV, npg, cg // N_DEV, cg).transpose(1, 0, 2, 3).reshape(npg, cg, cg)
    dt_raw = mm(xb, w_dt, "nn", F32, name="in_proj_dt")
    y, yn, states = _ssd_fwd(xbc, proj, dt_raw, dtb, alog, dskip_x, normw, emat, bl, inner, c_z)
    yssd = mmt(yn, w_ssd, "nn", BF16, name="ssd_proj")
    ypr = _pool_fwd(proj, wpg, bl, d, c_u)
    merged, r1 = _merge_fwd(proj, ypr, yssd, x2, w_out, b_gates, pool_scale, d, c_lg, tm)
    tmm = min(1024, t)
    up, h1, dr2, loss8, dg2, db2 = _mlp_fwd(r1, tgt2, w_up, w_down, ln1_g, ln1_b, ln2_g, ln2_b, tmm)

    dup, dr1, dg1, db1 = _mlp_bwd(dr2, up, r1, w_up, w_down, ln1_g, tmm)
    relu2 = lambda v: jnp.square(jnp.maximum(v, 0.0))
    g = {}
    g["w_down"] = mmt(up, dr2, "tn", BF16, name="dw_down", a_fn=relu2)
    g["w_up"] = mmt(h1, dup, "tn", BF16, name="dw_up", col_blocks=N_DEV)
    g["w_out"] = mmt(merged, dr1, "tn", BF16, name="dw_out")
    dlg, dyp, dys, dbg, dps = _merge_bwd(dr1, proj, ypr, yssd, w_out, b_gates, pool_scale, d, c_lg, tm)
    du, dwpg = _pool_bwd(proj, dyp, wpg, bl, d, c_u)
    g["w_pool_group"] = dwpg.reshape(npg, N_DEV, cg // N_DEV, cg).transpose(1, 0, 2, 3).reshape(
        N_DEV, npg * cg // N_DEV, cg).astype(BF16)
    dyn = mm(dys, w_ssd, "nt", BF16, name="d_ssd_proj")
    g["w_ssd_proj"] = mmt(yn, dys, "tn", BF16, name="dw_ssd_proj")

    def chip_sums(names, tag):
        parts = [g.pop(n).reshape((N_DEV,) + shards_2d[n]) for n in names]
        recv = _run_comm(_rs_sibling_comm(parts), "rs_sibling_" + tag)
        return [_add_pairs(core, p, r, "rs_add_" + n) for n, p, r in zip(names, parts, recv)]

    shards_2d = {n: s.shape for n, s in shards.items()}
    shards_2d["w_in"] = w["w_in_blocks"].shape[1:]
    dxs, dbm, dcm, dz, ddt, dnw, dsk, dalog, ddtb, *recv_others = _ssd_bwd(
        xbc, proj, dt_raw, y, dyn, states, dtb, alog, dskip_x, normw, emat, emat_t, bl, inner, c_z,
        comm=_rs_chips_comm(chip_sums(OTHERS, "a")))
    dxs_p, dcw_x, dcb_x = _conv_bwd(proj, dsl, dxs, conv_w8, nct, 0, inner, ct, rt, "conv_bwd_x")
    dbm_p, dcw_b, dcb_b = _conv_bwd(proj, dsl, dbm, conv_w8, nct, inner, gs, ct, rt, "conv_bwd_b")
    dcm_p, dcw_c, dcb_c = _conv_bwd(proj, dsl, dcm, conv_w8, nct, inner + gs, gs, ct, rt, "conv_bwd_c")
    segs = [dxs_p, dbm_p, dcm_p, dz, dlg, du]
    keys = [k for k, _, _ in _col_segments(d)]
    dws = {k: mmt(xb, s, "tn", BF16, name="dw_in_" + k) for k, s in zip(keys, segs + [ddt])}
    g["w_in"] = _w_in_grad_blocks(dws, d, w["w_in_blocks"].shape[2])
    grad_x, recv_w_in = _dx_kernel(segs, w_main, ddt, w_dt, dr1, tmm, tk,
                                   comm=_rs_chips_comm(chip_sums(["w_in"], "b")))
    recv = dict(zip(OTHERS, recv_others))
    recv["w_in"] = recv_w_in
    g["conv_w"] = jnp.concatenate([dcw_x, dcw_b, dcw_c], axis=1)[:CONV_K]
    g["conv_b"] = jnp.concatenate([dcb_x, dcb_b, dcb_c], axis=1)[0]
    g["b_gates"], g["pool_scale"] = dbg[0], dps[0]
    g["dt_bias"], g["a_log"], g["d_skip"] = ddtb[0, :heads], dalog[0, :heads], dsk[0, :heads]
    g["ssd_norm_w"] = dnw[:, 0, :].reshape(inner)
    g["ln1_g"], g["ln1_b"], g["ln2_g"], g["ln2_b"] = dg1[0], db1[0], dg2[0], db2[0]
    return loss8, grad_x, g, recv


BIG = ("w_in", "w_ssd_proj", "w_pool_group", "w_out", "w_up", "w_down")
OTHERS = BIG[1:]
SMALL = ("b_gates", "conv_b", "dt_bias", "a_log", "d_skip", "ssd_norm_w", "pool_scale", "ln1_g", "ln1_b", "ln2_g",
         "ln2_b")
SMALL_PACK = SMALL + ("conv_w",)
NAMES = ("w_in", "b_gates", "conv_w", "conv_b", "dt_bias", "a_log", "d_skip", "ssd_norm_w", "w_ssd_proj",
         "w_pool_group", "pool_scale", "w_out", "ln1_g", "ln1_b", "w_up", "w_down", "ln2_g", "ln2_b")


def _size(shape):
    n = 1
    for s in shape:
        n *= s
    return n


def _rows128(v):
    v = v.astype(F32).reshape((-1, v.shape[-1]))
    n = v.shape[-1]
    v = jnp.pad(v, ((0, 0), (0, -n % LANES)))
    return v.reshape(-1, LANES)


def _pack_small(vals, extra):
    parts = [_rows128(vals[n]) for n in SMALL_PACK]
    parts.append(jnp.pad(extra.reshape(1, 1).astype(F32), ((0, 0), (0, LANES - 1))))
    buf = jnp.concatenate(parts, axis=0)
    return jnp.pad(buf, ((0, -buf.shape[0] % SUBLANES), (0, 0)))


def _unpack_small(buf, shapes):
    out, off = {}, 0
    for n in SMALL_PACK:
        lead, last = _size(shapes[n][:-1]), shapes[n][-1]
        per = -(-last // LANES)
        out[n] = buf[off:off + lead * per].reshape(lead, per * LANES)[:, :last].reshape(shapes[n])
        off += lead * per
    return out, buf[off, 0]


def _col_segments(d):
    inner, heads, cd, (o_z, o_xbc, o_dt, o_lg) = _dims(d)
    gs = GROUPS * STATE
    return [("xs", o_xbc, inner), ("B", o_xbc + inner, gs), ("C", o_xbc + inner + gs, gs), ("z", o_z, inner),
            ("lg", o_lg, 2 * d), ("u", 0, d), ("dt", o_dt, heads)]


def _cols_from_blocks(blocks, start, width, bw):
    parts, pos = [], start
    while pos < start + width:
        k, off = divmod(pos, bw)
        n = min(bw - off, start + width - pos)
        parts.append(blocks[k][:, off:off + n])
        pos += n
    return parts


def _w_in_internal(blocks, d):
    bw = blocks.shape[2]
    segs = _col_segments(d)
    heads = segs[-1][2]
    main = [p for _, s, w_ in segs[:-1] for p in _cols_from_blocks(blocks, s, w_, bw)]
    w_dt = jnp.concatenate(_cols_from_blocks(blocks, segs[-1][1], heads, bw), axis=1)
    return jnp.concatenate(main, axis=1), jnp.pad(w_dt, ((0, 0), (0, LANES - heads)))


def _w_in_grad_blocks(dws, d, bw):
    order = sorted(_col_segments(d), key=lambda s: s[1])
    blocks = []
    for k in range(N_DEV):
        lo, hi, parts = k * bw, (k + 1) * bw, []
        for key, s, w_ in order:
            a, b = max(lo, s), min(hi, s + w_)
            if a < b:
                parts.append(dws[key][:, a - s:b - s])
        blocks.append(jnp.concatenate(parts, axis=1))
    return jnp.stack(blocks)


def _mesh_pos():
    return lax.axis_index("x"), lax.axis_index("y"), lax.axis_index("c")


def _all_gather_comm(shards):
    nw = len(shards)

    def setup(x_refs, out_refs, scr):
        send_sems, recv_sems, local_sems = scr
        x, y, c = _mesh_pos()
        me, sibling = (x, y, c), (x, y, 1 - c)
        chips = [(1 - x, y), (x, 1 - y), (1 - x, 1 - y)]

        def copy(wi, k, block, to, from_input=False):
            px, py, pc = block
            blk = out_refs[wi].at[4 * px + 2 * py + pc]
            return pltpu.make_async_remote_copy(
                src_ref=x_refs[wi] if from_input else blk, dst_ref=blk,
                send_sem=send_sems.at[7 * wi + k], recv_sem=recv_sems.at[7 * wi + k], device_id=to,
                device_id_type=MESH)

        mine = [pltpu.make_async_copy(x_refs[wi], out_refs[wi].at[4 * x + 2 * y + c], local_sems.at[wi])
                for wi in range(nw)]
        sends = []
        for wi in range(nw):
            sends.append(copy(wi, 0, me, sibling, True))
            sends += [copy(wi, 1 + j, me, (*chip, c), True) for j, chip in enumerate(chips)]
        return copy, mine, sends, me, sibling, chips, c

    def start(x_refs, out_refs, scr):
        _, mine, sends, _, _, _, _ = setup(x_refs, out_refs, scr)
        for cp in mine + sends:
            cp.start()

    def wait(x_refs, out_refs, scr):
        copy, mine, sends, me, sibling, chips, c = setup(x_refs, out_refs, scr)
        passed = []
        for wi in range(nw):
            for j, chip in enumerate(chips):
                copy(wi, 1 + j, (*chip, c), me).wait_recv()
                passed.append(copy(wi, 4 + j, (*chip, c), sibling))
                passed[-1].start()
        for wi in range(nw):
            copy(wi, 0, sibling, me).wait_recv()
            for j, chip in enumerate(chips):
                copy(wi, 4 + j, (*chip, 1 - c), me).wait_recv()
        for cp in sends + passed:
            cp.wait_send()
        for cp in mine:
            cp.wait()

    return _Comm(
        inputs=list(shards),
        out_shapes=[jax.ShapeDtypeStruct((N_DEV,) + s.shape, s.dtype) for s in shards],
        scratch=[pltpu.SemaphoreType.DMA((7 * nw,)), pltpu.SemaphoreType.DMA((7 * nw,)),
                 pltpu.SemaphoreType.DMA((nw,))],
        start=start, wait=wait)


def _rs_sibling_comm(parts):
    nw = len(parts)
    half = N_DEV // 2

    def copies(p_refs, recv_refs, scr):
        send_sems, recv_sems = scr
        x, y, c = _mesh_pos()
        return [pltpu.make_async_remote_copy(
            src_ref=p_refs[wi].at[2 * q + 1 - c], dst_ref=recv_refs[wi].at[q],
            send_sem=send_sems.at[half * wi + q], recv_sem=recv_sems.at[half * wi + q],
            device_id=(x, y, 1 - c), device_id_type=MESH) for wi in range(nw) for q in range(half)]

    def start(p_refs, recv_refs, scr):
        for cp in copies(p_refs, recv_refs, scr):
            cp.start()

    def wait(p_refs, recv_refs, scr):
        for cp in copies(p_refs, recv_refs, scr):
            cp.wait()

    return _Comm(
        inputs=list(parts),
        out_shapes=[jax.ShapeDtypeStruct((half,) + p.shape[1:], p.dtype) for p in parts],
        scratch=[pltpu.SemaphoreType.DMA((half * nw,)), pltpu.SemaphoreType.DMA((half * nw,))],
        start=start, wait=wait)


def _rs_chips_comm(tbs):
    nw = len(tbs)

    def copies(t_refs, o_refs, scr):
        send_sems, recv_sems, local_sems = scr
        x, y, c = _mesh_pos()
        p = 2 * x + y
        chips = [(1 - x, y), (x, 1 - y), (1 - x, 1 - y)]
        own = [pltpu.make_async_copy(t_refs[wi].at[p], o_refs[wi].at[p], local_sems.at[wi]) for wi in range(nw)]
        remote = [pltpu.make_async_remote_copy(
            src_ref=t_refs[wi].at[2 * qx + qy], dst_ref=o_refs[wi].at[p], send_sem=send_sems.at[3 * wi + j],
            recv_sem=recv_sems.at[3 * wi + j], device_id=(qx, qy, c), device_id_type=MESH)
            for wi in range(nw) for j, (qx, qy) in enumerate(chips)]
        arriving = [pltpu.make_async_remote_copy(
            src_ref=t_refs[wi].at[p], dst_ref=o_refs[wi].at[2 * qx + qy], send_sem=send_sems.at[3 * wi + j],
            recv_sem=recv_sems.at[3 * wi + j], device_id=(qx, qy, c), device_id_type=MESH)
            for wi in range(nw) for j, (qx, qy) in enumerate(chips)]
        return own, remote, arriving

    def start(t_refs, o_refs, scr):
        own, remote, _ = copies(t_refs, o_refs, scr)
        for cp in own + remote:
            cp.start()

    def wait(t_refs, o_refs, scr):
        own, remote, arriving = copies(t_refs, o_refs, scr)
        for cp in arriving:
            cp.wait_recv()
        for cp in remote:
            cp.wait_send()
        for cp in own:
            cp.wait()

    return _Comm(
        inputs=list(tbs),
        out_shapes=[jax.ShapeDtypeStruct(t_.shape, t_.dtype) for t_ in tbs],
        scratch=[pltpu.SemaphoreType.DMA((3 * nw,)), pltpu.SemaphoreType.DMA((3 * nw,)),
                 pltpu.SemaphoreType.DMA((nw,))],
        start=start, wait=wait)


def _row_tile(rows, cap=256):
    if rows <= cap:
        return rows
    return max(t_ for t_ in range(SUBLANES, cap + 1, SUBLANES) if rows % t_ == 0)


def _add_pairs(core, part, recv, name):
    n, r, c_ = recv.shape
    tr = _row_tile(r)

    def body(core_ref, a_ref, b_ref, o_ref):
        o_ref[...] = (a_ref[...].astype(F32) + b_ref[...].astype(F32)).astype(o_ref.dtype)

    spec = pl.BlockSpec((1, tr, c_), lambda q, i, core_ref: (q, i, 0))
    return pl.pallas_call(
        body, name=name,
        grid_spec=pltpu.PrefetchScalarGridSpec(
            num_scalar_prefetch=1, grid=(n, r // tr),
            in_specs=[pl.BlockSpec((1, tr, c_), lambda q, i, core_ref: (2 * q + core_ref[0], i, 0)), spec],
            out_specs=spec),
        out_shape=jax.ShapeDtypeStruct(recv.shape, BF16), compiler_params=_params(("parallel", "parallel")),
    )(core, part, recv)


def _small_allreduce(vec, name):
    rows = vec.shape[0]

    def body(x_ref, o_ref, buf, send_sems, recv_sems):
        x, y, c = _mesh_pos()
        me = 4 * x + 2 * y + c
        buf[me] = x_ref[...]
        cps = []
        for k in range(1, N_DEV):
            peer = (1 - x if k & 4 else x, 1 - y if k & 2 else y, 1 - c if k & 1 else c)
            cps.append(pltpu.make_async_remote_copy(
                src_ref=x_ref, dst_ref=buf.at[me], send_sem=send_sems.at[k - 1], recv_sem=recv_sems.at[k - 1],
                device_id=peer, device_id_type=MESH))
        for cp in cps:
            cp.start()
        for k in range(1, N_DEV):
            px, py, pc = (1 - x if k & 4 else x, 1 - y if k & 2 else y, 1 - c if k & 1 else c)
            pltpu.make_async_remote_copy(
                src_ref=x_ref, dst_ref=buf.at[4 * px + 2 * py + pc], send_sem=send_sems.at[k - 1],
                recv_sem=recv_sems.at[k - 1], device_id=(px, py, pc), device_id_type=MESH).wait_recv()
        for cp in cps:
            cp.wait_send()
        acc = buf[0]
        for k in range(1, N_DEV):
            acc = acc + buf[k]
        o_ref[...] = acc

    vm = pl.BlockSpec(memory_space=pltpu.VMEM)
    return pl.pallas_call(
        body, name=name,
        in_specs=[vm], out_specs=vm,
        out_shape=jax.ShapeDtypeStruct(vec.shape, F32),
        scratch_shapes=[pltpu.VMEM((N_DEV, rows, LANES), F32), pltpu.SemaphoreType.DMA((N_DEV - 1,)),
                        pltpu.SemaphoreType.DMA((N_DEV - 1,))],
    )(vec)


def _adamw(gparts, w, m, v, name):
    n, r, c_ = gparts.shape
    tr = _row_tile(r)
    c1 = 1.0 / (1.0 - B1 ** STEP)
    c2 = 1.0 / (1.0 - B2 ** STEP)

    def body(g_ref, w_ref, m_ref, v_ref, go_ref, d_ref, mo_ref, vo_ref):
        g = g_ref[0].astype(F32)
        for q in range(1, n):
            g = g + g_ref[q].astype(F32)
        mn = B1 * m_ref[...] + (1.0 - B1) * g
        vn = B2 * v_ref[...] + (1.0 - B2) * (g * g)
        go_ref[...] = g
        mo_ref[...] = mn
        vo_ref[...] = vn
        d_ref[...] = -LR * ((mn * c1) / (jnp.sqrt(vn * c2) + ADAM_EPS) + WD * w_ref[...])

    spec = pl.BlockSpec((tr, c_), lambda i: (i, 0))
    out = jax.ShapeDtypeStruct((r, c_), F32)
    return pl.pallas_call(
        body, name=name, grid=(r // tr,),
        in_specs=[pl.BlockSpec((n, tr, c_), lambda i: (0, i, 0)), spec, spec, spec],
        out_specs=[spec] * 4, out_shape=[out] * 4, compiler_params=_params(("parallel",)),
    )(gparts, w, m, v)


def kernel(x, w_in, b_gates, conv_w, conv_b, dt_bias, a_log, d_skip, ssd_norm_w, w_ssd_proj, w_pool_group, pool_scale, w_out, ln1_g, ln1_b, w_up, w_down, ln2_g, ln2_b, loss_target, m_w_in, m_b_gates, m_conv_w, m_conv_b, m_dt_bias, m_a_log, m_d_skip, m_ssd_norm_w, m_w_ssd_proj, m_w_pool_group, m_pool_scale, m_w_out, m_ln1_g, m_ln1_b, m_w_up, m_w_down, m_ln2_g, m_ln2_b, v_w_in, v_b_gates, v_conv_w, v_conv_b, v_dt_bias, v_a_log, v_d_skip, v_ssd_norm_w, v_w_ssd_proj, v_w_pool_group, v_pool_scale, v_w_out, v_ln1_g, v_ln1_b, v_w_up, v_w_down, v_ln2_g, v_ln2_b):
    ws = (w_in, b_gates, conv_w, conv_b, dt_bias, a_log, d_skip, ssd_norm_w, w_ssd_proj, w_pool_group, pool_scale,
          w_out, ln1_g, ln1_b, w_up, w_down, ln2_g, ln2_b)
    ms = (m_w_in, m_b_gates, m_conv_w, m_conv_b, m_dt_bias, m_a_log, m_d_skip, m_ssd_norm_w, m_w_ssd_proj,
          m_w_pool_group, m_pool_scale, m_w_out, m_ln1_g, m_ln1_b, m_w_up, m_w_down, m_ln2_g, m_ln2_b)
    vs = (v_w_in, v_b_gates, v_conv_w, v_conv_b, v_dt_bias, v_a_log, v_d_skip, v_ssd_norm_w, v_w_ssd_proj,
          v_w_pool_group, v_pool_scale, v_w_out, v_ln1_g, v_ln1_b, v_w_up, v_w_down, v_ln2_g, v_ln2_b)
    w = {n: a[0] for n, a in zip(NAMES, ws)}
    m = {n: a[0] for n, a in zip(NAMES, ms)}
    v = {n: a[0] for n, a in zip(NAMES, vs)}
    out_shapes = {n: a.shape for n, a in zip(NAMES, ws)}
    bl, s, d = x.shape
    x2, tgt2 = x.reshape(bl * s, d), loss_target.reshape(bl * s, d)
    xi, yi, ci = _mesh_pos()
    me = 4 * xi + 2 * yi + ci
    zero = jnp.zeros((), F32)
    shapes = {n: w[n].shape for n in NAMES}
    shape2d = {n: (_size(shapes[n][:-1]), shapes[n][-1]) for n in BIG}
    cwl = shapes["conv_w"][1]

    shards = {n: w[n].astype(BF16).reshape(shape2d[n]) for n in BIG}
    full = {n: w[n] for n in SMALL}
    full["w_in_blocks"], conv_blocks = _run_comm(_all_gather_comm([shards.pop("w_in"), w["conv_w"]]),
                                                 "all_gather_w_in")
    full["conv_w"] = conv_blocks.transpose(1, 0, 2).reshape(CONV_K, N_DEV * cwl)
    loss8, grad_x, g, recv = _local_step(x2, tgt2, full, shards, ci.astype(jnp.int32).reshape(1), bl)

    small_sum = _small_allreduce(_pack_small(g, loss8[0, 0]), "small_allreduce")
    ex_shapes = {n: shapes[n] for n in SMALL}
    ex_shapes["conv_w"] = (CONV_K, N_DEV * cwl)
    gsum, loss = _unpack_small(small_sum, ex_shapes)
    gsum["conv_w"] = lax.dynamic_slice(gsum["conv_w"], (0, me * cwl), (CONV_K, cwl))
    gs_pk = _pack_small(gsum, zero)
    ws_pk, ms_pk, vs_pk = (_pack_small(t_, zero) for t_ in (w, m, v))
    small_out = _adamw(gs_pk[None], ws_pk, ms_pk, vs_pk, "adamw_small")
    loc_shapes = {n: shapes[n] for n in SMALL_PACK}
    res = [_unpack_small(o, loc_shapes)[0] for o in small_out]

    for n in BIG:
        outs = _adamw(recv[n], *(t_[n].reshape(shape2d[n]) for t_ in (w, m, v)), "adamw_" + n)
        for r_, o in zip(res, outs):
            r_[n] = o

    def ordered(r_):
        return [r_[n].reshape(out_shapes[n]) for n in NAMES]

    return (loss, grad_x.reshape(bl, s, d), *ordered(res[0]), *ordered(res[1]), *ordered(res[2]), *ordered(res[3]))
```

```python
import collections
import functools

import jax
import jax.numpy as jnp
from jax import lax
from jax.experimental import pallas as pl
from jax.experimental.pallas import tpu as pltpu

F32 = jnp.float32
BF16 = jnp.bfloat16
MESH = pl.DeviceIdType.MESH

HEAD_DIM = 64
STATE = 128
GROUPS = 8
CONV_K = 4
CHUNK = 256
POOL_WINDOWS = (2, 4, 8, 16)
ALPHA = 2.0 ** 0.25
LN_EPS = 1e-5
RMS_EPS = 1e-5
LR, B1, B2, ADAM_EPS, WD, STEP = 0.001, 0.9, 0.999, 1e-08, 0.01, 10
N_DEV = 8
LANES = 128
SUBLANES = 8
VMEM_LIMIT = 56 * 1024 * 1024
NEG_BIG = -1e30

NN = (((1,), (0,)), ((), ()))
NT = (((1,), (1,)), ((), ()))
TN = (((0,), (0,)), ((), ()))


def _dot(a, b, dims=NN):
    return lax.dot_general(a.astype(BF16), b.astype(BF16), dims, preferred_element_type=F32)


def _dot_exact01(q, e, dims=NN):
    hi = q.astype(BF16)
    r1 = q - hi.astype(F32)
    mid = r1.astype(BF16)
    lo = (r1 - mid.astype(F32)).astype(BF16)
    f = lambda p: lax.dot_general(p, e, dims, preferred_element_type=F32)
    return f(hi) + f(mid) + f(lo)


def _params(sem):
    return pltpu.CompilerParams(dimension_semantics=sem, vmem_limit_bytes=VMEM_LIMIT)


def _sigmoid(x):
    return 1.0 / (1.0 + jnp.exp(-x))


def _colsum(x):
    return jnp.sum(x, axis=0, keepdims=True)


def _ln_fwd(r):
    mu = jnp.mean(r, axis=-1, keepdims=True)
    xc = r - mu
    var = jnp.mean(xc * xc, axis=-1, keepdims=True)
    rstd = lax.rsqrt(var + LN_EPS)
    return xc * rstd, rstd


def _ln_bwd(dy, xhat, rstd, g):
    dxh = dy * g
    m1 = jnp.mean(dxh, axis=-1, keepdims=True)
    m2 = jnp.mean(dxh * xhat, axis=-1, keepdims=True)
    return rstd * (dxh - m1 - xhat * m2)


_Comm = collections.namedtuple("_Comm", "inputs out_shapes scratch start wait")
ANY = pl.BlockSpec(memory_space=pl.ANY)


def _fuse_comm(body, grid, n_in, n_out, comm):
    if comm is None:
        return body
    ci, co = len(comm.inputs), len(comm.out_shapes)

    def fused(*refs):
        ins, cins = refs[:n_in], refs[n_in:n_in + ci]
        o0 = n_in + ci
        outs, couts = refs[o0:o0 + n_out], refs[o0 + n_out:o0 + n_out + co]
        rest = refs[o0 + n_out + co:]
        scr, cscr = rest[:len(rest) - len(comm.scratch)], rest[len(rest) - len(comm.scratch):]
        ids = [pl.program_id(a) for a in range(len(grid))]
        first, last = ids[0] == 0, ids[0] == grid[0] - 1
        for a in range(1, len(grid)):
            first, last = first & (ids[a] == 0), last & (ids[a] == grid[a] - 1)

        @pl.when(first)
        def _():
            comm.start(cins, couts, cscr)

        body(*ins, *outs, *scr)

        @pl.when(last)
        def _():
            comm.wait(cins, couts, cscr)

    return fused


def _comm_specs(comm):
    if comm is None:
        return [], [], [], []
    return list(comm.inputs), [ANY] * len(comm.inputs), [ANY] * len(comm.out_shapes), list(comm.out_shapes)


def _run_comm(comm, name):
    ci, co = len(comm.inputs), len(comm.out_shapes)

    def body(*refs):
        comm.start(refs[:ci], refs[ci:ci + co], refs[ci + co:])
        comm.wait(refs[:ci], refs[ci:ci + co], refs[ci + co:])

    return pl.pallas_call(body, name=name, in_specs=[ANY] * ci, out_specs=[ANY] * co, out_shape=list(comm.out_shapes),
                          scratch_shapes=list(comm.scratch))(*comm.inputs)


def _matmul(a, b, mode, out_dtype, bm, bn, bk, name, a_fn=None, col_blocks=0, comm=None):
    if mode == "nn":
        (m, k), n, dims = a.shape, b.shape[1], NN
    elif mode == "nt":
        (m, k), n, dims = a.shape, b.shape[0], NT
    else:
        (k, m), n, dims = a.shape, b.shape[1], TN
    bm, bn, bk = min(bm, m), min(bn, n), min(bk, k)
    assert m % bm == 0 and n % bn == 0 and k % bk == 0, (name, m, n, k, bm, bn, bk)
    nk = k // bk
    if mode == "nn":
        a_spec = pl.BlockSpec((bm, bk), lambda i, j, kk: (i, kk))
        b_spec = pl.BlockSpec((bk, bn), lambda i, j, kk: (kk, j))
    elif mode == "nt":
        a_spec = pl.BlockSpec((bm, bk), lambda i, j, kk: (i, kk))
        b_spec = pl.BlockSpec((bn, bk), lambda i, j, kk: (j, kk))
    else:
        a_spec = pl.BlockSpec((bk, bm), lambda i, j, kk: (kk, i))
        b_spec = pl.BlockSpec((bk, bn), lambda i, j, kk: (kk, j))

    def body(a_ref, b_ref, o_ref, acc_ref):
        kk = pl.program_id(2)
        av = a_ref[...]
        if a_fn is not None:
            av = a_fn(av.astype(F32))
        prod = _dot(av, b_ref[...], dims)

        def emit(total):
            if col_blocks:
                for s in range(bn // slab):
                    o_ref[s] = total[:, s * slab:(s + 1) * slab].astype(o_ref.dtype)
            else:
                o_ref[...] = total.astype(o_ref.dtype)

        if nk == 1:
            emit(prod)
        else:
            @pl.when(kk == 0)
            def _():
                acc_ref[...] = prod

            @pl.when((kk > 0) & (kk < nk - 1))
            def _():
                acc_ref[...] += prod

            @pl.when(kk == nk - 1)
            def _():
                emit(acc_ref[...] + prod)

    if col_blocks:
        slab = n // col_blocks
        assert n % col_blocks == 0 and bn % slab == 0, (name, n, col_blocks, bn)
        out_spec = pl.BlockSpec((bn // slab, bm, slab), lambda i, j, kk: (j, i, 0))
        out_shape = jax.ShapeDtypeStruct((col_blocks, m, slab), out_dtype)
    else:
        out_spec = pl.BlockSpec((bm, bn), lambda i, j, kk: (i, j))
        out_shape = jax.ShapeDtypeStruct((m, n), out_dtype)
    grid = (m // bm, n // bn, nk)
    c_in, c_in_specs, c_out_specs, c_out_shapes = _comm_specs(comm)
    res = pl.pallas_call(
        _fuse_comm(body, grid, 2, 1, comm), name=name,
        grid=grid,
        in_specs=[a_spec, b_spec] + c_in_specs,
        out_specs=[out_spec] + c_out_specs,
        out_shape=[out_shape] + c_out_shapes,
        scratch_shapes=[pltpu.VMEM((bm, bn), F32)] + (list(comm.scratch) if comm else []),
        compiler_params=_params(("arbitrary",) * 3 if comm else ("parallel", "parallel", "arbitrary")),
    )(a, b, *c_in)
    return res if comm else res[0]


CONV_STRIP = 16
CONV_COLS = 512


def _conv_pre(ext_ref, w_ref, b_ref, r0, rows, cols=slice(None)):
    n = rows + SUBLANES
    win = ext_ref[pl.ds(r0, n), cols]
    acc = b_ref[:, cols] + w_ref[CONV_K - 1:CONV_K, cols] * win[SUBLANES:, :]
    for k in range(CONV_K - 1):
        off = SUBLANES - (CONV_K - 1) + k
        acc = acc + w_ref[k:k + 1, cols] * pltpu.roll(win, n - off, 0)[0:rows, :]
    return acc


def _in_proj(xb, w_main, conv_w8, conv_b, cd, seq_len, bm, bn, comm):
    t, d = xb.shape
    pw = w_main.shape[1]
    bm, bn = min(bm, seq_len), min(bn, d)
    assert t % bm == 0 and seq_len % bm == 0 and pw % bn == 0 and cd % bn == 0
    ncj = cd // bn
    tiles_per_seq = seq_len // bm

    def body(x_ref, w_ref, cw_ref, cb_ref, p_ref, xbc_ref, dsl_ref, ext_ref, carry_ref):
        i = pl.program_id(0)
        j = pl.program_id(1)
        pq = _dot(x_ref[...], w_ref[...]).astype(BF16)
        p_ref[...] = pq

        @pl.when(j < ncj)
        def _():
            jc = jnp.minimum(j, ncj - 1)
            ext_ref[0:SUBLANES, :] = jnp.where((i % tiles_per_seq) == 0, 0.0, carry_ref[jc])
            ext_ref[SUBLANES:, :] = pq.astype(F32)
            carry_ref[jc] = ext_ref[bm:bm + SUBLANES, :]
            cw = min(bn, CONV_COLS)
            for c0 in range(0, bn, cw):
                cols = slice(c0, c0 + cw)
                for r0 in range(0, bm, CONV_STRIP):
                    rows = slice(r0, r0 + CONV_STRIP)
                    acc = _conv_pre(ext_ref, cw_ref, cb_ref, r0, CONV_STRIP, cols)
                    sg = _sigmoid(acc)
                    xbc_ref[rows, cols] = (acc * sg).astype(xbc_ref.dtype)
                    dsl_ref[rows, cols] = (sg * (1.0 + acc * (1.0 - sg))).astype(dsl_ref.dtype)

    grid = (t // bm, pw // bn)
    conv_col = lambda i, j: (0, jnp.minimum(j, ncj - 1))
    c_in, c_in_specs, c_out_specs, c_out_shapes = _comm_specs(comm)
    conv_tile = pl.BlockSpec((bm, bn), lambda i, j: (i, jnp.minimum(j, ncj - 1)))
    conv_out = jax.ShapeDtypeStruct((t, cd), BF16)
    return pl.pallas_call(
        _fuse_comm(body, grid, 4, 3, comm), name="in_proj",
        grid=grid,
        in_specs=[pl.BlockSpec((bm, d), lambda i, j: (i, 0)), pl.BlockSpec((d, bn), lambda i, j: (0, j)),
                  pl.BlockSpec((SUBLANES, bn), conv_col), pl.BlockSpec((1, bn), conv_col)] + c_in_specs,
        out_specs=[pl.BlockSpec((bm, bn), lambda i, j: (i, j)), conv_tile, conv_tile] + c_out_specs,
        out_shape=[jax.ShapeDtypeStruct((t, pw), BF16), conv_out, conv_out] + c_out_shapes,
        scratch_shapes=[pltpu.VMEM((bm + SUBLANES, bn), F32), pltpu.VMEM((ncj, SUBLANES, bn), F32)]
        + (list(comm.scratch) if comm else []),
        compiler_params=_params(("arbitrary", "arbitrary")),
    )(xb, w_main, conv_w8, conv_b, *c_in)


def _conv_bwd(proj, dsilu, dxbc, conv_w8, n_seq_chunks, col0, width, ct, L, name):
    t = proj.shape[0]
    nbc = t // L
    hb = L // SUBLANES
    ct = min(ct, width)
    assert col0 % ct == 0 and width % ct == 0
    cb0 = col0 // ct
    last_hb = t // SUBLANES - 1

    def body(x_ref, xb_ref, s_ref, sa_ref, d_ref, da_ref, w_ref, o_ref, dw_ref, db_ref, ext_ref, dc_ref):
        bc = pl.program_id(1)
        first = (bc % n_seq_chunks) == 0
        last = (bc % n_seq_chunks) == n_seq_chunks - 1

        @pl.when(bc == 0)
        def _():
            dw_ref[...] = jnp.zeros_like(dw_ref)
            db_ref[...] = jnp.zeros_like(db_ref)

        ext_ref[0:SUBLANES, :] = jnp.where(first, 0.0, xb_ref[...].astype(F32))
        ext_ref[SUBLANES:, :] = x_ref[...].astype(F32)
        for r0 in range(0, L, CONV_STRIP):
            rows = slice(r0, r0 + CONV_STRIP)
            dc_ref[rows, :] = d_ref[rows, :].astype(F32) * s_ref[rows, :].astype(F32)
        dc_ref[L:, :] = jnp.where(last, 0.0, da_ref[...].astype(F32)) * sa_ref[...].astype(F32)
        fold = lambda v: v[0:SUBLANES] + v[SUBLANES:CONV_STRIP]
        dws = [jnp.zeros((SUBLANES, ct), F32) for _ in range(CONV_K)]
        dbs = jnp.zeros((SUBLANES, ct), F32)
        n = CONV_STRIP + SUBLANES
        for r0 in range(0, L, CONV_STRIP):
            dcw = dc_ref[pl.ds(r0, n), :]
            xq = ext_ref[r0 + SUBLANES:r0 + SUBLANES + CONV_STRIP, :]
            dx = jnp.zeros((CONV_STRIP, ct), F32)
            for k in range(CONV_K):
                off = CONV_K - 1 - k
                dck = dcw[0:CONV_STRIP, :] if off == 0 else pltpu.roll(dcw, n - off, 0)[0:CONV_STRIP, :]
                dx = dx + w_ref[k:k + 1, :] * dck
                dws[k] = dws[k] + fold(dck * xq)
            dbs = dbs + fold(dcw[0:CONV_STRIP, :])
            o_ref[r0:r0 + CONV_STRIP, :] = dx.astype(o_ref.dtype)
        for k in range(CONV_K):
            dw_ref[k:k + 1, :] += _colsum(dws[k])
        db_ref[0:1, :] += _colsum(dbs)

    return pl.pallas_call(
        body, name=name,
        grid=(width // ct, nbc),
        in_specs=[
            pl.BlockSpec((L, ct), lambda j, i: (i, cb0 + j)),
            pl.BlockSpec((SUBLANES, ct), lambda j, i: (jnp.maximum(i * hb - 1, 0), cb0 + j)),
            pl.BlockSpec((L, ct), lambda j, i: (i, cb0 + j)),
            pl.BlockSpec((SUBLANES, ct), lambda j, i: (jnp.minimum((i + 1) * hb, last_hb), cb0 + j)),
            pl.BlockSpec((L, ct), lambda j, i: (i, j)),
            pl.BlockSpec((SUBLANES, ct), lambda j, i: (jnp.minimum((i + 1) * hb, last_hb), j)),
            pl.BlockSpec((SUBLANES, ct), lambda j, i: (0, cb0 + j)),
        ],
        out_specs=[
            pl.BlockSpec((L, ct), lambda j, i: (i, j)),
            pl.BlockSpec((SUBLANES, ct), lambda j, i: (0, j)),
            pl.BlockSpec((SUBLANES, ct), lambda j, i: (0, j)),
        ],
        out_shape=[
            jax.ShapeDtypeStruct((t, width), BF16),
            jax.ShapeDtypeStruct((SUBLANES, width), F32),
            jax.ShapeDtypeStruct((SUBLANES, width), F32),
        ],
        scratch_shapes=[pltpu.VMEM((L + SUBLANES, ct), F32), pltpu.VMEM((L + SUBLANES, ct), F32)],
        compiler_params=_params(("parallel", "arbitrary")),
    )(proj, proj, dsilu, dsilu, dxbc, dxbc, conv_w8)


def _cumsum_rows(x, reverse=False):
    n = x.shape[0]
    row = lax.broadcasted_iota(jnp.int32, x.shape, 0)
    s = 1
    while s < n:
        if reverse:
            x = x + jnp.where(row < n - s, pltpu.roll(x, n - s, 0), 0.0)
        else:
            x = x + jnp.where(row >= s, pltpu.roll(x, s, 0), 0.0)
        s *= 2
    return x


def _ssd_scalars(dtr, dtb, alog):
    pre = dtr + dtb
    dt = jnp.maximum(pre, 0.0) + jnp.log(1.0 + jnp.exp(-jnp.abs(pre)))
    a = -jnp.exp(alog)
    acs = _cumsum_rows(dt * a) * LOG2E
    n = acs.shape[0]
    return pre, dt, a, acs, jnp.exp2(acs), jnp.exp2(acs[n - 1:n, :] - acs)


LOG2E = 1.4426950408889634


def _dot_2piece(q, e):
    hi = q.astype(BF16)
    mid = (q - hi.astype(F32)).astype(BF16)
    return lax.dot_general(jnp.concatenate([hi, mid], axis=1), jnp.concatenate([e, e], axis=0), NN,
                           preferred_element_type=F32)


def _ssd_group_common(dt_s, e_s, dec_s, e):
    return _dot_2piece(dt_s, e), _dot_2piece(e_s, e), _dot_2piece(dec_s, e)


def _decay_matrix(acs, acs_t, h, tri):
    return jnp.exp2(jnp.where(tri, acs[:, h:h + 1] - acs_t[h:h + 1, :], NEG_BIG))


def _head_mask(r, gw, dtype):
    lane = lax.broadcasted_iota(jnp.int32, (1, gw), 1)
    return ((lane >= r * HEAD_DIM) & (lane < (r + 1) * HEAD_DIM)).astype(dtype)


def _ssd_fwd(xbc, proj, dt_raw, dtb, alog, dskip_x, normw, emat, bl, inner, z_col0):
    t = xbc.shape[0]
    L = CHUNK
    nc = t // bl // L
    G = GROUPS
    gw = inner // G
    hpg = gw // HEAD_DIM
    assert z_col0 % gw == 0
    zb0 = z_col0 // gw
    bb0 = inner // STATE
    cb0 = bb0 + G

    P = G
    assert bb0 % P == 0 and cb0 % P == 0 and zb0 % P == 0

    def body(xs_ref, b_ref, c_ref, z_ref, dtr_ref, dtb_ref, alog_ref, dsk_ref, nw_ref, e_ref,
             y_ref, yn_ref, st_ref, h_ref):
        c = pl.program_id(1)
        _, dt_s, _, acs, e_s, dec_s = _ssd_scalars(dtr_ref[...], dtb_ref[...], alog_ref[...])
        acs_t = acs.T
        tri = lax.broadcasted_iota(jnp.int32, (L, L), 0) >= lax.broadcasted_iota(jnp.int32, (L, L), 1)
        lane = lax.broadcasted_iota(jnp.int32, (L, gw), 1)
        for g in range(G):
            cols = slice(g * gw, (g + 1) * gw)
            ncol = slice(g * STATE, (g + 1) * STATE)

            @pl.when(c == 0)
            def _():
                h_ref[g] = jnp.zeros((STATE, gw), F32)

            xs = xs_ref[:, cols].astype(F32)
            bg = b_ref[:, ncol]
            cg = c_ref[:, ncol]
            dt_x, e_x, dec_x = _ssd_group_common(dt_s, e_s, dec_s, e_ref[:, cols])
            xdt = xs * dt_x
            cb = _dot(cg, bg, NT)
            h = h_ref[g]
            st_ref[0, g] = h
            y = _dot(cg, h) * e_x + dsk_ref[:, cols] * xs
            for r in range(hpg):
                m = cb * _decay_matrix(acs, acs_t, g * hpg + r, tri)
                xr = jnp.where((lane >= r * HEAD_DIM) & (lane < (r + 1) * HEAD_DIM), xdt, 0.0)
                y = y + _dot(m, xr)
            h_ref[g] = h * e_x[L - 1:L, :] + _dot(bg, xdt * dec_x, TN)
            yq = y.astype(y_ref.dtype)
            y_ref[:, cols] = yq
            z = z_ref[:, cols].astype(F32)
            yg = yq.astype(F32) * (z * _sigmoid(z))
            rs = lax.rsqrt(jnp.mean(yg * yg, axis=-1, keepdims=True) + RMS_EPS)
            yn_ref[:, cols] = (yg * rs * nw_ref[:, cols]).astype(yn_ref.dtype)

    return pl.pallas_call(
        body, name="ssd_fwd",
        grid=(bl, nc, G // P),
        in_specs=[
            pl.BlockSpec((L, P * gw), lambda b, c, g: (b * nc + c, g)),
            pl.BlockSpec((L, P * STATE), lambda b, c, g: (b * nc + c, bb0 // P + g)),
            pl.BlockSpec((L, P * STATE), lambda b, c, g: (b * nc + c, cb0 // P + g)),
            pl.BlockSpec((L, P * gw), lambda b, c, g: (b * nc + c, zb0 // P + g)),
            pl.BlockSpec((L, LANES), lambda b, c, g: (b * nc + c, 0)),
            pl.BlockSpec((1, LANES), lambda b, c, g: (0, 0)),
            pl.BlockSpec((1, LANES), lambda b, c, g: (0, 0)),
            pl.BlockSpec((1, P * gw), lambda b, c, g: (0, g)),
            pl.BlockSpec((1, P * gw), lambda b, c, g: (0, g)),
            pl.BlockSpec((LANES, P * gw), lambda b, c, g: (0, g)),
        ],
        out_specs=[
            pl.BlockSpec((L, P * gw), lambda b, c, g: (b * nc + c, g)),
            pl.BlockSpec((L, P * gw), lambda b, c, g: (b * nc + c, g)),
            pl.BlockSpec((1, P, STATE, gw), lambda b, c, g: (b * nc + c, g, 0, 0)),
        ],
        out_shape=[
            jax.ShapeDtypeStruct((t, inner), BF16),
            jax.ShapeDtypeStruct((t, inner), BF16),
            jax.ShapeDtypeStruct((bl * nc, G, STATE, gw), F32),
        ],
        scratch_shapes=[pltpu.VMEM((G, STATE, gw), F32)],
        compiler_params=_params(("arbitrary", "arbitrary", "arbitrary")),
    )(xbc, xbc, xbc, proj, dt_raw, dtb, alog, dskip_x, normw, emat)


def _ssd_bwd(xbc, proj, dt_raw, y, dyn, states, dtb, alog, dskip_x, normw, emat, emat_t, bl, inner, z_col0,
             comm=None):
    t = xbc.shape[0]
    L = CHUNK
    nc = t // bl // L
    G = GROUPS
    gw = inner // G
    hpg = gw // HEAD_DIM
    zb0 = z_col0 // gw
    bb0 = inner // STATE
    cb0 = bb0 + G
    P = G

    def rc(j):
        return nc - 1 - j

    def body(xs_ref, b_ref, c_ref, z_ref, dtr_ref, y_ref, dyn_ref, st_ref, dtb_ref, alog_ref, dsk_ref,
             nw_ref, e_ref, et_ref,
             dxs_ref, db_ref, dc_ref, dz_ref, ddt_ref, dnw_ref, dsk_acc, dalog_acc, ddtb_acc,
             dh_ref):
        b = pl.program_id(0)
        j = pl.program_id(1)

        @pl.when((b == 0) & (j == 0))
        def _():
            dsk_acc[...] = jnp.zeros_like(dsk_acc)
            dalog_acc[...] = jnp.zeros_like(dalog_acc)
            ddtb_acc[...] = jnp.zeros_like(ddtb_acc)

        pre, dt_s, a_row, acs, e_s, dec_s = _ssd_scalars(dtr_ref[...], dtb_ref[...], alog_ref[...])
        acs_t = acs.T
        wacs = jnp.zeros((L, LANES), F32)
        wdt = jnp.zeros((L, LANES), F32)
        tri = lax.broadcasted_iota(jnp.int32, (L, L), 0) >= lax.broadcasted_iota(jnp.int32, (L, L), 1)
        rowi = lax.broadcasted_iota(jnp.int32, (L, gw), 0)
        for g in range(G):
            cols = slice(g * gw, (g + 1) * gw)
            ncol = slice(g * STATE, (g + 1) * STATE)

            @pl.when((b == 0) & (j == 0))
            def _():
                dnw_ref[g] = jnp.zeros((SUBLANES, gw), F32)

            @pl.when(j == 0)
            def _():
                dh_ref[g] = jnp.zeros((STATE, gw), F32)

            xs = xs_ref[:, cols].astype(F32)
            bg = b_ref[:, ncol]
            cg = c_ref[:, ncol]
            dt_x, e_x, dec_x = _ssd_group_common(dt_s, e_s, dec_s, e_ref[:, cols])
            xdt = xs * dt_x
            xdt_b = xdt.astype(BF16)
            cb = _dot(cg, bg, NT)
            h = st_ref[0, g]
            hb16 = h.astype(BF16)
            dsk = dsk_ref[:, cols]

            yv = y_ref[:, cols].astype(F32)
            z = z_ref[:, cols].astype(F32)
            sgz = _sigmoid(z)
            sz = z * sgz
            yg = yv * sz
            rs = lax.rsqrt(jnp.mean(yg * yg, axis=-1, keepdims=True) + RMS_EPS)
            yhat = yg * rs
            dyn_v = dyn_ref[:, cols].astype(F32)
            dnw_ref[g] += _colsum(dyn_v * yhat)
            dyh = dyn_v * nw_ref[:, cols]
            dyg = rs * (dyh - yhat * jnp.mean(dyh * yhat, axis=-1, keepdims=True))
            dy = dyg * sz
            dz_ref[:, cols] = (dyg * yv * (sgz * (1.0 + z * (1.0 - sgz)))).astype(dz_ref.dtype)

            dy_b = dy.astype(BF16)
            dcb = jnp.zeros((L, L), F32)
            dxdt_d = jnp.zeros((L, gw), F32)
            ydiag = jnp.zeros((L, gw), F32)
            for r in range(hpg):
                lm = _decay_matrix(acs, acs_t, g * hpg + r, tri)
                m = (cb * lm).astype(BF16)
                hm = _head_mask(r, gw, BF16)
                dyr = dy_b * hm
                xr = xdt_b * hm
                ydiag = ydiag + _dot(m, xr)
                dcb = dcb + _dot(dyr, xdt_b, NT) * lm
                dxdt_d = dxdt_d + _dot(m, dyr, TN)
            dh = dh_ref[g]
            dh16 = dh.astype(BF16)
            xdec_b = (xdt * dec_x).astype(BF16)
            bdh = _dot(bg, dh16)
            dxdt = dxdt_d + dec_x * bdh
            dcb16 = dcb.astype(BF16)
            dye = (dy * e_x).astype(BF16)
            db_ref[:, ncol] = (_dot(dcb16, cg, TN) + _dot(xdec_b, dh16, NT)).astype(db_ref.dtype)
            dc_ref[:, ncol] = (_dot(dcb16, bg) + _dot(dye, hb16, NT)).astype(dc_ref.dtype)
            dprev = _dot(cg, dye, TN)
            cd_row = e_x[L - 1:L, :]
            s_new = _dot(bg, xdec_b, TN)
            last_term = _colsum(dh16.astype(F32) * s_new) + _colsum(dh * h) * cd_row
            yoff = _dot(cg, hb16) * e_x
            wfold = (dy_b.astype(F32) * ydiag + dy * yoff - dxdt_d * xdt_b.astype(F32) - bdh * xdec_b.astype(F32)
                     + jnp.where(rowi == L - 1, last_term, 0.0))
            et = et_ref[cols, :]
            wacs = wacs + _dot_2piece(wfold, et)
            wdt = wdt + _dot_2piece(dxdt * xs, et)
            dsk_acc[...] += _dot_exact01(jnp.broadcast_to(_colsum(dy * xs), (SUBLANES, gw)), et)
            dxs_ref[:, cols] = (dsk * dy + dxdt * dt_x).astype(dxs_ref.dtype)
            dh_ref[g] = dprev + cd_row * dh

        dda = _cumsum_rows(wacs, reverse=True)
        ddt_raw = (wdt + dda * a_row) * _sigmoid(pre)
        ddt_ref[...] = ddt_raw
        dalog_acc[...] += _colsum(dda * dt_s) * a_row
        ddtb_acc[...] += _colsum(ddt_raw)

    def cidx(b, j):
        return b * nc + rc(j)

    accs = lambda shape: pl.BlockSpec(shape, lambda b, j, g: tuple(0 for _ in shape))
    grid = (bl, nc, G // P)
    c_in, c_in_specs, c_out_specs, c_out_shapes = _comm_specs(comm)
    return pl.pallas_call(
        _fuse_comm(body, grid, 14, 9, comm), name="ssd_bwd",
        grid=grid,
        in_specs=[
            pl.BlockSpec((L, P * gw), lambda b, j, g: (cidx(b, j), g)),
            pl.BlockSpec((L, P * STATE), lambda b, j, g: (cidx(b, j), bb0 // P + g)),
            pl.BlockSpec((L, P * STATE), lambda b, j, g: (cidx(b, j), cb0 // P + g)),
            pl.BlockSpec((L, P * gw), lambda b, j, g: (cidx(b, j), zb0 // P + g)),
            pl.BlockSpec((L, LANES), lambda b, j, g: (cidx(b, j), 0)),
            pl.BlockSpec((L, P * gw), lambda b, j, g: (cidx(b, j), g)),
            pl.BlockSpec((L, P * gw), lambda b, j, g: (cidx(b, j), g)),
            pl.BlockSpec((1, P, STATE, gw), lambda b, j, g: (cidx(b, j), g, 0, 0)),
            pl.BlockSpec((1, LANES), lambda b, j, g: (0, 0)),
            pl.BlockSpec((1, LANES), lambda b, j, g: (0, 0)),
            pl.BlockSpec((1, P * gw), lambda b, j, g: (0, g)),
            pl.BlockSpec((1, P * gw), lambda b, j, g: (0, g)),
            pl.BlockSpec((LANES, P * gw), lambda b, j, g: (0, g)),
            pl.BlockSpec((P * gw, LANES), lambda b, j, g: (g, 0)),
        ] + c_in_specs,
        out_specs=[
            pl.BlockSpec((L, P * gw), lambda b, j, g: (cidx(b, j), g)),
            pl.BlockSpec((L, P * STATE), lambda b, j, g: (cidx(b, j), g)),
            pl.BlockSpec((L, P * STATE), lambda b, j, g: (cidx(b, j), g)),
            pl.BlockSpec((L, P * gw), lambda b, j, g: (cidx(b, j), g)),
            pl.BlockSpec((L, LANES), lambda b, j, g: (cidx(b, j), 0)),
            accs((G, SUBLANES, gw)),
            accs((SUBLANES, LANES)),
            accs((SUBLANES, LANES)),
            accs((SUBLANES, LANES)),
        ] + c_out_specs,
        out_shape=[
            jax.ShapeDtypeStruct((t, inner), BF16),
            jax.ShapeDtypeStruct((t, G * STATE), BF16),
            jax.ShapeDtypeStruct((t, G * STATE), BF16),
            jax.ShapeDtypeStruct((t, inner), BF16),
            jax.ShapeDtypeStruct((t, LANES), F32),
            jax.ShapeDtypeStruct((G, SUBLANES, gw), F32),
            jax.ShapeDtypeStruct((SUBLANES, LANES), F32),
            jax.ShapeDtypeStruct((SUBLANES, LANES), F32),
            jax.ShapeDtypeStruct((SUBLANES, LANES), F32),
        ] + c_out_shapes,
        scratch_shapes=[pltpu.VMEM((G, STATE, gw), F32)]
        + (list(comm.scratch) if comm else []),
        compiler_params=_params(("arbitrary", "arbitrary", "arbitrary")),
    )(xbc, xbc, xbc, proj, dt_raw, y, dyn, states, dtb, alog, dskip_x, normw, emat, emat_t, *c_in)


def _pool_window(u, w, anti):
    n = u.shape[0]
    row = lax.broadcasted_iota(jnp.int32, u.shape, 0)
    acc = u
    s = 1
    while s < w:
        if anti:
            acc = acc + jnp.where(row < n - s, pltpu.roll(acc, n - s, 0), 0.0)
        else:
            acc = acc + jnp.where(row >= s, pltpu.roll(acc, s, 0), 0.0)
        s *= 2
    return acc


def _pool_cnt(shape, w):
    row = lax.broadcasted_iota(jnp.int32, shape, 0)
    return jnp.minimum(row + 1, w).astype(F32)


def _pool_fwd(proj, wpg, bl, d, u_col0):
    t = proj.shape[0]
    s = t // bl
    pg = len(POOL_WINDOWS)
    cg = d // pg
    ub0 = u_col0 // d

    def body(u_ref, w_ref, o_ref):
        for gi, w in enumerate(POOL_WINDOWS):
            u = u_ref[:, gi * cg:(gi + 1) * cg].astype(F32)
            pooled = _pool_window(u, w, False) / _pool_cnt(u.shape, w) - u
            o_ref[:, gi * cg:(gi + 1) * cg] = _dot(pooled, w_ref[gi]).astype(o_ref.dtype)

    return pl.pallas_call(
        body, name="pool_fwd",
        grid=(bl,),
        in_specs=[pl.BlockSpec((s, d), lambda b: (b, ub0)), pl.BlockSpec((pg, cg, cg), lambda b: (0, 0, 0))],
        out_specs=pl.BlockSpec((s, d), lambda b: (b, 0)),
        out_shape=jax.ShapeDtypeStruct((t, d), BF16),
        compiler_params=_params(("parallel",)),
    )(proj, wpg)


def _pool_bwd(proj, dyp, wpg, bl, d, u_col0):
    t = proj.shape[0]
    s = t // bl
    pg = len(POOL_WINDOWS)
    cg = d // pg
    ub0 = u_col0 // d

    def body(u_ref, dy_ref, w_ref, du_ref, dw_ref):
        @pl.when(pl.program_id(0) == 0)
        def _():
            dw_ref[...] = jnp.zeros_like(dw_ref)

        for gi, w in enumerate(POOL_WINDOWS):
            u = u_ref[:, gi * cg:(gi + 1) * cg].astype(F32)
            cnt = _pool_cnt(u.shape, w)
            pooled = _pool_window(u, w, False) / cnt - u
            dy = dy_ref[:, gi * cg:(gi + 1) * cg]
            dw_ref[gi] += _dot(pooled, dy, TN)
            dp = _dot(dy, w_ref[gi], NT)
            du_ref[:, gi * cg:(gi + 1) * cg] = (_pool_window(dp / cnt, w, True) - dp).astype(du_ref.dtype)

    return pl.pallas_call(
        body, name="pool_bwd",
        grid=(bl,),
        in_specs=[pl.BlockSpec((s, d), lambda b: (b, ub0)), pl.BlockSpec((s, d), lambda b: (b, 0)),
                  pl.BlockSpec((pg, cg, cg), lambda b: (0, 0, 0))],
        out_specs=[pl.BlockSpec((s, d), lambda b: (b, 0)), pl.BlockSpec((pg, cg, cg), lambda b: (0, 0, 0))],
        out_shape=[jax.ShapeDtypeStruct((t, d), BF16), jax.ShapeDtypeStruct((pg, cg, cg), F32)],
        compiler_params=_params(("arbitrary",)),
    )(proj, dyp, wpg)


def _merge_fwd(proj, ypr, yssd, x, w_out, b_gates, pool_scale, d, lg_col0, tm):
    t = x.shape[0]
    lb0 = lg_col0 // (2 * d)

    def body(lg_ref, yp_ref, ys_ref, x_ref, w_ref, bg_ref, ps_ref, mg_ref, r1_ref):
        lg = lg_ref[...].astype(F32) + bg_ref[...]
        ga = _sigmoid(lg[:, :d])
        gb = _sigmoid(lg[:, d:])
        merged = ga * (yp_ref[...].astype(F32) * ps_ref[...]) + gb * ys_ref[...].astype(F32)
        mg_ref[...] = merged.astype(mg_ref.dtype)
        r1_ref[...] = ALPHA * x_ref[...] + _dot(mg_ref[...], w_ref[...])

    row = lambda w: pl.BlockSpec((tm, w), lambda i: (i, 0))
    full = lambda a: pl.BlockSpec(a.shape, lambda i: (0, 0))
    return pl.pallas_call(
        body, name="merge_fwd",
        grid=(t // tm,),
        in_specs=[pl.BlockSpec((tm, 2 * d), lambda i: (i, lb0)), row(d), row(d), row(d), full(w_out), full(b_gates),
                  full(pool_scale)],
        out_specs=[row(d), row(d)],
        out_shape=[jax.ShapeDtypeStruct((t, d), BF16), jax.ShapeDtypeStruct((t, d), F32)],
        compiler_params=_params(("parallel",)),
    )(proj, ypr, yssd, x, w_out, b_gates, pool_scale)


def _merge_bwd(dr1, proj, ypr, yssd, w_out, b_gates, pool_scale, d, lg_col0, tm):
    t = dr1.shape[0]
    lb0 = lg_col0 // (2 * d)

    def body(dr_ref, lg_ref, yp_ref, ys_ref, w_ref, bg_ref, ps_ref, dlg_ref, dyp_ref, dys_ref, dbg_ref, dps_ref):
        @pl.when(pl.program_id(0) == 0)
        def _():
            dbg_ref[...] = jnp.zeros_like(dbg_ref)
            dps_ref[...] = jnp.zeros_like(dps_ref)

        dm = _dot(dr_ref[...], w_ref[...], NT)
        lg = lg_ref[...].astype(F32) + bg_ref[...]
        ga = _sigmoid(lg[:, :d])
        gb = _sigmoid(lg[:, d:])
        ypr_v = yp_ref[...].astype(F32)
        ys_v = ys_ref[...].astype(F32)
        ps = ps_ref[...]
        dga = dm * ypr_v * ps
        dla = dga * ga * (1.0 - ga)
        dlb = dm * ys_v * gb * (1.0 - gb)
        dlg_ref[:, :d] = dla.astype(dlg_ref.dtype)
        dlg_ref[:, d:] = dlb.astype(dlg_ref.dtype)
        dyp_ref[...] = (dm * ga * ps).astype(dyp_ref.dtype)
        dys_ref[...] = (dm * gb).astype(dys_ref.dtype)
        dbg_ref[0:1, :d] += _colsum(dla)
        dbg_ref[0:1, d:] += _colsum(dlb)
        dps_ref[0:1, :] += _colsum(dm * ga * ypr_v)

    row = lambda w: pl.BlockSpec((tm, w), lambda i: (i, 0))
    full = lambda a: pl.BlockSpec(a.shape, lambda i: (0, 0))
    acc = lambda w: pl.BlockSpec((SUBLANES, w), lambda i: (0, 0))
    return pl.pallas_call(
        body, name="merge_bwd",
        grid=(t // tm,),
        in_specs=[row(d), pl.BlockSpec((tm, 2 * d), lambda i: (i, lb0)), row(d), row(d), full(w_out), full(b_gates),
                  full(pool_scale)],
        out_specs=[row(2 * d), row(d), row(d), acc(2 * d), acc(d)],
        out_shape=[jax.ShapeDtypeStruct((t, 2 * d), BF16), jax.ShapeDtypeStruct((t, d), BF16),
                   jax.ShapeDtypeStruct((t, d), BF16), jax.ShapeDtypeStruct((SUBLANES, 2 * d), F32),
                   jax.ShapeDtypeStruct((SUBLANES, d), F32)],
        compiler_params=_params(("arbitrary",)),
    )(dr1, proj, ypr, yssd, w_out, b_gates, pool_scale)


MLP_SLABS_PER_STEP = 2


def _mlp_fwd(r1, target, w_up, w_down, ln1_g, ln1_b, ln2_g, ln2_b, tm):
    t, d = r1.shape
    ns, _, sw = w_up.shape
    spb = MLP_SLABS_PER_STEP
    assert ns % spb == 0
    nf, tf, ff = ns // spb, spb * sw, ns * sw

    def body(r1_ref, tg_ref, wu_ref, wd_ref, g1_ref, b1_ref, g2_ref, b2_ref,
             up_ref, h1_ref, dr2_ref, loss_ref, dg2_ref, db2_ref, h1f, acc):
        i = pl.program_id(0)
        f = pl.program_id(1)

        @pl.when((i == 0) & (f == 0))
        def _():
            loss_ref[...] = jnp.zeros_like(loss_ref)
            dg2_ref[...] = jnp.zeros_like(dg2_ref)
            db2_ref[...] = jnp.zeros_like(db2_ref)

        @pl.when(f == 0)
        def _():
            xhat, _ = _ln_fwd(r1_ref[...])
            h1 = xhat * g1_ref[...] + b1_ref[...]
            h1f[...] = h1
            h1_ref[...] = h1.astype(h1_ref.dtype)
            acc[...] = jnp.zeros_like(acc)

        for s in range(spb):
            up_ref[:, s * sw:(s + 1) * sw] = _dot(h1_ref[...], wu_ref[s]).astype(up_ref.dtype)
        upq = jnp.maximum(up_ref[...].astype(F32), 0.0)
        acc[...] += _dot(upq * upq, wd_ref[...])

        @pl.when(f == nf - 1)
        def _():
            xhat, rstd = _ln_fwd(ALPHA * h1f[...] + acc[...])
            g2 = g2_ref[...]
            diff = xhat * g2 + b2_ref[...] - tg_ref[...]
            loss_ref[...] += 0.5 / d * jnp.sum(diff * diff)
            dh2 = diff * (1.0 / d)
            dg2_ref[0:1, :] += _colsum(dh2 * xhat)
            db2_ref[0:1, :] += _colsum(dh2)
            dr2_ref[...] = _ln_bwd(dh2, xhat, rstd, g2).astype(dr2_ref.dtype)

    row = pl.BlockSpec((tm, d), lambda i, f: (i, 0))
    vec = pl.BlockSpec((1, d), lambda i, f: (0, 0))
    acc8 = pl.BlockSpec((SUBLANES, d), lambda i, f: (0, 0))
    return pl.pallas_call(
        body, name="mlp_fwd",
        grid=(t // tm, nf),
        in_specs=[row, row, pl.BlockSpec((spb, d, sw), lambda i, f: (f, 0, 0)), pl.BlockSpec((tf, d), lambda i, f: (f, 0)),
                  vec, vec, vec, vec],
        out_specs=[pl.BlockSpec((tm, tf), lambda i, f: (i, f)), row, row,
                   pl.BlockSpec((SUBLANES, LANES), lambda i, f: (0, 0)), acc8, acc8],
        out_shape=[jax.ShapeDtypeStruct((t, ff), BF16), jax.ShapeDtypeStruct((t, d), BF16),
                   jax.ShapeDtypeStruct((t, d), BF16), jax.ShapeDtypeStruct((SUBLANES, LANES), F32),
                   jax.ShapeDtypeStruct((SUBLANES, d), F32), jax.ShapeDtypeStruct((SUBLANES, d), F32)],
        scratch_shapes=[pltpu.VMEM((tm, d), F32), pltpu.VMEM((tm, d), F32)],
        compiler_params=_params(("arbitrary", "arbitrary")),
    )(r1, target, w_up, w_down, ln1_g, ln1_b, ln2_g, ln2_b)


def _mlp_bwd(dr2, up, r1, w_up, w_down, ln1_g, tm):
    t, d = r1.shape
    ns, _, sw = w_up.shape
    spb = MLP_SLABS_PER_STEP
    assert ns % spb == 0
    nf, tf, ff = ns // spb, spb * sw, ns * sw

    def body(dr2_ref, up_ref, r1_ref, wu_ref, wd_ref, g1_ref, dup_ref, dr1_ref, dg1_ref, db1_ref, acc):
        i = pl.program_id(0)
        f = pl.program_id(1)

        @pl.when((i == 0) & (f == 0))
        def _():
            dg1_ref[...] = jnp.zeros_like(dg1_ref)
            db1_ref[...] = jnp.zeros_like(db1_ref)

        @pl.when(f == 0)
        def _():
            acc[...] = jnp.zeros_like(acc)

        dact = _dot(dr2_ref[...], wd_ref[...], NT)
        dup_ref[...] = (dact * 2.0 * jnp.maximum(up_ref[...].astype(F32), 0.0)).astype(dup_ref.dtype)
        for s in range(spb):
            acc[...] += _dot(dup_ref[:, s * sw:(s + 1) * sw], wu_ref[s], NT)

        @pl.when(f == nf - 1)
        def _():
            dh1 = acc[...] + ALPHA * dr2_ref[...].astype(F32)
            xhat, rstd = _ln_fwd(r1_ref[...])
            dg1_ref[0:1, :] += _colsum(dh1 * xhat)
            db1_ref[0:1, :] += _colsum(dh1)
            dr1_ref[...] = _ln_bwd(dh1, xhat, rstd, g1_ref[...]).astype(dr1_ref.dtype)

    row = pl.BlockSpec((tm, d), lambda i, f: (i, 0))
    acc8 = pl.BlockSpec((SUBLANES, d), lambda i, f: (0, 0))
    return pl.pallas_call(
        body, name="mlp_bwd",
        grid=(t // tm, nf),
        in_specs=[row, pl.BlockSpec((tm, tf), lambda i, f: (i, f)), row,
                  pl.BlockSpec((spb, d, sw), lambda i, f: (f, 0, 0)), pl.BlockSpec((tf, d), lambda i, f: (f, 0)),
                  pl.BlockSpec((1, d), lambda i, f: (0, 0))],
        out_specs=[pl.BlockSpec((tm, tf), lambda i, f: (i, f)), row, acc8, acc8],
        out_shape=[jax.ShapeDtypeStruct((t, ff), BF16), jax.ShapeDtypeStruct((t, d), BF16),
                   jax.ShapeDtypeStruct((SUBLANES, d), F32), jax.ShapeDtypeStruct((SUBLANES, d), F32)],
        scratch_shapes=[pltpu.VMEM((tm, d), F32)],
        compiler_params=_params(("arbitrary", "arbitrary")),
    )(dr2, up, r1, w_up, w_down, ln1_g)


def _dx_kernel(segs, w_main, ddt, w_dt, dr1, tm, tk, comm=None):
    t, d = dr1.shape
    nblk = [s.shape[1] // tk for s in segs]
    starts = [sum(nblk[:i]) for i in range(len(segs))]
    nk = sum(nblk)
    nseg = len(segs)

    def body(*refs):
        seg_refs = refs[:nseg]
        w_ref, ddt_ref, wdt_ref, dr1_ref, o_ref, acc = refs[nseg:]
        k = pl.program_id(1)

        @pl.when(k == 0)
        def _():
            acc[...] = ALPHA * dr1_ref[...].astype(F32) + _dot(ddt_ref[...], wdt_ref[...], NT)

        for si in range(nseg):
            @pl.when((k >= starts[si]) & (k < starts[si] + nblk[si]))
            def _(si=si):
                acc[...] += _dot(seg_refs[si][...], w_ref[...], NT)

        @pl.when(k == nk - 1)
        def _():
            o_ref[...] = acc[...]

    def seg_spec(si):
        return pl.BlockSpec((tm, tk), lambda i, k: (i, jnp.clip(k - starts[si], 0, nblk[si] - 1)))

    row = pl.BlockSpec((tm, d), lambda i, k: (i, 0))
    grid = (t // tm, nk)
    c_in, c_in_specs, c_out_specs, c_out_shapes = _comm_specs(comm)
    return pl.pallas_call(
        _fuse_comm(body, grid, nseg + 4, 1, comm), name="dx",
        grid=grid,
        in_specs=[seg_spec(si) for si in range(nseg)] + [
            pl.BlockSpec((d, tk), lambda i, k: (0, k)), pl.BlockSpec((tm, LANES), lambda i, k: (i, 0)),
            pl.BlockSpec((d, LANES), lambda i, k: (0, 0)), row] + c_in_specs,
        out_specs=[row] + c_out_specs,
        out_shape=[jax.ShapeDtypeStruct((t, d), F32)] + c_out_shapes,
        scratch_shapes=[pltpu.VMEM((tm, d), F32)] + (list(comm.scratch) if comm else []),
        compiler_params=_params(("arbitrary", "arbitrary")),
    )(*segs, w_main, ddt, w_dt, dr1, *c_in)


def _dims(d):
    inner = 2 * d
    heads = inner // HEAD_DIM
    cd = inner + 2 * GROUPS * STATE
    assert heads <= LANES and inner % (GROUPS * LANES) == 0 and d % (len(POOL_WINDOWS) * LANES) == 0
    o_z, o_xbc, o_dt, o_lg = d, d + inner, d + inner + cd, d + inner + cd + heads
    return inner, heads, cd, (o_z, o_xbc, o_dt, o_lg)


def _row(v, width=None):
    v = v.reshape(1, -1).astype(F32)
    if width is not None and v.shape[1] < width:
        v = jnp.pad(v, ((0, 0), (0, width - v.shape[1])))
    return v


def _local_step(x2, tgt2, w, shards, core, bl):
    t, d = x2.shape
    inner, heads, cd, _ = _dims(d)
    gs = GROUPS * STATE
    nc = t // bl // CHUNK
    w_main, w_dt = _w_in_internal(w["w_in_blocks"], d)
    c_z, c_lg, c_u = cd, cd + inner, cd + inner + 2 * d
    conv_w8 = jnp.pad(w["conv_w"].astype(F32), ((0, SUBLANES - CONV_K), (0, 0)))
    conv_b = _row(w["conv_b"])
    dtb, alog = _row(w["dt_bias"], LANES), _row(w["a_log"], LANES)
    dskip_x = _row(jnp.repeat(w["d_skip"].reshape(-1), HEAD_DIM))
    normw = _row(w["ssd_norm_w"])
    col_head = lax.broadcasted_iota(jnp.int32, (LANES, inner), 1) // HEAD_DIM
    emat = (col_head == lax.broadcasted_iota(jnp.int32, (LANES, inner), 0)).astype(BF16)
    emat_t = emat.T
    w_main, w_dt = w_main.astype(BF16), w_dt.astype(BF16)
    b_gates, pool_scale = _row(w["b_gates"]), _row(w["pool_scale"])
    ln1_g, ln1_b, ln2_g, ln2_b = _row(w["ln1_g"]), _row(w["ln1_b"]), _row(w["ln2_g"]), _row(w["ln2_b"])

    tm = min(512, t)
    tk = min(1024, d)
    ct = min(512, d)
    rt = min(512, t // bl)
    nct = t // bl // rt
    mm = functools.partial(_matmul, bm=1024, bn=tk, bk=1024)
    mmt = functools.partial(_matmul, bm=1024, bn=tk, bk=2048)
    xb = x2.astype(BF16)

    proj, xbc, dsl, *gathered = _in_proj(xb, w_main, conv_w8, conv_b, cd, t // bl, 1024, tk,
                                    _all_gather_comm([shards[n] for n in OTHERS]))
    gathered = dict(zip(OTHERS, gathered))
    w_ssd, w_out, w_down = (gathered[n].reshape(-1, d) for n in ("w_ssd_proj", "w_out", "w_down"))
    w_up = gathered["w_up"]
    npg = len(POOL_WINDOWS)
    cg = d // npg
    wpg = gathered["w_pool_group"].reshape(N_DEV, npg, cg // N_DEV, cg).transpose(1, 0, 2, 3).reshape(npg, cg, cg)
    dt_raw = mm(xb, w_dt, "nn", F32, name="in_proj_dt")
    y, yn, states = _ssd_fwd(xbc, proj, dt_raw, dtb, alog, dskip_x, normw, emat, bl, inner, c_z)
    yssd = mmt(yn, w_ssd, "nn", BF16, name="ssd_proj")
    ypr = _pool_fwd(proj, wpg, bl, d, c_u)
    merged, r1 = _merge_fwd(proj, ypr, yssd, x2, w_out, b_gates, pool_scale, d, c_lg, tm)
    tmm = min(1024, t)
    up, h1, dr2, loss8, dg2, db2 = _mlp_fwd(r1, tgt2, w_up, w_down, ln1_g, ln1_b, ln2_g, ln2_b, tmm)

    dup, dr1, dg1, db1 = _mlp_bwd(dr2, up, r1, w_up, w_down, ln1_g, tmm)
    relu2 = lambda v: jnp.square(jnp.maximum(v, 0.0))
    g = {}
    g["w_down"] = mmt(up, dr2, "tn", BF16, name="dw_down", a_fn=relu2)
    g["w_up"] = mmt(h1, dup, "tn", BF16, name="dw_up", col_blocks=N_DEV)
    g["w_out"] = mmt(merged, dr1, "tn", BF16, name="dw_out")
    dlg, dyp, dys, dbg, dps = _merge_bwd(dr1, proj, ypr, yssd, w_out, b_gates, pool_scale, d, c_lg, tm)
    du, dwpg = _pool_bwd(proj, dyp, wpg, bl, d, c_u)
    g["w_pool_group"] = dwpg.reshape(npg, N_DEV, cg // N_DEV, cg).transpose(1, 0, 2, 3).reshape(
        N_DEV, npg * cg // N_DEV, cg).astype(BF16)
    dyn = mm(dys, w_ssd, "nt", BF16, name="d_ssd_proj")
    g["w_ssd_proj"] = mmt(yn, dys, "tn", BF16, name="dw_ssd_proj")

    def chip_sums(names, tag):
        parts = [g.pop(n).reshape((N_DEV,) + shards_2d[n]) for n in names]
        recv = _run_comm(_rs_sibling_comm(parts), "rs_sibling_" + tag)
        return [_add_pairs(core, p, r, "rs_add_" + n) for n, p, r in zip(names, parts, recv)]

    shards_2d = {n: s.shape for n, s in shards.items()}
    shards_2d["w_in"] = w["w_in_blocks"].shape[1:]
    dxs, dbm, dcm, dz, ddt, dnw, dsk, dalog, ddtb, *recv_others = _ssd_bwd(
        xbc, proj, dt_raw, y, dyn, states, dtb, alog, dskip_x, normw, emat, emat_t, bl, inner, c_z,
        comm=_rs_chips_comm(chip_sums(OTHERS, "a")))
    dxs_p, dcw_x, dcb_x = _conv_bwd(proj, dsl, dxs, conv_w8, nct, 0, inner, ct, rt, "conv_bwd_x")
    dbm_p, dcw_b, dcb_b = _conv_bwd(proj, dsl, dbm, conv_w8, nct, inner, gs, ct, rt, "conv_bwd_b")
    dcm_p, dcw_c, dcb_c = _conv_bwd(proj, dsl, dcm, conv_w8, nct, inner + gs, gs, ct, rt, "conv_bwd_c")
    segs = [dxs_p, dbm_p, dcm_p, dz, dlg, du]
    keys = [k for k, _, _ in _col_segments(d)]
    dws = {k: mmt(xb, s, "tn", BF16, name="dw_in_" + k) for k, s in zip(keys, segs + [ddt])}
    g["w_in"] = _w_in_grad_blocks(dws, d, w["w_in_blocks"].shape[2])
    grad_x, recv_w_in = _dx_kernel(segs, w_main, ddt, w_dt, dr1, tmm, tk,
                                   comm=_rs_chips_comm(chip_sums(["w_in"], "b")))
    recv = dict(zip(OTHERS, recv_others))
    recv["w_in"] = recv_w_in
    g["conv_w"] = jnp.concatenate([dcw_x, dcw_b, dcw_c], axis=1)[:CONV_K]
    g["conv_b"] = jnp.concatenate([dcb_x, dcb_b, dcb_c], axis=1)[0]
    g["b_gates"], g["pool_scale"] = dbg[0], dps[0]
    g["dt_bias"], g["a_log"], g["d_skip"] = ddtb[0, :heads], dalog[0, :heads], dsk[0, :heads]
    g["ssd_norm_w"] = dnw[:, 0, :].reshape(inner)
    g["ln1_g"], g["ln1_b"], g["ln2_g"], g["ln2_b"] = dg1[0], db1[0], dg2[0], db2[0]
    return loss8, grad_x, g, recv


BIG = ("w_in", "w_ssd_proj", "w_pool_group", "w_out", "w_up", "w_down")
OTHERS = BIG[1:]
SMALL = ("b_gates", "conv_b", "dt_bias", "a_log", "d_skip", "ssd_norm_w", "pool_scale", "ln1_g", "ln1_b", "ln2_g",
         "ln2_b")
SMALL_PACK = SMALL + ("conv_w",)
NAMES = ("w_in", "b_gates", "conv_w", "conv_b", "dt_bias", "a_log", "d_skip", "ssd_norm_w", "w_ssd_proj",
         "w_pool_group", "pool_scale", "w_out", "ln1_g", "ln1_b", "w_up", "w_down", "ln2_g", "ln2_b")


def _size(shape):
    n = 1
    for s in shape:
        n *= s
    return n


def _rows128(v):
    v = v.astype(F32).reshape((-1, v.shape[-1]))
    n = v.shape[-1]
    v = jnp.pad(v, ((0, 0), (0, -n % LANES)))
    return v.reshape(-1, LANES)


def _pack_small(vals, extra):
    parts = [_rows128(vals[n]) for n in SMALL_PACK]
    parts.append(jnp.pad(extra.reshape(1, 1).astype(F32), ((0, 0), (0, LANES - 1))))
    buf = jnp.concatenate(parts, axis=0)
    return jnp.pad(buf, ((0, -buf.shape[0] % SUBLANES), (0, 0)))


def _unpack_small(buf, shapes):
    out, off = {}, 0
    for n in SMALL_PACK:
        lead, last = _size(shapes[n][:-1]), shapes[n][-1]
        per = -(-last // LANES)
        out[n] = buf[off:off + lead * per].reshape(lead, per * LANES)[:, :last].reshape(shapes[n])
        off += lead * per
    return out, buf[off, 0]


def _col_segments(d):
    inner, heads, cd, (o_z, o_xbc, o_dt, o_lg) = _dims(d)
    gs = GROUPS * STATE
    return [("xs", o_xbc, inner), ("B", o_xbc + inner, gs), ("C", o_xbc + inner + gs, gs), ("z", o_z, inner),
            ("lg", o_lg, 2 * d), ("u", 0, d), ("dt", o_dt, heads)]


def _cols_from_blocks(blocks, start, width, bw):
    parts, pos = [], start
    while pos < start + width:
        k, off = divmod(pos, bw)
        n = min(bw - off, start + width - pos)
        parts.append(blocks[k][:, off:off + n])
        pos += n
    return parts


def _w_in_internal(blocks, d):
    bw = blocks.shape[2]
    segs = _col_segments(d)
    heads = segs[-1][2]
    main = [p for _, s, w_ in segs[:-1] for p in _cols_from_blocks(blocks, s, w_, bw)]
    w_dt = jnp.concatenate(_cols_from_blocks(blocks, segs[-1][1], heads, bw), axis=1)
    return jnp.concatenate(main, axis=1), jnp.pad(w_dt, ((0, 0), (0, LANES - heads)))


def _w_in_grad_blocks(dws, d, bw):
    order = sorted(_col_segments(d), key=lambda s: s[1])
    blocks = []
    for k in range(N_DEV):
        lo, hi, parts = k * bw, (k + 1) * bw, []
        for key, s, w_ in order:
            a, b = max(lo, s), min(hi, s + w_)
            if a < b:
                parts.append(dws[key][:, a - s:b - s])
        blocks.append(jnp.concatenate(parts, axis=1))
    return jnp.stack(blocks)


def _mesh_pos():
    return lax.axis_index("x"), lax.axis_index("y"), lax.axis_index("c")


def _all_gather_comm(shards):
    nw = len(shards)

    def setup(x_refs, out_refs, scr):
        send_sems, recv_sems, local_sems = scr
        x, y, c = _mesh_pos()
        me, sibling = (x, y, c), (x, y, 1 - c)
        chips = [(1 - x, y), (x, 1 - y), (1 - x, 1 - y)]

        def copy(wi, k, block, to, from_input=False):
            px, py, pc = block
            blk = out_refs[wi].at[4 * px + 2 * py + pc]
            return pltpu.make_async_remote_copy(
                src_ref=x_refs[wi] if from_input else blk, dst_ref=blk,
                send_sem=send_sems.at[7 * wi + k], recv_sem=recv_sems.at[7 * wi + k], device_id=to,
                device_id_type=MESH)

        mine = [pltpu.make_async_copy(x_refs[wi], out_refs[wi].at[4 * x + 2 * y + c], local_sems.at[wi])
                for wi in range(nw)]
        sends = []
        for wi in range(nw):
            sends.append(copy(wi, 0, me, sibling, True))
            sends += [copy(wi, 1 + j, me, (*chip, c), True) for j, chip in enumerate(chips)]
        return copy, mine, sends, me, sibling, chips, c

    def start(x_refs, out_refs, scr):
        _, mine, sends, _, _, _, _ = setup(x_refs, out_refs, scr)
        for cp in mine + sends:
            cp.start()

    def wait(x_refs, out_refs, scr):
        copy, mine, sends, me, sibling, chips, c = setup(x_refs, out_refs, scr)
        passed = []
        for wi in range(nw):
            for j, chip in enumerate(chips):
                copy(wi, 1 + j, (*chip, c), me).wait_recv()
                passed.append(copy(wi, 4 + j, (*chip, c), sibling))
                passed[-1].start()
        for wi in range(nw):
            copy(wi, 0, sibling, me).wait_recv()
            for j, chip in enumerate(chips):
                copy(wi, 4 + j, (*chip, 1 - c), me).wait_recv()
        for cp in sends + passed:
            cp.wait_send()
        for cp in mine:
            cp.wait()

    return _Comm(
        inputs=list(shards),
        out_shapes=[jax.ShapeDtypeStruct((N_DEV,) + s.shape, s.dtype) for s in shards],
        scratch=[pltpu.SemaphoreType.DMA((7 * nw,)), pltpu.SemaphoreType.DMA((7 * nw,)),
                 pltpu.SemaphoreType.DMA((nw,))],
        start=start, wait=wait)


def _rs_sibling_comm(parts):
    nw = len(parts)
    half = N_DEV // 2

    def copies(p_refs, recv_refs, scr):
        send_sems, recv_sems = scr
        x, y, c = _mesh_pos()
        return [pltpu.make_async_remote_copy(
            src_ref=p_refs[wi].at[2 * q + 1 - c], dst_ref=recv_refs[wi].at[q],
            send_sem=send_sems.at[half * wi + q], recv_sem=recv_sems.at[half * wi + q],
            device_id=(x, y, 1 - c), device_id_type=MESH) for wi in range(nw) for q in range(half)]

    def start(p_refs, recv_refs, scr):
        for cp in copies(p_refs, recv_refs, scr):
            cp.start()

    def wait(p_refs, recv_refs, scr):
        for cp in copies(p_refs, recv_refs, scr):
            cp.wait()

    return _Comm(
        inputs=list(parts),
        out_shapes=[jax.ShapeDtypeStruct((half,) + p.shape[1:], p.dtype) for p in parts],
        scratch=[pltpu.SemaphoreType.DMA((half * nw,)), pltpu.SemaphoreType.DMA((half * nw,))],
        start=start, wait=wait)


def _rs_chips_comm(tbs):
    nw = len(tbs)

    def copies(t_refs, o_refs, scr):
        send_sems, recv_sems, local_sems = scr
        x, y, c = _mesh_pos()
        p = 2 * x + y
        chips = [(1 - x, y), (x, 1 - y), (1 - x, 1 - y)]
        own = [pltpu.make_async_copy(t_refs[wi].at[p], o_refs[wi].at[p], local_sems.at[wi]) for wi in range(nw)]
        remote = [pltpu.make_async_remote_copy(
            src_ref=t_refs[wi].at[2 * qx + qy], dst_ref=o_refs[wi].at[p], send_sem=send_sems.at[3 * wi + j],
            recv_sem=recv_sems.at[3 * wi + j], device_id=(qx, qy, c), device_id_type=MESH)
            for wi in range(nw) for j, (qx, qy) in enumerate(chips)]
        arriving = [pltpu.make_async_remote_copy(
            src_ref=t_refs[wi].at[p], dst_ref=o_refs[wi].at[2 * qx + qy], send_sem=send_sems.at[3 * wi + j],
            recv_sem=recv_sems.at[3 * wi + j], device_id=(qx, qy, c), device_id_type=MESH)
            for wi in range(nw) for j, (qx, qy) in enumerate(chips)]
        return own, remote, arriving

    def start(t_refs, o_refs, scr):
        own, remote, _ = copies(t_refs, o_refs, scr)
        for cp in own + remote:
            cp.start()

    def wait(t_refs, o_refs, scr):
        own, remote, arriving = copies(t_refs, o_refs, scr)
        for cp in arriving:
            cp.wait_recv()
        for cp in remote:
            cp.wait_send()
        for cp in own:
            cp.wait()

    return _Comm(
        inputs=list(tbs),
        out_shapes=[jax.ShapeDtypeStruct(t_.shape, t_.dtype) for t_ in tbs],
        scratch=[pltpu.SemaphoreType.DMA((3 * nw,)), pltpu.SemaphoreType.DMA((3 * nw,)),
                 pltpu.SemaphoreType.DMA((nw,))],
        start=start, wait=wait)


def _row_tile(rows, cap=256):
    if rows <= cap:
        return rows
    return max(t_ for t_ in range(SUBLANES, cap + 1, SUBLANES) if rows % t_ == 0)


def _add_pairs(core, part, recv, name):
    n, r, c_ = recv.shape
    tr = _row_tile(r)

    def body(core_ref, a_ref, b_ref, o_ref):
        o_ref[...] = (a_ref[...].astype(F32) + b_ref[...].astype(F32)).astype(o_ref.dtype)

    spec = pl.BlockSpec((1, tr, c_), lambda q, i, core_ref: (q, i, 0))
    return pl.pallas_call(
        body, name=name,
        grid_spec=pltpu.PrefetchScalarGridSpec(
            num_scalar_prefetch=1, grid=(n, r // tr),
            in_specs=[pl.BlockSpec((1, tr, c_), lambda q, i, core_ref: (2 * q + core_ref[0], i, 0)), spec],
            out_specs=spec),
        out_shape=jax.ShapeDtypeStruct(recv.shape, BF16), compiler_params=_params(("parallel", "parallel")),
    )(core, part, recv)


def _small_allreduce(vec, name):
    rows = vec.shape[0]

    def body(x_ref, o_ref, buf, send_sems, recv_sems):
        x, y, c = _mesh_pos()
        me = 4 * x + 2 * y + c
        buf[me] = x_ref[...]
        cps = []
        for k in range(1, N_DEV):
            peer = (1 - x if k & 4 else x, 1 - y if k & 2 else y, 1 - c if k & 1 else c)
            cps.append(pltpu.make_async_remote_copy(
                src_ref=x_ref, dst_ref=buf.at[me], send_sem=send_sems.at[k - 1], recv_sem=recv_sems.at[k - 1],
                device_id=peer, device_id_type=MESH))
        for cp in cps:
            cp.start()
        for k in range(1, N_DEV):
            px, py, pc = (1 - x if k & 4 else x, 1 - y if k & 2 else y, 1 - c if k & 1 else c)
            pltpu.make_async_remote_copy(
                src_ref=x_ref, dst_ref=buf.at[4 * px + 2 * py + pc], send_sem=send_sems.at[k - 1],
                recv_sem=recv_sems.at[k - 1], device_id=(px, py, pc), device_id_type=MESH).wait_recv()
        for cp in cps:
            cp.wait_send()
        acc = buf[0]
        for k in range(1, N_DEV):
            acc = acc + buf[k]
        o_ref[...] = acc

    vm = pl.BlockSpec(memory_space=pltpu.VMEM)
    return pl.pallas_call(
        body, name=name,
        in_specs=[vm], out_specs=vm,
        out_shape=jax.ShapeDtypeStruct(vec.shape, F32),
        scratch_shapes=[pltpu.VMEM((N_DEV, rows, LANES), F32), pltpu.SemaphoreType.DMA((N_DEV - 1,)),
                        pltpu.SemaphoreType.DMA((N_DEV - 1,))],
    )(vec)


def _adamw(gparts, w, m, v, name):
    n, r, c_ = gparts.shape
    tr = _row_tile(r)
    c1 = 1.0 / (1.0 - B1 ** STEP)
    c2 = 1.0 / (1.0 - B2 ** STEP)

    def body(g_ref, w_ref, m_ref, v_ref, go_ref, d_ref, mo_ref, vo_ref):
        g = g_ref[0].astype(F32)
        for q in range(1, n):
            g = g + g_ref[q].astype(F32)
        mn = B1 * m_ref[...] + (1.0 - B1) * g
        vn = B2 * v_ref[...] + (1.0 - B2) * (g * g)
        go_ref[...] = g
        mo_ref[...] = mn
        vo_ref[...] = vn
        d_ref[...] = -LR * ((mn * c1) / (jnp.sqrt(vn * c2) + ADAM_EPS) + WD * w_ref[...])

    spec = pl.BlockSpec((tr, c_), lambda i: (i, 0))
    out = jax.ShapeDtypeStruct((r, c_), F32)
    return pl.pallas_call(
        body, name=name, grid=(r // tr,),
        in_specs=[pl.BlockSpec((n, tr, c_), lambda i: (0, i, 0)), spec, spec, spec],
        out_specs=[spec] * 4, out_shape=[out] * 4, compiler_params=_params(("parallel",)),
    )(gparts, w, m, v)


def kernel(x, w_in, b_gates, conv_w, conv_b, dt_bias, a_log, d_skip, ssd_norm_w, w_ssd_proj, w_pool_group, pool_scale, w_out, ln1_g, ln1_b, w_up, w_down, ln2_g, ln2_b, loss_target, m_w_in, m_b_gates, m_conv_w, m_conv_b, m_dt_bias, m_a_log, m_d_skip, m_ssd_norm_w, m_w_ssd_proj, m_w_pool_group, m_pool_scale, m_w_out, m_ln1_g, m_ln1_b, m_w_up, m_w_down, m_ln2_g, m_ln2_b, v_w_in, v_b_gates, v_conv_w, v_conv_b, v_dt_bias, v_a_log, v_d_skip, v_ssd_norm_w, v_w_ssd_proj, v_w_pool_group, v_pool_scale, v_w_out, v_ln1_g, v_ln1_b, v_w_up, v_w_down, v_ln2_g, v_ln2_b):
    ws = (w_in, b_gates, conv_w, conv_b, dt_bias, a_log, d_skip, ssd_norm_w, w_ssd_proj, w_pool_group, pool_scale,
          w_out, ln1_g, ln1_b, w_up, w_down, ln2_g, ln2_b)
    ms = (m_w_in, m_b_gates, m_conv_w, m_conv_b, m_dt_bias, m_a_log, m_d_skip, m_ssd_norm_w, m_w_ssd_proj,
          m_w_pool_group, m_pool_scale, m_w_out, m_ln1_g, m_ln1_b, m_w_up, m_w_down, m_ln2_g, m_ln2_b)
    vs = (v_w_in, v_b_gates, v_conv_w, v_conv_b, v_dt_bias, v_a_log, v_d_skip, v_ssd_norm_w, v_w_ssd_proj,
          v_w_pool_group, v_pool_scale, v_w_out, v_ln1_g, v_ln1_b, v_w_up, v_w_down, v_ln2_g, v_ln2_b)
    w = {n: a[0] for n, a in zip(NAMES, ws)}
    m = {n: a[0] for n, a in zip(NAMES, ms)}
    v = {n: a[0] for n, a in zip(NAMES, vs)}
    out_shapes = {n: a.shape for n, a in zip(NAMES, ws)}
    bl, s, d = x.shape
    x2, tgt2 = x.reshape(bl * s, d), loss_target.reshape(bl * s, d)
    xi, yi, ci = _mesh_pos()
    me = 4 * xi + 2 * yi + ci
    zero = jnp.zeros((), F32)
    shapes = {n: w[n].shape for n in NAMES}
    shape2d = {n: (_size(shapes[n][:-1]), shapes[n][-1]) for n in BIG}
    cwl = shapes["conv_w"][1]

    shards = {n: w[n].astype(BF16).reshape(shape2d[n]) for n in BIG}
    full = {n: w[n] for n in SMALL}
    full["w_in_blocks"], conv_blocks = _run_comm(_all_gather_comm([shards.pop("w_in"), w["conv_w"]]),
                                                 "all_gather_w_in")
    full["conv_w"] = conv_blocks.transpose(1, 0, 2).reshape(CONV_K, N_DEV * cwl)
    loss8, grad_x, g, recv = _local_step(x2, tgt2, full, shards, ci.astype(jnp.int32).reshape(1), bl)

    small_sum = _small_allreduce(_pack_small(g, loss8[0, 0]), "small_allreduce")
    ex_shapes = {n: shapes[n] for n in SMALL}
    ex_shapes["conv_w"] = (CONV_K, N_DEV * cwl)
    gsum, loss = _unpack_small(small_sum, ex_shapes)
    gsum["conv_w"] = lax.dynamic_slice(gsum["conv_w"], (0, me * cwl), (CONV_K, cwl))
    gs_pk = _pack_small(gsum, zero)
    ws_pk, ms_pk, vs_pk = (_pack_small(t_, zero) for t_ in (w, m, v))
    small_out = _adamw(gs_pk[None], ws_pk, ms_pk, vs_pk, "adamw_small")
    loc_shapes = {n: shapes[n] for n in SMALL_PACK}
    res = [_unpack_small(o, loc_shapes)[0] for o in small_out]

    for n in BIG:
        outs = _adamw(recv[n], *(t_[n].reshape(shape2d[n]) for t_ in (w, m, v)), "adamw_" + n)
        for r_, o in zip(res, outs):
            r_[n] = o

    def ordered(r_):
        return [r_[n].reshape(out_shapes[n]) for n in NAMES]

    return (loss, grad_x.reshape(bl, s, d), *ordered(res[0]), *ordered(res[1]), *ordered(res[2]), *ordered(res[3]))
```

```python
import collections
import functools

import jax
import jax.numpy as jnp
from jax import lax
from jax.experimental import pallas as pl
from jax.experimental.pallas import tpu as pltpu

F32 = jnp.float32
BF16 = jnp.bfloat16
MESH = pl.DeviceIdType.MESH

HEAD_DIM = 64
STATE = 128
GROUPS = 8
CONV_K = 4
CHUNK = 256
POOL_WINDOWS = (2, 4, 8, 16)
ALPHA = 2.0 ** 0.25
LN_EPS = 1e-5
RMS_EPS = 1e-5
LR, B1, B2, ADAM_EPS, WD, STEP = 0.001, 0.9, 0.999, 1e-08, 0.01, 10
N_DEV = 8
LANES = 128
SUBLANES = 8
VMEM_LIMIT = 56 * 1024 * 1024
NEG_BIG = -1e30

NN = (((1,), (0,)), ((), ()))
NT = (((1,), (1,)), ((), ()))
TN = (((0,), (0,)), ((), ()))


def _dot(a, b, dims=NN):
    return lax.dot_general(a.astype(BF16), b.astype(BF16), dims, preferred_element_type=F32)


def _dot_exact01(q, e, dims=NN):
    hi = q.astype(BF16)
    r1 = q - hi.astype(F32)
    mid = r1.astype(BF16)
    lo = (r1 - mid.astype(F32)).astype(BF16)
    f = lambda p: lax.dot_general(p, e, dims, preferred_element_type=F32)
    return f(hi) + f(mid) + f(lo)


def _params(sem):
    return pltpu.CompilerParams(dimension_semantics=sem, vmem_limit_bytes=VMEM_LIMIT)


def _sigmoid(x):
    return 1.0 / (1.0 + jnp.exp(-x))


def _colsum(x):
    return jnp.sum(x, axis=0, keepdims=True)


def _ln_fwd(r):
    mu = jnp.mean(r, axis=-1, keepdims=True)
    xc = r - mu
    var = jnp.mean(xc * xc, axis=-1, keepdims=True)
    rstd = lax.rsqrt(var + LN_EPS)
    return xc * rstd, rstd


def _ln_bwd(dy, xhat, rstd, g):
    dxh = dy * g
    m1 = jnp.mean(dxh, axis=-1, keepdims=True)
    m2 = jnp.mean(dxh * xhat, axis=-1, keepdims=True)
    return rstd * (dxh - m1 - xhat * m2)


_Comm = collections.namedtuple("_Comm", "inputs out_shapes scratch start wait")
ANY = pl.BlockSpec(memory_space=pl.ANY)


def _fuse_comm(body, grid, n_in, n_out, comm):
    if comm is None:
        return body
    ci, co = len(comm.inputs), len(comm.out_shapes)

    def fused(*refs):
        ins, cins = refs[:n_in], refs[n_in:n_in + ci]
        o0 = n_in + ci
        outs, couts = refs[o0:o0 + n_out], refs[o0 + n_out:o0 + n_out + co]
        rest = refs[o0 + n_out + co:]
        scr, cscr = rest[:len(rest) - len(comm.scratch)], rest[len(rest) - len(comm.scratch):]
        ids = [pl.program_id(a) for a in range(len(grid))]
        first, last = ids[0] == 0, ids[0] == grid[0] - 1
        for a in range(1, len(grid)):
            first, last = first & (ids[a] == 0), last & (ids[a] == grid[a] - 1)

        @pl.when(first)
        def _():
            comm.start(cins, couts, cscr)

        body(*ins, *outs, *scr)

        @pl.when(last)
        def _():
            comm.wait(cins, couts, cscr)

    return fused


def _comm_specs(comm):
    if comm is None:
        return [], [], [], []
    return list(comm.inputs), [ANY] * len(comm.inputs), [ANY] * len(comm.out_shapes), list(comm.out_shapes)


def _run_comm(comm, name):
    ci, co = len(comm.inputs), len(comm.out_shapes)

    def body(*refs):
        comm.start(refs[:ci], refs[ci:ci + co], refs[ci + co:])
        comm.wait(refs[:ci], refs[ci:ci + co], refs[ci + co:])

    return pl.pallas_call(body, name=name, in_specs=[ANY] * ci, out_specs=[ANY] * co, out_shape=list(comm.out_shapes),
                          scratch_shapes=list(comm.scratch))(*comm.inputs)


def _cast_bf16(x2, comm):
    t, d = x2.shape
    tm = min(1024, t)

    def body(x_ref, o_ref):
        o_ref[...] = x_ref[...].astype(o_ref.dtype)

    grid = (t // tm,)
    spec = pl.BlockSpec((tm, d), lambda i: (i, 0))
    c_in, c_in_specs, c_out_specs, c_out_shapes = _comm_specs(comm)
    return pl.pallas_call(
        _fuse_comm(body, grid, 1, 1, comm), name="cast_x", grid=grid,
        in_specs=[spec] + c_in_specs, out_specs=[spec] + c_out_specs,
        out_shape=[jax.ShapeDtypeStruct((t, d), BF16)] + c_out_shapes,
        scratch_shapes=list(comm.scratch), compiler_params=_params(("arbitrary",)),
    )(x2, *c_in)


def _matmul(a, b, mode, out_dtype, bm, bn, bk, name, a_fn=None, col_blocks=0, comm=None):
    if mode == "nn":
        (m, k), n, dims = a.shape, b.shape[1], NN
    elif mode == "nt":
        (m, k), n, dims = a.shape, b.shape[0], NT
    else:
        (k, m), n, dims = a.shape, b.shape[1], TN
    bm, bn, bk = min(bm, m), min(bn, n), min(bk, k)
    assert m % bm == 0 and n % bn == 0 and k % bk == 0, (name, m, n, k, bm, bn, bk)
    nk = k // bk
    if mode == "nn":
        a_spec = pl.BlockSpec((bm, bk), lambda i, j, kk: (i, kk))
        b_spec = pl.BlockSpec((bk, bn), lambda i, j, kk: (kk, j))
    elif mode == "nt":
        a_spec = pl.BlockSpec((bm, bk), lambda i, j, kk: (i, kk))
        b_spec = pl.BlockSpec((bn, bk), lambda i, j, kk: (j, kk))
    else:
        a_spec = pl.BlockSpec((bk, bm), lambda i, j, kk: (kk, i))
        b_spec = pl.BlockSpec((bk, bn), lambda i, j, kk: (kk, j))

    def body(a_ref, b_ref, o_ref, acc_ref):
        kk = pl.program_id(2)
        av = a_ref[...]
        if a_fn is not None:
            av = a_fn(av.astype(F32))
        prod = _dot(av, b_ref[...], dims)

        def emit(total):
            if col_blocks:
                for s in range(bn // slab):
                    o_ref[s] = total[:, s * slab:(s + 1) * slab].astype(o_ref.dtype)
            else:
                o_ref[...] = total.astype(o_ref.dtype)

        if nk == 1:
            emit(prod)
        else:
            @pl.when(kk == 0)
            def _():
                acc_ref[...] = prod

            @pl.when((kk > 0) & (kk < nk - 1))
            def _():
                acc_ref[...] += prod

            @pl.when(kk == nk - 1)
            def _():
                emit(acc_ref[...] + prod)

    if col_blocks:
        slab = n // col_blocks
        assert n % col_blocks == 0 and bn % slab == 0, (name, n, col_blocks, bn)
        out_spec = pl.BlockSpec((bn // slab, bm, slab), lambda i, j, kk: (j, i, 0))
        out_shape = jax.ShapeDtypeStruct((col_blocks, m, slab), out_dtype)
    else:
        out_spec = pl.BlockSpec((bm, bn), lambda i, j, kk: (i, j))
        out_shape = jax.ShapeDtypeStruct((m, n), out_dtype)
    grid = (m // bm, n // bn, nk)
    c_in, c_in_specs, c_out_specs, c_out_shapes = _comm_specs(comm)
    res = pl.pallas_call(
        _fuse_comm(body, grid, 2, 1, comm), name=name,
        grid=grid,
        in_specs=[a_spec, b_spec] + c_in_specs,
        out_specs=[out_spec] + c_out_specs,
        out_shape=[out_shape] + c_out_shapes,
        scratch_shapes=[pltpu.VMEM((bm, bn), F32)] + (list(comm.scratch) if comm else []),
        compiler_params=_params(("arbitrary",) * 3 if comm else ("parallel", "parallel", "arbitrary")),
    )(a, b, *c_in)
    return res if comm else res[0]


CONV_STRIP = 16
CONV_COLS = 512


def _conv_pre(ext_ref, w_ref, b_ref, r0, rows, cols=slice(None)):
    n = rows + SUBLANES
    win = ext_ref[pl.ds(r0, n), cols]
    acc = b_ref[:, cols] + w_ref[CONV_K - 1:CONV_K, cols] * win[SUBLANES:, :]
    for k in range(CONV_K - 1):
        off = SUBLANES - (CONV_K - 1) + k
        acc = acc + w_ref[k:k + 1, cols] * pltpu.roll(win, n - off, 0)[0:rows, :]
    return acc


def _in_proj(xb, w_main, conv_w8, conv_b, cd, seq_len, bm, bn, comm):
    t, d = xb.shape
    pw = w_main.shape[1]
    bm, bn = min(bm, seq_len), min(bn, d)
    assert t % bm == 0 and seq_len % bm == 0 and pw % bn == 0 and cd % bn == 0
    ncj = cd // bn
    tiles_per_seq = seq_len // bm

    def body(x_ref, w_ref, cw_ref, cb_ref, p_ref, xbc_ref, dsl_ref, ext_ref, carry_ref):
        i = pl.program_id(0)
        j = pl.program_id(1)
        pq = _dot(x_ref[...], w_ref[...]).astype(BF16)
        p_ref[...] = pq

        @pl.when(j < ncj)
        def _():
            jc = jnp.minimum(j, ncj - 1)
            ext_ref[0:SUBLANES, :] = jnp.where((i % tiles_per_seq) == 0, 0.0, carry_ref[jc])
            ext_ref[SUBLANES:, :] = pq.astype(F32)
            carry_ref[jc] = ext_ref[bm:bm + SUBLANES, :]
            cw = min(bn, CONV_COLS)
            for c0 in range(0, bn, cw):
                cols = slice(c0, c0 + cw)
                for r0 in range(0, bm, CONV_STRIP):
                    rows = slice(r0, r0 + CONV_STRIP)
                    acc = _conv_pre(ext_ref, cw_ref, cb_ref, r0, CONV_STRIP, cols)
                    sg = _sigmoid(acc)
                    xbc_ref[rows, cols] = (acc * sg).astype(xbc_ref.dtype)
                    dsl_ref[rows, cols] = (sg * (1.0 + acc * (1.0 - sg))).astype(dsl_ref.dtype)

    grid = (t // bm, pw // bn)
    conv_col = lambda i, j: (0, jnp.minimum(j, ncj - 1))
    c_in, c_in_specs, c_out_specs, c_out_shapes = _comm_specs(comm)
    conv_tile = pl.BlockSpec((bm, bn), lambda i, j: (i, jnp.minimum(j, ncj - 1)))
    conv_out = jax.ShapeDtypeStruct((t, cd), BF16)
    return pl.pallas_call(
        _fuse_comm(body, grid, 4, 3, comm), name="in_proj",
        grid=grid,
        in_specs=[pl.BlockSpec((bm, d), lambda i, j: (i, 0)), pl.BlockSpec((d, bn), lambda i, j: (0, j)),
                  pl.BlockSpec((SUBLANES, bn), conv_col), pl.BlockSpec((1, bn), conv_col)] + c_in_specs,
        out_specs=[pl.BlockSpec((bm, bn), lambda i, j: (i, j)), conv_tile, conv_tile] + c_out_specs,
        out_shape=[jax.ShapeDtypeStruct((t, pw), BF16), conv_out, conv_out] + c_out_shapes,
        scratch_shapes=[pltpu.VMEM((bm + SUBLANES, bn), F32), pltpu.VMEM((ncj, SUBLANES, bn), F32)]
        + (list(comm.scratch) if comm else []),
        compiler_params=_params(("arbitrary", "arbitrary")),
    )(xb, w_main, conv_w8, conv_b, *c_in)


def _conv_bwd(proj, dsilu, dxbc, conv_w8, n_seq_chunks, col0, width, ct, L, name):
    t = proj.shape[0]
    nbc = t // L
    hb = L // SUBLANES
    ct = min(ct, width)
    assert col0 % ct == 0 and width % ct == 0
    cb0 = col0 // ct
    last_hb = t // SUBLANES - 1

    def body(x_ref, xb_ref, s_ref, sa_ref, d_ref, da_ref, w_ref, o_ref, dw_ref, db_ref, ext_ref, dc_ref):
        bc = pl.program_id(1)
        first = (bc % n_seq_chunks) == 0
        last = (bc % n_seq_chunks) == n_seq_chunks - 1

        @pl.when(bc == 0)
        def _():
            dw_ref[...] = jnp.zeros_like(dw_ref)
            db_ref[...] = jnp.zeros_like(db_ref)

        ext_ref[0:SUBLANES, :] = jnp.where(first, 0.0, xb_ref[...].astype(F32))
        ext_ref[SUBLANES:, :] = x_ref[...].astype(F32)
        for r0 in range(0, L, CONV_STRIP):
            rows = slice(r0, r0 + CONV_STRIP)
            dc_ref[rows, :] = d_ref[rows, :].astype(F32) * s_ref[rows, :].astype(F32)
        dc_ref[L:, :] = jnp.where(last, 0.0, da_ref[...].astype(F32)) * sa_ref[...].astype(F32)
        fold = lambda v: v[0:SUBLANES] + v[SUBLANES:CONV_STRIP]
        dws = [jnp.zeros((SUBLANES, ct), F32) for _ in range(CONV_K)]
        dbs = jnp.zeros((SUBLANES, ct), F32)
        n = CONV_STRIP + SUBLANES
        for r0 in range(0, L, CONV_STRIP):
            dcw = dc_ref[pl.ds(r0, n), :]
            xq = ext_ref[r0 + SUBLANES:r0 + SUBLANES + CONV_STRIP, :]
            dx = jnp.zeros((CONV_STRIP, ct), F32)
            for k in range(CONV_K):
                off = CONV_K - 1 - k
                dck = dcw[0:CONV_STRIP, :] if off == 0 else pltpu.roll(dcw, n - off, 0)[0:CONV_STRIP, :]
                dx = dx + w_ref[k:k + 1, :] * dck
                dws[k] = dws[k] + fold(dck * xq)
            dbs = dbs + fold(dcw[0:CONV_STRIP, :])
            o_ref[r0:r0 + CONV_STRIP, :] = dx.astype(o_ref.dtype)
        for k in range(CONV_K):
            dw_ref[k:k + 1, :] += _colsum(dws[k])
        db_ref[0:1, :] += _colsum(dbs)

    return pl.pallas_call(
        body, name=name,
        grid=(width // ct, nbc),
        in_specs=[
            pl.BlockSpec((L, ct), lambda j, i: (i, cb0 + j)),
            pl.BlockSpec((SUBLANES, ct), lambda j, i: (jnp.maximum(i * hb - 1, 0), cb0 + j)),
            pl.BlockSpec((L, ct), lambda j, i: (i, cb0 + j)),
            pl.BlockSpec((SUBLANES, ct), lambda j, i: (jnp.minimum((i + 1) * hb, last_hb), cb0 + j)),
            pl.BlockSpec((L, ct), lambda j, i: (i, j)),
            pl.BlockSpec((SUBLANES, ct), lambda j, i: (jnp.minimum((i + 1) * hb, last_hb), j)),
            pl.BlockSpec((SUBLANES, ct), lambda j, i: (0, cb0 + j)),
        ],
        out_specs=[
            pl.BlockSpec((L, ct), lambda j, i: (i, j)),
            pl.BlockSpec((SUBLANES, ct), lambda j, i: (0, j)),
            pl.BlockSpec((SUBLANES, ct), lambda j, i: (0, j)),
        ],
        out_shape=[
            jax.ShapeDtypeStruct((t, width), BF16),
            jax.ShapeDtypeStruct((SUBLANES, width), F32),
            jax.ShapeDtypeStruct((SUBLANES, width), F32),
        ],
        scratch_shapes=[pltpu.VMEM((L + SUBLANES, ct), F32), pltpu.VMEM((L + SUBLANES, ct), F32)],
        compiler_params=_params(("parallel", "arbitrary")),
    )(proj, proj, dsilu, dsilu, dxbc, dxbc, conv_w8)


def _cumsum_rows(x, reverse=False):
    n = x.shape[0]
    row = lax.broadcasted_iota(jnp.int32, x.shape, 0)
    s = 1
    while s < n:
        if reverse:
            x = x + jnp.where(row < n - s, pltpu.roll(x, n - s, 0), 0.0)
        else:
            x = x + jnp.where(row >= s, pltpu.roll(x, s, 0), 0.0)
        s *= 2
    return x


def _ssd_scalars(dtr, dtb, alog):
    pre = dtr + dtb
    dt = jnp.maximum(pre, 0.0) + jnp.log(1.0 + jnp.exp(-jnp.abs(pre)))
    a = -jnp.exp(alog)
    acs = _cumsum_rows(dt * a) * LOG2E
    n = acs.shape[0]
    return pre, dt, a, acs, jnp.exp2(acs), jnp.exp2(acs[n - 1:n, :] - acs)


LOG2E = 1.4426950408889634


def _dot_2piece(q, e):
    hi = q.astype(BF16)
    mid = (q - hi.astype(F32)).astype(BF16)
    return lax.dot_general(jnp.concatenate([hi, mid], axis=1), jnp.concatenate([e, e], axis=0), NN,
                           preferred_element_type=F32)


def _ssd_group_common(dt_s, e_s, dec_s, e):
    return _dot_2piece(dt_s, e), _dot_2piece(e_s, e), _dot_2piece(dec_s, e)


def _decay_matrix(acs, acs_t, h, tri):
    return jnp.exp2(jnp.where(tri, acs[:, h:h + 1] - acs_t[h:h + 1, :], NEG_BIG))


def _head_mask(r, gw, dtype):
    lane = lax.broadcasted_iota(jnp.int32, (1, gw), 1)
    return ((lane >= r * HEAD_DIM) & (lane < (r + 1) * HEAD_DIM)).astype(dtype)


def _ssd_fwd(xbc, proj, dt_raw, dtb, alog, dskip_x, normw, emat, bl, inner, z_col0):
    t = xbc.shape[0]
    L = CHUNK
    nc = t // bl // L
    G = GROUPS
    gw = inner // G
    hpg = gw // HEAD_DIM
    assert z_col0 % gw == 0
    zb0 = z_col0 // gw
    bb0 = inner // STATE
    cb0 = bb0 + G

    P = G
    assert bb0 % P == 0 and cb0 % P == 0 and zb0 % P == 0

    def body(xs_ref, b_ref, c_ref, z_ref, dtr_ref, dtb_ref, alog_ref, dsk_ref, nw_ref, e_ref,
             y_ref, yn_ref, st_ref, h_ref):
        c = pl.program_id(1)
        _, dt_s, _, acs, e_s, dec_s = _ssd_scalars(dtr_ref[...], dtb_ref[...], alog_ref[...])
        acs_t = acs.T
        tri = lax.broadcasted_iota(jnp.int32, (L, L), 0) >= lax.broadcasted_iota(jnp.int32, (L, L), 1)
        lane = lax.broadcasted_iota(jnp.int32, (L, gw), 1)
        for g in range(G):
            cols = slice(g * gw, (g + 1) * gw)
            ncol = slice(g * STATE, (g + 1) * STATE)

            @pl.when(c == 0)
            def _():
                h_ref[g] = jnp.zeros((STATE, gw), F32)

            xs = xs_ref[:, cols].astype(F32)
            bg = b_ref[:, ncol]
            cg = c_ref[:, ncol]
            dt_x, e_x, dec_x = _ssd_group_common(dt_s, e_s, dec_s, e_ref[:, cols])
            xdt = xs * dt_x
            cb = _dot(cg, bg, NT)
            h = h_ref[g]
            st_ref[0, g] = h
            y = _dot(cg, h) * e_x + dsk_ref[:, cols] * xs
            for r in range(hpg):
                m = cb * _decay_matrix(acs, acs_t, g * hpg + r, tri)
                xr = jnp.where((lane >= r * HEAD_DIM) & (lane < (r + 1) * HEAD_DIM), xdt, 0.0)
                y = y + _dot(m, xr)
            h_ref[g] = h * e_x[L - 1:L, :] + _dot(bg, xdt * dec_x, TN)
            yq = y.astype(y_ref.dtype)
            y_ref[:, cols] = yq
            z = z_ref[:, cols].astype(F32)
            yg = yq.astype(F32) * (z * _sigmoid(z))
            rs = lax.rsqrt(jnp.mean(yg * yg, axis=-1, keepdims=True) + RMS_EPS)
            yn_ref[:, cols] = (yg * rs * nw_ref[:, cols]).astype(yn_ref.dtype)

    return pl.pallas_call(
        body, name="ssd_fwd",
        grid=(bl, nc, G // P),
        in_specs=[
            pl.BlockSpec((L, P * gw), lambda b, c, g: (b * nc + c, g)),
            pl.BlockSpec((L, P * STATE), lambda b, c, g: (b * nc + c, bb0 // P + g)),
            pl.BlockSpec((L, P * STATE), lambda b, c, g: (b * nc + c, cb0 // P + g)),
            pl.BlockSpec((L, P * gw), lambda b, c, g: (b * nc + c, zb0 // P + g)),
            pl.BlockSpec((L, LANES), lambda b, c, g: (b * nc + c, 0)),
            pl.BlockSpec((1, LANES), lambda b, c, g: (0, 0)),
            pl.BlockSpec((1, LANES), lambda b, c, g: (0, 0)),
            pl.BlockSpec((1, P * gw), lambda b, c, g: (0, g)),
            pl.BlockSpec((1, P * gw), lambda b, c, g: (0, g)),
            pl.BlockSpec((LANES, P * gw), lambda b, c, g: (0, g)),
        ],
        out_specs=[
            pl.BlockSpec((L, P * gw), lambda b, c, g: (b * nc + c, g)),
            pl.BlockSpec((L, P * gw), lambda b, c, g: (b * nc + c, g)),
            pl.BlockSpec((1, P, STATE, gw), lambda b, c, g: (b * nc + c, g, 0, 0)),
        ],
        out_shape=[
            jax.ShapeDtypeStruct((t, inner), BF16),
            jax.ShapeDtypeStruct((t, inner), BF16),
            jax.ShapeDtypeStruct((bl * nc, G, STATE, gw), F32),
        ],
        scratch_shapes=[pltpu.VMEM((G, STATE, gw), F32)],
        compiler_params=_params(("arbitrary", "arbitrary", "arbitrary")),
    )(xbc, xbc, xbc, proj, dt_raw, dtb, alog, dskip_x, normw, emat)


def _ssd_bwd(xbc, proj, dt_raw, y, dyn, states, dtb, alog, dskip_x, normw, emat, emat_t, bl, inner, z_col0,
             comm=None):
    t = xbc.shape[0]
    L = CHUNK
    nc = t // bl // L
    G = GROUPS
    gw = inner // G
    hpg = gw // HEAD_DIM
    zb0 = z_col0 // gw
    bb0 = inner // STATE
    cb0 = bb0 + G
    P = G

    def rc(j):
        return nc - 1 - j

    def body(xs_ref, b_ref, c_ref, z_ref, dtr_ref, y_ref, dyn_ref, st_ref, dtb_ref, alog_ref, dsk_ref,
             nw_ref, e_ref, et_ref,
             dxs_ref, db_ref, dc_ref, dz_ref, ddt_ref, dnw_ref, dsk_acc, dalog_acc, ddtb_acc,
             dh_ref):
        b = pl.program_id(0)
        j = pl.program_id(1)

        @pl.when((b == 0) & (j == 0))
        def _():
            dsk_acc[...] = jnp.zeros_like(dsk_acc)
            dalog_acc[...] = jnp.zeros_like(dalog_acc)
            ddtb_acc[...] = jnp.zeros_like(ddtb_acc)

        pre, dt_s, a_row, acs, e_s, dec_s = _ssd_scalars(dtr_ref[...], dtb_ref[...], alog_ref[...])
        acs_t = acs.T
        wacs = jnp.zeros((L, LANES), F32)
        wdt = jnp.zeros((L, LANES), F32)
        tri = lax.broadcasted_iota(jnp.int32, (L, L), 0) >= lax.broadcasted_iota(jnp.int32, (L, L), 1)
        rowi = lax.broadcasted_iota(jnp.int32, (L, gw), 0)
        for g in range(G):
            cols = slice(g * gw, (g + 1) * gw)
            ncol = slice(g * STATE, (g + 1) * STATE)

            @pl.when((b == 0) & (j == 0))
            def _():
                dnw_ref[g] = jnp.zeros((SUBLANES, gw), F32)

            @pl.when(j == 0)
            def _():
                dh_ref[g] = jnp.zeros((STATE, gw), F32)

            xs = xs_ref[:, cols].astype(F32)
            bg = b_ref[:, ncol]
            cg = c_ref[:, ncol]
            dt_x, e_x, dec_x = _ssd_group_common(dt_s, e_s, dec_s, e_ref[:, cols])
            xdt = xs * dt_x
            xdt_b = xdt.astype(BF16)
            cb = _dot(cg, bg, NT)
            h = st_ref[0, g]
            hb16 = h.astype(BF16)
            dsk = dsk_ref[:, cols]

            yv = y_ref[:, cols].astype(F32)
            z = z_ref[:, cols].astype(F32)
            sgz = _sigmoid(z)
            sz = z * sgz
            yg = yv * sz
            rs = lax.rsqrt(jnp.mean(yg * yg, axis=-1, keepdims=True) + RMS_EPS)
            yhat = yg * rs
            dyn_v = dyn_ref[:, cols].astype(F32)
            dnw_ref[g] += _colsum(dyn_v * yhat)
            dyh = dyn_v * nw_ref[:, cols]
            dyg = rs * (dyh - yhat * jnp.mean(dyh * yhat, axis=-1, keepdims=True))
            dy = dyg * sz
            dz_ref[:, cols] = (dyg * yv * (sgz * (1.0 + z * (1.0 - sgz)))).astype(dz_ref.dtype)

            dy_b = dy.astype(BF16)
            dcb = jnp.zeros((L, L), F32)
            dxdt_d = jnp.zeros((L, gw), F32)
            ydiag = jnp.zeros((L, gw), F32)
            for r in range(hpg):
                lm = _decay_matrix(acs, acs_t, g * hpg + r, tri)
                m = (cb * lm).astype(BF16)
                hm = _head_mask(r, gw, BF16)
                dyr = dy_b * hm
                xr = xdt_b * hm
                ydiag = ydiag + _dot(m, xr)
                dcb = dcb + _dot(dyr, xdt_b, NT) * lm
                dxdt_d = dxdt_d + _dot(m, dyr, TN)
            dh = dh_ref[g]
            dh16 = dh.astype(BF16)
            xdec_b = (xdt * dec_x).astype(BF16)
            bdh = _dot(bg, dh16)
            dxdt = dxdt_d + dec_x * bdh
            dcb16 = dcb.astype(BF16)
            dye = (dy * e_x).astype(BF16)
            db_ref[:, ncol] = (_dot(dcb16, cg, TN) + _dot(xdec_b, dh16, NT)).astype(db_ref.dtype)
            dc_ref[:, ncol] = (_dot(dcb16, bg) + _dot(dye, hb16, NT)).astype(dc_ref.dtype)
            dprev = _dot(cg, dye, TN)
            cd_row = e_x[L - 1:L, :]
            s_new = _dot(bg, xdec_b, TN)
            last_term = _colsum(dh16.astype(F32) * s_new) + _colsum(dh * h) * cd_row
            yoff = _dot(cg, hb16) * e_x
            wfold = (dy_b.astype(F32) * ydiag + dy * yoff - dxdt_d * xdt_b.astype(F32) - bdh * xdec_b.astype(F32)
                     + jnp.where(rowi == L - 1, last_term, 0.0))
            et = et_ref[cols, :]
            wacs = wacs + _dot_2piece(wfold, et)
            wdt = wdt + _dot_2piece(dxdt * xs, et)
            dsk_acc[...] += _dot_exact01(jnp.broadcast_to(_colsum(dy * xs), (SUBLANES, gw)), et)
            dxs_ref[:, cols] = (dsk * dy + dxdt * dt_x).astype(dxs_ref.dtype)
            dh_ref[g] = dprev + cd_row * dh

        dda = _cumsum_rows(wacs, reverse=True)
        ddt_raw = (wdt + dda * a_row) * _sigmoid(pre)
        ddt_ref[...] = ddt_raw
        dalog_acc[...] += _colsum(dda * dt_s) * a_row
        ddtb_acc[...] += _colsum(ddt_raw)

    def cidx(b, j):
        return b * nc + rc(j)

    accs = lambda shape: pl.BlockSpec(shape, lambda b, j, g: tuple(0 for _ in shape))
    grid = (bl, nc, G // P)
    c_in, c_in_specs, c_out_specs, c_out_shapes = _comm_specs(comm)
    return pl.pallas_call(
        _fuse_comm(body, grid, 14, 9, comm), name="ssd_bwd",
        grid=grid,
        in_specs=[
            pl.BlockSpec((L, P * gw), lambda b, j, g: (cidx(b, j), g)),
            pl.BlockSpec((L, P * STATE), lambda b, j, g: (cidx(b, j), bb0 // P + g)),
            pl.BlockSpec((L, P * STATE), lambda b, j, g: (cidx(b, j), cb0 // P + g)),
            pl.BlockSpec((L, P * gw), lambda b, j, g: (cidx(b, j), zb0 // P + g)),
            pl.BlockSpec((L, LANES), lambda b, j, g: (cidx(b, j), 0)),
            pl.BlockSpec((L, P * gw), lambda b, j, g: (cidx(b, j), g)),
            pl.BlockSpec((L, P * gw), lambda b, j, g: (cidx(b, j), g)),
            pl.BlockSpec((1, P, STATE, gw), lambda b, j, g: (cidx(b, j), g, 0, 0)),
            pl.BlockSpec((1, LANES), lambda b, j, g: (0, 0)),
            pl.BlockSpec((1, LANES), lambda b, j, g: (0, 0)),
            pl.BlockSpec((1, P * gw), lambda b, j, g: (0, g)),
            pl.BlockSpec((1, P * gw), lambda b, j, g: (0, g)),
            pl.BlockSpec((LANES, P * gw), lambda b, j, g: (0, g)),
            pl.BlockSpec((P * gw, LANES), lambda b, j, g: (g, 0)),
        ] + c_in_specs,
        out_specs=[
            pl.BlockSpec((L, P * gw), lambda b, j, g: (cidx(b, j), g)),
            pl.BlockSpec((L, P * STATE), lambda b, j, g: (cidx(b, j), g)),
            pl.BlockSpec((L, P * STATE), lambda b, j, g: (cidx(b, j), g)),
            pl.BlockSpec((L, P * gw), lambda b, j, g: (cidx(b, j), g)),
            pl.BlockSpec((L, LANES), lambda b, j, g: (cidx(b, j), 0)),
            accs((G, SUBLANES, gw)),
            accs((SUBLANES, LANES)),
            accs((SUBLANES, LANES)),
            accs((SUBLANES, LANES)),
        ] + c_out_specs,
        out_shape=[
            jax.ShapeDtypeStruct((t, inner), BF16),
            jax.ShapeDtypeStruct((t, G * STATE), BF16),
            jax.ShapeDtypeStruct((t, G * STATE), BF16),
            jax.ShapeDtypeStruct((t, inner), BF16),
            jax.ShapeDtypeStruct((t, LANES), F32),
            jax.ShapeDtypeStruct((G, SUBLANES, gw), F32),
            jax.ShapeDtypeStruct((SUBLANES, LANES), F32),
            jax.ShapeDtypeStruct((SUBLANES, LANES), F32),
            jax.ShapeDtypeStruct((SUBLANES, LANES), F32),
        ] + c_out_shapes,
        scratch_shapes=[pltpu.VMEM((G, STATE, gw), F32)]
        + (list(comm.scratch) if comm else []),
        compiler_params=_params(("arbitrary", "arbitrary", "arbitrary")),
    )(xbc, xbc, xbc, proj, dt_raw, y, dyn, states, dtb, alog, dskip_x, normw, emat, emat_t, *c_in)


def _pool_window(u, w, anti):
    n = u.shape[0]
    row = lax.broadcasted_iota(jnp.int32, u.shape, 0)
    acc = u
    s = 1
    while s < w:
        if anti:
            acc = acc + jnp.where(row < n - s, pltpu.roll(acc, n - s, 0), 0.0)
        else:
            acc = acc + jnp.where(row >= s, pltpu.roll(acc, s, 0), 0.0)
        s *= 2
    return acc


def _pool_cnt(shape, w):
    row = lax.broadcasted_iota(jnp.int32, shape, 0)
    return jnp.minimum(row + 1, w).astype(F32)


def _pool_fwd(proj, wpg, bl, d, u_col0):
    t = proj.shape[0]
    s = t // bl
    pg = len(POOL_WINDOWS)
    cg = d // pg
    ub0 = u_col0 // d

    def body(u_ref, w_ref, o_ref):
        for gi, w in enumerate(POOL_WINDOWS):
            u = u_ref[:, gi * cg:(gi + 1) * cg].astype(F32)
            pooled = _pool_window(u, w, False) / _pool_cnt(u.shape, w) - u
            o_ref[:, gi * cg:(gi + 1) * cg] = _dot(pooled, w_ref[gi]).astype(o_ref.dtype)

    return pl.pallas_call(
        body, name="pool_fwd",
        grid=(bl,),
        in_specs=[pl.BlockSpec((s, d), lambda b: (b, ub0)), pl.BlockSpec((pg, cg, cg), lambda b: (0, 0, 0))],
        out_specs=pl.BlockSpec((s, d), lambda b: (b, 0)),
        out_shape=jax.ShapeDtypeStruct((t, d), BF16),
        compiler_params=_params(("parallel",)),
    )(proj, wpg)


def _pool_bwd(proj, dyp, wpg, bl, d, u_col0):
    t = proj.shape[0]
    s = t // bl
    pg = len(POOL_WINDOWS)
    cg = d // pg
    ub0 = u_col0 // d

    def body(u_ref, dy_ref, w_ref, du_ref, dw_ref):
        @pl.when(pl.program_id(0) == 0)
        def _():
            dw_ref[...] = jnp.zeros_like(dw_ref)

        for gi, w in enumerate(POOL_WINDOWS):
            u = u_ref[:, gi * cg:(gi + 1) * cg].astype(F32)
            cnt = _pool_cnt(u.shape, w)
            pooled = _pool_window(u, w, False) / cnt - u
            dy = dy_ref[:, gi * cg:(gi + 1) * cg]
            dw_ref[gi] += _dot(pooled, dy, TN)
            dp = _dot(dy, w_ref[gi], NT)
            du_ref[:, gi * cg:(gi + 1) * cg] = (_pool_window(dp / cnt, w, True) - dp).astype(du_ref.dtype)

    return pl.pallas_call(
        body, name="pool_bwd",
        grid=(bl,),
        in_specs=[pl.BlockSpec((s, d), lambda b: (b, ub0)), pl.BlockSpec((s, d), lambda b: (b, 0)),
                  pl.BlockSpec((pg, cg, cg), lambda b: (0, 0, 0))],
        out_specs=[pl.BlockSpec((s, d), lambda b: (b, 0)), pl.BlockSpec((pg, cg, cg), lambda b: (0, 0, 0))],
        out_shape=[jax.ShapeDtypeStruct((t, d), BF16), jax.ShapeDtypeStruct((pg, cg, cg), F32)],
        compiler_params=_params(("arbitrary",)),
    )(proj, dyp, wpg)


def _merge_fwd(proj, ypr, yssd, x, w_out, b_gates, pool_scale, d, lg_col0, tm):
    t = x.shape[0]
    lb0 = lg_col0 // (2 * d)

    def body(lg_ref, yp_ref, ys_ref, x_ref, w_ref, bg_ref, ps_ref, mg_ref, r1_ref):
        lg = lg_ref[...].astype(F32) + bg_ref[...]
        ga = _sigmoid(lg[:, :d])
        gb = _sigmoid(lg[:, d:])
        merged = ga * (yp_ref[...].astype(F32) * ps_ref[...]) + gb * ys_ref[...].astype(F32)
        mg_ref[...] = merged.astype(mg_ref.dtype)
        r1_ref[...] = ALPHA * x_ref[...] + _dot(mg_ref[...], w_ref[...])

    row = lambda w: pl.BlockSpec((tm, w), lambda i: (i, 0))
    full = lambda a: pl.BlockSpec(a.shape, lambda i: (0, 0))
    return pl.pallas_call(
        body, name="merge_fwd",
        grid=(t // tm,),
        in_specs=[pl.BlockSpec((tm, 2 * d), lambda i: (i, lb0)), row(d), row(d), row(d), full(w_out), full(b_gates),
                  full(pool_scale)],
        out_specs=[row(d), row(d)],
        out_shape=[jax.ShapeDtypeStruct((t, d), BF16), jax.ShapeDtypeStruct((t, d), F32)],
        compiler_params=_params(("parallel",)),
    )(proj, ypr, yssd, x, w_out, b_gates, pool_scale)


def _merge_bwd(dr1, proj, ypr, yssd, w_out, b_gates, pool_scale, d, lg_col0, tm):
    t = dr1.shape[0]
    lb0 = lg_col0 // (2 * d)

    def body(dr_ref, lg_ref, yp_ref, ys_ref, w_ref, bg_ref, ps_ref, dlg_ref, dyp_ref, dys_ref, dbg_ref, dps_ref):
        @pl.when(pl.program_id(0) == 0)
        def _():
            dbg_ref[...] = jnp.zeros_like(dbg_ref)
            dps_ref[...] = jnp.zeros_like(dps_ref)

        dm = _dot(dr_ref[...], w_ref[...], NT)
        lg = lg_ref[...].astype(F32) + bg_ref[...]
        ga = _sigmoid(lg[:, :d])
        gb = _sigmoid(lg[:, d:])
        ypr_v = yp_ref[...].astype(F32)
        ys_v = ys_ref[...].astype(F32)
        ps = ps_ref[...]
        dga = dm * ypr_v * ps
        dla = dga * ga * (1.0 - ga)
        dlb = dm * ys_v * gb * (1.0 - gb)
        dlg_ref[:, :d] = dla.astype(dlg_ref.dtype)
        dlg_ref[:, d:] = dlb.astype(dlg_ref.dtype)
        dyp_ref[...] = (dm * ga * ps).astype(dyp_ref.dtype)
        dys_ref[...] = (dm * gb).astype(dys_ref.dtype)
        dbg_ref[0:1, :d] += _colsum(dla)
        dbg_ref[0:1, d:] += _colsum(dlb)
        dps_ref[0:1, :] += _colsum(dm * ga * ypr_v)

    row = lambda w: pl.BlockSpec((tm, w), lambda i: (i, 0))
    full = lambda a: pl.BlockSpec(a.shape, lambda i: (0, 0))
    acc = lambda w: pl.BlockSpec((SUBLANES, w), lambda i: (0, 0))
    return pl.pallas_call(
        body, name="merge_bwd",
        grid=(t // tm,),
        in_specs=[row(d), pl.BlockSpec((tm, 2 * d), lambda i: (i, lb0)), row(d), row(d), full(w_out), full(b_gates),
                  full(pool_scale)],
        out_specs=[row(2 * d), row(d), row(d), acc(2 * d), acc(d)],
        out_shape=[jax.ShapeDtypeStruct((t, 2 * d), BF16), jax.ShapeDtypeStruct((t, d), BF16),
                   jax.ShapeDtypeStruct((t, d), BF16), jax.ShapeDtypeStruct((SUBLANES, 2 * d), F32),
                   jax.ShapeDtypeStruct((SUBLANES, d), F32)],
        compiler_params=_params(("arbitrary",)),
    )(dr1, proj, ypr, yssd, w_out, b_gates, pool_scale)


MLP_SLABS_PER_STEP = 2


def _mlp_fwd(r1, target, w_up, w_down, ln1_g, ln1_b, ln2_g, ln2_b, tm):
    t, d = r1.shape
    ns, _, sw = w_up.shape
    spb = MLP_SLABS_PER_STEP
    assert ns % spb == 0
    nf, tf, ff = ns // spb, spb * sw, ns * sw

    def body(r1_ref, tg_ref, wu_ref, wd_ref, g1_ref, b1_ref, g2_ref, b2_ref,
             up_ref, h1_ref, dr2_ref, loss_ref, dg2_ref, db2_ref, h1f, acc):
        i = pl.program_id(0)
        f = pl.program_id(1)

        @pl.when((i == 0) & (f == 0))
        def _():
            loss_ref[...] = jnp.zeros_like(loss_ref)
            dg2_ref[...] = jnp.zeros_like(dg2_ref)
            db2_ref[...] = jnp.zeros_like(db2_ref)

        @pl.when(f == 0)
        def _():
            xhat, _ = _ln_fwd(r1_ref[...])
            h1 = xhat * g1_ref[...] + b1_ref[...]
            h1f[...] = h1
            h1_ref[...] = h1.astype(h1_ref.dtype)
            acc[...] = jnp.zeros_like(acc)

        for s in range(spb):
            up_ref[:, s * sw:(s + 1) * sw] = _dot(h1_ref[...], wu_ref[s]).astype(up_ref.dtype)
        upq = jnp.maximum(up_ref[...].astype(F32), 0.0)
        acc[...] += _dot(upq * upq, wd_ref[...])

        @pl.when(f == nf - 1)
        def _():
            xhat, rstd = _ln_fwd(ALPHA * h1f[...] + acc[...])
            g2 = g2_ref[...]
            diff = xhat * g2 + b2_ref[...] - tg_ref[...]
            loss_ref[...] += 0.5 / d * jnp.sum(diff * diff)
            dh2 = diff * (1.0 / d)
            dg2_ref[0:1, :] += _colsum(dh2 * xhat)
            db2_ref[0:1, :] += _colsum(dh2)
            dr2_ref[...] = _ln_bwd(dh2, xhat, rstd, g2).astype(dr2_ref.dtype)

    row = pl.BlockSpec((tm, d), lambda i, f: (i, 0))
    vec = pl.BlockSpec((1, d), lambda i, f: (0, 0))
    acc8 = pl.BlockSpec((SUBLANES, d), lambda i, f: (0, 0))
    return pl.pallas_call(
        body, name="mlp_fwd",
        grid=(t // tm, nf),
        in_specs=[row, row, pl.BlockSpec((spb, d, sw), lambda i, f: (f, 0, 0)), pl.BlockSpec((tf, d), lambda i, f: (f, 0)),
                  vec, vec, vec, vec],
        out_specs=[pl.BlockSpec((tm, tf), lambda i, f: (i, f)), row, row,
                   pl.BlockSpec((SUBLANES, LANES), lambda i, f: (0, 0)), acc8, acc8],
        out_shape=[jax.ShapeDtypeStruct((t, ff), BF16), jax.ShapeDtypeStruct((t, d), BF16),
                   jax.ShapeDtypeStruct((t, d), BF16), jax.ShapeDtypeStruct((SUBLANES, LANES), F32),
                   jax.ShapeDtypeStruct((SUBLANES, d), F32), jax.ShapeDtypeStruct((SUBLANES, d), F32)],
        scratch_shapes=[pltpu.VMEM((tm, d), F32), pltpu.VMEM((tm, d), F32)],
        compiler_params=_params(("arbitrary", "arbitrary")),
    )(r1, target, w_up, w_down, ln1_g, ln1_b, ln2_g, ln2_b)


def _mlp_bwd(dr2, up, r1, w_up, w_down, ln1_g, tm):
    t, d = r1.shape
    ns, _, sw = w_up.shape
    spb = MLP_SLABS_PER_STEP
    assert ns % spb == 0
    nf, tf, ff = ns // spb, spb * sw, ns * sw

    def body(dr2_ref, up_ref, r1_ref, wu_ref, wd_ref, g1_ref, dup_ref, dr1_ref, dg1_ref, db1_ref, acc):
        i = pl.program_id(0)
        f = pl.program_id(1)

        @pl.when((i == 0) & (f == 0))
        def _():
            dg1_ref[...] = jnp.zeros_like(dg1_ref)
            db1_ref[...] = jnp.zeros_like(db1_ref)

        @pl.when(f == 0)
        def _():
            acc[...] = jnp.zeros_like(acc)

        dact = _dot(dr2_ref[...], wd_ref[...], NT)
        dup_ref[...] = (dact * 2.0 * jnp.maximum(up_ref[...].astype(F32), 0.0)).astype(dup_ref.dtype)
        for s in range(spb):
            acc[...] += _dot(dup_ref[:, s * sw:(s + 1) * sw], wu_ref[s], NT)

        @pl.when(f == nf - 1)
        def _():
            dh1 = acc[...] + ALPHA * dr2_ref[...].astype(F32)
            xhat, rstd = _ln_fwd(r1_ref[...])
            dg1_ref[0:1, :] += _colsum(dh1 * xhat)
            db1_ref[0:1, :] += _colsum(dh1)
            dr1_ref[...] = _ln_bwd(dh1, xhat, rstd, g1_ref[...]).astype(dr1_ref.dtype)

    row = pl.BlockSpec((tm, d), lambda i, f: (i, 0))
    acc8 = pl.BlockSpec((SUBLANES, d), lambda i, f: (0, 0))
    return pl.pallas_call(
        body, name="mlp_bwd",
        grid=(t // tm, nf),
        in_specs=[row, pl.BlockSpec((tm, tf), lambda i, f: (i, f)), row,
                  pl.BlockSpec((spb, d, sw), lambda i, f: (f, 0, 0)), pl.BlockSpec((tf, d), lambda i, f: (f, 0)),
                  pl.BlockSpec((1, d), lambda i, f: (0, 0))],
        out_specs=[pl.BlockSpec((tm, tf), lambda i, f: (i, f)), row, acc8, acc8],
        out_shape=[jax.ShapeDtypeStruct((t, ff), BF16), jax.ShapeDtypeStruct((t, d), BF16),
                   jax.ShapeDtypeStruct((SUBLANES, d), F32), jax.ShapeDtypeStruct((SUBLANES, d), F32)],
        scratch_shapes=[pltpu.VMEM((tm, d), F32)],
        compiler_params=_params(("arbitrary", "arbitrary")),
    )(dr2, up, r1, w_up, w_down, ln1_g)


def _dx_kernel(segs, w_main, ddt, w_dt, dr1, tm, tk, comm=None):
    t, d = dr1.shape
    nblk = [s.shape[1] // tk for s in segs]
    starts = [sum(nblk[:i]) for i in range(len(segs))]
    nk = sum(nblk)
    nseg = len(segs)

    def body(*refs):
        seg_refs = refs[:nseg]
        w_ref, ddt_ref, wdt_ref, dr1_ref, o_ref, acc = refs[nseg:]
        k = pl.program_id(1)

        @pl.when(k == 0)
        def _():
            acc[...] = ALPHA * dr1_ref[...].astype(F32) + _dot(ddt_ref[...], wdt_ref[...], NT)

        for si in range(nseg):
            @pl.when((k >= starts[si]) & (k < starts[si] + nblk[si]))
            def _(si=si):
                acc[...] += _dot(seg_refs[si][...], w_ref[...], NT)

        @pl.when(k == nk - 1)
        def _():
            o_ref[...] = acc[...]

    def seg_spec(si):
        return pl.BlockSpec((tm, tk), lambda i, k: (i, jnp.clip(k - starts[si], 0, nblk[si] - 1)))

    row = pl.BlockSpec((tm, d), lambda i, k: (i, 0))
    grid = (t // tm, nk)
    c_in, c_in_specs, c_out_specs, c_out_shapes = _comm_specs(comm)
    return pl.pallas_call(
        _fuse_comm(body, grid, nseg + 4, 1, comm), name="dx",
        grid=grid,
        in_specs=[seg_spec(si) for si in range(nseg)] + [
            pl.BlockSpec((d, tk), lambda i, k: (0, k)), pl.BlockSpec((tm, LANES), lambda i, k: (i, 0)),
            pl.BlockSpec((d, LANES), lambda i, k: (0, 0)), row] + c_in_specs,
        out_specs=[row] + c_out_specs,
        out_shape=[jax.ShapeDtypeStruct((t, d), F32)] + c_out_shapes,
        scratch_shapes=[pltpu.VMEM((tm, d), F32)] + (list(comm.scratch) if comm else []),
        compiler_params=_params(("arbitrary", "arbitrary")),
    )(*segs, w_main, ddt, w_dt, dr1, *c_in)


def _dims(d):
    inner = 2 * d
    heads = inner // HEAD_DIM
    cd = inner + 2 * GROUPS * STATE
    assert heads <= LANES and inner % (GROUPS * LANES) == 0 and d % (len(POOL_WINDOWS) * LANES) == 0
    o_z, o_xbc, o_dt, o_lg = d, d + inner, d + inner + cd, d + inner + cd + heads
    return inner, heads, cd, (o_z, o_xbc, o_dt, o_lg)


def _row(v, width=None):
    v = v.reshape(1, -1).astype(F32)
    if width is not None and v.shape[1] < width:
        v = jnp.pad(v, ((0, 0), (0, width - v.shape[1])))
    return v


def _local_step(x2, xb, tgt2, w, shards, core, bl):
    t, d = x2.shape
    inner, heads, cd, _ = _dims(d)
    gs = GROUPS * STATE
    nc = t // bl // CHUNK
    w_main, w_dt = _w_in_internal(w["w_in_blocks"], d)
    c_z, c_lg, c_u = cd, cd + inner, cd + inner + 2 * d
    conv_w8 = jnp.pad(w["conv_w"].astype(F32), ((0, SUBLANES - CONV_K), (0, 0)))
    conv_b = _row(w["conv_b"])
    dtb, alog = _row(w["dt_bias"], LANES), _row(w["a_log"], LANES)
    dskip_x = _row(jnp.repeat(w["d_skip"].reshape(-1), HEAD_DIM))
    normw = _row(w["ssd_norm_w"])
    col_head = lax.broadcasted_iota(jnp.int32, (LANES, inner), 1) // HEAD_DIM
    emat = (col_head == lax.broadcasted_iota(jnp.int32, (LANES, inner), 0)).astype(BF16)
    emat_t = emat.T
    w_main, w_dt = w_main.astype(BF16), w_dt.astype(BF16)
    b_gates, pool_scale = _row(w["b_gates"]), _row(w["pool_scale"])
    ln1_g, ln1_b, ln2_g, ln2_b = _row(w["ln1_g"]), _row(w["ln1_b"]), _row(w["ln2_g"]), _row(w["ln2_b"])

    tm = min(512, t)
    tk = min(1024, d)
    ct = min(512, d)
    rt = min(512, t // bl)
    nct = t // bl // rt
    mm = functools.partial(_matmul, bm=1024, bn=tk, bk=1024)
    mmt = functools.partial(_matmul, bm=1024, bn=tk, bk=2048)

    proj, xbc, dsl, *gathered = _in_proj(xb, w_main, conv_w8, conv_b, cd, t // bl, 1024, tk,
                                    _all_gather_comm([shards[n] for n in OTHERS]))
    gathered = dict(zip(OTHERS, gathered))
    w_ssd, w_out, w_down = (gathered[n].reshape(-1, d) for n in ("w_ssd_proj", "w_out", "w_down"))
    w_up = gathered["w_up"]
    npg = len(POOL_WINDOWS)
    cg = d // npg
    wpg = gathered["w_pool_group"].reshape(N_DEV, npg, cg // N_DEV, cg).transpose(1, 0, 2, 3).reshape(npg, cg, cg)
    dt_raw = mm(xb, w_dt, "nn", F32, name="in_proj_dt")
    y, yn, states = _ssd_fwd(xbc, proj, dt_raw, dtb, alog, dskip_x, normw, emat, bl, inner, c_z)
    yssd = mmt(yn, w_ssd, "nn", BF16, name="ssd_proj")
    ypr = _pool_fwd(proj, wpg, bl, d, c_u)
    merged, r1 = _merge_fwd(proj, ypr, yssd, x2, w_out, b_gates, pool_scale, d, c_lg, tm)
    tmm = min(1024, t)
    up, h1, dr2, loss8, dg2, db2 = _mlp_fwd(r1, tgt2, w_up, w_down, ln1_g, ln1_b, ln2_g, ln2_b, tmm)

    dup, dr1, dg1, db1 = _mlp_bwd(dr2, up, r1, w_up, w_down, ln1_g, tmm)
    relu2 = lambda v: jnp.square(jnp.maximum(v, 0.0))
    g = {}
    g["w_down"] = mmt(up, dr2, "tn", BF16, name="dw_down", a_fn=relu2)
    g["w_up"] = mmt(h1, dup, "tn", BF16, name="dw_up", col_blocks=N_DEV)
    g["w_out"] = mmt(merged, dr1, "tn", BF16, name="dw_out")
    dlg, dyp, dys, dbg, dps = _merge_bwd(dr1, proj, ypr, yssd, w_out, b_gates, pool_scale, d, c_lg, tm)
    du, dwpg = _pool_bwd(proj, dyp, wpg, bl, d, c_u)
    g["w_pool_group"] = dwpg.reshape(npg, N_DEV, cg // N_DEV, cg).transpose(1, 0, 2, 3).reshape(
        N_DEV, npg * cg // N_DEV, cg).astype(BF16)
    dyn = mm(dys, w_ssd, "nt", BF16, name="d_ssd_proj")
    g["w_ssd_proj"] = mmt(yn, dys, "tn", BF16, name="dw_ssd_proj")

    def chip_sums(names, tag):
        parts = [g.pop(n).reshape((N_DEV,) + shards_2d[n]) for n in names]
        recv = _run_comm(_rs_sibling_comm(parts), "rs_sibling_" + tag)
        return [_add_pairs(core, p, r, "rs_add_" + n) for n, p, r in zip(names, parts, recv)]

    shards_2d = {n: s.shape for n, s in shards.items()}
    shards_2d["w_in"] = w["w_in_blocks"].shape[1:]
    dxs, dbm, dcm, dz, ddt, dnw, dsk, dalog, ddtb, *recv_others = _ssd_bwd(
        xbc, proj, dt_raw, y, dyn, states, dtb, alog, dskip_x, normw, emat, emat_t, bl, inner, c_z,
        comm=_rs_chips_comm(chip_sums(OTHERS, "a")))
    dxs_p, dcw_x, dcb_x = _conv_bwd(proj, dsl, dxs, conv_w8, nct, 0, inner, ct, rt, "conv_bwd_x")
    dbm_p, dcw_b, dcb_b = _conv_bwd(proj, dsl, dbm, conv_w8, nct, inner, gs, ct, rt, "conv_bwd_b")
    dcm_p, dcw_c, dcb_c = _conv_bwd(proj, dsl, dcm, conv_w8, nct, inner + gs, gs, ct, rt, "conv_bwd_c")
    segs = [dxs_p, dbm_p, dcm_p, dz, dlg, du]
    keys = [k for k, _, _ in _col_segments(d)]
    dws = {k: mmt(xb, s, "tn", BF16, name="dw_in_" + k) for k, s in zip(keys, segs + [ddt])}
    g["w_in"] = _w_in_grad_blocks(dws, d, w["w_in_blocks"].shape[2])
    grad_x, recv_w_in = _dx_kernel(segs, w_main, ddt, w_dt, dr1, tmm, tk,
                                   comm=_rs_chips_comm(chip_sums(["w_in"], "b")))
    recv = dict(zip(OTHERS, recv_others))
    recv["w_in"] = recv_w_in
    g["conv_w"] = jnp.concatenate([dcw_x, dcw_b, dcw_c], axis=1)[:CONV_K]
    g["conv_b"] = jnp.concatenate([dcb_x, dcb_b, dcb_c], axis=1)[0]
    g["b_gates"], g["pool_scale"] = dbg[0], dps[0]
    g["dt_bias"], g["a_log"], g["d_skip"] = ddtb[0, :heads], dalog[0, :heads], dsk[0, :heads]
    g["ssd_norm_w"] = dnw[:, 0, :].reshape(inner)
    g["ln1_g"], g["ln1_b"], g["ln2_g"], g["ln2_b"] = dg1[0], db1[0], dg2[0], db2[0]
    return loss8, grad_x, g, recv


BIG = ("w_in", "w_ssd_proj", "w_pool_group", "w_out", "w_up", "w_down")
OTHERS = BIG[1:]
SMALL = ("b_gates", "conv_b", "dt_bias", "a_log", "d_skip", "ssd_norm_w", "pool_scale", "ln1_g", "ln1_b", "ln2_g",
         "ln2_b")
SMALL_PACK = SMALL + ("conv_w",)
NAMES = ("w_in", "b_gates", "conv_w", "conv_b", "dt_bias", "a_log", "d_skip", "ssd_norm_w", "w_ssd_proj",
         "w_pool_group", "pool_scale", "w_out", "ln1_g", "ln1_b", "w_up", "w_down", "ln2_g", "ln2_b")


def _size(shape):
    n = 1
    for s in shape:
        n *= s
    return n


def _rows128(v):
    v = v.astype(F32).reshape((-1, v.shape[-1]))
    n = v.shape[-1]
    v = jnp.pad(v, ((0, 0), (0, -n % LANES)))
    return v.reshape(-1, LANES)


def _pack_small(vals, extra):
    parts = [_rows128(vals[n]) for n in SMALL_PACK]
    parts.append(jnp.pad(extra.reshape(1, 1).astype(F32), ((0, 0), (0, LANES - 1))))
    buf = jnp.concatenate(parts, axis=0)
    return jnp.pad(buf, ((0, -buf.shape[0] % SUBLANES), (0, 0)))


def _unpack_small(buf, shapes):
    out, off = {}, 0
    for n in SMALL_PACK:
        lead, last = _size(shapes[n][:-1]), shapes[n][-1]
        per = -(-last // LANES)
        out[n] = buf[off:off + lead * per].reshape(lead, per * LANES)[:, :last].reshape(shapes[n])
        off += lead * per
    return out, buf[off, 0]


def _col_segments(d):
    inner, heads, cd, (o_z, o_xbc, o_dt, o_lg) = _dims(d)
    gs = GROUPS * STATE
    return [("xs", o_xbc, inner), ("B", o_xbc + inner, gs), ("C", o_xbc + inner + gs, gs), ("z", o_z, inner),
            ("lg", o_lg, 2 * d), ("u", 0, d), ("dt", o_dt, heads)]


def _cols_from_blocks(blocks, start, width, bw):
    parts, pos = [], start
    while pos < start + width:
        k, off = divmod(pos, bw)
        n = min(bw - off, start + width - pos)
        parts.append(blocks[k][:, off:off + n])
        pos += n
    return parts


def _w_in_internal(blocks, d):
    bw = blocks.shape[2]
    segs = _col_segments(d)
    heads = segs[-1][2]
    main = [p for _, s, w_ in segs[:-1] for p in _cols_from_blocks(blocks, s, w_, bw)]
    w_dt = jnp.concatenate(_cols_from_blocks(blocks, segs[-1][1], heads, bw), axis=1)
    return jnp.concatenate(main, axis=1), jnp.pad(w_dt, ((0, 0), (0, LANES - heads)))


def _w_in_grad_blocks(dws, d, bw):
    order = sorted(_col_segments(d), key=lambda s: s[1])
    blocks = []
    for k in range(N_DEV):
        lo, hi, parts = k * bw, (k + 1) * bw, []
        for key, s, w_ in order:
            a, b = max(lo, s), min(hi, s + w_)
            if a < b:
                parts.append(dws[key][:, a - s:b - s])
        blocks.append(jnp.concatenate(parts, axis=1))
    return jnp.stack(blocks)


def _mesh_pos():
    return lax.axis_index("x"), lax.axis_index("y"), lax.axis_index("c")


def _all_gather_comm(shards):
    nw = len(shards)

    def setup(x_refs, out_refs, scr):
        send_sems, recv_sems, local_sems = scr
        x, y, c = _mesh_pos()
        me, sibling = (x, y, c), (x, y, 1 - c)
        chips = [(1 - x, y), (x, 1 - y), (1 - x, 1 - y)]

        def copy(wi, k, block, to, from_input=False):
            px, py, pc = block
            blk = out_refs[wi].at[4 * px + 2 * py + pc]
            return pltpu.make_async_remote_copy(
                src_ref=x_refs[wi] if from_input else blk, dst_ref=blk,
                send_sem=send_sems.at[7 * wi + k], recv_sem=recv_sems.at[7 * wi + k], device_id=to,
                device_id_type=MESH)

        mine = [pltpu.make_async_copy(x_refs[wi], out_refs[wi].at[4 * x + 2 * y + c], local_sems.at[wi])
                for wi in range(nw)]
        sends = []
        for wi in range(nw):
            sends.append(copy(wi, 0, me, sibling, True))
            sends += [copy(wi, 1 + j, me, (*chip, c), True) for j, chip in enumerate(chips)]
        return copy, mine, sends, me, sibling, chips, c

    def start(x_refs, out_refs, scr):
        _, mine, sends, _, _, _, _ = setup(x_refs, out_refs, scr)
        for cp in mine + sends:
            cp.start()

    def wait(x_refs, out_refs, scr):
        copy, mine, sends, me, sibling, chips, c = setup(x_refs, out_refs, scr)
        passed = []
        for wi in range(nw):
            for j, chip in enumerate(chips):
                copy(wi, 1 + j, (*chip, c), me).wait_recv()
                passed.append(copy(wi, 4 + j, (*chip, c), sibling))
                passed[-1].start()
        for wi in range(nw):
            copy(wi, 0, sibling, me).wait_recv()
            for j, chip in enumerate(chips):
                copy(wi, 4 + j, (*chip, 1 - c), me).wait_recv()
        for cp in sends + passed:
            cp.wait_send()
        for cp in mine:
            cp.wait()

    return _Comm(
        inputs=list(shards),
        out_shapes=[jax.ShapeDtypeStruct((N_DEV,) + s.shape, s.dtype) for s in shards],
        scratch=[pltpu.SemaphoreType.DMA((7 * nw,)), pltpu.SemaphoreType.DMA((7 * nw,)),
                 pltpu.SemaphoreType.DMA((nw,))],
        start=start, wait=wait)


def _rs_sibling_comm(parts):
    nw = len(parts)
    half = N_DEV // 2

    def copies(p_refs, recv_refs, scr):
        send_sems, recv_sems = scr
        x, y, c = _mesh_pos()
        return [pltpu.make_async_remote_copy(
            src_ref=p_refs[wi].at[2 * q + 1 - c], dst_ref=recv_refs[wi].at[q],
            send_sem=send_sems.at[half * wi + q], recv_sem=recv_sems.at[half * wi + q],
            device_id=(x, y, 1 - c), device_id_type=MESH) for wi in range(nw) for q in range(half)]

    def start(p_refs, recv_refs, scr):
        for cp in copies(p_refs, recv_refs, scr):
            cp.start()

    def wait(p_refs, recv_refs, scr):
        for cp in copies(p_refs, recv_refs, scr):
            cp.wait()

    return _Comm(
        inputs=list(parts),
        out_shapes=[jax.ShapeDtypeStruct((half,) + p.shape[1:], p.dtype) for p in parts],
        scratch=[pltpu.SemaphoreType.DMA((half * nw,)), pltpu.SemaphoreType.DMA((half * nw,))],
        start=start, wait=wait)


def _rs_chips_comm(tbs):
    nw = len(tbs)

    def copies(t_refs, o_refs, scr):
        send_sems, recv_sems, local_sems = scr
        x, y, c = _mesh_pos()
        p = 2 * x + y
        chips = [(1 - x, y), (x, 1 - y), (1 - x, 1 - y)]
        own = [pltpu.make_async_copy(t_refs[wi].at[p], o_refs[wi].at[p], local_sems.at[wi]) for wi in range(nw)]
        remote = [pltpu.make_async_remote_copy(
            src_ref=t_refs[wi].at[2 * qx + qy], dst_ref=o_refs[wi].at[p], send_sem=send_sems.at[3 * wi + j],
            recv_sem=recv_sems.at[3 * wi + j], device_id=(qx, qy, c), device_id_type=MESH)
            for wi in range(nw) for j, (qx, qy) in enumerate(chips)]
        arriving = [pltpu.make_async_remote_copy(
            src_ref=t_refs[wi].at[p], dst_ref=o_refs[wi].at[2 * qx + qy], send_sem=send_sems.at[3 * wi + j],
            recv_sem=recv_sems.at[3 * wi + j], device_id=(qx, qy, c), device_id_type=MESH)
            for wi in range(nw) for j, (qx, qy) in enumerate(chips)]
        return own, remote, arriving

    def start(t_refs, o_refs, scr):
        own, remote, _ = copies(t_refs, o_refs, scr)
        for cp in own + remote:
            cp.start()

    def wait(t_refs, o_refs, scr):
        own, remote, arriving = copies(t_refs, o_refs, scr)
        for cp in arriving:
            cp.wait_recv()
        for cp in remote:
            cp.wait_send()
        for cp in own:
            cp.wait()

    return _Comm(
        inputs=list(tbs),
        out_shapes=[jax.ShapeDtypeStruct(t_.shape, t_.dtype) for t_ in tbs],
        scratch=[pltpu.SemaphoreType.DMA((3 * nw,)), pltpu.SemaphoreType.DMA((3 * nw,)),
                 pltpu.SemaphoreType.DMA((nw,))],
        start=start, wait=wait)


def _row_tile(rows, cap=256):
    if rows <= cap:
        return rows
    return max(t_ for t_ in range(SUBLANES, cap + 1, SUBLANES) if rows % t_ == 0)


def _add_pairs(core, part, recv, name):
    n, r, c_ = recv.shape
    tr = _row_tile(r)

    def body(core_ref, a_ref, b_ref, o_ref):
        o_ref[...] = (a_ref[...].astype(F32) + b_ref[...].astype(F32)).astype(o_ref.dtype)

    spec = pl.BlockSpec((1, tr, c_), lambda q, i, core_ref: (q, i, 0))
    return pl.pallas_call(
        body, name=name,
        grid_spec=pltpu.PrefetchScalarGridSpec(
            num_scalar_prefetch=1, grid=(n, r // tr),
            in_specs=[pl.BlockSpec((1, tr, c_), lambda q, i, core_ref: (2 * q + core_ref[0], i, 0)), spec],
            out_specs=spec),
        out_shape=jax.ShapeDtypeStruct(recv.shape, BF16), compiler_params=_params(("parallel", "parallel")),
    )(core, part, recv)


def _small_allreduce(vec, name):
    rows = vec.shape[0]

    def body(x_ref, o_ref, buf, send_sems, recv_sems):
        x, y, c = _mesh_pos()
        me = 4 * x + 2 * y + c
        buf[me] = x_ref[...]
        cps = []
        for k in range(1, N_DEV):
            peer = (1 - x if k & 4 else x, 1 - y if k & 2 else y, 1 - c if k & 1 else c)
            cps.append(pltpu.make_async_remote_copy(
                src_ref=x_ref, dst_ref=buf.at[me], send_sem=send_sems.at[k - 1], recv_sem=recv_sems.at[k - 1],
                device_id=peer, device_id_type=MESH))
        for cp in cps:
            cp.start()
        for k in range(1, N_DEV):
            px, py, pc = (1 - x if k & 4 else x, 1 - y if k & 2 else y, 1 - c if k & 1 else c)
            pltpu.make_async_remote_copy(
                src_ref=x_ref, dst_ref=buf.at[4 * px + 2 * py + pc], send_sem=send_sems.at[k - 1],
                recv_sem=recv_sems.at[k - 1], device_id=(px, py, pc), device_id_type=MESH).wait_recv()
        for cp in cps:
            cp.wait_send()
        acc = buf[0]
        for k in range(1, N_DEV):
            acc = acc + buf[k]
        o_ref[...] = acc

    vm = pl.BlockSpec(memory_space=pltpu.VMEM)
    return pl.pallas_call(
        body, name=name,
        in_specs=[vm], out_specs=vm,
        out_shape=jax.ShapeDtypeStruct(vec.shape, F32),
        scratch_shapes=[pltpu.VMEM((N_DEV, rows, LANES), F32), pltpu.SemaphoreType.DMA((N_DEV - 1,)),
                        pltpu.SemaphoreType.DMA((N_DEV - 1,))],
    )(vec)


def _adamw(gparts, w, m, v, name):
    n, r, c_ = gparts.shape
    tr = _row_tile(r)
    c1 = 1.0 / (1.0 - B1 ** STEP)
    c2 = 1.0 / (1.0 - B2 ** STEP)

    def body(g_ref, w_ref, m_ref, v_ref, go_ref, d_ref, mo_ref, vo_ref):
        g = g_ref[0].astype(F32)
        for q in range(1, n):
            g = g + g_ref[q].astype(F32)
        mn = B1 * m_ref[...] + (1.0 - B1) * g
        vn = B2 * v_ref[...] + (1.0 - B2) * (g * g)
        go_ref[...] = g
        mo_ref[...] = mn
        vo_ref[...] = vn
        d_ref[...] = -LR * ((mn * c1) / (jnp.sqrt(vn * c2) + ADAM_EPS) + WD * w_ref[...])

    spec = pl.BlockSpec((tr, c_), lambda i: (i, 0))
    out = jax.ShapeDtypeStruct((r, c_), F32)
    return pl.pallas_call(
        body, name=name, grid=(r // tr,),
        in_specs=[pl.BlockSpec((n, tr, c_), lambda i: (0, i, 0)), spec, spec, spec],
        out_specs=[spec] * 4, out_shape=[out] * 4, compiler_params=_params(("parallel",)),
    )(gparts, w, m, v)


def kernel(x, w_in, b_gates, conv_w, conv_b, dt_bias, a_log, d_skip, ssd_norm_w, w_ssd_proj, w_pool_group, pool_scale, w_out, ln1_g, ln1_b, w_up, w_down, ln2_g, ln2_b, loss_target, m_w_in, m_b_gates, m_conv_w, m_conv_b, m_dt_bias, m_a_log, m_d_skip, m_ssd_norm_w, m_w_ssd_proj, m_w_pool_group, m_pool_scale, m_w_out, m_ln1_g, m_ln1_b, m_w_up, m_w_down, m_ln2_g, m_ln2_b, v_w_in, v_b_gates, v_conv_w, v_conv_b, v_dt_bias, v_a_log, v_d_skip, v_ssd_norm_w, v_w_ssd_proj, v_w_pool_group, v_pool_scale, v_w_out, v_ln1_g, v_ln1_b, v_w_up, v_w_down, v_ln2_g, v_ln2_b):
    ws = (w_in, b_gates, conv_w, conv_b, dt_bias, a_log, d_skip, ssd_norm_w, w_ssd_proj, w_pool_group, pool_scale,
          w_out, ln1_g, ln1_b, w_up, w_down, ln2_g, ln2_b)
    ms = (m_w_in, m_b_gates, m_conv_w, m_conv_b, m_dt_bias, m_a_log, m_d_skip, m_ssd_norm_w, m_w_ssd_proj,
          m_w_pool_group, m_pool_scale, m_w_out, m_ln1_g, m_ln1_b, m_w_up, m_w_down, m_ln2_g, m_ln2_b)
    vs = (v_w_in, v_b_gates, v_conv_w, v_conv_b, v_dt_bias, v_a_log, v_d_skip, v_ssd_norm_w, v_w_ssd_proj,
          v_w_pool_group, v_pool_scale, v_w_out, v_ln1_g, v_ln1_b, v_w_up, v_w_down, v_ln2_g, v_ln2_b)
    w = {n: a[0] for n, a in zip(NAMES, ws)}
    m = {n: a[0] for n, a in zip(NAMES, ms)}
    v = {n: a[0] for n, a in zip(NAMES, vs)}
    out_shapes = {n: a.shape for n, a in zip(NAMES, ws)}
    bl, s, d = x.shape
    x2, tgt2 = x.reshape(bl * s, d), loss_target.reshape(bl * s, d)
    xi, yi, ci = _mesh_pos()
    me = 4 * xi + 2 * yi + ci
    zero = jnp.zeros((), F32)
    shapes = {n: w[n].shape for n in NAMES}
    shape2d = {n: (_size(shapes[n][:-1]), shapes[n][-1]) for n in BIG}
    cwl = shapes["conv_w"][1]

    shards = {n: w[n].astype(BF16).reshape(shape2d[n]) for n in BIG}
    full = {n: w[n] for n in SMALL}
    xb, full["w_in_blocks"], conv_blocks = _cast_bf16(x2, _all_gather_comm([shards.pop("w_in"), w["conv_w"]]))
    full["conv_w"] = conv_blocks.transpose(1, 0, 2).reshape(CONV_K, N_DEV * cwl)
    loss8, grad_x, g, recv = _local_step(x2, xb, tgt2, full, shards, ci.astype(jnp.int32).reshape(1), bl)

    small_sum = _small_allreduce(_pack_small(g, loss8[0, 0]), "small_allreduce")
    ex_shapes = {n: shapes[n] for n in SMALL}
    ex_shapes["conv_w"] = (CONV_K, N_DEV * cwl)
    gsum, loss = _unpack_small(small_sum, ex_shapes)
    gsum["conv_w"] = lax.dynamic_slice(gsum["conv_w"], (0, me * cwl), (CONV_K, cwl))
    gs_pk = _pack_small(gsum, zero)
    ws_pk, ms_pk, vs_pk = (_pack_small(t_, zero) for t_ in (w, m, v))
    small_out = _adamw(gs_pk[None], ws_pk, ms_pk, vs_pk, "adamw_small")
    loc_shapes = {n: shapes[n] for n in SMALL_PACK}
    res = [_unpack_small(o, loc_shapes)[0] for o in small_out]

    for n in BIG:
        outs = _adamw(recv[n], *(t_[n].reshape(shape2d[n]) for t_ in (w, m, v)), "adamw_" + n)
        for r_, o in zip(res, outs):
            r_[n] = o

    def ordered(r_):
        return [r_[n].reshape(out_shapes[n]) for n in NAMES]

    return (loss, grad_x.reshape(bl, s, d), *ordered(res[0]), *ordered(res[1]), *ordered(res[2]), *ordered(res[3]))
```
